```python
import jax, jax.numpy as jnp
from jax import lax
import numpy as np

D_MODEL = 2048
BATCH = 8
SEQ = 8192
DEPTH = 1

HEAD_DIM = 128
N_HEADS_DIL = 8
N_HEADS_FOX = 8
D_DIL = N_HEADS_DIL * HEAD_DIM
D_FOX = N_HEADS_FOX * HEAD_DIM
DIL_PATTERNS = ((128, 1), (512, 4), (2048, 16))
MAX_WINDOW = 2048
Q_BLOCK = 128
ROPE_THETA = 500000.0
ROPE_DIM = HEAD_DIM // 4
D_FF = 5632
NORM_EPS = 1e-6
IN_SPLITS = (D_DIL, D_DIL, D_DIL, D_FOX, D_FOX, D_FOX, N_HEADS_FOX, D_MODEL, D_MODEL)
IN_COLS = sum(IN_SPLITS)

kernel_name = "hybrid_dilated_fox_gated_macaron"


def rmsnorm(x, g):
    xf = x.astype(jnp.float32)
    y = xf * lax.rsqrt(jnp.mean(xf * xf, axis=-1, keepdims=True) + NORM_EPS)
    return (y * g.astype(jnp.float32)).astype(x.dtype)


def swiglu(x, w_gate, w_up, w_down):
    return (jax.nn.silu(x @ w_gate) * (x @ w_up)) @ w_down


def split_heads(t, n_heads):
    b, s, _ = t.shape
    return t.reshape(b, s, n_heads, HEAD_DIM).transpose(0, 2, 1, 3)


def merge_heads(t):
    b, h, s, d = t.shape
    return t.transpose(0, 2, 1, 3).reshape(b, s, h * d)


def partial_rope(t):
    s = t.shape[2]
    half = ROPE_DIM // 2
    pos = jnp.arange(s, dtype=jnp.float32)
    inv_freq = ROPE_THETA ** (-jnp.arange(0, ROPE_DIM, 2, dtype=jnp.float32) / ROPE_DIM)
    ang = pos[:, None] * inv_freq[None, :]
    cos, sin = jnp.cos(ang), jnp.sin(ang)
    rot = t[..., :ROPE_DIM].astype(jnp.float32)
    x1, x2 = rot[..., :half], rot[..., half:]
    rotated = jnp.concatenate([x1 * cos - x2 * sin, x2 * cos + x1 * sin], axis=-1)
    return jnp.concatenate([rotated.astype(t.dtype), t[..., ROPE_DIM:]], axis=-1)


def dilated_attention(q, k, v):
    b, h, s, dh = q.shape
    n_blocks = s // Q_BLOCK
    pad = ((0, 0), (0, 0), (MAX_WINDOW, 0), (0, 0))
    kp = jnp.pad(k, pad)
    vp = jnp.pad(v, pad)
    qb = q.reshape(b, h, n_blocks, Q_BLOCK, dh).transpose(2, 0, 1, 3, 4)

    def block(args):
        qblk, blk = args
        t0 = blk * Q_BLOCK
        nums, dens, maxs = [], [], []
        for w, d in DIL_PATTERNS:
            length = w + Q_BLOCK
            rows = length // d
            nq = Q_BLOCK // d
            start = t0 + MAX_WINDOW - w
            kr = lax.dynamic_slice_in_dim(kp, start, length, axis=2).reshape(b, h, rows, d, dh)
            vr = lax.dynamic_slice_in_dim(vp, start, length, axis=2).reshape(b, h, rows, d, dh)
            qr = qblk.reshape(b, h, nq, d, dh)
            sc = jnp.einsum('bhicd,bhrcd->bhcir', qr, kr, preferred_element_type=jnp.float32)
            r = jnp.arange(rows)
            ip = jnp.arange(nq)
            cc = jnp.arange(d)
            band = (r[None, :] >= ip[:, None]) & (r[None, :] <= ip[:, None] + w // d)
            valid = (t0 - w + r[None, :] * d + cc[:, None]) >= 0
            mask = band[None, :, :] & valid[:, None, :]
            sc = jnp.where(mask, sc, -jnp.inf)
            m = jnp.max(sc, axis=-1, keepdims=True)
            p = jnp.exp(sc - m)
            den = jnp.sum(p, axis=-1)
            num = jnp.einsum('bhcir,bhrcd->bhicd', p, vr.astype(jnp.float32))
            nums.append(num.reshape(b, h, Q_BLOCK, dh))
            dens.append(den.transpose(0, 1, 3, 2).reshape(b, h, Q_BLOCK))
            maxs.append(m[..., 0].transpose(0, 1, 3, 2).reshape(b, h, Q_BLOCK))
        num = jnp.stack(nums)
        den = jnp.stack(dens)
        mx = jnp.stack(maxs)
        wgt = jnp.exp(mx - jnp.max(mx, axis=0, keepdims=True))
        out = jnp.sum(num * wgt[..., None], axis=0) / jnp.sum(den * wgt, axis=0)[..., None]
        return out.astype(q.dtype)

    out = lax.map(block, (qb, jnp.arange(n_blocks)))
    return out.transpose(1, 2, 0, 3, 4).reshape(b, h, s, dh)


def forgetting_attention(q, k, v, log_f):
    b, h, s, dh = q.shape
    n_blocks = s // Q_BLOCK
    c = jnp.cumsum(log_f, axis=-1)
    qb = q.reshape(b, h, n_blocks, Q_BLOCK, dh).transpose(2, 0, 1, 3, 4)
    cb = c.reshape(b, h, n_blocks, Q_BLOCK).transpose(2, 0, 1, 3)
    kpos = jnp.arange(s)

    def block(args):
        qblk, cq, blk = args
        sc = jnp.einsum('bhqd,bhkd->bhqk', qblk, k, preferred_element_type=jnp.float32)
        sc = sc + cq[..., :, None] - c[:, :, None, :]
        qpos = blk * Q_BLOCK + jnp.arange(Q_BLOCK)
        sc = jnp.where(kpos[None, :] <= qpos[:, None], sc, -jnp.inf)
        p = jax.nn.softmax(sc, axis=-1)
        return jnp.einsum('bhqk,bhkd->bhqd', p.astype(v.dtype), v)

    out = lax.map(block, (qb, cb, jnp.arange(n_blocks)))
    return out.transpose(1, 2, 0, 3, 4).reshape(b, h, s, dh)


def _fwd_setup_inputs(seed: int = 0) -> dict:
    key = jax.random.key(seed)
    ks = jax.random.split(key, 20)
    f32 = jnp.float32

    def w(k, shape, fan_in):
        return jax.random.normal(k, shape, f32) * fan_in ** -0.5

    def gain(k):
        return 1.0 + 0.02 * jax.random.normal(k, (DEPTH, D_MODEL), f32)

    return {
        "x": jax.random.normal(ks[0], (BATCH, SEQ, D_MODEL), f32),
        "ffn1_norm": gain(ks[1]),
        "ffn1_w_gate": w(ks[2], (DEPTH, D_MODEL, D_FF), D_MODEL),
        "ffn1_w_up": w(ks[3], (DEPTH, D_MODEL, D_FF), D_MODEL),
        "ffn1_w_down": w(ks[4], (DEPTH, D_FF, D_MODEL), D_FF),
        "mix_norm": gain(ks[5]),
        "w_in": w(ks[6], (DEPTH, D_MODEL, IN_COLS), D_MODEL),
        "b_forget": 4.0 + jax.random.normal(ks[7], (DEPTH, N_HEADS_FOX), f32),
        "b_gate_dil": 0.02 * jax.random.normal(ks[8], (DEPTH, D_MODEL), f32),
        "b_gate_fox": 0.02 * jax.random.normal(ks[9], (DEPTH, D_MODEL), f32),
        "w_proj_dil": w(ks[10], (DEPTH, D_DIL, D_MODEL), D_DIL),
        "w_proj_fox": w(ks[11], (DEPTH, D_FOX, D_MODEL), D_FOX),
        "w_out": w(ks[12], (DEPTH, D_MODEL, D_MODEL), D_MODEL),
        "ffn2_norm": gain(ks[13]),
        "ffn2_w_gate": w(ks[14], (DEPTH, D_MODEL, D_FF), D_MODEL),
        "ffn2_w_up": w(ks[15], (DEPTH, D_MODEL, D_FF), D_MODEL),
        "ffn2_w_down": w(ks[16], (DEPTH, D_FF, D_MODEL), D_FF),
        "final_norm": 1.0 + 0.02 * jax.random.normal(ks[17], (D_MODEL,), f32),
    }


def _fwd_reference(x, ffn1_norm, ffn1_w_gate, ffn1_w_up, ffn1_w_down, mix_norm, w_in,
              b_forget, b_gate_dil, b_gate_fox, w_proj_dil, w_proj_fox, w_out,
              ffn2_norm, ffn2_w_gate, ffn2_w_up, ffn2_w_down, final_norm):
    split_points = list(np.cumsum(IN_SPLITS)[:-1])
    scale = HEAD_DIM ** -0.5
    for l in range(DEPTH):
        x = x + 0.5 * swiglu(rmsnorm(x, ffn1_norm[l]), ffn1_w_gate[l], ffn1_w_up[l], ffn1_w_down[l])

        h = rmsnorm(x, mix_norm[l])
        proj = h @ w_in[l]
        q_d, k_d, v_d, q_f, k_f, v_f, f_logit, g_d, g_f = jnp.split(proj, split_points, axis=-1)

        qa = partial_rope(split_heads(q_d, N_HEADS_DIL)) * scale
        ka = partial_rope(split_heads(k_d, N_HEADS_DIL))
        va = split_heads(v_d, N_HEADS_DIL)
        y_dil = merge_heads(dilated_attention(qa, ka, va))

        log_f = jax.nn.log_sigmoid((f_logit + b_forget[l]).astype(jnp.float32))
        log_f = log_f.transpose(0, 2, 1)
        qb_ = split_heads(q_f, N_HEADS_FOX) * scale
        kb_ = split_heads(k_f, N_HEADS_FOX)
        vb_ = split_heads(v_f, N_HEADS_FOX)
        y_fox = merge_heads(forgetting_attention(qb_, kb_, vb_, log_f))

        merged = (jax.nn.sigmoid(g_d + b_gate_dil[l]) * (y_dil @ w_proj_dil[l])
                  + jax.nn.sigmoid(g_f + b_gate_fox[l]) * (y_fox @ w_proj_fox[l]))
        x = x + merged @ w_out[l]

        x = x + 0.5 * swiglu(rmsnorm(x, ffn2_norm[l]), ffn2_w_gate[l], ffn2_w_up[l], ffn2_w_down[l])
    return rmsnorm(x, final_norm)


import jax as _jax
import jax.numpy as _jnp

TWIN_FORMAT = 'train_step'
FWD_PARAMS = ['x', 'ffn1_norm', 'ffn1_w_gate', 'ffn1_w_up', 'ffn1_w_down', 'mix_norm', 'w_in', 'b_forget', 'b_gate_dil', 'b_gate_fox', 'w_proj_dil', 'w_proj_fox', 'w_out', 'ffn2_norm', 'ffn2_w_gate', 'ffn2_w_up', 'ffn2_w_down', 'final_norm']
TWIN_WEIGHTS = ['ffn1_norm', 'ffn1_w_gate', 'ffn1_w_up', 'ffn1_w_down', 'mix_norm', 'w_in', 'b_forget', 'b_gate_dil', 'b_gate_fox', 'w_proj_dil', 'w_proj_fox', 'w_out', 'ffn2_norm', 'ffn2_w_gate', 'ffn2_w_up', 'ffn2_w_down', 'final_norm']
TWIN_DIFF_INPUT = 'x'
TWIN_INPUTS = ['x', 'ffn1_norm', 'ffn1_w_gate', 'ffn1_w_up', 'ffn1_w_down', 'mix_norm', 'w_in', 'b_forget', 'b_gate_dil', 'b_gate_fox', 'w_proj_dil', 'w_proj_fox', 'w_out', 'ffn2_norm', 'ffn2_w_gate', 'ffn2_w_up', 'ffn2_w_down', 'final_norm', 'loss_target', 'm_ffn1_norm', 'm_ffn1_w_gate', 'm_ffn1_w_up', 'm_ffn1_w_down', 'm_mix_norm', 'm_w_in', 'm_b_forget', 'm_b_gate_dil', 'm_b_gate_fox', 'm_w_proj_dil', 'm_w_proj_fox', 'm_w_out', 'm_ffn2_norm', 'm_ffn2_w_gate', 'm_ffn2_w_up', 'm_ffn2_w_down', 'm_final_norm', 'v_ffn1_norm', 'v_ffn1_w_gate', 'v_ffn1_w_up', 'v_ffn1_w_down', 'v_mix_norm', 'v_w_in', 'v_b_forget', 'v_b_gate_dil', 'v_b_gate_fox', 'v_w_proj_dil', 'v_w_proj_fox', 'v_w_out', 'v_ffn2_norm', 'v_ffn2_w_gate', 'v_ffn2_w_up', 'v_ffn2_w_down', 'v_final_norm']
TWIN_OUTPUTS = ['loss', 'grad_x', 'grad_ffn1_norm', 'grad_ffn1_w_gate', 'grad_ffn1_w_up', 'grad_ffn1_w_down', 'grad_mix_norm', 'grad_w_in', 'grad_b_forget', 'grad_b_gate_dil', 'grad_b_gate_fox', 'grad_w_proj_dil', 'grad_w_proj_fox', 'grad_w_out', 'grad_ffn2_norm', 'grad_ffn2_w_gate', 'grad_ffn2_w_up', 'grad_ffn2_w_down', 'grad_final_norm', 'delta_ffn1_norm', 'delta_ffn1_w_gate', 'delta_ffn1_w_up', 'delta_ffn1_w_down', 'delta_mix_norm', 'delta_w_in', 'delta_b_forget', 'delta_b_gate_dil', 'delta_b_gate_fox', 'delta_w_proj_dil', 'delta_w_proj_fox', 'delta_w_out', 'delta_ffn2_norm', 'delta_ffn2_w_gate', 'delta_ffn2_w_up', 'delta_ffn2_w_down', 'delta_final_norm', 'new_m_ffn1_norm', 'new_m_ffn1_w_gate', 'new_m_ffn1_w_up', 'new_m_ffn1_w_down', 'new_m_mix_norm', 'new_m_w_in', 'new_m_b_forget', 'new_m_b_gate_dil', 'new_m_b_gate_fox', 'new_m_w_proj_dil', 'new_m_w_proj_fox', 'new_m_w_out', 'new_m_ffn2_norm', 'new_m_ffn2_w_gate', 'new_m_ffn2_w_up', 'new_m_ffn2_w_down', 'new_m_final_norm', 'new_v_ffn1_norm', 'new_v_ffn1_w_gate', 'new_v_ffn1_w_up', 'new_v_ffn1_w_down', 'new_v_mix_norm', 'new_v_w_in', 'new_v_b_forget', 'new_v_b_gate_dil', 'new_v_b_gate_fox', 'new_v_w_proj_dil', 'new_v_w_proj_fox', 'new_v_w_out', 'new_v_ffn2_norm', 'new_v_ffn2_w_gate', 'new_v_ffn2_w_up', 'new_v_ffn2_w_down', 'new_v_final_norm']
TWIN_LEAF_KINDS = {'loss': 'loss', 'grad_x': 'grad_x', 'grad_ffn1_norm': 'grad_w', 'grad_ffn1_w_gate': 'grad_w', 'grad_ffn1_w_up': 'grad_w', 'grad_ffn1_w_down': 'grad_w', 'grad_mix_norm': 'grad_w', 'grad_w_in': 'grad_w', 'grad_b_forget': 'grad_w', 'grad_b_gate_dil': 'grad_w', 'grad_b_gate_fox': 'grad_w', 'grad_w_proj_dil': 'grad_w', 'grad_w_proj_fox': 'grad_w', 'grad_w_out': 'grad_w', 'grad_ffn2_norm': 'grad_w', 'grad_ffn2_w_gate': 'grad_w', 'grad_ffn2_w_up': 'grad_w', 'grad_ffn2_w_down': 'grad_w', 'grad_final_norm': 'grad_w', 'delta_ffn1_norm': 'delta_w', 'delta_ffn1_w_gate': 'delta_w', 'delta_ffn1_w_up': 'delta_w', 'delta_ffn1_w_down': 'delta_w', 'delta_mix_norm': 'delta_w', 'delta_w_in': 'delta_w', 'delta_b_forget': 'delta_w', 'delta_b_gate_dil': 'delta_w', 'delta_b_gate_fox': 'delta_w', 'delta_w_proj_dil': 'delta_w', 'delta_w_proj_fox': 'delta_w', 'delta_w_out': 'delta_w', 'delta_ffn2_norm': 'delta_w', 'delta_ffn2_w_gate': 'delta_w', 'delta_ffn2_w_up': 'delta_w', 'delta_ffn2_w_down': 'delta_w', 'delta_final_norm': 'delta_w', 'new_m_ffn1_norm': 'new_m', 'new_m_ffn1_w_gate': 'new_m', 'new_m_ffn1_w_up': 'new_m', 'new_m_ffn1_w_down': 'new_m', 'new_m_mix_norm': 'new_m', 'new_m_w_in': 'new_m', 'new_m_b_forget': 'new_m', 'new_m_b_gate_dil': 'new_m', 'new_m_b_gate_fox': 'new_m', 'new_m_w_proj_dil': 'new_m', 'new_m_w_proj_fox': 'new_m', 'new_m_w_out': 'new_m', 'new_m_ffn2_norm': 'new_m', 'new_m_ffn2_w_gate': 'new_m', 'new_m_ffn2_w_up': 'new_m', 'new_m_ffn2_w_down': 'new_m', 'new_m_final_norm': 'new_m', 'new_v_ffn1_norm': 'new_v', 'new_v_ffn1_w_gate': 'new_v', 'new_v_ffn1_w_up': 'new_v', 'new_v_ffn1_w_down': 'new_v', 'new_v_mix_norm': 'new_v', 'new_v_w_in': 'new_v', 'new_v_b_forget': 'new_v', 'new_v_b_gate_dil': 'new_v', 'new_v_b_gate_fox': 'new_v', 'new_v_w_proj_dil': 'new_v', 'new_v_w_proj_fox': 'new_v', 'new_v_w_out': 'new_v', 'new_v_ffn2_norm': 'new_v', 'new_v_ffn2_w_gate': 'new_v', 'new_v_ffn2_w_up': 'new_v', 'new_v_ffn2_w_down': 'new_v', 'new_v_final_norm': 'new_v'}


def _forward(args):
    return _fwd_reference(*[args[k] for k in FWD_PARAMS])


def _output_shape():
    def fwd():
        inp = _fwd_setup_inputs(0)
        return _fwd_reference(*[inp[k] for k in FWD_PARAMS])
    out = _jax.eval_shape(fwd)
    return out.shape, out.dtype

N_MICROBATCH = 1
ADAM_LR = 0.001
ADAM_B1 = 0.9
ADAM_B2 = 0.999
ADAM_EPS = 1e-08
ADAM_WD = 0.01
ADAM_STEP = 10
PER_EXAMPLE_BATCH_AXIS = {'x': 0, 'loss_target': 0}
SHARED_INPUTS = []
_WEIGHT_DTYPES = {'ffn1_norm': _jnp.float32, 'ffn1_w_gate': _jnp.float32, 'ffn1_w_up': _jnp.float32, 'ffn1_w_down': _jnp.float32, 'mix_norm': _jnp.float32, 'w_in': _jnp.float32, 'b_forget': _jnp.float32, 'b_gate_dil': _jnp.float32, 'b_gate_fox': _jnp.float32, 'w_proj_dil': _jnp.float32, 'w_proj_fox': _jnp.float32, 'w_out': _jnp.float32, 'ffn2_norm': _jnp.float32, 'ffn2_w_gate': _jnp.float32, 'ffn2_w_up': _jnp.float32, 'ffn2_w_down': _jnp.float32, 'final_norm': _jnp.float32}
MOMENT_SCALE = {'ffn1_norm': 5.514681e-02, 'ffn1_w_gate': 2.375039e-02, 'ffn1_w_up': 2.300070e-02, 'ffn1_w_down': 3.815062e-02, 'mix_norm': 3.925439e-02, 'w_in': 1.764804e-02, 'b_forget': 3.019563e-01, 'b_gate_dil': 5.906258e-03, 'b_gate_fox': 7.831327e-03, 'w_proj_dil': 1.489196e-02, 'w_proj_fox': 2.005206e-02, 'w_out': 2.500225e-02, 'ffn2_norm': 4.922803e-02, 'ffn2_w_gate': 2.138374e-02, 'ffn2_w_up': 2.071044e-02, 'ffn2_w_down': 3.435943e-02, 'final_norm': 3.196659e+01}


def _to_microbatches(a, axis):
    t = _jnp.moveaxis(a, axis, 0)
    t = t.reshape((N_MICROBATCH, t.shape[0] // N_MICROBATCH) + t.shape[1:])
    return _jnp.moveaxis(t, 1, axis + 1)


def setup_inputs(seed: int = 0) -> dict:
    inp = _fwd_setup_inputs(seed)
    key = _jax.random.fold_in(_jax.random.key(seed), 7919)
    shape, _ = _output_shape()
    out = dict(inp)
    out["loss_target"] = _jax.random.normal(_jax.random.fold_in(key, 0), shape, _jnp.float32)
    for i, name in enumerate(TWIN_WEIGHTS):
        w = inp[name].astype(_jnp.float32)
        if MOMENT_SCALE is None:
            s = _jnp.sqrt(_jnp.mean(_jnp.square(w)) + 1e-30)
        else:
            s = MOMENT_SCALE[name]
        km, kv = _jax.random.split(_jax.random.fold_in(key, i + 1))
        out[name] = w
        out["m_" + name] = s * _jax.random.normal(km, w.shape, _jnp.float32)
        out["v_" + name] = (s * s) * _jax.random.uniform(kv, w.shape, _jnp.float32, 0.5, 1.5)
    if N_MICROBATCH > 1:
        for name, axis in PER_EXAMPLE_BATCH_AXIS.items():
            out[name] = _to_microbatches(out[name], axis)
    return {'x': out['x'], 'ffn1_norm': out['ffn1_norm'], 'ffn1_w_gate': out['ffn1_w_gate'], 'ffn1_w_up': out['ffn1_w_up'], 'ffn1_w_down': out['ffn1_w_down'], 'mix_norm': out['mix_norm'], 'w_in': out['w_in'], 'b_forget': out['b_forget'], 'b_gate_dil': out['b_gate_dil'], 'b_gate_fox': out['b_gate_fox'], 'w_proj_dil': out['w_proj_dil'], 'w_proj_fox': out['w_proj_fox'], 'w_out': out['w_out'], 'ffn2_norm': out['ffn2_norm'], 'ffn2_w_gate': out['ffn2_w_gate'], 'ffn2_w_up': out['ffn2_w_up'], 'ffn2_w_down': out['ffn2_w_down'], 'final_norm': out['final_norm'], 'loss_target': out['loss_target'], 'm_ffn1_norm': out['m_ffn1_norm'], 'm_ffn1_w_gate': out['m_ffn1_w_gate'], 'm_ffn1_w_up': out['m_ffn1_w_up'], 'm_ffn1_w_down': out['m_ffn1_w_down'], 'm_mix_norm': out['m_mix_norm'], 'm_w_in': out['m_w_in'], 'm_b_forget': out['m_b_forget'], 'm_b_gate_dil': out['m_b_gate_dil'], 'm_b_gate_fox': out['m_b_gate_fox'], 'm_w_proj_dil': out['m_w_proj_dil'], 'm_w_proj_fox': out['m_w_proj_fox'], 'm_w_out': out['m_w_out'], 'm_ffn2_norm': out['m_ffn2_norm'], 'm_ffn2_w_gate': out['m_ffn2_w_gate'], 'm_ffn2_w_up': out['m_ffn2_w_up'], 'm_ffn2_w_down': out['m_ffn2_w_down'], 'm_final_norm': out['m_final_norm'], 'v_ffn1_norm': out['v_ffn1_norm'], 'v_ffn1_w_gate': out['v_ffn1_w_gate'], 'v_ffn1_w_up': out['v_ffn1_w_up'], 'v_ffn1_w_down': out['v_ffn1_w_down'], 'v_mix_norm': out['v_mix_norm'], 'v_w_in': out['v_w_in'], 'v_b_forget': out['v_b_forget'], 'v_b_gate_dil': out['v_b_gate_dil'], 'v_b_gate_fox': out['v_b_gate_fox'], 'v_w_proj_dil': out['v_w_proj_dil'], 'v_w_proj_fox': out['v_w_proj_fox'], 'v_w_out': out['v_w_out'], 'v_ffn2_norm': out['v_ffn2_norm'], 'v_ffn2_w_gate': out['v_ffn2_w_gate'], 'v_ffn2_w_up': out['v_ffn2_w_up'], 'v_ffn2_w_down': out['v_ffn2_w_down'], 'v_final_norm': out['v_final_norm']}


def _loss(weights, diff, rest, loss_target):
    with _jax.named_scope("forward"):
        args = {**rest, TWIN_DIFF_INPUT: diff, **{k: w.astype(_WEIGHT_DTYPES[k]) for k, w in weights.items()}}
        y = _forward(args)
    with _jax.named_scope("loss_head"):
        err = _jnp.square(y.astype(_jnp.float32) - loss_target)
        return 0.5 * _jnp.sum(_jnp.mean(err, axis=-1)) if err.ndim else 0.5 * err


def _adamw(w, g, m, v):
    m = ADAM_B1 * m + (1.0 - ADAM_B1) * g
    v = ADAM_B2 * v + (1.0 - ADAM_B2) * _jnp.square(g)
    m_hat = m / (1.0 - ADAM_B1 ** ADAM_STEP)
    v_hat = v / (1.0 - ADAM_B2 ** ADAM_STEP)
    delta = -ADAM_LR * (m_hat / (_jnp.sqrt(v_hat) + ADAM_EPS) + ADAM_WD * w)
    return delta, m, v


def reference(x, ffn1_norm, ffn1_w_gate, ffn1_w_up, ffn1_w_down, mix_norm, w_in, b_forget, b_gate_dil, b_gate_fox, w_proj_dil, w_proj_fox, w_out, ffn2_norm, ffn2_w_gate, ffn2_w_up, ffn2_w_down, final_norm, loss_target, m_ffn1_norm, m_ffn1_w_gate, m_ffn1_w_up, m_ffn1_w_down, m_mix_norm, m_w_in, m_b_forget, m_b_gate_dil, m_b_gate_fox, m_w_proj_dil, m_w_proj_fox, m_w_out, m_ffn2_norm, m_ffn2_w_gate, m_ffn2_w_up, m_ffn2_w_down, m_final_norm, v_ffn1_norm, v_ffn1_w_gate, v_ffn1_w_up, v_ffn1_w_down, v_mix_norm, v_w_in, v_b_forget, v_b_gate_dil, v_b_gate_fox, v_w_proj_dil, v_w_proj_fox, v_w_out, v_ffn2_norm, v_ffn2_w_gate, v_ffn2_w_up, v_ffn2_w_down, v_final_norm):
    given = dict(x=x, ffn1_norm=ffn1_norm, ffn1_w_gate=ffn1_w_gate, ffn1_w_up=ffn1_w_up, ffn1_w_down=ffn1_w_down, mix_norm=mix_norm, w_in=w_in, b_forget=b_forget, b_gate_dil=b_gate_dil, b_gate_fox=b_gate_fox, w_proj_dil=w_proj_dil, w_proj_fox=w_proj_fox, w_out=w_out, ffn2_norm=ffn2_norm, ffn2_w_gate=ffn2_w_gate, ffn2_w_up=ffn2_w_up, ffn2_w_down=ffn2_w_down, final_norm=final_norm, loss_target=loss_target, m_ffn1_norm=m_ffn1_norm, m_ffn1_w_gate=m_ffn1_w_gate, m_ffn1_w_up=m_ffn1_w_up, m_ffn1_w_down=m_ffn1_w_down, m_mix_norm=m_mix_norm, m_w_in=m_w_in, m_b_forget=m_b_forget, m_b_gate_dil=m_b_gate_dil, m_b_gate_fox=m_b_gate_fox, m_w_proj_dil=m_w_proj_dil, m_w_proj_fox=m_w_proj_fox, m_w_out=m_w_out, m_ffn2_norm=m_ffn2_norm, m_ffn2_w_gate=m_ffn2_w_gate, m_ffn2_w_up=m_ffn2_w_up, m_ffn2_w_down=m_ffn2_w_down, m_final_norm=m_final_norm, v_ffn1_norm=v_ffn1_norm, v_ffn1_w_gate=v_ffn1_w_gate, v_ffn1_w_up=v_ffn1_w_up, v_ffn1_w_down=v_ffn1_w_down, v_mix_norm=v_mix_norm, v_w_in=v_w_in, v_b_forget=v_b_forget, v_b_gate_dil=v_b_gate_dil, v_b_gate_fox=v_b_gate_fox, v_w_proj_dil=v_w_proj_dil, v_w_proj_fox=v_w_proj_fox, v_w_out=v_w_out, v_ffn2_norm=v_ffn2_norm, v_ffn2_w_gate=v_ffn2_w_gate, v_ffn2_w_up=v_ffn2_w_up, v_ffn2_w_down=v_ffn2_w_down, v_final_norm=v_final_norm)
    weights = {n: given[n] for n in TWIN_WEIGHTS}
    shared = {n: given[n] for n in SHARED_INPUTS}
    per_example = {n: given[n] for n in ['x']}
    grad_fn = _jax.value_and_grad(_loss, argnums=(0, 1))

    def one_microbatch(ex, loss_target):
        ex = dict(ex)
        diff = ex.pop(TWIN_DIFF_INPUT)
        return grad_fn(weights, diff, {**shared, **ex}, loss_target)

    if N_MICROBATCH == 1:
        loss, (grad_w, grad_x) = one_microbatch(per_example, given["loss_target"])
    else:
        def body(carry, xs):
            loss_sum, grad_sum = carry
            l_k, (gw_k, gx_k) = one_microbatch(xs[0], xs[1])
            with _jax.named_scope("update"):
                return (loss_sum + l_k, _jax.tree.map(_jnp.add, grad_sum, gw_k)), gx_k

        init = (_jnp.zeros((), _jnp.float32), _jax.tree.map(_jnp.zeros_like, weights))
        (loss, grad_w), grad_x = _jax.lax.scan(body, init, (per_example, given["loss_target"]))
    with _jax.named_scope("update"):
        delta_w, new_m, new_v = {}, {}, {}
        for n in TWIN_WEIGHTS:
            delta_w[n], new_m[n], new_v[n] = _adamw(weights[n], grad_w[n], given["m_" + n], given["v_" + n])
    return (loss, grad_x, *[grad_w[n] for n in TWIN_WEIGHTS], *[delta_w[n] for n in TWIN_WEIGHTS],
            *[new_m[n] for n in TWIN_WEIGHTS], *[new_v[n] for n in TWIN_WEIGHTS])
```

```python
import functools
import math

import numpy as np
import jax
import jax.numpy as jnp
from jax import lax
from jax.experimental import pallas as pl
from jax.experimental.pallas import tpu as pltpu

HEAD_DIM = 128
ROPE_DIM = HEAD_DIM // 4
ROPE_THETA = 500000.0
DIL_PATTERNS = ((128, 1), (512, 4), (2048, 16))
MAX_WINDOW = 2048
NORM_EPS = 1e-6
ADAM_LR = 0.001
ADAM_B1 = 0.9
ADAM_B2 = 0.999
ADAM_EPS = 1e-08
ADAM_WD = 0.01
ADAM_STEP = 10

BF = jnp.bfloat16
F32 = jnp.float32
NEG = -1e30
LANES = 128
ATT_BLOCK = 256
FLAT_COLS = 1024
FLAT_ROW_ALIGN = 1024
VMEM_LIMIT = 56 * 1024 * 1024
N_CHIPS = 4
N_DEV = 8
MESH = pl.DeviceIdType.MESH

SHARDED = ("ffn1_w_gate", "ffn1_w_up", "ffn1_w_down", "w_in", "w_proj_dil", "w_proj_fox", "w_out",
           "ffn2_w_gate", "ffn2_w_up", "ffn2_w_down")
ROW_SHARDED = ("ffn1_w_down", "w_out", "ffn2_w_down")
SMALL = ("ffn1_norm", "mix_norm", "b_forget", "b_gate_dil", "b_gate_fox", "ffn2_norm", "final_norm")
WEIGHTS = ("ffn1_norm", "ffn1_w_gate", "ffn1_w_up", "ffn1_w_down", "mix_norm", "w_in", "b_forget",
           "b_gate_dil", "b_gate_fox", "w_proj_dil", "w_proj_fox", "w_out", "ffn2_norm", "ffn2_w_gate",
           "ffn2_w_up", "ffn2_w_down", "final_norm")


def _pick(n, target, align):
    best = None
    for d in range(align, min(n, target) + 1, align):
        if n % d == 0:
            best = d
    return n if best is None else best


def _params(sem=None):
    return pltpu.CompilerParams(dimension_semantics=sem, vmem_limit_bytes=VMEM_LIMIT)


_DIMS = {"nn": (((1,), (0,)), ((), ())), "nt": (((1,), (1,)), ((), ())), "tn": (((0,), (0,)), ((), ()))}


def _mm(name, pairs, mode, m, n, k, *, epilogue, out_dtypes, extras=(), rows=(), n_colsum=0,
        sum_pairs=False, tm=1024, tn=512, tk=512):
    tm = _pick(m, tm, LANES if mode == "tn" else 8)
    tn, tk = _pick(n, tn, LANES), _pick(k, tk, LANES)
    nk = k // tk
    n_acc = 1 if sum_pairs else len(pairs)
    n_in = 2 * len(pairs) + len(extras) + len(rows)
    n_out = len(out_dtypes) + n_colsum

    def body(*refs):
        ins, outs, accs = refs[:n_in], refs[n_in:n_in + n_out], refs[n_in + n_out:]
        kk = pl.program_id(2)

        @pl.when(kk == 0)
        def _():
            for acc in accs:
                acc[...] = jnp.zeros_like(acc)

        for p in range(len(pairs)):
            a = ins[2 * p][...].astype(BF)
            b = ins[2 * p + 1][...].astype(BF)
            accs[0 if sum_pairs else p][...] += lax.dot_general(a, b, _DIMS[mode], preferred_element_type=F32)

        @pl.when(kk == nk - 1)
        def _():
            ex = [r[...] for r in ins[2 * len(pairs):2 * len(pairs) + len(extras)]]
            rw = [r[...] for r in ins[2 * len(pairs) + len(extras):]]
            res = epilogue([acc[...] for acc in accs], ex, rw)
            for o, r in zip(outs, res):
                o[...] = r.astype(o.dtype)

    in_specs, args = [], []
    for a, b in pairs:
        if mode == "tn":
            in_specs.append(pl.BlockSpec((tk, tm), lambda i, j, kk: (kk, i)))
        else:
            in_specs.append(pl.BlockSpec((tm, tk), lambda i, j, kk: (i, kk)))
        if mode == "nt":
            in_specs.append(pl.BlockSpec((tn, tk), lambda i, j, kk: (j, kk)))
        else:
            in_specs.append(pl.BlockSpec((tk, tn), lambda i, j, kk: (kk, j)))
        args += [a, b]
    for arr, off in extras:
        assert off % tn == 0
        in_specs.append(pl.BlockSpec((tm, tn), functools.partial(lambda i, j, kk, o: (i, j + o), o=off // tn)))
        args.append(arr)
    for arr in rows:
        in_specs.append(pl.BlockSpec((1, tn), lambda i, j, kk: (0, j)))
        args.append(arr)
    out_specs = [pl.BlockSpec((tm, tn), lambda i, j, kk: (i, j)) for _ in out_dtypes]
    out_shape = [jax.ShapeDtypeStruct((m, n), d) for d in out_dtypes]
    for _ in range(n_colsum):
        out_specs.append(pl.BlockSpec((None, 1, tn), lambda i, j, kk: (i, 0, j)))
        out_shape.append(jax.ShapeDtypeStruct((m // tm, 1, n), F32))
    return pl.pallas_call(
        body, grid=(m // tm, n // tn, nk), in_specs=in_specs, out_specs=out_specs, out_shape=out_shape,
        scratch_shapes=[pltpu.VMEM((tm, tn), F32) for _ in range(n_acc)],
        compiler_params=_params(("parallel", "parallel", "arbitrary")), name=name)(*args)


def _sigmoid(z):
    return 1.0 / (1.0 + jnp.exp(-z))


def _row_tile(s):
    return _pick(s, 256, 8)


def _fold8(t):
    r, d = t.shape
    return jnp.sum(t.reshape(r // 8, 8, d), axis=0)


def _rms_fwd(name, x, g):
    s, d = x.shape
    tr = _row_tile(s)

    def body(x_ref, g_ref, h_ref):
        xf = x_ref[...]
        y = xf * lax.rsqrt(jnp.mean(xf * xf, axis=-1, keepdims=True) + NORM_EPS)
        h_ref[...] = (y * g_ref[...]).astype(BF)

    return pl.pallas_call(
        body, grid=(s // tr,),
        in_specs=[pl.BlockSpec((tr, d), lambda i: (i, 0)), pl.BlockSpec((1, d), lambda i: (0, 0))],
        out_specs=pl.BlockSpec((tr, d), lambda i: (i, 0)), out_shape=jax.ShapeDtypeStruct((s, d), BF),
        compiler_params=_params(("parallel",)), name=name)(x, g)


def _rms_bwd(name, x, g, dh, dres):
    s, d = x.shape
    tr = _row_tile(s)

    def body(x_ref, g_ref, dh_ref, dres_ref, dx_ref, dg_ref):
        @pl.when(pl.program_id(0) == 0)
        def _():
            dg_ref[...] = jnp.zeros_like(dg_ref)

        xf = x_ref[...]
        rstd = lax.rsqrt(jnp.mean(xf * xf, axis=-1, keepdims=True) + NORM_EPS)
        xhat = xf * rstd
        dhf = dh_ref[...]
        dg_ref[...] += _fold8(dhf * xhat)
        dxh = dhf * g_ref[...]
        dx_ref[...] = dres_ref[...] + rstd * (dxh - xhat * jnp.mean(dxh * xhat, axis=-1, keepdims=True))

    blk = pl.BlockSpec((tr, d), lambda i: (i, 0))
    return pl.pallas_call(
        body, grid=(s // tr,),
        in_specs=[blk, pl.BlockSpec((1, d), lambda i: (0, 0)), blk, blk],
        out_specs=[blk, pl.BlockSpec((8, d), lambda i: (0, 0))],
        out_shape=[jax.ShapeDtypeStruct((s, d), F32), jax.ShapeDtypeStruct((8, d), F32)],
        compiler_params=_params(("arbitrary",)), name=name)(x, g, dh, dres)


def _final(name, x, g, tgt):
    s, d = x.shape
    tr = _row_tile(s)

    def body(x_ref, g_ref, t_ref, dx_ref, dg_ref, sq_ref):
        @pl.when(pl.program_id(0) == 0)
        def _():
            dg_ref[...] = jnp.zeros_like(dg_ref)
            sq_ref[...] = jnp.zeros_like(sq_ref)

        xf = x_ref[...]
        rstd = lax.rsqrt(jnp.mean(xf * xf, axis=-1, keepdims=True) + NORM_EPS)
        xhat = xf * rstd
        gf = g_ref[...]
        err = xhat * gf - t_ref[...]
        sq_ref[...] += _fold8(err * err)
        dy = err * (1.0 / d)
        dg_ref[...] += _fold8(dy * xhat)
        dxh = dy * gf
        dx_ref[...] = rstd * (dxh - xhat * jnp.mean(dxh * xhat, axis=-1, keepdims=True))

    blk = pl.BlockSpec((tr, d), lambda i: (i, 0))
    acc = pl.BlockSpec((8, d), lambda i: (0, 0))
    return pl.pallas_call(
        body, grid=(s // tr,), in_specs=[blk, pl.BlockSpec((1, d), lambda i: (0, 0)), blk],
        out_specs=[blk, acc, acc],
        out_shape=[jax.ShapeDtypeStruct((s, d), F32), jax.ShapeDtypeStruct((8, d), F32),
                   jax.ShapeDtypeStruct((8, d), F32)],
        compiler_params=_params(("arbitrary",)), name=name)(x, g, tgt)


def _ffn_fwd(tag, x, g, w_gate, w_up, w_down):
    s, d = x.shape
    f = w_gate.shape[1]
    h = _rms_fwd(tag + "_rms", x, g)

    def up_epi(accs, ex, rw):
        a, b = accs
        return a, b, a * _sigmoid(a) * b

    a, b, act = _mm(tag + "_up", [(h, w_gate), (h, w_up)], "nn", s, f, d, epilogue=up_epi,
                    out_dtypes=(F32, F32, BF))

    def down_epi(accs, ex, rw):
        return (ex[0] + 0.5 * accs[0],)

    (y,) = _mm(tag + "_down", [(act, w_down)], "nn", s, d, f, epilogue=down_epi, out_dtypes=(F32,),
               extras=[(x, 0)])
    return y, (h, a, b, act)


def _ffn_bwd(tag, x, g, w_gate, w_up, w_down, saved, dy):
    s, d = x.shape
    f = w_gate.shape[1]
    h, a, b, act = saved

    def act_epi(accs, ex, rw):
        dact = 0.5 * accs[0]
        av, bv = ex
        sg = _sigmoid(av)
        return dact * bv * (sg * (1.0 + av * (1.0 - sg))), dact * (av * sg)

    da, db = _mm(tag + "_dact", [(dy, w_down)], "nt", s, f, d, epilogue=act_epi, out_dtypes=(BF, BF),
                 extras=[(a, 0), (b, 0)])
    ident = lambda accs, ex, rw: (accs[0],)
    (dw_down,) = _mm(tag + "_dwd", [(act, dy)], "tn", f, d, s, epilogue=lambda accs, ex, rw: (0.5 * accs[0],),
                     out_dtypes=(F32,))
    (dw_gate,) = _mm(tag + "_dwg", [(h, da)], "tn", d, f, s, epilogue=ident, out_dtypes=(F32,))
    (dw_up,) = _mm(tag + "_dwu", [(h, db)], "tn", d, f, s, epilogue=ident, out_dtypes=(F32,))
    (dh,) = _mm(tag + "_dh", [(da, w_gate), (db, w_up)], "nt", s, d, f, epilogue=ident, out_dtypes=(F32,),
                sum_pairs=True)
    dx, dg = _rms_bwd(tag + "_rmsb", x, g, dh, dy)
    return dx, dg, dw_gate, dw_up, dw_down


def _rope_tables(s):
    half = ROPE_DIM // 2
    pos = jnp.arange(s, dtype=F32)
    inv_freq = ROPE_THETA ** (-jnp.arange(0, ROPE_DIM, 2, dtype=F32) / ROPE_DIM)
    ang = pos[:, None] * inv_freq[None, :]
    cos, sin = jnp.cos(ang), jnp.sin(ang)
    rest = HEAD_DIM - ROPE_DIM
    cos_t = jnp.concatenate([cos, cos, jnp.ones((s, rest), F32)], axis=-1)
    sin_t = jnp.concatenate([-sin, sin, jnp.zeros((s, rest), F32)], axis=-1)
    return cos_t, sin_t


def _swap_halves(t):
    lane = lax.broadcasted_iota(jnp.int32, t.shape, 1)
    half = ROPE_DIM // 2
    return jnp.where(lane < half, pltpu.roll(t, HEAD_DIM - half, 1), pltpu.roll(t, half, 1))


def _dil_bias(blk):
    n_delta = MAX_WINDOW // blk + 1
    delta = jnp.arange(n_delta, dtype=jnp.int32)[:, None, None]
    r = jnp.arange(blk, dtype=jnp.int32)[None, :, None]
    c = jnp.arange(blk, dtype=jnp.int32)[None, None, :]
    o = delta * blk + r - c
    mult = jnp.zeros(o.shape, F32)
    for w, dd in DIL_PATTERNS:
        mult = mult + ((o >= 0) & (o <= w) & (o % dd == 0)).astype(F32)
    tab = jnp.where(mult > 0, jnp.log(jnp.maximum(mult, 1.0)), NEG)
    return tab, tab.transpose(0, 2, 1)


def _att_block(s):
    return _pick(s, ATT_BLOCK, LANES)


def _flash_fwd(name, qkv, q_off, k_off, v_off, n_heads, *, fox, tab=None, c_col=None, c_row=None):
    s = qkv.shape[0]
    blk = _att_block(s)
    nq = s // blk
    n_delta = MAX_WINDOW // blk + 1

    def body(*refs):
        if fox:
            q_ref, k_ref, v_ref, cc_ref, cr_ref, o_ref, lse_ref, acc, m_s, l_s = refs
        else:
            q_ref, k_ref, v_ref, tab_ref, o_ref, lse_ref, acc, m_s, l_s = refs
        i = pl.program_id(1)
        acc[...] = jnp.zeros_like(acc)
        m_s[...] = jnp.full_like(m_s, NEG)
        l_s[...] = jnp.zeros_like(l_s)
        q = q_ref[...]

        def step(j, carry):
            ks = pl.ds(pl.multiple_of(j * blk, blk), blk)
            sc = lax.dot_general(q, k_ref[ks, :], _DIMS["nt"], preferred_element_type=F32)
            if fox:
                sc = sc + cc_ref[...] - cr_ref[j]
                row = lax.broadcasted_iota(jnp.int32, sc.shape, 0) + i * blk
                col = lax.broadcasted_iota(jnp.int32, sc.shape, 1) + j * blk
                sc = jnp.where(col <= row, sc, NEG)
            else:
                sc = sc + tab_ref[i - j]
            m_prev = m_s[...]
            m_new = jnp.maximum(m_prev, jnp.max(sc, axis=-1, keepdims=True))
            alpha = jnp.exp(m_prev - m_new)
            p = jnp.exp(sc - m_new)
            l_s[...] = alpha * l_s[...] + jnp.sum(p, axis=-1, keepdims=True)
            acc[...] = alpha * acc[...] + jnp.dot(p.astype(BF), v_ref[ks, :], preferred_element_type=F32)
            m_s[...] = m_new
            return carry

        lo = 0 if fox else jnp.maximum(i - (n_delta - 1), 0)
        lax.fori_loop(lo, i + 1, step, 0)
        o_ref[...] = (acc[...] / l_s[...]).astype(o_ref.dtype)
        lse_ref[...] = m_s[...] + jnp.log(l_s[...])

    off = lambda o: functools.partial(lambda h, i, o: (0, o + h), o=o)
    in_specs = [pl.BlockSpec((blk, HEAD_DIM), functools.partial(lambda h, i, o: (i, o + h), o=q_off)),
                pl.BlockSpec((s, HEAD_DIM), off(k_off)), pl.BlockSpec((s, HEAD_DIM), off(v_off))]
    args = [qkv, qkv, qkv]
    if fox:
        in_specs += [pl.BlockSpec((None, blk, 1), lambda h, i: (h, i, 0)),
                     pl.BlockSpec((None, nq, 1, blk), lambda h, i: (h, 0, 0, 0))]
        args += [c_col, c_row]
    else:
        in_specs.append(pl.BlockSpec((n_delta, blk, blk), lambda h, i: (0, 0, 0)))
        args.append(tab)
    return pl.pallas_call(
        body, grid=(n_heads, nq), in_specs=in_specs,
        out_specs=[pl.BlockSpec((blk, HEAD_DIM), lambda h, i: (i, h)),
                   pl.BlockSpec((None, blk, 1), lambda h, i: (h, i, 0))],
        out_shape=[jax.ShapeDtypeStruct((s, n_heads * HEAD_DIM), BF), jax.ShapeDtypeStruct((n_heads, s, 1), F32)],
        scratch_shapes=[pltpu.VMEM((blk, HEAD_DIM), F32), pltpu.VMEM((blk, 1), F32), pltpu.VMEM((blk, 1), F32)],
        compiler_params=_params(("parallel", "parallel")), name=name)(*args)


def _att_delta(name, do, o, n_heads):
    s = do.shape[0]
    blk = _att_block(s)

    def body(do_ref, o_ref, d_ref):
        d_ref[...] = jnp.sum(do_ref[...].astype(F32) * o_ref[...].astype(F32), axis=-1, keepdims=True)

    spec = pl.BlockSpec((blk, HEAD_DIM), lambda h, i: (i, h))
    return pl.pallas_call(
        body, grid=(n_heads, s // blk), in_specs=[spec, spec],
        out_specs=pl.BlockSpec((None, blk, 1), lambda h, i: (h, i, 0)),
        out_shape=jax.ShapeDtypeStruct((n_heads, s, 1), F32),
        compiler_params=_params(("parallel", "parallel")), name=name)(do, o)


def _flash_bwd(name, qkv, q_off, k_off, v_off, do, lse_row, delta_row, n_heads, *, fox, tab_t=None,
               c_col=None, c_row=None):
    s = qkv.shape[0]
    blk = _att_block(s)
    nq = s // blk
    n_delta = MAX_WINDOW // blk + 1

    def body(*refs):
        if fox:
            (q_ref, do_ref, k_ref, v_ref, lse_ref, dl_ref, cr_ref, cc_ref,
             dq_ref, dk_ref, dv_ref, dc_ref, dcq_ref, dk_acc, dv_acc, dc_acc) = refs
        else:
            (q_ref, do_ref, k_ref, v_ref, lse_ref, dl_ref, tab_ref,
             dq_ref, dk_ref, dv_ref, dk_acc, dv_acc) = refs
        j = pl.program_id(1)

        @pl.when(j == 0)
        def _():
            dq_ref[...] = jnp.zeros_like(dq_ref)
            if fox:
                dcq_ref[...] = jnp.zeros_like(dcq_ref)

        dk_acc[...] = jnp.zeros_like(dk_acc)
        dv_acc[...] = jnp.zeros_like(dv_acc)
        if fox:
            dc_acc[...] = jnp.zeros_like(dc_acc)
        kb = k_ref[...]
        vb = v_ref[...]

        def step(i, carry):
            qs = pl.ds(pl.multiple_of(i * blk, blk), blk)
            qb = q_ref[qs, :]
            dob = do_ref[qs, :]
            st = lax.dot_general(kb, qb, _DIMS["nt"], preferred_element_type=F32)
            if fox:
                st = st + cr_ref[i] - cc_ref[...]
                kpos = lax.broadcasted_iota(jnp.int32, st.shape, 0) + j * blk
                qpos = lax.broadcasted_iota(jnp.int32, st.shape, 1) + i * blk
                st = jnp.where(kpos <= qpos, st, NEG)
            else:
                st = st + tab_ref[i - j]
            pt = jnp.exp(st - lse_ref[i])
            dv_acc[...] += jnp.dot(pt.astype(BF), dob, preferred_element_type=F32)
            dpt = lax.dot_general(vb, dob, _DIMS["nt"], preferred_element_type=F32)
            dst = pt * (dpt - dl_ref[i])
            dsb = dst.astype(BF)
            dk_acc[...] += jnp.dot(dsb, qb, preferred_element_type=F32)
            dq_ref[qs, :] += lax.dot_general(dsb, kb, _DIMS["tn"], preferred_element_type=F32)
            if fox:
                dc_acc[...] -= jnp.sum(dst, axis=-1, keepdims=True)
                dcq_ref[i] += jnp.sum(dst, axis=0, keepdims=True)
            return carry

        hi = nq if fox else jnp.minimum(nq, j + n_delta)
        lax.fori_loop(j, hi, step, 0)
        dk_ref[...] = dk_acc[...]
        dv_ref[...] = dv_acc[...].astype(dv_ref.dtype)
        if fox:
            dc_ref[...] = dc_acc[...]

    full = lambda o: pl.BlockSpec((s, HEAD_DIM), functools.partial(lambda h, j, o: (0, o + h), o=o))
    tile = lambda o: pl.BlockSpec((blk, HEAD_DIM), functools.partial(lambda h, j, o: (j, o + h), o=o))
    per_q = pl.BlockSpec((None, nq, 1, blk), lambda h, j: (h, 0, 0, 0))
    per_k = pl.BlockSpec((None, blk, 1), lambda h, j: (h, j, 0))
    in_specs = [full(q_off), full(0), tile(k_off), tile(v_off), per_q, per_q]
    args = [qkv, do, qkv, qkv, lse_row, delta_row]
    out_specs = [full(0), tile(0), tile(0)]
    hd = n_heads * HEAD_DIM
    out_shape = [jax.ShapeDtypeStruct((s, hd), F32), jax.ShapeDtypeStruct((s, hd), F32),
                 jax.ShapeDtypeStruct((s, hd), BF)]
    scratch = [pltpu.VMEM((blk, HEAD_DIM), F32), pltpu.VMEM((blk, HEAD_DIM), F32)]
    if fox:
        in_specs += [per_q, per_k]
        args += [c_row, c_col]
        out_specs += [per_k, per_q]
        out_shape += [jax.ShapeDtypeStruct((n_heads, s, 1), F32), jax.ShapeDtypeStruct((n_heads, nq, 1, blk), F32)]
        scratch.append(pltpu.VMEM((blk, 1), F32))
    else:
        in_specs.append(pl.BlockSpec((n_delta, blk, blk), lambda h, j: (0, 0, 0)))
        args.append(tab_t)
    return pl.pallas_call(
        body, grid=(n_heads, nq), in_specs=in_specs, out_specs=out_specs, out_shape=out_shape,
        scratch_shapes=scratch, compiler_params=_params(("parallel", "arbitrary")), name=name)(*args)


def _split3(t):
    hi = t.astype(BF)
    r1 = t - hi.astype(F32)
    mid = r1.astype(BF)
    lo = (r1 - mid.astype(F32)).astype(BF)
    return hi, mid, lo


def _tri_dot(tri, t):
    hi, mid, lo = _split3(t)
    return (jnp.dot(tri, hi, preferred_element_type=F32) + jnp.dot(tri, mid, preferred_element_type=F32)
            + jnp.dot(tri, lo, preferred_element_type=F32))


def _log_sigmoid(z):
    return jnp.minimum(z, 0.0) - jnp.log(1.0 + jnp.exp(-jnp.abs(z)))


def _forget_cumsum(name, proj, f_col, bias):
    s = proj.shape[0]
    blk = _att_block(s)

    def body(f_ref, b_ref, c_ref, carry):
        @pl.when(pl.program_id(0) == 0)
        def _():
            carry[...] = jnp.zeros_like(carry)

        lf = _log_sigmoid(f_ref[...] + b_ref[...])
        r = lax.broadcasted_iota(jnp.int32, (blk, blk), 0)
        c = lax.broadcasted_iota(jnp.int32, (blk, blk), 1)
        tri = (c <= r).astype(BF)
        c_ref[...] = _tri_dot(tri, lf) + carry[...]
        carry[...] = c_ref[pl.ds(blk - 1, 1), :]

    return pl.pallas_call(
        body, grid=(s // blk,),
        in_specs=[pl.BlockSpec((blk, LANES), lambda i: (i, f_col)), pl.BlockSpec((1, LANES), lambda i: (0, 0))],
        out_specs=pl.BlockSpec((blk, LANES), lambda i: (i, 0)), out_shape=jax.ShapeDtypeStruct((s, LANES), F32),
        scratch_shapes=[pltpu.VMEM((1, LANES), F32)],
        compiler_params=_params(("arbitrary",)), name=name)(proj, bias)


def _forget_bwd(name, proj, f_col, bias, dc):
    s = proj.shape[0]
    blk = _att_block(s)
    nb = s // blk

    def body(f_ref, b_ref, dc_ref, df_ref, db_ref, carry):
        @pl.when(pl.program_id(0) == 0)
        def _():
            carry[...] = jnp.zeros_like(carry)
            db_ref[...] = jnp.zeros_like(db_ref)

        r = lax.broadcasted_iota(jnp.int32, (blk, blk), 0)
        c = lax.broadcasted_iota(jnp.int32, (blk, blk), 1)
        tri = (c >= r).astype(BF)
        r = lax.broadcasted_iota(jnp.int32, (blk, LANES), 0)
        dlf = _tri_dot(tri, dc_ref[...]) + carry[...]
        carry[...] = jnp.sum(jnp.where(r == 0, dlf, 0.0), axis=0, keepdims=True)
        dz = dlf * _sigmoid(-(f_ref[...] + b_ref[...]))
        df_ref[...] = dz.astype(BF)
        db_ref[...] += _fold8(dz)

    rev = lambda i: (nb - 1 - i, 0)
    return pl.pallas_call(
        body, grid=(nb,),
        in_specs=[pl.BlockSpec((blk, LANES), lambda i: (nb - 1 - i, f_col)), pl.BlockSpec((1, LANES), lambda i: (0, 0)),
                  pl.BlockSpec((blk, LANES), rev)],
        out_specs=[pl.BlockSpec((blk, LANES), rev), pl.BlockSpec((8, LANES), lambda i: (0, 0))],
        out_shape=[jax.ShapeDtypeStruct((s, LANES), BF), jax.ShapeDtypeStruct((8, LANES), F32)],
        scratch_shapes=[pltpu.VMEM((1, LANES), F32)],
        compiler_params=_params(("arbitrary",)), name=name)(proj, bias, dc)


def _head_prep(name, proj, cos_t, sin_t, n_heads):
    s = proj.shape[0]
    tr = _row_tile(s)
    scale = HEAD_DIM ** -0.5
    hh = n_heads

    def body(p_ref, cos_ref, sin_ref, o_ref):
        j = pl.program_id(1)
        t = p_ref[...]

        @pl.when(j < 2 * hh)
        def _():
            r = t * cos_ref[...] + _swap_halves(t) * sin_ref[...]
            o_ref[...] = jnp.where(j < hh, r * scale, r).astype(BF)

        @pl.when(j >= 2 * hh)
        def _():
            is_q = jnp.logical_and(j >= 3 * hh, j < 4 * hh)
            o_ref[...] = jnp.where(is_q, t * scale, t).astype(BF)

    tab = pl.BlockSpec((tr, HEAD_DIM), lambda i, j: (i, 0))
    blk = pl.BlockSpec((tr, HEAD_DIM), lambda i, j: (i, j))
    return pl.pallas_call(
        body, grid=(s // tr, 6 * hh), in_specs=[blk, tab, tab], out_specs=blk,
        out_shape=jax.ShapeDtypeStruct((s, 6 * hh * HEAD_DIM), BF),
        compiler_params=_params(("parallel", "arbitrary")), name=name)(proj, cos_t, sin_t)


def _dproj_assemble(name, parts, dgate, df, cos_t, sin_t, n_heads, d_model):
    s = cos_t.shape[0]
    tr = _row_tile(s)
    scale = HEAD_DIM ** -0.5
    hh = n_heads
    n_head_blocks = 6 * hh
    n_gate_blocks = 2 * d_model // HEAD_DIM
    n_blocks = n_head_blocks + n_gate_blocks + 1

    def body(*refs):
        p_refs, g_ref, f_ref, cos_ref, sin_ref, o_ref = refs[:6], refs[6], refs[7], refs[8], refs[9], refs[10]
        j = pl.program_id(1)
        for kind in range(6):
            @pl.when(jnp.logical_and(j >= kind * hh, j < (kind + 1) * hh))
            def _(kind=kind):
                t = p_refs[kind][...].astype(F32)
                if kind in (0, 3):
                    t = t * scale
                if kind in (0, 1):
                    t = t * cos_ref[...] - _swap_halves(t) * sin_ref[...]
                o_ref[...] = t.astype(BF)

        @pl.when(jnp.logical_and(j >= n_head_blocks, j < n_head_blocks + n_gate_blocks))
        def _():
            o_ref[...] = g_ref[...]

        @pl.when(j == n_blocks - 1)
        def _():
            o_ref[...] = f_ref[...]

    def head_spec(kind):
        return pl.BlockSpec((tr, HEAD_DIM), functools.partial(
            lambda i, j, kind: (i, jnp.clip(j - kind * hh, 0, hh - 1)), kind=kind))

    in_specs = [head_spec(kind) for kind in range(6)]
    in_specs.append(pl.BlockSpec((tr, HEAD_DIM), lambda i, j: (i, jnp.clip(j - n_head_blocks, 0, n_gate_blocks - 1))))
    tab = pl.BlockSpec((tr, HEAD_DIM), lambda i, j: (i, 0))
    in_specs += [tab, tab, tab]
    return pl.pallas_call(
        body, grid=(s // tr, n_blocks), in_specs=in_specs,
        out_specs=pl.BlockSpec((tr, HEAD_DIM), lambda i, j: (i, j)),
        out_shape=jax.ShapeDtypeStruct((s, n_blocks * HEAD_DIM), BF),
        compiler_params=_params(("parallel", "arbitrary")), name=name)(*parts, dgate, df, cos_t, sin_t)


def _per_head(c_pad, n_heads, blk):
    s = c_pad.shape[0]
    t = c_pad[:, :n_heads].T
    return t.reshape(n_heads, s, 1), t.reshape(n_heads, s // blk, 1, blk)


def _local_step(x, tgt, w, small):
    s, d = x.shape
    hh = w["w_proj_dil"].shape[0] // HEAD_DIM
    hd = hh * HEAD_DIM
    blk = _att_block(s)
    gate_off = 6 * hd
    f_col = (gate_off + 2 * d) // LANES
    n_proj = gate_off + 2 * d + LANES
    ident = lambda accs, ex, rw: (accs[0],)

    x1, saved1 = _ffn_fwd("ffn1", x, small["ffn1_norm"], w["ffn1_w_gate"], w["ffn1_w_up"], w["ffn1_w_down"])

    hm = _rms_fwd("mix_rms", x1, small["mix_norm"])
    (proj,) = _mm("proj", [(hm, w["w_in"])], "nn", s, n_proj, d, epilogue=ident, out_dtypes=(F32,))
    cos_t, sin_t = _rope_tables(s)
    qkv = _head_prep("head_prep", proj, cos_t, sin_t, hh)
    tab, tab_t = _dil_bias(blk)
    y_dil, lse_d = _flash_fwd("dil_fwd", qkv, 0, hh, 2 * hh, hh, fox=False, tab=tab)
    bias_f = jnp.pad(small["b_forget"], ((0, 0), (0, LANES - hh)))
    c_pad = _forget_cumsum("forget_cumsum", proj, f_col, bias_f)
    c_col, c_row = _per_head(c_pad, hh, blk)
    y_fox, lse_f = _flash_fwd("fox_fwd", qkv, 3 * hh, 4 * hh, 5 * hh, hh, fox=True, c_col=c_col, c_row=c_row)

    def merge_epi(accs, ex, rw):
        ud, uf = accs
        return ud, uf, _sigmoid(ex[0] + rw[0]) * ud + _sigmoid(ex[1] + rw[1]) * uf

    u_d, u_f, merged = _mm("merge", [(y_dil, w["w_proj_dil"]), (y_fox, w["w_proj_fox"])], "nn", s, d, hd,
                           epilogue=merge_epi, out_dtypes=(F32, F32, BF),
                           extras=[(proj, gate_off), (proj, gate_off + d)],
                           rows=[small["b_gate_dil"], small["b_gate_fox"]])
    (x2,) = _mm("mix_out", [(merged, w["w_out"])], "nn", s, d, d,
                epilogue=lambda accs, ex, rw: (ex[0] + accs[0],), out_dtypes=(F32,), extras=[(x1, 0)])

    x3, saved2 = _ffn_fwd("ffn2", x2, small["ffn2_norm"], w["ffn2_w_gate"], w["ffn2_w_up"], w["ffn2_w_down"])
    dx3, dg_final, sq = _final("final", x3, small["final_norm"].reshape(1, d), tgt)

    dx2, dg_ffn2, dw_g2, dw_u2, dw_d2 = _ffn_bwd("ffn2", x2, small["ffn2_norm"], w["ffn2_w_gate"], w["ffn2_w_up"],
                                                 w["ffn2_w_down"], saved2, dx3)

    def dmerge_epi(accs, ex, rw):
        dm = accs[0]
        gd, gf, ud, uf = ex
        sd, sf = _sigmoid(gd + rw[0]), _sigmoid(gf + rw[1])
        dgd = dm * ud * (sd * (1.0 - sd))
        dgf = dm * uf * (sf * (1.0 - sf))
        return (dm * sd, dm * sf, dgd, dgf, jnp.sum(dgd, axis=0, keepdims=True), jnp.sum(dgf, axis=0, keepdims=True))

    du_d, du_f, dg_d, dg_f, dbg_d, dbg_f = _mm(
        "dmerge", [(dx2, w["w_out"])], "nt", s, d, d, epilogue=dmerge_epi, out_dtypes=(BF, BF, BF, BF), n_colsum=2,
        extras=[(proj, gate_off), (proj, gate_off + d), (u_d, 0), (u_f, 0)],
        rows=[small["b_gate_dil"], small["b_gate_fox"]])
    (dw_out,) = _mm("dw_out", [(merged, dx2)], "tn", d, d, s, epilogue=ident, out_dtypes=(F32,))
    (dw_pd,) = _mm("dw_pd", [(y_dil, du_d)], "tn", hd, d, s, epilogue=ident, out_dtypes=(F32,))
    (dw_pf,) = _mm("dw_pf", [(y_fox, du_f)], "tn", hd, d, s, epilogue=ident, out_dtypes=(F32,))
    (dy_dil,) = _mm("dy_dil", [(du_d, w["w_proj_dil"])], "nt", s, hd, d, epilogue=ident, out_dtypes=(BF,))
    (dy_fox,) = _mm("dy_fox", [(du_f, w["w_proj_fox"])], "nt", s, hd, d, epilogue=ident, out_dtypes=(BF,))

    row = lambda t: t.reshape(hh, s // blk, 1, blk)
    delta_d = _att_delta("dil_delta", dy_dil, y_dil, hh)
    dq_d, dk_d, dv_d = _flash_bwd("dil_bwd", qkv, 0, hh, 2 * hh, dy_dil, row(lse_d), row(delta_d), hh, fox=False,
                                  tab_t=tab_t)
    delta_f = _att_delta("fox_delta", dy_fox, y_fox, hh)
    dq_f, dk_f, dv_f, dc_k, dc_q = _flash_bwd("fox_bwd", qkv, 3 * hh, 4 * hh, 5 * hh, dy_fox, row(lse_f), row(delta_f), hh,
                                      fox=True, c_col=c_col, c_row=c_row)
    dc = dc_k.reshape(hh, s) + dc_q.reshape(hh, s)
    dc_pad = jnp.pad(dc.T, ((0, 0), (0, LANES - hh)))
    df, db_forget = _forget_bwd("forget_bwd", proj, f_col, bias_f, dc_pad)
    dgate = jnp.concatenate([dg_d, dg_f], axis=1)
    dproj = _dproj_assemble("dproj", [dq_d, dk_d, dv_d, dq_f, dk_f, dv_f], dgate, df, cos_t, sin_t, hh, d)
    (dhm,) = _mm("dhm", [(dproj, w["w_in"])], "nt", s, d, n_proj, epilogue=ident, out_dtypes=(F32,))
    (dw_in,) = _mm("dw_in", [(hm, dproj)], "tn", d, n_proj, s, epilogue=ident, out_dtypes=(F32,))
    dx1, dg_mix = _rms_bwd("mix_rmsb", x1, small["mix_norm"], dhm, dx2)

    dx0, dg_ffn1, dw_g1, dw_u1, dw_d1 = _ffn_bwd("ffn1", x, small["ffn1_norm"], w["ffn1_w_gate"], w["ffn1_w_up"],
                                                 w["ffn1_w_down"], saved1, dx1)

    grads = {"ffn1_w_gate": dw_g1, "ffn1_w_up": dw_u1, "ffn1_w_down": dw_d1, "w_in": dw_in, "w_proj_dil": dw_pd,
             "w_proj_fox": dw_pf, "w_out": dw_out, "ffn2_w_gate": dw_g2, "ffn2_w_up": dw_u2, "ffn2_w_down": dw_d2}
    partials = {"ffn1_norm": dg_ffn1, "mix_norm": dg_mix, "ffn2_norm": dg_ffn2, "final_norm": dg_final,
                "b_gate_dil": dbg_d.reshape(-1, d), "b_gate_fox": dbg_f.reshape(-1, d), "b_forget": db_forget, "sq": sq}
    return dx0, grads, partials


def _coords():
    return lax.axis_index("x"), lax.axis_index("y"), lax.axis_index("c")


def _other_chips(x, y):
    return [(1 - x, y), (x, 1 - y), (1 - x, 1 - y)]


def _gather_chips(name, shard):
    rows, cols = shard.shape
    half = rows // 2
    any_spec = pl.BlockSpec(memory_space=pl.ANY)

    def body(src, out, send_sems, recv_sems, local_sem):
        x, y, c = _coords()
        sibling = (x, y, 1 - c)
        chips = _other_chips(x, y)

        def slot(px, py, pc):
            return out.at[2 * px + py, pl.ds(pc * half, half), :]

        def copy(k, src_ref, dst_ref, to):
            return pltpu.make_async_remote_copy(src_ref=src_ref, dst_ref=dst_ref, send_sem=send_sems.at[k],
                                                recv_sem=recv_sems.at[k], device_id=to, device_id_type=MESH)

        mine = pltpu.make_async_copy(src, out.at[2 * x + y], local_sem)
        mine.start()
        first = [copy(j, src.at[pl.ds(c * half, half), :], slot(x, y, c), (*chip, c)) for j, chip in enumerate(chips)]
        for cp in first:
            cp.start()
        passed = [copy(3 + j, slot(*chip, c), slot(*chip, c), sibling) for j, chip in enumerate(chips)]
        for j, chip in enumerate(chips):
            copy(j, slot(*chip, c), slot(*chip, c), (*chip, c)).wait_recv()
            passed[j].start()
        for j, chip in enumerate(chips):
            copy(3 + j, slot(*chip, 1 - c), slot(*chip, 1 - c), sibling).wait_recv()
        for cp in first + passed:
            cp.wait_send()
        mine.wait()

    return pl.pallas_call(
        body, in_specs=[any_spec], out_specs=any_spec,
        out_shape=jax.ShapeDtypeStruct((N_CHIPS, rows, cols), shard.dtype),
        scratch_shapes=[pltpu.SemaphoreType.DMA((6,)), pltpu.SemaphoreType.DMA((6,)), pltpu.SemaphoreType.DMA],
        name=name)(shard)


def _swap_halves_d2d(name, g):
    n, rows, cols = g.shape
    half = rows // 2
    any_spec = pl.BlockSpec(memory_space=pl.ANY)

    def body(src, own, other, send_sem, recv_sem, local_sem):
        x, y, c = _coords()
        mine = pltpu.make_async_copy(src.at[:, pl.ds(c * half, half), :], own, local_sem)
        mine.start()
        cp = pltpu.make_async_remote_copy(src_ref=src.at[:, pl.ds((1 - c) * half, half), :], dst_ref=other,
                                          send_sem=send_sem, recv_sem=recv_sem, device_id=(x, y, 1 - c),
                                          device_id_type=MESH)
        cp.start()
        cp.wait()
        mine.wait()

    shape = jax.ShapeDtypeStruct((n, half, cols), g.dtype)
    return pl.pallas_call(
        body, in_specs=[any_spec], out_specs=[any_spec, any_spec], out_shape=[shape, shape],
        scratch_shapes=[pltpu.SemaphoreType.DMA, pltpu.SemaphoreType.DMA, pltpu.SemaphoreType.DMA], name=name)(g)


def _add_pair(name, a, b):
    n, rows, cols = a.shape
    tr = _pick(rows, 512, 8)

    def body(a_ref, b_ref, out_ref):
        out_ref[...] = a_ref[...] + b_ref[...]

    blk = pl.BlockSpec((None, tr, cols), lambda p, i: (p, i, 0))
    return pl.pallas_call(body, grid=(n, rows // tr), in_specs=[blk, blk], out_specs=blk,
                          out_shape=jax.ShapeDtypeStruct((n, rows, cols), F32),
                          compiler_params=_params(("parallel", "parallel")), name=name)(a, b)


def _scatter_chips(name, h):
    n, half, cols = h.shape
    any_spec = pl.BlockSpec(memory_space=pl.ANY)

    def body(src, out, send_sems, recv_sems, local_sem):
        x, y, c = _coords()
        me = 2 * x + y
        mine = pltpu.make_async_copy(src.at[me], out.at[me], local_sem)
        mine.start()
        chips = _other_chips(x, y)
        sends = [pltpu.make_async_remote_copy(src_ref=src.at[2 * px + py], dst_ref=out.at[me], send_sem=send_sems.at[k],
                                              recv_sem=recv_sems.at[k], device_id=(px, py, c), device_id_type=MESH)
                 for k, (px, py) in enumerate(chips)]
        for cp in sends:
            cp.start()
        for k, (px, py) in enumerate(chips):
            pltpu.make_async_remote_copy(src_ref=src.at[me], dst_ref=out.at[2 * px + py], send_sem=send_sems.at[k],
                                         recv_sem=recv_sems.at[k], device_id=(px, py, c),
                                         device_id_type=MESH).wait_recv()
        for cp in sends:
            cp.wait_send()
        mine.wait()

    return pl.pallas_call(
        body, in_specs=[any_spec], out_specs=any_spec, out_shape=jax.ShapeDtypeStruct((n, half, cols), h.dtype),
        scratch_shapes=[pltpu.SemaphoreType.DMA((3,)), pltpu.SemaphoreType.DMA((3,)), pltpu.SemaphoreType.DMA],
        name=name)(h)


def _sum_slots(name, u):
    n, rows, cols = u.shape
    tr = _pick(rows, 512, 8)

    def body(u_ref, out_ref):
        t = u_ref[0]
        for k in range(1, n):
            t = t + u_ref[k]
        out_ref[...] = t

    return pl.pallas_call(
        body, grid=(rows // tr,), in_specs=[pl.BlockSpec((n, tr, cols), lambda i: (0, i, 0))],
        out_specs=pl.BlockSpec((tr, cols), lambda i: (i, 0)), out_shape=jax.ShapeDtypeStruct((rows, cols), F32),
        compiler_params=_params(("parallel",)), name=name)(u)


def _join_halves(name, v):
    half, cols = v.shape
    any_spec = pl.BlockSpec(memory_space=pl.ANY)

    def body(src, out, send_sem, recv_sem, local_sem):
        x, y, c = _coords()
        mine = pltpu.make_async_copy(src, out.at[pl.ds(c * half, half), :], local_sem)
        mine.start()
        cp = pltpu.make_async_remote_copy(src_ref=src, dst_ref=out.at[pl.ds(c * half, half), :], send_sem=send_sem,
                                          recv_sem=recv_sem, device_id=(x, y, 1 - c), device_id_type=MESH)
        cp.start()
        pltpu.make_async_remote_copy(src_ref=src, dst_ref=out.at[pl.ds((1 - c) * half, half), :], send_sem=send_sem,
                                     recv_sem=recv_sem, device_id=(x, y, 1 - c), device_id_type=MESH).wait_recv()
        cp.wait_send()
        mine.wait()

    return pl.pallas_call(
        body, in_specs=[any_spec], out_specs=any_spec, out_shape=jax.ShapeDtypeStruct((2 * half, cols), v.dtype),
        scratch_shapes=[pltpu.SemaphoreType.DMA, pltpu.SemaphoreType.DMA, pltpu.SemaphoreType.DMA],
        name=name)(v)


def _gather_all(name, t):
    rows, cols = t.shape

    def body(src, out, send_sems, recv_sems, local_sem):
        x, y, c = _coords()
        me = 4 * x + 2 * y + c
        mine = pltpu.make_async_copy(src, out.at[me], local_sem)
        mine.start()
        peers = [(x ^ (k >> 2 & 1), y ^ (k >> 1 & 1), c ^ (k & 1)) for k in range(1, N_DEV)]
        sends = [pltpu.make_async_remote_copy(src_ref=src, dst_ref=out.at[me], send_sem=send_sems.at[k],
                                              recv_sem=recv_sems.at[k], device_id=peer, device_id_type=MESH)
                 for k, peer in enumerate(peers)]
        for cp in sends:
            cp.start()
        for k, (px, py, pc) in enumerate(peers):
            pltpu.make_async_remote_copy(src_ref=src, dst_ref=out.at[4 * px + 2 * py + pc], send_sem=send_sems.at[k],
                                         recv_sem=recv_sems.at[k], device_id=(px, py, pc),
                                         device_id_type=MESH).wait_recv()
        for cp in sends:
            cp.wait_send()
        mine.wait()

    vmem = pl.BlockSpec(memory_space=pltpu.VMEM)
    return pl.pallas_call(
        body, in_specs=[vmem], out_specs=vmem, out_shape=jax.ShapeDtypeStruct((N_DEV, rows, cols), t.dtype),
        scratch_shapes=[pltpu.SemaphoreType.DMA((7,)), pltpu.SemaphoreType.DMA((7,)), pltpu.SemaphoreType.DMA],
        name=name)(t)


def _adamw_math(w, g, m, v):
    m = ADAM_B1 * m + (1.0 - ADAM_B1) * g
    v = ADAM_B2 * v + (1.0 - ADAM_B2) * (g * g)
    m_hat = m / (1.0 - ADAM_B1 ** ADAM_STEP)
    v_hat = v / (1.0 - ADAM_B2 ** ADAM_STEP)
    delta = -ADAM_LR * (m_hat / (jnp.sqrt(v_hat) + ADAM_EPS) + ADAM_WD * w)
    return delta, m, v


def _adamw(name, w, g, m, v):
    rows, cols = w.shape
    tr = _pick(rows, 256, 8)

    def body(w_ref, g_ref, m_ref, v_ref, d_out, m_out, v_out):
        d_out[...], m_out[...], v_out[...] = _adamw_math(w_ref[...], g_ref[...], m_ref[...], v_ref[...])

    blk = pl.BlockSpec((tr, cols), lambda i: (i, 0))
    shape = jax.ShapeDtypeStruct((rows, cols), F32)
    return pl.pallas_call(body, grid=(rows // tr,), in_specs=[blk] * 4, out_specs=[blk] * 3, out_shape=[shape] * 3,
                          compiler_params=_params(("parallel",)), name=name)(w, g, m, v)


def _small_reduce(name, parts, width):
    def body(*refs):
        out = refs[-1]
        out[...] = jnp.zeros_like(out)
        for k, r in enumerate(refs[:-1]):
            out[pl.ds(k, 1), :] = jnp.sum(r[...], axis=0, keepdims=True)

    vmem = pl.BlockSpec(memory_space=pltpu.VMEM)
    return pl.pallas_call(body, in_specs=[vmem] * len(parts), out_specs=vmem,
                          out_shape=jax.ShapeDtypeStruct((8, width), F32), name=name)(*parts)


def _small_adamw(name, gathered, w, m, v, loss_row, loss_scale):
    def body(gt_ref, w_ref, m_ref, v_ref, g_out, d_out, m_out, v_out, loss_out):
        g = gt_ref[0]
        for k in range(1, N_DEV):
            g = g + gt_ref[k]
        g_out[...] = g
        row = lax.broadcasted_iota(jnp.int32, g.shape, 0)
        loss_out[...] = jnp.sum(jnp.where(row == loss_row, g, 0.0), keepdims=True) * loss_scale
        d_out[...], m_out[...], v_out[...] = _adamw_math(w_ref[...], g, m_ref[...], v_ref[...])

    vmem = pl.BlockSpec(memory_space=pltpu.VMEM)
    shape = jax.ShapeDtypeStruct(w.shape, F32)
    return pl.pallas_call(body, in_specs=[vmem] * 4, out_specs=[vmem] * 5,
                          out_shape=[shape] * 4 + [jax.ShapeDtypeStruct((1, 1), F32)], name=name)(gathered, w, m, v)


def _flat_rows(n):
    return -(-n // FLAT_COLS)


def _flat_layout(shard_shapes):
    layout, r = {}, 0
    for name in SHARDED:
        n = int(np.prod(shard_shapes[name]))
        layout[name] = (r, _flat_rows(n))
        r += _flat_rows(n)
    return layout, -(-r // FLAT_ROW_ALIGN) * FLAT_ROW_ALIGN


def _to_flat(pieces, layout, total_rows, lead):
    segs, r = [], 0
    for name in SHARDED:
        t = pieces[name]
        n = t.shape[-1]
        rows = layout[name][1]
        t = jnp.pad(t, [(0, 0)] * len(lead) + [(0, rows * FLAT_COLS - n)])
        segs.append(t.reshape(*lead, rows, FLAT_COLS))
        r += rows
    if total_rows > r:
        segs.append(jnp.zeros((*lead, total_rows - r, FLAT_COLS), segs[0].dtype))
    return jnp.concatenate(segs, axis=len(lead))


def _from_flat(flat, layout, name, n):
    r0, rows = layout[name]
    t = flat[..., r0:r0 + rows, :]
    return t.reshape(*t.shape[:-2], rows * FLAT_COLS)[..., :n]


def _full_from_pieces(name, pieces, shard_shape):
    rows, cols = shard_shape
    t = pieces.reshape(N_CHIPS, rows, cols)
    if name in ROW_SHARDED:
        return t.reshape(N_CHIPS * rows, cols)
    return t.transpose(1, 0, 2).reshape(rows, N_CHIPS * cols)


def _pieces_from_full(name, full, shard_shape):
    rows, cols = shard_shape
    if name in ROW_SHARDED:
        return full.reshape(N_CHIPS, rows * cols)
    return full.reshape(rows, N_CHIPS, cols).transpose(1, 0, 2).reshape(N_CHIPS, rows * cols)


def _repack_w_in(w_in, hd, d):
    hh = hd // HEAD_DIM
    qkv, f, gates = w_in[:, :6 * hd], w_in[:, 6 * hd:6 * hd + hh], w_in[:, 6 * hd + hh:]
    return jnp.concatenate([qkv, gates, jnp.pad(f, ((0, 0), (0, LANES - hh)))], axis=1)


def _unpack_dw_in(dw, hd, d):
    hh = hd // HEAD_DIM
    return jnp.concatenate([dw[:, :6 * hd], dw[:, 6 * hd + 2 * d:6 * hd + 2 * d + hh], dw[:, 6 * hd:6 * hd + 2 * d]],
                           axis=1)


def _small_pack(vals, width):
    rows = []
    for name in SMALL:
        t = vals[name].reshape(1, -1)
        rows.append(jnp.pad(t, ((0, 0), (0, width - t.shape[1]))))
    rows.append(jnp.zeros((8 - len(SMALL), width), F32))
    return jnp.concatenate(rows, axis=0)


def kernel(x, ffn1_norm, ffn1_w_gate, ffn1_w_up, ffn1_w_down, mix_norm, w_in, b_forget, b_gate_dil, b_gate_fox, w_proj_dil, w_proj_fox, w_out, ffn2_norm, ffn2_w_gate, ffn2_w_up, ffn2_w_down, final_norm, loss_target, m_ffn1_norm, m_ffn1_w_gate, m_ffn1_w_up, m_ffn1_w_down, m_mix_norm, m_w_in, m_b_forget, m_b_gate_dil, m_b_gate_fox, m_w_proj_dil, m_w_proj_fox, m_w_out, m_ffn2_norm, m_ffn2_w_gate, m_ffn2_w_up, m_ffn2_w_down, m_final_norm, v_ffn1_norm, v_ffn1_w_gate, v_ffn1_w_up, v_ffn1_w_down, v_mix_norm, v_w_in, v_b_forget, v_b_gate_dil, v_b_gate_fox, v_w_proj_dil, v_w_proj_fox, v_w_out, v_ffn2_norm, v_ffn2_w_gate, v_ffn2_w_up, v_ffn2_w_down, v_final_norm):
    given = dict(locals())
    wts = {n: given[n] for n in WEIGHTS}
    mom_m = {n: given["m_" + n] for n in WEIGHTS}
    mom_v = {n: given["v_" + n] for n in WEIGHTS}
    s, d = x.shape[1], x.shape[2]
    hd = w_proj_dil.shape[1]
    shard_shapes = {n: wts[n].shape[1:] for n in SHARDED}
    layout, total_rows = _flat_layout(shard_shapes)

    flat_w = _to_flat({n: wts[n].astype(BF).reshape(-1) for n in SHARDED}, layout, total_rows, ())
    gathered = _gather_chips("gather_weights", flat_w)
    full = {}
    for n in SHARDED:
        pieces = _from_flat(gathered, layout, n, int(np.prod(shard_shapes[n])))
        full[n] = _full_from_pieces(n, pieces, shard_shapes[n])
    full["w_in"] = _repack_w_in(full["w_in"], hd, d)
    small = {n: wts[n] for n in SMALL}

    grad_x, grads, partials = _local_step(x[0], loss_target[0], full, small)

    grads["w_in"] = _unpack_dw_in(grads["w_in"], hd, d)
    flat_g = _to_flat({n: _pieces_from_full(n, grads[n], shard_shapes[n]) for n in SHARDED}, layout, total_rows,
                      (N_CHIPS,))
    own, from_sibling = _swap_halves_d2d("rs_swap", flat_g)
    chip_sum = _add_pair("rs_add", own, from_sibling)
    from_chips = _scatter_chips("rs_scatter", chip_sum)
    my_half = _sum_slots("rs_sum", from_chips)
    flat_total = _join_halves("rs_join", my_half)

    out_g, out_d, out_m, out_v = {}, {}, {}, {}
    for n in SHARDED:
        shp = shard_shapes[n]
        g = _from_flat(flat_total, layout, n, int(np.prod(shp))).reshape(shp)
        dl, nm, nv = _adamw("adamw_" + n, wts[n][0], g, mom_m[n][0], mom_v[n][0])
        out_g[n], out_d[n], out_m[n], out_v[n] = g[None], dl[None], nm[None], nv[None]

    width = d
    part_rows = []
    for n in SMALL:
        t = partials[n]
        part_rows.append(jnp.pad(t, ((0, 0), (0, width - t.shape[1]))))
    part_rows.append(partials["sq"])
    local_small = _small_reduce("small_reduce", part_rows, width)
    gathered_small = _gather_all("small_gather", local_small)
    sg, sd_, sm, sv, loss = _small_adamw("small_adamw", gathered_small, _small_pack(wts, width),
                                         _small_pack(mom_m, width), _small_pack(mom_v, width), len(SMALL), 0.5 / d)
    for k, n in enumerate(SMALL):
        shp = wts[n].shape
        take = lambda t: t[k, :shp[-1]].reshape(shp)
        out_g[n], out_d[n], out_m[n], out_v[n] = take(sg), take(sd_), take(sm), take(sv)
    return (loss[0, 0], grad_x[None], *[out_g[n] for n in WEIGHTS], *[out_d[n] for n in WEIGHTS],
            *[out_m[n] for n in WEIGHTS], *[out_v[n] for n in WEIGHTS])
```

```python
import functools
import math

import numpy as np
import jax
import jax.numpy as jnp
from jax import lax
from jax.experimental import pallas as pl
from jax.experimental.pallas import tpu as pltpu

HEAD_DIM = 128
ROPE_DIM = HEAD_DIM // 4
ROPE_THETA = 500000.0
DIL_PATTERNS = ((128, 1), (512, 4), (2048, 16))
MAX_WINDOW = 2048
NORM_EPS = 1e-6
ADAM_LR = 0.001
ADAM_B1 = 0.9
ADAM_B2 = 0.999
ADAM_EPS = 1e-08
ADAM_WD = 0.01
ADAM_STEP = 10

BF = jnp.bfloat16
F32 = jnp.float32
NEG = -1e30
LANES = 128
ATT_BLOCK = 256
VMEM_LIMIT = 56 * 1024 * 1024
MM_VMEM_BUDGET = 40 * 1024 * 1024
N_CHIPS = 4
N_DEV = 8
MESH = pl.DeviceIdType.MESH

SHARDED = ("ffn1_w_gate", "ffn1_w_up", "ffn1_w_down", "w_in", "w_proj_dil", "w_proj_fox", "w_out",
           "ffn2_w_gate", "ffn2_w_up", "ffn2_w_down")
ROW_SHARDED = ("ffn1_w_down", "w_out", "ffn2_w_down")
SMALL = ("ffn1_norm", "mix_norm", "b_forget", "b_gate_dil", "b_gate_fox", "ffn2_norm", "final_norm")
WEIGHTS = ("ffn1_norm", "ffn1_w_gate", "ffn1_w_up", "ffn1_w_down", "mix_norm", "w_in", "b_forget",
           "b_gate_dil", "b_gate_fox", "w_proj_dil", "w_proj_fox", "w_out", "ffn2_norm", "ffn2_w_gate",
           "ffn2_w_up", "ffn2_w_down", "final_norm")


def _pick(n, target, align):
    best = None
    for d in range(align, min(n, target) + 1, align):
        if n % d == 0:
            best = d
    return n if best is None else best


def _params(sem=None):
    return pltpu.CompilerParams(dimension_semantics=sem, vmem_limit_bytes=VMEM_LIMIT)


_DIMS = {"nn": (((1,), (0,)), ((), ())), "nt": (((1,), (1,)), ((), ())), "tn": (((0,), (0,)), ((), ()))}


def _mm(name, pairs, mode, m, n, k, *, epilogue, out_dtypes, extras=(), rows=(), n_colsum=0,
        sum_pairs=False, tm=1024, tn=1152, tk=2048, piece_layout=False):
    m_align = LANES if mode == "tn" else 8
    tm = _pick(m, tm, m_align)
    tn = n // N_CHIPS if piece_layout else _pick(n, tn, LANES)
    tk = _pick(k, tk, LANES)
    n_acc = 1 if sum_pairs else len(pairs)
    n_in = 2 * len(pairs) + len(extras) + len(rows)
    n_out = len(out_dtypes) + n_colsum

    def vmem_bytes(tm_, tn_, tk_):
        tiles = sum(tm_ * tk_ * a.dtype.itemsize + tn_ * tk_ * b.dtype.itemsize for a, b in pairs)
        tiles += sum(tm_ * tn_ * arr.dtype.itemsize for arr, _ in extras)
        tiles += sum(tm_ * tn_ * jnp.dtype(dt).itemsize for dt in out_dtypes)
        return 2 * tiles + (n_acc + len(extras) + len(out_dtypes)) * tm_ * tn_ * 4

    while vmem_bytes(tm, tn, tk) > MM_VMEM_BUDGET:
        if tk > 512:
            tk = _pick(k, tk - LANES, LANES)
        elif tn > 512 and not piece_layout:
            tn = _pick(n, tn - LANES, LANES)
        elif tm > 256:
            tm = _pick(m, tm - m_align, m_align)
        else:
            break
    nk = k // tk

    def body(*refs):
        ins, outs, accs = refs[:n_in], refs[n_in:n_in + n_out], refs[n_in + n_out:]
        kk = pl.program_id(2)

        @pl.when(kk == 0)
        def _():
            for acc in accs:
                acc[...] = jnp.zeros_like(acc)

        for p in range(len(pairs)):
            a = ins[2 * p][...].astype(BF)
            b = ins[2 * p + 1][...].astype(BF)
            accs[0 if sum_pairs else p][...] += lax.dot_general(a, b, _DIMS[mode], preferred_element_type=F32)

        @pl.when(kk == nk - 1)
        def _():
            ex = [r[...] for r in ins[2 * len(pairs):2 * len(pairs) + len(extras)]]
            rw = [r[...] for r in ins[2 * len(pairs) + len(extras):]]
            res = epilogue([acc[...] for acc in accs], ex, rw)
            for o, r in zip(outs, res):
                o[...] = r.astype(o.dtype)

    in_specs, args = [], []
    for a, b in pairs:
        if mode == "tn":
            in_specs.append(pl.BlockSpec((tk, tm), lambda i, j, kk: (kk, i)))
        else:
            in_specs.append(pl.BlockSpec((tm, tk), lambda i, j, kk: (i, kk)))
        if mode == "nt":
            in_specs.append(pl.BlockSpec((tn, tk), lambda i, j, kk: (j, kk)))
        else:
            in_specs.append(pl.BlockSpec((tk, tn), lambda i, j, kk: (kk, j)))
        args += [a, b]
    for arr, off in extras:
        assert off % tn == 0
        in_specs.append(pl.BlockSpec((tm, tn), functools.partial(lambda i, j, kk, o: (i, j + o), o=off // tn)))
        args.append(arr)
    for arr in rows:
        in_specs.append(pl.BlockSpec((1, tn), lambda i, j, kk: (0, j)))
        args.append(arr)
    if piece_layout:
        out_specs = [pl.BlockSpec((None, tm, tn), lambda i, j, kk: (j, i, 0)) for _ in out_dtypes]
        out_shape = [jax.ShapeDtypeStruct((n // tn, m, tn), d) for d in out_dtypes]
    else:
        out_specs = [pl.BlockSpec((tm, tn), lambda i, j, kk: (i, j)) for _ in out_dtypes]
        out_shape = [jax.ShapeDtypeStruct((m, n), d) for d in out_dtypes]
    for _ in range(n_colsum):
        out_specs.append(pl.BlockSpec((None, 1, tn), lambda i, j, kk: (i, 0, j)))
        out_shape.append(jax.ShapeDtypeStruct((m // tm, 1, n), F32))
    return pl.pallas_call(
        body, grid=(m // tm, n // tn, nk), in_specs=in_specs, out_specs=out_specs, out_shape=out_shape,
        scratch_shapes=[pltpu.VMEM((tm, tn), F32) for _ in range(n_acc)],
        compiler_params=_params(("parallel", "parallel", "arbitrary")), name=name)(*args)


def _col_pieces(full):
    rows, cols = full.shape
    return full.reshape(rows, N_CHIPS, cols // N_CHIPS).transpose(1, 0, 2)


def _dw_col_pieces(name, a, b, m, n, k):
    ident = lambda accs, ex, rw: (accs[0],)
    if (n // N_CHIPS) % LANES == 0:
        return _mm(name, [(a, b)], "tn", m, n, k, epilogue=ident, out_dtypes=(F32,), tm=512, piece_layout=True)[0]
    return _col_pieces(_mm(name, [(a, b)], "tn", m, n, k, epilogue=ident, out_dtypes=(F32,))[0])


def _sigmoid(z):
    return 1.0 / (1.0 + jnp.exp(-z))


def _row_tile(s):
    return _pick(s, 256, 8)


def _fold8(t):
    r, d = t.shape
    return jnp.sum(t.reshape(r // 8, 8, d), axis=0)


def _rms_fwd(name, x, g):
    s, d = x.shape
    tr = _row_tile(s)

    def body(x_ref, g_ref, h_ref):
        xf = x_ref[...]
        y = xf * lax.rsqrt(jnp.mean(xf * xf, axis=-1, keepdims=True) + NORM_EPS)
        h_ref[...] = (y * g_ref[...]).astype(BF)

    return pl.pallas_call(
        body, grid=(s // tr,),
        in_specs=[pl.BlockSpec((tr, d), lambda i: (i, 0)), pl.BlockSpec((1, d), lambda i: (0, 0))],
        out_specs=pl.BlockSpec((tr, d), lambda i: (i, 0)), out_shape=jax.ShapeDtypeStruct((s, d), BF),
        compiler_params=_params(("parallel",)), name=name)(x, g)


def _rms_bwd(name, x, g, dh, dres):
    s, d = x.shape
    tr = _row_tile(s)

    def body(x_ref, g_ref, dh_ref, dres_ref, dx_ref, dg_ref):
        @pl.when(pl.program_id(0) == 0)
        def _():
            dg_ref[...] = jnp.zeros_like(dg_ref)

        xf = x_ref[...]
        rstd = lax.rsqrt(jnp.mean(xf * xf, axis=-1, keepdims=True) + NORM_EPS)
        xhat = xf * rstd
        dhf = dh_ref[...]
        dg_ref[...] += _fold8(dhf * xhat)
        dxh = dhf * g_ref[...]
        dx_ref[...] = dres_ref[...] + rstd * (dxh - xhat * jnp.mean(dxh * xhat, axis=-1, keepdims=True))

    blk = pl.BlockSpec((tr, d), lambda i: (i, 0))
    return pl.pallas_call(
        body, grid=(s // tr,),
        in_specs=[blk, pl.BlockSpec((1, d), lambda i: (0, 0)), blk, blk],
        out_specs=[blk, pl.BlockSpec((8, d), lambda i: (0, 0))],
        out_shape=[jax.ShapeDtypeStruct((s, d), F32), jax.ShapeDtypeStruct((8, d), F32)],
        compiler_params=_params(("arbitrary",)), name=name)(x, g, dh, dres)


def _final(name, x, g, tgt):
    s, d = x.shape
    tr = _row_tile(s)

    def body(x_ref, g_ref, t_ref, dx_ref, dg_ref, sq_ref):
        @pl.when(pl.program_id(0) == 0)
        def _():
            dg_ref[...] = jnp.zeros_like(dg_ref)
            sq_ref[...] = jnp.zeros_like(sq_ref)

        xf = x_ref[...]
        rstd = lax.rsqrt(jnp.mean(xf * xf, axis=-1, keepdims=True) + NORM_EPS)
        xhat = xf * rstd
        gf = g_ref[...]
        err = xhat * gf - t_ref[...]
        sq_ref[...] += _fold8(err * err)
        dy = err * (1.0 / d)
        dg_ref[...] += _fold8(dy * xhat)
        dxh = dy * gf
        dx_ref[...] = rstd * (dxh - xhat * jnp.mean(dxh * xhat, axis=-1, keepdims=True))

    blk = pl.BlockSpec((tr, d), lambda i: (i, 0))
    acc = pl.BlockSpec((8, d), lambda i: (0, 0))
    return pl.pallas_call(
        body, grid=(s // tr,), in_specs=[blk, pl.BlockSpec((1, d), lambda i: (0, 0)), blk],
        out_specs=[blk, acc, acc],
        out_shape=[jax.ShapeDtypeStruct((s, d), F32), jax.ShapeDtypeStruct((8, d), F32),
                   jax.ShapeDtypeStruct((8, d), F32)],
        compiler_params=_params(("arbitrary",)), name=name)(x, g, tgt)


def _ffn_fwd(tag, x, g, w_gate, w_up, w_down):
    s, d = x.shape
    f = w_gate.shape[1]
    h = _rms_fwd(tag + "_rms", x, g)

    def up_epi(accs, ex, rw):
        a, b = accs
        return a, b, a * _sigmoid(a) * b

    a, b, act = _mm(tag + "_up", [(h, w_gate), (h, w_up)], "nn", s, f, d, epilogue=up_epi,
                    out_dtypes=(F32, F32, BF))

    def down_epi(accs, ex, rw):
        return (ex[0] + 0.5 * accs[0],)

    (y,) = _mm(tag + "_down", [(act, w_down)], "nn", s, d, f, epilogue=down_epi, out_dtypes=(F32,),
               extras=[(x, 0)])
    return y, (h, a, b, act)


def _ffn_bwd(tag, x, g, w_gate, w_up, w_down, saved, dy):
    s, d = x.shape
    f = w_gate.shape[1]
    h, a, b, act = saved

    def act_epi(accs, ex, rw):
        dact = 0.5 * accs[0]
        av, bv = ex
        sg = _sigmoid(av)
        return dact * bv * (sg * (1.0 + av * (1.0 - sg))), dact * (av * sg)

    da, db = _mm(tag + "_dact", [(dy, w_down)], "nt", s, f, d, epilogue=act_epi, out_dtypes=(BF, BF),
                 extras=[(a, 0), (b, 0)])
    ident = lambda accs, ex, rw: (accs[0],)
    (dw_down,) = _mm(tag + "_dwd", [(act, dy)], "tn", f, d, s, epilogue=lambda accs, ex, rw: (0.5 * accs[0],),
                     out_dtypes=(F32,))
    dw_down = dw_down.reshape(N_CHIPS, f // N_CHIPS, d)
    dw_gate = _dw_col_pieces(tag + "_dwg", h, da, d, f, s)
    dw_up = _dw_col_pieces(tag + "_dwu", h, db, d, f, s)
    (dh,) = _mm(tag + "_dh", [(da, w_gate), (db, w_up)], "nt", s, d, f, epilogue=ident, out_dtypes=(F32,),
                sum_pairs=True)
    dx, dg = _rms_bwd(tag + "_rmsb", x, g, dh, dy)
    return dx, dg, dw_gate, dw_up, dw_down


def _rope_tables(s):
    half = ROPE_DIM // 2
    pos = jnp.arange(s, dtype=F32)
    inv_freq = ROPE_THETA ** (-jnp.arange(0, ROPE_DIM, 2, dtype=F32) / ROPE_DIM)
    ang = pos[:, None] * inv_freq[None, :]
    cos, sin = jnp.cos(ang), jnp.sin(ang)
    rest = HEAD_DIM - ROPE_DIM
    cos_t = jnp.concatenate([cos, cos, jnp.ones((s, rest), F32)], axis=-1)
    sin_t = jnp.concatenate([-sin, sin, jnp.zeros((s, rest), F32)], axis=-1)
    return cos_t, sin_t


def _swap_halves(t):
    lane = lax.broadcasted_iota(jnp.int32, t.shape, 1)
    half = ROPE_DIM // 2
    return jnp.where(lane < half, pltpu.roll(t, HEAD_DIM - half, 1), pltpu.roll(t, half, 1))


def _dil_bias(blk):
    n_delta = MAX_WINDOW // blk + 1
    delta = jnp.arange(n_delta, dtype=jnp.int32)[:, None, None]
    r = jnp.arange(blk, dtype=jnp.int32)[None, :, None]
    c = jnp.arange(blk, dtype=jnp.int32)[None, None, :]
    o = delta * blk + r - c
    mult = jnp.zeros(o.shape, F32)
    for w, dd in DIL_PATTERNS:
        mult = mult + ((o >= 0) & (o <= w) & (o % dd == 0)).astype(F32)
    tab = jnp.where(mult > 0, jnp.log(jnp.maximum(mult, 1.0)), NEG)
    return tab, tab.transpose(0, 2, 1)


def _att_block(s):
    return _pick(s, ATT_BLOCK, LANES)


def _flash_fwd(name, qkv, q_off, k_off, v_off, n_heads, *, fox, tab=None, c_col=None, c_row=None):
    s = qkv.shape[0]
    blk = _att_block(s)
    nq = s // blk
    n_delta = MAX_WINDOW // blk + 1

    def body(*refs):
        if fox:
            q_ref, k_ref, v_ref, cc_ref, cr_ref, o_ref, lse_ref, acc, m_s, l_s = refs
        else:
            q_ref, k_ref, v_ref, tab_ref, o_ref, lse_ref, acc, m_s, l_s = refs
        i = pl.program_id(1)
        acc[...] = jnp.zeros_like(acc)
        m_s[...] = jnp.full_like(m_s, NEG)
        l_s[...] = jnp.zeros_like(l_s)
        q = q_ref[...]

        def step(j, carry):
            ks = pl.ds(pl.multiple_of(j * blk, blk), blk)
            sc = lax.dot_general(q, k_ref[ks, :], _DIMS["nt"], preferred_element_type=F32)
            if fox:
                sc = sc + cc_ref[...] - cr_ref[j]
                row = lax.broadcasted_iota(jnp.int32, sc.shape, 0) + i * blk
                col = lax.broadcasted_iota(jnp.int32, sc.shape, 1) + j * blk
                sc = jnp.where(col <= row, sc, NEG)
            else:
                sc = sc + tab_ref[i - j]
            m_prev = m_s[...]
            m_new = jnp.maximum(m_prev, jnp.max(sc, axis=-1, keepdims=True))
            alpha = jnp.exp(m_prev - m_new)
            p = jnp.exp(sc - m_new)
            l_s[...] = alpha * l_s[...] + jnp.sum(p, axis=-1, keepdims=True)
            acc[...] = alpha * acc[...] + jnp.dot(p.astype(BF), v_ref[ks, :], preferred_element_type=F32)
            m_s[...] = m_new
            return carry

        lo = 0 if fox else jnp.maximum(i - (n_delta - 1), 0)
        lax.fori_loop(lo, i + 1, step, 0)
        o_ref[...] = (acc[...] / l_s[...]).astype(o_ref.dtype)
        lse_ref[...] = m_s[...] + jnp.log(l_s[...])

    off = lambda o: functools.partial(lambda h, i, o: (0, o + h), o=o)
    in_specs = [pl.BlockSpec((blk, HEAD_DIM), functools.partial(lambda h, i, o: (i, o + h), o=q_off)),
                pl.BlockSpec((s, HEAD_DIM), off(k_off)), pl.BlockSpec((s, HEAD_DIM), off(v_off))]
    args = [qkv, qkv, qkv]
    if fox:
        in_specs += [pl.BlockSpec((None, blk, 1), lambda h, i: (h, i, 0)),
                     pl.BlockSpec((None, nq, 1, blk), lambda h, i: (h, 0, 0, 0))]
        args += [c_col, c_row]
    else:
        in_specs.append(pl.BlockSpec((n_delta, blk, blk), lambda h, i: (0, 0, 0)))
        args.append(tab)
    return pl.pallas_call(
        body, grid=(n_heads, nq), in_specs=in_specs,
        out_specs=[pl.BlockSpec((blk, HEAD_DIM), lambda h, i: (i, h)),
                   pl.BlockSpec((None, blk, 1), lambda h, i: (h, i, 0))],
        out_shape=[jax.ShapeDtypeStruct((s, n_heads * HEAD_DIM), BF), jax.ShapeDtypeStruct((n_heads, s, 1), F32)],
        scratch_shapes=[pltpu.VMEM((blk, HEAD_DIM), F32), pltpu.VMEM((blk, 1), F32), pltpu.VMEM((blk, 1), F32)],
        compiler_params=_params(("parallel", "parallel")), name=name)(*args)


def _att_delta(name, do, o, n_heads):
    s = do.shape[0]
    blk = _att_block(s)

    def body(do_ref, o_ref, d_ref):
        d_ref[...] = jnp.sum(do_ref[...].astype(F32) * o_ref[...].astype(F32), axis=-1, keepdims=True)

    spec = pl.BlockSpec((blk, HEAD_DIM), lambda h, i: (i, h))
    return pl.pallas_call(
        body, grid=(n_heads, s // blk), in_specs=[spec, spec],
        out_specs=pl.BlockSpec((None, blk, 1), lambda h, i: (h, i, 0)),
        out_shape=jax.ShapeDtypeStruct((n_heads, s, 1), F32),
        compiler_params=_params(("parallel", "parallel")), name=name)(do, o)


def _flash_bwd(name, qkv, q_off, k_off, v_off, do, lse_row, delta_row, n_heads, *, fox, tab_t=None,
               c_col=None, c_row=None):
    s = qkv.shape[0]
    blk = _att_block(s)
    nq = s // blk
    n_delta = MAX_WINDOW // blk + 1

    def body(*refs):
        if fox:
            (q_ref, do_ref, k_ref, v_ref, lse_ref, dl_ref, cr_ref, cc_ref,
             dq_ref, dk_ref, dv_ref, dc_ref, dcq_ref, dk_acc, dv_acc, dc_acc) = refs
        else:
            (q_ref, do_ref, k_ref, v_ref, lse_ref, dl_ref, tab_ref,
             dq_ref, dk_ref, dv_ref, dk_acc, dv_acc) = refs
        j = pl.program_id(1)

        @pl.when(j == 0)
        def _():
            dq_ref[...] = jnp.zeros_like(dq_ref)
            if fox:
                dcq_ref[...] = jnp.zeros_like(dcq_ref)

        dk_acc[...] = jnp.zeros_like(dk_acc)
        dv_acc[...] = jnp.zeros_like(dv_acc)
        if fox:
            dc_acc[...] = jnp.zeros_like(dc_acc)
        kb = k_ref[...]
        vb = v_ref[...]

        def step(i, carry):
            qs = pl.ds(pl.multiple_of(i * blk, blk), blk)
            qb = q_ref[qs, :]
            dob = do_ref[qs, :]
            st = lax.dot_general(kb, qb, _DIMS["nt"], preferred_element_type=F32)
            if fox:
                st = st + cr_ref[i] - cc_ref[...]
                kpos = lax.broadcasted_iota(jnp.int32, st.shape, 0) + j * blk
                qpos = lax.broadcasted_iota(jnp.int32, st.shape, 1) + i * blk
                st = jnp.where(kpos <= qpos, st, NEG)
            else:
                st = st + tab_ref[i - j]
            pt = jnp.exp(st - lse_ref[i])
            dv_acc[...] += jnp.dot(pt.astype(BF), dob, preferred_element_type=F32)
            dpt = lax.dot_general(vb, dob, _DIMS["nt"], preferred_element_type=F32)
            dst = pt * (dpt - dl_ref[i])
            dsb = dst.astype(BF)
            dk_acc[...] += jnp.dot(dsb, qb, preferred_element_type=F32)
            dq_ref[qs, :] += lax.dot_general(dsb, kb, _DIMS["tn"], preferred_element_type=F32)
            if fox:
                dc_acc[...] -= jnp.sum(dst, axis=-1, keepdims=True)
                dcq_ref[i] += jnp.sum(dst, axis=0, keepdims=True)
            return carry

        hi = nq if fox else jnp.minimum(nq, j + n_delta)
        lax.fori_loop(j, hi, step, 0)
        dk_ref[...] = dk_acc[...]
        dv_ref[...] = dv_acc[...].astype(dv_ref.dtype)
        if fox:
            dc_ref[...] = dc_acc[...]

    full = lambda o: pl.BlockSpec((s, HEAD_DIM), functools.partial(lambda h, j, o: (0, o + h), o=o))
    tile = lambda o: pl.BlockSpec((blk, HEAD_DIM), functools.partial(lambda h, j, o: (j, o + h), o=o))
    per_q = pl.BlockSpec((None, nq, 1, blk), lambda h, j: (h, 0, 0, 0))
    per_k = pl.BlockSpec((None, blk, 1), lambda h, j: (h, j, 0))
    in_specs = [full(q_off), full(0), tile(k_off), tile(v_off), per_q, per_q]
    args = [qkv, do, qkv, qkv, lse_row, delta_row]
    out_specs = [full(0), tile(0), tile(0)]
    hd = n_heads * HEAD_DIM
    out_shape = [jax.ShapeDtypeStruct((s, hd), F32), jax.ShapeDtypeStruct((s, hd), F32),
                 jax.ShapeDtypeStruct((s, hd), BF)]
    scratch = [pltpu.VMEM((blk, HEAD_DIM), F32), pltpu.VMEM((blk, HEAD_DIM), F32)]
    if fox:
        in_specs += [per_q, per_k]
        args += [c_row, c_col]
        out_specs += [per_k, per_q]
        out_shape += [jax.ShapeDtypeStruct((n_heads, s, 1), F32), jax.ShapeDtypeStruct((n_heads, nq, 1, blk), F32)]
        scratch.append(pltpu.VMEM((blk, 1), F32))
    else:
        in_specs.append(pl.BlockSpec((n_delta, blk, blk), lambda h, j: (0, 0, 0)))
        args.append(tab_t)
    return pl.pallas_call(
        body, grid=(n_heads, nq), in_specs=in_specs, out_specs=out_specs, out_shape=out_shape,
        scratch_shapes=scratch, compiler_params=_params(("parallel", "arbitrary")), name=name)(*args)


def _split3(t):
    hi = t.astype(BF)
    r1 = t - hi.astype(F32)
    mid = r1.astype(BF)
    lo = (r1 - mid.astype(F32)).astype(BF)
    return hi, mid, lo


def _tri_dot(tri, t):
    hi, mid, lo = _split3(t)
    return (jnp.dot(tri, hi, preferred_element_type=F32) + jnp.dot(tri, mid, preferred_element_type=F32)
            + jnp.dot(tri, lo, preferred_element_type=F32))


def _log_sigmoid(z):
    return jnp.minimum(z, 0.0) - jnp.log(1.0 + jnp.exp(-jnp.abs(z)))


def _forget_cumsum(name, proj, f_col, bias):
    s = proj.shape[0]
    blk = _att_block(s)

    def body(f_ref, b_ref, c_ref, carry):
        @pl.when(pl.program_id(0) == 0)
        def _():
            carry[...] = jnp.zeros_like(carry)

        lf = _log_sigmoid(f_ref[...] + b_ref[...])
        r = lax.broadcasted_iota(jnp.int32, (blk, blk), 0)
        c = lax.broadcasted_iota(jnp.int32, (blk, blk), 1)
        tri = (c <= r).astype(BF)
        c_ref[...] = _tri_dot(tri, lf) + carry[...]
        carry[...] = c_ref[pl.ds(blk - 1, 1), :]

    return pl.pallas_call(
        body, grid=(s // blk,),
        in_specs=[pl.BlockSpec((blk, LANES), lambda i: (i, f_col)), pl.BlockSpec((1, LANES), lambda i: (0, 0))],
        out_specs=pl.BlockSpec((blk, LANES), lambda i: (i, 0)), out_shape=jax.ShapeDtypeStruct((s, LANES), F32),
        scratch_shapes=[pltpu.VMEM((1, LANES), F32)],
        compiler_params=_params(("arbitrary",)), name=name)(proj, bias)


def _forget_bwd(name, proj, f_col, bias, dc):
    s = proj.shape[0]
    blk = _att_block(s)
    nb = s // blk

    def body(f_ref, b_ref, dc_ref, df_ref, db_ref, carry):
        @pl.when(pl.program_id(0) == 0)
        def _():
            carry[...] = jnp.zeros_like(carry)
            db_ref[...] = jnp.zeros_like(db_ref)

        r = lax.broadcasted_iota(jnp.int32, (blk, blk), 0)
        c = lax.broadcasted_iota(jnp.int32, (blk, blk), 1)
        tri = (c >= r).astype(BF)
        r = lax.broadcasted_iota(jnp.int32, (blk, LANES), 0)
        dlf = _tri_dot(tri, dc_ref[...]) + carry[...]
        carry[...] = jnp.sum(jnp.where(r == 0, dlf, 0.0), axis=0, keepdims=True)
        dz = dlf * _sigmoid(-(f_ref[...] + b_ref[...]))
        df_ref[...] = dz.astype(BF)
        db_ref[...] += _fold8(dz)

    rev = lambda i: (nb - 1 - i, 0)
    return pl.pallas_call(
        body, grid=(nb,),
        in_specs=[pl.BlockSpec((blk, LANES), lambda i: (nb - 1 - i, f_col)), pl.BlockSpec((1, LANES), lambda i: (0, 0)),
                  pl.BlockSpec((blk, LANES), rev)],
        out_specs=[pl.BlockSpec((blk, LANES), rev), pl.BlockSpec((8, LANES), lambda i: (0, 0))],
        out_shape=[jax.ShapeDtypeStruct((s, LANES), BF), jax.ShapeDtypeStruct((8, LANES), F32)],
        scratch_shapes=[pltpu.VMEM((1, LANES), F32)],
        compiler_params=_params(("arbitrary",)), name=name)(proj, bias, dc)


def _head_prep(name, proj, cos_t, sin_t, n_heads):
    s = proj.shape[0]
    tr = _pick(s, 1024, 16)
    scale = HEAD_DIM ** -0.5
    hh = n_heads

    def body(p_ref, cos_ref, sin_ref, o_ref):
        j = pl.program_id(1)
        t = p_ref[...]

        @pl.when(j < 2 * hh)
        def _():
            r = t * cos_ref[...] + _swap_halves(t) * sin_ref[...]
            o_ref[...] = jnp.where(j < hh, r * scale, r).astype(BF)

        @pl.when(j >= 2 * hh)
        def _():
            is_q = jnp.logical_and(j >= 3 * hh, j < 4 * hh)
            o_ref[...] = jnp.where(is_q, t * scale, t).astype(BF)

    tab = pl.BlockSpec((tr, HEAD_DIM), lambda i, j: (i, 0))
    blk = pl.BlockSpec((tr, HEAD_DIM), lambda i, j: (i, j))
    return pl.pallas_call(
        body, grid=(s // tr, 6 * hh), in_specs=[blk, tab, tab], out_specs=blk,
        out_shape=jax.ShapeDtypeStruct((s, 6 * hh * HEAD_DIM), BF),
        compiler_params=_params(("parallel", "arbitrary")), name=name)(proj, cos_t, sin_t)


def _dproj_assemble(name, parts, dgate, df, cos_t, sin_t, n_heads, d_model):
    s = cos_t.shape[0]
    tr = _pick(s, 1024, 16)
    scale = HEAD_DIM ** -0.5
    hh = n_heads
    n_head_blocks = 6 * hh
    n_gate_blocks = 2 * d_model // HEAD_DIM
    n_blocks = n_head_blocks + n_gate_blocks + 1

    def body(*refs):
        p_refs, g_ref, f_ref, cos_ref, sin_ref, o_ref = refs[:6], refs[6], refs[7], refs[8], refs[9], refs[10]
        j = pl.program_id(1)
        for kind in range(6):
            @pl.when(jnp.logical_and(j >= kind * hh, j < (kind + 1) * hh))
            def _(kind=kind):
                t = p_refs[kind][...].astype(F32)
                if kind in (0, 3):
                    t = t * scale
                if kind in (0, 1):
                    t = t * cos_ref[...] - _swap_halves(t) * sin_ref[...]
                o_ref[...] = t.astype(BF)

        @pl.when(jnp.logical_and(j >= n_head_blocks, j < n_head_blocks + n_gate_blocks))
        def _():
            o_ref[...] = g_ref[...]

        @pl.when(j == n_blocks - 1)
        def _():
            o_ref[...] = f_ref[...]

    def head_spec(kind):
        return pl.BlockSpec((tr, HEAD_DIM), functools.partial(
            lambda i, j, kind: (i, jnp.clip(j - kind * hh, 0, hh - 1)), kind=kind))

    in_specs = [head_spec(kind) for kind in range(6)]
    in_specs.append(pl.BlockSpec((tr, HEAD_DIM), lambda i, j: (i, jnp.clip(j - n_head_blocks, 0, n_gate_blocks - 1))))
    tab = pl.BlockSpec((tr, HEAD_DIM), lambda i, j: (i, 0))
    in_specs += [tab, tab, tab]
    return pl.pallas_call(
        body, grid=(s // tr, n_blocks), in_specs=in_specs,
        out_specs=pl.BlockSpec((tr, HEAD_DIM), lambda i, j: (i, j)),
        out_shape=jax.ShapeDtypeStruct((s, n_blocks * HEAD_DIM), BF),
        compiler_params=_params(("parallel", "arbitrary")), name=name)(*parts, dgate, df, cos_t, sin_t)


def _per_head(c_pad, n_heads, blk):
    s = c_pad.shape[0]
    t = c_pad[:, :n_heads].T
    return t.reshape(n_heads, s, 1), t.reshape(n_heads, s // blk, 1, blk)


def _local_step(x, tgt, w, small):
    s, d = x.shape
    hh = w["w_proj_dil"].shape[0] // HEAD_DIM
    hd = hh * HEAD_DIM
    blk = _att_block(s)
    gate_off = 6 * hd
    f_col = (gate_off + 2 * d) // LANES
    n_proj = gate_off + 2 * d + LANES
    ident = lambda accs, ex, rw: (accs[0],)

    x1, saved1 = _ffn_fwd("ffn1", x, small["ffn1_norm"], w["ffn1_w_gate"], w["ffn1_w_up"], w["ffn1_w_down"])

    hm = _rms_fwd("mix_rms", x1, small["mix_norm"])
    (proj,) = _mm("proj", [(hm, w["w_in"])], "nn", s, n_proj, d, epilogue=ident, out_dtypes=(F32,))
    cos_t, sin_t = _rope_tables(s)
    qkv = _head_prep("head_prep", proj, cos_t, sin_t, hh)
    tab, tab_t = _dil_bias(blk)
    y_dil, lse_d = _flash_fwd("dil_fwd", qkv, 0, hh, 2 * hh, hh, fox=False, tab=tab)
    bias_f = jnp.pad(small["b_forget"], ((0, 0), (0, LANES - hh)))
    c_pad = _forget_cumsum("forget_cumsum", proj, f_col, bias_f)
    c_col, c_row = _per_head(c_pad, hh, blk)
    y_fox, lse_f = _flash_fwd("fox_fwd", qkv, 3 * hh, 4 * hh, 5 * hh, hh, fox=True, c_col=c_col, c_row=c_row)

    def merge_epi(accs, ex, rw):
        ud, uf = accs
        return ud, uf, _sigmoid(ex[0] + rw[0]) * ud + _sigmoid(ex[1] + rw[1]) * uf

    u_d, u_f, merged = _mm("merge", [(y_dil, w["w_proj_dil"]), (y_fox, w["w_proj_fox"])], "nn", s, d, hd,
                           epilogue=merge_epi, out_dtypes=(F32, F32, BF),
                           extras=[(proj, gate_off), (proj, gate_off + d)],
                           rows=[small["b_gate_dil"], small["b_gate_fox"]])
    (x2,) = _mm("mix_out", [(merged, w["w_out"])], "nn", s, d, d,
                epilogue=lambda accs, ex, rw: (ex[0] + accs[0],), out_dtypes=(F32,), extras=[(x1, 0)])

    x3, saved2 = _ffn_fwd("ffn2", x2, small["ffn2_norm"], w["ffn2_w_gate"], w["ffn2_w_up"], w["ffn2_w_down"])
    dx3, dg_final, sq = _final("final", x3, small["final_norm"].reshape(1, d), tgt)

    dx2, dg_ffn2, dw_g2, dw_u2, dw_d2 = _ffn_bwd("ffn2", x2, small["ffn2_norm"], w["ffn2_w_gate"], w["ffn2_w_up"],
                                                 w["ffn2_w_down"], saved2, dx3)

    def dmerge_epi(accs, ex, rw):
        dm = accs[0]
        gd, gf, ud, uf = ex
        sd, sf = _sigmoid(gd + rw[0]), _sigmoid(gf + rw[1])
        dgd = dm * ud * (sd * (1.0 - sd))
        dgf = dm * uf * (sf * (1.0 - sf))
        return (dm * sd, dm * sf, dgd, dgf, jnp.sum(dgd, axis=0, keepdims=True), jnp.sum(dgf, axis=0, keepdims=True))

    du_d, du_f, dg_d, dg_f, dbg_d, dbg_f = _mm(
        "dmerge", [(dx2, w["w_out"])], "nt", s, d, d, epilogue=dmerge_epi, out_dtypes=(BF, BF, BF, BF), n_colsum=2,
        extras=[(proj, gate_off), (proj, gate_off + d), (u_d, 0), (u_f, 0)],
        rows=[small["b_gate_dil"], small["b_gate_fox"]])
    (dw_out,) = _mm("dw_out", [(merged, dx2)], "tn", d, d, s, epilogue=ident, out_dtypes=(F32,))
    dw_out = dw_out.reshape(N_CHIPS, d // N_CHIPS, d)
    dw_pd = _dw_col_pieces("dw_pd", y_dil, du_d, hd, d, s)
    dw_pf = _dw_col_pieces("dw_pf", y_fox, du_f, hd, d, s)
    (dy_dil,) = _mm("dy_dil", [(du_d, w["w_proj_dil"])], "nt", s, hd, d, epilogue=ident, out_dtypes=(BF,))
    (dy_fox,) = _mm("dy_fox", [(du_f, w["w_proj_fox"])], "nt", s, hd, d, epilogue=ident, out_dtypes=(BF,))

    row = lambda t: t.reshape(hh, s // blk, 1, blk)
    delta_d = _att_delta("dil_delta", dy_dil, y_dil, hh)
    dq_d, dk_d, dv_d = _flash_bwd("dil_bwd", qkv, 0, hh, 2 * hh, dy_dil, row(lse_d), row(delta_d), hh, fox=False,
                                  tab_t=tab_t)
    delta_f = _att_delta("fox_delta", dy_fox, y_fox, hh)
    dq_f, dk_f, dv_f, dc_k, dc_q = _flash_bwd("fox_bwd", qkv, 3 * hh, 4 * hh, 5 * hh, dy_fox, row(lse_f), row(delta_f), hh,
                                      fox=True, c_col=c_col, c_row=c_row)
    dc = dc_k.reshape(hh, s) + dc_q.reshape(hh, s)
    dc_pad = jnp.pad(dc.T, ((0, 0), (0, LANES - hh)))
    df, db_forget = _forget_bwd("forget_bwd", proj, f_col, bias_f, dc_pad)
    dgate = jnp.concatenate([dg_d, dg_f], axis=1)
    dproj = _dproj_assemble("dproj", [dq_d, dk_d, dv_d, dq_f, dk_f, dv_f], dgate, df, cos_t, sin_t, hh, d)
    (dhm,) = _mm("dhm", [(dproj, w["w_in"])], "nt", s, d, n_proj, epilogue=ident, out_dtypes=(F32,))
    (dw_in,) = _mm("dw_in", [(hm, dproj)], "tn", d, n_proj, s, epilogue=ident, out_dtypes=(F32,))
    dw_in = _col_pieces(_unpack_dw_in(dw_in, hd, d))
    dx1, dg_mix = _rms_bwd("mix_rmsb", x1, small["mix_norm"], dhm, dx2)

    dx0, dg_ffn1, dw_g1, dw_u1, dw_d1 = _ffn_bwd("ffn1", x, small["ffn1_norm"], w["ffn1_w_gate"], w["ffn1_w_up"],
                                                 w["ffn1_w_down"], saved1, dx1)

    grads = {"ffn1_w_gate": dw_g1, "ffn1_w_up": dw_u1, "ffn1_w_down": dw_d1, "w_in": dw_in, "w_proj_dil": dw_pd,
             "w_proj_fox": dw_pf, "w_out": dw_out, "ffn2_w_gate": dw_g2, "ffn2_w_up": dw_u2, "ffn2_w_down": dw_d2}
    partials = {"ffn1_norm": dg_ffn1, "mix_norm": dg_mix, "ffn2_norm": dg_ffn2, "final_norm": dg_final,
                "b_gate_dil": dbg_d.reshape(-1, d), "b_gate_fox": dbg_f.reshape(-1, d), "b_forget": db_forget, "sq": sq}
    return dx0, grads, partials


def _coords():
    return lax.axis_index("x"), lax.axis_index("y"), lax.axis_index("c")


def _other_chips(x, y):
    return [(1 - x, y), (x, 1 - y), (1 - x, 1 - y)]


ANY_SPEC = pl.BlockSpec(memory_space=pl.ANY)


def _gather_weights(name, shards):
    nw = len(shards)

    def body(*refs):
        srcs, outs = refs[:nw], refs[nw:2 * nw]
        send_sems, recv_sems, local_sems = refs[2 * nw:]
        x, y, c = _coords()
        sibling = (x, y, 1 - c)
        chips = _other_chips(x, y)

        def slot(w, px, py, pc):
            half = shards[w].shape[0] // 2
            return outs[w].at[2 * px + py, pl.ds(pc * half, half), :]

        def copy(w, k, src_ref, dst_ref, to):
            return pltpu.make_async_remote_copy(src_ref=src_ref, dst_ref=dst_ref, send_sem=send_sems.at[6 * w + k],
                                                recv_sem=recv_sems.at[6 * w + k], device_id=to, device_id_type=MESH)

        mine, first, passed = [], [], []
        for w in range(nw):
            half = shards[w].shape[0] // 2
            mine.append(pltpu.make_async_copy(srcs[w], outs[w].at[2 * x + y], local_sems.at[w]))
            mine[-1].start()
            for j, chip in enumerate(chips):
                first.append(copy(w, j, srcs[w].at[pl.ds(c * half, half), :], slot(w, x, y, c), (*chip, c)))
                first[-1].start()
        for w in range(nw):
            for j, chip in enumerate(chips):
                copy(w, j, slot(w, *chip, c), slot(w, *chip, c), (*chip, c)).wait_recv()
                passed.append(copy(w, 3 + j, slot(w, *chip, c), slot(w, *chip, c), sibling))
                passed[-1].start()
        for w in range(nw):
            for j, chip in enumerate(chips):
                copy(w, 3 + j, slot(w, *chip, 1 - c), slot(w, *chip, 1 - c), sibling).wait_recv()
        for cp in first + passed:
            cp.wait_send()
        for cp in mine:
            cp.wait()

    return pl.pallas_call(
        body, in_specs=[ANY_SPEC] * nw, out_specs=[ANY_SPEC] * nw,
        out_shape=[jax.ShapeDtypeStruct((N_CHIPS, *t.shape), t.dtype) for t in shards],
        scratch_shapes=[pltpu.SemaphoreType.DMA((6 * nw,)), pltpu.SemaphoreType.DMA((6 * nw,)),
                        pltpu.SemaphoreType.DMA((nw,))],
        name=name)(*shards)


def _rs_swap(name, grads):
    nw = len(grads)

    def body(*refs):
        srcs, outs = refs[:nw], refs[nw:2 * nw]
        send_sems, recv_sems = refs[2 * nw:]
        x, y, c = _coords()
        copies = []
        for w in range(nw):
            half = grads[w].shape[1] // 2
            for p in range(N_CHIPS):
                k = N_CHIPS * w + p
                copies.append(pltpu.make_async_remote_copy(
                    src_ref=srcs[w].at[p, pl.ds((1 - c) * half, half), :], dst_ref=outs[w].at[p],
                    send_sem=send_sems.at[k], recv_sem=recv_sems.at[k], device_id=(x, y, 1 - c), device_id_type=MESH))
                copies[-1].start()
        for cp in copies:
            cp.wait()

    return pl.pallas_call(
        body, in_specs=[ANY_SPEC] * nw, out_specs=[ANY_SPEC] * nw,
        out_shape=[jax.ShapeDtypeStruct((N_CHIPS, t.shape[1] // 2, t.shape[2]), t.dtype) for t in grads],
        scratch_shapes=[pltpu.SemaphoreType.DMA((N_CHIPS * nw,)), pltpu.SemaphoreType.DMA((N_CHIPS * nw,))],
        name=name)(*grads)


def _rs_add(name, ids, g, other):
    n, rows, cols = g.shape
    half = rows // 2
    tr = _pick(half, 256, 16)
    nb = half // tr

    def body(ids_ref, g_ref, o_ref, out_ref):
        out_ref[...] = (g_ref[...] + o_ref[...]).astype(BF)

    grid_spec = pltpu.PrefetchScalarGridSpec(
        num_scalar_prefetch=1, grid=(n, nb),
        in_specs=[pl.BlockSpec((None, tr, cols), lambda p, i, ids_ref: (p, ids_ref[1] * nb + i, 0)),
                  pl.BlockSpec((None, tr, cols), lambda p, i, ids_ref: (p, i, 0))],
        out_specs=pl.BlockSpec((None, tr, cols), lambda p, i, ids_ref: (p, i, 0)))
    return pl.pallas_call(body, grid_spec=grid_spec, out_shape=jax.ShapeDtypeStruct((n, half, cols), BF),
                          compiler_params=_params(("parallel", "parallel")), name=name)(ids, g, other)


def _rs_scatter(name, sums):
    nw = len(sums)

    def body(*refs):
        srcs, outs = refs[:nw], refs[nw:2 * nw]
        send_sems, recv_sems = refs[2 * nw:]
        x, y, c = _coords()
        chips = _other_chips(x, y)
        sends = []
        for w in range(nw):
            for k, (px, py) in enumerate(chips):
                sends.append(pltpu.make_async_remote_copy(
                    src_ref=srcs[w].at[2 * px + py], dst_ref=outs[w].at[k], send_sem=send_sems.at[3 * w + k],
                    recv_sem=recv_sems.at[3 * w + k], device_id=(px, py, c), device_id_type=MESH))
                sends[-1].start()
        for w in range(nw):
            for k, (px, py) in enumerate(chips):
                pltpu.make_async_remote_copy(
                    src_ref=srcs[w].at[2 * px + py], dst_ref=outs[w].at[k], send_sem=send_sems.at[3 * w + k],
                    recv_sem=recv_sems.at[3 * w + k], device_id=(px, py, c), device_id_type=MESH).wait_recv()
        for cp in sends:
            cp.wait_send()

    return pl.pallas_call(
        body, in_specs=[ANY_SPEC] * nw, out_specs=[ANY_SPEC] * nw,
        out_shape=[jax.ShapeDtypeStruct((3, *t.shape[1:]), t.dtype) for t in sums],
        scratch_shapes=[pltpu.SemaphoreType.DMA((3 * nw,)), pltpu.SemaphoreType.DMA((3 * nw,))],
        name=name)(*sums)


def _rs_sum(name, ids, own, got):
    n, half, cols = own.shape
    tr = _pick(half, 256, 16)
    nb = half // tr

    def body(ids_ref, own_ref, got_ref, out_ref):
        t = own_ref[...].astype(F32)
        for k in range(3):
            t = t + got_ref[k].astype(F32)
        out_ref[...] = t

    grid_spec = pltpu.PrefetchScalarGridSpec(
        num_scalar_prefetch=1, grid=(nb,),
        in_specs=[pl.BlockSpec((None, tr, cols), lambda i, ids_ref: (ids_ref[0], i, 0)),
                  pl.BlockSpec((3, tr, cols), lambda i, ids_ref: (0, i, 0))],
        out_specs=pl.BlockSpec((tr, cols), lambda i, ids_ref: (ids_ref[1] * nb + i, 0)))
    return pl.pallas_call(body, grid_spec=grid_spec, out_shape=jax.ShapeDtypeStruct((2 * half, cols), F32),
                          compiler_params=_params(("parallel",)), name=name)(ids, own, got)


def _rs_join(name, totals):
    nw = len(totals)

    def body(*refs):
        bufs = refs[nw:2 * nw]
        send_sems, recv_sems = refs[2 * nw:]
        x, y, c = _coords()
        copies = []
        for w in range(nw):
            half = totals[w].shape[0] // 2
            copies.append(pltpu.make_async_remote_copy(
                src_ref=bufs[w].at[pl.ds(c * half, half), :], dst_ref=bufs[w].at[pl.ds(c * half, half), :],
                send_sem=send_sems.at[w], recv_sem=recv_sems.at[w], device_id=(x, y, 1 - c), device_id_type=MESH))
            copies[-1].start()
        for w in range(nw):
            half = totals[w].shape[0] // 2
            pltpu.make_async_remote_copy(
                src_ref=bufs[w].at[pl.ds(c * half, half), :], dst_ref=bufs[w].at[pl.ds((1 - c) * half, half), :],
                send_sem=send_sems.at[w], recv_sem=recv_sems.at[w], device_id=(x, y, 1 - c),
                device_id_type=MESH).wait_recv()
        for cp in copies:
            cp.wait_send()

    return pl.pallas_call(
        body, in_specs=[ANY_SPEC] * nw, out_specs=[ANY_SPEC] * nw,
        out_shape=[jax.ShapeDtypeStruct(t.shape, t.dtype) for t in totals],
        input_output_aliases={w: w for w in range(nw)},
        scratch_shapes=[pltpu.SemaphoreType.DMA((nw,)), pltpu.SemaphoreType.DMA((nw,))],
        name=name)(*totals)


def _gather_all(name, t):
    rows, cols = t.shape

    def body(src, out, send_sems, recv_sems, local_sem):
        x, y, c = _coords()
        me = 4 * x + 2 * y + c
        mine = pltpu.make_async_copy(src, out.at[me], local_sem)
        mine.start()
        peers = [(x ^ (k >> 2 & 1), y ^ (k >> 1 & 1), c ^ (k & 1)) for k in range(1, N_DEV)]
        sends = [pltpu.make_async_remote_copy(src_ref=src, dst_ref=out.at[me], send_sem=send_sems.at[k],
                                              recv_sem=recv_sems.at[k], device_id=peer, device_id_type=MESH)
                 for k, peer in enumerate(peers)]
        for cp in sends:
            cp.start()
        for k, (px, py, pc) in enumerate(peers):
            pltpu.make_async_remote_copy(src_ref=src, dst_ref=out.at[4 * px + 2 * py + pc], send_sem=send_sems.at[k],
                                         recv_sem=recv_sems.at[k], device_id=(px, py, pc),
                                         device_id_type=MESH).wait_recv()
        for cp in sends:
            cp.wait_send()
        mine.wait()

    vmem = pl.BlockSpec(memory_space=pltpu.VMEM)
    return pl.pallas_call(
        body, in_specs=[vmem], out_specs=vmem, out_shape=jax.ShapeDtypeStruct((N_DEV, rows, cols), t.dtype),
        scratch_shapes=[pltpu.SemaphoreType.DMA((7,)), pltpu.SemaphoreType.DMA((7,)), pltpu.SemaphoreType.DMA],
        name=name)(t)


def _adamw_math(w, g, m, v):
    m = ADAM_B1 * m + (1.0 - ADAM_B1) * g
    v = ADAM_B2 * v + (1.0 - ADAM_B2) * (g * g)
    m_hat = m / (1.0 - ADAM_B1 ** ADAM_STEP)
    v_hat = v / (1.0 - ADAM_B2 ** ADAM_STEP)
    delta = -ADAM_LR * (m_hat / (jnp.sqrt(v_hat) + ADAM_EPS) + ADAM_WD * w)
    return delta, m, v


def _adamw(name, w, g, m, v):
    rows, cols = w.shape
    tr = _pick(rows, 256, 8)

    def body(w_ref, g_ref, m_ref, v_ref, d_out, m_out, v_out):
        d_out[...], m_out[...], v_out[...] = _adamw_math(w_ref[...], g_ref[...], m_ref[...], v_ref[...])

    blk = pl.BlockSpec((tr, cols), lambda i: (i, 0))
    shape = jax.ShapeDtypeStruct((rows, cols), F32)
    return pl.pallas_call(body, grid=(rows // tr,), in_specs=[blk] * 4, out_specs=[blk] * 3, out_shape=[shape] * 3,
                          compiler_params=_params(("parallel",)), name=name)(w, g, m, v)


def _small_reduce(name, parts, width):
    def body(*refs):
        out = refs[-1]
        out[...] = jnp.zeros_like(out)
        for k, r in enumerate(refs[:-1]):
            out[pl.ds(k, 1), :] = jnp.sum(r[...], axis=0, keepdims=True)

    vmem = pl.BlockSpec(memory_space=pltpu.VMEM)
    return pl.pallas_call(body, in_specs=[vmem] * len(parts), out_specs=vmem,
                          out_shape=jax.ShapeDtypeStruct((8, width), F32), name=name)(*parts)


def _small_adamw(name, gathered, w, m, v, loss_row, loss_scale):
    def body(gt_ref, w_ref, m_ref, v_ref, g_out, d_out, m_out, v_out, loss_out):
        g = gt_ref[0]
        for k in range(1, N_DEV):
            g = g + gt_ref[k]
        g_out[...] = g
        row = lax.broadcasted_iota(jnp.int32, g.shape, 0)
        loss_out[...] = jnp.sum(jnp.where(row == loss_row, g, 0.0), keepdims=True) * loss_scale
        d_out[...], m_out[...], v_out[...] = _adamw_math(w_ref[...], g, m_ref[...], v_ref[...])

    vmem = pl.BlockSpec(memory_space=pltpu.VMEM)
    shape = jax.ShapeDtypeStruct(w.shape, F32)
    return pl.pallas_call(body, in_specs=[vmem] * 4, out_specs=[vmem] * 5,
                          out_shape=[shape] * 4 + [jax.ShapeDtypeStruct((1, 1), F32)], name=name)(gathered, w, m, v)


def _full_from_pieces(name, pieces):
    _, rows, cols = pieces.shape
    if name in ROW_SHARDED:
        return pieces.reshape(N_CHIPS * rows, cols)
    return pieces.transpose(1, 0, 2).reshape(rows, N_CHIPS * cols)


def _repack_w_in(w_in, hd, d):
    hh = hd // HEAD_DIM
    qkv, f, gates = w_in[:, :6 * hd], w_in[:, 6 * hd:6 * hd + hh], w_in[:, 6 * hd + hh:]
    return jnp.concatenate([qkv, gates, jnp.pad(f, ((0, 0), (0, LANES - hh)))], axis=1)


def _unpack_dw_in(dw, hd, d):
    hh = hd // HEAD_DIM
    return jnp.concatenate([dw[:, :6 * hd], dw[:, 6 * hd + 2 * d:6 * hd + 2 * d + hh], dw[:, 6 * hd:6 * hd + 2 * d]],
                           axis=1)


def _small_pack(vals, width):
    rows = []
    for name in SMALL:
        t = vals[name].reshape(1, -1)
        rows.append(jnp.pad(t, ((0, 0), (0, width - t.shape[1]))))
    rows.append(jnp.zeros((8 - len(SMALL), width), F32))
    return jnp.concatenate(rows, axis=0)


def kernel(x, ffn1_norm, ffn1_w_gate, ffn1_w_up, ffn1_w_down, mix_norm, w_in, b_forget, b_gate_dil, b_gate_fox, w_proj_dil, w_proj_fox, w_out, ffn2_norm, ffn2_w_gate, ffn2_w_up, ffn2_w_down, final_norm, loss_target, m_ffn1_norm, m_ffn1_w_gate, m_ffn1_w_up, m_ffn1_w_down, m_mix_norm, m_w_in, m_b_forget, m_b_gate_dil, m_b_gate_fox, m_w_proj_dil, m_w_proj_fox, m_w_out, m_ffn2_norm, m_ffn2_w_gate, m_ffn2_w_up, m_ffn2_w_down, m_final_norm, v_ffn1_norm, v_ffn1_w_gate, v_ffn1_w_up, v_ffn1_w_down, v_mix_norm, v_w_in, v_b_forget, v_b_gate_dil, v_b_gate_fox, v_w_proj_dil, v_w_proj_fox, v_w_out, v_ffn2_norm, v_ffn2_w_gate, v_ffn2_w_up, v_ffn2_w_down, v_final_norm):
    given = dict(locals())
    wts = {n: given[n] for n in WEIGHTS}
    mom_m = {n: given["m_" + n] for n in WEIGHTS}
    mom_v = {n: given["v_" + n] for n in WEIGHTS}
    s, d = x.shape[1], x.shape[2]
    hd = w_proj_dil.shape[1]

    gathered = _gather_weights("gather_weights", [wts[n][0].astype(BF) for n in SHARDED])
    full = {n: _full_from_pieces(n, t) for n, t in zip(SHARDED, gathered)}
    full["w_in"] = _repack_w_in(full["w_in"], hd, d)
    small = {n: wts[n] for n in SMALL}

    grad_x, grads, partials = _local_step(x[0], loss_target[0], full, small)

    ids = jnp.stack([2 * lax.axis_index("x") + lax.axis_index("y"), lax.axis_index("c")]).astype(jnp.int32)
    pieces = [grads[n] for n in SHARDED]
    from_sibling = _rs_swap("rs_swap", pieces)
    chip_sums = [_rs_add("rs_add_" + n, ids, g, o) for n, g, o in zip(SHARDED, pieces, from_sibling)]
    from_chips = _rs_scatter("rs_scatter", chip_sums)
    halves = [_rs_sum("rs_sum_" + n, ids, own, got) for n, own, got in zip(SHARDED, chip_sums, from_chips)]
    totals = _rs_join("rs_join", halves)

    out_g, out_d, out_m, out_v = {}, {}, {}, {}
    for n, g in zip(SHARDED, totals):
        dl, nm, nv = _adamw("adamw_" + n, wts[n][0], g, mom_m[n][0], mom_v[n][0])
        out_g[n], out_d[n], out_m[n], out_v[n] = g[None], dl[None], nm[None], nv[None]

    width = d
    part_rows = []
    for n in SMALL:
        t = partials[n]
        part_rows.append(jnp.pad(t, ((0, 0), (0, width - t.shape[1]))))
    part_rows.append(partials["sq"])
    local_small = _small_reduce("small_reduce", part_rows, width)
    gathered_small = _gather_all("small_gather", local_small)
    sg, sd_, sm, sv, loss = _small_adamw("small_adamw", gathered_small, _small_pack(wts, width),
                                         _small_pack(mom_m, width), _small_pack(mom_v, width), len(SMALL), 0.5 / d)
    for k, n in enumerate(SMALL):
        shp = wts[n].shape
        take = lambda t: t[k, :shp[-1]].reshape(shp)
        out_g[n], out_d[n], out_m[n], out_v[n] = take(sg), take(sd_), take(sm), take(sv)
    return (loss[0, 0], grad_x[None], *[out_g[n] for n in WEIGHTS], *[out_d[n] for n in WEIGHTS],
            *[out_m[n] for n in WEIGHTS], *[out_v[n] for n in WEIGHTS])
```

```python
import functools
import math

import numpy as np
import jax
import jax.numpy as jnp
from jax import lax
from jax.experimental import pallas as pl
from jax.experimental.pallas import tpu as pltpu

HEAD_DIM = 128
ROPE_DIM = HEAD_DIM // 4
ROPE_THETA = 500000.0
DIL_PATTERNS = ((128, 1), (512, 4), (2048, 16))
MAX_WINDOW = 2048
NORM_EPS = 1e-6
ADAM_LR = 0.001
ADAM_B1 = 0.9
ADAM_B2 = 0.999
ADAM_EPS = 1e-08
ADAM_WD = 0.01
ADAM_STEP = 10

BF = jnp.bfloat16
F32 = jnp.float32
NEG = -1e30
LANES = 128
ATT_BLOCK = 512
ATT_HEADS = 2
VMEM_LIMIT = 56 * 1024 * 1024
MM_VMEM_BUDGET = 40 * 1024 * 1024
N_CHIPS = 4
N_DEV = 8
MESH = pl.DeviceIdType.MESH

SHARDED = ("ffn1_w_gate", "ffn1_w_up", "ffn1_w_down", "w_in", "w_proj_dil", "w_proj_fox", "w_out",
           "ffn2_w_gate", "ffn2_w_up", "ffn2_w_down")
ROW_SHARDED = ("ffn1_w_down", "w_out", "ffn2_w_down")
SMALL = ("ffn1_norm", "mix_norm", "b_forget", "b_gate_dil", "b_gate_fox", "ffn2_norm", "final_norm")
WEIGHTS = ("ffn1_norm", "ffn1_w_gate", "ffn1_w_up", "ffn1_w_down", "mix_norm", "w_in", "b_forget",
           "b_gate_dil", "b_gate_fox", "w_proj_dil", "w_proj_fox", "w_out", "ffn2_norm", "ffn2_w_gate",
           "ffn2_w_up", "ffn2_w_down", "final_norm")


def _pick(n, target, align):
    best = None
    for d in range(align, min(n, target) + 1, align):
        if n % d == 0:
            best = d
    return n if best is None else best


def _params(sem=None):
    return pltpu.CompilerParams(dimension_semantics=sem, vmem_limit_bytes=VMEM_LIMIT)


_DIMS = {"nn": (((1,), (0,)), ((), ())), "nt": (((1,), (1,)), ((), ())), "tn": (((0,), (0,)), ((), ()))}


def _mm(name, pairs, mode, m, n, k, *, epilogue, out_dtypes, extras=(), rows=(), n_colsum=0,
        sum_pairs=False, tm=1024, tn=1152, tk=2048, piece_layout=False):
    m_align = LANES if mode == "tn" else 8
    tm = _pick(m, tm, m_align)
    tn = n // N_CHIPS if piece_layout else _pick(n, tn, LANES)
    tk = _pick(k, tk, LANES)
    n_acc = 1 if sum_pairs else len(pairs)
    n_in = 2 * len(pairs) + len(extras) + len(rows)
    n_out = len(out_dtypes) + n_colsum

    def vmem_bytes(tm_, tn_, tk_):
        tiles = sum(tm_ * tk_ * a.dtype.itemsize + tn_ * tk_ * b.dtype.itemsize for a, b in pairs)
        tiles += sum(tm_ * tn_ * arr.dtype.itemsize for arr, _ in extras)
        tiles += sum(tm_ * tn_ * jnp.dtype(dt).itemsize for dt in out_dtypes)
        return 2 * tiles + (n_acc + len(extras) + len(out_dtypes)) * tm_ * tn_ * 4

    while vmem_bytes(tm, tn, tk) > MM_VMEM_BUDGET:
        if tk > 512:
            tk = _pick(k, tk - LANES, LANES)
        elif tn > 512 and not piece_layout:
            tn = _pick(n, tn - LANES, LANES)
        elif tm > 256:
            tm = _pick(m, tm - m_align, m_align)
        else:
            break
    nk = k // tk

    def body(*refs):
        ins, outs, accs = refs[:n_in], refs[n_in:n_in + n_out], refs[n_in + n_out:]
        kk = pl.program_id(2)

        @pl.when(kk == 0)
        def _():
            for acc in accs:
                acc[...] = jnp.zeros_like(acc)

        for p in range(len(pairs)):
            a = ins[2 * p][...].astype(BF)
            b = ins[2 * p + 1][...].astype(BF)
            accs[0 if sum_pairs else p][...] += lax.dot_general(a, b, _DIMS[mode], preferred_element_type=F32)

        @pl.when(kk == nk - 1)
        def _():
            ex = [r[...] for r in ins[2 * len(pairs):2 * len(pairs) + len(extras)]]
            rw = [r[...] for r in ins[2 * len(pairs) + len(extras):]]
            res = epilogue([acc[...] for acc in accs], ex, rw)
            for o, r in zip(outs, res):
                o[...] = r.astype(o.dtype)

    in_specs, args = [], []
    for a, b in pairs:
        if mode == "tn":
            in_specs.append(pl.BlockSpec((tk, tm), lambda i, j, kk: (kk, i)))
        else:
            in_specs.append(pl.BlockSpec((tm, tk), lambda i, j, kk: (i, kk)))
        if mode == "nt":
            in_specs.append(pl.BlockSpec((tn, tk), lambda i, j, kk: (j, kk)))
        else:
            in_specs.append(pl.BlockSpec((tk, tn), lambda i, j, kk: (kk, j)))
        args += [a, b]
    for arr, off in extras:
        assert off % tn == 0
        in_specs.append(pl.BlockSpec((tm, tn), functools.partial(lambda i, j, kk, o: (i, j + o), o=off // tn)))
        args.append(arr)
    for arr in rows:
        in_specs.append(pl.BlockSpec((1, tn), lambda i, j, kk: (0, j)))
        args.append(arr)
    if piece_layout:
        out_specs = [pl.BlockSpec((None, tm, tn), lambda i, j, kk: (j, i, 0)) for _ in out_dtypes]
        out_shape = [jax.ShapeDtypeStruct((n // tn, m, tn), d) for d in out_dtypes]
    else:
        out_specs = [pl.BlockSpec((tm, tn), lambda i, j, kk: (i, j)) for _ in out_dtypes]
        out_shape = [jax.ShapeDtypeStruct((m, n), d) for d in out_dtypes]
    for _ in range(n_colsum):
        out_specs.append(pl.BlockSpec((None, 1, tn), lambda i, j, kk: (i, 0, j)))
        out_shape.append(jax.ShapeDtypeStruct((m // tm, 1, n), F32))
    return pl.pallas_call(
        body, grid=(m // tm, n // tn, nk), in_specs=in_specs, out_specs=out_specs, out_shape=out_shape,
        scratch_shapes=[pltpu.VMEM((tm, tn), F32) for _ in range(n_acc)],
        compiler_params=_params(("parallel", "parallel", "arbitrary")), name=name)(*args)


def _col_pieces(full):
    rows, cols = full.shape
    return full.reshape(rows, N_CHIPS, cols // N_CHIPS).transpose(1, 0, 2)


def _dw_col_pieces(name, a, b, m, n, k):
    ident = lambda accs, ex, rw: (accs[0],)
    if (n // N_CHIPS) % LANES == 0:
        return _mm(name, [(a, b)], "tn", m, n, k, epilogue=ident, out_dtypes=(F32,), tm=512, piece_layout=True)[0]
    return _col_pieces(_mm(name, [(a, b)], "tn", m, n, k, epilogue=ident, out_dtypes=(F32,))[0])


def _sigmoid(z):
    return 0.5 * jnp.tanh(0.5 * z) + 0.5


def _row_tile(s):
    return _pick(s, 256, 8)


def _fold8(t):
    r, d = t.shape
    return jnp.sum(t.reshape(r // 8, 8, d), axis=0)


def _rms_fwd(name, x, g):
    s, d = x.shape
    tr = _row_tile(s)

    def body(x_ref, g_ref, h_ref):
        xf = x_ref[...]
        y = xf * lax.rsqrt(jnp.mean(xf * xf, axis=-1, keepdims=True) + NORM_EPS)
        h_ref[...] = (y * g_ref[...]).astype(BF)

    return pl.pallas_call(
        body, grid=(s // tr,),
        in_specs=[pl.BlockSpec((tr, d), lambda i: (i, 0)), pl.BlockSpec((1, d), lambda i: (0, 0))],
        out_specs=pl.BlockSpec((tr, d), lambda i: (i, 0)), out_shape=jax.ShapeDtypeStruct((s, d), BF),
        compiler_params=_params(("parallel",)), name=name)(x, g)


def _rms_bwd(name, x, g, dh, dres):
    s, d = x.shape
    tr = _row_tile(s)

    def body(x_ref, g_ref, dh_ref, dres_ref, dx_ref, dg_ref):
        @pl.when(pl.program_id(0) == 0)
        def _():
            dg_ref[...] = jnp.zeros_like(dg_ref)

        xf = x_ref[...]
        rstd = lax.rsqrt(jnp.mean(xf * xf, axis=-1, keepdims=True) + NORM_EPS)
        xhat = xf * rstd
        dhf = dh_ref[...]
        dg_ref[...] += _fold8(dhf * xhat)
        dxh = dhf * g_ref[...]
        dx_ref[...] = dres_ref[...] + rstd * (dxh - xhat * jnp.mean(dxh * xhat, axis=-1, keepdims=True))

    blk = pl.BlockSpec((tr, d), lambda i: (i, 0))
    return pl.pallas_call(
        body, grid=(s // tr,),
        in_specs=[blk, pl.BlockSpec((1, d), lambda i: (0, 0)), blk, blk],
        out_specs=[blk, pl.BlockSpec((8, d), lambda i: (0, 0))],
        out_shape=[jax.ShapeDtypeStruct((s, d), F32), jax.ShapeDtypeStruct((8, d), F32)],
        compiler_params=_params(("arbitrary",)), name=name)(x, g, dh, dres)


def _final(name, x, g, tgt):
    s, d = x.shape
    tr = _row_tile(s)

    def body(x_ref, g_ref, t_ref, dx_ref, dg_ref, sq_ref):
        @pl.when(pl.program_id(0) == 0)
        def _():
            dg_ref[...] = jnp.zeros_like(dg_ref)
            sq_ref[...] = jnp.zeros_like(sq_ref)

        xf = x_ref[...]
        rstd = lax.rsqrt(jnp.mean(xf * xf, axis=-1, keepdims=True) + NORM_EPS)
        xhat = xf * rstd
        gf = g_ref[...]
        err = xhat * gf - t_ref[...]
        sq_ref[...] += _fold8(err * err)
        dy = err * (1.0 / d)
        dg_ref[...] += _fold8(dy * xhat)
        dxh = dy * gf
        dx_ref[...] = rstd * (dxh - xhat * jnp.mean(dxh * xhat, axis=-1, keepdims=True))

    blk = pl.BlockSpec((tr, d), lambda i: (i, 0))
    acc = pl.BlockSpec((8, d), lambda i: (0, 0))
    return pl.pallas_call(
        body, grid=(s // tr,), in_specs=[blk, pl.BlockSpec((1, d), lambda i: (0, 0)), blk],
        out_specs=[blk, acc, acc],
        out_shape=[jax.ShapeDtypeStruct((s, d), F32), jax.ShapeDtypeStruct((8, d), F32),
                   jax.ShapeDtypeStruct((8, d), F32)],
        compiler_params=_params(("arbitrary",)), name=name)(x, g, tgt)


def _ffn_fwd(tag, x, g, w_gate, w_up, w_down):
    s, d = x.shape
    f = w_gate.shape[1]
    h = _rms_fwd(tag + "_rms", x, g)

    def up_epi(accs, ex, rw):
        a, b = accs
        return a, b, a * _sigmoid(a) * b

    a, b, act = _mm(tag + "_up", [(h, w_gate), (h, w_up)], "nn", s, f, d, epilogue=up_epi,
                    out_dtypes=(F32, F32, BF))

    def down_epi(accs, ex, rw):
        return (ex[0] + 0.5 * accs[0],)

    (y,) = _mm(tag + "_down", [(act, w_down)], "nn", s, d, f, epilogue=down_epi, out_dtypes=(F32,),
               extras=[(x, 0)])
    return y, (h, a, b, act)


def _ffn_bwd(tag, x, g, w_gate, w_up, w_down, saved, dy):
    s, d = x.shape
    f = w_gate.shape[1]
    h, a, b, act = saved

    def act_epi(accs, ex, rw):
        dact = 0.5 * accs[0]
        av, bv = ex
        sg = _sigmoid(av)
        return dact * bv * (sg * (1.0 + av * (1.0 - sg))), dact * (av * sg)

    da, db = _mm(tag + "_dact", [(dy, w_down)], "nt", s, f, d, epilogue=act_epi, out_dtypes=(BF, BF),
                 extras=[(a, 0), (b, 0)])
    ident = lambda accs, ex, rw: (accs[0],)
    (dw_down,) = _mm(tag + "_dwd", [(act, dy)], "tn", f, d, s, epilogue=lambda accs, ex, rw: (0.5 * accs[0],),
                     out_dtypes=(F32,))
    dw_down = dw_down.reshape(N_CHIPS, f // N_CHIPS, d)
    dw_gate = _dw_col_pieces(tag + "_dwg", h, da, d, f, s)
    dw_up = _dw_col_pieces(tag + "_dwu", h, db, d, f, s)
    (dh,) = _mm(tag + "_dh", [(da, w_gate), (db, w_up)], "nt", s, d, f, epilogue=ident, out_dtypes=(F32,),
                sum_pairs=True)
    dx, dg = _rms_bwd(tag + "_rmsb", x, g, dh, dy)
    return dx, dg, dw_gate, dw_up, dw_down


def _rope_tables(s):
    half = ROPE_DIM // 2
    pos = jnp.arange(s, dtype=F32)
    inv_freq = ROPE_THETA ** (-jnp.arange(0, ROPE_DIM, 2, dtype=F32) / ROPE_DIM)
    ang = pos[:, None] * inv_freq[None, :]
    cos, sin = jnp.cos(ang), jnp.sin(ang)
    rest = HEAD_DIM - ROPE_DIM
    cos_t = jnp.concatenate([cos, cos, jnp.ones((s, rest), F32)], axis=-1)
    sin_t = jnp.concatenate([-sin, sin, jnp.zeros((s, rest), F32)], axis=-1)
    return cos_t, sin_t


def _swap_halves(t):
    lane = lax.broadcasted_iota(jnp.int32, t.shape, 1)
    half = ROPE_DIM // 2
    return jnp.where(lane < half, pltpu.roll(t, HEAD_DIM - half, 1), pltpu.roll(t, half, 1))


def _dil_bias(blk):
    n_delta = MAX_WINDOW // blk + 1
    delta = jnp.arange(n_delta, dtype=jnp.int32)[:, None, None]
    r = jnp.arange(blk, dtype=jnp.int32)[None, None, :]
    c = jnp.arange(blk, dtype=jnp.int32)[None, :, None]
    o = delta * blk + r - c
    mult = jnp.zeros(o.shape, F32)
    for w, dd in DIL_PATTERNS:
        mult = mult + ((o >= 0) & (o <= w) & (o % dd == 0)).astype(F32)
    return jnp.where(mult > 0, jnp.log(jnp.maximum(mult, 1.0)), NEG)


def _att_block(s):
    return _pick(s, ATT_BLOCK, LANES)


def _fox_aug(name, c_pad, n_heads):
    s = c_pad.shape[0]
    tr = _pick(s, 1024, 16)

    def body(c_ref, qx_ref, kx_ref):
        h = pl.program_id(1)
        lane = lax.broadcasted_iota(jnp.int32, (tr, LANES), 1)
        ch = jnp.sum(jnp.where(lane == h, c_ref[...], 0.0), axis=1, keepdims=True)
        hi, mid, lo = (t.astype(F32) for t in _split3(ch))
        zero = jnp.zeros((tr, LANES), F32)
        is_hi = jnp.logical_or(lane == 0, lane == 3)
        is_mid = jnp.logical_or(lane == 1, lane == 4)
        parts = jnp.where(is_hi, hi, jnp.where(is_mid, mid, lo))
        qx_ref[...] = jnp.where(lane < 3, 1.0, jnp.where(lane < 6, parts, zero)).astype(BF)
        kx_ref[...] = jnp.where(lane < 3, -parts, jnp.where(lane < 6, 1.0, zero)).astype(BF)

    spec = pl.BlockSpec((tr, LANES), lambda i, h: (i, h))
    shape = jax.ShapeDtypeStruct((s, n_heads * LANES), BF)
    return pl.pallas_call(
        body, grid=(s // tr, n_heads), in_specs=[pl.BlockSpec((tr, LANES), lambda i, h: (i, 0))],
        out_specs=[spec, spec], out_shape=[shape, shape],
        compiler_params=_params(("parallel", "arbitrary")), name=name)(c_pad)


def _causal_mask(st):
    kpos = lax.broadcasted_iota(jnp.int32, st.shape, 0)
    qpos = lax.broadcasted_iota(jnp.int32, st.shape, 1)
    return jnp.where(kpos <= qpos, st, NEG)


def _flash_fwd(name, qkv, q_off, k_off, v_off, n_heads, *, fox, tab_t=None, qx=None, kx=None):
    s = qkv.shape[0]
    blk = _att_block(s)
    nq = s // blk
    n_delta = MAX_WINDOW // blk + 1
    grp = ATT_HEADS
    assert n_heads % grp == 0 and q_off % grp == 0 and k_off % grp == 0 and v_off % grp == 0
    wide = grp * HEAD_DIM

    def body(*refs):
        if fox:
            q_ref, k_ref, v_ref, qx_ref, kx_ref, o_ref, lse_ref, acc, m_s, l_s = refs
        else:
            q_ref, k_ref, v_ref, tab_ref, o_ref, lse_ref, acc, m_s, l_s = refs
        i = pl.program_id(1)
        acc[...] = jnp.zeros_like(acc)
        m_s[...] = jnp.full_like(m_s, NEG)
        l_s[...] = jnp.zeros_like(l_s)

        def step(j, diagonal):
            ks = pl.ds(pl.multiple_of(j * blk, blk), blk)
            for g in range(grp):
                cols = slice(g * HEAD_DIM, (g + 1) * HEAD_DIM)
                st = lax.dot_general(k_ref[ks, cols], q_ref[:, cols], _DIMS["nt"], preferred_element_type=F32)
                if fox:
                    st = st + lax.dot_general(kx_ref[ks, cols], qx_ref[:, cols], _DIMS["nt"],
                                              preferred_element_type=F32)
                    if diagonal:
                        st = _causal_mask(st)
                else:
                    st = st + tab_ref[i - j]
                m_prev = m_s[g]
                m_new = jnp.maximum(m_prev, jnp.max(st, axis=0, keepdims=True))
                alpha = jnp.exp(m_prev - m_new)
                p = jnp.exp(st - m_new)
                l_s[g] = alpha * l_s[g] + jnp.sum(p, axis=0, keepdims=True)
                acc[g] = alpha * acc[g] + lax.dot_general(v_ref[ks, cols], p.astype(BF), _DIMS["tn"],
                                                          preferred_element_type=F32)
                m_s[g] = m_new

        def loop_step(j, carry):
            step(j, False)
            return carry

        if fox:
            lax.fori_loop(0, i, loop_step, 0)
            step(i, True)
        else:
            lax.fori_loop(jnp.maximum(i - (n_delta - 1), 0), i + 1, loop_step, 0)
        for g in range(grp):
            o_ref[:, g * HEAD_DIM:(g + 1) * HEAD_DIM] = (acc[g] / l_s[g]).T.astype(o_ref.dtype)
            lse_ref[g] = m_s[g] + jnp.log(l_s[g])

    off = lambda o: functools.partial(lambda h, i, o: (0, o + h), o=o // grp)
    in_specs = [pl.BlockSpec((blk, wide), functools.partial(lambda h, i, o: (i, o + h), o=q_off // grp)),
                pl.BlockSpec((s, wide), off(k_off)), pl.BlockSpec((s, wide), off(v_off))]
    args = [qkv, qkv, qkv]
    if fox:
        in_specs += [pl.BlockSpec((blk, wide), lambda h, i: (i, h)), pl.BlockSpec((s, wide), lambda h, i: (0, h))]
        args += [qx, kx]
    else:
        in_specs.append(pl.BlockSpec((n_delta, blk, blk), lambda h, i: (0, 0, 0)))
        args.append(tab_t)
    return pl.pallas_call(
        body, grid=(n_heads // grp, nq), in_specs=in_specs,
        out_specs=[pl.BlockSpec((blk, wide), lambda h, i: (i, h)),
                   pl.BlockSpec((grp, None, 1, blk), lambda h, i: (h, i, 0, 0))],
        out_shape=[jax.ShapeDtypeStruct((s, n_heads * HEAD_DIM), BF),
                   jax.ShapeDtypeStruct((n_heads, nq, 1, blk), F32)],
        scratch_shapes=[pltpu.VMEM((grp, HEAD_DIM, blk), F32), pltpu.VMEM((grp, 1, blk), F32),
                        pltpu.VMEM((grp, 1, blk), F32)],
        compiler_params=_params(("parallel", "parallel")), name=name)(*args)


def _att_delta(name, do, o, n_heads):
    s = do.shape[0]
    blk = _pick(s, 1024, 16)

    def body(do_ref, o_ref, d_ref):
        d_ref[...] = jnp.sum(do_ref[...].astype(F32) * o_ref[...].astype(F32), axis=-1, keepdims=True)

    spec = pl.BlockSpec((blk, HEAD_DIM), lambda h, i: (i, h))
    return pl.pallas_call(
        body, grid=(n_heads, s // blk), in_specs=[spec, spec],
        out_specs=pl.BlockSpec((None, blk, 1), lambda h, i: (h, i, 0)),
        out_shape=jax.ShapeDtypeStruct((n_heads, s, 1), F32),
        compiler_params=_params(("parallel", "parallel")), name=name)(do, o)


def _flash_bwd(name, qkv, q_off, k_off, v_off, do, lse_row, delta_row, n_heads, *, fox, tab_t=None,
               qx=None, kx=None):
    s = qkv.shape[0]
    blk = _att_block(s)
    nq = s // blk
    n_delta = MAX_WINDOW // blk + 1
    grp = ATT_HEADS
    assert n_heads % grp == 0 and q_off % grp == 0 and k_off % grp == 0 and v_off % grp == 0
    wide = grp * HEAD_DIM

    def body(*refs):
        if fox:
            (q_ref, do_ref, k_ref, v_ref, lse_ref, dl_ref, qx_ref, kx_ref,
             dq_ref, dk_ref, dv_ref, dc_ref, dcq_ref, dk_acc, dv_acc, dc_acc) = refs
        else:
            (q_ref, do_ref, k_ref, v_ref, lse_ref, dl_ref, tab_ref,
             dq_ref, dk_ref, dv_ref, dk_acc, dv_acc) = refs
        j = pl.program_id(1)

        @pl.when(j == 0)
        def _():
            dq_ref[...] = jnp.zeros_like(dq_ref)
            if fox:
                dcq_ref[...] = jnp.zeros_like(dcq_ref)

        dk_acc[...] = jnp.zeros_like(dk_acc)
        dv_acc[...] = jnp.zeros_like(dv_acc)
        if fox:
            dc_acc[...] = jnp.zeros_like(dc_acc)

        def step(i, diagonal):
            qs = pl.ds(pl.multiple_of(i * blk, blk), blk)
            for g in range(grp):
                cols = slice(g * HEAD_DIM, (g + 1) * HEAD_DIM)
                kb, vb = k_ref[:, cols], v_ref[:, cols]
                qb, dob = q_ref[qs, cols], do_ref[qs, cols]
                st = lax.dot_general(kb, qb, _DIMS["nt"], preferred_element_type=F32)
                if fox:
                    st = st + lax.dot_general(kx_ref[:, cols], qx_ref[qs, cols], _DIMS["nt"],
                                              preferred_element_type=F32)
                    if diagonal:
                        st = _causal_mask(st)
                else:
                    st = st + tab_ref[i - j]
                pt = jnp.exp(st - lse_ref[g, i])
                dv_acc[:, cols] += jnp.dot(pt.astype(BF), dob, preferred_element_type=F32)
                dpt = lax.dot_general(vb, dob, _DIMS["nt"], preferred_element_type=F32)
                dst = pt * (dpt - dl_ref[g, i])
                dsb = dst.astype(BF)
                dk_acc[:, cols] += jnp.dot(dsb, qb, preferred_element_type=F32)
                dq_ref[qs, cols] += lax.dot_general(dsb, kb, _DIMS["tn"], preferred_element_type=F32)
                if fox:
                    folded = dst[:, :LANES]
                    for part in range(1, blk // LANES):
                        folded = folded + dst[:, part * LANES:(part + 1) * LANES]
                    dc_acc[g] -= folded
                    dcq_ref[g, i] += jnp.sum(dst, axis=0, keepdims=True)

        def loop_step(i, carry):
            step(i, False)
            return carry

        if fox:
            step(j, True)
            lax.fori_loop(j + 1, nq, loop_step, 0)
        else:
            lax.fori_loop(j, jnp.minimum(nq, j + n_delta), loop_step, 0)
        dk_ref[...] = dk_acc[...]
        dv_ref[...] = dv_acc[...].astype(dv_ref.dtype)
        if fox:
            for g in range(grp):
                dc_ref[g] = jnp.sum(dc_acc[g], axis=-1, keepdims=True)

    full = lambda o: pl.BlockSpec((s, wide), functools.partial(lambda h, j, o: (0, o + h), o=o // grp))
    tile = lambda o: pl.BlockSpec((blk, wide), functools.partial(lambda h, j, o: (j, o + h), o=o // grp))
    per_q = pl.BlockSpec((grp, nq, 1, blk), lambda h, j: (h, 0, 0, 0))
    per_k = pl.BlockSpec((grp, blk, 1), lambda h, j: (h, j, 0))
    in_specs = [full(q_off), full(0), tile(k_off), tile(v_off), per_q, per_q]
    args = [qkv, do, qkv, qkv, lse_row, delta_row]
    out_specs = [full(0), tile(0), tile(0)]
    hd = n_heads * HEAD_DIM
    out_shape = [jax.ShapeDtypeStruct((s, hd), F32), jax.ShapeDtypeStruct((s, hd), F32),
                 jax.ShapeDtypeStruct((s, hd), BF)]
    scratch = [pltpu.VMEM((blk, wide), F32), pltpu.VMEM((blk, wide), F32)]
    if fox:
        in_specs += [full(0), tile(0)]
        args += [qx, kx]
        out_specs += [per_k, per_q]
        out_shape += [jax.ShapeDtypeStruct((n_heads, s, 1), F32), jax.ShapeDtypeStruct((n_heads, nq, 1, blk), F32)]
        scratch.append(pltpu.VMEM((grp, blk, LANES), F32))
    else:
        in_specs.append(pl.BlockSpec((n_delta, blk, blk), lambda h, j: (0, 0, 0)))
        args.append(tab_t)
    return pl.pallas_call(
        body, grid=(n_heads // grp, nq), in_specs=in_specs, out_specs=out_specs, out_shape=out_shape,
        scratch_shapes=scratch, compiler_params=_params(("parallel", "arbitrary")), name=name)(*args)


def _split3(t):
    hi = t.astype(BF)
    r1 = t - hi.astype(F32)
    mid = r1.astype(BF)
    lo = (r1 - mid.astype(F32)).astype(BF)
    return hi, mid, lo


def _tri_dot(tri, t):
    hi, mid, lo = _split3(t)
    return (jnp.dot(tri, hi, preferred_element_type=F32) + jnp.dot(tri, mid, preferred_element_type=F32)
            + jnp.dot(tri, lo, preferred_element_type=F32))


def _log_sigmoid(z):
    return jnp.minimum(z, 0.0) - jnp.log(1.0 + jnp.exp(-jnp.abs(z)))


def _forget_cumsum(name, proj, f_col, bias):
    s = proj.shape[0]
    blk = _att_block(s)

    def body(f_ref, b_ref, c_ref, carry):
        @pl.when(pl.program_id(0) == 0)
        def _():
            carry[...] = jnp.zeros_like(carry)

        lf = _log_sigmoid(f_ref[...] + b_ref[...])
        r = lax.broadcasted_iota(jnp.int32, (blk, blk), 0)
        c = lax.broadcasted_iota(jnp.int32, (blk, blk), 1)
        tri = (c <= r).astype(BF)
        c_ref[...] = _tri_dot(tri, lf) + carry[...]
        carry[...] = c_ref[pl.ds(blk - 1, 1), :]

    return pl.pallas_call(
        body, grid=(s // blk,),
        in_specs=[pl.BlockSpec((blk, LANES), lambda i: (i, f_col)), pl.BlockSpec((1, LANES), lambda i: (0, 0))],
        out_specs=pl.BlockSpec((blk, LANES), lambda i: (i, 0)), out_shape=jax.ShapeDtypeStruct((s, LANES), F32),
        scratch_shapes=[pltpu.VMEM((1, LANES), F32)],
        compiler_params=_params(("arbitrary",)), name=name)(proj, bias)


def _forget_bwd(name, proj, f_col, bias, dc):
    s = proj.shape[0]
    blk = _att_block(s)
    nb = s // blk

    def body(f_ref, b_ref, dc_ref, df_ref, db_ref, carry):
        @pl.when(pl.program_id(0) == 0)
        def _():
            carry[...] = jnp.zeros_like(carry)
            db_ref[...] = jnp.zeros_like(db_ref)

        r = lax.broadcasted_iota(jnp.int32, (blk, blk), 0)
        c = lax.broadcasted_iota(jnp.int32, (blk, blk), 1)
        tri = (c >= r).astype(BF)
        r = lax.broadcasted_iota(jnp.int32, (blk, LANES), 0)
        dlf = _tri_dot(tri, dc_ref[...]) + carry[...]
        carry[...] = jnp.sum(jnp.where(r == 0, dlf, 0.0), axis=0, keepdims=True)
        dz = dlf * _sigmoid(-(f_ref[...] + b_ref[...]))
        df_ref[...] = dz.astype(BF)
        db_ref[...] += _fold8(dz)

    rev = lambda i: (nb - 1 - i, 0)
    return pl.pallas_call(
        body, grid=(nb,),
        in_specs=[pl.BlockSpec((blk, LANES), lambda i: (nb - 1 - i, f_col)), pl.BlockSpec((1, LANES), lambda i: (0, 0)),
                  pl.BlockSpec((blk, LANES), rev)],
        out_specs=[pl.BlockSpec((blk, LANES), rev), pl.BlockSpec((8, LANES), lambda i: (0, 0))],
        out_shape=[jax.ShapeDtypeStruct((s, LANES), BF), jax.ShapeDtypeStruct((8, LANES), F32)],
        scratch_shapes=[pltpu.VMEM((1, LANES), F32)],
        compiler_params=_params(("arbitrary",)), name=name)(proj, bias, dc)


def _head_prep(name, proj, cos_t, sin_t, n_heads):
    s = proj.shape[0]
    tr = _pick(s, 1024, 16)
    scale = HEAD_DIM ** -0.5
    hh = n_heads

    def body(p_ref, cos_ref, sin_ref, o_ref):
        j = pl.program_id(1)
        t = p_ref[...]

        @pl.when(j < 2 * hh)
        def _():
            r = t * cos_ref[...] + _swap_halves(t) * sin_ref[...]
            o_ref[...] = jnp.where(j < hh, r * scale, r).astype(BF)

        @pl.when(j >= 2 * hh)
        def _():
            is_q = jnp.logical_and(j >= 3 * hh, j < 4 * hh)
            o_ref[...] = jnp.where(is_q, t * scale, t).astype(BF)

    tab = pl.BlockSpec((tr, HEAD_DIM), lambda i, j: (i, 0))
    blk = pl.BlockSpec((tr, HEAD_DIM), lambda i, j: (i, j))
    return pl.pallas_call(
        body, grid=(s // tr, 6 * hh), in_specs=[blk, tab, tab], out_specs=blk,
        out_shape=jax.ShapeDtypeStruct((s, 6 * hh * HEAD_DIM), BF),
        compiler_params=_params(("parallel", "arbitrary")), name=name)(proj, cos_t, sin_t)


def _dproj_assemble(name, parts, dgate, df, cos_t, sin_t, n_heads, d_model):
    s = cos_t.shape[0]
    tr = _pick(s, 1024, 16)
    scale = HEAD_DIM ** -0.5
    hh = n_heads
    n_head_blocks = 6 * hh
    n_gate_blocks = 2 * d_model // HEAD_DIM
    n_blocks = n_head_blocks + n_gate_blocks + 1

    def body(*refs):
        p_refs, g_ref, f_ref, cos_ref, sin_ref, o_ref = refs[:6], refs[6], refs[7], refs[8], refs[9], refs[10]
        j = pl.program_id(1)
        for kind in range(6):
            @pl.when(jnp.logical_and(j >= kind * hh, j < (kind + 1) * hh))
            def _(kind=kind):
                t = p_refs[kind][...].astype(F32)
                if kind in (0, 3):
                    t = t * scale
                if kind in (0, 1):
                    t = t * cos_ref[...] - _swap_halves(t) * sin_ref[...]
                o_ref[...] = t.astype(BF)

        @pl.when(jnp.logical_and(j >= n_head_blocks, j < n_head_blocks + n_gate_blocks))
        def _():
            o_ref[...] = g_ref[...]

        @pl.when(j == n_blocks - 1)
        def _():
            o_ref[...] = f_ref[...]

    def head_spec(kind):
        return pl.BlockSpec((tr, HEAD_DIM), functools.partial(
            lambda i, j, kind: (i, jnp.clip(j - kind * hh, 0, hh - 1)), kind=kind))

    in_specs = [head_spec(kind) for kind in range(6)]
    in_specs.append(pl.BlockSpec((tr, HEAD_DIM), lambda i, j: (i, jnp.clip(j - n_head_blocks, 0, n_gate_blocks - 1))))
    tab = pl.BlockSpec((tr, HEAD_DIM), lambda i, j: (i, 0))
    in_specs += [tab, tab, tab]
    return pl.pallas_call(
        body, grid=(s // tr, n_blocks), in_specs=in_specs,
        out_specs=pl.BlockSpec((tr, HEAD_DIM), lambda i, j: (i, j)),
        out_shape=jax.ShapeDtypeStruct((s, n_blocks * HEAD_DIM), BF),
        compiler_params=_params(("parallel", "arbitrary")), name=name)(*parts, dgate, df, cos_t, sin_t)


def _local_step(x, tgt, w, small):
    s, d = x.shape
    hh = w["w_proj_dil"].shape[0] // HEAD_DIM
    hd = hh * HEAD_DIM
    blk = _att_block(s)
    gate_off = 6 * hd
    f_col = (gate_off + 2 * d) // LANES
    n_proj = gate_off + 2 * d + LANES
    ident = lambda accs, ex, rw: (accs[0],)

    x1, saved1 = _ffn_fwd("ffn1", x, small["ffn1_norm"], w["ffn1_w_gate"], w["ffn1_w_up"], w["ffn1_w_down"])

    hm = _rms_fwd("mix_rms", x1, small["mix_norm"])
    (proj,) = _mm("proj", [(hm, w["w_in"])], "nn", s, n_proj, d, epilogue=ident, out_dtypes=(F32,))
    cos_t, sin_t = _rope_tables(s)
    qkv = _head_prep("head_prep", proj, cos_t, sin_t, hh)
    tab_t = _dil_bias(blk)
    y_dil, lse_d = _flash_fwd("dil_fwd", qkv, 0, hh, 2 * hh, hh, fox=False, tab_t=tab_t)
    bias_f = jnp.pad(small["b_forget"], ((0, 0), (0, LANES - hh)))
    c_pad = _forget_cumsum("forget_cumsum", proj, f_col, bias_f)
    qx, kx = _fox_aug("fox_aug", c_pad, hh)
    y_fox, lse_f = _flash_fwd("fox_fwd", qkv, 3 * hh, 4 * hh, 5 * hh, hh, fox=True, qx=qx, kx=kx)

    def merge_epi(accs, ex, rw):
        ud, uf = accs
        return ud, uf, _sigmoid(ex[0] + rw[0]) * ud + _sigmoid(ex[1] + rw[1]) * uf

    u_d, u_f, merged = _mm("merge", [(y_dil, w["w_proj_dil"]), (y_fox, w["w_proj_fox"])], "nn", s, d, hd,
                           epilogue=merge_epi, out_dtypes=(F32, F32, BF),
                           extras=[(proj, gate_off), (proj, gate_off + d)],
                           rows=[small["b_gate_dil"], small["b_gate_fox"]])
    (x2,) = _mm("mix_out", [(merged, w["w_out"])], "nn", s, d, d,
                epilogue=lambda accs, ex, rw: (ex[0] + accs[0],), out_dtypes=(F32,), extras=[(x1, 0)])

    x3, saved2 = _ffn_fwd("ffn2", x2, small["ffn2_norm"], w["ffn2_w_gate"], w["ffn2_w_up"], w["ffn2_w_down"])
    dx3, dg_final, sq = _final("final", x3, small["final_norm"].reshape(1, d), tgt)

    dx2, dg_ffn2, dw_g2, dw_u2, dw_d2 = _ffn_bwd("ffn2", x2, small["ffn2_norm"], w["ffn2_w_gate"], w["ffn2_w_up"],
                                                 w["ffn2_w_down"], saved2, dx3)

    def dmerge_epi(accs, ex, rw):
        dm = accs[0]
        gd, gf, ud, uf = ex
        sd, sf = _sigmoid(gd + rw[0]), _sigmoid(gf + rw[1])
        dgd = dm * ud * (sd * (1.0 - sd))
        dgf = dm * uf * (sf * (1.0 - sf))
        return (dm * sd, dm * sf, dgd, dgf, jnp.sum(dgd, axis=0, keepdims=True), jnp.sum(dgf, axis=0, keepdims=True))

    du_d, du_f, dg_d, dg_f, dbg_d, dbg_f = _mm(
        "dmerge", [(dx2, w["w_out"])], "nt", s, d, d, epilogue=dmerge_epi, out_dtypes=(BF, BF, BF, BF), n_colsum=2,
        extras=[(proj, gate_off), (proj, gate_off + d), (u_d, 0), (u_f, 0)],
        rows=[small["b_gate_dil"], small["b_gate_fox"]])
    (dw_out,) = _mm("dw_out", [(merged, dx2)], "tn", d, d, s, epilogue=ident, out_dtypes=(F32,))
    dw_out = dw_out.reshape(N_CHIPS, d // N_CHIPS, d)
    dw_pd = _dw_col_pieces("dw_pd", y_dil, du_d, hd, d, s)
    dw_pf = _dw_col_pieces("dw_pf", y_fox, du_f, hd, d, s)
    (dy_dil,) = _mm("dy_dil", [(du_d, w["w_proj_dil"])], "nt", s, hd, d, epilogue=ident, out_dtypes=(BF,))
    (dy_fox,) = _mm("dy_fox", [(du_f, w["w_proj_fox"])], "nt", s, hd, d, epilogue=ident, out_dtypes=(BF,))

    row = lambda t: t.reshape(hh, s // blk, 1, blk)
    delta_d = _att_delta("dil_delta", dy_dil, y_dil, hh)
    dq_d, dk_d, dv_d = _flash_bwd("dil_bwd", qkv, 0, hh, 2 * hh, dy_dil, lse_d, row(delta_d), hh, fox=False,
                                  tab_t=tab_t)
    delta_f = _att_delta("fox_delta", dy_fox, y_fox, hh)
    dq_f, dk_f, dv_f, dc_k, dc_q = _flash_bwd("fox_bwd", qkv, 3 * hh, 4 * hh, 5 * hh, dy_fox, lse_f, row(delta_f), hh,
                                              fox=True, qx=qx, kx=kx)
    dc = dc_k.reshape(hh, s) + dc_q.reshape(hh, s)
    dc_pad = jnp.pad(dc.T, ((0, 0), (0, LANES - hh)))
    df, db_forget = _forget_bwd("forget_bwd", proj, f_col, bias_f, dc_pad)
    dgate = jnp.concatenate([dg_d, dg_f], axis=1)
    dproj = _dproj_assemble("dproj", [dq_d, dk_d, dv_d, dq_f, dk_f, dv_f], dgate, df, cos_t, sin_t, hh, d)
    (dhm,) = _mm("dhm", [(dproj, w["w_in"])], "nt", s, d, n_proj, epilogue=ident, out_dtypes=(F32,))
    (dw_in,) = _mm("dw_in", [(hm, dproj)], "tn", d, n_proj, s, epilogue=ident, out_dtypes=(F32,))
    dw_in = _col_pieces(_unpack_dw_in(dw_in, hd, d))
    dx1, dg_mix = _rms_bwd("mix_rmsb", x1, small["mix_norm"], dhm, dx2)

    dx0, dg_ffn1, dw_g1, dw_u1, dw_d1 = _ffn_bwd("ffn1", x, small["ffn1_norm"], w["ffn1_w_gate"], w["ffn1_w_up"],
                                                 w["ffn1_w_down"], saved1, dx1)

    grads = {"ffn1_w_gate": dw_g1, "ffn1_w_up": dw_u1, "ffn1_w_down": dw_d1, "w_in": dw_in, "w_proj_dil": dw_pd,
             "w_proj_fox": dw_pf, "w_out": dw_out, "ffn2_w_gate": dw_g2, "ffn2_w_up": dw_u2, "ffn2_w_down": dw_d2}
    partials = {"ffn1_norm": dg_ffn1, "mix_norm": dg_mix, "ffn2_norm": dg_ffn2, "final_norm": dg_final,
                "b_gate_dil": dbg_d.reshape(-1, d), "b_gate_fox": dbg_f.reshape(-1, d), "b_forget": db_forget, "sq": sq}
    return dx0, grads, partials


def _coords():
    return lax.axis_index("x"), lax.axis_index("y"), lax.axis_index("c")


def _other_chips(x, y):
    return [(1 - x, y), (x, 1 - y), (1 - x, 1 - y)]


ANY_SPEC = pl.BlockSpec(memory_space=pl.ANY)


def _gather_weights(name, shards):
    nw = len(shards)

    def body(*refs):
        srcs, outs = refs[:nw], refs[nw:2 * nw]
        send_sems, recv_sems = refs[2 * nw:]
        x, y, c = _coords()
        sibling = (x, y, 1 - c)
        chips = _other_chips(x, y)

        def slot(w, px, py, pc):
            half = shards[w].shape[0] // 2
            return outs[w].at[2 * px + py, pl.ds(pc * half, half), :]

        def copy(w, k, src_ref, dst_ref, to):
            return pltpu.make_async_remote_copy(src_ref=src_ref, dst_ref=dst_ref, send_sem=send_sems.at[6 * w + k],
                                                recv_sem=recv_sems.at[6 * w + k], device_id=to, device_id_type=MESH)

        first, passed = [], []
        for w in range(nw):
            half = shards[w].shape[0] // 2
            for j, chip in enumerate(chips):
                first.append(copy(w, j, srcs[w].at[pl.ds(c * half, half), :], slot(w, x, y, c), (*chip, c)))
                first[-1].start()
        for w in range(nw):
            for j, chip in enumerate(chips):
                copy(w, j, slot(w, *chip, c), slot(w, *chip, c), (*chip, c)).wait_recv()
                passed.append(copy(w, 3 + j, slot(w, *chip, c), slot(w, *chip, c), sibling))
                passed[-1].start()
        for w in range(nw):
            for j, chip in enumerate(chips):
                copy(w, 3 + j, slot(w, *chip, 1 - c), slot(w, *chip, 1 - c), sibling).wait_recv()
        for cp in first + passed:
            cp.wait_send()

    return pl.pallas_call(
        body, in_specs=[ANY_SPEC] * nw, out_specs=[ANY_SPEC] * nw,
        out_shape=[jax.ShapeDtypeStruct((N_CHIPS, *t.shape), t.dtype) for t in shards],
        scratch_shapes=[pltpu.SemaphoreType.DMA((6 * nw,)), pltpu.SemaphoreType.DMA((6 * nw,))],
        name=name)(*shards)


def _rs_swap(name, grads):
    nw = len(grads)

    def body(*refs):
        srcs, outs = refs[:nw], refs[nw:2 * nw]
        send_sems, recv_sems = refs[2 * nw:]
        x, y, c = _coords()
        copies = []
        for w in range(nw):
            half = grads[w].shape[1] // 2
            for p in range(N_CHIPS):
                k = N_CHIPS * w + p
                copies.append(pltpu.make_async_remote_copy(
                    src_ref=srcs[w].at[p, pl.ds((1 - c) * half, half), :], dst_ref=outs[w].at[p],
                    send_sem=send_sems.at[k], recv_sem=recv_sems.at[k], device_id=(x, y, 1 - c), device_id_type=MESH))
                copies[-1].start()
        for cp in copies:
            cp.wait()

    return pl.pallas_call(
        body, in_specs=[ANY_SPEC] * nw, out_specs=[ANY_SPEC] * nw,
        out_shape=[jax.ShapeDtypeStruct((N_CHIPS, t.shape[1] // 2, t.shape[2]), t.dtype) for t in grads],
        scratch_shapes=[pltpu.SemaphoreType.DMA((N_CHIPS * nw,)), pltpu.SemaphoreType.DMA((N_CHIPS * nw,))],
        name=name)(*grads)


def _rs_add(name, ids, g, other):
    n, rows, cols = g.shape
    half = rows // 2
    tr = _pick(half, 256, 16)
    nb = half // tr

    def body(ids_ref, g_ref, o_ref, out_ref):
        out_ref[...] = (g_ref[...] + o_ref[...]).astype(BF)

    grid_spec = pltpu.PrefetchScalarGridSpec(
        num_scalar_prefetch=1, grid=(n, nb),
        in_specs=[pl.BlockSpec((None, tr, cols), lambda p, i, ids_ref: (p, ids_ref[1] * nb + i, 0)),
                  pl.BlockSpec((None, tr, cols), lambda p, i, ids_ref: (p, i, 0))],
        out_specs=pl.BlockSpec((None, tr, cols), lambda p, i, ids_ref: (p, i, 0)))
    return pl.pallas_call(body, grid_spec=grid_spec, out_shape=jax.ShapeDtypeStruct((n, half, cols), BF),
                          compiler_params=_params(("parallel", "parallel")), name=name)(ids, g, other)


def _rs_scatter(name, sums):
    nw = len(sums)

    def body(*refs):
        srcs, outs = refs[:nw], refs[nw:2 * nw]
        send_sems, recv_sems = refs[2 * nw:]
        x, y, c = _coords()
        chips = _other_chips(x, y)
        sends = []
        for w in range(nw):
            for k, (px, py) in enumerate(chips):
                sends.append(pltpu.make_async_remote_copy(
                    src_ref=srcs[w].at[2 * px + py], dst_ref=outs[w].at[k], send_sem=send_sems.at[3 * w + k],
                    recv_sem=recv_sems.at[3 * w + k], device_id=(px, py, c), device_id_type=MESH))
                sends[-1].start()
        for w in range(nw):
            for k, (px, py) in enumerate(chips):
                pltpu.make_async_remote_copy(
                    src_ref=srcs[w].at[2 * px + py], dst_ref=outs[w].at[k], send_sem=send_sems.at[3 * w + k],
                    recv_sem=recv_sems.at[3 * w + k], device_id=(px, py, c), device_id_type=MESH).wait_recv()
        for cp in sends:
            cp.wait_send()

    return pl.pallas_call(
        body, in_specs=[ANY_SPEC] * nw, out_specs=[ANY_SPEC] * nw,
        out_shape=[jax.ShapeDtypeStruct((3, *t.shape[1:]), t.dtype) for t in sums],
        scratch_shapes=[pltpu.SemaphoreType.DMA((3 * nw,)), pltpu.SemaphoreType.DMA((3 * nw,))],
        name=name)(*sums)


def _rs_sum(name, ids, own, got):
    n, half, cols = own.shape
    tr = _pick(half, 256, 16)
    nb = half // tr

    def body(ids_ref, own_ref, got_ref, out_ref):
        t = own_ref[...].astype(F32)
        for k in range(3):
            t = t + got_ref[k].astype(F32)
        out_ref[...] = t

    grid_spec = pltpu.PrefetchScalarGridSpec(
        num_scalar_prefetch=1, grid=(nb,),
        in_specs=[pl.BlockSpec((None, tr, cols), lambda i, ids_ref: (ids_ref[0], i, 0)),
                  pl.BlockSpec((3, tr, cols), lambda i, ids_ref: (0, i, 0))],
        out_specs=pl.BlockSpec((tr, cols), lambda i, ids_ref: (ids_ref[1] * nb + i, 0)))
    return pl.pallas_call(body, grid_spec=grid_spec, out_shape=jax.ShapeDtypeStruct((2 * half, cols), F32),
                          compiler_params=_params(("parallel",)), name=name)(ids, own, got)


def _rs_join(name, totals):
    nw = len(totals)

    def body(*refs):
        bufs = refs[nw:2 * nw]
        send_sems, recv_sems = refs[2 * nw:]
        x, y, c = _coords()
        copies = []
        for w in range(nw):
            half = totals[w].shape[0] // 2
            copies.append(pltpu.make_async_remote_copy(
                src_ref=bufs[w].at[pl.ds(c * half, half), :], dst_ref=bufs[w].at[pl.ds(c * half, half), :],
                send_sem=send_sems.at[w], recv_sem=recv_sems.at[w], device_id=(x, y, 1 - c), device_id_type=MESH))
            copies[-1].start()
        for w in range(nw):
            half = totals[w].shape[0] // 2
            pltpu.make_async_remote_copy(
                src_ref=bufs[w].at[pl.ds(c * half, half), :], dst_ref=bufs[w].at[pl.ds((1 - c) * half, half), :],
                send_sem=send_sems.at[w], recv_sem=recv_sems.at[w], device_id=(x, y, 1 - c),
                device_id_type=MESH).wait_recv()
        for cp in copies:
            cp.wait_send()

    return pl.pallas_call(
        body, in_specs=[ANY_SPEC] * nw, out_specs=[ANY_SPEC] * nw,
        out_shape=[jax.ShapeDtypeStruct(t.shape, t.dtype) for t in totals],
        input_output_aliases={w: w for w in range(nw)},
        scratch_shapes=[pltpu.SemaphoreType.DMA((nw,)), pltpu.SemaphoreType.DMA((nw,))],
        name=name)(*totals)


def _gather_all(name, t):
    rows, cols = t.shape

    def body(src, out, send_sems, recv_sems, local_sem):
        x, y, c = _coords()
        me = 4 * x + 2 * y + c
        mine = pltpu.make_async_copy(src, out.at[me], local_sem)
        mine.start()
        peers = [(x ^ (k >> 2 & 1), y ^ (k >> 1 & 1), c ^ (k & 1)) for k in range(1, N_DEV)]
        sends = [pltpu.make_async_remote_copy(src_ref=src, dst_ref=out.at[me], send_sem=send_sems.at[k],
                                              recv_sem=recv_sems.at[k], device_id=peer, device_id_type=MESH)
                 for k, peer in enumerate(peers)]
        for cp in sends:
            cp.start()
        for k, (px, py, pc) in enumerate(peers):
            pltpu.make_async_remote_copy(src_ref=src, dst_ref=out.at[4 * px + 2 * py + pc], send_sem=send_sems.at[k],
                                         recv_sem=recv_sems.at[k], device_id=(px, py, pc),
                                         device_id_type=MESH).wait_recv()
        for cp in sends:
            cp.wait_send()
        mine.wait()

    vmem = pl.BlockSpec(memory_space=pltpu.VMEM)
    return pl.pallas_call(
        body, in_specs=[vmem], out_specs=vmem, out_shape=jax.ShapeDtypeStruct((N_DEV, rows, cols), t.dtype),
        scratch_shapes=[pltpu.SemaphoreType.DMA((7,)), pltpu.SemaphoreType.DMA((7,)), pltpu.SemaphoreType.DMA],
        name=name)(t)


def _adamw_math(w, g, m, v):
    m = ADAM_B1 * m + (1.0 - ADAM_B1) * g
    v = ADAM_B2 * v + (1.0 - ADAM_B2) * (g * g)
    m_hat = m / (1.0 - ADAM_B1 ** ADAM_STEP)
    v_hat = v / (1.0 - ADAM_B2 ** ADAM_STEP)
    delta = -ADAM_LR * (m_hat / (jnp.sqrt(v_hat) + ADAM_EPS) + ADAM_WD * w)
    return delta, m, v


def _adamw(name, w, g, m, v):
    rows, cols = w.shape
    tr = _pick(rows, 256, 8)

    def body(w_ref, g_ref, m_ref, v_ref, d_out, m_out, v_out):
        d_out[...], m_out[...], v_out[...] = _adamw_math(w_ref[...], g_ref[...], m_ref[...], v_ref[...])

    blk = pl.BlockSpec((tr, cols), lambda i: (i, 0))
    shape = jax.ShapeDtypeStruct((rows, cols), F32)
    return pl.pallas_call(body, grid=(rows // tr,), in_specs=[blk] * 4, out_specs=[blk] * 3, out_shape=[shape] * 3,
                          compiler_params=_params(("parallel",)), name=name)(w, g, m, v)


def _small_reduce(name, parts, width):
    def body(*refs):
        out = refs[-1]
        out[...] = jnp.zeros_like(out)
        for k, r in enumerate(refs[:-1]):
            out[pl.ds(k, 1), :] = jnp.sum(r[...], axis=0, keepdims=True)

    vmem = pl.BlockSpec(memory_space=pltpu.VMEM)
    return pl.pallas_call(body, in_specs=[vmem] * len(parts), out_specs=vmem,
                          out_shape=jax.ShapeDtypeStruct((8, width), F32), name=name)(*parts)


def _small_adamw(name, gathered, w, m, v, loss_row, loss_scale):
    def body(gt_ref, w_ref, m_ref, v_ref, g_out, d_out, m_out, v_out, loss_out):
        g = gt_ref[0]
        for k in range(1, N_DEV):
            g = g + gt_ref[k]
        g_out[...] = g
        row = lax.broadcasted_iota(jnp.int32, g.shape, 0)
        loss_out[...] = jnp.sum(jnp.where(row == loss_row, g, 0.0), keepdims=True) * loss_scale
        d_out[...], m_out[...], v_out[...] = _adamw_math(w_ref[...], g, m_ref[...], v_ref[...])

    vmem = pl.BlockSpec(memory_space=pltpu.VMEM)
    shape = jax.ShapeDtypeStruct(w.shape, F32)
    return pl.pallas_call(body, in_specs=[vmem] * 4, out_specs=[vmem] * 5,
                          out_shape=[shape] * 4 + [jax.ShapeDtypeStruct((1, 1), F32)], name=name)(gathered, w, m, v)


def _full_from_pieces(name, pieces):
    _, rows, cols = pieces.shape
    if name in ROW_SHARDED:
        return pieces.reshape(N_CHIPS * rows, cols)
    return pieces.transpose(1, 0, 2).reshape(rows, N_CHIPS * cols)


def _repack_w_in(w_in, hd, d):
    hh = hd // HEAD_DIM
    qkv, f, gates = w_in[:, :6 * hd], w_in[:, 6 * hd:6 * hd + hh], w_in[:, 6 * hd + hh:]
    return jnp.concatenate([qkv, gates, jnp.pad(f, ((0, 0), (0, LANES - hh)))], axis=1)


def _unpack_dw_in(dw, hd, d):
    hh = hd // HEAD_DIM
    return jnp.concatenate([dw[:, :6 * hd], dw[:, 6 * hd + 2 * d:6 * hd + 2 * d + hh], dw[:, 6 * hd:6 * hd + 2 * d]],
                           axis=1)


def _small_pack(vals, width):
    rows = []
    for name in SMALL:
        t = vals[name].reshape(1, -1)
        rows.append(jnp.pad(t, ((0, 0), (0, width - t.shape[1]))))
    rows.append(jnp.zeros((8 - len(SMALL), width), F32))
    return jnp.concatenate(rows, axis=0)


def kernel(x, ffn1_norm, ffn1_w_gate, ffn1_w_up, ffn1_w_down, mix_norm, w_in, b_forget, b_gate_dil, b_gate_fox, w_proj_dil, w_proj_fox, w_out, ffn2_norm, ffn2_w_gate, ffn2_w_up, ffn2_w_down, final_norm, loss_target, m_ffn1_norm, m_ffn1_w_gate, m_ffn1_w_up, m_ffn1_w_down, m_mix_norm, m_w_in, m_b_forget, m_b_gate_dil, m_b_gate_fox, m_w_proj_dil, m_w_proj_fox, m_w_out, m_ffn2_norm, m_ffn2_w_gate, m_ffn2_w_up, m_ffn2_w_down, m_final_norm, v_ffn1_norm, v_ffn1_w_gate, v_ffn1_w_up, v_ffn1_w_down, v_mix_norm, v_w_in, v_b_forget, v_b_gate_dil, v_b_gate_fox, v_w_proj_dil, v_w_proj_fox, v_w_out, v_ffn2_norm, v_ffn2_w_gate, v_ffn2_w_up, v_ffn2_w_down, v_final_norm):
    given = dict(locals())
    wts = {n: given[n] for n in WEIGHTS}
    mom_m = {n: given["m_" + n] for n in WEIGHTS}
    mom_v = {n: given["v_" + n] for n in WEIGHTS}
    s, d = x.shape[1], x.shape[2]
    hd = w_proj_dil.shape[1]

    shards = [wts[n][0].astype(BF) for n in SHARDED]
    gathered = _gather_weights("gather_weights", shards)
    chip = 2 * lax.axis_index("x") + lax.axis_index("y")
    gathered = [lax.dynamic_update_index_in_dim(t, own, chip, 0) for t, own in zip(gathered, shards)]
    full = {n: _full_from_pieces(n, t) for n, t in zip(SHARDED, gathered)}
    full["w_in"] = _repack_w_in(full["w_in"], hd, d)
    small = {n: wts[n] for n in SMALL}

    grad_x, grads, partials = _local_step(x[0], loss_target[0], full, small)

    ids = jnp.stack([2 * lax.axis_index("x") + lax.axis_index("y"), lax.axis_index("c")]).astype(jnp.int32)
    pieces = [grads[n] for n in SHARDED]
    from_sibling = _rs_swap("rs_swap", pieces)
    chip_sums = [_rs_add("rs_add_" + n, ids, g, o) for n, g, o in zip(SHARDED, pieces, from_sibling)]
    from_chips = _rs_scatter("rs_scatter", chip_sums)
    halves = [_rs_sum("rs_sum_" + n, ids, own, got) for n, own, got in zip(SHARDED, chip_sums, from_chips)]
    totals = _rs_join("rs_join", halves)

    out_g, out_d, out_m, out_v = {}, {}, {}, {}
    for n, g in zip(SHARDED, totals):
        dl, nm, nv = _adamw("adamw_" + n, wts[n][0], g, mom_m[n][0], mom_v[n][0])
        out_g[n], out_d[n], out_m[n], out_v[n] = g[None], dl[None], nm[None], nv[None]

    width = d
    part_rows = []
    for n in SMALL:
        t = partials[n]
        part_rows.append(jnp.pad(t, ((0, 0), (0, width - t.shape[1]))))
    part_rows.append(partials["sq"])
    local_small = _small_reduce("small_reduce", part_rows, width)
    gathered_small = _gather_all("small_gather", local_small)
    sg, sd_, sm, sv, loss = _small_adamw("small_adamw", gathered_small, _small_pack(wts, width),
                                         _small_pack(mom_m, width), _small_pack(mom_v, width), len(SMALL), 0.5 / d)
    for k, n in enumerate(SMALL):
        shp = wts[n].shape
        take = lambda t: t[k, :shp[-1]].reshape(shp)
        out_g[n], out_d[n], out_m[n], out_v[n] = take(sg), take(sd_), take(sm), take(sv)
    return (loss[0, 0], grad_x[None], *[out_g[n] for n in WEIGHTS], *[out_d[n] for n in WEIGHTS],
            *[out_m[n] for n in WEIGHTS], *[out_v[n] for n in WEIGHTS])
```

```python
import functools
import math

import numpy as np
import jax
import jax.numpy as jnp
from jax import lax
from jax.experimental import pallas as pl
from jax.experimental.pallas import tpu as pltpu

HEAD_DIM = 128
ROPE_DIM = HEAD_DIM // 4
ROPE_THETA = 500000.0
DIL_PATTERNS = ((128, 1), (512, 4), (2048, 16))
MAX_WINDOW = 2048
NORM_EPS = 1e-6
ADAM_LR = 0.001
ADAM_B1 = 0.9
ADAM_B2 = 0.999
ADAM_EPS = 1e-08
ADAM_WD = 0.01
ADAM_STEP = 10

BF = jnp.bfloat16
F32 = jnp.float32
NEG = -1e30
LANES = 128
ATT_BLOCK = 512
PREP_COLS = 512
ATT_HEADS = 2
VMEM_LIMIT = 56 * 1024 * 1024
MM_VMEM_BUDGET = 40 * 1024 * 1024
N_CHIPS = 4
N_DEV = 8
MESH = pl.DeviceIdType.MESH

SHARDED = ("ffn1_w_gate", "ffn1_w_up", "ffn1_w_down", "w_in", "w_proj_dil", "w_proj_fox", "w_out",
           "ffn2_w_gate", "ffn2_w_up", "ffn2_w_down")
ROW_SHARDED = ("ffn1_w_down", "w_out", "ffn2_w_down")
SMALL = ("ffn1_norm", "mix_norm", "b_forget", "b_gate_dil", "b_gate_fox", "ffn2_norm", "final_norm")
WEIGHTS = ("ffn1_norm", "ffn1_w_gate", "ffn1_w_up", "ffn1_w_down", "mix_norm", "w_in", "b_forget",
           "b_gate_dil", "b_gate_fox", "w_proj_dil", "w_proj_fox", "w_out", "ffn2_norm", "ffn2_w_gate",
           "ffn2_w_up", "ffn2_w_down", "final_norm")


def _pick(n, target, align):
    best = None
    for d in range(align, min(n, target) + 1, align):
        if n % d == 0:
            best = d
    return n if best is None else best


def _params(sem=None):
    return pltpu.CompilerParams(dimension_semantics=sem, vmem_limit_bytes=VMEM_LIMIT)


_DIMS = {"nn": (((1,), (0,)), ((), ())), "nt": (((1,), (1,)), ((), ())), "tn": (((0,), (0,)), ((), ()))}


def _mm(name, pairs, mode, m, n, k, *, epilogue, out_dtypes, extras=(), rows=(), n_colsum=0,
        sum_pairs=False, tm=1024, tn=1152, tk=2048, piece_layout=False):
    m_align = LANES if mode == "tn" else 8
    tm = _pick(m, tm, m_align)
    tn = n // N_CHIPS if piece_layout else _pick(n, tn, LANES)
    tk = _pick(k, tk, LANES)
    n_acc = 1 if sum_pairs else len(pairs)
    lhs = []
    for a, _ in pairs:
        if not any(a is t for t in lhs):
            lhs.append(a)
    lhs_of = [next(t for t in range(len(lhs)) if lhs[t] is a) for a, _ in pairs]
    n_mm = len(lhs) + len(pairs)
    n_in = n_mm + len(extras) + len(rows)
    n_out = len(out_dtypes) + n_colsum

    def vmem_bytes(tm_, tn_, tk_):
        tiles = sum(tm_ * tk_ * a.dtype.itemsize for a in lhs) + sum(tn_ * tk_ * b.dtype.itemsize for _, b in pairs)
        tiles += sum(tm_ * tn_ * arr.dtype.itemsize for arr, _ in extras)
        tiles += sum(tm_ * tn_ * jnp.dtype(dt).itemsize for dt in out_dtypes)
        return 2 * tiles + (n_acc + len(extras) + len(out_dtypes)) * tm_ * tn_ * 4

    while vmem_bytes(tm, tn, tk) > MM_VMEM_BUDGET:
        if tn > 512 and not piece_layout:
            tn = _pick(n, tn - LANES, LANES)
        elif tm > 512:
            tm = _pick(m, tm - m_align, m_align)
        elif tk > 512:
            tk = _pick(k, tk - LANES, LANES)
        elif tm > 256:
            tm = _pick(m, tm - m_align, m_align)
        else:
            break
    nk = k // tk

    def body(*refs):
        ins, outs, accs = refs[:n_in], refs[n_in:n_in + n_out], refs[n_in + n_out:]
        kk = pl.program_id(2)

        @pl.when(kk == 0)
        def _():
            for acc in accs:
                acc[...] = jnp.zeros_like(acc)

        a_tiles = [r[...].astype(BF) for r in ins[:len(lhs)]]
        for p in range(len(pairs)):
            b = ins[len(lhs) + p][...].astype(BF)
            accs[0 if sum_pairs else p][...] += lax.dot_general(a_tiles[lhs_of[p]], b, _DIMS[mode],
                                                                preferred_element_type=F32)

        @pl.when(kk == nk - 1)
        def _():
            ex = [r[...] for r in ins[n_mm:n_mm + len(extras)]]
            rw = [r[...] for r in ins[n_mm + len(extras):]]
            res = epilogue([acc[...] for acc in accs], ex, rw)
            for o, r in zip(outs, res):
                o[...] = r.astype(o.dtype)

    in_specs, args = [], []
    for a in lhs:
        if mode == "tn":
            in_specs.append(pl.BlockSpec((tk, tm), lambda i, j, kk: (kk, i)))
        else:
            in_specs.append(pl.BlockSpec((tm, tk), lambda i, j, kk: (i, kk)))
        args.append(a)
    for _, b in pairs:
        if mode == "nt":
            in_specs.append(pl.BlockSpec((tn, tk), lambda i, j, kk: (j, kk)))
        else:
            in_specs.append(pl.BlockSpec((tk, tn), lambda i, j, kk: (kk, j)))
        args.append(b)
    for arr, off in extras:
        assert off % tn == 0
        in_specs.append(pl.BlockSpec((tm, tn), functools.partial(lambda i, j, kk, o: (i, j + o), o=off // tn)))
        args.append(arr)
    for arr in rows:
        in_specs.append(pl.BlockSpec((1, tn), lambda i, j, kk: (0, j)))
        args.append(arr)
    if piece_layout:
        out_specs = [pl.BlockSpec((None, tm, tn), lambda i, j, kk: (j, i, 0)) for _ in out_dtypes]
        out_shape = [jax.ShapeDtypeStruct((n // tn, m, tn), d) for d in out_dtypes]
    else:
        out_specs = [pl.BlockSpec((tm, tn), lambda i, j, kk: (i, j)) for _ in out_dtypes]
        out_shape = [jax.ShapeDtypeStruct((m, n), d) for d in out_dtypes]
    for _ in range(n_colsum):
        out_specs.append(pl.BlockSpec((None, 1, tn), lambda i, j, kk: (i, 0, j)))
        out_shape.append(jax.ShapeDtypeStruct((m // tm, 1, n), F32))
    return pl.pallas_call(
        body, grid=(m // tm, n // tn, nk), in_specs=in_specs, out_specs=out_specs, out_shape=out_shape,
        scratch_shapes=[pltpu.VMEM((tm, tn), F32) for _ in range(n_acc)],
        compiler_params=_params(("parallel", "parallel", "arbitrary")), name=name)(*args)


def _col_pieces(full):
    rows, cols = full.shape
    return full.reshape(rows, N_CHIPS, cols // N_CHIPS).transpose(1, 0, 2)


def _dw_col_pieces(name, a, b, m, n, k):
    ident = lambda accs, ex, rw: (accs[0],)
    if (n // N_CHIPS) % LANES == 0:
        return _mm(name, [(a, b)], "tn", m, n, k, epilogue=ident, out_dtypes=(F32,), tm=512, piece_layout=True)[0]
    return _col_pieces(_mm(name, [(a, b)], "tn", m, n, k, epilogue=ident, out_dtypes=(F32,))[0])


def _sigmoid(z):
    return 0.5 * jnp.tanh(0.5 * z) + 0.5


def _row_tile(s):
    return _pick(s, 256, 8)


def _fold8(t):
    r, d = t.shape
    return jnp.sum(t.reshape(r // 8, 8, d), axis=0)


def _rms_fwd(name, x, g):
    s, d = x.shape
    tr = _row_tile(s)

    def body(x_ref, g_ref, h_ref):
        xf = x_ref[...]
        y = xf * lax.rsqrt(jnp.mean(xf * xf, axis=-1, keepdims=True) + NORM_EPS)
        h_ref[...] = (y * g_ref[...]).astype(BF)

    return pl.pallas_call(
        body, grid=(s // tr,),
        in_specs=[pl.BlockSpec((tr, d), lambda i: (i, 0)), pl.BlockSpec((1, d), lambda i: (0, 0))],
        out_specs=pl.BlockSpec((tr, d), lambda i: (i, 0)), out_shape=jax.ShapeDtypeStruct((s, d), BF),
        compiler_params=_params(("parallel",)), name=name)(x, g)


def _rms_bwd(name, x, g, dh, dres):
    s, d = x.shape
    tr = _row_tile(s)

    def body(x_ref, g_ref, dh_ref, dres_ref, dx_ref, dxb_ref, dg_ref):
        @pl.when(pl.program_id(0) == 0)
        def _():
            dg_ref[...] = jnp.zeros_like(dg_ref)

        xf = x_ref[...]
        rstd = lax.rsqrt(jnp.mean(xf * xf, axis=-1, keepdims=True) + NORM_EPS)
        xhat = xf * rstd
        dhf = dh_ref[...]
        dg_ref[...] += _fold8(dhf * xhat)
        dxh = dhf * g_ref[...]
        dx = dres_ref[...] + rstd * (dxh - xhat * jnp.mean(dxh * xhat, axis=-1, keepdims=True))
        dx_ref[...] = dx
        dxb_ref[...] = dx.astype(BF)

    blk = pl.BlockSpec((tr, d), lambda i: (i, 0))
    return pl.pallas_call(
        body, grid=(s // tr,),
        in_specs=[blk, pl.BlockSpec((1, d), lambda i: (0, 0)), blk, blk],
        out_specs=[blk, blk, pl.BlockSpec((8, d), lambda i: (0, 0))],
        out_shape=[jax.ShapeDtypeStruct((s, d), F32), jax.ShapeDtypeStruct((s, d), BF),
                   jax.ShapeDtypeStruct((8, d), F32)],
        compiler_params=_params(("arbitrary",)), name=name)(x, g, dh, dres)


def _final(name, x, g, tgt):
    s, d = x.shape
    tr = _row_tile(s)

    def body(x_ref, g_ref, t_ref, dx_ref, dxb_ref, dg_ref, sq_ref):
        @pl.when(pl.program_id(0) == 0)
        def _():
            dg_ref[...] = jnp.zeros_like(dg_ref)
            sq_ref[...] = jnp.zeros_like(sq_ref)

        xf = x_ref[...]
        rstd = lax.rsqrt(jnp.mean(xf * xf, axis=-1, keepdims=True) + NORM_EPS)
        xhat = xf * rstd
        gf = g_ref[...]
        err = xhat * gf - t_ref[...]
        sq_ref[...] += _fold8(err * err)
        dy = err * (1.0 / d)
        dg_ref[...] += _fold8(dy * xhat)
        dxh = dy * gf
        dx = rstd * (dxh - xhat * jnp.mean(dxh * xhat, axis=-1, keepdims=True))
        dx_ref[...] = dx
        dxb_ref[...] = dx.astype(BF)

    blk = pl.BlockSpec((tr, d), lambda i: (i, 0))
    acc = pl.BlockSpec((8, d), lambda i: (0, 0))
    return pl.pallas_call(
        body, grid=(s // tr,), in_specs=[blk, pl.BlockSpec((1, d), lambda i: (0, 0)), blk],
        out_specs=[blk, blk, acc, acc],
        out_shape=[jax.ShapeDtypeStruct((s, d), F32), jax.ShapeDtypeStruct((s, d), BF),
                   jax.ShapeDtypeStruct((8, d), F32), jax.ShapeDtypeStruct((8, d), F32)],
        compiler_params=_params(("arbitrary",)), name=name)(x, g, tgt)


def _ffn_fwd(tag, x, g, w_gate, w_up, w_down):
    s, d = x.shape
    f = w_gate.shape[1]
    h = _rms_fwd(tag + "_rms", x, g)

    def up_epi(accs, ex, rw):
        a, b = accs
        return a, b, a * _sigmoid(a) * b

    a, b, act = _mm(tag + "_up", [(h, w_gate), (h, w_up)], "nn", s, f, d, epilogue=up_epi,
                    out_dtypes=(BF, BF, BF))

    def down_epi(accs, ex, rw):
        return (ex[0] + 0.5 * accs[0],)

    (y,) = _mm(tag + "_down", [(act, w_down)], "nn", s, d, f, epilogue=down_epi, out_dtypes=(F32,),
               extras=[(x, 0)])
    return y, (h, a, b, act)


def _ffn_bwd(tag, x, g, w_gate, w_up, w_down, saved, dy, dy_bf):
    s, d = x.shape
    f = w_gate.shape[1]
    h, a, b, act = saved

    def act_epi(accs, ex, rw):
        dact = 0.5 * accs[0]
        av, bv = ex[0].astype(F32), ex[1].astype(F32)
        sg = _sigmoid(av)
        return dact * bv * (sg * (1.0 + av * (1.0 - sg))), dact * (av * sg)

    da, db = _mm(tag + "_dact", [(dy_bf, w_down)], "nt", s, f, d, epilogue=act_epi, out_dtypes=(BF, BF),
                 extras=[(a, 0), (b, 0)])
    ident = lambda accs, ex, rw: (accs[0],)
    (dw_down,) = _mm(tag + "_dwd", [(act, dy_bf)], "tn", f, d, s, epilogue=lambda accs, ex, rw: (0.5 * accs[0],),
                     out_dtypes=(F32,))
    dw_down = dw_down.reshape(N_CHIPS, f // N_CHIPS, d)
    dw_gate = _dw_col_pieces(tag + "_dwg", h, da, d, f, s)
    dw_up = _dw_col_pieces(tag + "_dwu", h, db, d, f, s)
    (dh,) = _mm(tag + "_dh", [(da, w_gate), (db, w_up)], "nt", s, d, f, epilogue=ident, out_dtypes=(F32,),
                sum_pairs=True)
    dx, dx_bf, dg = _rms_bwd(tag + "_rmsb", x, g, dh, dy)
    return dx, dx_bf, dg, dw_gate, dw_up, dw_down


def _rope_tables(s):
    half = ROPE_DIM // 2
    pos = jnp.arange(s, dtype=F32)
    inv_freq = ROPE_THETA ** (-jnp.arange(0, ROPE_DIM, 2, dtype=F32) / ROPE_DIM)
    ang = pos[:, None] * inv_freq[None, :]
    cos, sin = jnp.cos(ang), jnp.sin(ang)
    rest = HEAD_DIM - ROPE_DIM
    cos_t = jnp.concatenate([cos, cos, jnp.ones((s, rest), F32)], axis=-1)
    sin_t = jnp.concatenate([-sin, sin, jnp.zeros((s, rest), F32)], axis=-1)
    return cos_t, sin_t


def _swap_halves(t):
    lane = lax.broadcasted_iota(jnp.int32, t.shape, 1) & (HEAD_DIM - 1)
    half = ROPE_DIM // 2
    return jnp.where(lane < half, pltpu.roll(t, t.shape[1] - half, 1), pltpu.roll(t, half, 1))


def _dil_bias(blk):
    n_delta = MAX_WINDOW // blk + 1
    delta = jnp.arange(n_delta, dtype=jnp.int32)[:, None, None]
    r = jnp.arange(blk, dtype=jnp.int32)[None, None, :]
    c = jnp.arange(blk, dtype=jnp.int32)[None, :, None]
    o = delta * blk + r - c
    mult = jnp.zeros(o.shape, F32)
    for w, dd in DIL_PATTERNS:
        mult = mult + ((o >= 0) & (o <= w) & (o % dd == 0)).astype(F32)
    return jnp.where(mult > 0, jnp.log(jnp.maximum(mult, 1.0)), NEG)


def _att_block(s):
    return _pick(s, ATT_BLOCK, LANES)


def _fox_aug(name, c_pad, n_heads):
    s = c_pad.shape[0]
    tr = _pick(s, 1024, 16)

    def body(c_ref, qx_ref, kx_ref):
        h = pl.program_id(1)
        lane = lax.broadcasted_iota(jnp.int32, (tr, LANES), 1)
        ch = jnp.sum(jnp.where(lane == h, c_ref[...], 0.0), axis=1, keepdims=True)
        hi, mid, lo = (t.astype(F32) for t in _split3(ch))
        zero = jnp.zeros((tr, LANES), F32)
        is_hi = jnp.logical_or(lane == 0, lane == 3)
        is_mid = jnp.logical_or(lane == 1, lane == 4)
        parts = jnp.where(is_hi, hi, jnp.where(is_mid, mid, lo))
        qx_ref[...] = jnp.where(lane < 3, 1.0, jnp.where(lane < 6, parts, zero)).astype(BF)
        kx_ref[...] = jnp.where(lane < 3, -parts, jnp.where(lane < 6, 1.0, zero)).astype(BF)

    spec = pl.BlockSpec((tr, LANES), lambda i, h: (i, h))
    shape = jax.ShapeDtypeStruct((s, n_heads * LANES), BF)
    return pl.pallas_call(
        body, grid=(s // tr, n_heads), in_specs=[pl.BlockSpec((tr, LANES), lambda i, h: (i, 0))],
        out_specs=[spec, spec], out_shape=[shape, shape],
        compiler_params=_params(("parallel", "arbitrary")), name=name)(c_pad)


def _causal_mask(st):
    kpos = lax.broadcasted_iota(jnp.int32, st.shape, 0)
    qpos = lax.broadcasted_iota(jnp.int32, st.shape, 1)
    return jnp.where(kpos <= qpos, st, NEG)


def _flash_fwd(name, qkv, q_off, k_off, v_off, n_heads, *, fox, tab_t=None, qx=None, kx=None):
    s = qkv.shape[0]
    blk = _att_block(s)
    nq = s // blk
    n_delta = MAX_WINDOW // blk + 1
    grp = ATT_HEADS
    assert n_heads % grp == 0 and q_off % grp == 0 and k_off % grp == 0 and v_off % grp == 0
    wide = grp * HEAD_DIM

    def body(*refs):
        if fox:
            q_ref, k_ref, v_ref, qx_ref, kx_ref, o_ref, lse_ref, acc, m_s, l_s = refs
        else:
            q_ref, k_ref, v_ref, tab_ref, o_ref, lse_ref, acc, m_s, l_s = refs
        i = pl.program_id(1)
        acc[...] = jnp.zeros_like(acc)
        m_s[...] = jnp.full_like(m_s, NEG)
        l_s[...] = jnp.zeros_like(l_s)

        def step(j, diagonal):
            ks = pl.ds(pl.multiple_of(j * blk, blk), blk)
            for g in range(grp):
                cols = slice(g * HEAD_DIM, (g + 1) * HEAD_DIM)
                st = lax.dot_general(k_ref[ks, cols], q_ref[:, cols], _DIMS["nt"], preferred_element_type=F32)
                if fox:
                    st = st + lax.dot_general(kx_ref[ks, cols], qx_ref[:, cols], _DIMS["nt"],
                                              preferred_element_type=F32)
                    if diagonal:
                        st = _causal_mask(st)
                else:
                    st = st + tab_ref[i - j]
                m_prev = m_s[g]
                m_new = jnp.maximum(m_prev, jnp.max(st, axis=0, keepdims=True))
                alpha = jnp.exp(m_prev - m_new)
                p = jnp.exp(st - m_new)
                l_s[g] = alpha * l_s[g] + jnp.sum(p, axis=0, keepdims=True)
                acc[g] = alpha * acc[g] + lax.dot_general(v_ref[ks, cols], p.astype(BF), _DIMS["tn"],
                                                          preferred_element_type=F32)
                m_s[g] = m_new

        def loop_step(j, carry):
            step(j, False)
            return carry

        if fox:
            lax.fori_loop(0, i, loop_step, 0)
            step(i, True)
        else:
            lax.fori_loop(jnp.maximum(i - (n_delta - 1), 0), i + 1, loop_step, 0)
        for g in range(grp):
            o_ref[:, g * HEAD_DIM:(g + 1) * HEAD_DIM] = (acc[g] / l_s[g]).T.astype(o_ref.dtype)
            lse_ref[g] = m_s[g] + jnp.log(l_s[g])

    off = lambda o: functools.partial(lambda h, i, o: (0, o + h), o=o // grp)
    in_specs = [pl.BlockSpec((blk, wide), functools.partial(lambda h, i, o: (i, o + h), o=q_off // grp)),
                pl.BlockSpec((s, wide), off(k_off)), pl.BlockSpec((s, wide), off(v_off))]
    args = [qkv, qkv, qkv]
    if fox:
        in_specs += [pl.BlockSpec((blk, wide), lambda h, i: (i, h)), pl.BlockSpec((s, wide), lambda h, i: (0, h))]
        args += [qx, kx]
    else:
        in_specs.append(pl.BlockSpec((n_delta, blk, blk), lambda h, i: (0, 0, 0)))
        args.append(tab_t)
    return pl.pallas_call(
        body, grid=(n_heads // grp, nq), in_specs=in_specs,
        out_specs=[pl.BlockSpec((blk, wide), lambda h, i: (i, h)),
                   pl.BlockSpec((grp, None, 1, blk), lambda h, i: (h, i, 0, 0))],
        out_shape=[jax.ShapeDtypeStruct((s, n_heads * HEAD_DIM), BF),
                   jax.ShapeDtypeStruct((n_heads, nq, 1, blk), F32)],
        scratch_shapes=[pltpu.VMEM((grp, HEAD_DIM, blk), F32), pltpu.VMEM((grp, 1, blk), F32),
                        pltpu.VMEM((grp, 1, blk), F32)],
        compiler_params=_params(("parallel", "parallel")), name=name)(*args)


def _att_delta(name, do, o, n_heads):
    s = do.shape[0]
    blk = _pick(s, 1024, 16)

    def body(do_ref, o_ref, d_ref):
        d_ref[...] = jnp.sum(do_ref[...].astype(F32) * o_ref[...].astype(F32), axis=-1, keepdims=True)

    spec = pl.BlockSpec((blk, HEAD_DIM), lambda h, i: (i, h))
    return pl.pallas_call(
        body, grid=(n_heads, s // blk), in_specs=[spec, spec],
        out_specs=pl.BlockSpec((None, blk, 1), lambda h, i: (h, i, 0)),
        out_shape=jax.ShapeDtypeStruct((n_heads, s, 1), F32),
        compiler_params=_params(("parallel", "parallel")), name=name)(do, o)


def _flash_bwd(name, qkv, q_off, k_off, v_off, do, lse_row, delta_row, n_heads, *, fox, tab_t=None,
               qx=None, kx=None):
    s = qkv.shape[0]
    blk = _att_block(s)
    nq = s // blk
    n_delta = MAX_WINDOW // blk + 1
    grp = ATT_HEADS
    assert n_heads % grp == 0 and q_off % grp == 0 and k_off % grp == 0 and v_off % grp == 0
    wide = grp * HEAD_DIM

    def body(*refs):
        if fox:
            (q_ref, do_ref, k_ref, v_ref, lse_ref, dl_ref, qx_ref, kx_ref,
             dq_ref, dk_ref, dv_ref, dc_ref, dcq_ref, dk_acc, dv_acc, dc_acc) = refs
        else:
            (q_ref, do_ref, k_ref, v_ref, lse_ref, dl_ref, tab_ref,
             dq_ref, dk_ref, dv_ref, dk_acc, dv_acc) = refs
        j = pl.program_id(1)

        @pl.when(j == 0)
        def _():
            dq_ref[...] = jnp.zeros_like(dq_ref)
            if fox:
                dcq_ref[...] = jnp.zeros_like(dcq_ref)

        dk_acc[...] = jnp.zeros_like(dk_acc)
        dv_acc[...] = jnp.zeros_like(dv_acc)
        if fox:
            dc_acc[...] = jnp.zeros_like(dc_acc)

        def step(i, diagonal):
            qs = pl.ds(pl.multiple_of(i * blk, blk), blk)
            for g in range(grp):
                cols = slice(g * HEAD_DIM, (g + 1) * HEAD_DIM)
                kb, vb = k_ref[:, cols], v_ref[:, cols]
                qb, dob = q_ref[qs, cols], do_ref[qs, cols]
                st = lax.dot_general(kb, qb, _DIMS["nt"], preferred_element_type=F32)
                if fox:
                    st = st + lax.dot_general(kx_ref[:, cols], qx_ref[qs, cols], _DIMS["nt"],
                                              preferred_element_type=F32)
                    if diagonal:
                        st = _causal_mask(st)
                else:
                    st = st + tab_ref[i - j]
                pt = jnp.exp(st - lse_ref[g, i])
                dv_acc[:, cols] += jnp.dot(pt.astype(BF), dob, preferred_element_type=F32)
                dpt = lax.dot_general(vb, dob, _DIMS["nt"], preferred_element_type=F32)
                dst = pt * (dpt - dl_ref[g, i])
                dsb = dst.astype(BF)
                dk_acc[:, cols] += jnp.dot(dsb, qb, preferred_element_type=F32)
                dq_ref[qs, cols] += lax.dot_general(dsb, kb, _DIMS["tn"], preferred_element_type=F32)
                if fox:
                    folded = dst[:, :LANES]
                    for part in range(1, blk // LANES):
                        folded = folded + dst[:, part * LANES:(part + 1) * LANES]
                    dc_acc[g] -= folded
                    dcq_ref[g, i] += jnp.sum(dst, axis=0, keepdims=True)

        def loop_step(i, carry):
            step(i, False)
            return carry

        if fox:
            step(j, True)
            lax.fori_loop(j + 1, nq, loop_step, 0)
        else:
            lax.fori_loop(j, jnp.minimum(nq, j + n_delta), loop_step, 0)
        dk_ref[...] = dk_acc[...]
        dv_ref[...] = dv_acc[...].astype(dv_ref.dtype)
        if fox:
            for g in range(grp):
                dc_ref[g] = jnp.sum(dc_acc[g], axis=-1, keepdims=True)

    full = lambda o: pl.BlockSpec((s, wide), functools.partial(lambda h, j, o: (0, o + h), o=o // grp))
    tile = lambda o: pl.BlockSpec((blk, wide), functools.partial(lambda h, j, o: (j, o + h), o=o // grp))
    per_q = pl.BlockSpec((grp, nq, 1, blk), lambda h, j: (h, 0, 0, 0))
    per_k = pl.BlockSpec((grp, blk, 1), lambda h, j: (h, j, 0))
    in_specs = [full(q_off), full(0), tile(k_off), tile(v_off), per_q, per_q]
    args = [qkv, do, qkv, qkv, lse_row, delta_row]
    out_specs = [full(0), tile(0), tile(0)]
    hd = n_heads * HEAD_DIM
    out_shape = [jax.ShapeDtypeStruct((s, hd), F32), jax.ShapeDtypeStruct((s, hd), F32),
                 jax.ShapeDtypeStruct((s, hd), BF)]
    scratch = [pltpu.VMEM((blk, wide), F32), pltpu.VMEM((blk, wide), F32)]
    if fox:
        in_specs += [full(0), tile(0)]
        args += [qx, kx]
        out_specs += [per_k, per_q]
        out_shape += [jax.ShapeDtypeStruct((n_heads, s, 1), F32), jax.ShapeDtypeStruct((n_heads, nq, 1, blk), F32)]
        scratch.append(pltpu.VMEM((grp, blk, LANES), F32))
    else:
        in_specs.append(pl.BlockSpec((n_delta, blk, blk), lambda h, j: (0, 0, 0)))
        args.append(tab_t)
    return pl.pallas_call(
        body, grid=(n_heads // grp, nq), in_specs=in_specs, out_specs=out_specs, out_shape=out_shape,
        scratch_shapes=scratch, compiler_params=_params(("parallel", "arbitrary")), name=name)(*args)


def _split3(t):
    hi = t.astype(BF)
    r1 = t - hi.astype(F32)
    mid = r1.astype(BF)
    lo = (r1 - mid.astype(F32)).astype(BF)
    return hi, mid, lo


def _tri_dot(tri, t):
    hi, mid, lo = _split3(t)
    return (jnp.dot(tri, hi, preferred_element_type=F32) + jnp.dot(tri, mid, preferred_element_type=F32)
            + jnp.dot(tri, lo, preferred_element_type=F32))


def _log_sigmoid(z):
    return jnp.minimum(z, 0.0) - jnp.log(1.0 + jnp.exp(-jnp.abs(z)))


def _forget_cumsum(name, proj, f_col, bias):
    s = proj.shape[0]
    blk = _att_block(s)

    def body(f_ref, b_ref, c_ref, carry):
        @pl.when(pl.program_id(0) == 0)
        def _():
            carry[...] = jnp.zeros_like(carry)

        lf = _log_sigmoid(f_ref[...] + b_ref[...])
        r = lax.broadcasted_iota(jnp.int32, (blk, blk), 0)
        c = lax.broadcasted_iota(jnp.int32, (blk, blk), 1)
        tri = (c <= r).astype(BF)
        c_ref[...] = _tri_dot(tri, lf) + carry[...]
        carry[...] = c_ref[pl.ds(blk - 1, 1), :]

    return pl.pallas_call(
        body, grid=(s // blk,),
        in_specs=[pl.BlockSpec((blk, LANES), lambda i: (i, f_col)), pl.BlockSpec((1, LANES), lambda i: (0, 0))],
        out_specs=pl.BlockSpec((blk, LANES), lambda i: (i, 0)), out_shape=jax.ShapeDtypeStruct((s, LANES), F32),
        scratch_shapes=[pltpu.VMEM((1, LANES), F32)],
        compiler_params=_params(("arbitrary",)), name=name)(proj, bias)


def _forget_bwd(name, proj, f_col, bias, dc):
    s = proj.shape[0]
    blk = _att_block(s)
    nb = s // blk

    def body(f_ref, b_ref, dc_ref, df_ref, db_ref, carry):
        @pl.when(pl.program_id(0) == 0)
        def _():
            carry[...] = jnp.zeros_like(carry)
            db_ref[...] = jnp.zeros_like(db_ref)

        r = lax.broadcasted_iota(jnp.int32, (blk, blk), 0)
        c = lax.broadcasted_iota(jnp.int32, (blk, blk), 1)
        tri = (c >= r).astype(BF)
        r = lax.broadcasted_iota(jnp.int32, (blk, LANES), 0)
        dlf = _tri_dot(tri, dc_ref[...]) + carry[...]
        carry[...] = jnp.sum(jnp.where(r == 0, dlf, 0.0), axis=0, keepdims=True)
        dz = dlf * _sigmoid(-(f_ref[...] + b_ref[...]))
        df_ref[...] = dz.astype(BF)
        db_ref[...] += _fold8(dz)

    rev = lambda i: (nb - 1 - i, 0)
    return pl.pallas_call(
        body, grid=(nb,),
        in_specs=[pl.BlockSpec((blk, LANES), lambda i: (nb - 1 - i, f_col)), pl.BlockSpec((1, LANES), lambda i: (0, 0)),
                  pl.BlockSpec((blk, LANES), rev)],
        out_specs=[pl.BlockSpec((blk, LANES), rev), pl.BlockSpec((8, LANES), lambda i: (0, 0))],
        out_shape=[jax.ShapeDtypeStruct((s, LANES), BF), jax.ShapeDtypeStruct((8, LANES), F32)],
        scratch_shapes=[pltpu.VMEM((1, LANES), F32)],
        compiler_params=_params(("arbitrary",)), name=name)(proj, bias, dc)


def _head_prep(name, proj, cos_t, sin_t, n_heads):
    s = proj.shape[0]
    tr = _pick(s, 512, 16)
    scale = HEAD_DIM ** -0.5
    hd = n_heads * HEAD_DIM
    cw = _pick(hd, PREP_COLS, HEAD_DIM)
    per_kind = hd // cw
    cos_w, sin_w = jnp.tile(cos_t, (1, cw // HEAD_DIM)), jnp.tile(sin_t, (1, cw // HEAD_DIM))

    def body(p_ref, cos_ref, sin_ref, o_ref):
        j = pl.program_id(1)
        t = p_ref[...]

        @pl.when(j < 2 * per_kind)
        def _():
            r = t * cos_ref[...] + _swap_halves(t) * sin_ref[...]
            o_ref[...] = jnp.where(j < per_kind, r * scale, r).astype(BF)

        @pl.when(j >= 2 * per_kind)
        def _():
            is_q = jnp.logical_and(j >= 3 * per_kind, j < 4 * per_kind)
            o_ref[...] = jnp.where(is_q, t * scale, t).astype(BF)

    tab = pl.BlockSpec((tr, cw), lambda i, j: (i, 0))
    blk = pl.BlockSpec((tr, cw), lambda i, j: (i, j))
    return pl.pallas_call(
        body, grid=(s // tr, 6 * per_kind), in_specs=[blk, tab, tab], out_specs=blk,
        out_shape=jax.ShapeDtypeStruct((s, 6 * hd), BF),
        compiler_params=_params(("parallel", "arbitrary")), name=name)(proj, cos_w, sin_w)


def _dproj_assemble(name, parts, gates, cos_t, sin_t, n_heads, d_model):
    s = cos_t.shape[0]
    tr = _pick(s, 512, 16)
    scale = HEAD_DIM ** -0.5
    hd = n_heads * HEAD_DIM
    cw = _pick(math.gcd(hd, d_model), PREP_COLS, HEAD_DIM)
    widths = [hd] * 6 + [d_model] * 2
    starts = [sum(widths[:t]) // cw for t in range(len(widths) + 1)]
    cos_w, sin_w = jnp.tile(cos_t, (1, cw // HEAD_DIM)), jnp.tile(sin_t, (1, cw // HEAD_DIM))

    def body(*refs):
        p_refs, cos_ref, sin_ref, o_ref = refs[:8], refs[8], refs[9], refs[10]
        j = pl.program_id(1)
        for kind in range(8):
            @pl.when(jnp.logical_and(j >= starts[kind], j < starts[kind + 1]))
            def _(kind=kind):
                t = p_refs[kind][...]
                if kind in (0, 1, 3):
                    t = t.astype(F32)
                if kind in (0, 3):
                    t = t * scale
                if kind in (0, 1):
                    t = t * cos_ref[...] - _swap_halves(t) * sin_ref[...]
                o_ref[...] = t.astype(BF)

    def part_spec(kind):
        return pl.BlockSpec((tr, cw), functools.partial(
            lambda i, j, kind: (i, jnp.clip(j - starts[kind], 0, widths[kind] // cw - 1)), kind=kind))

    tab = pl.BlockSpec((tr, cw), lambda i, j: (i, 0))
    return pl.pallas_call(
        body, grid=(s // tr, starts[-1]), in_specs=[part_spec(kind) for kind in range(8)] + [tab, tab],
        out_specs=pl.BlockSpec((tr, cw), lambda i, j: (i, j)),
        out_shape=jax.ShapeDtypeStruct((s, sum(widths)), BF),
        compiler_params=_params(("parallel", "arbitrary")), name=name)(*parts, *gates, cos_w, sin_w)


def _local_step(x, tgt, w, small):
    s, d = x.shape
    hh = w["w_proj_dil"].shape[0] // HEAD_DIM
    hd = hh * HEAD_DIM
    blk = _att_block(s)
    gate_off = 6 * hd
    f_col = 0
    n_proj = gate_off + 2 * d
    ident = lambda accs, ex, rw: (accs[0],)

    x1, saved1 = _ffn_fwd("ffn1", x, small["ffn1_norm"], w["ffn1_w_gate"], w["ffn1_w_up"], w["ffn1_w_down"])

    hm = _rms_fwd("mix_rms", x1, small["mix_norm"])
    (proj,) = _mm("proj", [(hm, w["w_in"])], "nn", s, n_proj, d, epilogue=ident, out_dtypes=(F32,))
    (f_logit,) = _mm("proj_f", [(hm, w["w_in_f"])], "nn", s, LANES, d, epilogue=ident, out_dtypes=(F32,))
    cos_t, sin_t = _rope_tables(s)
    qkv = _head_prep("head_prep", proj, cos_t, sin_t, hh)
    tab_t = _dil_bias(blk)
    y_dil, lse_d = _flash_fwd("dil_fwd", qkv, 0, hh, 2 * hh, hh, fox=False, tab_t=tab_t)
    bias_f = jnp.pad(small["b_forget"], ((0, 0), (0, LANES - hh)))
    c_pad = _forget_cumsum("forget_cumsum", f_logit, f_col, bias_f)
    qx, kx = _fox_aug("fox_aug", c_pad, hh)
    y_fox, lse_f = _flash_fwd("fox_fwd", qkv, 3 * hh, 4 * hh, 5 * hh, hh, fox=True, qx=qx, kx=kx)

    def merge_epi(accs, ex, rw):
        ud, uf = accs
        return ud, uf, _sigmoid(ex[0] + rw[0]) * ud + _sigmoid(ex[1] + rw[1]) * uf

    u_d, u_f, merged = _mm("merge", [(y_dil, w["w_proj_dil"]), (y_fox, w["w_proj_fox"])], "nn", s, d, hd,
                           epilogue=merge_epi, out_dtypes=(BF, BF, BF),
                           extras=[(proj, gate_off), (proj, gate_off + d)],
                           rows=[small["b_gate_dil"], small["b_gate_fox"]])
    (x2,) = _mm("mix_out", [(merged, w["w_out"])], "nn", s, d, d,
                epilogue=lambda accs, ex, rw: (ex[0] + accs[0],), out_dtypes=(F32,), extras=[(x1, 0)])

    x3, saved2 = _ffn_fwd("ffn2", x2, small["ffn2_norm"], w["ffn2_w_gate"], w["ffn2_w_up"], w["ffn2_w_down"])
    dx3, dx3_bf, dg_final, sq = _final("final", x3, small["final_norm"].reshape(1, d), tgt)

    dx2, dx2_bf, dg_ffn2, dw_g2, dw_u2, dw_d2 = _ffn_bwd("ffn2", x2, small["ffn2_norm"], w["ffn2_w_gate"],
                                                         w["ffn2_w_up"], w["ffn2_w_down"], saved2, dx3, dx3_bf)

    def dmerge_epi(accs, ex, rw):
        dm = accs[0]
        gd, gf, ud, uf = ex[0], ex[1], ex[2].astype(F32), ex[3].astype(F32)
        sd, sf = _sigmoid(gd + rw[0]), _sigmoid(gf + rw[1])
        dgd = dm * ud * (sd * (1.0 - sd))
        dgf = dm * uf * (sf * (1.0 - sf))
        return (dm * sd, dm * sf, dgd, dgf, jnp.sum(dgd, axis=0, keepdims=True), jnp.sum(dgf, axis=0, keepdims=True))

    du_d, du_f, dg_d, dg_f, dbg_d, dbg_f = _mm(
        "dmerge", [(dx2_bf, w["w_out"])], "nt", s, d, d, epilogue=dmerge_epi, out_dtypes=(BF, BF, BF, BF), n_colsum=2,
        extras=[(proj, gate_off), (proj, gate_off + d), (u_d, 0), (u_f, 0)],
        rows=[small["b_gate_dil"], small["b_gate_fox"]])
    (dw_out,) = _mm("dw_out", [(merged, dx2_bf)], "tn", d, d, s, epilogue=ident, out_dtypes=(F32,))
    dw_out = dw_out.reshape(N_CHIPS, d // N_CHIPS, d)
    dw_pd = _dw_col_pieces("dw_pd", y_dil, du_d, hd, d, s)
    dw_pf = _dw_col_pieces("dw_pf", y_fox, du_f, hd, d, s)
    (dy_dil,) = _mm("dy_dil", [(du_d, w["w_proj_dil"])], "nt", s, hd, d, epilogue=ident, out_dtypes=(BF,))
    (dy_fox,) = _mm("dy_fox", [(du_f, w["w_proj_fox"])], "nt", s, hd, d, epilogue=ident, out_dtypes=(BF,))

    row = lambda t: t.reshape(hh, s // blk, 1, blk)
    delta_d = _att_delta("dil_delta", dy_dil, y_dil, hh)
    dq_d, dk_d, dv_d = _flash_bwd("dil_bwd", qkv, 0, hh, 2 * hh, dy_dil, lse_d, row(delta_d), hh, fox=False,
                                  tab_t=tab_t)
    delta_f = _att_delta("fox_delta", dy_fox, y_fox, hh)
    dq_f, dk_f, dv_f, dc_k, dc_q = _flash_bwd("fox_bwd", qkv, 3 * hh, 4 * hh, 5 * hh, dy_fox, lse_f, row(delta_f), hh,
                                              fox=True, qx=qx, kx=kx)
    dc = dc_k.reshape(hh, s) + dc_q.reshape(hh, s)
    dc_pad = jnp.pad(dc.T, ((0, 0), (0, LANES - hh)))
    df, db_forget = _forget_bwd("forget_bwd", f_logit, f_col, bias_f, dc_pad)
    dproj = _dproj_assemble("dproj", [dq_d, dk_d, dv_d, dq_f, dk_f, dv_f], [dg_d, dg_f], cos_t, sin_t, hh, d)
    (dhm_f,) = _mm("dhm_f", [(df, w["w_in_f"])], "nt", s, d, LANES, epilogue=ident, out_dtypes=(F32,))
    (dhm,) = _mm("dhm", [(dproj, w["w_in"])], "nt", s, d, n_proj, epilogue=lambda accs, ex, rw: (accs[0] + ex[0],),
                 out_dtypes=(F32,), extras=[(dhm_f, 0)])
    (dw_in,) = _mm("dw_in", [(hm, dproj)], "tn", d, n_proj, s, epilogue=ident, out_dtypes=(F32,))
    (dw_in_f,) = _mm("dw_in_f", [(hm, df)], "tn", d, LANES, s, epilogue=ident, out_dtypes=(F32,))
    dw_in = _col_pieces(_unpack_dw_in(dw_in, dw_in_f, hd, d))
    dx1, dx1_bf, dg_mix = _rms_bwd("mix_rmsb", x1, small["mix_norm"], dhm, dx2)

    dx0, _, dg_ffn1, dw_g1, dw_u1, dw_d1 = _ffn_bwd("ffn1", x, small["ffn1_norm"], w["ffn1_w_gate"], w["ffn1_w_up"],
                                                    w["ffn1_w_down"], saved1, dx1, dx1_bf)

    grads = {"ffn1_w_gate": dw_g1, "ffn1_w_up": dw_u1, "ffn1_w_down": dw_d1, "w_in": dw_in, "w_proj_dil": dw_pd,
             "w_proj_fox": dw_pf, "w_out": dw_out, "ffn2_w_gate": dw_g2, "ffn2_w_up": dw_u2, "ffn2_w_down": dw_d2}
    partials = {"ffn1_norm": dg_ffn1, "mix_norm": dg_mix, "ffn2_norm": dg_ffn2, "final_norm": dg_final,
                "b_gate_dil": dbg_d.reshape(-1, d), "b_gate_fox": dbg_f.reshape(-1, d), "b_forget": db_forget, "sq": sq}
    return dx0, grads, partials


def _coords():
    return lax.axis_index("x"), lax.axis_index("y"), lax.axis_index("c")


def _other_chips(x, y):
    return [(1 - x, y), (x, 1 - y), (1 - x, 1 - y)]


ANY_SPEC = pl.BlockSpec(memory_space=pl.ANY)


def _gather_weights(name, shards):
    nw = len(shards)

    def body(*refs):
        srcs, outs = refs[:nw], refs[nw:2 * nw]
        send_sems, recv_sems = refs[2 * nw:]
        x, y, c = _coords()
        sibling = (x, y, 1 - c)
        chips = _other_chips(x, y)

        def slot(w, px, py, pc):
            half = shards[w].shape[0] // 2
            return outs[w].at[2 * px + py, pl.ds(pc * half, half), :]

        def copy(w, k, src_ref, dst_ref, to):
            return pltpu.make_async_remote_copy(src_ref=src_ref, dst_ref=dst_ref, send_sem=send_sems.at[6 * w + k],
                                                recv_sem=recv_sems.at[6 * w + k], device_id=to, device_id_type=MESH)

        first, passed = [], []
        for w in range(nw):
            half = shards[w].shape[0] // 2
            for j, chip in enumerate(chips):
                first.append(copy(w, j, srcs[w].at[pl.ds(c * half, half), :], slot(w, x, y, c), (*chip, c)))
                first[-1].start()
        for w in range(nw):
            for j, chip in enumerate(chips):
                copy(w, j, slot(w, *chip, c), slot(w, *chip, c), (*chip, c)).wait_recv()
                passed.append(copy(w, 3 + j, slot(w, *chip, c), slot(w, *chip, c), sibling))
                passed[-1].start()
        for w in range(nw):
            for j, chip in enumerate(chips):
                copy(w, 3 + j, slot(w, *chip, 1 - c), slot(w, *chip, 1 - c), sibling).wait_recv()
        for cp in first + passed:
            cp.wait_send()

    return pl.pallas_call(
        body, in_specs=[ANY_SPEC] * nw, out_specs=[ANY_SPEC] * nw,
        out_shape=[jax.ShapeDtypeStruct((N_CHIPS, *t.shape), t.dtype) for t in shards],
        scratch_shapes=[pltpu.SemaphoreType.DMA((6 * nw,)), pltpu.SemaphoreType.DMA((6 * nw,))],
        name=name)(*shards)


def _rs_swap(name, grads):
    nw = len(grads)

    def body(*refs):
        srcs, outs = refs[:nw], refs[nw:2 * nw]
        send_sems, recv_sems = refs[2 * nw:]
        x, y, c = _coords()
        copies = []
        for w in range(nw):
            half = grads[w].shape[1] // 2
            for p in range(N_CHIPS):
                k = N_CHIPS * w + p
                copies.append(pltpu.make_async_remote_copy(
                    src_ref=srcs[w].at[p, pl.ds((1 - c) * half, half), :], dst_ref=outs[w].at[p],
                    send_sem=send_sems.at[k], recv_sem=recv_sems.at[k], device_id=(x, y, 1 - c), device_id_type=MESH))
                copies[-1].start()
        for cp in copies:
            cp.wait()

    return pl.pallas_call(
        body, in_specs=[ANY_SPEC] * nw, out_specs=[ANY_SPEC] * nw,
        out_shape=[jax.ShapeDtypeStruct((N_CHIPS, t.shape[1] // 2, t.shape[2]), t.dtype) for t in grads],
        scratch_shapes=[pltpu.SemaphoreType.DMA((N_CHIPS * nw,)), pltpu.SemaphoreType.DMA((N_CHIPS * nw,))],
        name=name)(*grads)


def _rs_add(name, ids, g, other):
    n, rows, cols = g.shape
    half = rows // 2
    tr = _pick(half, 256, 16)
    nb = half // tr

    def body(ids_ref, g_ref, o_ref, out_ref):
        out_ref[...] = (g_ref[...] + o_ref[...]).astype(BF)

    grid_spec = pltpu.PrefetchScalarGridSpec(
        num_scalar_prefetch=1, grid=(n, nb),
        in_specs=[pl.BlockSpec((None, tr, cols), lambda p, i, ids_ref: (p, ids_ref[1] * nb + i, 0)),
                  pl.BlockSpec((None, tr, cols), lambda p, i, ids_ref: (p, i, 0))],
        out_specs=pl.BlockSpec((None, tr, cols), lambda p, i, ids_ref: (p, i, 0)))
    return pl.pallas_call(body, grid_spec=grid_spec, out_shape=jax.ShapeDtypeStruct((n, half, cols), BF),
                          compiler_params=_params(("parallel", "parallel")), name=name)(ids, g, other)


def _rs_scatter(name, sums):
    nw = len(sums)

    def body(*refs):
        srcs, outs = refs[:nw], refs[nw:2 * nw]
        send_sems, recv_sems = refs[2 * nw:]
        x, y, c = _coords()
        chips = _other_chips(x, y)
        sends = []
        for w in range(nw):
            for k, (px, py) in enumerate(chips):
                sends.append(pltpu.make_async_remote_copy(
                    src_ref=srcs[w].at[2 * px + py], dst_ref=outs[w].at[k], send_sem=send_sems.at[3 * w + k],
                    recv_sem=recv_sems.at[3 * w + k], device_id=(px, py, c), device_id_type=MESH))
                sends[-1].start()
        for w in range(nw):
            for k, (px, py) in enumerate(chips):
                pltpu.make_async_remote_copy(
                    src_ref=srcs[w].at[2 * px + py], dst_ref=outs[w].at[k], send_sem=send_sems.at[3 * w + k],
                    recv_sem=recv_sems.at[3 * w + k], device_id=(px, py, c), device_id_type=MESH).wait_recv()
        for cp in sends:
            cp.wait_send()

    return pl.pallas_call(
        body, in_specs=[ANY_SPEC] * nw, out_specs=[ANY_SPEC] * nw,
        out_shape=[jax.ShapeDtypeStruct((3, *t.shape[1:]), t.dtype) for t in sums],
        scratch_shapes=[pltpu.SemaphoreType.DMA((3 * nw,)), pltpu.SemaphoreType.DMA((3 * nw,))],
        name=name)(*sums)


def _rs_sum(name, ids, own, got):
    n, half, cols = own.shape
    tr = _pick(half, 256, 16)
    nb = half // tr

    def body(ids_ref, own_ref, got_ref, out_ref):
        t = own_ref[...].astype(F32)
        for k in range(3):
            t = t + got_ref[k].astype(F32)
        out_ref[...] = t

    grid_spec = pltpu.PrefetchScalarGridSpec(
        num_scalar_prefetch=1, grid=(nb,),
        in_specs=[pl.BlockSpec((None, tr, cols), lambda i, ids_ref: (ids_ref[0], i, 0)),
                  pl.BlockSpec((3, tr, cols), lambda i, ids_ref: (0, i, 0))],
        out_specs=pl.BlockSpec((tr, cols), lambda i, ids_ref: (ids_ref[1] * nb + i, 0)))
    return pl.pallas_call(body, grid_spec=grid_spec, out_shape=jax.ShapeDtypeStruct((2 * half, cols), F32),
                          compiler_params=_params(("parallel",)), name=name)(ids, own, got)


def _rs_join(name, totals):
    nw = len(totals)

    def body(*refs):
        bufs = refs[nw:2 * nw]
        send_sems, recv_sems = refs[2 * nw:]
        x, y, c = _coords()
        copies = []
        for w in range(nw):
            half = totals[w].shape[0] // 2
            copies.append(pltpu.make_async_remote_copy(
                src_ref=bufs[w].at[pl.ds(c * half, half), :], dst_ref=bufs[w].at[pl.ds(c * half, half), :],
                send_sem=send_sems.at[w], recv_sem=recv_sems.at[w], device_id=(x, y, 1 - c), device_id_type=MESH))
            copies[-1].start()
        for w in range(nw):
            half = totals[w].shape[0] // 2
            pltpu.make_async_remote_copy(
                src_ref=bufs[w].at[pl.ds(c * half, half), :], dst_ref=bufs[w].at[pl.ds((1 - c) * half, half), :],
                send_sem=send_sems.at[w], recv_sem=recv_sems.at[w], device_id=(x, y, 1 - c),
                device_id_type=MESH).wait_recv()
        for cp in copies:
            cp.wait_send()

    return pl.pallas_call(
        body, in_specs=[ANY_SPEC] * nw, out_specs=[ANY_SPEC] * nw,
        out_shape=[jax.ShapeDtypeStruct(t.shape, t.dtype) for t in totals],
        input_output_aliases={w: w for w in range(nw)},
        scratch_shapes=[pltpu.SemaphoreType.DMA((nw,)), pltpu.SemaphoreType.DMA((nw,))],
        name=name)(*totals)


def _gather_all(name, t):
    rows, cols = t.shape

    def body(src, out, send_sems, recv_sems, local_sem):
        x, y, c = _coords()
        me = 4 * x + 2 * y + c
        mine = pltpu.make_async_copy(src, out.at[me], local_sem)
        mine.start()
        peers = [(x ^ (k >> 2 & 1), y ^ (k >> 1 & 1), c ^ (k & 1)) for k in range(1, N_DEV)]
        sends = [pltpu.make_async_remote_copy(src_ref=src, dst_ref=out.at[me], send_sem=send_sems.at[k],
                                              recv_sem=recv_sems.at[k], device_id=peer, device_id_type=MESH)
                 for k, peer in enumerate(peers)]
        for cp in sends:
            cp.start()
        for k, (px, py, pc) in enumerate(peers):
            pltpu.make_async_remote_copy(src_ref=src, dst_ref=out.at[4 * px + 2 * py + pc], send_sem=send_sems.at[k],
                                         recv_sem=recv_sems.at[k], device_id=(px, py, pc),
                                         device_id_type=MESH).wait_recv()
        for cp in sends:
            cp.wait_send()
        mine.wait()

    vmem = pl.BlockSpec(memory_space=pltpu.VMEM)
    return pl.pallas_call(
        body, in_specs=[vmem], out_specs=vmem, out_shape=jax.ShapeDtypeStruct((N_DEV, rows, cols), t.dtype),
        scratch_shapes=[pltpu.SemaphoreType.DMA((7,)), pltpu.SemaphoreType.DMA((7,)), pltpu.SemaphoreType.DMA],
        name=name)(t)


def _adamw_math(w, g, m, v):
    m = ADAM_B1 * m + (1.0 - ADAM_B1) * g
    v = ADAM_B2 * v + (1.0 - ADAM_B2) * (g * g)
    m_hat = m / (1.0 - ADAM_B1 ** ADAM_STEP)
    v_hat = v / (1.0 - ADAM_B2 ** ADAM_STEP)
    delta = -ADAM_LR * (m_hat / (jnp.sqrt(v_hat) + ADAM_EPS) + ADAM_WD * w)
    return delta, m, v


def _adamw(name, w, g, m, v):
    rows, cols = w.shape
    tr = _pick(rows, 256, 8)

    def body(w_ref, g_ref, m_ref, v_ref, d_out, m_out, v_out):
        d_out[...], m_out[...], v_out[...] = _adamw_math(w_ref[...], g_ref[...], m_ref[...], v_ref[...])

    blk = pl.BlockSpec((tr, cols), lambda i: (i, 0))
    shape = jax.ShapeDtypeStruct((rows, cols), F32)
    return pl.pallas_call(body, grid=(rows // tr,), in_specs=[blk] * 4, out_specs=[blk] * 3, out_shape=[shape] * 3,
                          compiler_params=_params(("parallel",)), name=name)(w, g, m, v)


def _small_reduce(name, parts, width):
    def body(*refs):
        out = refs[-1]
        out[...] = jnp.zeros_like(out)
        for k, r in enumerate(refs[:-1]):
            out[pl.ds(k, 1), :] = jnp.sum(r[...], axis=0, keepdims=True)

    vmem = pl.BlockSpec(memory_space=pltpu.VMEM)
    return pl.pallas_call(body, in_specs=[vmem] * len(parts), out_specs=vmem,
                          out_shape=jax.ShapeDtypeStruct((8, width), F32), name=name)(*parts)


def _small_adamw(name, gathered, w, m, v, loss_row, loss_scale):
    def body(gt_ref, w_ref, m_ref, v_ref, g_out, d_out, m_out, v_out, loss_out):
        g = gt_ref[0]
        for k in range(1, N_DEV):
            g = g + gt_ref[k]
        g_out[...] = g
        row = lax.broadcasted_iota(jnp.int32, g.shape, 0)
        loss_out[...] = jnp.sum(jnp.where(row == loss_row, g, 0.0), keepdims=True) * loss_scale
        d_out[...], m_out[...], v_out[...] = _adamw_math(w_ref[...], g, m_ref[...], v_ref[...])

    vmem = pl.BlockSpec(memory_space=pltpu.VMEM)
    shape = jax.ShapeDtypeStruct(w.shape, F32)
    return pl.pallas_call(body, in_specs=[vmem] * 4, out_specs=[vmem] * 5,
                          out_shape=[shape] * 4 + [jax.ShapeDtypeStruct((1, 1), F32)], name=name)(gathered, w, m, v)


def _full_from_pieces(name, pieces):
    _, rows, cols = pieces.shape
    if name in ROW_SHARDED:
        return pieces.reshape(N_CHIPS * rows, cols)
    return pieces.transpose(1, 0, 2).reshape(rows, N_CHIPS * cols)


def _repack_w_in(w_in, hd, d):
    hh = hd // HEAD_DIM
    qkv, f, gates = w_in[:, :6 * hd], w_in[:, 6 * hd:6 * hd + hh], w_in[:, 6 * hd + hh:]
    return jnp.concatenate([qkv, gates], axis=1), jnp.pad(f, ((0, 0), (0, LANES - hh)))


def _unpack_dw_in(dw, dw_f, hd, d):
    hh = hd // HEAD_DIM
    return jnp.concatenate([dw[:, :6 * hd], dw_f[:, :hh], dw[:, 6 * hd:]], axis=1)


def _small_pack(vals, width):
    rows = []
    for name in SMALL:
        t = vals[name].reshape(1, -1)
        rows.append(jnp.pad(t, ((0, 0), (0, width - t.shape[1]))))
    rows.append(jnp.zeros((8 - len(SMALL), width), F32))
    return jnp.concatenate(rows, axis=0)


def kernel(x, ffn1_norm, ffn1_w_gate, ffn1_w_up, ffn1_w_down, mix_norm, w_in, b_forget, b_gate_dil, b_gate_fox, w_proj_dil, w_proj_fox, w_out, ffn2_norm, ffn2_w_gate, ffn2_w_up, ffn2_w_down, final_norm, loss_target, m_ffn1_norm, m_ffn1_w_gate, m_ffn1_w_up, m_ffn1_w_down, m_mix_norm, m_w_in, m_b_forget, m_b_gate_dil, m_b_gate_fox, m_w_proj_dil, m_w_proj_fox, m_w_out, m_ffn2_norm, m_ffn2_w_gate, m_ffn2_w_up, m_ffn2_w_down, m_final_norm, v_ffn1_norm, v_ffn1_w_gate, v_ffn1_w_up, v_ffn1_w_down, v_mix_norm, v_w_in, v_b_forget, v_b_gate_dil, v_b_gate_fox, v_w_proj_dil, v_w_proj_fox, v_w_out, v_ffn2_norm, v_ffn2_w_gate, v_ffn2_w_up, v_ffn2_w_down, v_final_norm):
    given = dict(locals())
    wts = {n: given[n] for n in WEIGHTS}
    mom_m = {n: given["m_" + n] for n in WEIGHTS}
    mom_v = {n: given["v_" + n] for n in WEIGHTS}
    s, d = x.shape[1], x.shape[2]
    hd = w_proj_dil.shape[1]

    shards = [wts[n][0].astype(BF) for n in SHARDED]
    gathered = _gather_weights("gather_weights", shards)
    chip = 2 * lax.axis_index("x") + lax.axis_index("y")
    gathered = [lax.dynamic_update_index_in_dim(t, own, chip, 0) for t, own in zip(gathered, shards)]
    full = {n: _full_from_pieces(n, t) for n, t in zip(SHARDED, gathered)}
    full["w_in"], full["w_in_f"] = _repack_w_in(full["w_in"], hd, d)
    small = {n: wts[n] for n in SMALL}

    grad_x, grads, partials = _local_step(x[0], loss_target[0], full, small)

    ids = jnp.stack([2 * lax.axis_index("x") + lax.axis_index("y"), lax.axis_index("c")]).astype(jnp.int32)
    pieces = [grads[n] for n in SHARDED]
    from_sibling = _rs_swap("rs_swap", pieces)
    chip_sums = [_rs_add("rs_add_" + n, ids, g, o) for n, g, o in zip(SHARDED, pieces, from_sibling)]
    from_chips = _rs_scatter("rs_scatter", chip_sums)
    halves = [_rs_sum("rs_sum_" + n, ids, own, got) for n, own, got in zip(SHARDED, chip_sums, from_chips)]
    totals = _rs_join("rs_join", halves)

    out_g, out_d, out_m, out_v = {}, {}, {}, {}
    for n, g in zip(SHARDED, totals):
        dl, nm, nv = _adamw("adamw_" + n, wts[n][0], g, mom_m[n][0], mom_v[n][0])
        out_g[n], out_d[n], out_m[n], out_v[n] = g[None], dl[None], nm[None], nv[None]

    width = d
    part_rows = []
    for n in SMALL:
        t = partials[n]
        part_rows.append(jnp.pad(t, ((0, 0), (0, width - t.shape[1]))))
    part_rows.append(partials["sq"])
    local_small = _small_reduce("small_reduce", part_rows, width)
    gathered_small = _gather_all("small_gather", local_small)
    sg, sd_, sm, sv, loss = _small_adamw("small_adamw", gathered_small, _small_pack(wts, width),
                                         _small_pack(mom_m, width), _small_pack(mom_v, width), len(SMALL), 0.5 / d)
    for k, n in enumerate(SMALL):
        shp = wts[n].shape
        take = lambda t: t[k, :shp[-1]].reshape(shp)
        out_g[n], out_d[n], out_m[n], out_v[n] = take(sg), take(sd_), take(sm), take(sv)
    return (loss[0, 0], grad_x[None], *[out_g[n] for n in WEIGHTS], *[out_d[n] for n in WEIGHTS],
            *[out_m[n] for n in WEIGHTS], *[out_v[n] for n in WEIGHTS])
```

```python
import functools
import math

import numpy as np
import jax
import jax.numpy as jnp
from jax import lax
from jax.experimental import pallas as pl
from jax.experimental.pallas import tpu as pltpu

HEAD_DIM = 128
ROPE_DIM = HEAD_DIM // 4
ROPE_THETA = 500000.0
DIL_PATTERNS = ((128, 1), (512, 4), (2048, 16))
MAX_WINDOW = 2048
NORM_EPS = 1e-6
ADAM_LR = 0.001
ADAM_B1 = 0.9
ADAM_B2 = 0.999
ADAM_EPS = 1e-08
ADAM_WD = 0.01
ADAM_STEP = 10

BF = jnp.bfloat16
F32 = jnp.float32
NEG = -1e30
LANES = 128
ATT_BLOCK = 512
PREP_COLS = 512
ATT_HEADS = 2
VMEM_LIMIT = 56 * 1024 * 1024
MM_VMEM_BUDGET = 40 * 1024 * 1024
N_CHIPS = 4
N_DEV = 8
MESH = pl.DeviceIdType.MESH

SHARDED = ("ffn1_w_gate", "ffn1_w_up", "ffn1_w_down", "w_in", "w_proj_dil", "w_proj_fox", "w_out",
           "ffn2_w_gate", "ffn2_w_up", "ffn2_w_down")
ROW_SHARDED = ("ffn1_w_down", "w_out", "ffn2_w_down")
SMALL = ("ffn1_norm", "mix_norm", "b_forget", "b_gate_dil", "b_gate_fox", "ffn2_norm", "final_norm")
WEIGHTS = ("ffn1_norm", "ffn1_w_gate", "ffn1_w_up", "ffn1_w_down", "mix_norm", "w_in", "b_forget",
           "b_gate_dil", "b_gate_fox", "w_proj_dil", "w_proj_fox", "w_out", "ffn2_norm", "ffn2_w_gate",
           "ffn2_w_up", "ffn2_w_down", "final_norm")


def _pick(n, target, align):
    best = None
    for d in range(align, min(n, target) + 1, align):
        if n % d == 0:
            best = d
    return n if best is None else best


def _params(sem=None):
    return pltpu.CompilerParams(dimension_semantics=sem, vmem_limit_bytes=VMEM_LIMIT)


_DIMS = {"nn": (((1,), (0,)), ((), ())), "nt": (((1,), (1,)), ((), ())), "tn": (((0,), (0,)), ((), ()))}


class _Side:
    def __init__(self, inputs, out_shapes, n_sems, start, finish, aliases=None):
        self.inputs, self.out_shapes, self.n_sems = list(inputs), list(out_shapes), n_sems
        self.start, self.finish, self.aliases = start, finish, aliases or {}

    def scratch(self):
        return [pltpu.SemaphoreType.DMA((self.n_sems,)), pltpu.SemaphoreType.DMA((self.n_sems,))]

    def call(self, name):
        n_in, n_out = len(self.inputs), len(self.out_shapes)

        def body(*refs):
            ins, outs, sems = refs[:n_in], refs[n_in:n_in + n_out], refs[n_in + n_out:]
            self.start(ins, outs, *sems)
            self.finish(ins, outs, *sems)

        return pl.pallas_call(body, in_specs=[ANY_SPEC] * n_in, out_specs=[ANY_SPEC] * n_out, out_shape=self.out_shapes,
                              input_output_aliases=self.aliases, scratch_shapes=self.scratch(), name=name)(*self.inputs)


def _mm(name, pairs, mode, m, n, k, *, epilogue, out_dtypes, extras=(), rows=(), n_colsum=0,
        sum_pairs=False, tm=1024, tn=1152, tk=2048, piece_layout=False, side=None):
    m_align = LANES if mode == "tn" else 8
    tm = _pick(m, tm, m_align)
    tn = n // N_CHIPS if piece_layout else _pick(n, tn, LANES)
    tk = _pick(k, tk, LANES)
    n_acc = 1 if sum_pairs else len(pairs)
    lhs = []
    for a, _ in pairs:
        if not any(a is t for t in lhs):
            lhs.append(a)
    lhs_of = [next(t for t in range(len(lhs)) if lhs[t] is a) for a, _ in pairs]
    n_mm = len(lhs) + len(pairs)
    n_in = n_mm + len(extras) + len(rows)
    n_out = len(out_dtypes) + n_colsum

    def vmem_bytes(tm_, tn_, tk_):
        tiles = sum(tm_ * tk_ * a.dtype.itemsize for a in lhs) + sum(tn_ * tk_ * b.dtype.itemsize for _, b in pairs)
        tiles += sum(tm_ * tn_ * arr.dtype.itemsize for arr, _ in extras)
        tiles += sum(tm_ * tn_ * jnp.dtype(dt).itemsize for dt in out_dtypes)
        return 2 * tiles + (n_acc + len(extras) + len(out_dtypes)) * tm_ * tn_ * 4

    while vmem_bytes(tm, tn, tk) > MM_VMEM_BUDGET:
        if tn > 512 and not piece_layout:
            tn = _pick(n, tn - LANES, LANES)
        elif tm > 512:
            tm = _pick(m, tm - m_align, m_align)
        elif tk > 512:
            tk = _pick(k, tk - LANES, LANES)
        elif tm > 256:
            tm = _pick(m, tm - m_align, m_align)
        else:
            break
    nk = k // tk

    n_side_in = len(side.inputs) if side else 0
    n_side_out = len(side.out_shapes) if side else 0
    grid = (m // tm, n // tn, nk)

    def body(*refs):
        ins, refs = refs[:n_in], refs[n_in:]
        side_ins, refs = refs[:n_side_in], refs[n_side_in:]
        outs, refs = refs[:n_out], refs[n_out:]
        side_outs, refs = refs[:n_side_out], refs[n_side_out:]
        accs, side_sems = refs[:n_acc], refs[n_acc:]
        kk = pl.program_id(2)
        if side:
            at = [pl.program_id(t) for t in range(3)]

            @pl.when(jnp.logical_and(jnp.logical_and(at[0] == 0, at[1] == 0), at[2] == 0))
            def _():
                side.start(side_ins, side_outs, *side_sems)

        @pl.when(kk == 0)
        def _():
            for acc in accs:
                acc[...] = jnp.zeros_like(acc)

        a_tiles = [r[...].astype(BF) for r in ins[:len(lhs)]]
        for p in range(len(pairs)):
            b = ins[len(lhs) + p][...].astype(BF)
            accs[0 if sum_pairs else p][...] += lax.dot_general(a_tiles[lhs_of[p]], b, _DIMS[mode],
                                                                preferred_element_type=F32)

        @pl.when(kk == nk - 1)
        def _():
            ex = [r[...] for r in ins[n_mm:n_mm + len(extras)]]
            rw = [r[...] for r in ins[n_mm + len(extras):]]
            res = epilogue([acc[...] for acc in accs], ex, rw)
            for o, r in zip(outs, res):
                o[...] = r.astype(o.dtype)

        if side:
            @pl.when(jnp.logical_and(jnp.logical_and(at[0] == grid[0] - 1, at[1] == grid[1] - 1), at[2] == nk - 1))
            def _():
                side.finish(side_ins, side_outs, *side_sems)

    in_specs, args = [], []
    for a in lhs:
        if mode == "tn":
            in_specs.append(pl.BlockSpec((tk, tm), lambda i, j, kk: (kk, i)))
        else:
            in_specs.append(pl.BlockSpec((tm, tk), lambda i, j, kk: (i, kk)))
        args.append(a)
    for _, b in pairs:
        if mode == "nt":
            in_specs.append(pl.BlockSpec((tn, tk), lambda i, j, kk: (j, kk)))
        else:
            in_specs.append(pl.BlockSpec((tk, tn), lambda i, j, kk: (kk, j)))
        args.append(b)
    for arr, off in extras:
        assert off % tn == 0
        in_specs.append(pl.BlockSpec((tm, tn), functools.partial(lambda i, j, kk, o: (i, j + o), o=off // tn)))
        args.append(arr)
    for arr in rows:
        in_specs.append(pl.BlockSpec((1, tn), lambda i, j, kk: (0, j)))
        args.append(arr)
    if piece_layout:
        out_specs = [pl.BlockSpec((None, tm, tn), lambda i, j, kk: (j, i, 0)) for _ in out_dtypes]
        out_shape = [jax.ShapeDtypeStruct((n // tn, m, tn), d) for d in out_dtypes]
    else:
        out_specs = [pl.BlockSpec((tm, tn), lambda i, j, kk: (i, j)) for _ in out_dtypes]
        out_shape = [jax.ShapeDtypeStruct((m, n), d) for d in out_dtypes]
    for _ in range(n_colsum):
        out_specs.append(pl.BlockSpec((None, 1, tn), lambda i, j, kk: (i, 0, j)))
        out_shape.append(jax.ShapeDtypeStruct((m // tm, 1, n), F32))
    scratch = [pltpu.VMEM((tm, tn), F32) for _ in range(n_acc)]
    if side is None:
        return pl.pallas_call(
            body, grid=grid, in_specs=in_specs, out_specs=out_specs, out_shape=out_shape, scratch_shapes=scratch,
            compiler_params=_params(("parallel", "parallel", "arbitrary")), name=name)(*args)
    res = pl.pallas_call(
        body, grid=grid, in_specs=in_specs + [ANY_SPEC] * n_side_in, out_specs=out_specs + [ANY_SPEC] * n_side_out,
        out_shape=out_shape + side.out_shapes, scratch_shapes=scratch + side.scratch(),
        input_output_aliases={n_in + t: n_out + o for t, o in side.aliases.items()},
        compiler_params=_params(("arbitrary", "arbitrary", "arbitrary")), name=name)(*args, *side.inputs)
    return res[:n_out], res[n_out:]


def _col_pieces(full):
    rows, cols = full.shape
    return full.reshape(rows, N_CHIPS, cols // N_CHIPS).transpose(1, 0, 2)


def _dw_col_pieces(name, a, b, m, n, k):
    ident = lambda accs, ex, rw: (accs[0],)
    if (n // N_CHIPS) % LANES == 0:
        return _mm(name, [(a, b)], "tn", m, n, k, epilogue=ident, out_dtypes=(F32,), tm=512, piece_layout=True)[0]
    return _col_pieces(_mm(name, [(a, b)], "tn", m, n, k, epilogue=ident, out_dtypes=(F32,))[0])


def _sigmoid(z):
    return 0.5 * jnp.tanh(0.5 * z) + 0.5


def _row_tile(s):
    return _pick(s, 256, 8)


def _fold8(t):
    r, d = t.shape
    return jnp.sum(t.reshape(r // 8, 8, d), axis=0)


def _rms_fwd(name, x, g):
    s, d = x.shape
    tr = _row_tile(s)

    def body(x_ref, g_ref, h_ref):
        xf = x_ref[...]
        y = xf * lax.rsqrt(jnp.mean(xf * xf, axis=-1, keepdims=True) + NORM_EPS)
        h_ref[...] = (y * g_ref[...]).astype(BF)

    return pl.pallas_call(
        body, grid=(s // tr,),
        in_specs=[pl.BlockSpec((tr, d), lambda i: (i, 0)), pl.BlockSpec((1, d), lambda i: (0, 0))],
        out_specs=pl.BlockSpec((tr, d), lambda i: (i, 0)), out_shape=jax.ShapeDtypeStruct((s, d), BF),
        compiler_params=_params(("parallel",)), name=name)(x, g)


def _rms_bwd(name, x, g, dh, dres):
    s, d = x.shape
    tr = _row_tile(s)

    def body(x_ref, g_ref, dh_ref, dres_ref, dx_ref, dxb_ref, dg_ref):
        @pl.when(pl.program_id(0) == 0)
        def _():
            dg_ref[...] = jnp.zeros_like(dg_ref)

        xf = x_ref[...]
        rstd = lax.rsqrt(jnp.mean(xf * xf, axis=-1, keepdims=True) + NORM_EPS)
        xhat = xf * rstd
        dhf = dh_ref[...]
        dg_ref[...] += _fold8(dhf * xhat)
        dxh = dhf * g_ref[...]
        dx = dres_ref[...] + rstd * (dxh - xhat * jnp.mean(dxh * xhat, axis=-1, keepdims=True))
        dx_ref[...] = dx
        dxb_ref[...] = dx.astype(BF)

    blk = pl.BlockSpec((tr, d), lambda i: (i, 0))
    return pl.pallas_call(
        body, grid=(s // tr,),
        in_specs=[blk, pl.BlockSpec((1, d), lambda i: (0, 0)), blk, blk],
        out_specs=[blk, blk, pl.BlockSpec((8, d), lambda i: (0, 0))],
        out_shape=[jax.ShapeDtypeStruct((s, d), F32), jax.ShapeDtypeStruct((s, d), BF),
                   jax.ShapeDtypeStruct((8, d), F32)],
        compiler_params=_params(("arbitrary",)), name=name)(x, g, dh, dres)


def _final(name, x, g, tgt):
    s, d = x.shape
    tr = _row_tile(s)

    def body(x_ref, g_ref, t_ref, dx_ref, dxb_ref, dg_ref, sq_ref):
        @pl.when(pl.program_id(0) == 0)
        def _():
            dg_ref[...] = jnp.zeros_like(dg_ref)
            sq_ref[...] = jnp.zeros_like(sq_ref)

        xf = x_ref[...]
        rstd = lax.rsqrt(jnp.mean(xf * xf, axis=-1, keepdims=True) + NORM_EPS)
        xhat = xf * rstd
        gf = g_ref[...]
        err = xhat * gf - t_ref[...]
        sq_ref[...] += _fold8(err * err)
        dy = err * (1.0 / d)
        dg_ref[...] += _fold8(dy * xhat)
        dxh = dy * gf
        dx = rstd * (dxh - xhat * jnp.mean(dxh * xhat, axis=-1, keepdims=True))
        dx_ref[...] = dx
        dxb_ref[...] = dx.astype(BF)

    blk = pl.BlockSpec((tr, d), lambda i: (i, 0))
    acc = pl.BlockSpec((8, d), lambda i: (0, 0))
    return pl.pallas_call(
        body, grid=(s // tr,), in_specs=[blk, pl.BlockSpec((1, d), lambda i: (0, 0)), blk],
        out_specs=[blk, blk, acc, acc],
        out_shape=[jax.ShapeDtypeStruct((s, d), F32), jax.ShapeDtypeStruct((s, d), BF),
                   jax.ShapeDtypeStruct((8, d), F32), jax.ShapeDtypeStruct((8, d), F32)],
        compiler_params=_params(("arbitrary",)), name=name)(x, g, tgt)


def _ffn_fwd(tag, x, g, w_gate, w_up, w_down, side=None):
    s, d = x.shape
    f = w_gate.shape[1]
    h = _rms_fwd(tag + "_rms", x, g)

    def up_epi(accs, ex, rw):
        a, b = accs
        return a, b, a * _sigmoid(a) * b

    res = _mm(tag + "_up", [(h, w_gate), (h, w_up)], "nn", s, f, d, epilogue=up_epi, out_dtypes=(BF, BF, BF),
              side=side)
    (a, b, act), side_res = res if side else (res, None)

    def down_epi(accs, ex, rw):
        return (ex[0] + 0.5 * accs[0],)

    (y,) = _mm(tag + "_down", [(act, w_down)], "nn", s, d, f, epilogue=down_epi, out_dtypes=(F32,),
               extras=[(x, 0)])
    return y, (h, a, b, act), side_res


def _ffn_bwd(tag, x, g, w_gate, w_up, w_down, saved, dy, dy_bf, side_dact=None, side_dh=None):
    s, d = x.shape
    f = w_gate.shape[1]
    h, a, b, act = saved

    def act_epi(accs, ex, rw):
        dact = 0.5 * accs[0]
        av, bv = ex[0].astype(F32), ex[1].astype(F32)
        sg = _sigmoid(av)
        return dact * bv * (sg * (1.0 + av * (1.0 - sg))), dact * (av * sg)

    res = _mm(tag + "_dact", [(dy_bf, w_down)], "nt", s, f, d, epilogue=act_epi, out_dtypes=(BF, BF),
              extras=[(a, 0), (b, 0)], side=side_dact)
    (da, db), dact_res = res if side_dact else (res, None)
    ident = lambda accs, ex, rw: (accs[0],)
    (dw_down,) = _mm(tag + "_dwd", [(act, dy_bf)], "tn", f, d, s, epilogue=lambda accs, ex, rw: (0.5 * accs[0],),
                     out_dtypes=(F32,))
    dw_down = dw_down.reshape(N_CHIPS, f // N_CHIPS, d)
    dw_gate = _dw_col_pieces(tag + "_dwg", h, da, d, f, s)
    dw_up = _dw_col_pieces(tag + "_dwu", h, db, d, f, s)
    side = side_dh(dact_res) if side_dh else None
    res = _mm(tag + "_dh", [(da, w_gate), (db, w_up)], "nt", s, d, f, epilogue=ident, out_dtypes=(F32,),
              sum_pairs=True, side=side)
    (dh,), dh_res = res if side else (res, None)
    dx, dx_bf, dg = _rms_bwd(tag + "_rmsb", x, g, dh, dy)
    return dx, dx_bf, dg, dw_gate, dw_up, dw_down, dh_res


def _rope_tables(s):
    half = ROPE_DIM // 2
    pos = jnp.arange(s, dtype=F32)
    inv_freq = ROPE_THETA ** (-jnp.arange(0, ROPE_DIM, 2, dtype=F32) / ROPE_DIM)
    ang = pos[:, None] * inv_freq[None, :]
    cos, sin = jnp.cos(ang), jnp.sin(ang)
    rest = HEAD_DIM - ROPE_DIM
    cos_t = jnp.concatenate([cos, cos, jnp.ones((s, rest), F32)], axis=-1)
    sin_t = jnp.concatenate([-sin, sin, jnp.zeros((s, rest), F32)], axis=-1)
    return cos_t, sin_t


def _swap_halves(t):
    lane = lax.broadcasted_iota(jnp.int32, t.shape, 1) & (HEAD_DIM - 1)
    half = ROPE_DIM // 2
    return jnp.where(lane < half, pltpu.roll(t, t.shape[1] - half, 1), pltpu.roll(t, half, 1))


def _dil_bias(blk):
    n_delta = MAX_WINDOW // blk + 1
    delta = jnp.arange(n_delta, dtype=jnp.int32)[:, None, None]
    r = jnp.arange(blk, dtype=jnp.int32)[None, None, :]
    c = jnp.arange(blk, dtype=jnp.int32)[None, :, None]
    o = delta * blk + r - c
    mult = jnp.zeros(o.shape, F32)
    for w, dd in DIL_PATTERNS:
        mult = mult + ((o >= 0) & (o <= w) & (o % dd == 0)).astype(F32)
    return jnp.where(mult > 0, jnp.log(jnp.maximum(mult, 1.0)), NEG)


def _att_block(s):
    return _pick(s, ATT_BLOCK, LANES)


def _fox_aug(name, c_pad, n_heads):
    s = c_pad.shape[0]
    tr = _pick(s, 1024, 16)

    def body(c_ref, qx_ref, kx_ref):
        h = pl.program_id(1)
        lane = lax.broadcasted_iota(jnp.int32, (tr, LANES), 1)
        ch = jnp.sum(jnp.where(lane == h, c_ref[...], 0.0), axis=1, keepdims=True)
        hi, mid, lo = (t.astype(F32) for t in _split3(ch))
        zero = jnp.zeros((tr, LANES), F32)
        is_hi = jnp.logical_or(lane == 0, lane == 3)
        is_mid = jnp.logical_or(lane == 1, lane == 4)
        parts = jnp.where(is_hi, hi, jnp.where(is_mid, mid, lo))
        qx_ref[...] = jnp.where(lane < 3, 1.0, jnp.where(lane < 6, parts, zero)).astype(BF)
        kx_ref[...] = jnp.where(lane < 3, -parts, jnp.where(lane < 6, 1.0, zero)).astype(BF)

    spec = pl.BlockSpec((tr, LANES), lambda i, h: (i, h))
    shape = jax.ShapeDtypeStruct((s, n_heads * LANES), BF)
    return pl.pallas_call(
        body, grid=(s // tr, n_heads), in_specs=[pl.BlockSpec((tr, LANES), lambda i, h: (i, 0))],
        out_specs=[spec, spec], out_shape=[shape, shape],
        compiler_params=_params(("parallel", "arbitrary")), name=name)(c_pad)


def _causal_mask(st):
    kpos = lax.broadcasted_iota(jnp.int32, st.shape, 0)
    qpos = lax.broadcasted_iota(jnp.int32, st.shape, 1)
    return jnp.where(kpos <= qpos, st, NEG)


def _flash_fwd(name, qkv, q_off, k_off, v_off, n_heads, *, fox, tab_t=None, qx=None, kx=None):
    s = qkv.shape[0]
    blk = _att_block(s)
    nq = s // blk
    n_delta = MAX_WINDOW // blk + 1
    grp = ATT_HEADS
    assert n_heads % grp == 0 and q_off % grp == 0 and k_off % grp == 0 and v_off % grp == 0
    wide = grp * HEAD_DIM

    def body(*refs):
        if fox:
            q_ref, k_ref, v_ref, qx_ref, kx_ref, o_ref, lse_ref, acc, m_s, l_s = refs
        else:
            q_ref, k_ref, v_ref, tab_ref, o_ref, lse_ref, acc, m_s, l_s = refs
        i = pl.program_id(1)
        acc[...] = jnp.zeros_like(acc)
        m_s[...] = jnp.full_like(m_s, NEG)
        l_s[...] = jnp.zeros_like(l_s)

        def step(j, diagonal):
            ks = pl.ds(pl.multiple_of(j * blk, blk), blk)
            for g in range(grp):
                cols = slice(g * HEAD_DIM, (g + 1) * HEAD_DIM)
                st = lax.dot_general(k_ref[ks, cols], q_ref[:, cols], _DIMS["nt"], preferred_element_type=F32)
                if fox:
                    st = st + lax.dot_general(kx_ref[ks, cols], qx_ref[:, cols], _DIMS["nt"],
                                              preferred_element_type=F32)
                    if diagonal:
                        st = _causal_mask(st)
                else:
                    st = st + tab_ref[i - j]
                m_prev = m_s[g]
                m_new = jnp.maximum(m_prev, jnp.max(st, axis=0, keepdims=True))
                alpha = jnp.exp(m_prev - m_new)
                p = jnp.exp(st - m_new)
                l_s[g] = alpha * l_s[g] + jnp.sum(p, axis=0, keepdims=True)
                acc[g] = alpha * acc[g] + lax.dot_general(v_ref[ks, cols], p.astype(BF), _DIMS["tn"],
                                                          preferred_element_type=F32)
                m_s[g] = m_new

        def loop_step(j, carry):
            step(j, False)
            return carry

        if fox:
            lax.fori_loop(0, i, loop_step, 0)
            step(i, True)
        else:
            lax.fori_loop(jnp.maximum(i - (n_delta - 1), 0), i + 1, loop_step, 0)
        for g in range(grp):
            o_ref[:, g * HEAD_DIM:(g + 1) * HEAD_DIM] = (acc[g] / l_s[g]).T.astype(o_ref.dtype)
            lse_ref[g] = m_s[g] + jnp.log(l_s[g])

    off = lambda o: functools.partial(lambda h, i, o: (0, o + h), o=o // grp)
    in_specs = [pl.BlockSpec((blk, wide), functools.partial(lambda h, i, o: (i, o + h), o=q_off // grp)),
                pl.BlockSpec((s, wide), off(k_off)), pl.BlockSpec((s, wide), off(v_off))]
    args = [qkv, qkv, qkv]
    if fox:
        in_specs += [pl.BlockSpec((blk, wide), lambda h, i: (i, h)), pl.BlockSpec((s, wide), lambda h, i: (0, h))]
        args += [qx, kx]
    else:
        in_specs.append(pl.BlockSpec((n_delta, blk, blk), lambda h, i: (0, 0, 0)))
        args.append(tab_t)
    return pl.pallas_call(
        body, grid=(n_heads // grp, nq), in_specs=in_specs,
        out_specs=[pl.BlockSpec((blk, wide), lambda h, i: (i, h)),
                   pl.BlockSpec((grp, None, 1, blk), lambda h, i: (h, i, 0, 0))],
        out_shape=[jax.ShapeDtypeStruct((s, n_heads * HEAD_DIM), BF),
                   jax.ShapeDtypeStruct((n_heads, nq, 1, blk), F32)],
        scratch_shapes=[pltpu.VMEM((grp, HEAD_DIM, blk), F32), pltpu.VMEM((grp, 1, blk), F32),
                        pltpu.VMEM((grp, 1, blk), F32)],
        compiler_params=_params(("parallel", "parallel")), name=name)(*args)


def _att_delta(name, do, o, n_heads):
    s = do.shape[0]
    blk = _pick(s, 1024, 16)

    def body(do_ref, o_ref, d_ref):
        d_ref[...] = jnp.sum(do_ref[...].astype(F32) * o_ref[...].astype(F32), axis=-1, keepdims=True)

    spec = pl.BlockSpec((blk, HEAD_DIM), lambda h, i: (i, h))
    return pl.pallas_call(
        body, grid=(n_heads, s // blk), in_specs=[spec, spec],
        out_specs=pl.BlockSpec((None, blk, 1), lambda h, i: (h, i, 0)),
        out_shape=jax.ShapeDtypeStruct((n_heads, s, 1), F32),
        compiler_params=_params(("parallel", "parallel")), name=name)(do, o)


def _flash_bwd(name, qkv, q_off, k_off, v_off, do, lse_row, delta_row, n_heads, *, fox, tab_t=None,
               qx=None, kx=None):
    s = qkv.shape[0]
    blk = _att_block(s)
    nq = s // blk
    n_delta = MAX_WINDOW // blk + 1
    grp = ATT_HEADS
    assert n_heads % grp == 0 and q_off % grp == 0 and k_off % grp == 0 and v_off % grp == 0
    wide = grp * HEAD_DIM

    def body(*refs):
        if fox:
            (q_ref, do_ref, k_ref, v_ref, lse_ref, dl_ref, qx_ref, kx_ref,
             dq_ref, dk_ref, dv_ref, dc_ref, dcq_ref, dk_acc, dv_acc, dc_acc) = refs
        else:
            (q_ref, do_ref, k_ref, v_ref, lse_ref, dl_ref, tab_ref,
             dq_ref, dk_ref, dv_ref, dk_acc, dv_acc) = refs
        j = pl.program_id(1)

        @pl.when(j == 0)
        def _():
            dq_ref[...] = jnp.zeros_like(dq_ref)
            if fox:
                dcq_ref[...] = jnp.zeros_like(dcq_ref)

        dk_acc[...] = jnp.zeros_like(dk_acc)
        dv_acc[...] = jnp.zeros_like(dv_acc)
        if fox:
            dc_acc[...] = jnp.zeros_like(dc_acc)

        def step(i, diagonal):
            qs = pl.ds(pl.multiple_of(i * blk, blk), blk)
            for g in range(grp):
                cols = slice(g * HEAD_DIM, (g + 1) * HEAD_DIM)
                kb, vb = k_ref[:, cols], v_ref[:, cols]
                qb, dob = q_ref[qs, cols], do_ref[qs, cols]
                st = lax.dot_general(kb, qb, _DIMS["nt"], preferred_element_type=F32)
                if fox:
                    st = st + lax.dot_general(kx_ref[:, cols], qx_ref[qs, cols], _DIMS["nt"],
                                              preferred_element_type=F32)
                    if diagonal:
                        st = _causal_mask(st)
                else:
                    st = st + tab_ref[i - j]
                pt = jnp.exp(st - lse_ref[g, i])
                dv_acc[:, cols] += jnp.dot(pt.astype(BF), dob, preferred_element_type=F32)
                dpt = lax.dot_general(vb, dob, _DIMS["nt"], preferred_element_type=F32)
                dst = pt * (dpt - dl_ref[g, i])
                dsb = dst.astype(BF)
                dk_acc[:, cols] += jnp.dot(dsb, qb, preferred_element_type=F32)
                dq_ref[qs, cols] += lax.dot_general(dsb, kb, _DIMS["tn"], preferred_element_type=F32)
                if fox:
                    folded = dst[:, :LANES]
                    for part in range(1, blk // LANES):
                        folded = folded + dst[:, part * LANES:(part + 1) * LANES]
                    dc_acc[g] -= folded
                    dcq_ref[g, i] += jnp.sum(dst, axis=0, keepdims=True)

        def loop_step(i, carry):
            step(i, False)
            return carry

        if fox:
            step(j, True)
            lax.fori_loop(j + 1, nq, loop_step, 0)
        else:
            lax.fori_loop(j, jnp.minimum(nq, j + n_delta), loop_step, 0)
        dk_ref[...] = dk_acc[...]
        dv_ref[...] = dv_acc[...].astype(dv_ref.dtype)
        if fox:
            for g in range(grp):
                dc_ref[g] = jnp.sum(dc_acc[g], axis=-1, keepdims=True)

    full = lambda o: pl.BlockSpec((s, wide), functools.partial(lambda h, j, o: (0, o + h), o=o // grp))
    tile = lambda o: pl.BlockSpec((blk, wide), functools.partial(lambda h, j, o: (j, o + h), o=o // grp))
    per_q = pl.BlockSpec((grp, nq, 1, blk), lambda h, j: (h, 0, 0, 0))
    per_k = pl.BlockSpec((grp, blk, 1), lambda h, j: (h, j, 0))
    in_specs = [full(q_off), full(0), tile(k_off), tile(v_off), per_q, per_q]
    args = [qkv, do, qkv, qkv, lse_row, delta_row]
    out_specs = [full(0), tile(0), tile(0)]
    hd = n_heads * HEAD_DIM
    out_shape = [jax.ShapeDtypeStruct((s, hd), F32), jax.ShapeDtypeStruct((s, hd), F32),
                 jax.ShapeDtypeStruct((s, hd), BF)]
    scratch = [pltpu.VMEM((blk, wide), F32), pltpu.VMEM((blk, wide), F32)]
    if fox:
        in_specs += [full(0), tile(0)]
        args += [qx, kx]
        out_specs += [per_k, per_q]
        out_shape += [jax.ShapeDtypeStruct((n_heads, s, 1), F32), jax.ShapeDtypeStruct((n_heads, nq, 1, blk), F32)]
        scratch.append(pltpu.VMEM((grp, blk, LANES), F32))
    else:
        in_specs.append(pl.BlockSpec((n_delta, blk, blk), lambda h, j: (0, 0, 0)))
        args.append(tab_t)
    return pl.pallas_call(
        body, grid=(n_heads // grp, nq), in_specs=in_specs, out_specs=out_specs, out_shape=out_shape,
        scratch_shapes=scratch, compiler_params=_params(("parallel", "arbitrary")), name=name)(*args)


def _split3(t):
    hi = t.astype(BF)
    r1 = t - hi.astype(F32)
    mid = r1.astype(BF)
    lo = (r1 - mid.astype(F32)).astype(BF)
    return hi, mid, lo


def _tri_dot(tri, t):
    hi, mid, lo = _split3(t)
    return (jnp.dot(tri, hi, preferred_element_type=F32) + jnp.dot(tri, mid, preferred_element_type=F32)
            + jnp.dot(tri, lo, preferred_element_type=F32))


def _log_sigmoid(z):
    return jnp.minimum(z, 0.0) - jnp.log(1.0 + jnp.exp(-jnp.abs(z)))


def _forget_cumsum(name, proj, f_col, bias):
    s = proj.shape[0]
    blk = _att_block(s)

    def body(f_ref, b_ref, c_ref, carry):
        @pl.when(pl.program_id(0) == 0)
        def _():
            carry[...] = jnp.zeros_like(carry)

        lf = _log_sigmoid(f_ref[...] + b_ref[...])
        r = lax.broadcasted_iota(jnp.int32, (blk, blk), 0)
        c = lax.broadcasted_iota(jnp.int32, (blk, blk), 1)
        tri = (c <= r).astype(BF)
        c_ref[...] = _tri_dot(tri, lf) + carry[...]
        carry[...] = c_ref[pl.ds(blk - 1, 1), :]

    return pl.pallas_call(
        body, grid=(s // blk,),
        in_specs=[pl.BlockSpec((blk, LANES), lambda i: (i, f_col)), pl.BlockSpec((1, LANES), lambda i: (0, 0))],
        out_specs=pl.BlockSpec((blk, LANES), lambda i: (i, 0)), out_shape=jax.ShapeDtypeStruct((s, LANES), F32),
        scratch_shapes=[pltpu.VMEM((1, LANES), F32)],
        compiler_params=_params(("arbitrary",)), name=name)(proj, bias)


def _forget_bwd(name, proj, f_col, bias, dc):
    s = proj.shape[0]
    blk = _att_block(s)
    nb = s // blk

    def body(f_ref, b_ref, dc_ref, df_ref, db_ref, carry):
        @pl.when(pl.program_id(0) == 0)
        def _():
            carry[...] = jnp.zeros_like(carry)
            db_ref[...] = jnp.zeros_like(db_ref)

        r = lax.broadcasted_iota(jnp.int32, (blk, blk), 0)
        c = lax.broadcasted_iota(jnp.int32, (blk, blk), 1)
        tri = (c >= r).astype(BF)
        r = lax.broadcasted_iota(jnp.int32, (blk, LANES), 0)
        dlf = _tri_dot(tri, dc_ref[...]) + carry[...]
        carry[...] = jnp.sum(jnp.where(r == 0, dlf, 0.0), axis=0, keepdims=True)
        dz = dlf * _sigmoid(-(f_ref[...] + b_ref[...]))
        df_ref[...] = dz.astype(BF)
        db_ref[...] += _fold8(dz)

    rev = lambda i: (nb - 1 - i, 0)
    return pl.pallas_call(
        body, grid=(nb,),
        in_specs=[pl.BlockSpec((blk, LANES), lambda i: (nb - 1 - i, f_col)), pl.BlockSpec((1, LANES), lambda i: (0, 0)),
                  pl.BlockSpec((blk, LANES), rev)],
        out_specs=[pl.BlockSpec((blk, LANES), rev), pl.BlockSpec((8, LANES), lambda i: (0, 0))],
        out_shape=[jax.ShapeDtypeStruct((s, LANES), BF), jax.ShapeDtypeStruct((8, LANES), F32)],
        scratch_shapes=[pltpu.VMEM((1, LANES), F32)],
        compiler_params=_params(("arbitrary",)), name=name)(proj, bias, dc)


def _head_prep(name, proj, cos_t, sin_t, n_heads):
    s = proj.shape[0]
    tr = _pick(s, 512, 16)
    scale = HEAD_DIM ** -0.5
    hd = n_heads * HEAD_DIM
    cw = _pick(hd, PREP_COLS, HEAD_DIM)
    per_kind = hd // cw
    cos_w, sin_w = jnp.tile(cos_t, (1, cw // HEAD_DIM)), jnp.tile(sin_t, (1, cw // HEAD_DIM))

    def body(p_ref, cos_ref, sin_ref, o_ref):
        j = pl.program_id(1)
        t = p_ref[...]

        @pl.when(j < 2 * per_kind)
        def _():
            r = t * cos_ref[...] + _swap_halves(t) * sin_ref[...]
            o_ref[...] = jnp.where(j < per_kind, r * scale, r).astype(BF)

        @pl.when(j >= 2 * per_kind)
        def _():
            is_q = jnp.logical_and(j >= 3 * per_kind, j < 4 * per_kind)
            o_ref[...] = jnp.where(is_q, t * scale, t).astype(BF)

    tab = pl.BlockSpec((tr, cw), lambda i, j: (i, 0))
    blk = pl.BlockSpec((tr, cw), lambda i, j: (i, j))
    return pl.pallas_call(
        body, grid=(s // tr, 6 * per_kind), in_specs=[blk, tab, tab], out_specs=blk,
        out_shape=jax.ShapeDtypeStruct((s, 6 * hd), BF),
        compiler_params=_params(("parallel", "arbitrary")), name=name)(proj, cos_w, sin_w)


def _dproj_assemble(name, parts, gates, cos_t, sin_t, n_heads, d_model):
    s = cos_t.shape[0]
    tr = _pick(s, 512, 16)
    scale = HEAD_DIM ** -0.5
    hd = n_heads * HEAD_DIM
    cw = _pick(math.gcd(hd, d_model), PREP_COLS, HEAD_DIM)
    widths = [hd] * 6 + [d_model] * 2
    starts = [sum(widths[:t]) // cw for t in range(len(widths) + 1)]
    cos_w, sin_w = jnp.tile(cos_t, (1, cw // HEAD_DIM)), jnp.tile(sin_t, (1, cw // HEAD_DIM))

    def body(*refs):
        p_refs, cos_ref, sin_ref, o_ref = refs[:8], refs[8], refs[9], refs[10]
        j = pl.program_id(1)
        for kind in range(8):
            @pl.when(jnp.logical_and(j >= starts[kind], j < starts[kind + 1]))
            def _(kind=kind):
                t = p_refs[kind][...]
                if kind in (0, 1, 3):
                    t = t.astype(F32)
                if kind in (0, 3):
                    t = t * scale
                if kind in (0, 1):
                    t = t * cos_ref[...] - _swap_halves(t) * sin_ref[...]
                o_ref[...] = t.astype(BF)

    def part_spec(kind):
        return pl.BlockSpec((tr, cw), functools.partial(
            lambda i, j, kind: (i, jnp.clip(j - starts[kind], 0, widths[kind] // cw - 1)), kind=kind))

    tab = pl.BlockSpec((tr, cw), lambda i, j: (i, 0))
    return pl.pallas_call(
        body, grid=(s // tr, starts[-1]), in_specs=[part_spec(kind) for kind in range(8)] + [tab, tab],
        out_specs=pl.BlockSpec((tr, cw), lambda i, j: (i, j)),
        out_shape=jax.ShapeDtypeStruct((s, sum(widths)), BF),
        compiler_params=_params(("parallel", "arbitrary")), name=name)(*parts, *gates, cos_w, sin_w)


FFN1 = ("ffn1_w_gate", "ffn1_w_up", "ffn1_w_down")
MIXER = ("w_in", "w_proj_dil", "w_proj_fox", "w_out")
FFN2 = ("ffn2_w_gate", "ffn2_w_up", "ffn2_w_down")


def _device_step(x, tgt, shards, small):
    s, d = x.shape
    hd = shards["w_proj_dil"].shape[0]
    hh = hd // HEAD_DIM
    blk = _att_block(s)
    gate_off = 6 * hd
    f_col = 0
    n_proj = gate_off + 2 * d
    ident = lambda accs, ex, rw: (accs[0],)
    chip = 2 * lax.axis_index("x") + lax.axis_index("y")
    ids = jnp.stack([chip, lax.axis_index("c")]).astype(jnp.int32)
    w = {}

    def take_gathered(names, gathered):
        for n, t in zip(names, gathered):
            w[n] = _full_from_pieces(n, lax.dynamic_update_index_in_dim(t, shards[n], chip, 0))

    def chip_sums(names, pieces, from_sibling):
        return [_rs_add("rs_add_" + n, ids, g, o) for n, g, o in zip(names, pieces, from_sibling)]

    def core_halves(names, sums, from_chips):
        return [_rs_sum("rs_sum_" + n, ids, own, got) for n, own, got in zip(names, sums, from_chips)]

    take_gathered(FFN1, _gather_side([shards[n] for n in FFN1]).call("gather_ffn1"))
    x1, saved1, gathered = _ffn_fwd("ffn1", x, small["ffn1_norm"], w["ffn1_w_gate"], w["ffn1_w_up"],
                                    w["ffn1_w_down"], side=_gather_side([shards[n] for n in MIXER]))
    take_gathered(MIXER, gathered)
    w["w_in"], w["w_in_f"] = _repack_w_in(w["w_in"], hd, d)

    hm = _rms_fwd("mix_rms", x1, small["mix_norm"])
    (proj,), gathered = _mm("proj", [(hm, w["w_in"])], "nn", s, n_proj, d, epilogue=ident, out_dtypes=(F32,),
                            side=_gather_side([shards[n] for n in FFN2]))
    take_gathered(FFN2, gathered)
    (f_logit,) = _mm("proj_f", [(hm, w["w_in_f"])], "nn", s, LANES, d, epilogue=ident, out_dtypes=(F32,))
    cos_t, sin_t = _rope_tables(s)
    qkv = _head_prep("head_prep", proj, cos_t, sin_t, hh)
    tab_t = _dil_bias(blk)
    y_dil, lse_d = _flash_fwd("dil_fwd", qkv, 0, hh, 2 * hh, hh, fox=False, tab_t=tab_t)
    bias_f = jnp.pad(small["b_forget"], ((0, 0), (0, LANES - hh)))
    c_pad = _forget_cumsum("forget_cumsum", f_logit, f_col, bias_f)
    qx, kx = _fox_aug("fox_aug", c_pad, hh)
    y_fox, lse_f = _flash_fwd("fox_fwd", qkv, 3 * hh, 4 * hh, 5 * hh, hh, fox=True, qx=qx, kx=kx)

    def merge_epi(accs, ex, rw):
        ud, uf = accs
        return ud, uf, _sigmoid(ex[0] + rw[0]) * ud + _sigmoid(ex[1] + rw[1]) * uf

    u_d, u_f, merged = _mm("merge", [(y_dil, w["w_proj_dil"]), (y_fox, w["w_proj_fox"])], "nn", s, d, hd,
                           epilogue=merge_epi, out_dtypes=(BF, BF, BF),
                           extras=[(proj, gate_off), (proj, gate_off + d)],
                           rows=[small["b_gate_dil"], small["b_gate_fox"]])
    (x2,) = _mm("mix_out", [(merged, w["w_out"])], "nn", s, d, d,
                epilogue=lambda accs, ex, rw: (ex[0] + accs[0],), out_dtypes=(F32,), extras=[(x1, 0)])

    x3, saved2, _ = _ffn_fwd("ffn2", x2, small["ffn2_norm"], w["ffn2_w_gate"], w["ffn2_w_up"], w["ffn2_w_down"])
    dx3, dx3_bf, dg_final, sq = _final("final", x3, small["final_norm"].reshape(1, d), tgt)

    dx2, dx2_bf, dg_ffn2, dw_g2, dw_u2, dw_d2, _ = _ffn_bwd("ffn2", x2, small["ffn2_norm"], w["ffn2_w_gate"],
                                                            w["ffn2_w_up"], w["ffn2_w_down"], saved2, dx3, dx3_bf)
    pieces2 = [dw_g2, dw_u2, dw_d2]

    def dmerge_epi(accs, ex, rw):
        dm = accs[0]
        gd, gf, ud, uf = ex[0], ex[1], ex[2].astype(F32), ex[3].astype(F32)
        sd, sf = _sigmoid(gd + rw[0]), _sigmoid(gf + rw[1])
        dgd = dm * ud * (sd * (1.0 - sd))
        dgf = dm * uf * (sf * (1.0 - sf))
        return (dm * sd, dm * sf, dgd, dgf, jnp.sum(dgd, axis=0, keepdims=True), jnp.sum(dgf, axis=0, keepdims=True))

    (du_d, du_f, dg_d, dg_f, dbg_d, dbg_f), from_sibling = _mm(
        "dmerge", [(dx2_bf, w["w_out"])], "nt", s, d, d, epilogue=dmerge_epi, out_dtypes=(BF, BF, BF, BF), n_colsum=2,
        extras=[(proj, gate_off), (proj, gate_off + d), (u_d, 0), (u_f, 0)],
        rows=[small["b_gate_dil"], small["b_gate_fox"]], side=_swap_side(pieces2))
    sums2 = chip_sums(FFN2, pieces2, from_sibling)
    (dw_out,) = _mm("dw_out", [(merged, dx2_bf)], "tn", d, d, s, epilogue=ident, out_dtypes=(F32,))
    dw_out = dw_out.reshape(N_CHIPS, d // N_CHIPS, d)
    dw_pd = _dw_col_pieces("dw_pd", y_dil, du_d, hd, d, s)
    dw_pf = _dw_col_pieces("dw_pf", y_fox, du_f, hd, d, s)
    (dy_dil,) = _mm("dy_dil", [(du_d, w["w_proj_dil"])], "nt", s, hd, d, epilogue=ident, out_dtypes=(BF,))
    (dy_fox,) = _mm("dy_fox", [(du_f, w["w_proj_fox"])], "nt", s, hd, d, epilogue=ident, out_dtypes=(BF,))

    row = lambda t: t.reshape(hh, s // blk, 1, blk)
    delta_d = _att_delta("dil_delta", dy_dil, y_dil, hh)
    dq_d, dk_d, dv_d = _flash_bwd("dil_bwd", qkv, 0, hh, 2 * hh, dy_dil, lse_d, row(delta_d), hh, fox=False,
                                  tab_t=tab_t)
    delta_f = _att_delta("fox_delta", dy_fox, y_fox, hh)
    dq_f, dk_f, dv_f, dc_k, dc_q = _flash_bwd("fox_bwd", qkv, 3 * hh, 4 * hh, 5 * hh, dy_fox, lse_f, row(delta_f), hh,
                                              fox=True, qx=qx, kx=kx)
    dc = dc_k.reshape(hh, s) + dc_q.reshape(hh, s)
    dc_pad = jnp.pad(dc.T, ((0, 0), (0, LANES - hh)))
    df, db_forget = _forget_bwd("forget_bwd", f_logit, f_col, bias_f, dc_pad)
    dproj = _dproj_assemble("dproj", [dq_d, dk_d, dv_d, dq_f, dk_f, dv_f], [dg_d, dg_f], cos_t, sin_t, hh, d)
    (dhm_f,) = _mm("dhm_f", [(df, w["w_in_f"])], "nt", s, d, LANES, epilogue=ident, out_dtypes=(F32,))
    (dhm,), from_chips = _mm("dhm", [(dproj, w["w_in"])], "nt", s, d, n_proj,
                             epilogue=lambda accs, ex, rw: (accs[0] + ex[0],), out_dtypes=(F32,),
                             extras=[(dhm_f, 0)], side=_scatter_side(sums2))
    halves2 = core_halves(FFN2, sums2, from_chips)
    (dw_in,) = _mm("dw_in", [(hm, dproj)], "tn", d, n_proj, s, epilogue=ident, out_dtypes=(F32,))
    (dw_in_f,) = _mm("dw_in_f", [(hm, df)], "tn", d, LANES, s, epilogue=ident, out_dtypes=(F32,))
    dw_in = _col_pieces(_unpack_dw_in(dw_in, dw_in_f, hd, d))
    dx1, dx1_bf, dg_mix = _rms_bwd("mix_rmsb", x1, small["mix_norm"], dhm, dx2)
    pieces_m = [dw_in, dw_pd, dw_pf, dw_out]
    sums_m = []

    def scatter_mixer(from_sibling):
        sums_m.extend(chip_sums(MIXER, pieces_m, from_sibling))
        return _scatter_side(sums_m)

    dx0, _, dg_ffn1, dw_g1, dw_u1, dw_d1, from_chips = _ffn_bwd(
        "ffn1", x, small["ffn1_norm"], w["ffn1_w_gate"], w["ffn1_w_up"], w["ffn1_w_down"], saved1, dx1, dx1_bf,
        side_dact=_swap_side(pieces_m), side_dh=scatter_mixer)
    halves_m = core_halves(MIXER, sums_m, from_chips)

    pieces1 = [dw_g1, dw_u1, dw_d1]
    sums1 = chip_sums(FFN1, pieces1, _swap_side(pieces1).call("rs_swap_ffn1"))
    halves1 = core_halves(FFN1, sums1, _scatter_side(sums1).call("rs_scatter_ffn1"))

    totals = _join_side(halves1 + halves_m + halves2).call("rs_join")
    grads = dict(zip(FFN1 + MIXER + FFN2, totals))
    partials = {"ffn1_norm": dg_ffn1, "mix_norm": dg_mix, "ffn2_norm": dg_ffn2, "final_norm": dg_final,
                "b_gate_dil": dbg_d.reshape(-1, d), "b_gate_fox": dbg_f.reshape(-1, d), "b_forget": db_forget, "sq": sq}
    return dx0, grads, partials


def _coords():
    return lax.axis_index("x"), lax.axis_index("y"), lax.axis_index("c")


def _other_chips(x, y):
    return [(1 - x, y), (x, 1 - y), (1 - x, 1 - y)]


ANY_SPEC = pl.BlockSpec(memory_space=pl.ANY)


def _gather_side(shards):
    nw = len(shards)

    def copies(srcs, outs, send_sems, recv_sems):
        x, y, c = _coords()
        chips = _other_chips(x, y)

        def slot(w, px, py, pc):
            half = shards[w].shape[0] // 2
            return outs[w].at[2 * px + py, pl.ds(pc * half, half), :]

        def copy(w, k, src_ref, dst_ref, to):
            return pltpu.make_async_remote_copy(src_ref=src_ref, dst_ref=dst_ref, send_sem=send_sems.at[6 * w + k],
                                                recv_sem=recv_sems.at[6 * w + k], device_id=to, device_id_type=MESH)

        first, arrive, passed, arrive2 = [], [], [], []
        for w in range(nw):
            half = shards[w].shape[0] // 2
            for j, chip in enumerate(chips):
                first.append(copy(w, j, srcs[w].at[pl.ds(c * half, half), :], slot(w, x, y, c), (*chip, c)))
                arrive.append(copy(w, j, slot(w, *chip, c), slot(w, *chip, c), (*chip, c)))
                passed.append(copy(w, 3 + j, slot(w, *chip, c), slot(w, *chip, c), (x, y, 1 - c)))
                arrive2.append(copy(w, 3 + j, slot(w, *chip, 1 - c), slot(w, *chip, 1 - c), (x, y, 1 - c)))
        return first, arrive, passed, arrive2

    def start(srcs, outs, send_sems, recv_sems):
        for cp in copies(srcs, outs, send_sems, recv_sems)[0]:
            cp.start()

    def finish(srcs, outs, send_sems, recv_sems):
        first, arrive, passed, arrive2 = copies(srcs, outs, send_sems, recv_sems)
        for got, fwd in zip(arrive, passed):
            got.wait_recv()
            fwd.start()
        for got in arrive2:
            got.wait_recv()
        for cp in first + passed:
            cp.wait_send()

    return _Side(shards, [jax.ShapeDtypeStruct((N_CHIPS, *t.shape), t.dtype) for t in shards], 6 * nw, start, finish)


def _swap_side(grads):
    nw = len(grads)

    def copies(srcs, outs, send_sems, recv_sems):
        x, y, c = _coords()
        res = []
        for w in range(nw):
            half = grads[w].shape[1] // 2
            for p in range(N_CHIPS):
                k = N_CHIPS * w + p
                res.append(pltpu.make_async_remote_copy(
                    src_ref=srcs[w].at[p, pl.ds((1 - c) * half, half), :], dst_ref=outs[w].at[p],
                    send_sem=send_sems.at[k], recv_sem=recv_sems.at[k], device_id=(x, y, 1 - c), device_id_type=MESH))
        return res

    def start(*refs):
        for cp in copies(*refs):
            cp.start()

    def finish(*refs):
        for cp in copies(*refs):
            cp.wait()

    shapes = [jax.ShapeDtypeStruct((N_CHIPS, t.shape[1] // 2, t.shape[2]), t.dtype) for t in grads]
    return _Side(grads, shapes, N_CHIPS * nw, start, finish)


def _rs_add(name, ids, g, other):
    n, rows, cols = g.shape
    half = rows // 2
    tr = _pick(half, 256, 16)
    nb = half // tr

    def body(ids_ref, g_ref, o_ref, out_ref):
        out_ref[...] = (g_ref[...] + o_ref[...]).astype(BF)

    grid_spec = pltpu.PrefetchScalarGridSpec(
        num_scalar_prefetch=1, grid=(n, nb),
        in_specs=[pl.BlockSpec((None, tr, cols), lambda p, i, ids_ref: (p, ids_ref[1] * nb + i, 0)),
                  pl.BlockSpec((None, tr, cols), lambda p, i, ids_ref: (p, i, 0))],
        out_specs=pl.BlockSpec((None, tr, cols), lambda p, i, ids_ref: (p, i, 0)))
    return pl.pallas_call(body, grid_spec=grid_spec, out_shape=jax.ShapeDtypeStruct((n, half, cols), BF),
                          compiler_params=_params(("parallel", "parallel")), name=name)(ids, g, other)


def _scatter_side(sums):
    nw = len(sums)

    def copies(srcs, outs, send_sems, recv_sems):
        x, y, c = _coords()
        res = []
        for w in range(nw):
            for k, (px, py) in enumerate(_other_chips(x, y)):
                res.append(pltpu.make_async_remote_copy(
                    src_ref=srcs[w].at[2 * px + py], dst_ref=outs[w].at[k], send_sem=send_sems.at[3 * w + k],
                    recv_sem=recv_sems.at[3 * w + k], device_id=(px, py, c), device_id_type=MESH))
        return res

    def start(*refs):
        for cp in copies(*refs):
            cp.start()

    def finish(*refs):
        for cp in copies(*refs):
            cp.wait()

    return _Side(sums, [jax.ShapeDtypeStruct((3, *t.shape[1:]), t.dtype) for t in sums], 3 * nw, start, finish)


def _rs_sum(name, ids, own, got):
    n, half, cols = own.shape
    tr = _pick(half, 256, 16)
    nb = half // tr

    def body(ids_ref, own_ref, got_ref, out_ref):
        t = own_ref[...].astype(F32)
        for k in range(3):
            t = t + got_ref[k].astype(F32)
        out_ref[...] = t

    grid_spec = pltpu.PrefetchScalarGridSpec(
        num_scalar_prefetch=1, grid=(nb,),
        in_specs=[pl.BlockSpec((None, tr, cols), lambda i, ids_ref: (ids_ref[0], i, 0)),
                  pl.BlockSpec((3, tr, cols), lambda i, ids_ref: (0, i, 0))],
        out_specs=pl.BlockSpec((tr, cols), lambda i, ids_ref: (ids_ref[1] * nb + i, 0)))
    return pl.pallas_call(body, grid_spec=grid_spec, out_shape=jax.ShapeDtypeStruct((2 * half, cols), F32),
                          compiler_params=_params(("parallel",)), name=name)(ids, own, got)


def _join_side(totals):
    nw = len(totals)

    def start(ins, bufs, send_sems, recv_sems):
        x, y, c = _coords()
        for w in range(nw):
            half = totals[w].shape[0] // 2
            pltpu.make_async_remote_copy(
                src_ref=bufs[w].at[pl.ds(c * half, half), :], dst_ref=bufs[w].at[pl.ds(c * half, half), :],
                send_sem=send_sems.at[w], recv_sem=recv_sems.at[w], device_id=(x, y, 1 - c),
                device_id_type=MESH).start()

    def finish(ins, bufs, send_sems, recv_sems):
        x, y, c = _coords()
        for w in range(nw):
            half = totals[w].shape[0] // 2
            arrival = pltpu.make_async_remote_copy(
                src_ref=bufs[w].at[pl.ds(c * half, half), :], dst_ref=bufs[w].at[pl.ds((1 - c) * half, half), :],
                send_sem=send_sems.at[w], recv_sem=recv_sems.at[w], device_id=(x, y, 1 - c), device_id_type=MESH)
            arrival.wait_recv()
            arrival.wait_send()

    return _Side(totals, [jax.ShapeDtypeStruct(t.shape, t.dtype) for t in totals], nw, start, finish,
                 aliases={w: w for w in range(nw)})


def _gather_all(name, t):
    rows, cols = t.shape

    def body(src, out, send_sems, recv_sems, local_sem):
        x, y, c = _coords()
        me = 4 * x + 2 * y + c
        mine = pltpu.make_async_copy(src, out.at[me], local_sem)
        mine.start()
        peers = [(x ^ (k >> 2 & 1), y ^ (k >> 1 & 1), c ^ (k & 1)) for k in range(1, N_DEV)]
        sends = [pltpu.make_async_remote_copy(src_ref=src, dst_ref=out.at[me], send_sem=send_sems.at[k],
                                              recv_sem=recv_sems.at[k], device_id=peer, device_id_type=MESH)
                 for k, peer in enumerate(peers)]
        for cp in sends:
            cp.start()
        for k, (px, py, pc) in enumerate(peers):
            pltpu.make_async_remote_copy(src_ref=src, dst_ref=out.at[4 * px + 2 * py + pc], send_sem=send_sems.at[k],
                                         recv_sem=recv_sems.at[k], device_id=(px, py, pc),
                                         device_id_type=MESH).wait_recv()
        for cp in sends:
            cp.wait_send()
        mine.wait()

    vmem = pl.BlockSpec(memory_space=pltpu.VMEM)
    return pl.pallas_call(
        body, in_specs=[vmem], out_specs=vmem, out_shape=jax.ShapeDtypeStruct((N_DEV, rows, cols), t.dtype),
        scratch_shapes=[pltpu.SemaphoreType.DMA((7,)), pltpu.SemaphoreType.DMA((7,)), pltpu.SemaphoreType.DMA],
        name=name)(t)


def _adamw_math(w, g, m, v):
    m = ADAM_B1 * m + (1.0 - ADAM_B1) * g
    v = ADAM_B2 * v + (1.0 - ADAM_B2) * (g * g)
    m_hat = m / (1.0 - ADAM_B1 ** ADAM_STEP)
    v_hat = v / (1.0 - ADAM_B2 ** ADAM_STEP)
    delta = -ADAM_LR * (m_hat / (jnp.sqrt(v_hat) + ADAM_EPS) + ADAM_WD * w)
    return delta, m, v


def _adamw(name, w, g, m, v):
    rows, cols = w.shape
    tr = _pick(rows, 256, 8)

    def body(w_ref, g_ref, m_ref, v_ref, d_out, m_out, v_out):
        d_out[...], m_out[...], v_out[...] = _adamw_math(w_ref[...], g_ref[...], m_ref[...], v_ref[...])

    blk = pl.BlockSpec((tr, cols), lambda i: (i, 0))
    shape = jax.ShapeDtypeStruct((rows, cols), F32)
    return pl.pallas_call(body, grid=(rows // tr,), in_specs=[blk] * 4, out_specs=[blk] * 3, out_shape=[shape] * 3,
                          compiler_params=_params(("parallel",)), name=name)(w, g, m, v)


def _small_reduce(name, parts, width):
    def body(*refs):
        out = refs[-1]
        out[...] = jnp.zeros_like(out)
        for k, r in enumerate(refs[:-1]):
            out[pl.ds(k, 1), :] = jnp.sum(r[...], axis=0, keepdims=True)

    vmem = pl.BlockSpec(memory_space=pltpu.VMEM)
    return pl.pallas_call(body, in_specs=[vmem] * len(parts), out_specs=vmem,
                          out_shape=jax.ShapeDtypeStruct((8, width), F32), name=name)(*parts)


def _small_adamw(name, gathered, w, m, v, loss_row, loss_scale):
    def body(gt_ref, w_ref, m_ref, v_ref, g_out, d_out, m_out, v_out, loss_out):
        g = gt_ref[0]
        for k in range(1, N_DEV):
            g = g + gt_ref[k]
        g_out[...] = g
        row = lax.broadcasted_iota(jnp.int32, g.shape, 0)
        loss_out[...] = jnp.sum(jnp.where(row == loss_row, g, 0.0), keepdims=True) * loss_scale
        d_out[...], m_out[...], v_out[...] = _adamw_math(w_ref[...], g, m_ref[...], v_ref[...])

    vmem = pl.BlockSpec(memory_space=pltpu.VMEM)
    shape = jax.ShapeDtypeStruct(w.shape, F32)
    return pl.pallas_call(body, in_specs=[vmem] * 4, out_specs=[vmem] * 5,
                          out_shape=[shape] * 4 + [jax.ShapeDtypeStruct((1, 1), F32)], name=name)(gathered, w, m, v)


def _full_from_pieces(name, pieces):
    _, rows, cols = pieces.shape
    if name in ROW_SHARDED:
        return pieces.reshape(N_CHIPS * rows, cols)
    return pieces.transpose(1, 0, 2).reshape(rows, N_CHIPS * cols)


def _repack_w_in(w_in, hd, d):
    hh = hd // HEAD_DIM
    qkv, f, gates = w_in[:, :6 * hd], w_in[:, 6 * hd:6 * hd + hh], w_in[:, 6 * hd + hh:]
    return jnp.concatenate([qkv, gates], axis=1), jnp.pad(f, ((0, 0), (0, LANES - hh)))


def _unpack_dw_in(dw, dw_f, hd, d):
    hh = hd // HEAD_DIM
    return jnp.concatenate([dw[:, :6 * hd], dw_f[:, :hh], dw[:, 6 * hd:]], axis=1)


def _small_pack(vals, width):
    rows = []
    for name in SMALL:
        t = vals[name].reshape(1, -1)
        rows.append(jnp.pad(t, ((0, 0), (0, width - t.shape[1]))))
    rows.append(jnp.zeros((8 - len(SMALL), width), F32))
    return jnp.concatenate(rows, axis=0)


def kernel(x, ffn1_norm, ffn1_w_gate, ffn1_w_up, ffn1_w_down, mix_norm, w_in, b_forget, b_gate_dil, b_gate_fox, w_proj_dil, w_proj_fox, w_out, ffn2_norm, ffn2_w_gate, ffn2_w_up, ffn2_w_down, final_norm, loss_target, m_ffn1_norm, m_ffn1_w_gate, m_ffn1_w_up, m_ffn1_w_down, m_mix_norm, m_w_in, m_b_forget, m_b_gate_dil, m_b_gate_fox, m_w_proj_dil, m_w_proj_fox, m_w_out, m_ffn2_norm, m_ffn2_w_gate, m_ffn2_w_up, m_ffn2_w_down, m_final_norm, v_ffn1_norm, v_ffn1_w_gate, v_ffn1_w_up, v_ffn1_w_down, v_mix_norm, v_w_in, v_b_forget, v_b_gate_dil, v_b_gate_fox, v_w_proj_dil, v_w_proj_fox, v_w_out, v_ffn2_norm, v_ffn2_w_gate, v_ffn2_w_up, v_ffn2_w_down, v_final_norm):
    given = dict(locals())
    wts = {n: given[n] for n in WEIGHTS}
    mom_m = {n: given["m_" + n] for n in WEIGHTS}
    mom_v = {n: given["v_" + n] for n in WEIGHTS}
    d = x.shape[2]

    shards = {n: wts[n][0].astype(BF) for n in SHARDED}
    small = {n: wts[n] for n in SMALL}
    grad_x, grads, partials = _device_step(x[0], loss_target[0], shards, small)

    out_g, out_d, out_m, out_v = {}, {}, {}, {}
    for n in SHARDED:
        g = grads[n]
        dl, nm, nv = _adamw("adamw_" + n, wts[n][0], g, mom_m[n][0], mom_v[n][0])
        out_g[n], out_d[n], out_m[n], out_v[n] = g[None], dl[None], nm[None], nv[None]

    width = d
    part_rows = []
    for n in SMALL:
        t = partials[n]
        part_rows.append(jnp.pad(t, ((0, 0), (0, width - t.shape[1]))))
    part_rows.append(partials["sq"])
    local_small = _small_reduce("small_reduce", part_rows, width)
    gathered_small = _gather_all("small_gather", local_small)
    sg, sd_, sm, sv, loss = _small_adamw("small_adamw", gathered_small, _small_pack(wts, width),
                                         _small_pack(mom_m, width), _small_pack(mom_v, width), len(SMALL), 0.5 / d)
    for k, n in enumerate(SMALL):
        shp = wts[n].shape
        take = lambda t: t[k, :shp[-1]].reshape(shp)
        out_g[n], out_d[n], out_m[n], out_v[n] = take(sg), take(sd_), take(sm), take(sv)
    return (loss[0, 0], grad_x[None], *[out_g[n] for n in WEIGHTS], *[out_d[n] for n in WEIGHTS],
            *[out_m[n] for n in WEIGHTS], *[out_v[n] for n in WEIGHTS])
```

```python
import functools
import math

import numpy as np
import jax
import jax.numpy as jnp
from jax import lax
from jax.experimental import pallas as pl
from jax.experimental.pallas import tpu as pltpu

HEAD_DIM = 128
ROPE_DIM = HEAD_DIM // 4
ROPE_THETA = 500000.0
DIL_PATTERNS = ((128, 1), (512, 4), (2048, 16))
MAX_WINDOW = 2048
NORM_EPS = 1e-6
ADAM_LR = 0.001
ADAM_B1 = 0.9
ADAM_B2 = 0.999
ADAM_EPS = 1e-08
ADAM_WD = 0.01
ADAM_STEP = 10

BF = jnp.bfloat16
F32 = jnp.float32
NEG = -1e30
LANES = 128
ATT_BLOCK = 512
PREP_COLS = 512
ATT_HEADS = 2
VMEM_LIMIT = 56 * 1024 * 1024
MM_VMEM_BUDGET = 40 * 1024 * 1024
N_CHIPS = 4
N_DEV = 8
MESH = pl.DeviceIdType.MESH

SHARDED = ("ffn1_w_gate", "ffn1_w_up", "ffn1_w_down", "w_in", "w_proj_dil", "w_proj_fox", "w_out",
           "ffn2_w_gate", "ffn2_w_up", "ffn2_w_down")
ROW_SHARDED = ("ffn1_w_down", "w_out", "ffn2_w_down")
SMALL = ("ffn1_norm", "mix_norm", "b_forget", "b_gate_dil", "b_gate_fox", "ffn2_norm", "final_norm")
WEIGHTS = ("ffn1_norm", "ffn1_w_gate", "ffn1_w_up", "ffn1_w_down", "mix_norm", "w_in", "b_forget",
           "b_gate_dil", "b_gate_fox", "w_proj_dil", "w_proj_fox", "w_out", "ffn2_norm", "ffn2_w_gate",
           "ffn2_w_up", "ffn2_w_down", "final_norm")


def _pick(n, target, align):
    best = None
    for d in range(align, min(n, target) + 1, align):
        if n % d == 0:
            best = d
    return n if best is None else best


def _params(sem=None):
    return pltpu.CompilerParams(dimension_semantics=sem, vmem_limit_bytes=VMEM_LIMIT)


_DIMS = {"nn": (((1,), (0,)), ((), ())), "nt": (((1,), (1,)), ((), ())), "tn": (((0,), (0,)), ((), ()))}


class _SemsFrom:
    def __init__(self, sems, first):
        self.sems, self.first = sems, first

    @property
    def at(self):
        return self

    def __getitem__(self, k):
        return self.sems.at[self.first + k]


class _Side:
    def __init__(self, inputs, out_shapes, n_sems, start, finish, aliases=None):
        self.inputs, self.out_shapes, self.n_sems = list(inputs), list(out_shapes), n_sems
        self.start, self.finish, self.aliases = start, finish, aliases or {}

    def scratch(self):
        return [pltpu.SemaphoreType.DMA((self.n_sems,)), pltpu.SemaphoreType.DMA((self.n_sems,))]

    def beside(self, other):
        n_in, n_out, n_sems = len(self.inputs), len(self.out_shapes), self.n_sems

        def part(fn_a, fn_b):
            def run(ins, outs, send_sems, recv_sems):
                fn_a(ins[:n_in], outs[:n_out], send_sems, recv_sems)
                fn_b(ins[n_in:], outs[n_out:], _SemsFrom(send_sems, n_sems), _SemsFrom(recv_sems, n_sems))
            return run

        assert not self.aliases and not other.aliases
        return _Side(self.inputs + other.inputs, self.out_shapes + other.out_shapes, n_sems + other.n_sems,
                     part(self.start, other.start), part(self.finish, other.finish))

    def call(self, name):
        n_in, n_out = len(self.inputs), len(self.out_shapes)

        def body(*refs):
            ins, outs, sems = refs[:n_in], refs[n_in:n_in + n_out], refs[n_in + n_out:]
            self.start(ins, outs, *sems)
            self.finish(ins, outs, *sems)

        return pl.pallas_call(body, in_specs=[ANY_SPEC] * n_in, out_specs=[ANY_SPEC] * n_out, out_shape=self.out_shapes,
                              input_output_aliases=self.aliases, scratch_shapes=self.scratch(), name=name)(*self.inputs)


def _mm(name, pairs, mode, m, n, k, *, epilogue, out_dtypes, extras=(), rows=(), n_colsum=0,
        sum_pairs=False, tm=1024, tn=1152, tk=2048, piece_layout=False, side=None):
    m_align = LANES if mode == "tn" else 8
    tm = _pick(m, tm, m_align)
    tn = n // N_CHIPS if piece_layout else _pick(n, tn, LANES)
    tk = _pick(k, tk, LANES)
    n_acc = 1 if sum_pairs else len(pairs)
    lhs = []
    for a, _ in pairs:
        if not any(a is t for t in lhs):
            lhs.append(a)
    lhs_of = [next(t for t in range(len(lhs)) if lhs[t] is a) for a, _ in pairs]
    n_mm = len(lhs) + len(pairs)
    n_in = n_mm + len(extras) + len(rows)
    n_out = len(out_dtypes) + n_colsum

    def vmem_bytes(tm_, tn_, tk_):
        tiles = sum(tm_ * tk_ * a.dtype.itemsize for a in lhs) + sum(tn_ * tk_ * b.dtype.itemsize for _, b in pairs)
        tiles += sum(tm_ * tn_ * arr.dtype.itemsize for arr, _ in extras)
        tiles += sum(tm_ * tn_ * jnp.dtype(dt).itemsize for dt in out_dtypes)
        return 2 * tiles + (n_acc + len(extras) + len(out_dtypes)) * tm_ * tn_ * 4

    while vmem_bytes(tm, tn, tk) > MM_VMEM_BUDGET:
        if tn > 512 and not piece_layout:
            tn = _pick(n, tn - LANES, LANES)
        elif tm > 512:
            tm = _pick(m, tm - m_align, m_align)
        elif tk > 512:
            tk = _pick(k, tk - LANES, LANES)
        elif tm > 256:
            tm = _pick(m, tm - m_align, m_align)
        else:
            break
    nk = k // tk

    n_side_in = len(side.inputs) if side else 0
    n_side_out = len(side.out_shapes) if side else 0
    grid = (m // tm, n // tn, nk)

    def body(*refs):
        ins, refs = refs[:n_in], refs[n_in:]
        side_ins, refs = refs[:n_side_in], refs[n_side_in:]
        outs, refs = refs[:n_out], refs[n_out:]
        side_outs, refs = refs[:n_side_out], refs[n_side_out:]
        accs, side_sems = refs[:n_acc], refs[n_acc:]
        kk = pl.program_id(2)
        if side:
            at = [pl.program_id(t) for t in range(3)]

            @pl.when(jnp.logical_and(jnp.logical_and(at[0] == 0, at[1] == 0), at[2] == 0))
            def _():
                side.start(side_ins, side_outs, *side_sems)

        @pl.when(kk == 0)
        def _():
            for acc in accs:
                acc[...] = jnp.zeros_like(acc)

        a_tiles = [r[...].astype(BF) for r in ins[:len(lhs)]]
        for p in range(len(pairs)):
            b = ins[len(lhs) + p][...].astype(BF)
            accs[0 if sum_pairs else p][...] += lax.dot_general(a_tiles[lhs_of[p]], b, _DIMS[mode],
                                                                preferred_element_type=F32)

        @pl.when(kk == nk - 1)
        def _():
            ex = [r[...] for r in ins[n_mm:n_mm + len(extras)]]
            rw = [r[...] for r in ins[n_mm + len(extras):]]
            res = epilogue([acc[...] for acc in accs], ex, rw)
            for o, r in zip(outs, res):
                o[...] = r.astype(o.dtype)

        if side:
            @pl.when(jnp.logical_and(jnp.logical_and(at[0] == grid[0] - 1, at[1] == grid[1] - 1), at[2] == nk - 1))
            def _():
                side.finish(side_ins, side_outs, *side_sems)

    in_specs, args = [], []
    for a in lhs:
        if mode == "tn":
            in_specs.append(pl.BlockSpec((tk, tm), lambda i, j, kk: (kk, i)))
        else:
            in_specs.append(pl.BlockSpec((tm, tk), lambda i, j, kk: (i, kk)))
        args.append(a)
    for _, b in pairs:
        if mode == "nt":
            in_specs.append(pl.BlockSpec((tn, tk), lambda i, j, kk: (j, kk)))
        else:
            in_specs.append(pl.BlockSpec((tk, tn), lambda i, j, kk: (kk, j)))
        args.append(b)
    for arr, off in extras:
        assert off % tn == 0
        in_specs.append(pl.BlockSpec((tm, tn), functools.partial(lambda i, j, kk, o: (i, j + o), o=off // tn)))
        args.append(arr)
    for arr in rows:
        in_specs.append(pl.BlockSpec((1, tn), lambda i, j, kk: (0, j)))
        args.append(arr)
    if piece_layout:
        out_specs = [pl.BlockSpec((None, tm, tn), lambda i, j, kk: (j, i, 0)) for _ in out_dtypes]
        out_shape = [jax.ShapeDtypeStruct((n // tn, m, tn), d) for d in out_dtypes]
    else:
        out_specs = [pl.BlockSpec((tm, tn), lambda i, j, kk: (i, j)) for _ in out_dtypes]
        out_shape = [jax.ShapeDtypeStruct((m, n), d) for d in out_dtypes]
    for _ in range(n_colsum):
        out_specs.append(pl.BlockSpec((None, 1, tn), lambda i, j, kk: (i, 0, j)))
        out_shape.append(jax.ShapeDtypeStruct((m // tm, 1, n), F32))
    scratch = [pltpu.VMEM((tm, tn), F32) for _ in range(n_acc)]
    if side is None:
        return pl.pallas_call(
            body, grid=grid, in_specs=in_specs, out_specs=out_specs, out_shape=out_shape, scratch_shapes=scratch,
            compiler_params=_params(("parallel", "parallel", "arbitrary")), name=name)(*args)
    res = pl.pallas_call(
        body, grid=grid, in_specs=in_specs + [ANY_SPEC] * n_side_in, out_specs=out_specs + [ANY_SPEC] * n_side_out,
        out_shape=out_shape + side.out_shapes, scratch_shapes=scratch + side.scratch(),
        input_output_aliases={n_in + t: n_out + o for t, o in side.aliases.items()},
        compiler_params=_params(("arbitrary", "arbitrary", "arbitrary")), name=name)(*args, *side.inputs)
    return res[:n_out], res[n_out:]


def _col_pieces(full):
    rows, cols = full.shape
    return full.reshape(rows, N_CHIPS, cols // N_CHIPS).transpose(1, 0, 2)


def _dw_col_pieces(name, a, b, m, n, k, side=None):
    ident = lambda accs, ex, rw: (accs[0],)
    aligned = (n // N_CHIPS) % LANES == 0
    res = _mm(name, [(a, b)], "tn", m, n, k, epilogue=ident, out_dtypes=(F32,), tm=512 if aligned else 1024,
              piece_layout=aligned, side=side)
    (out,), side_res = res if side else (res, None)
    out = out if aligned else _col_pieces(out)
    return (out, side_res) if side else out


def _hosted(sides, got, key, call):
    make = sides.get(key) if sides else None
    if make is None:
        return call(None)
    outs, got[key] = call(make(got))
    return outs


def _sigmoid(z):
    return 0.5 * jnp.tanh(0.5 * z) + 0.5


def _row_tile(s):
    return _pick(s, 256, 8)


def _fold8(t):
    r, d = t.shape
    return jnp.sum(t.reshape(r // 8, 8, d), axis=0)


def _rms_fwd(name, x, g):
    s, d = x.shape
    tr = _row_tile(s)

    def body(x_ref, g_ref, h_ref):
        xf = x_ref[...]
        y = xf * lax.rsqrt(jnp.mean(xf * xf, axis=-1, keepdims=True) + NORM_EPS)
        h_ref[...] = (y * g_ref[...]).astype(BF)

    return pl.pallas_call(
        body, grid=(s // tr,),
        in_specs=[pl.BlockSpec((tr, d), lambda i: (i, 0)), pl.BlockSpec((1, d), lambda i: (0, 0))],
        out_specs=pl.BlockSpec((tr, d), lambda i: (i, 0)), out_shape=jax.ShapeDtypeStruct((s, d), BF),
        compiler_params=_params(("parallel",)), name=name)(x, g)


def _rms_bwd(name, x, g, dh, dres):
    s, d = x.shape
    tr = _row_tile(s)

    def body(x_ref, g_ref, dh_ref, dres_ref, dx_ref, dxb_ref, dg_ref):
        @pl.when(pl.program_id(0) == 0)
        def _():
            dg_ref[...] = jnp.zeros_like(dg_ref)

        xf = x_ref[...]
        rstd = lax.rsqrt(jnp.mean(xf * xf, axis=-1, keepdims=True) + NORM_EPS)
        xhat = xf * rstd
        dhf = dh_ref[...]
        dg_ref[...] += _fold8(dhf * xhat)
        dxh = dhf * g_ref[...]
        dx = dres_ref[...] + rstd * (dxh - xhat * jnp.mean(dxh * xhat, axis=-1, keepdims=True))
        dx_ref[...] = dx
        dxb_ref[...] = dx.astype(BF)

    blk = pl.BlockSpec((tr, d), lambda i: (i, 0))
    return pl.pallas_call(
        body, grid=(s // tr,),
        in_specs=[blk, pl.BlockSpec((1, d), lambda i: (0, 0)), blk, blk],
        out_specs=[blk, blk, pl.BlockSpec((8, d), lambda i: (0, 0))],
        out_shape=[jax.ShapeDtypeStruct((s, d), F32), jax.ShapeDtypeStruct((s, d), BF),
                   jax.ShapeDtypeStruct((8, d), F32)],
        compiler_params=_params(("arbitrary",)), name=name)(x, g, dh, dres)


def _final(name, x, g, tgt):
    s, d = x.shape
    tr = _row_tile(s)

    def body(x_ref, g_ref, t_ref, dx_ref, dxb_ref, dg_ref, sq_ref):
        @pl.when(pl.program_id(0) == 0)
        def _():
            dg_ref[...] = jnp.zeros_like(dg_ref)
            sq_ref[...] = jnp.zeros_like(sq_ref)

        xf = x_ref[...]
        rstd = lax.rsqrt(jnp.mean(xf * xf, axis=-1, keepdims=True) + NORM_EPS)
        xhat = xf * rstd
        gf = g_ref[...]
        err = xhat * gf - t_ref[...]
        sq_ref[...] += _fold8(err * err)
        dy = err * (1.0 / d)
        dg_ref[...] += _fold8(dy * xhat)
        dxh = dy * gf
        dx = rstd * (dxh - xhat * jnp.mean(dxh * xhat, axis=-1, keepdims=True))
        dx_ref[...] = dx
        dxb_ref[...] = dx.astype(BF)

    blk = pl.BlockSpec((tr, d), lambda i: (i, 0))
    acc = pl.BlockSpec((8, d), lambda i: (0, 0))
    return pl.pallas_call(
        body, grid=(s // tr,), in_specs=[blk, pl.BlockSpec((1, d), lambda i: (0, 0)), blk],
        out_specs=[blk, blk, acc, acc],
        out_shape=[jax.ShapeDtypeStruct((s, d), F32), jax.ShapeDtypeStruct((s, d), BF),
                   jax.ShapeDtypeStruct((8, d), F32), jax.ShapeDtypeStruct((8, d), F32)],
        compiler_params=_params(("arbitrary",)), name=name)(x, g, tgt)


def _ffn_fwd(tag, x, g, w_gate, w_up, get_w_down, sides=None):
    s, d = x.shape
    f = w_gate.shape[1]
    got = {}
    h = _rms_fwd(tag + "_rms", x, g)

    def up_epi(accs, ex, rw):
        a, b = accs
        return a, b, a * _sigmoid(a) * b

    a, b, act = _hosted(sides, got, "up", lambda side: _mm(
        tag + "_up", [(h, w_gate), (h, w_up)], "nn", s, f, d, epilogue=up_epi, out_dtypes=(BF, BF, BF), side=side))

    def down_epi(accs, ex, rw):
        return (ex[0] + 0.5 * accs[0],)

    w_down = get_w_down(got)
    (y,) = _hosted(sides, got, "down", lambda side: _mm(
        tag + "_down", [(act, w_down)], "nn", s, d, f, epilogue=down_epi, out_dtypes=(F32,), extras=[(x, 0)],
        side=side))
    return y, (h, a, b, act), got


def _ffn_bwd(tag, x, g, w_gate, w_up, w_down, saved, dy, dy_bf, sides=None):
    s, d = x.shape
    f = w_gate.shape[1]
    h, a, b, act = saved
    got = {}

    def act_epi(accs, ex, rw):
        dact = 0.5 * accs[0]
        av, bv = ex[0].astype(F32), ex[1].astype(F32)
        sg = _sigmoid(av)
        return dact * bv * (sg * (1.0 + av * (1.0 - sg))), dact * (av * sg)

    da, db = _hosted(sides, got, "dact", lambda side: _mm(
        tag + "_dact", [(dy_bf, w_down)], "nt", s, f, d, epilogue=act_epi, out_dtypes=(BF, BF),
        extras=[(a, 0), (b, 0)], side=side))
    ident = lambda accs, ex, rw: (accs[0],)
    (dw_down,) = _hosted(sides, got, "dwd", lambda side: _mm(
        tag + "_dwd", [(act, dy_bf)], "tn", f, d, s, epilogue=lambda accs, ex, rw: (0.5 * accs[0],),
        out_dtypes=(F32,), side=side))
    got["dw_down"] = dw_down = dw_down.reshape(N_CHIPS, f // N_CHIPS, d)
    got["dw_gate"] = dw_gate = _hosted(sides, got, "dwg", lambda side: _dw_col_pieces(
        tag + "_dwg", h, da, d, f, s, side=side))
    got["dw_up"] = dw_up = _hosted(sides, got, "dwu", lambda side: _dw_col_pieces(
        tag + "_dwu", h, db, d, f, s, side=side))
    (dh,) = _hosted(sides, got, "dh", lambda side: _mm(
        tag + "_dh", [(da, w_gate), (db, w_up)], "nt", s, d, f, epilogue=ident, out_dtypes=(F32,), sum_pairs=True,
        side=side))
    dx, dx_bf, dg = _rms_bwd(tag + "_rmsb", x, g, dh, dy)
    return dx, dx_bf, dg, dw_gate, dw_up, dw_down, got


def _rope_tables(s):
    half = ROPE_DIM // 2
    pos = jnp.arange(s, dtype=F32)
    inv_freq = ROPE_THETA ** (-jnp.arange(0, ROPE_DIM, 2, dtype=F32) / ROPE_DIM)
    ang = pos[:, None] * inv_freq[None, :]
    cos, sin = jnp.cos(ang), jnp.sin(ang)
    rest = HEAD_DIM - ROPE_DIM
    cos_t = jnp.concatenate([cos, cos, jnp.ones((s, rest), F32)], axis=-1)
    sin_t = jnp.concatenate([-sin, sin, jnp.zeros((s, rest), F32)], axis=-1)
    return cos_t, sin_t


def _swap_halves(t):
    lane = lax.broadcasted_iota(jnp.int32, t.shape, 1) & (HEAD_DIM - 1)
    half = ROPE_DIM // 2
    return jnp.where(lane < half, pltpu.roll(t, t.shape[1] - half, 1), pltpu.roll(t, half, 1))


def _dil_bias(blk):
    n_delta = MAX_WINDOW // blk + 1
    delta = jnp.arange(n_delta, dtype=jnp.int32)[:, None, None]
    r = jnp.arange(blk, dtype=jnp.int32)[None, None, :]
    c = jnp.arange(blk, dtype=jnp.int32)[None, :, None]
    o = delta * blk + r - c
    mult = jnp.zeros(o.shape, F32)
    for w, dd in DIL_PATTERNS:
        mult = mult + ((o >= 0) & (o <= w) & (o % dd == 0)).astype(F32)
    return jnp.where(mult > 0, jnp.log(jnp.maximum(mult, 1.0)), NEG)


def _att_block(s):
    return _pick(s, ATT_BLOCK, LANES)


def _fox_aug(name, c_pad, qkv, q_off, k_off, n_heads):
    s = c_pad.shape[0]
    tr = _pick(s, 1024, 16)

    def body(c_ref, q_ref, k_ref, qc_ref, kc_ref):
        h = pl.program_id(1)
        lane = lax.broadcasted_iota(jnp.int32, (tr, LANES), 1)
        ch = jnp.sum(jnp.where(lane == h, c_ref[...], 0.0), axis=1, keepdims=True)
        hi, mid, lo = (t.astype(F32) for t in _split3(ch))
        zero = jnp.zeros((tr, LANES), F32)
        is_hi = jnp.logical_or(lane == 0, lane == 3)
        is_mid = jnp.logical_or(lane == 1, lane == 4)
        parts = jnp.where(is_hi, hi, jnp.where(is_mid, mid, lo))
        qc_ref[:, :HEAD_DIM] = q_ref[...]
        kc_ref[:, :HEAD_DIM] = k_ref[...]
        qc_ref[:, HEAD_DIM:] = jnp.where(lane < 3, 1.0, jnp.where(lane < 6, parts, zero)).astype(BF)
        kc_ref[:, HEAD_DIM:] = jnp.where(lane < 3, -parts, jnp.where(lane < 6, 1.0, zero)).astype(BF)

    head = lambda o: pl.BlockSpec((tr, HEAD_DIM), functools.partial(lambda i, h, o: (i, o + h), o=o))
    spec = pl.BlockSpec((tr, 2 * HEAD_DIM), lambda i, h: (i, h))
    shape = jax.ShapeDtypeStruct((s, n_heads * 2 * HEAD_DIM), BF)
    return pl.pallas_call(
        body, grid=(s // tr, n_heads),
        in_specs=[pl.BlockSpec((tr, LANES), lambda i, h: (i, 0)), head(q_off), head(k_off)],
        out_specs=[spec, spec], out_shape=[shape, shape],
        compiler_params=_params(("parallel", "arbitrary")), name=name)(c_pad, qkv, qkv)


def _causal_mask(st):
    kpos = lax.broadcasted_iota(jnp.int32, st.shape, 0)
    qpos = lax.broadcasted_iota(jnp.int32, st.shape, 1)
    return jnp.where(kpos <= qpos, st, NEG)


def _flash_fwd(name, q_src, k_src, v_src, n_heads, *, fox, tab_t=None):
    (q_arr, q_off, qw), (k_arr, k_off, kw), (v_arr, v_off, _) = q_src, k_src, v_src
    s = q_arr.shape[0]
    blk = _att_block(s)
    nq = s // blk
    n_delta = MAX_WINDOW // blk + 1
    grp = ATT_HEADS
    assert qw == kw and n_heads % grp == 0 and q_off % grp == 0 and k_off % grp == 0 and v_off % grp == 0
    wide = grp * HEAD_DIM

    def body(*refs):
        if fox:
            q_ref, k_ref, v_ref, o_ref, lse_ref, acc, m_s, l_s = refs
        else:
            q_ref, k_ref, v_ref, tab_ref, o_ref, lse_ref, acc, m_s, l_s = refs
        i = pl.program_id(1)
        acc[...] = jnp.zeros_like(acc)
        m_s[...] = jnp.full_like(m_s, NEG)
        l_s[...] = jnp.zeros_like(l_s)

        def step(j, diagonal):
            ks = pl.ds(pl.multiple_of(j * blk, blk), blk)
            for g in range(grp):
                cols = slice(g * HEAD_DIM, (g + 1) * HEAD_DIM)
                qk_cols = slice(g * qw, (g + 1) * qw)
                st = lax.dot_general(k_ref[ks, qk_cols], q_ref[:, qk_cols], _DIMS["nt"], preferred_element_type=F32)
                if fox:
                    if diagonal:
                        st = _causal_mask(st)
                else:
                    st = st + tab_ref[i - j]
                m_prev = m_s[g]
                m_new = jnp.maximum(m_prev, jnp.max(st, axis=0, keepdims=True))
                alpha = jnp.exp(m_prev - m_new)
                p = jnp.exp(st - m_new)
                l_s[g] = alpha * l_s[g] + jnp.sum(p, axis=0, keepdims=True)
                acc[g] = alpha * acc[g] + lax.dot_general(v_ref[ks, cols], p.astype(BF), _DIMS["tn"],
                                                          preferred_element_type=F32)
                m_s[g] = m_new

        def loop_step(j, carry):
            step(j, False)
            return carry

        if fox:
            lax.fori_loop(0, i, loop_step, 0)
            step(i, True)
        else:
            lax.fori_loop(jnp.maximum(i - (n_delta - 1), 0), i + 1, loop_step, 0)
        for g in range(grp):
            o_ref[:, g * HEAD_DIM:(g + 1) * HEAD_DIM] = (acc[g] / l_s[g]).T.astype(o_ref.dtype)
            lse_ref[g] = m_s[g] + jnp.log(l_s[g])

    off = lambda o: functools.partial(lambda h, i, o: (0, o + h), o=o // grp)
    in_specs = [pl.BlockSpec((blk, grp * qw), functools.partial(lambda h, i, o: (i, o + h), o=q_off // grp)),
                pl.BlockSpec((s, grp * kw), off(k_off)), pl.BlockSpec((s, wide), off(v_off))]
    args = [q_arr, k_arr, v_arr]
    if not fox:
        in_specs.append(pl.BlockSpec((n_delta, blk, blk), lambda h, i: (0, 0, 0)))
        args.append(tab_t)
    return pl.pallas_call(
        body, grid=(n_heads // grp, nq), in_specs=in_specs,
        out_specs=[pl.BlockSpec((blk, wide), lambda h, i: (i, h)),
                   pl.BlockSpec((grp, None, 1, blk), lambda h, i: (h, i, 0, 0))],
        out_shape=[jax.ShapeDtypeStruct((s, n_heads * HEAD_DIM), BF),
                   jax.ShapeDtypeStruct((n_heads, nq, 1, blk), F32)],
        scratch_shapes=[pltpu.VMEM((grp, HEAD_DIM, blk), F32), pltpu.VMEM((grp, 1, blk), F32),
                        pltpu.VMEM((grp, 1, blk), F32)],
        compiler_params=_params(("parallel", "parallel")), name=name)(*args)


def _att_delta(name, do, o, n_heads):
    s = do.shape[0]
    blk = _pick(s, 1024, 16)

    def body(do_ref, o_ref, d_ref):
        d_ref[...] = jnp.sum(do_ref[...].astype(F32) * o_ref[...].astype(F32), axis=-1, keepdims=True)

    spec = pl.BlockSpec((blk, HEAD_DIM), lambda h, i: (i, h))
    return pl.pallas_call(
        body, grid=(n_heads, s // blk), in_specs=[spec, spec],
        out_specs=pl.BlockSpec((None, blk, 1), lambda h, i: (h, i, 0)),
        out_shape=jax.ShapeDtypeStruct((n_heads, s, 1), F32),
        compiler_params=_params(("parallel", "parallel")), name=name)(do, o)


def _flash_bwd(name, q_src, k_src, v_src, do, lse_row, delta_row, n_heads, *, fox, tab_t=None):
    (q_arr, q_off, qw), (k_arr, k_off, kw), (v_arr, v_off, _) = q_src, k_src, v_src
    s = q_arr.shape[0]
    blk = _att_block(s)
    nq = s // blk
    n_delta = MAX_WINDOW // blk + 1
    grp = ATT_HEADS
    assert qw == kw and n_heads % grp == 0 and q_off % grp == 0 and k_off % grp == 0 and v_off % grp == 0
    wide = grp * HEAD_DIM

    def body(*refs):
        if fox:
            (q_ref, do_ref, k_ref, v_ref, lse_ref, dl_ref,
             dq_ref, dk_ref, dv_ref, dc_ref, dcq_ref, dk_acc, dv_acc, dc_acc) = refs
        else:
            (q_ref, do_ref, k_ref, v_ref, lse_ref, dl_ref, tab_ref,
             dq_ref, dk_ref, dv_ref, dk_acc, dv_acc) = refs
        j = pl.program_id(1)

        @pl.when(j == 0)
        def _():
            dq_ref[...] = jnp.zeros_like(dq_ref)
            if fox:
                dcq_ref[...] = jnp.zeros_like(dcq_ref)

        dk_acc[...] = jnp.zeros_like(dk_acc)
        dv_acc[...] = jnp.zeros_like(dv_acc)
        if fox:
            dc_acc[...] = jnp.zeros_like(dc_acc)

        def step(i, diagonal):
            qs = pl.ds(pl.multiple_of(i * blk, blk), blk)
            for g in range(grp):
                cols = slice(g * HEAD_DIM, (g + 1) * HEAD_DIM)
                qk_cols = slice(g * qw, (g + 1) * qw)
                plain = slice(g * qw, g * qw + HEAD_DIM)
                kb, vb = k_ref[:, plain], v_ref[:, cols]
                qb, dob = q_ref[qs, plain], do_ref[qs, cols]
                st = lax.dot_general(k_ref[:, qk_cols], q_ref[qs, qk_cols], _DIMS["nt"], preferred_element_type=F32)
                if fox:
                    if diagonal:
                        st = _causal_mask(st)
                else:
                    st = st + tab_ref[i - j]
                pt = jnp.exp(st - lse_ref[g, i])
                dv_acc[:, cols] += jnp.dot(pt.astype(BF), dob, preferred_element_type=F32)
                dpt = lax.dot_general(vb, dob, _DIMS["nt"], preferred_element_type=F32)
                dst = pt * (dpt - dl_ref[g, i])
                dsb = dst.astype(BF)
                dk_acc[:, cols] += jnp.dot(dsb, qb, preferred_element_type=F32)
                dq_ref[qs, cols] += lax.dot_general(dsb, kb, _DIMS["tn"], preferred_element_type=F32)
                if fox:
                    folded = dst[:, :LANES]
                    for part in range(1, blk // LANES):
                        folded = folded + dst[:, part * LANES:(part + 1) * LANES]
                    dc_acc[g] -= folded
                    dcq_ref[g, i] += jnp.sum(dst, axis=0, keepdims=True)

        def loop_step(i, carry):
            step(i, False)
            return carry

        if fox:
            step(j, True)
            lax.fori_loop(j + 1, nq, loop_step, 0)
        else:
            lax.fori_loop(j, jnp.minimum(nq, j + n_delta), loop_step, 0)
        dk_ref[...] = dk_acc[...]
        dv_ref[...] = dv_acc[...].astype(dv_ref.dtype)
        if fox:
            for g in range(grp):
                dc_ref[g] = jnp.sum(dc_acc[g], axis=-1, keepdims=True)

    full = lambda o, wd=wide: pl.BlockSpec((s, wd), functools.partial(lambda h, j, o: (0, o + h), o=o // grp))
    tile = lambda o, wd=wide: pl.BlockSpec((blk, wd), functools.partial(lambda h, j, o: (j, o + h), o=o // grp))
    per_q = pl.BlockSpec((grp, nq, 1, blk), lambda h, j: (h, 0, 0, 0))
    per_k = pl.BlockSpec((grp, blk, 1), lambda h, j: (h, j, 0))
    in_specs = [full(q_off, grp * qw), full(0), tile(k_off, grp * kw), tile(v_off), per_q, per_q]
    args = [q_arr, do, k_arr, v_arr, lse_row, delta_row]
    out_specs = [full(0), tile(0), tile(0)]
    hd = n_heads * HEAD_DIM
    out_shape = [jax.ShapeDtypeStruct((s, hd), F32), jax.ShapeDtypeStruct((s, hd), F32),
                 jax.ShapeDtypeStruct((s, hd), BF)]
    scratch = [pltpu.VMEM((blk, wide), F32), pltpu.VMEM((blk, wide), F32)]
    if fox:
        out_specs += [per_k, per_q]
        out_shape += [jax.ShapeDtypeStruct((n_heads, s, 1), F32), jax.ShapeDtypeStruct((n_heads, nq, 1, blk), F32)]
        scratch.append(pltpu.VMEM((grp, blk, LANES), F32))
    else:
        in_specs.append(pl.BlockSpec((n_delta, blk, blk), lambda h, j: (0, 0, 0)))
        args.append(tab_t)
    return pl.pallas_call(
        body, grid=(n_heads // grp, nq), in_specs=in_specs, out_specs=out_specs, out_shape=out_shape,
        scratch_shapes=scratch, compiler_params=_params(("parallel", "arbitrary")), name=name)(*args)


def _split3(t):
    hi = t.astype(BF)
    r1 = t - hi.astype(F32)
    mid = r1.astype(BF)
    lo = (r1 - mid.astype(F32)).astype(BF)
    return hi, mid, lo


def _tri_dot(tri, t):
    hi, mid, lo = _split3(t)
    return (jnp.dot(tri, hi, preferred_element_type=F32) + jnp.dot(tri, mid, preferred_element_type=F32)
            + jnp.dot(tri, lo, preferred_element_type=F32))


def _log_sigmoid(z):
    return jnp.minimum(z, 0.0) - jnp.log(1.0 + jnp.exp(-jnp.abs(z)))


def _forget_cumsum(name, proj, f_col, bias):
    s = proj.shape[0]
    blk = _att_block(s)

    def body(f_ref, b_ref, c_ref, carry):
        @pl.when(pl.program_id(0) == 0)
        def _():
            carry[...] = jnp.zeros_like(carry)

        lf = _log_sigmoid(f_ref[...] + b_ref[...])
        r = lax.broadcasted_iota(jnp.int32, (blk, blk), 0)
        c = lax.broadcasted_iota(jnp.int32, (blk, blk), 1)
        tri = (c <= r).astype(BF)
        c_ref[...] = _tri_dot(tri, lf) + carry[...]
        carry[...] = c_ref[pl.ds(blk - 1, 1), :]

    return pl.pallas_call(
        body, grid=(s // blk,),
        in_specs=[pl.BlockSpec((blk, LANES), lambda i: (i, f_col)), pl.BlockSpec((1, LANES), lambda i: (0, 0))],
        out_specs=pl.BlockSpec((blk, LANES), lambda i: (i, 0)), out_shape=jax.ShapeDtypeStruct((s, LANES), F32),
        scratch_shapes=[pltpu.VMEM((1, LANES), F32)],
        compiler_params=_params(("arbitrary",)), name=name)(proj, bias)


def _forget_bwd(name, proj, f_col, bias, dc):
    s = proj.shape[0]
    blk = _att_block(s)
    nb = s // blk

    def body(f_ref, b_ref, dc_ref, df_ref, db_ref, carry):
        @pl.when(pl.program_id(0) == 0)
        def _():
            carry[...] = jnp.zeros_like(carry)
            db_ref[...] = jnp.zeros_like(db_ref)

        r = lax.broadcasted_iota(jnp.int32, (blk, blk), 0)
        c = lax.broadcasted_iota(jnp.int32, (blk, blk), 1)
        tri = (c >= r).astype(BF)
        r = lax.broadcasted_iota(jnp.int32, (blk, LANES), 0)
        dlf = _tri_dot(tri, dc_ref[...]) + carry[...]
        carry[...] = jnp.sum(jnp.where(r == 0, dlf, 0.0), axis=0, keepdims=True)
        dz = dlf * _sigmoid(-(f_ref[...] + b_ref[...]))
        df_ref[...] = dz.astype(BF)
        db_ref[...] += _fold8(dz)

    rev = lambda i: (nb - 1 - i, 0)
    return pl.pallas_call(
        body, grid=(nb,),
        in_specs=[pl.BlockSpec((blk, LANES), lambda i: (nb - 1 - i, f_col)), pl.BlockSpec((1, LANES), lambda i: (0, 0)),
                  pl.BlockSpec((blk, LANES), rev)],
        out_specs=[pl.BlockSpec((blk, LANES), rev), pl.BlockSpec((8, LANES), lambda i: (0, 0))],
        out_shape=[jax.ShapeDtypeStruct((s, LANES), BF), jax.ShapeDtypeStruct((8, LANES), F32)],
        scratch_shapes=[pltpu.VMEM((1, LANES), F32)],
        compiler_params=_params(("arbitrary",)), name=name)(proj, bias, dc)


def _head_prep(name, proj, cos_t, sin_t, n_heads):
    s = proj.shape[0]
    tr = _pick(s, 512, 16)
    scale = HEAD_DIM ** -0.5
    hd = n_heads * HEAD_DIM
    cw = _pick(hd, PREP_COLS, HEAD_DIM)
    per_kind = hd // cw
    cos_w, sin_w = jnp.tile(cos_t, (1, cw // HEAD_DIM)), jnp.tile(sin_t, (1, cw // HEAD_DIM))

    def body(p_ref, cos_ref, sin_ref, o_ref):
        j = pl.program_id(1)
        t = p_ref[...]

        @pl.when(j < 2 * per_kind)
        def _():
            r = t * cos_ref[...] + _swap_halves(t) * sin_ref[...]
            o_ref[...] = jnp.where(j < per_kind, r * scale, r).astype(BF)

        @pl.when(j >= 2 * per_kind)
        def _():
            is_q = jnp.logical_and(j >= 3 * per_kind, j < 4 * per_kind)
            o_ref[...] = jnp.where(is_q, t * scale, t).astype(BF)

    tab = pl.BlockSpec((tr, cw), lambda i, j: (i, 0))
    blk = pl.BlockSpec((tr, cw), lambda i, j: (i, j))
    return pl.pallas_call(
        body, grid=(s // tr, 6 * per_kind), in_specs=[blk, tab, tab], out_specs=blk,
        out_shape=jax.ShapeDtypeStruct((s, 6 * hd), BF),
        compiler_params=_params(("parallel", "arbitrary")), name=name)(proj, cos_w, sin_w)


def _dproj_assemble(name, parts, gates, cos_t, sin_t, n_heads, d_model):
    s = cos_t.shape[0]
    tr = _pick(s, 512, 16)
    scale = HEAD_DIM ** -0.5
    hd = n_heads * HEAD_DIM
    cw = _pick(math.gcd(hd, d_model), PREP_COLS, HEAD_DIM)
    widths = [hd] * 6 + [d_model] * 2
    starts = [sum(widths[:t]) // cw for t in range(len(widths) + 1)]
    cos_w, sin_w = jnp.tile(cos_t, (1, cw // HEAD_DIM)), jnp.tile(sin_t, (1, cw // HEAD_DIM))

    def body(*refs):
        p_refs, cos_ref, sin_ref, o_ref = refs[:8], refs[8], refs[9], refs[10]
        j = pl.program_id(1)
        for kind in range(8):
            @pl.when(jnp.logical_and(j >= starts[kind], j < starts[kind + 1]))
            def _(kind=kind):
                t = p_refs[kind][...]
                if kind in (0, 1, 3):
                    t = t.astype(F32)
                if kind in (0, 3):
                    t = t * scale
                if kind in (0, 1):
                    t = t * cos_ref[...] - _swap_halves(t) * sin_ref[...]
                o_ref[...] = t.astype(BF)

    def part_spec(kind):
        return pl.BlockSpec((tr, cw), functools.partial(
            lambda i, j, kind: (i, jnp.clip(j - starts[kind], 0, widths[kind] // cw - 1)), kind=kind))

    tab = pl.BlockSpec((tr, cw), lambda i, j: (i, 0))
    return pl.pallas_call(
        body, grid=(s // tr, starts[-1]), in_specs=[part_spec(kind) for kind in range(8)] + [tab, tab],
        out_specs=pl.BlockSpec((tr, cw), lambda i, j: (i, j)),
        out_shape=jax.ShapeDtypeStruct((s, sum(widths)), BF),
        compiler_params=_params(("parallel", "arbitrary")), name=name)(*parts, *gates, cos_w, sin_w)


FFN2 =("ffn2_w_gate", "ffn2_w_up", "ffn2_w_down")


def _device_step(x, tgt, shards, small):
    s, d = x.shape
    hd = shards["w_proj_dil"].shape[0]
    hh = hd // HEAD_DIM
    blk = _att_block(s)
    gate_off = 6 * hd
    f_col = 0
    n_proj = gate_off + 2 * d
    ident = lambda accs, ex, rw: (accs[0],)
    chip = 2 * lax.axis_index("x") + lax.axis_index("y")
    ids = jnp.stack([chip, lax.axis_index("c")]).astype(jnp.int32)
    w = {}

    def take_gathered(names, gathered):
        for n, t in zip(names, gathered):
            w[n] = _full_from_pieces(n, lax.dynamic_update_index_in_dim(t, shards[n], chip, 0))

    def chip_sums(names, pieces, from_sibling):
        return [_rs_add("rs_add_" + n, ids, g, o) for n, g, o in zip(names, pieces, from_sibling)]

    def core_halves(names, sums, from_chips):
        return [_rs_sum("rs_sum_" + n, ids, own, got) for n, own, got in zip(names, sums, from_chips)]

    def gather(names):
        return _gather_side([shards[n] for n in names])

    first, under_up = ("ffn1_w_gate", "ffn1_w_up"), ("ffn1_w_down", "w_in")
    under_down = ("w_proj_dil", "w_proj_fox", "w_out", "ffn2_w_gate")
    under_proj = ("ffn2_w_up", "ffn2_w_down")
    take_gathered(first, gather(first).call("gather_first"))

    def w_down_of_ffn1(got):
        take_gathered(under_up, got["up"])
        return w["ffn1_w_down"]

    x1, saved1, got = _ffn_fwd("ffn1", x, small["ffn1_norm"], w["ffn1_w_gate"], w["ffn1_w_up"], w_down_of_ffn1,
                               sides={"up": lambda got: gather(under_up), "down": lambda got: gather(under_down)})
    take_gathered(under_down, got["down"])
    w["w_in"], w["w_in_f"] = _repack_w_in(w["w_in"], hd, d)

    hm = _rms_fwd("mix_rms", x1, small["mix_norm"])
    (proj,), gathered = _mm("proj", [(hm, w["w_in"])], "nn", s, n_proj, d, epilogue=ident, out_dtypes=(F32,),
                            side=gather(under_proj))
    take_gathered(under_proj, gathered)
    (f_logit,) = _mm("proj_f", [(hm, w["w_in_f"])], "nn", s, LANES, d, epilogue=ident, out_dtypes=(F32,))
    cos_t, sin_t = _rope_tables(s)
    qkv = _head_prep("head_prep", proj, cos_t, sin_t, hh)
    tab_t = _dil_bias(blk)
    dil_src = ((qkv, 0, HEAD_DIM), (qkv, hh, HEAD_DIM), (qkv, 2 * hh, HEAD_DIM))
    y_dil, lse_d = _flash_fwd("dil_fwd", *dil_src, hh, fox=False, tab_t=tab_t)
    bias_f = jnp.pad(small["b_forget"], ((0, 0), (0, LANES - hh)))
    c_pad = _forget_cumsum("forget_cumsum", f_logit, f_col, bias_f)
    q_cat, k_cat = _fox_aug("fox_aug", c_pad, qkv, 3 * hh, 4 * hh, hh)
    fox_src = ((q_cat, 0, 2 * HEAD_DIM), (k_cat, 0, 2 * HEAD_DIM), (qkv, 5 * hh, HEAD_DIM))
    y_fox, lse_f = _flash_fwd("fox_fwd", *fox_src, hh, fox=True)

    def merge_epi(accs, ex, rw):
        ud, uf = accs
        return ud, uf, _sigmoid(ex[0] + rw[0]) * ud + _sigmoid(ex[1] + rw[1]) * uf

    u_d, u_f, merged = _mm("merge", [(y_dil, w["w_proj_dil"]), (y_fox, w["w_proj_fox"])], "nn", s, d, hd,
                           epilogue=merge_epi, out_dtypes=(BF, BF, BF),
                           extras=[(proj, gate_off), (proj, gate_off + d)],
                           rows=[small["b_gate_dil"], small["b_gate_fox"]])
    (x2,) = _mm("mix_out", [(merged, w["w_out"])], "nn", s, d, d,
                epilogue=lambda accs, ex, rw: (ex[0] + accs[0],), out_dtypes=(F32,), extras=[(x1, 0)])

    x3, saved2, _ = _ffn_fwd("ffn2", x2, small["ffn2_norm"], w["ffn2_w_gate"], w["ffn2_w_up"],
                             lambda got: w["ffn2_w_down"])
    dx3, dx3_bf, dg_final, sq = _final("final", x3, small["final_norm"].reshape(1, d), tgt)

    dx2, dx2_bf, dg_ffn2, dw_g2, dw_u2, dw_d2, _ = _ffn_bwd("ffn2", x2, small["ffn2_norm"], w["ffn2_w_gate"],
                                                            w["ffn2_w_up"], w["ffn2_w_down"], saved2, dx3, dx3_bf)
    pieces2 = [dw_g2, dw_u2, dw_d2]

    def dmerge_epi(accs, ex, rw):
        dm = accs[0]
        gd, gf, ud, uf = ex[0], ex[1], ex[2].astype(F32), ex[3].astype(F32)
        sd, sf = _sigmoid(gd + rw[0]), _sigmoid(gf + rw[1])
        dgd = dm * ud * (sd * (1.0 - sd))
        dgf = dm * uf * (sf * (1.0 - sf))
        return (dm * sd, dm * sf, dgd, dgf, jnp.sum(dgd, axis=0, keepdims=True), jnp.sum(dgf, axis=0, keepdims=True))

    (du_d, du_f, dg_d, dg_f, dbg_d, dbg_f), from_sibling = _mm(
        "dmerge", [(dx2_bf, w["w_out"])], "nt", s, d, d, epilogue=dmerge_epi, out_dtypes=(BF, BF, BF, BF), n_colsum=2,
        extras=[(proj, gate_off), (proj, gate_off + d), (u_d, 0), (u_f, 0)],
        rows=[small["b_gate_dil"], small["b_gate_fox"]], side=_swap_side(pieces2))
    sums2 = chip_sums(FFN2, pieces2, from_sibling)
    (dw_out,) = _mm("dw_out", [(merged, dx2_bf)], "tn", d, d, s, epilogue=ident, out_dtypes=(F32,))
    dw_out = dw_out.reshape(N_CHIPS, d // N_CHIPS, d)
    dw_pd = _dw_col_pieces("dw_pd", y_dil, du_d, hd, d, s)
    dw_pf = _dw_col_pieces("dw_pf", y_fox, du_f, hd, d, s)
    (dy_dil,) = _mm("dy_dil", [(du_d, w["w_proj_dil"])], "nt", s, hd, d, epilogue=ident, out_dtypes=(BF,))
    (dy_fox,) = _mm("dy_fox", [(du_f, w["w_proj_fox"])], "nt", s, hd, d, epilogue=ident, out_dtypes=(BF,))

    row = lambda t: t.reshape(hh, s // blk, 1, blk)
    delta_d = _att_delta("dil_delta", dy_dil, y_dil, hh)
    dq_d, dk_d, dv_d = _flash_bwd("dil_bwd", *dil_src, dy_dil, lse_d, row(delta_d), hh, fox=False, tab_t=tab_t)
    delta_f = _att_delta("fox_delta", dy_fox, y_fox, hh)
    dq_f, dk_f, dv_f, dc_k, dc_q = _flash_bwd("fox_bwd", *fox_src, dy_fox, lse_f, row(delta_f), hh, fox=True)
    dc = dc_k.reshape(hh, s) + dc_q.reshape(hh, s)
    dc_pad = jnp.pad(dc.T, ((0, 0), (0, LANES - hh)))
    df, db_forget = _forget_bwd("forget_bwd", f_logit, f_col, bias_f, dc_pad)
    dproj = _dproj_assemble("dproj", [dq_d, dk_d, dv_d, dq_f, dk_f, dv_f], [dg_d, dg_f], cos_t, sin_t, hh, d)
    (dhm_f,) = _mm("dhm_f", [(df, w["w_in_f"])], "nt", s, d, LANES, epilogue=ident, out_dtypes=(F32,))
    (dhm,), from_chips = _mm("dhm", [(dproj, w["w_in"])], "nt", s, d, n_proj,
                             epilogue=lambda accs, ex, rw: (accs[0] + ex[0],), out_dtypes=(F32,),
                             extras=[(dhm_f, 0)], side=_scatter_side(sums2))
    halves2 = core_halves(FFN2, sums2, from_chips)
    small_mixer = ("w_proj_dil", "w_proj_fox", "w_out")
    pieces_sm = [dw_pd, dw_pf, dw_out]
    (dw_in,), from_sibling = _mm("dw_in", [(hm, dproj)], "tn", d, n_proj, s, epilogue=ident, out_dtypes=(F32,),
                                 side=_swap_side(pieces_sm))
    sums_sm = chip_sums(small_mixer, pieces_sm, from_sibling)
    (dw_in_f,) = _mm("dw_in_f", [(hm, df)], "tn", d, LANES, s, epilogue=ident, out_dtypes=(F32,))
    dw_in = _col_pieces(_unpack_dw_in(dw_in, dw_in_f, hd, d))
    dx1, dx1_bf, dg_mix = _rms_bwd("mix_rmsb", x1, small["mix_norm"], dhm, dx2)

    sums = {}

    def under_dwd(got):
        sums["w_in"] = chip_sums(["w_in"], [dw_in], got["dact"])
        return _scatter_side(sums_sm)

    def under_dwu(got):
        return _swap_side([got["dw_down"], got["dw_gate"]])

    def under_dh(got):
        sums["dg"] = chip_sums(["ffn1_w_down", "ffn1_w_gate"], [got["dw_down"], got["dw_gate"]], got["dwu"])
        return _scatter_side(sums["dg"]).beside(_swap_side([got["dw_up"]]))

    dx0, _, dg_ffn1, dw_g1, dw_u1, dw_d1, got = _ffn_bwd(
        "ffn1", x, small["ffn1_norm"], w["ffn1_w_gate"], w["ffn1_w_up"], w["ffn1_w_down"], saved1, dx1, dx1_bf,
        sides={"dact": lambda got: _swap_side([dw_in]), "dwd": under_dwd,
               "dwg": lambda got: _scatter_side(sums["w_in"]), "dwu": under_dwu, "dh": under_dh})
    halves_sm = core_halves(small_mixer, sums_sm, got["dwd"])
    halves_in = core_halves(["w_in"], sums["w_in"], got["dwg"])
    halves_dg = core_halves(["ffn1_w_down", "ffn1_w_gate"], sums["dg"], got["dh"][:2])
    sums_u = chip_sums(["ffn1_w_up"], [dw_u1], got["dh"][2:])
    halves_u = core_halves(["ffn1_w_up"], sums_u, _scatter_side(sums_u).call("rs_scatter_last"))

    names = ("ffn1_w_down", "ffn1_w_gate", "ffn1_w_up", "w_in") + small_mixer + FFN2
    totals = _join_side(halves_dg + halves_u + halves_in + halves_sm + halves2).call("rs_join")
    grads = dict(zip(names, totals))
    partials = {"ffn1_norm": dg_ffn1, "mix_norm": dg_mix, "ffn2_norm": dg_ffn2, "final_norm": dg_final,
                "b_gate_dil": dbg_d.reshape(-1, d), "b_gate_fox": dbg_f.reshape(-1, d), "b_forget": db_forget, "sq": sq}
    return dx0, grads, partials


def _coords():
    return lax.axis_index("x"), lax.axis_index("y"), lax.axis_index("c")


def _other_chips(x, y):
    return [(1 - x, y), (x, 1 - y), (1 - x, 1 - y)]


ANY_SPEC = pl.BlockSpec(memory_space=pl.ANY)


def _gather_side(shards):
    nw = len(shards)

    def copies(srcs, outs, send_sems, recv_sems):
        x, y, c = _coords()
        chips = _other_chips(x, y)

        def slot(w, px, py, pc):
            half = shards[w].shape[0] // 2
            return outs[w].at[2 * px + py, pl.ds(pc * half, half), :]

        def copy(w, k, src_ref, dst_ref, to):
            return pltpu.make_async_remote_copy(src_ref=src_ref, dst_ref=dst_ref, send_sem=send_sems.at[6 * w + k],
                                                recv_sem=recv_sems.at[6 * w + k], device_id=to, device_id_type=MESH)

        first, arrive, passed, arrive2 = [], [], [], []
        for w in range(nw):
            half = shards[w].shape[0] // 2
            for j, chip in enumerate(chips):
                first.append(copy(w, j, srcs[w].at[pl.ds(c * half, half), :], slot(w, x, y, c), (*chip, c)))
                arrive.append(copy(w, j, slot(w, *chip, c), slot(w, *chip, c), (*chip, c)))
                passed.append(copy(w, 3 + j, slot(w, *chip, c), slot(w, *chip, c), (x, y, 1 - c)))
                arrive2.append(copy(w, 3 + j, slot(w, *chip, 1 - c), slot(w, *chip, 1 - c), (x, y, 1 - c)))
        return first, arrive, passed, arrive2

    def start(srcs, outs, send_sems, recv_sems):
        for cp in copies(srcs, outs, send_sems, recv_sems)[0]:
            cp.start()

    def finish(srcs, outs, send_sems, recv_sems):
        first, arrive, passed, arrive2 = copies(srcs, outs, send_sems, recv_sems)
        for got, fwd in zip(arrive, passed):
            got.wait_recv()
            fwd.start()
        for got in arrive2:
            got.wait_recv()
        for cp in first + passed:
            cp.wait_send()

    return _Side(shards, [jax.ShapeDtypeStruct((N_CHIPS, *t.shape), t.dtype) for t in shards], 6 * nw, start, finish)


def _swap_side(grads):
    nw = len(grads)

    def copies(srcs, outs, send_sems, recv_sems):
        x, y, c = _coords()
        res = []
        for w in range(nw):
            half = grads[w].shape[1] // 2
            for p in range(N_CHIPS):
                k = N_CHIPS * w + p
                res.append(pltpu.make_async_remote_copy(
                    src_ref=srcs[w].at[p, pl.ds((1 - c) * half, half), :], dst_ref=outs[w].at[p],
                    send_sem=send_sems.at[k], recv_sem=recv_sems.at[k], device_id=(x, y, 1 - c), device_id_type=MESH))
        return res

    def start(*refs):
        for cp in copies(*refs):
            cp.start()

    def finish(*refs):
        for cp in copies(*refs):
            cp.wait()

    shapes = [jax.ShapeDtypeStruct((N_CHIPS, t.shape[1] // 2, t.shape[2]), t.dtype) for t in grads]
    return _Side(grads, shapes, N_CHIPS * nw, start, finish)


def _rs_add(name, ids, g, other):
    n, rows, cols = g.shape
    half = rows // 2
    tr = _pick(half, 256, 16)
    nb = half // tr

    def body(ids_ref, g_ref, o_ref, out_ref):
        out_ref[...] = (g_ref[...] + o_ref[...]).astype(BF)

    grid_spec = pltpu.PrefetchScalarGridSpec(
        num_scalar_prefetch=1, grid=(n, nb),
        in_specs=[pl.BlockSpec((None, tr, cols), lambda p, i, ids_ref: (p, ids_ref[1] * nb + i, 0)),
                  pl.BlockSpec((None, tr, cols), lambda p, i, ids_ref: (p, i, 0))],
        out_specs=pl.BlockSpec((None, tr, cols), lambda p, i, ids_ref: (p, i, 0)))
    return pl.pallas_call(body, grid_spec=grid_spec, out_shape=jax.ShapeDtypeStruct((n, half, cols), BF),
                          compiler_params=_params(("parallel", "parallel")), name=name)(ids, g, other)


def _scatter_side(sums):
    nw = len(sums)

    def copies(srcs, outs, send_sems, recv_sems):
        x, y, c = _coords()
        res = []
        for w in range(nw):
            for k, (px, py) in enumerate(_other_chips(x, y)):
                res.append(pltpu.make_async_remote_copy(
                    src_ref=srcs[w].at[2 * px + py], dst_ref=outs[w].at[k], send_sem=send_sems.at[3 * w + k],
                    recv_sem=recv_sems.at[3 * w + k], device_id=(px, py, c), device_id_type=MESH))
        return res

    def start(*refs):
        for cp in copies(*refs):
            cp.start()

    def finish(*refs):
        for cp in copies(*refs):
            cp.wait()

    return _Side(sums, [jax.ShapeDtypeStruct((3, *t.shape[1:]), t.dtype) for t in sums], 3 * nw, start, finish)


def _rs_sum(name, ids, own, got):
    n, half, cols = own.shape
    tr = _pick(half, 256, 16)
    nb = half // tr

    def body(ids_ref, own_ref, got_ref, out_ref):
        t = own_ref[...].astype(F32)
        for k in range(3):
            t = t + got_ref[k].astype(F32)
        out_ref[...] = t

    grid_spec = pltpu.PrefetchScalarGridSpec(
        num_scalar_prefetch=1, grid=(nb,),
        in_specs=[pl.BlockSpec((None, tr, cols), lambda i, ids_ref: (ids_ref[0], i, 0)),
                  pl.BlockSpec((3, tr, cols), lambda i, ids_ref: (0, i, 0))],
        out_specs=pl.BlockSpec((tr, cols), lambda i, ids_ref: (ids_ref[1] * nb + i, 0)))
    return pl.pallas_call(body, grid_spec=grid_spec, out_shape=jax.ShapeDtypeStruct((2 * half, cols), F32),
                          compiler_params=_params(("parallel",)), name=name)(ids, own, got)


def _join_side(totals):
    nw = len(totals)

    def start(ins, bufs, send_sems, recv_sems):
        x, y, c = _coords()
        for w in range(nw):
            half = totals[w].shape[0] // 2
            pltpu.make_async_remote_copy(
                src_ref=bufs[w].at[pl.ds(c * half, half), :], dst_ref=bufs[w].at[pl.ds(c * half, half), :],
                send_sem=send_sems.at[w], recv_sem=recv_sems.at[w], device_id=(x, y, 1 - c),
                device_id_type=MESH).start()

    def finish(ins, bufs, send_sems, recv_sems):
        x, y, c = _coords()
        for w in range(nw):
            half = totals[w].shape[0] // 2
            arrival = pltpu.make_async_remote_copy(
                src_ref=bufs[w].at[pl.ds(c * half, half), :], dst_ref=bufs[w].at[pl.ds((1 - c) * half, half), :],
                send_sem=send_sems.at[w], recv_sem=recv_sems.at[w], device_id=(x, y, 1 - c), device_id_type=MESH)
            arrival.wait_recv()
            arrival.wait_send()

    return _Side(totals, [jax.ShapeDtypeStruct(t.shape, t.dtype) for t in totals], nw, start, finish,
                 aliases={w: w for w in range(nw)})


def _gather_all(name, t):
    rows, cols = t.shape

    def body(src, out, send_sems, recv_sems, local_sem):
        x, y, c = _coords()
        me = 4 * x + 2 * y + c
        mine = pltpu.make_async_copy(src, out.at[me], local_sem)
        mine.start()
        peers = [(x ^ (k >> 2 & 1), y ^ (k >> 1 & 1), c ^ (k & 1)) for k in range(1, N_DEV)]
        sends = [pltpu.make_async_remote_copy(src_ref=src, dst_ref=out.at[me], send_sem=send_sems.at[k],
                                              recv_sem=recv_sems.at[k], device_id=peer, device_id_type=MESH)
                 for k, peer in enumerate(peers)]
        for cp in sends:
            cp.start()
        for k, (px, py, pc) in enumerate(peers):
            pltpu.make_async_remote_copy(src_ref=src, dst_ref=out.at[4 * px + 2 * py + pc], send_sem=send_sems.at[k],
                                         recv_sem=recv_sems.at[k], device_id=(px, py, pc),
                                         device_id_type=MESH).wait_recv()
        for cp in sends:
            cp.wait_send()
        mine.wait()

    vmem = pl.BlockSpec(memory_space=pltpu.VMEM)
    return pl.pallas_call(
        body, in_specs=[vmem], out_specs=vmem, out_shape=jax.ShapeDtypeStruct((N_DEV, rows, cols), t.dtype),
        scratch_shapes=[pltpu.SemaphoreType.DMA((7,)), pltpu.SemaphoreType.DMA((7,)), pltpu.SemaphoreType.DMA],
        name=name)(t)


def _adamw_math(w, g, m, v):
    m = ADAM_B1 * m + (1.0 - ADAM_B1) * g
    v = ADAM_B2 * v + (1.0 - ADAM_B2) * (g * g)
    m_hat = m / (1.0 - ADAM_B1 ** ADAM_STEP)
    v_hat = v / (1.0 - ADAM_B2 ** ADAM_STEP)
    delta = -ADAM_LR * (m_hat / (jnp.sqrt(v_hat) + ADAM_EPS) + ADAM_WD * w)
    return delta, m, v


def _adamw(name, w, g, m, v):
    rows, cols = w.shape
    tr = _pick(rows, 256, 8)

    def body(w_ref, g_ref, m_ref, v_ref, d_out, m_out, v_out):
        d_out[...], m_out[...], v_out[...] = _adamw_math(w_ref[...], g_ref[...], m_ref[...], v_ref[...])

    blk = pl.BlockSpec((tr, cols), lambda i: (i, 0))
    shape = jax.ShapeDtypeStruct((rows, cols), F32)
    return pl.pallas_call(body, grid=(rows // tr,), in_specs=[blk] * 4, out_specs=[blk] * 3, out_shape=[shape] * 3,
                          compiler_params=_params(("parallel",)), name=name)(w, g, m, v)


def _small_reduce(name, parts, width):
    def body(*refs):
        out = refs[-1]
        out[...] = jnp.zeros_like(out)
        for k, r in enumerate(refs[:-1]):
            out[pl.ds(k, 1), :] = jnp.sum(r[...], axis=0, keepdims=True)

    vmem = pl.BlockSpec(memory_space=pltpu.VMEM)
    return pl.pallas_call(body, in_specs=[vmem] * len(parts), out_specs=vmem,
                          out_shape=jax.ShapeDtypeStruct((8, width), F32), name=name)(*parts)


def _small_adamw(name, gathered, w, m, v, loss_row, loss_scale):
    def body(gt_ref, w_ref, m_ref, v_ref, g_out, d_out, m_out, v_out, loss_out):
        g = gt_ref[0]
        for k in range(1, N_DEV):
            g = g + gt_ref[k]
        g_out[...] = g
        row = lax.broadcasted_iota(jnp.int32, g.shape, 0)
        loss_out[...] = jnp.sum(jnp.where(row == loss_row, g, 0.0), keepdims=True) * loss_scale
        d_out[...], m_out[...], v_out[...] = _adamw_math(w_ref[...], g, m_ref[...], v_ref[...])

    vmem = pl.BlockSpec(memory_space=pltpu.VMEM)
    shape = jax.ShapeDtypeStruct(w.shape, F32)
    return pl.pallas_call(body, in_specs=[vmem] * 4, out_specs=[vmem] * 5,
                          out_shape=[shape] * 4 + [jax.ShapeDtypeStruct((1, 1), F32)], name=name)(gathered, w, m, v)


def _full_from_pieces(name, pieces):
    _, rows, cols = pieces.shape
    if name in ROW_SHARDED:
        return pieces.reshape(N_CHIPS * rows, cols)
    return pieces.transpose(1, 0, 2).reshape(rows, N_CHIPS * cols)


def _repack_w_in(w_in, hd, d):
    hh = hd // HEAD_DIM
    qkv, f, gates = w_in[:, :6 * hd], w_in[:, 6 * hd:6 * hd + hh], w_in[:, 6 * hd + hh:]
    return jnp.concatenate([qkv, gates], axis=1), jnp.pad(f, ((0, 0), (0, LANES - hh)))


def _unpack_dw_in(dw, dw_f, hd, d):
    hh = hd // HEAD_DIM
    return jnp.concatenate([dw[:, :6 * hd], dw_f[:, :hh], dw[:, 6 * hd:]], axis=1)


def _small_pack(vals, width):
    rows = []
    for name in SMALL:
        t = vals[name].reshape(1, -1)
        rows.append(jnp.pad(t, ((0, 0), (0, width - t.shape[1]))))
    rows.append(jnp.zeros((8 - len(SMALL), width), F32))
    return jnp.concatenate(rows, axis=0)


def kernel(x, ffn1_norm, ffn1_w_gate, ffn1_w_up, ffn1_w_down, mix_norm, w_in, b_forget, b_gate_dil, b_gate_fox, w_proj_dil, w_proj_fox, w_out, ffn2_norm, ffn2_w_gate, ffn2_w_up, ffn2_w_down, final_norm, loss_target, m_ffn1_norm, m_ffn1_w_gate, m_ffn1_w_up, m_ffn1_w_down, m_mix_norm, m_w_in, m_b_forget, m_b_gate_dil, m_b_gate_fox, m_w_proj_dil, m_w_proj_fox, m_w_out, m_ffn2_norm, m_ffn2_w_gate, m_ffn2_w_up, m_ffn2_w_down, m_final_norm, v_ffn1_norm, v_ffn1_w_gate, v_ffn1_w_up, v_ffn1_w_down, v_mix_norm, v_w_in, v_b_forget, v_b_gate_dil, v_b_gate_fox, v_w_proj_dil, v_w_proj_fox, v_w_out, v_ffn2_norm, v_ffn2_w_gate, v_ffn2_w_up, v_ffn2_w_down, v_final_norm):
    given = dict(locals())
    wts = {n: given[n] for n in WEIGHTS}
    mom_m = {n: given["m_" + n] for n in WEIGHTS}
    mom_v = {n: given["v_" + n] for n in WEIGHTS}
    d = x.shape[2]

    shards = {n: wts[n][0].astype(BF) for n in SHARDED}
    small = {n: wts[n] for n in SMALL}
    grad_x, grads, partials = _device_step(x[0], loss_target[0], shards, small)

    out_g, out_d, out_m, out_v = {}, {}, {}, {}
    for n in SHARDED:
        g = grads[n]
        dl, nm, nv = _adamw("adamw_" + n, wts[n][0], g, mom_m[n][0], mom_v[n][0])
        out_g[n], out_d[n], out_m[n], out_v[n] = g[None], dl[None], nm[None], nv[None]

    width = d
    part_rows = []
    for n in SMALL:
        t = partials[n]
        part_rows.append(jnp.pad(t, ((0, 0), (0, width - t.shape[1]))))
    part_rows.append(partials["sq"])
    local_small = _small_reduce("small_reduce", part_rows, width)
    gathered_small = _gather_all("small_gather", local_small)
    sg, sd_, sm, sv, loss = _small_adamw("small_adamw", gathered_small, _small_pack(wts, width),
                                         _small_pack(mom_m, width), _small_pack(mom_v, width), len(SMALL), 0.5 / d)
    for k, n in enumerate(SMALL):
        shp = wts[n].shape
        take = lambda t: t[k, :shp[-1]].reshape(shp)
        out_g[n], out_d[n], out_m[n], out_v[n] = take(sg), take(sd_), take(sm), take(sv)
    return (loss[0, 0], grad_x[None], *[out_g[n] for n in WEIGHTS], *[out_d[n] for n in WEIGHTS],
            *[out_m[n] for n in WEIGHTS], *[out_v[n] for n in WEIGHTS])
```

```python
import functools
import math

import numpy as np
import jax
import jax.numpy as jnp
from jax import lax
from jax.experimental import pallas as pl
from jax.experimental.pallas import tpu as pltpu

HEAD_DIM = 128
ROPE_DIM = HEAD_DIM // 4
ROPE_THETA = 500000.0
DIL_PATTERNS = ((128, 1), (512, 4), (2048, 16))
MAX_WINDOW = 2048
NORM_EPS = 1e-6
ADAM_LR = 0.001
ADAM_B1 = 0.9
ADAM_B2 = 0.999
ADAM_EPS = 1e-08
ADAM_WD = 0.01
ADAM_STEP = 10

BF = jnp.bfloat16
F32 = jnp.float32
NEG = -1e30
LANES = 128
ATT_BLOCK = 512
PREP_COLS = 512
ATT_HEADS = 2
VMEM_LIMIT = 56 * 1024 * 1024
MM_VMEM_BUDGET = 40 * 1024 * 1024
N_CHIPS = 4
N_DEV = 8
MESH = pl.DeviceIdType.MESH

SHARDED = ("ffn1_w_gate", "ffn1_w_up", "ffn1_w_down", "w_in", "w_proj_dil", "w_proj_fox", "w_out",
           "ffn2_w_gate", "ffn2_w_up", "ffn2_w_down")
ROW_SHARDED = ("ffn1_w_down", "w_out", "ffn2_w_down")
SMALL = ("ffn1_norm", "mix_norm", "b_forget", "b_gate_dil", "b_gate_fox", "ffn2_norm", "final_norm")
WEIGHTS = ("ffn1_norm", "ffn1_w_gate", "ffn1_w_up", "ffn1_w_down", "mix_norm", "w_in", "b_forget",
           "b_gate_dil", "b_gate_fox", "w_proj_dil", "w_proj_fox", "w_out", "ffn2_norm", "ffn2_w_gate",
           "ffn2_w_up", "ffn2_w_down", "final_norm")


def _pick(n, target, align):
    best = None
    for d in range(align, min(n, target) + 1, align):
        if n % d == 0:
            best = d
    return n if best is None else best


def _params(sem=None):
    return pltpu.CompilerParams(dimension_semantics=sem, vmem_limit_bytes=VMEM_LIMIT)


_DIMS = {"nn": (((1,), (0,)), ((), ())), "nt": (((1,), (1,)), ((), ())), "tn": (((0,), (0,)), ((), ()))}


class _SemsFrom:
    def __init__(self, sems, first):
        self.sems, self.first = sems, first

    @property
    def at(self):
        return self

    def __getitem__(self, k):
        return self.sems.at[self.first + k]


class _Side:
    def __init__(self, inputs, out_shapes, n_sems, start, finish, aliases=None):
        self.inputs, self.out_shapes, self.n_sems = list(inputs), list(out_shapes), n_sems
        self.start, self.finish, self.aliases = start, finish, aliases or {}

    def scratch(self):
        return [pltpu.SemaphoreType.DMA((self.n_sems,)), pltpu.SemaphoreType.DMA((self.n_sems,))]

    def beside(self, other):
        n_in, n_out, n_sems = len(self.inputs), len(self.out_shapes), self.n_sems

        def part(fn_a, fn_b):
            def run(ins, outs, send_sems, recv_sems):
                fn_a(ins[:n_in], outs[:n_out], send_sems, recv_sems)
                fn_b(ins[n_in:], outs[n_out:], _SemsFrom(send_sems, n_sems), _SemsFrom(recv_sems, n_sems))
            return run

        assert not self.aliases and not other.aliases
        return _Side(self.inputs + other.inputs, self.out_shapes + other.out_shapes, n_sems + other.n_sems,
                     part(self.start, other.start), part(self.finish, other.finish))

    def call(self, name):
        n_in, n_out = len(self.inputs), len(self.out_shapes)

        def body(*refs):
            ins, outs, sems = refs[:n_in], refs[n_in:n_in + n_out], refs[n_in + n_out:]
            self.start(ins, outs, *sems)
            self.finish(ins, outs, *sems)

        return pl.pallas_call(body, in_specs=[ANY_SPEC] * n_in, out_specs=[ANY_SPEC] * n_out, out_shape=self.out_shapes,
                              input_output_aliases=self.aliases, scratch_shapes=self.scratch(), name=name)(*self.inputs)


def _mm(name, pairs, mode, m, n, k, *, epilogue, out_dtypes, extras=(), rows=(), n_colsum=0,
        sum_pairs=False, tm=1024, tn=1152, tk=2048, piece_layout=False, side=None, b_off=0, keep_tn=False):
    m_align = LANES if mode == "tn" else 8
    tm = _pick(m, tm, m_align)
    tn = n // N_CHIPS if piece_layout else _pick(n, tn, LANES)
    tk = _pick(k, tk, LANES)
    n_acc = 1 if sum_pairs else len(pairs)
    lhs = []
    for a, _ in pairs:
        if not any(a is t for t in lhs):
            lhs.append(a)
    lhs_of = [next(t for t in range(len(lhs)) if lhs[t] is a) for a, _ in pairs]
    n_mm = len(lhs) + len(pairs)
    n_in = n_mm + len(extras) + len(rows)
    n_out = len(out_dtypes) + n_colsum

    def vmem_bytes(tm_, tn_, tk_):
        tiles = sum(tm_ * tk_ * a.dtype.itemsize for a in lhs) + sum(tn_ * tk_ * b.dtype.itemsize for _, b in pairs)
        tiles += sum(tm_ * tn_ * arr.dtype.itemsize for arr, _ in extras)
        tiles += sum(tm_ * tn_ * jnp.dtype(dt).itemsize for dt in out_dtypes)
        return 2 * tiles + (n_acc + len(extras) + len(out_dtypes)) * tm_ * tn_ * 4

    while vmem_bytes(tm, tn, tk) > MM_VMEM_BUDGET:
        if tn > 512 and not piece_layout and not keep_tn:
            tn = _pick(n, tn - LANES, LANES)
        elif tm > 512:
            tm = _pick(m, tm - m_align, m_align)
        elif tk > 512:
            tk = _pick(k, tk - LANES, LANES)
        elif tm > 256:
            tm = _pick(m, tm - m_align, m_align)
        else:
            break
    nk = k // tk

    n_side_in = len(side.inputs) if side else 0
    n_side_out = len(side.out_shapes) if side else 0
    grid = (m // tm, n // tn, nk)

    def body(*refs):
        ins, refs = refs[:n_in], refs[n_in:]
        side_ins, refs = refs[:n_side_in], refs[n_side_in:]
        outs, refs = refs[:n_out], refs[n_out:]
        side_outs, refs = refs[:n_side_out], refs[n_side_out:]
        accs, side_sems = refs[:n_acc], refs[n_acc:]
        kk = pl.program_id(2)
        if side:
            at = [pl.program_id(t) for t in range(3)]

            @pl.when(jnp.logical_and(jnp.logical_and(at[0] == 0, at[1] == 0), at[2] == 0))
            def _():
                side.start(side_ins, side_outs, *side_sems)

        @pl.when(kk == 0)
        def _():
            for acc in accs:
                acc[...] = jnp.zeros_like(acc)

        a_tiles = [r[...].astype(BF) for r in ins[:len(lhs)]]
        for p in range(len(pairs)):
            b = ins[len(lhs) + p][...].astype(BF)
            accs[0 if sum_pairs else p][...] += lax.dot_general(a_tiles[lhs_of[p]], b, _DIMS[mode],
                                                                preferred_element_type=F32)

        @pl.when(kk == nk - 1)
        def _():
            ex = [r[...] for r in ins[n_mm:n_mm + len(extras)]]
            rw = [r[...] for r in ins[n_mm + len(extras):]]
            res = epilogue([acc[...] for acc in accs], ex, rw)
            for o, r in zip(outs, res):
                o[...] = r.astype(o.dtype)

        if side:
            @pl.when(jnp.logical_and(jnp.logical_and(at[0] == grid[0] - 1, at[1] == grid[1] - 1), at[2] == nk - 1))
            def _():
                side.finish(side_ins, side_outs, *side_sems)

    in_specs, args = [], []
    for a in lhs:
        if mode == "tn":
            in_specs.append(pl.BlockSpec((tk, tm), lambda i, j, kk: (kk, i)))
        else:
            in_specs.append(pl.BlockSpec((tm, tk), lambda i, j, kk: (i, kk)))
        args.append(a)
    assert b_off % tn == 0 and (b_off == 0 or mode == "nn")
    for _, b in pairs:
        if mode == "nt":
            in_specs.append(pl.BlockSpec((tn, tk), lambda i, j, kk: (j, kk)))
        else:
            in_specs.append(pl.BlockSpec((tk, tn), functools.partial(lambda i, j, kk, o: (kk, j + o), o=b_off // tn)))
        args.append(b)
    for arr, off in extras:
        if off is None:
            in_specs.append(pl.BlockSpec((tm, tn), lambda i, j, kk: (i, 0)))
        else:
            assert off % tn == 0
            in_specs.append(pl.BlockSpec((tm, tn), functools.partial(lambda i, j, kk, o: (i, j + o), o=off // tn)))
        args.append(arr)
    for arr in rows:
        in_specs.append(pl.BlockSpec((1, tn), lambda i, j, kk: (0, j)))
        args.append(arr)
    if piece_layout:
        out_specs = [pl.BlockSpec((None, tm, tn), lambda i, j, kk: (j, i, 0)) for _ in out_dtypes]
        out_shape = [jax.ShapeDtypeStruct((n // tn, m, tn), d) for d in out_dtypes]
    else:
        out_specs = [pl.BlockSpec((tm, tn), lambda i, j, kk: (i, j)) for _ in out_dtypes]
        out_shape = [jax.ShapeDtypeStruct((m, n), d) for d in out_dtypes]
    for _ in range(n_colsum):
        out_specs.append(pl.BlockSpec((None, 1, tn), lambda i, j, kk: (i, 0, j)))
        out_shape.append(jax.ShapeDtypeStruct((m // tm, 1, n), F32))
    scratch = [pltpu.VMEM((tm, tn), F32) for _ in range(n_acc)]
    if side is None:
        return pl.pallas_call(
            body, grid=grid, in_specs=in_specs, out_specs=out_specs, out_shape=out_shape, scratch_shapes=scratch,
            compiler_params=_params(("parallel", "parallel", "arbitrary")), name=name)(*args)
    res = pl.pallas_call(
        body, grid=grid, in_specs=in_specs + [ANY_SPEC] * n_side_in, out_specs=out_specs + [ANY_SPEC] * n_side_out,
        out_shape=out_shape + side.out_shapes, scratch_shapes=scratch + side.scratch(),
        input_output_aliases={n_in + t: n_out + o for t, o in side.aliases.items()},
        compiler_params=_params(("arbitrary", "arbitrary", "arbitrary")), name=name)(*args, *side.inputs)
    return res[:n_out], res[n_out:]


def _col_pieces(full):
    rows, cols = full.shape
    return full.reshape(rows, N_CHIPS, cols // N_CHIPS).transpose(1, 0, 2)


def _dw_col_pieces(name, a, b, m, n, k, side=None):
    ident = lambda accs, ex, rw: (accs[0],)
    aligned = (n // N_CHIPS) % LANES == 0
    res = _mm(name, [(a, b)], "tn", m, n, k, epilogue=ident, out_dtypes=(F32,), tm=512 if aligned else 1024,
              piece_layout=aligned, side=side)
    (out,), side_res = res if side else (res, None)
    out = out if aligned else _col_pieces(out)
    return (out, side_res) if side else out


def _hosted(sides, got, key, call):
    make = sides.get(key) if sides else None
    if make is None:
        return call(None)
    outs, got[key] = call(make(got))
    return outs


def _sigmoid(z):
    return 0.5 * jnp.tanh(0.5 * z) + 0.5


def _row_tile(s):
    return _pick(s, 256, 8)


def _fold8(t):
    r, d = t.shape
    return jnp.sum(t.reshape(r // 8, 8, d), axis=0)


def _rms_fwd(name, x, g):
    s, d = x.shape
    tr = _row_tile(s)

    def body(x_ref, g_ref, h_ref):
        xf = x_ref[...]
        y = xf * lax.rsqrt(jnp.mean(xf * xf, axis=-1, keepdims=True) + NORM_EPS)
        h_ref[...] = (y * g_ref[...]).astype(BF)

    return pl.pallas_call(
        body, grid=(s // tr,),
        in_specs=[pl.BlockSpec((tr, d), lambda i: (i, 0)), pl.BlockSpec((1, d), lambda i: (0, 0))],
        out_specs=pl.BlockSpec((tr, d), lambda i: (i, 0)), out_shape=jax.ShapeDtypeStruct((s, d), BF),
        compiler_params=_params(("parallel",)), name=name)(x, g)


def _rms_bwd(name, x, g, dh, dres):
    s, d = x.shape
    tr = _row_tile(s)

    def body(x_ref, g_ref, dh_ref, dres_ref, dx_ref, dxb_ref, dg_ref):
        @pl.when(pl.program_id(0) == 0)
        def _():
            dg_ref[...] = jnp.zeros_like(dg_ref)

        xf = x_ref[...]
        rstd = lax.rsqrt(jnp.mean(xf * xf, axis=-1, keepdims=True) + NORM_EPS)
        xhat = xf * rstd
        dhf = dh_ref[...]
        dg_ref[...] += _fold8(dhf * xhat)
        dxh = dhf * g_ref[...]
        dx = dres_ref[...] + rstd * (dxh - xhat * jnp.mean(dxh * xhat, axis=-1, keepdims=True))
        dx_ref[...] = dx
        dxb_ref[...] = dx.astype(BF)

    blk = pl.BlockSpec((tr, d), lambda i: (i, 0))
    return pl.pallas_call(
        body, grid=(s // tr,),
        in_specs=[blk, pl.BlockSpec((1, d), lambda i: (0, 0)), blk, blk],
        out_specs=[blk, blk, pl.BlockSpec((8, d), lambda i: (0, 0))],
        out_shape=[jax.ShapeDtypeStruct((s, d), F32), jax.ShapeDtypeStruct((s, d), BF),
                   jax.ShapeDtypeStruct((8, d), F32)],
        compiler_params=_params(("arbitrary",)), name=name)(x, g, dh, dres)


def _final(name, x, g, tgt):
    s, d = x.shape
    tr = _row_tile(s)

    def body(x_ref, g_ref, t_ref, dx_ref, dxb_ref, dg_ref, sq_ref):
        @pl.when(pl.program_id(0) == 0)
        def _():
            dg_ref[...] = jnp.zeros_like(dg_ref)
            sq_ref[...] = jnp.zeros_like(sq_ref)

        xf = x_ref[...]
        rstd = lax.rsqrt(jnp.mean(xf * xf, axis=-1, keepdims=True) + NORM_EPS)
        xhat = xf * rstd
        gf = g_ref[...]
        err = xhat * gf - t_ref[...]
        sq_ref[...] += _fold8(err * err)
        dy = err * (1.0 / d)
        dg_ref[...] += _fold8(dy * xhat)
        dxh = dy * gf
        dx = rstd * (dxh - xhat * jnp.mean(dxh * xhat, axis=-1, keepdims=True))
        dx_ref[...] = dx
        dxb_ref[...] = dx.astype(BF)

    blk = pl.BlockSpec((tr, d), lambda i: (i, 0))
    acc = pl.BlockSpec((8, d), lambda i: (0, 0))
    return pl.pallas_call(
        body, grid=(s // tr,), in_specs=[blk, pl.BlockSpec((1, d), lambda i: (0, 0)), blk],
        out_specs=[blk, blk, acc, acc],
        out_shape=[jax.ShapeDtypeStruct((s, d), F32), jax.ShapeDtypeStruct((s, d), BF),
                   jax.ShapeDtypeStruct((8, d), F32), jax.ShapeDtypeStruct((8, d), F32)],
        compiler_params=_params(("arbitrary",)), name=name)(x, g, tgt)


def _ffn_fwd(tag, x, g, w_gate, w_up, get_w_down, sides=None):
    s, d = x.shape
    f = w_gate.shape[1]
    got = {}
    h = _rms_fwd(tag + "_rms", x, g)

    def up_epi(accs, ex, rw):
        a, b = accs
        return a, b, a * _sigmoid(a) * b

    a, b, act = _hosted(sides, got, "up", lambda side: _mm(
        tag + "_up", [(h, w_gate), (h, w_up)], "nn", s, f, d, epilogue=up_epi, out_dtypes=(BF, BF, BF), side=side))

    def down_epi(accs, ex, rw):
        return (ex[0] + 0.5 * accs[0],)

    w_down = get_w_down(got)
    (y,) = _hosted(sides, got, "down", lambda side: _mm(
        tag + "_down", [(act, w_down)], "nn", s, d, f, epilogue=down_epi, out_dtypes=(F32,), extras=[(x, 0)],
        side=side))
    return y, (h, a, b, act), got


def _ffn_bwd(tag, x, g, w_gate, w_up, w_down, saved, dy, dy_bf, sides=None):
    s, d = x.shape
    f = w_gate.shape[1]
    h, a, b, act = saved
    got = {}

    def act_epi(accs, ex, rw):
        dact = 0.5 * accs[0]
        av, bv = ex[0].astype(F32), ex[1].astype(F32)
        sg = _sigmoid(av)
        return dact * bv * (sg * (1.0 + av * (1.0 - sg))), dact * (av * sg)

    da, db = _hosted(sides, got, "dact", lambda side: _mm(
        tag + "_dact", [(dy_bf, w_down)], "nt", s, f, d, epilogue=act_epi, out_dtypes=(BF, BF),
        extras=[(a, 0), (b, 0)], side=side))
    ident = lambda accs, ex, rw: (accs[0],)
    (dw_down,) = _hosted(sides, got, "dwd", lambda side: _mm(
        tag + "_dwd", [(act, dy_bf)], "tn", f, d, s, epilogue=lambda accs, ex, rw: (0.5 * accs[0],),
        out_dtypes=(F32,), side=side))
    got["dw_down"] = dw_down = dw_down.reshape(N_CHIPS, f // N_CHIPS, d)
    got["dw_gate"] = dw_gate = _hosted(sides, got, "dwg", lambda side: _dw_col_pieces(
        tag + "_dwg", h, da, d, f, s, side=side))
    got["dw_up"] = dw_up = _hosted(sides, got, "dwu", lambda side: _dw_col_pieces(
        tag + "_dwu", h, db, d, f, s, side=side))
    (dh,) = _hosted(sides, got, "dh", lambda side: _mm(
        tag + "_dh", [(da, w_gate), (db, w_up)], "nt", s, d, f, epilogue=ident, out_dtypes=(F32,), sum_pairs=True,
        side=side))
    dx, dx_bf, dg = _rms_bwd(tag + "_rmsb", x, g, dh, dy)
    return dx, dx_bf, dg, dw_gate, dw_up, dw_down, got


def _rope_tables(s):
    half = ROPE_DIM // 2
    pos = jnp.arange(s, dtype=F32)
    inv_freq = ROPE_THETA ** (-jnp.arange(0, ROPE_DIM, 2, dtype=F32) / ROPE_DIM)
    ang = pos[:, None] * inv_freq[None, :]
    cos, sin = jnp.cos(ang), jnp.sin(ang)
    rest = HEAD_DIM - ROPE_DIM
    cos_t = jnp.concatenate([cos, cos, jnp.ones((s, rest), F32)], axis=-1)
    sin_t = jnp.concatenate([-sin, sin, jnp.zeros((s, rest), F32)], axis=-1)
    return cos_t, sin_t


def _swap_halves(t):
    lane = lax.broadcasted_iota(jnp.int32, t.shape, 1) & (HEAD_DIM - 1)
    half = ROPE_DIM // 2
    return jnp.where(lane < half, pltpu.roll(t, t.shape[1] - half, 1), pltpu.roll(t, half, 1))


def _dil_bias(blk):
    n_delta = MAX_WINDOW // blk + 1
    delta = jnp.arange(n_delta, dtype=jnp.int32)[:, None, None]
    r = jnp.arange(blk, dtype=jnp.int32)[None, None, :]
    c = jnp.arange(blk, dtype=jnp.int32)[None, :, None]
    o = delta * blk + r - c
    mult = jnp.zeros(o.shape, F32)
    for w, dd in DIL_PATTERNS:
        mult = mult + ((o >= 0) & (o <= w) & (o % dd == 0)).astype(F32)
    return jnp.where(mult > 0, jnp.log(jnp.maximum(mult, 1.0)), NEG)


def _att_block(s):
    return _pick(s, ATT_BLOCK, LANES)


def _fox_aug(name, c_pad, qkv, q_off, k_off, n_heads):
    s = c_pad.shape[0]
    tr = _pick(s, 1024, 16)

    def body(c_ref, q_ref, k_ref, qc_ref, kc_ref):
        h = pl.program_id(1)
        lane = lax.broadcasted_iota(jnp.int32, (tr, LANES), 1)
        ch = jnp.sum(jnp.where(lane == h, c_ref[...], 0.0), axis=1, keepdims=True)
        hi, mid, lo = (t.astype(F32) for t in _split3(ch))
        zero = jnp.zeros((tr, LANES), F32)
        is_hi = jnp.logical_or(lane == 0, lane == 3)
        is_mid = jnp.logical_or(lane == 1, lane == 4)
        parts = jnp.where(is_hi, hi, jnp.where(is_mid, mid, lo))
        qc_ref[:, :HEAD_DIM] = q_ref[...]
        kc_ref[:, :HEAD_DIM] = k_ref[...]
        qc_ref[:, HEAD_DIM:] = jnp.where(lane < 3, 1.0, jnp.where(lane < 6, parts, zero)).astype(BF)
        kc_ref[:, HEAD_DIM:] = jnp.where(lane < 3, -parts, jnp.where(lane < 6, 1.0, zero)).astype(BF)

    head = lambda o: pl.BlockSpec((tr, HEAD_DIM), functools.partial(lambda i, h, o: (i, o + h), o=o))
    spec = pl.BlockSpec((tr, 2 * HEAD_DIM), lambda i, h: (i, h))
    shape = jax.ShapeDtypeStruct((s, n_heads * 2 * HEAD_DIM), BF)
    return pl.pallas_call(
        body, grid=(s // tr, n_heads),
        in_specs=[pl.BlockSpec((tr, LANES), lambda i, h: (i, 0)), head(q_off), head(k_off)],
        out_specs=[spec, spec], out_shape=[shape, shape],
        compiler_params=_params(("parallel", "arbitrary")), name=name)(c_pad, qkv, qkv)


def _causal_mask(st):
    kpos = lax.broadcasted_iota(jnp.int32, st.shape, 0)
    qpos = lax.broadcasted_iota(jnp.int32, st.shape, 1)
    return jnp.where(kpos <= qpos, st, NEG)


def _flash_fwd(name, q_src, k_src, v_src, n_heads, *, fox, tab_t=None):
    (q_arr, q_off, qw), (k_arr, k_off, kw), (v_arr, v_off, _) = q_src, k_src, v_src
    s = q_arr.shape[0]
    blk = _att_block(s)
    nq = s // blk
    n_delta = MAX_WINDOW // blk + 1
    grp = ATT_HEADS
    assert qw == kw and n_heads % grp == 0 and q_off % grp == 0 and k_off % grp == 0 and v_off % grp == 0
    wide = grp * HEAD_DIM

    def body(*refs):
        if fox:
            q_ref, k_ref, v_ref, o_ref, lse_ref, acc, m_s, l_s = refs
        else:
            q_ref, k_ref, v_ref, tab_ref, o_ref, lse_ref, acc, m_s, l_s = refs
        i = pl.program_id(1)
        acc[...] = jnp.zeros_like(acc)
        m_s[...] = jnp.full_like(m_s, NEG)
        l_s[...] = jnp.zeros_like(l_s)

        def step(j, diagonal):
            ks = pl.ds(pl.multiple_of(j * blk, blk), blk)
            for g in range(grp):
                cols = slice(g * HEAD_DIM, (g + 1) * HEAD_DIM)
                qk_cols = slice(g * qw, (g + 1) * qw)
                st = lax.dot_general(k_ref[ks, qk_cols], q_ref[:, qk_cols], _DIMS["nt"], preferred_element_type=F32)
                if fox:
                    if diagonal:
                        st = _causal_mask(st)
                else:
                    st = st + tab_ref[i - j]
                m_prev = m_s[g]
                m_new = jnp.maximum(m_prev, jnp.max(st, axis=0, keepdims=True))
                alpha = jnp.exp(m_prev - m_new)
                p = jnp.exp(st - m_new)
                l_s[g] = alpha * l_s[g] + jnp.sum(p, axis=0, keepdims=True)
                acc[g] = alpha * acc[g] + lax.dot_general(v_ref[ks, cols], p.astype(BF), _DIMS["tn"],
                                                          preferred_element_type=F32)
                m_s[g] = m_new

        def loop_step(j, carry):
            step(j, False)
            return carry

        if fox:
            lax.fori_loop(0, i, loop_step, 0)
            step(i, True)
        else:
            lax.fori_loop(jnp.maximum(i - (n_delta - 1), 0), i + 1, loop_step, 0)
        for g in range(grp):
            o_ref[:, g * HEAD_DIM:(g + 1) * HEAD_DIM] = (acc[g] / l_s[g]).T.astype(o_ref.dtype)
            lse_ref[g] = m_s[g] + jnp.log(l_s[g])

    off = lambda o: functools.partial(lambda h, i, o: (0, o + h), o=o // grp)
    in_specs = [pl.BlockSpec((blk, grp * qw), functools.partial(lambda h, i, o: (i, o + h), o=q_off // grp)),
                pl.BlockSpec((s, grp * kw), off(k_off)), pl.BlockSpec((s, wide), off(v_off))]
    args = [q_arr, k_arr, v_arr]
    if not fox:
        in_specs.append(pl.BlockSpec((n_delta, blk, blk), lambda h, i: (0, 0, 0)))
        args.append(tab_t)
    return pl.pallas_call(
        body, grid=(n_heads // grp, nq), in_specs=in_specs,
        out_specs=[pl.BlockSpec((blk, wide), lambda h, i: (i, h)),
                   pl.BlockSpec((grp, None, 1, blk), lambda h, i: (h, i, 0, 0))],
        out_shape=[jax.ShapeDtypeStruct((s, n_heads * HEAD_DIM), BF),
                   jax.ShapeDtypeStruct((n_heads, nq, 1, blk), F32)],
        scratch_shapes=[pltpu.VMEM((grp, HEAD_DIM, blk), F32), pltpu.VMEM((grp, 1, blk), F32),
                        pltpu.VMEM((grp, 1, blk), F32)],
        compiler_params=_params(("parallel", "parallel")), name=name)(*args)


def _att_delta(name, do, o, n_heads):
    s = do.shape[0]
    blk = _pick(s, 1024, 16)

    def body(do_ref, o_ref, d_ref):
        d_ref[...] = jnp.sum(do_ref[...].astype(F32) * o_ref[...].astype(F32), axis=-1, keepdims=True)

    spec = pl.BlockSpec((blk, HEAD_DIM), lambda h, i: (i, h))
    return pl.pallas_call(
        body, grid=(n_heads, s // blk), in_specs=[spec, spec],
        out_specs=pl.BlockSpec((None, blk, 1), lambda h, i: (h, i, 0)),
        out_shape=jax.ShapeDtypeStruct((n_heads, s, 1), F32),
        compiler_params=_params(("parallel", "parallel")), name=name)(do, o)


def _flash_bwd(name, q_src, k_src, v_src, do, lse_row, delta_row, n_heads, *, fox, tab_t=None):
    (q_arr, q_off, qw), (k_arr, k_off, kw), (v_arr, v_off, _) = q_src, k_src, v_src
    s = q_arr.shape[0]
    blk = _att_block(s)
    nq = s // blk
    n_delta = MAX_WINDOW // blk + 1
    grp = ATT_HEADS
    assert qw == kw and n_heads % grp == 0 and q_off % grp == 0 and k_off % grp == 0 and v_off % grp == 0
    wide = grp * HEAD_DIM

    def body(*refs):
        if fox:
            (q_ref, do_ref, k_ref, v_ref, lse_ref, dl_ref,
             dq_ref, dk_ref, dv_ref, dc_ref, dcq_ref, dk_acc, dv_acc, dc_acc) = refs
        else:
            (q_ref, do_ref, k_ref, v_ref, lse_ref, dl_ref, tab_ref,
             dq_ref, dk_ref, dv_ref, dk_acc, dv_acc) = refs
        j = pl.program_id(1)

        @pl.when(j == 0)
        def _():
            dq_ref[...] = jnp.zeros_like(dq_ref)
            if fox:
                dcq_ref[...] = jnp.zeros_like(dcq_ref)

        dk_acc[...] = jnp.zeros_like(dk_acc)
        dv_acc[...] = jnp.zeros_like(dv_acc)
        if fox:
            dc_acc[...] = jnp.zeros_like(dc_acc)

        def step(i, diagonal):
            qs = pl.ds(pl.multiple_of(i * blk, blk), blk)
            for g in range(grp):
                cols = slice(g * HEAD_DIM, (g + 1) * HEAD_DIM)
                qk_cols = slice(g * qw, (g + 1) * qw)
                plain = slice(g * qw, g * qw + HEAD_DIM)
                kb, vb = k_ref[:, plain], v_ref[:, cols]
                qb, dob = q_ref[qs, plain], do_ref[qs, cols]
                st = lax.dot_general(k_ref[:, qk_cols], q_ref[qs, qk_cols], _DIMS["nt"], preferred_element_type=F32)
                if fox:
                    if diagonal:
                        st = _causal_mask(st)
                else:
                    st = st + tab_ref[i - j]
                pt = jnp.exp(st - lse_ref[g, i])
                dv_acc[:, cols] += jnp.dot(pt.astype(BF), dob, preferred_element_type=F32)
                dpt = lax.dot_general(vb, dob, _DIMS["nt"], preferred_element_type=F32)
                dst = pt * (dpt - dl_ref[g, i])
                dsb = dst.astype(BF)
                dk_acc[:, cols] += jnp.dot(dsb, qb, preferred_element_type=F32)
                dq_ref[qs, cols] += lax.dot_general(dsb, kb, _DIMS["tn"], preferred_element_type=F32)
                if fox:
                    folded = dst[:, :LANES]
                    for part in range(1, blk // LANES):
                        folded = folded + dst[:, part * LANES:(part + 1) * LANES]
                    dc_acc[g] -= folded
                    dcq_ref[g, i] += jnp.sum(dst, axis=0, keepdims=True)

        def loop_step(i, carry):
            step(i, False)
            return carry

        if fox:
            step(j, True)
            lax.fori_loop(j + 1, nq, loop_step, 0)
        else:
            lax.fori_loop(j, jnp.minimum(nq, j + n_delta), loop_step, 0)
        dk_ref[...] = dk_acc[...]
        dv_ref[...] = dv_acc[...].astype(dv_ref.dtype)
        if fox:
            for g in range(grp):
                dc_ref[g] = jnp.sum(dc_acc[g], axis=-1, keepdims=True)

    full = lambda o, wd=wide: pl.BlockSpec((s, wd), functools.partial(lambda h, j, o: (0, o + h), o=o // grp))
    tile = lambda o, wd=wide: pl.BlockSpec((blk, wd), functools.partial(lambda h, j, o: (j, o + h), o=o // grp))
    per_q = pl.BlockSpec((grp, nq, 1, blk), lambda h, j: (h, 0, 0, 0))
    per_k = pl.BlockSpec((grp, blk, 1), lambda h, j: (h, j, 0))
    in_specs = [full(q_off, grp * qw), full(0), tile(k_off, grp * kw), tile(v_off), per_q, per_q]
    args = [q_arr, do, k_arr, v_arr, lse_row, delta_row]
    out_specs = [full(0), tile(0), tile(0)]
    hd = n_heads * HEAD_DIM
    out_shape = [jax.ShapeDtypeStruct((s, hd), F32), jax.ShapeDtypeStruct((s, hd), F32),
                 jax.ShapeDtypeStruct((s, hd), BF)]
    scratch = [pltpu.VMEM((blk, wide), F32), pltpu.VMEM((blk, wide), F32)]
    if fox:
        out_specs += [per_k, per_q]
        out_shape += [jax.ShapeDtypeStruct((n_heads, s, 1), F32), jax.ShapeDtypeStruct((n_heads, nq, 1, blk), F32)]
        scratch.append(pltpu.VMEM((grp, blk, LANES), F32))
    else:
        in_specs.append(pl.BlockSpec((n_delta, blk, blk), lambda h, j: (0, 0, 0)))
        args.append(tab_t)
    return pl.pallas_call(
        body, grid=(n_heads // grp, nq), in_specs=in_specs, out_specs=out_specs, out_shape=out_shape,
        scratch_shapes=scratch, compiler_params=_params(("parallel", "arbitrary")), name=name)(*args)


def _split3(t):
    hi = t.astype(BF)
    r1 = t - hi.astype(F32)
    mid = r1.astype(BF)
    lo = (r1 - mid.astype(F32)).astype(BF)
    return hi, mid, lo


def _tri_dot(tri, t):
    hi, mid, lo = _split3(t)
    return (jnp.dot(tri, hi, preferred_element_type=F32) + jnp.dot(tri, mid, preferred_element_type=F32)
            + jnp.dot(tri, lo, preferred_element_type=F32))


def _log_sigmoid(z):
    return jnp.minimum(z, 0.0) - jnp.log(1.0 + jnp.exp(-jnp.abs(z)))


def _forget_cumsum(name, proj, f_col, bias):
    s = proj.shape[0]
    blk = _att_block(s)

    def body(f_ref, b_ref, c_ref, carry):
        @pl.when(pl.program_id(0) == 0)
        def _():
            carry[...] = jnp.zeros_like(carry)

        lf = _log_sigmoid(f_ref[...] + b_ref[...])
        r = lax.broadcasted_iota(jnp.int32, (blk, blk), 0)
        c = lax.broadcasted_iota(jnp.int32, (blk, blk), 1)
        tri = (c <= r).astype(BF)
        c_ref[...] = _tri_dot(tri, lf) + carry[...]
        carry[...] = c_ref[pl.ds(blk - 1, 1), :]

    return pl.pallas_call(
        body, grid=(s // blk,),
        in_specs=[pl.BlockSpec((blk, LANES), lambda i: (i, f_col)), pl.BlockSpec((1, LANES), lambda i: (0, 0))],
        out_specs=pl.BlockSpec((blk, LANES), lambda i: (i, 0)), out_shape=jax.ShapeDtypeStruct((s, LANES), F32),
        scratch_shapes=[pltpu.VMEM((1, LANES), F32)],
        compiler_params=_params(("arbitrary",)), name=name)(proj, bias)


def _forget_bwd(name, proj, f_col, bias, dc):
    s = proj.shape[0]
    blk = _att_block(s)
    nb = s // blk

    def body(f_ref, b_ref, dc_ref, df_ref, db_ref, carry):
        @pl.when(pl.program_id(0) == 0)
        def _():
            carry[...] = jnp.zeros_like(carry)
            db_ref[...] = jnp.zeros_like(db_ref)

        r = lax.broadcasted_iota(jnp.int32, (blk, blk), 0)
        c = lax.broadcasted_iota(jnp.int32, (blk, blk), 1)
        tri = (c >= r).astype(BF)
        r = lax.broadcasted_iota(jnp.int32, (blk, LANES), 0)
        dlf = _tri_dot(tri, dc_ref[...]) + carry[...]
        carry[...] = jnp.sum(jnp.where(r == 0, dlf, 0.0), axis=0, keepdims=True)
        dz = dlf * _sigmoid(-(f_ref[...] + b_ref[...]))
        df_ref[...] = dz.astype(BF)
        db_ref[...] += _fold8(dz)

    rev = lambda i: (nb - 1 - i, 0)
    return pl.pallas_call(
        body, grid=(nb,),
        in_specs=[pl.BlockSpec((blk, LANES), lambda i: (nb - 1 - i, f_col)), pl.BlockSpec((1, LANES), lambda i: (0, 0)),
                  pl.BlockSpec((blk, LANES), rev)],
        out_specs=[pl.BlockSpec((blk, LANES), rev), pl.BlockSpec((8, LANES), lambda i: (0, 0))],
        out_shape=[jax.ShapeDtypeStruct((s, LANES), BF), jax.ShapeDtypeStruct((8, LANES), F32)],
        scratch_shapes=[pltpu.VMEM((1, LANES), F32)],
        compiler_params=_params(("arbitrary",)), name=name)(proj, bias, dc)


def _dproj_assemble(name, parts, gates, cos_t, sin_t, n_heads, d_model):
    s = cos_t.shape[0]
    tr = _pick(s, 512, 16)
    scale = HEAD_DIM ** -0.5
    hd = n_heads * HEAD_DIM
    cw = _pick(math.gcd(hd, d_model), PREP_COLS, HEAD_DIM)
    widths = [hd] * 6 + [d_model] * 2
    starts = [sum(widths[:t]) // cw for t in range(len(widths) + 1)]
    cos_w, sin_w = jnp.tile(cos_t, (1, cw // HEAD_DIM)), jnp.tile(sin_t, (1, cw // HEAD_DIM))

    def body(*refs):
        p_refs, cos_ref, sin_ref, o_ref = refs[:8], refs[8], refs[9], refs[10]
        j = pl.program_id(1)
        for kind in range(8):
            @pl.when(jnp.logical_and(j >= starts[kind], j < starts[kind + 1]))
            def _(kind=kind):
                t = p_refs[kind][...]
                if kind in (0, 1, 3):
                    t = t.astype(F32)
                if kind in (0, 3):
                    t = t * scale
                if kind in (0, 1):
                    t = t * cos_ref[...] - _swap_halves(t) * sin_ref[...]
                o_ref[...] = t.astype(BF)

    def part_spec(kind):
        return pl.BlockSpec((tr, cw), functools.partial(
            lambda i, j, kind: (i, jnp.clip(j - starts[kind], 0, widths[kind] // cw - 1)), kind=kind))

    tab = pl.BlockSpec((tr, cw), lambda i, j: (i, 0))
    return pl.pallas_call(
        body, grid=(s // tr, starts[-1]), in_specs=[part_spec(kind) for kind in range(8)] + [tab, tab],
        out_specs=pl.BlockSpec((tr, cw), lambda i, j: (i, j)),
        out_shape=jax.ShapeDtypeStruct((s, sum(widths)), BF),
        compiler_params=_params(("parallel", "arbitrary")), name=name)(*parts, *gates, cos_w, sin_w)


FFN2 =("ffn2_w_gate", "ffn2_w_up", "ffn2_w_down")


def _device_step(x, tgt, shards, small):
    s, d = x.shape
    hd = shards["w_proj_dil"].shape[0]
    hh = hd // HEAD_DIM
    blk = _att_block(s)
    gate_off = 6 * hd
    f_col = 0
    n_proj = gate_off + 2 * d
    ident = lambda accs, ex, rw: (accs[0],)
    chip = 2 * lax.axis_index("x") + lax.axis_index("y")
    ids = jnp.stack([chip, lax.axis_index("c")]).astype(jnp.int32)
    w = {}

    def take_gathered(names, gathered):
        for n, t in zip(names, gathered):
            t = lax.dynamic_update_index_in_dim(t, shards[n], chip, 0)
            if n == "w_in":
                w["w_in"], w["w_in_f"] = _repack_w_in(t, hd, d)
            else:
                w[n] = _full_from_pieces(n, t)

    def chip_sums(names, pieces, from_sibling):
        return [_rs_add("rs_add_" + n, ids, g, o) for n, g, o in zip(names, pieces, from_sibling)]

    def core_halves(names, sums, from_chips):
        return [_rs_sum("rs_sum_" + n, ids, own, got) for n, own, got in zip(names, sums, from_chips)]

    def gather(names):
        return _gather_side([shards[n] for n in names])

    first, under_up = ("ffn1_w_gate", "ffn1_w_up"), ("ffn1_w_down", "w_in")
    under_down = ("w_proj_dil", "w_proj_fox", "w_out", "ffn2_w_gate")
    under_proj = ("ffn2_w_up", "ffn2_w_down")
    take_gathered(first, gather(first).call("gather_first"))

    def w_down_of_ffn1(got):
        take_gathered(under_up, got["up"])
        return w["ffn1_w_down"]

    x1, saved1, got = _ffn_fwd("ffn1", x, small["ffn1_norm"], w["ffn1_w_gate"], w["ffn1_w_up"], w_down_of_ffn1,
                               sides={"up": lambda got: gather(under_up), "down": lambda got: gather(under_down)})
    take_gathered(under_down, got["down"])

    hm = _rms_fwd("mix_rms", x1, small["mix_norm"])
    cos_t, sin_t = _rope_tables(s)
    head_tile = _pick(hd, 1024, HEAD_DIM)
    tiles_per_kind = hd // head_tile
    scale = HEAD_DIM ** -0.5

    def heads_epi(accs, ex, rw):
        kind = pl.program_id(1) // tiles_per_kind
        t = accs[0]
        r = t * ex[0] + _swap_halves(t) * ex[1]
        t = jnp.where(kind < 2, r, t)
        return (jnp.where(jnp.logical_or(kind == 0, kind == 3), t * scale, t),)

    wide = lambda tab: jnp.tile(tab, (1, head_tile // HEAD_DIM))
    (qkv,), gathered = _mm("proj_heads", [(hm, w["w_in"])], "nn", s, gate_off, d, epilogue=heads_epi, out_dtypes=(BF,),
                           extras=[(wide(cos_t), None), (wide(sin_t), None)], tn=head_tile, keep_tn=True,
                           side=gather(under_proj))
    take_gathered(under_proj, gathered)
    (gates,) = _mm("proj_gates", [(hm, w["w_in"])], "nn", s, 2 * d, d, epilogue=ident, out_dtypes=(F32,),
                   b_off=gate_off)
    (f_logit,) = _mm("proj_f", [(hm, w["w_in_f"])], "nn", s, LANES, d, epilogue=ident, out_dtypes=(F32,))
    tab_t = _dil_bias(blk)
    dil_src = ((qkv, 0, HEAD_DIM), (qkv, hh, HEAD_DIM), (qkv, 2 * hh, HEAD_DIM))
    y_dil, lse_d = _flash_fwd("dil_fwd", *dil_src, hh, fox=False, tab_t=tab_t)
    bias_f = jnp.pad(small["b_forget"], ((0, 0), (0, LANES - hh)))
    c_pad = _forget_cumsum("forget_cumsum", f_logit, f_col, bias_f)
    q_cat, k_cat = _fox_aug("fox_aug", c_pad, qkv, 3 * hh, 4 * hh, hh)
    fox_src = ((q_cat, 0, 2 * HEAD_DIM), (k_cat, 0, 2 * HEAD_DIM), (qkv, 5 * hh, HEAD_DIM))
    y_fox, lse_f = _flash_fwd("fox_fwd", *fox_src, hh, fox=True)

    def merge_epi(accs, ex, rw):
        ud, uf = accs
        return ud, uf, _sigmoid(ex[0] + rw[0]) * ud + _sigmoid(ex[1] + rw[1]) * uf

    u_d, u_f, merged = _mm("merge", [(y_dil, w["w_proj_dil"]), (y_fox, w["w_proj_fox"])], "nn", s, d, hd,
                           epilogue=merge_epi, out_dtypes=(BF, BF, BF),
                           extras=[(gates, 0), (gates, d)],
                           rows=[small["b_gate_dil"], small["b_gate_fox"]])
    (x2,) = _mm("mix_out", [(merged, w["w_out"])], "nn", s, d, d,
                epilogue=lambda accs, ex, rw: (ex[0] + accs[0],), out_dtypes=(F32,), extras=[(x1, 0)])

    x3, saved2, _ = _ffn_fwd("ffn2", x2, small["ffn2_norm"], w["ffn2_w_gate"], w["ffn2_w_up"],
                             lambda got: w["ffn2_w_down"])
    dx3, dx3_bf, dg_final, sq = _final("final", x3, small["final_norm"].reshape(1, d), tgt)

    dx2, dx2_bf, dg_ffn2, dw_g2, dw_u2, dw_d2, _ = _ffn_bwd("ffn2", x2, small["ffn2_norm"], w["ffn2_w_gate"],
                                                            w["ffn2_w_up"], w["ffn2_w_down"], saved2, dx3, dx3_bf)
    pieces2 = [dw_g2, dw_u2, dw_d2]

    def dmerge_epi(accs, ex, rw):
        dm = accs[0]
        gd, gf, ud, uf = ex[0], ex[1], ex[2].astype(F32), ex[3].astype(F32)
        sd, sf = _sigmoid(gd + rw[0]), _sigmoid(gf + rw[1])
        dgd = dm * ud * (sd * (1.0 - sd))
        dgf = dm * uf * (sf * (1.0 - sf))
        return (dm * sd, dm * sf, dgd, dgf, jnp.sum(dgd, axis=0, keepdims=True), jnp.sum(dgf, axis=0, keepdims=True))

    (du_d, du_f, dg_d, dg_f, dbg_d, dbg_f), from_sibling = _mm(
        "dmerge", [(dx2_bf, w["w_out"])], "nt", s, d, d, epilogue=dmerge_epi, out_dtypes=(BF, BF, BF, BF), n_colsum=2,
        extras=[(gates, 0), (gates, d), (u_d, 0), (u_f, 0)],
        rows=[small["b_gate_dil"], small["b_gate_fox"]], side=_swap_side(pieces2))
    sums2 = chip_sums(FFN2, pieces2, from_sibling)
    (dw_out,) = _mm("dw_out", [(merged, dx2_bf)], "tn", d, d, s, epilogue=ident, out_dtypes=(F32,))
    dw_out = dw_out.reshape(N_CHIPS, d // N_CHIPS, d)
    dw_pd = _dw_col_pieces("dw_pd", y_dil, du_d, hd, d, s)
    dw_pf = _dw_col_pieces("dw_pf", y_fox, du_f, hd, d, s)
    (dy_dil,) = _mm("dy_dil", [(du_d, w["w_proj_dil"])], "nt", s, hd, d, epilogue=ident, out_dtypes=(BF,))
    (dy_fox,) = _mm("dy_fox", [(du_f, w["w_proj_fox"])], "nt", s, hd, d, epilogue=ident, out_dtypes=(BF,))

    row = lambda t: t.reshape(hh, s // blk, 1, blk)
    delta_d = _att_delta("dil_delta", dy_dil, y_dil, hh)
    dq_d, dk_d, dv_d = _flash_bwd("dil_bwd", *dil_src, dy_dil, lse_d, row(delta_d), hh, fox=False, tab_t=tab_t)
    delta_f = _att_delta("fox_delta", dy_fox, y_fox, hh)
    dq_f, dk_f, dv_f, dc_k, dc_q = _flash_bwd("fox_bwd", *fox_src, dy_fox, lse_f, row(delta_f), hh, fox=True)
    dc = dc_k.reshape(hh, s) + dc_q.reshape(hh, s)
    dc_pad = jnp.pad(dc.T, ((0, 0), (0, LANES - hh)))
    df, db_forget = _forget_bwd("forget_bwd", f_logit, f_col, bias_f, dc_pad)
    dproj = _dproj_assemble("dproj", [dq_d, dk_d, dv_d, dq_f, dk_f, dv_f], [dg_d, dg_f], cos_t, sin_t, hh, d)
    (dhm_f,) = _mm("dhm_f", [(df, w["w_in_f"])], "nt", s, d, LANES, epilogue=ident, out_dtypes=(F32,))
    (dhm,), from_chips = _mm("dhm", [(dproj, w["w_in"])], "nt", s, d, n_proj,
                             epilogue=lambda accs, ex, rw: (accs[0] + ex[0],), out_dtypes=(F32,),
                             extras=[(dhm_f, 0)], side=_scatter_side(sums2))
    halves2 = core_halves(FFN2, sums2, from_chips)
    small_mixer = ("w_proj_dil", "w_proj_fox", "w_out")
    pieces_sm = [dw_pd, dw_pf, dw_out]
    (dw_in,), from_sibling = _mm("dw_in", [(hm, dproj)], "tn", d, n_proj, s, epilogue=ident, out_dtypes=(F32,),
                                 side=_swap_side(pieces_sm))
    sums_sm = chip_sums(small_mixer, pieces_sm, from_sibling)
    (dw_in_f,) = _mm("dw_in_f", [(hm, df)], "tn", d, LANES, s, epilogue=ident, out_dtypes=(F32,))
    dw_in = _dw_in_pieces(dw_in, dw_in_f, hd, d)
    dx1, dx1_bf, dg_mix = _rms_bwd("mix_rmsb", x1, small["mix_norm"], dhm, dx2)

    sums = {}

    def under_dwd(got):
        sums["w_in"] = chip_sums(["w_in"], [dw_in], got["dact"])
        return _scatter_side(sums_sm)

    def under_dwu(got):
        return _swap_side([got["dw_down"], got["dw_gate"]])

    def under_dh(got):
        sums["dg"] = chip_sums(["ffn1_w_down", "ffn1_w_gate"], [got["dw_down"], got["dw_gate"]], got["dwu"])
        return _scatter_side(sums["dg"]).beside(_swap_side([got["dw_up"]]))

    dx0, _, dg_ffn1, dw_g1, dw_u1, dw_d1, got = _ffn_bwd(
        "ffn1", x, small["ffn1_norm"], w["ffn1_w_gate"], w["ffn1_w_up"], w["ffn1_w_down"], saved1, dx1, dx1_bf,
        sides={"dact": lambda got: _swap_side([dw_in]), "dwd": under_dwd,
               "dwg": lambda got: _scatter_side(sums["w_in"]), "dwu": under_dwu, "dh": under_dh})
    halves_sm = core_halves(small_mixer, sums_sm, got["dwd"])
    halves_in = core_halves(["w_in"], sums["w_in"], got["dwg"])
    halves_dg = core_halves(["ffn1_w_down", "ffn1_w_gate"], sums["dg"], got["dh"][:2])
    sums_u = chip_sums(["ffn1_w_up"], [dw_u1], got["dh"][2:])
    halves_u = core_halves(["ffn1_w_up"], sums_u, _scatter_side(sums_u).call("rs_scatter_last"))

    names = ("ffn1_w_down", "ffn1_w_gate", "ffn1_w_up", "w_in") + small_mixer + FFN2
    totals = _join_side(halves_dg + halves_u + halves_in + halves_sm + halves2).call("rs_join")
    grads = dict(zip(names, totals))
    partials = {"ffn1_norm": dg_ffn1, "mix_norm": dg_mix, "ffn2_norm": dg_ffn2, "final_norm": dg_final,
                "b_gate_dil": dbg_d.reshape(-1, d), "b_gate_fox": dbg_f.reshape(-1, d), "b_forget": db_forget, "sq": sq}
    return dx0, grads, partials


def _coords():
    return lax.axis_index("x"), lax.axis_index("y"), lax.axis_index("c")


def _other_chips(x, y):
    return [(1 - x, y), (x, 1 - y), (1 - x, 1 - y)]


ANY_SPEC = pl.BlockSpec(memory_space=pl.ANY)


def _gather_side(shards):
    nw = len(shards)

    def copies(srcs, outs, send_sems, recv_sems):
        x, y, c = _coords()
        chips = _other_chips(x, y)

        def slot(w, px, py, pc):
            half = shards[w].shape[0] // 2
            return outs[w].at[2 * px + py, pl.ds(pc * half, half), :]

        def copy(w, k, src_ref, dst_ref, to):
            return pltpu.make_async_remote_copy(src_ref=src_ref, dst_ref=dst_ref, send_sem=send_sems.at[6 * w + k],
                                                recv_sem=recv_sems.at[6 * w + k], device_id=to, device_id_type=MESH)

        first, arrive, passed, arrive2 = [], [], [], []
        for w in range(nw):
            half = shards[w].shape[0] // 2
            for j, chip in enumerate(chips):
                first.append(copy(w, j, srcs[w].at[pl.ds(c * half, half), :], slot(w, x, y, c), (*chip, c)))
                arrive.append(copy(w, j, slot(w, *chip, c), slot(w, *chip, c), (*chip, c)))
                passed.append(copy(w, 3 + j, slot(w, *chip, c), slot(w, *chip, c), (x, y, 1 - c)))
                arrive2.append(copy(w, 3 + j, slot(w, *chip, 1 - c), slot(w, *chip, 1 - c), (x, y, 1 - c)))
        return first, arrive, passed, arrive2

    def start(srcs, outs, send_sems, recv_sems):
        for cp in copies(srcs, outs, send_sems, recv_sems)[0]:
            cp.start()

    def finish(srcs, outs, send_sems, recv_sems):
        first, arrive, passed, arrive2 = copies(srcs, outs, send_sems, recv_sems)
        for got, fwd in zip(arrive, passed):
            got.wait_recv()
            fwd.start()
        for got in arrive2:
            got.wait_recv()
        for cp in first + passed:
            cp.wait_send()

    return _Side(shards, [jax.ShapeDtypeStruct((N_CHIPS, *t.shape), t.dtype) for t in shards], 6 * nw, start, finish)


def _swap_side(grads):
    nw = len(grads)

    def copies(srcs, outs, send_sems, recv_sems):
        x, y, c = _coords()
        res = []
        for w in range(nw):
            half = grads[w].shape[1] // 2
            for p in range(N_CHIPS):
                k = N_CHIPS * w + p
                res.append(pltpu.make_async_remote_copy(
                    src_ref=srcs[w].at[p, pl.ds((1 - c) * half, half), :], dst_ref=outs[w].at[p],
                    send_sem=send_sems.at[k], recv_sem=recv_sems.at[k], device_id=(x, y, 1 - c), device_id_type=MESH))
        return res

    def start(*refs):
        for cp in copies(*refs):
            cp.start()

    def finish(*refs):
        for cp in copies(*refs):
            cp.wait()

    shapes = [jax.ShapeDtypeStruct((N_CHIPS, t.shape[1] // 2, t.shape[2]), t.dtype) for t in grads]
    return _Side(grads, shapes, N_CHIPS * nw, start, finish)


def _rs_add(name, ids, g, other):
    n, rows, cols = g.shape
    half = rows // 2
    tr = _pick(half, 256, 16)
    nb = half // tr

    def body(ids_ref, g_ref, o_ref, out_ref):
        out_ref[...] = (g_ref[...] + o_ref[...]).astype(BF)

    grid_spec = pltpu.PrefetchScalarGridSpec(
        num_scalar_prefetch=1, grid=(n, nb),
        in_specs=[pl.BlockSpec((None, tr, cols), lambda p, i, ids_ref: (p, ids_ref[1] * nb + i, 0)),
                  pl.BlockSpec((None, tr, cols), lambda p, i, ids_ref: (p, i, 0))],
        out_specs=pl.BlockSpec((None, tr, cols), lambda p, i, ids_ref: (p, i, 0)))
    return pl.pallas_call(body, grid_spec=grid_spec, out_shape=jax.ShapeDtypeStruct((n, half, cols), BF),
                          compiler_params=_params(("parallel", "parallel")), name=name)(ids, g, other)


def _scatter_side(sums):
    nw = len(sums)

    def copies(srcs, outs, send_sems, recv_sems):
        x, y, c = _coords()
        res = []
        for w in range(nw):
            for k, (px, py) in enumerate(_other_chips(x, y)):
                res.append(pltpu.make_async_remote_copy(
                    src_ref=srcs[w].at[2 * px + py], dst_ref=outs[w].at[k], send_sem=send_sems.at[3 * w + k],
                    recv_sem=recv_sems.at[3 * w + k], device_id=(px, py, c), device_id_type=MESH))
        return res

    def start(*refs):
        for cp in copies(*refs):
            cp.start()

    def finish(*refs):
        for cp in copies(*refs):
            cp.wait()

    return _Side(sums, [jax.ShapeDtypeStruct((3, *t.shape[1:]), t.dtype) for t in sums], 3 * nw, start, finish)


def _rs_sum(name, ids, own, got):
    n, half, cols = own.shape
    tr = _pick(half, 256, 16)
    nb = half // tr

    def body(ids_ref, own_ref, got_ref, out_ref):
        t = own_ref[...].astype(F32)
        for k in range(3):
            t = t + got_ref[k].astype(F32)
        out_ref[...] = t

    grid_spec = pltpu.PrefetchScalarGridSpec(
        num_scalar_prefetch=1, grid=(nb,),
        in_specs=[pl.BlockSpec((None, tr, cols), lambda i, ids_ref: (ids_ref[0], i, 0)),
                  pl.BlockSpec((3, tr, cols), lambda i, ids_ref: (0, i, 0))],
        out_specs=pl.BlockSpec((tr, cols), lambda i, ids_ref: (ids_ref[1] * nb + i, 0)))
    return pl.pallas_call(body, grid_spec=grid_spec, out_shape=jax.ShapeDtypeStruct((2 * half, cols), F32),
                          compiler_params=_params(("parallel",)), name=name)(ids, own, got)


def _join_side(totals):
    nw = len(totals)

    def start(ins, bufs, send_sems, recv_sems):
        x, y, c = _coords()
        for w in range(nw):
            half = totals[w].shape[0] // 2
            pltpu.make_async_remote_copy(
                src_ref=bufs[w].at[pl.ds(c * half, half), :], dst_ref=bufs[w].at[pl.ds(c * half, half), :],
                send_sem=send_sems.at[w], recv_sem=recv_sems.at[w], device_id=(x, y, 1 - c),
                device_id_type=MESH).start()

    def finish(ins, bufs, send_sems, recv_sems):
        x, y, c = _coords()
        for w in range(nw):
            half = totals[w].shape[0] // 2
            arrival = pltpu.make_async_remote_copy(
                src_ref=bufs[w].at[pl.ds(c * half, half), :], dst_ref=bufs[w].at[pl.ds((1 - c) * half, half), :],
                send_sem=send_sems.at[w], recv_sem=recv_sems.at[w], device_id=(x, y, 1 - c), device_id_type=MESH)
            arrival.wait_recv()
            arrival.wait_send()

    return _Side(totals, [jax.ShapeDtypeStruct(t.shape, t.dtype) for t in totals], nw, start, finish,
                 aliases={w: w for w in range(nw)})


def _gather_all(name, t):
    rows, cols = t.shape

    def body(src, out, send_sems, recv_sems, local_sem):
        x, y, c = _coords()
        me = 4 * x + 2 * y + c
        mine = pltpu.make_async_copy(src, out.at[me], local_sem)
        mine.start()
        peers = [(x ^ (k >> 2 & 1), y ^ (k >> 1 & 1), c ^ (k & 1)) for k in range(1, N_DEV)]
        sends = [pltpu.make_async_remote_copy(src_ref=src, dst_ref=out.at[me], send_sem=send_sems.at[k],
                                              recv_sem=recv_sems.at[k], device_id=peer, device_id_type=MESH)
                 for k, peer in enumerate(peers)]
        for cp in sends:
            cp.start()
        for k, (px, py, pc) in enumerate(peers):
            pltpu.make_async_remote_copy(src_ref=src, dst_ref=out.at[4 * px + 2 * py + pc], send_sem=send_sems.at[k],
                                         recv_sem=recv_sems.at[k], device_id=(px, py, pc),
                                         device_id_type=MESH).wait_recv()
        for cp in sends:
            cp.wait_send()
        mine.wait()

    vmem = pl.BlockSpec(memory_space=pltpu.VMEM)
    return pl.pallas_call(
        body, in_specs=[vmem], out_specs=vmem, out_shape=jax.ShapeDtypeStruct((N_DEV, rows, cols), t.dtype),
        scratch_shapes=[pltpu.SemaphoreType.DMA((7,)), pltpu.SemaphoreType.DMA((7,)), pltpu.SemaphoreType.DMA],
        name=name)(t)


def _adamw_math(w, g, m, v):
    m = ADAM_B1 * m + (1.0 - ADAM_B1) * g
    v = ADAM_B2 * v + (1.0 - ADAM_B2) * (g * g)
    m_hat = m / (1.0 - ADAM_B1 ** ADAM_STEP)
    v_hat = v / (1.0 - ADAM_B2 ** ADAM_STEP)
    delta = -ADAM_LR * (m_hat / (jnp.sqrt(v_hat) + ADAM_EPS) + ADAM_WD * w)
    return delta, m, v


def _adamw(name, w, g, m, v):
    _, rows, cols = w.shape
    tr = _pick(rows, 256, 8)

    def body(w_ref, g_ref, m_ref, v_ref, g_out, d_out, m_out, v_out):
        g = g_ref[...]
        g_out[...] = g
        d_out[...], m_out[...], v_out[...] = _adamw_math(w_ref[...], g, m_ref[...], v_ref[...])

    blk = pl.BlockSpec((None, tr, cols), lambda i: (0, i, 0))
    shape = jax.ShapeDtypeStruct((1, rows, cols), F32)
    return pl.pallas_call(
        body, grid=(rows // tr,), in_specs=[blk, pl.BlockSpec((tr, cols), lambda i: (i, 0)), blk, blk],
        out_specs=[blk] * 4, out_shape=[shape] * 4, compiler_params=_params(("parallel",)), name=name)(w, g, m, v)


def _small_reduce(name, parts, width):
    def body(*refs):
        out = refs[-1]
        out[...] = jnp.zeros_like(out)
        for k, r in enumerate(refs[:-1]):
            out[pl.ds(k, 1), :] = jnp.sum(r[...], axis=0, keepdims=True)

    vmem = pl.BlockSpec(memory_space=pltpu.VMEM)
    return pl.pallas_call(body, in_specs=[vmem] * len(parts), out_specs=vmem,
                          out_shape=jax.ShapeDtypeStruct((8, width), F32), name=name)(*parts)


def _small_adamw(name, gathered, w, m, v, loss_row, loss_scale):
    def body(gt_ref, w_ref, m_ref, v_ref, g_out, d_out, m_out, v_out, loss_out):
        g = gt_ref[0]
        for k in range(1, N_DEV):
            g = g + gt_ref[k]
        g_out[...] = g
        row = lax.broadcasted_iota(jnp.int32, g.shape, 0)
        loss_out[...] = jnp.sum(jnp.where(row == loss_row, g, 0.0), keepdims=True) * loss_scale
        d_out[...], m_out[...], v_out[...] = _adamw_math(w_ref[...], g, m_ref[...], v_ref[...])

    vmem = pl.BlockSpec(memory_space=pltpu.VMEM)
    shape = jax.ShapeDtypeStruct(w.shape, F32)
    return pl.pallas_call(body, in_specs=[vmem] * 4, out_specs=[vmem] * 5,
                          out_shape=[shape] * 4 + [jax.ShapeDtypeStruct((1, 1), F32)], name=name)(gathered, w, m, v)


def _full_from_pieces(name, pieces):
    _, rows, cols = pieces.shape
    if name in ROW_SHARDED:
        return pieces.reshape(N_CHIPS * rows, cols)
    return pieces.transpose(1, 0, 2).reshape(rows, N_CHIPS * cols)


def _column_range(segments, lo, hi):
    out, start = [], 0
    for t in segments:
        a, b = max(lo, start), min(hi, start + t.shape[1])
        if a < b:
            out.append(t[:, a - start:b - start])
        start += t.shape[1]
    return out


def _repack_w_in(pieces, hd, d):
    hh = hd // HEAD_DIM
    segs = [pieces[p] for p in range(N_CHIPS)]
    total = N_CHIPS * pieces.shape[2]
    main = jnp.concatenate(_column_range(segs, 0, 6 * hd) + _column_range(segs, 6 * hd + hh, total), axis=1)
    f = jnp.concatenate(_column_range(segs, 6 * hd, 6 * hd + hh), axis=1)
    return main, jnp.pad(f, ((0, 0), (0, LANES - hh)))


def _dw_in_pieces(dw, dw_f, hd, d):
    hh = hd // HEAD_DIM
    segs = [dw[:, :6 * hd], dw_f[:, :hh], dw[:, 6 * hd:]]
    cs = (6 * hd + hh + 2 * d) // N_CHIPS
    return jnp.stack([jnp.concatenate(_column_range(segs, p * cs, (p + 1) * cs), axis=1) for p in range(N_CHIPS)])


def _small_pack(vals, width):
    rows = []
    for name in SMALL:
        t = vals[name].reshape(1, -1)
        rows.append(jnp.pad(t, ((0, 0), (0, width - t.shape[1]))))
    rows.append(jnp.zeros((8 - len(SMALL), width), F32))
    return jnp.concatenate(rows, axis=0)


def kernel(x, ffn1_norm, ffn1_w_gate, ffn1_w_up, ffn1_w_down, mix_norm, w_in, b_forget, b_gate_dil, b_gate_fox, w_proj_dil, w_proj_fox, w_out, ffn2_norm, ffn2_w_gate, ffn2_w_up, ffn2_w_down, final_norm, loss_target, m_ffn1_norm, m_ffn1_w_gate, m_ffn1_w_up, m_ffn1_w_down, m_mix_norm, m_w_in, m_b_forget, m_b_gate_dil, m_b_gate_fox, m_w_proj_dil, m_w_proj_fox, m_w_out, m_ffn2_norm, m_ffn2_w_gate, m_ffn2_w_up, m_ffn2_w_down, m_final_norm, v_ffn1_norm, v_ffn1_w_gate, v_ffn1_w_up, v_ffn1_w_down, v_mix_norm, v_w_in, v_b_forget, v_b_gate_dil, v_b_gate_fox, v_w_proj_dil, v_w_proj_fox, v_w_out, v_ffn2_norm, v_ffn2_w_gate, v_ffn2_w_up, v_ffn2_w_down, v_final_norm):
    given = dict(locals())
    wts = {n: given[n] for n in WEIGHTS}
    mom_m = {n: given["m_" + n] for n in WEIGHTS}
    mom_v = {n: given["v_" + n] for n in WEIGHTS}
    d = x.shape[2]

    shards = {n: wts[n][0].astype(BF) for n in SHARDED}
    small = {n: wts[n] for n in SMALL}
    grad_x, grads, partials = _device_step(x[0], loss_target[0], shards, small)

    out_g, out_d, out_m, out_v = {}, {}, {}, {}
    for n in SHARDED:
        out_g[n], out_d[n], out_m[n], out_v[n] = _adamw("adamw_" + n, wts[n], grads[n], mom_m[n], mom_v[n])

    width = d
    part_rows = []
    for n in SMALL:
        t = partials[n]
        part_rows.append(jnp.pad(t, ((0, 0), (0, width - t.shape[1]))))
    part_rows.append(partials["sq"])
    local_small = _small_reduce("small_reduce", part_rows, width)
    gathered_small = _gather_all("small_gather", local_small)
    sg, sd_, sm, sv, loss = _small_adamw("small_adamw", gathered_small, _small_pack(wts, width),
                                         _small_pack(mom_m, width), _small_pack(mom_v, width), len(SMALL), 0.5 / d)
    for k, n in enumerate(SMALL):
        shp = wts[n].shape
        take = lambda t: t[k, :shp[-1]].reshape(shp)
        out_g[n], out_d[n], out_m[n], out_v[n] = take(sg), take(sd_), take(sm), take(sv)
    return (loss[0, 0], grad_x[None], *[out_g[n] for n in WEIGHTS], *[out_d[n] for n in WEIGHTS],
            *[out_m[n] for n in WEIGHTS], *[out_v[n] for n in WEIGHTS])
```

```python
import functools
import math

import numpy as np
import jax
import jax.numpy as jnp
from jax import lax
from jax.experimental import pallas as pl
from jax.experimental.pallas import tpu as pltpu

HEAD_DIM = 128
ROPE_DIM = HEAD_DIM // 4
ROPE_THETA = 500000.0
DIL_PATTERNS = ((128, 1), (512, 4), (2048, 16))
MAX_WINDOW = 2048
NORM_EPS = 1e-6
ADAM_LR = 0.001
ADAM_B1 = 0.9
ADAM_B2 = 0.999
ADAM_EPS = 1e-08
ADAM_WD = 0.01
ADAM_STEP = 10

BF = jnp.bfloat16
F32 = jnp.float32
NEG = -1e30
LANES = 128
ATT_BLOCK = 512
PREP_COLS = 512
ATT_HEADS = 2
VMEM_LIMIT = 56 * 1024 * 1024
MM_VMEM_BUDGET = 40 * 1024 * 1024
N_CHIPS = 4
N_DEV = 8
MESH = pl.DeviceIdType.MESH

SHARDED = ("ffn1_w_gate", "ffn1_w_up", "ffn1_w_down", "w_in", "w_proj_dil", "w_proj_fox", "w_out",
           "ffn2_w_gate", "ffn2_w_up", "ffn2_w_down")
ROW_SHARDED = ("ffn1_w_down", "w_out", "ffn2_w_down")
SMALL = ("ffn1_norm", "mix_norm", "b_forget", "b_gate_dil", "b_gate_fox", "ffn2_norm", "final_norm")
WEIGHTS = ("ffn1_norm", "ffn1_w_gate", "ffn1_w_up", "ffn1_w_down", "mix_norm", "w_in", "b_forget",
           "b_gate_dil", "b_gate_fox", "w_proj_dil", "w_proj_fox", "w_out", "ffn2_norm", "ffn2_w_gate",
           "ffn2_w_up", "ffn2_w_down", "final_norm")


def _pick(n, target, align):
    best = None
    for d in range(align, min(n, target) + 1, align):
        if n % d == 0:
            best = d
    return n if best is None else best


def _params(sem=None):
    return pltpu.CompilerParams(dimension_semantics=sem, vmem_limit_bytes=VMEM_LIMIT)


_DIMS = {"nn": (((1,), (0,)), ((), ())), "nt": (((1,), (1,)), ((), ())), "tn": (((0,), (0,)), ((), ()))}


class _SemsFrom:
    def __init__(self, sems, first):
        self.sems, self.first = sems, first

    @property
    def at(self):
        return self

    def __getitem__(self, k):
        return self.sems.at[self.first + k]


class _Side:
    def __init__(self, inputs, out_shapes, n_sems, start, finish, aliases=None):
        self.inputs, self.out_shapes, self.n_sems = list(inputs), list(out_shapes), n_sems
        self.start, self.finish, self.aliases = start, finish, aliases or {}

    def scratch(self):
        return [pltpu.SemaphoreType.DMA((self.n_sems,)), pltpu.SemaphoreType.DMA((self.n_sems,))]

    def beside(self, other):
        n_in, n_out, n_sems = len(self.inputs), len(self.out_shapes), self.n_sems

        def part(fn_a, fn_b):
            def run(ins, outs, send_sems, recv_sems):
                fn_a(ins[:n_in], outs[:n_out], send_sems, recv_sems)
                fn_b(ins[n_in:], outs[n_out:], _SemsFrom(send_sems, n_sems), _SemsFrom(recv_sems, n_sems))
            return run

        assert not self.aliases and not other.aliases
        return _Side(self.inputs + other.inputs, self.out_shapes + other.out_shapes, n_sems + other.n_sems,
                     part(self.start, other.start), part(self.finish, other.finish))

    def call(self, name):
        n_in, n_out = len(self.inputs), len(self.out_shapes)

        def body(*refs):
            ins, outs, sems = refs[:n_in], refs[n_in:n_in + n_out], refs[n_in + n_out:]
            self.start(ins, outs, *sems)
            self.finish(ins, outs, *sems)

        return pl.pallas_call(body, in_specs=[ANY_SPEC] * n_in, out_specs=[ANY_SPEC] * n_out, out_shape=self.out_shapes,
                              input_output_aliases=self.aliases, scratch_shapes=self.scratch(), name=name)(*self.inputs)


def _mm(name, pairs, mode, m, n, k, *, epilogue, out_dtypes, extras=(), rows=(), n_colsum=0,
        sum_pairs=False, tm=1024, tn=1152, tk=2048, piece_layout=False, side=None, b_off=0, keep_tn=False):
    m_align = LANES if mode == "tn" else 8
    tm = _pick(m, tm, m_align)
    tn = n // N_CHIPS if piece_layout else _pick(n, tn, LANES)
    tk = _pick(k, tk, LANES)
    n_acc = 1 if sum_pairs else len(pairs)
    lhs = []
    for a, _ in pairs:
        if not any(a is t for t in lhs):
            lhs.append(a)
    lhs_of = [next(t for t in range(len(lhs)) if lhs[t] is a) for a, _ in pairs]
    n_mm = len(lhs) + len(pairs)
    n_in = n_mm + len(extras) + len(rows)
    n_out = len(out_dtypes) + n_colsum

    def vmem_bytes(tm_, tn_, tk_):
        tiles = sum(tm_ * tk_ * a.dtype.itemsize for a in lhs) + sum(tn_ * tk_ * b.dtype.itemsize for _, b in pairs)
        tiles += sum(tm_ * tn_ * arr.dtype.itemsize for arr, _ in extras)
        tiles += sum(tm_ * tn_ * jnp.dtype(dt).itemsize for dt in out_dtypes)
        return 2 * tiles + (n_acc + len(extras) + len(out_dtypes)) * tm_ * tn_ * 4

    while vmem_bytes(tm, tn, tk) > MM_VMEM_BUDGET:
        if tn > 512 and not piece_layout and not keep_tn:
            tn = _pick(n, tn - LANES, LANES)
        elif tm > 512:
            tm = _pick(m, tm - m_align, m_align)
        elif tk > 512:
            tk = _pick(k, tk - LANES, LANES)
        elif tm > 256:
            tm = _pick(m, tm - m_align, m_align)
        else:
            break
    nk = k // tk

    n_side_in = len(side.inputs) if side else 0
    n_side_out = len(side.out_shapes) if side else 0
    grid = (m // tm, n // tn, nk)

    def body(*refs):
        ins, refs = refs[:n_in], refs[n_in:]
        side_ins, refs = refs[:n_side_in], refs[n_side_in:]
        outs, refs = refs[:n_out], refs[n_out:]
        side_outs, refs = refs[:n_side_out], refs[n_side_out:]
        accs, side_sems = refs[:n_acc], refs[n_acc:]
        kk = pl.program_id(2)
        if side:
            at = [pl.program_id(t) for t in range(3)]

            @pl.when(jnp.logical_and(jnp.logical_and(at[0] == 0, at[1] == 0), at[2] == 0))
            def _():
                side.start(side_ins, side_outs, *side_sems)

        @pl.when(kk == 0)
        def _():
            for acc in accs:
                acc[...] = jnp.zeros_like(acc)

        a_tiles = [r[...].astype(BF) for r in ins[:len(lhs)]]
        for p in range(len(pairs)):
            b = ins[len(lhs) + p][...].astype(BF)
            accs[0 if sum_pairs else p][...] += lax.dot_general(a_tiles[lhs_of[p]], b, _DIMS[mode],
                                                                preferred_element_type=F32)

        @pl.when(kk == nk - 1)
        def _():
            ex = [r[...] for r in ins[n_mm:n_mm + len(extras)]]
            rw = [r[...] for r in ins[n_mm + len(extras):]]
            res = epilogue([acc[...] for acc in accs], ex, rw)
            for o, r in zip(outs, res):
                o[...] = r.astype(o.dtype)

        if side:
            @pl.when(jnp.logical_and(jnp.logical_and(at[0] == grid[0] - 1, at[1] == grid[1] - 1), at[2] == nk - 1))
            def _():
                side.finish(side_ins, side_outs, *side_sems)

    in_specs, args = [], []
    for a in lhs:
        if mode == "tn":
            in_specs.append(pl.BlockSpec((tk, tm), lambda i, j, kk: (kk, i)))
        else:
            in_specs.append(pl.BlockSpec((tm, tk), lambda i, j, kk: (i, kk)))
        args.append(a)
    assert b_off % tn == 0 and (b_off == 0 or mode == "nn")
    for _, b in pairs:
        if mode == "nt":
            in_specs.append(pl.BlockSpec((tn, tk), lambda i, j, kk: (j, kk)))
        else:
            in_specs.append(pl.BlockSpec((tk, tn), functools.partial(lambda i, j, kk, o: (kk, j + o), o=b_off // tn)))
        args.append(b)
    for arr, off in extras:
        if off is None:
            in_specs.append(pl.BlockSpec((tm, tn), lambda i, j, kk: (i, 0)))
        else:
            assert off % tn == 0
            in_specs.append(pl.BlockSpec((tm, tn), functools.partial(lambda i, j, kk, o: (i, j + o), o=off // tn)))
        args.append(arr)
    for arr in rows:
        in_specs.append(pl.BlockSpec((1, tn), lambda i, j, kk: (0, j)))
        args.append(arr)
    if piece_layout:
        out_specs = [pl.BlockSpec((None, tm, tn), lambda i, j, kk: (j, i, 0)) for _ in out_dtypes]
        out_shape = [jax.ShapeDtypeStruct((n // tn, m, tn), d) for d in out_dtypes]
    else:
        out_specs = [pl.BlockSpec((tm, tn), lambda i, j, kk: (i, j)) for _ in out_dtypes]
        out_shape = [jax.ShapeDtypeStruct((m, n), d) for d in out_dtypes]
    for _ in range(n_colsum):
        out_specs.append(pl.BlockSpec((None, 1, tn), lambda i, j, kk: (i, 0, j)))
        out_shape.append(jax.ShapeDtypeStruct((m // tm, 1, n), F32))
    scratch = [pltpu.VMEM((tm, tn), F32) for _ in range(n_acc)]
    if side is None:
        return pl.pallas_call(
            body, grid=grid, in_specs=in_specs, out_specs=out_specs, out_shape=out_shape, scratch_shapes=scratch,
            compiler_params=_params(("parallel", "parallel", "arbitrary")), name=name)(*args)
    res = pl.pallas_call(
        body, grid=grid, in_specs=in_specs + [ANY_SPEC] * n_side_in, out_specs=out_specs + [ANY_SPEC] * n_side_out,
        out_shape=out_shape + side.out_shapes, scratch_shapes=scratch + side.scratch(),
        input_output_aliases={n_in + t: n_out + o for t, o in side.aliases.items()},
        compiler_params=_params(("arbitrary", "arbitrary", "arbitrary")), name=name)(*args, *side.inputs)
    return res[:n_out], res[n_out:]


def _col_pieces(full):
    rows, cols = full.shape
    return full.reshape(rows, N_CHIPS, cols // N_CHIPS).transpose(1, 0, 2)


def _dw_col_pieces(name, a, b, m, n, k, side=None):
    ident = lambda accs, ex, rw: (accs[0],)
    aligned = (n // N_CHIPS) % LANES == 0
    res = _mm(name, [(a, b)], "tn", m, n, k, epilogue=ident, out_dtypes=(F32,), tm=512 if aligned else 1024,
              piece_layout=aligned, side=side)
    (out,), side_res = res if side else (res, None)
    out = out if aligned else _col_pieces(out)
    return (out, side_res) if side else out


def _hosted(sides, got, key, call):
    make = sides.get(key) if sides else None
    if make is None:
        return call(None)
    outs, got[key] = call(make(got))
    return outs


def _sigmoid(z):
    return 0.5 * jnp.tanh(0.5 * z) + 0.5


def _row_tile(s):
    return _pick(s, 256, 8)


def _fold8(t):
    r, d = t.shape
    return jnp.sum(t.reshape(r // 8, 8, d), axis=0)


def _rms_fwd(name, x, g):
    s, d = x.shape
    tr = _row_tile(s)

    def body(x_ref, g_ref, h_ref):
        xf = x_ref[...]
        y = xf * lax.rsqrt(jnp.mean(xf * xf, axis=-1, keepdims=True) + NORM_EPS)
        h_ref[...] = (y * g_ref[...]).astype(BF)

    return pl.pallas_call(
        body, grid=(s // tr,),
        in_specs=[pl.BlockSpec((tr, d), lambda i: (i, 0)), pl.BlockSpec((1, d), lambda i: (0, 0))],
        out_specs=pl.BlockSpec((tr, d), lambda i: (i, 0)), out_shape=jax.ShapeDtypeStruct((s, d), BF),
        compiler_params=_params(("parallel",)), name=name)(x, g)


def _rms_bwd(name, x, g, dh, dres):
    s, d = x.shape
    tr = _row_tile(s)

    def body(x_ref, g_ref, dh_ref, dres_ref, dx_ref, dxb_ref, dg_ref):
        @pl.when(pl.program_id(0) == 0)
        def _():
            dg_ref[...] = jnp.zeros_like(dg_ref)

        xf = x_ref[...]
        rstd = lax.rsqrt(jnp.mean(xf * xf, axis=-1, keepdims=True) + NORM_EPS)
        xhat = xf * rstd
        dhf = dh_ref[...]
        dg_ref[...] += _fold8(dhf * xhat)
        dxh = dhf * g_ref[...]
        dx = dres_ref[...] + rstd * (dxh - xhat * jnp.mean(dxh * xhat, axis=-1, keepdims=True))
        dx_ref[...] = dx
        dxb_ref[...] = dx.astype(BF)

    blk = pl.BlockSpec((tr, d), lambda i: (i, 0))
    return pl.pallas_call(
        body, grid=(s // tr,),
        in_specs=[blk, pl.BlockSpec((1, d), lambda i: (0, 0)), blk, blk],
        out_specs=[blk, blk, pl.BlockSpec((8, d), lambda i: (0, 0))],
        out_shape=[jax.ShapeDtypeStruct((s, d), F32), jax.ShapeDtypeStruct((s, d), BF),
                   jax.ShapeDtypeStruct((8, d), F32)],
        compiler_params=_params(("arbitrary",)), name=name)(x, g, dh, dres)


def _final(name, x, g, tgt):
    s, d = x.shape
    tr = _row_tile(s)

    def body(x_ref, g_ref, t_ref, dx_ref, dxb_ref, dg_ref, sq_ref):
        @pl.when(pl.program_id(0) == 0)
        def _():
            dg_ref[...] = jnp.zeros_like(dg_ref)
            sq_ref[...] = jnp.zeros_like(sq_ref)

        xf = x_ref[...]
        rstd = lax.rsqrt(jnp.mean(xf * xf, axis=-1, keepdims=True) + NORM_EPS)
        xhat = xf * rstd
        gf = g_ref[...]
        err = xhat * gf - t_ref[...]
        sq_ref[...] += _fold8(err * err)
        dy = err * (1.0 / d)
        dg_ref[...] += _fold8(dy * xhat)
        dxh = dy * gf
        dx = rstd * (dxh - xhat * jnp.mean(dxh * xhat, axis=-1, keepdims=True))
        dx_ref[...] = dx
        dxb_ref[...] = dx.astype(BF)

    blk = pl.BlockSpec((tr, d), lambda i: (i, 0))
    acc = pl.BlockSpec((8, d), lambda i: (0, 0))
    return pl.pallas_call(
        body, grid=(s // tr,), in_specs=[blk, pl.BlockSpec((1, d), lambda i: (0, 0)), blk],
        out_specs=[blk, blk, acc, acc],
        out_shape=[jax.ShapeDtypeStruct((s, d), F32), jax.ShapeDtypeStruct((s, d), BF),
                   jax.ShapeDtypeStruct((8, d), F32), jax.ShapeDtypeStruct((8, d), F32)],
        compiler_params=_params(("arbitrary",)), name=name)(x, g, tgt)


def _ffn_fwd(tag, x, g, w_gate, w_up, get_w_down, sides=None):
    s, d = x.shape
    f = w_gate.shape[1]
    got = {}
    h = _rms_fwd(tag + "_rms", x, g)

    def up_epi(accs, ex, rw):
        a, b = accs
        return a, b, a * _sigmoid(a) * b

    a, b, act = _hosted(sides, got, "up", lambda side: _mm(
        tag + "_up", [(h, w_gate), (h, w_up)], "nn", s, f, d, epilogue=up_epi, out_dtypes=(BF, BF, BF), side=side))

    def down_epi(accs, ex, rw):
        return (ex[0] + 0.5 * accs[0],)

    w_down = get_w_down(got)
    (y,) = _hosted(sides, got, "down", lambda side: _mm(
        tag + "_down", [(act, w_down)], "nn", s, d, f, epilogue=down_epi, out_dtypes=(F32,), extras=[(x, 0)],
        side=side))
    return y, (h, a, b, act), got


def _ffn_bwd(tag, x, g, w_gate, w_up, w_down, saved, dy, dy_bf, sides=None):
    s, d = x.shape
    f = w_gate.shape[1]
    h, a, b, act = saved
    got = {}

    def act_epi(accs, ex, rw):
        dact = 0.5 * accs[0]
        av, bv = ex[0].astype(F32), ex[1].astype(F32)
        sg = _sigmoid(av)
        return dact * bv * (sg * (1.0 + av * (1.0 - sg))), dact * (av * sg)

    da, db = _hosted(sides, got, "dact", lambda side: _mm(
        tag + "_dact", [(dy_bf, w_down)], "nt", s, f, d, epilogue=act_epi, out_dtypes=(BF, BF),
        extras=[(a, 0), (b, 0)], side=side))
    ident = lambda accs, ex, rw: (accs[0],)
    (dw_down,) = _hosted(sides, got, "dwd", lambda side: _mm(
        tag + "_dwd", [(act, dy_bf)], "tn", f, d, s, epilogue=lambda accs, ex, rw: (0.5 * accs[0],),
        out_dtypes=(F32,), side=side))
    got["dw_down"] = dw_down = dw_down.reshape(N_CHIPS, f // N_CHIPS, d)
    got["dw_gate"] = dw_gate = _hosted(sides, got, "dwg", lambda side: _dw_col_pieces(
        tag + "_dwg", h, da, d, f, s, side=side))
    got["dw_up"] = dw_up = _hosted(sides, got, "dwu", lambda side: _dw_col_pieces(
        tag + "_dwu", h, db, d, f, s, side=side))
    (dh,) = _hosted(sides, got, "dh", lambda side: _mm(
        tag + "_dh", [(da, w_gate), (db, w_up)], "nt", s, d, f, epilogue=ident, out_dtypes=(F32,), sum_pairs=True,
        side=side))
    dx, dx_bf, dg = _rms_bwd(tag + "_rmsb", x, g, dh, dy)
    return dx, dx_bf, dg, dw_gate, dw_up, dw_down, got


def _rope_tables(s):
    half = ROPE_DIM // 2
    pos = jnp.arange(s, dtype=F32)
    inv_freq = ROPE_THETA ** (-jnp.arange(0, ROPE_DIM, 2, dtype=F32) / ROPE_DIM)
    ang = pos[:, None] * inv_freq[None, :]
    cos, sin = jnp.cos(ang), jnp.sin(ang)
    rest = HEAD_DIM - ROPE_DIM
    cos_t = jnp.concatenate([cos, cos, jnp.ones((s, rest), F32)], axis=-1)
    sin_t = jnp.concatenate([-sin, sin, jnp.zeros((s, rest), F32)], axis=-1)
    return cos_t, sin_t


def _swap_halves(t):
    lane = lax.broadcasted_iota(jnp.int32, t.shape, 1) & (HEAD_DIM - 1)
    half = ROPE_DIM // 2
    return jnp.where(lane < half, pltpu.roll(t, t.shape[1] - half, 1), pltpu.roll(t, half, 1))


def _dil_bias(blk):
    n_delta = MAX_WINDOW // blk + 1
    delta = jnp.arange(n_delta, dtype=jnp.int32)[:, None, None]
    r = jnp.arange(blk, dtype=jnp.int32)[None, None, :]
    c = jnp.arange(blk, dtype=jnp.int32)[None, :, None]
    o = delta * blk + r - c
    mult = jnp.zeros(o.shape, F32)
    for w, dd in DIL_PATTERNS:
        mult = mult + ((o >= 0) & (o <= w) & (o % dd == 0)).astype(F32)
    return jnp.where(mult > 0, jnp.log(jnp.maximum(mult, 1.0)), NEG)


def _att_block(s):
    return _pick(s, ATT_BLOCK, LANES)


def _fox_aug(name, c_pad, qkv, q_off, k_off, n_heads):
    s = c_pad.shape[0]
    tr = _pick(s, 1024, 16)

    def body(c_ref, q_ref, k_ref, qc_ref, kc_ref):
        h = pl.program_id(1)
        lane = lax.broadcasted_iota(jnp.int32, (tr, LANES), 1)
        ch = jnp.sum(jnp.where(lane == h, c_ref[...], 0.0), axis=1, keepdims=True)
        hi, mid, lo = (t.astype(F32) for t in _split3(ch))
        zero = jnp.zeros((tr, LANES), F32)
        is_hi = jnp.logical_or(lane == 0, lane == 3)
        is_mid = jnp.logical_or(lane == 1, lane == 4)
        parts = jnp.where(is_hi, hi, jnp.where(is_mid, mid, lo))
        qc_ref[:, :HEAD_DIM] = q_ref[...]
        kc_ref[:, :HEAD_DIM] = k_ref[...]
        qc_ref[:, HEAD_DIM:] = jnp.where(lane < 3, 1.0, jnp.where(lane < 6, parts, zero)).astype(BF)
        kc_ref[:, HEAD_DIM:] = jnp.where(lane < 3, -parts, jnp.where(lane < 6, 1.0, zero)).astype(BF)

    head = lambda o: pl.BlockSpec((tr, HEAD_DIM), functools.partial(lambda i, h, o: (i, o + h), o=o))
    spec = pl.BlockSpec((tr, 2 * HEAD_DIM), lambda i, h: (i, h))
    shape = jax.ShapeDtypeStruct((s, n_heads * 2 * HEAD_DIM), BF)
    return pl.pallas_call(
        body, grid=(s // tr, n_heads),
        in_specs=[pl.BlockSpec((tr, LANES), lambda i, h: (i, 0)), head(q_off), head(k_off)],
        out_specs=[spec, spec], out_shape=[shape, shape],
        compiler_params=_params(("parallel", "arbitrary")), name=name)(c_pad, qkv, qkv)


def _causal_mask(st):
    kpos = lax.broadcasted_iota(jnp.int32, st.shape, 0)
    qpos = lax.broadcasted_iota(jnp.int32, st.shape, 1)
    return jnp.where(kpos <= qpos, st, NEG)


def _flash_fwd(name, q_src, k_src, v_src, n_heads, *, fox, tab_t=None):
    (q_arr, q_off, qw), (k_arr, k_off, kw), (v_arr, v_off, _) = q_src, k_src, v_src
    s = q_arr.shape[0]
    blk = _att_block(s)
    nq = s // blk
    n_delta = MAX_WINDOW // blk + 1
    grp = ATT_HEADS
    assert qw == kw and n_heads % grp == 0 and q_off % grp == 0 and k_off % grp == 0 and v_off % grp == 0
    wide = grp * HEAD_DIM

    def body(*refs):
        if fox:
            q_ref, k_ref, v_ref, o_ref, lse_ref, acc, m_s, l_s = refs
        else:
            q_ref, k_ref, v_ref, tab_ref, o_ref, lse_ref, acc, m_s, l_s = refs
        i = pl.program_id(1)
        acc[...] = jnp.zeros_like(acc)
        m_s[...] = jnp.full_like(m_s, NEG)
        l_s[...] = jnp.zeros_like(l_s)

        def step(j, diagonal):
            ks = pl.ds(pl.multiple_of(j * blk, blk), blk)
            for g in range(grp):
                cols = slice(g * HEAD_DIM, (g + 1) * HEAD_DIM)
                qk_cols = slice(g * qw, (g + 1) * qw)
                st = lax.dot_general(k_ref[ks, qk_cols], q_ref[:, qk_cols], _DIMS["nt"], preferred_element_type=F32)
                if fox:
                    if diagonal:
                        st = _causal_mask(st)
                else:
                    st = st + tab_ref[i - j]
                m_prev = m_s[g]
                m_new = jnp.maximum(m_prev, jnp.max(st, axis=0, keepdims=True))
                alpha = jnp.exp(m_prev - m_new)
                p = jnp.exp(st - m_new)
                l_s[g] = alpha * l_s[g] + jnp.sum(p, axis=0, keepdims=True)
                acc[g] = alpha * acc[g] + lax.dot_general(v_ref[ks, cols], p.astype(BF), _DIMS["tn"],
                                                          preferred_element_type=F32)
                m_s[g] = m_new

        def loop_step(j, carry):
            step(j, False)
            return carry

        if fox:
            lax.fori_loop(0, i, loop_step, 0)
            step(i, True)
        else:
            lax.fori_loop(jnp.maximum(i - (n_delta - 1), 0), i + 1, loop_step, 0)
        for g in range(grp):
            o_ref[:, g * HEAD_DIM:(g + 1) * HEAD_DIM] = (acc[g] / l_s[g]).T.astype(o_ref.dtype)
            lse_ref[g] = m_s[g] + jnp.log(l_s[g])

    off = lambda o: functools.partial(lambda h, i, o: (0, o + h), o=o // grp)
    in_specs = [pl.BlockSpec((blk, grp * qw), functools.partial(lambda h, i, o: (i, o + h), o=q_off // grp)),
                pl.BlockSpec((s, grp * kw), off(k_off)), pl.BlockSpec((s, wide), off(v_off))]
    args = [q_arr, k_arr, v_arr]
    if not fox:
        in_specs.append(pl.BlockSpec((n_delta, blk, blk), lambda h, i: (0, 0, 0)))
        args.append(tab_t)
    return pl.pallas_call(
        body, grid=(n_heads // grp, nq), in_specs=in_specs,
        out_specs=[pl.BlockSpec((blk, wide), lambda h, i: (i, h)),
                   pl.BlockSpec((grp, None, 1, blk), lambda h, i: (h, i, 0, 0))],
        out_shape=[jax.ShapeDtypeStruct((s, n_heads * HEAD_DIM), BF),
                   jax.ShapeDtypeStruct((n_heads, nq, 1, blk), F32)],
        scratch_shapes=[pltpu.VMEM((grp, HEAD_DIM, blk), F32), pltpu.VMEM((grp, 1, blk), F32),
                        pltpu.VMEM((grp, 1, blk), F32)],
        compiler_params=_params(("parallel", "parallel")), name=name)(*args)


def _att_delta(name, do, o, n_heads):
    s = do.shape[0]
    blk = _pick(s, 1024, 16)

    def body(do_ref, o_ref, d_ref):
        d_ref[...] = jnp.sum(do_ref[...].astype(F32) * o_ref[...].astype(F32), axis=-1, keepdims=True)

    spec = pl.BlockSpec((blk, HEAD_DIM), lambda h, i: (i, h))
    return pl.pallas_call(
        body, grid=(n_heads, s // blk), in_specs=[spec, spec],
        out_specs=pl.BlockSpec((None, blk, 1), lambda h, i: (h, i, 0)),
        out_shape=jax.ShapeDtypeStruct((n_heads, s, 1), F32),
        compiler_params=_params(("parallel", "parallel")), name=name)(do, o)


def _flash_bwd(name, q_src, k_src, v_src, do, lse_row, delta_row, n_heads, *, fox, tab_t=None):
    (q_arr, q_off, qw), (k_arr, k_off, kw), (v_arr, v_off, _) = q_src, k_src, v_src
    s = q_arr.shape[0]
    blk = _att_block(s)
    nq = s // blk
    n_delta = MAX_WINDOW // blk + 1
    grp = ATT_HEADS
    assert qw == kw and n_heads % grp == 0 and q_off % grp == 0 and k_off % grp == 0 and v_off % grp == 0
    wide = grp * HEAD_DIM

    def body(*refs):
        if fox:
            (q_ref, do_ref, k_ref, v_ref, lse_ref, dl_ref,
             dq_ref, dk_ref, dv_ref, dc_ref, dcq_ref, dk_acc, dv_acc, dc_acc) = refs
        else:
            (q_ref, do_ref, k_ref, v_ref, lse_ref, dl_ref, tab_ref,
             dq_ref, dk_ref, dv_ref, dk_acc, dv_acc) = refs
        j = pl.program_id(1)

        @pl.when(j == 0)
        def _():
            dq_ref[...] = jnp.zeros_like(dq_ref)
            if fox:
                dcq_ref[...] = jnp.zeros_like(dcq_ref)

        dk_acc[...] = jnp.zeros_like(dk_acc)
        dv_acc[...] = jnp.zeros_like(dv_acc)
        if fox:
            dc_acc[...] = jnp.zeros_like(dc_acc)

        def step(i, diagonal):
            qs = pl.ds(pl.multiple_of(i * blk, blk), blk)
            for g in range(grp):
                cols = slice(g * HEAD_DIM, (g + 1) * HEAD_DIM)
                qk_cols = slice(g * qw, (g + 1) * qw)
                plain = slice(g * qw, g * qw + HEAD_DIM)
                kb, vb = k_ref[:, plain], v_ref[:, cols]
                qb, dob = q_ref[qs, plain], do_ref[qs, cols]
                st = lax.dot_general(k_ref[:, qk_cols], q_ref[qs, qk_cols], _DIMS["nt"], preferred_element_type=F32)
                if fox:
                    if diagonal:
                        st = _causal_mask(st)
                else:
                    st = st + tab_ref[i - j]
                pt = jnp.exp(st - lse_ref[g, i])
                dv_acc[:, cols] += jnp.dot(pt.astype(BF), dob, preferred_element_type=F32)
                dpt = lax.dot_general(vb, dob, _DIMS["nt"], preferred_element_type=F32)
                dst = pt * (dpt - dl_ref[g, i])
                dsb = dst.astype(BF)
                dk_acc[:, cols] += jnp.dot(dsb, qb, preferred_element_type=F32)
                dq_ref[qs, cols] += lax.dot_general(dsb, kb, _DIMS["tn"], preferred_element_type=F32)
                if fox:
                    folded = dst[:, :LANES]
                    for part in range(1, blk // LANES):
                        folded = folded + dst[:, part * LANES:(part + 1) * LANES]
                    dc_acc[g] -= folded
                    dcq_ref[g, i] += jnp.sum(dst, axis=0, keepdims=True)

        def loop_step(i, carry):
            step(i, False)
            return carry

        if fox:
            step(j, True)
            lax.fori_loop(j + 1, nq, loop_step, 0)
        else:
            lax.fori_loop(j, jnp.minimum(nq, j + n_delta), loop_step, 0)
        dk_ref[...] = dk_acc[...]
        dv_ref[...] = dv_acc[...].astype(dv_ref.dtype)
        if fox:
            for g in range(grp):
                dc_ref[g] = jnp.sum(dc_acc[g], axis=-1, keepdims=True)

    full = lambda o, wd=wide: pl.BlockSpec((s, wd), functools.partial(lambda h, j, o: (0, o + h), o=o // grp))
    tile = lambda o, wd=wide: pl.BlockSpec((blk, wd), functools.partial(lambda h, j, o: (j, o + h), o=o // grp))
    per_q = pl.BlockSpec((grp, nq, 1, blk), lambda h, j: (h, 0, 0, 0))
    per_k = pl.BlockSpec((grp, blk, 1), lambda h, j: (h, j, 0))
    in_specs = [full(q_off, grp * qw), full(0), tile(k_off, grp * kw), tile(v_off), per_q, per_q]
    args = [q_arr, do, k_arr, v_arr, lse_row, delta_row]
    out_specs = [full(0), tile(0), tile(0)]
    hd = n_heads * HEAD_DIM
    out_shape = [jax.ShapeDtypeStruct((s, hd), F32), jax.ShapeDtypeStruct((s, hd), F32),
                 jax.ShapeDtypeStruct((s, hd), BF)]
    scratch = [pltpu.VMEM((blk, wide), F32), pltpu.VMEM((blk, wide), F32)]
    if fox:
        out_specs += [per_k, per_q]
        out_shape += [jax.ShapeDtypeStruct((n_heads, s, 1), F32), jax.ShapeDtypeStruct((n_heads, nq, 1, blk), F32)]
        scratch.append(pltpu.VMEM((grp, blk, LANES), F32))
    else:
        in_specs.append(pl.BlockSpec((n_delta, blk, blk), lambda h, j: (0, 0, 0)))
        args.append(tab_t)
    return pl.pallas_call(
        body, grid=(n_heads // grp, nq), in_specs=in_specs, out_specs=out_specs, out_shape=out_shape,
        scratch_shapes=scratch, compiler_params=_params(("parallel", "arbitrary")), name=name)(*args)


def _split3(t):
    hi = t.astype(BF)
    r1 = t - hi.astype(F32)
    mid = r1.astype(BF)
    lo = (r1 - mid.astype(F32)).astype(BF)
    return hi, mid, lo


def _tri_dot(tri, t):
    hi, mid, lo = _split3(t)
    return (jnp.dot(tri, hi, preferred_element_type=F32) + jnp.dot(tri, mid, preferred_element_type=F32)
            + jnp.dot(tri, lo, preferred_element_type=F32))


def _log_sigmoid(z):
    return jnp.minimum(z, 0.0) - jnp.log(1.0 + jnp.exp(-jnp.abs(z)))


def _forget_cumsum(name, proj, f_col, bias):
    s = proj.shape[0]
    blk = _att_block(s)

    def body(f_ref, b_ref, c_ref, carry):
        @pl.when(pl.program_id(0) == 0)
        def _():
            carry[...] = jnp.zeros_like(carry)

        lf = _log_sigmoid(f_ref[...] + b_ref[...])
        r = lax.broadcasted_iota(jnp.int32, (blk, blk), 0)
        c = lax.broadcasted_iota(jnp.int32, (blk, blk), 1)
        tri = (c <= r).astype(BF)
        c_ref[...] = _tri_dot(tri, lf) + carry[...]
        carry[...] = c_ref[pl.ds(blk - 1, 1), :]

    return pl.pallas_call(
        body, grid=(s // blk,),
        in_specs=[pl.BlockSpec((blk, LANES), lambda i: (i, f_col)), pl.BlockSpec((1, LANES), lambda i: (0, 0))],
        out_specs=pl.BlockSpec((blk, LANES), lambda i: (i, 0)), out_shape=jax.ShapeDtypeStruct((s, LANES), F32),
        scratch_shapes=[pltpu.VMEM((1, LANES), F32)],
        compiler_params=_params(("arbitrary",)), name=name)(proj, bias)


def _forget_bwd(name, proj, f_col, bias, dc):
    s = proj.shape[0]
    blk = _att_block(s)
    nb = s // blk

    def body(f_ref, b_ref, dc_ref, df_ref, db_ref, carry):
        @pl.when(pl.program_id(0) == 0)
        def _():
            carry[...] = jnp.zeros_like(carry)
            db_ref[...] = jnp.zeros_like(db_ref)

        r = lax.broadcasted_iota(jnp.int32, (blk, blk), 0)
        c = lax.broadcasted_iota(jnp.int32, (blk, blk), 1)
        tri = (c >= r).astype(BF)
        r = lax.broadcasted_iota(jnp.int32, (blk, LANES), 0)
        dlf = _tri_dot(tri, dc_ref[...]) + carry[...]
        carry[...] = jnp.sum(jnp.where(r == 0, dlf, 0.0), axis=0, keepdims=True)
        dz = dlf * _sigmoid(-(f_ref[...] + b_ref[...]))
        df_ref[...] = dz.astype(BF)
        db_ref[...] += _fold8(dz)

    rev = lambda i: (nb - 1 - i, 0)
    return pl.pallas_call(
        body, grid=(nb,),
        in_specs=[pl.BlockSpec((blk, LANES), lambda i: (nb - 1 - i, f_col)), pl.BlockSpec((1, LANES), lambda i: (0, 0)),
                  pl.BlockSpec((blk, LANES), rev)],
        out_specs=[pl.BlockSpec((blk, LANES), rev), pl.BlockSpec((8, LANES), lambda i: (0, 0))],
        out_shape=[jax.ShapeDtypeStruct((s, LANES), BF), jax.ShapeDtypeStruct((8, LANES), F32)],
        scratch_shapes=[pltpu.VMEM((1, LANES), F32)],
        compiler_params=_params(("arbitrary",)), name=name)(proj, bias, dc)


def _dproj_assemble(name, parts, gates, cos_t, sin_t, n_heads, d_model):
    s = cos_t.shape[0]
    tr = _pick(s, 1024, 16)
    scale = HEAD_DIM ** -0.5
    hd = n_heads * HEAD_DIM
    cw = _pick(math.gcd(hd, d_model), PREP_COLS, HEAD_DIM)
    widths = [hd] * 6 + [d_model] * 2
    starts = [sum(widths[:t]) // cw for t in range(len(widths) + 1)]
    cos_w, sin_w = jnp.tile(cos_t, (1, cw // HEAD_DIM)), jnp.tile(sin_t, (1, cw // HEAD_DIM))

    def body(*refs):
        p_refs, cos_ref, sin_ref, o_ref = refs[:8], refs[8], refs[9], refs[10]
        j = pl.program_id(1)
        for kind in range(8):
            @pl.when(jnp.logical_and(j >= starts[kind], j < starts[kind + 1]))
            def _(kind=kind):
                t = p_refs[kind][...]
                if kind in (0, 1, 3):
                    t = t.astype(F32)
                if kind in (0, 3):
                    t = t * scale
                if kind in (0, 1):
                    t = t * cos_ref[...] - _swap_halves(t) * sin_ref[...]
                o_ref[...] = t.astype(BF)

    def part_spec(kind):
        return pl.BlockSpec((tr, cw), functools.partial(
            lambda i, j, kind: (i, jnp.clip(j - starts[kind], 0, widths[kind] // cw - 1)), kind=kind))

    tab = pl.BlockSpec((tr, cw), lambda i, j: (i, 0))
    return pl.pallas_call(
        body, grid=(s // tr, starts[-1]), in_specs=[part_spec(kind) for kind in range(8)] + [tab, tab],
        out_specs=pl.BlockSpec((tr, cw), lambda i, j: (i, j)),
        out_shape=jax.ShapeDtypeStruct((s, sum(widths)), BF),
        compiler_params=_params(("parallel", "arbitrary")), name=name)(*parts, *gates, cos_w, sin_w)


FFN2 =("ffn2_w_gate", "ffn2_w_up", "ffn2_w_down")


def _device_step(x, tgt, shards, small):
    s, d = x.shape
    hd = shards["w_proj_dil"].shape[0]
    hh = hd // HEAD_DIM
    blk = _att_block(s)
    gate_off = 6 * hd
    f_col = 0
    n_proj = gate_off + 2 * d
    ident = lambda accs, ex, rw: (accs[0],)
    chip = 2 * lax.axis_index("x") + lax.axis_index("y")
    ids = jnp.stack([chip, lax.axis_index("c")]).astype(jnp.int32)
    w = {}

    def take_gathered(names, gathered):
        for n, t in zip(names, gathered):
            t = lax.dynamic_update_index_in_dim(t, shards[n], chip, 0)
            if n == "w_in":
                w["w_in"], w["w_in_f"] = _repack_w_in(t, hd, d)
            else:
                w[n] = _full_from_pieces(n, t)

    def chip_sums(names, pieces, from_sibling):
        return [_rs_add("rs_add_" + n, ids, g, o) for n, g, o in zip(names, pieces, from_sibling)]

    def core_halves(names, sums, from_chips):
        return [_rs_sum("rs_sum_" + n, ids, own, got) for n, own, got in zip(names, sums, from_chips)]

    def gather(names):
        return _gather_side([shards[n] for n in names])

    first, under_up = ("ffn1_w_gate", "ffn1_w_up"), ("ffn1_w_down", "w_in")
    under_down = ("w_proj_dil", "w_proj_fox", "w_out", "ffn2_w_gate")
    under_proj = ("ffn2_w_up", "ffn2_w_down")
    take_gathered(first, gather(first).call("gather_first"))

    def w_down_of_ffn1(got):
        take_gathered(under_up, got["up"])
        return w["ffn1_w_down"]

    x1, saved1, got = _ffn_fwd("ffn1", x, small["ffn1_norm"], w["ffn1_w_gate"], w["ffn1_w_up"], w_down_of_ffn1,
                               sides={"up": lambda got: gather(under_up), "down": lambda got: gather(under_down)})
    take_gathered(under_down, got["down"])

    hm = _rms_fwd("mix_rms", x1, small["mix_norm"])
    cos_t, sin_t = _rope_tables(s)
    head_tile = _pick(hd, 1024, HEAD_DIM)
    tiles_per_kind = hd // head_tile
    scale = HEAD_DIM ** -0.5

    def heads_epi(accs, ex, rw):
        kind = pl.program_id(1) // tiles_per_kind
        t = accs[0]
        r = t * ex[0] + _swap_halves(t) * ex[1]
        t = jnp.where(kind < 2, r, t)
        return (jnp.where(jnp.logical_or(kind == 0, kind == 3), t * scale, t),)

    wide = lambda tab: jnp.tile(tab, (1, head_tile // HEAD_DIM))
    (qkv,), gathered = _mm("proj_heads", [(hm, w["w_in"])], "nn", s, gate_off, d, epilogue=heads_epi, out_dtypes=(BF,),
                           extras=[(wide(cos_t), None), (wide(sin_t), None)], tn=head_tile, keep_tn=True,
                           side=gather(under_proj[:1]))
    take_gathered(under_proj[:1], gathered)
    (gates,), gathered = _mm("proj_gates", [(hm, w["w_in"])], "nn", s, 2 * d, d, epilogue=ident, out_dtypes=(F32,),
                             b_off=gate_off, side=gather(under_proj[1:]))
    take_gathered(under_proj[1:], gathered)
    (f_logit,) = _mm("proj_f", [(hm, w["w_in_f"])], "nn", s, LANES, d, epilogue=ident, out_dtypes=(F32,))
    tab_t = _dil_bias(blk)
    dil_src = ((qkv, 0, HEAD_DIM), (qkv, hh, HEAD_DIM), (qkv, 2 * hh, HEAD_DIM))
    y_dil, lse_d = _flash_fwd("dil_fwd", *dil_src, hh, fox=False, tab_t=tab_t)
    bias_f = jnp.pad(small["b_forget"], ((0, 0), (0, LANES - hh)))
    c_pad = _forget_cumsum("forget_cumsum", f_logit, f_col, bias_f)
    q_cat, k_cat = _fox_aug("fox_aug", c_pad, qkv, 3 * hh, 4 * hh, hh)
    fox_src = ((q_cat, 0, 2 * HEAD_DIM), (k_cat, 0, 2 * HEAD_DIM), (qkv, 5 * hh, HEAD_DIM))
    y_fox, lse_f = _flash_fwd("fox_fwd", *fox_src, hh, fox=True)

    def merge_epi(accs, ex, rw):
        ud, uf = accs
        return ud, uf, _sigmoid(ex[0] + rw[0]) * ud + _sigmoid(ex[1] + rw[1]) * uf

    u_d, u_f, merged = _mm("merge", [(y_dil, w["w_proj_dil"]), (y_fox, w["w_proj_fox"])], "nn", s, d, hd,
                           epilogue=merge_epi, out_dtypes=(BF, BF, BF),
                           extras=[(gates, 0), (gates, d)],
                           rows=[small["b_gate_dil"], small["b_gate_fox"]])
    (x2,) = _mm("mix_out", [(merged, w["w_out"])], "nn", s, d, d,
                epilogue=lambda accs, ex, rw: (ex[0] + accs[0],), out_dtypes=(F32,), extras=[(x1, 0)])

    x3, saved2, _ = _ffn_fwd("ffn2", x2, small["ffn2_norm"], w["ffn2_w_gate"], w["ffn2_w_up"],
                             lambda got: w["ffn2_w_down"])
    dx3, dx3_bf, dg_final, sq = _final("final", x3, small["final_norm"].reshape(1, d), tgt)

    dx2, dx2_bf, dg_ffn2, dw_g2, dw_u2, dw_d2, _ = _ffn_bwd("ffn2", x2, small["ffn2_norm"], w["ffn2_w_gate"],
                                                            w["ffn2_w_up"], w["ffn2_w_down"], saved2, dx3, dx3_bf)
    pieces2 = [dw_g2, dw_u2, dw_d2]

    def dmerge_epi(accs, ex, rw):
        dm = accs[0]
        gd, gf, ud, uf = ex[0], ex[1], ex[2].astype(F32), ex[3].astype(F32)
        sd, sf = _sigmoid(gd + rw[0]), _sigmoid(gf + rw[1])
        dgd = dm * ud * (sd * (1.0 - sd))
        dgf = dm * uf * (sf * (1.0 - sf))
        return (dm * sd, dm * sf, dgd, dgf, jnp.sum(dgd, axis=0, keepdims=True), jnp.sum(dgf, axis=0, keepdims=True))

    (du_d, du_f, dg_d, dg_f, dbg_d, dbg_f), from_sibling = _mm(
        "dmerge", [(dx2_bf, w["w_out"])], "nt", s, d, d, epilogue=dmerge_epi, out_dtypes=(BF, BF, BF, BF), n_colsum=2,
        extras=[(gates, 0), (gates, d), (u_d, 0), (u_f, 0)],
        rows=[small["b_gate_dil"], small["b_gate_fox"]], side=_swap_side(pieces2))
    sums2 = chip_sums(FFN2, pieces2, from_sibling)
    (dw_out,) = _mm("dw_out", [(merged, dx2_bf)], "tn", d, d, s, epilogue=ident, out_dtypes=(F32,))
    dw_out = dw_out.reshape(N_CHIPS, d // N_CHIPS, d)
    dw_pd = _dw_col_pieces("dw_pd", y_dil, du_d, hd, d, s)
    dw_pf = _dw_col_pieces("dw_pf", y_fox, du_f, hd, d, s)
    (dy_dil,) = _mm("dy_dil", [(du_d, w["w_proj_dil"])], "nt", s, hd, d, epilogue=ident, out_dtypes=(BF,))
    (dy_fox,) = _mm("dy_fox", [(du_f, w["w_proj_fox"])], "nt", s, hd, d, epilogue=ident, out_dtypes=(BF,))

    row = lambda t: t.reshape(hh, s // blk, 1, blk)
    delta_d = _att_delta("dil_delta", dy_dil, y_dil, hh)
    dq_d, dk_d, dv_d = _flash_bwd("dil_bwd", *dil_src, dy_dil, lse_d, row(delta_d), hh, fox=False, tab_t=tab_t)
    delta_f = _att_delta("fox_delta", dy_fox, y_fox, hh)
    dq_f, dk_f, dv_f, dc_k, dc_q = _flash_bwd("fox_bwd", *fox_src, dy_fox, lse_f, row(delta_f), hh, fox=True)
    dc = dc_k.reshape(hh, s) + dc_q.reshape(hh, s)
    dc_pad = jnp.pad(dc.T, ((0, 0), (0, LANES - hh)))
    df, db_forget = _forget_bwd("forget_bwd", f_logit, f_col, bias_f, dc_pad)
    dproj = _dproj_assemble("dproj", [dq_d, dk_d, dv_d, dq_f, dk_f, dv_f], [dg_d, dg_f], cos_t, sin_t, hh, d)
    (dhm_f,) = _mm("dhm_f", [(df, w["w_in_f"])], "nt", s, d, LANES, epilogue=ident, out_dtypes=(F32,))
    (dhm,), from_chips = _mm("dhm", [(dproj, w["w_in"])], "nt", s, d, n_proj,
                             epilogue=lambda accs, ex, rw: (accs[0] + ex[0],), out_dtypes=(F32,),
                             extras=[(dhm_f, 0)], side=_scatter_side(sums2))
    halves2 = core_halves(FFN2, sums2, from_chips)
    small_mixer = ("w_proj_dil", "w_proj_fox", "w_out")
    pieces_sm = [dw_pd, dw_pf, dw_out]
    (dw_in,), from_sibling = _mm("dw_in", [(hm, dproj)], "tn", d, n_proj, s, epilogue=ident, out_dtypes=(F32,),
                                 side=_swap_side(pieces_sm))
    sums_sm = chip_sums(small_mixer, pieces_sm, from_sibling)
    (dw_in_f,) = _mm("dw_in_f", [(hm, df)], "tn", d, LANES, s, epilogue=ident, out_dtypes=(F32,))
    dw_in = _dw_in_pieces(dw_in, dw_in_f, hd, d)
    dx1, dx1_bf, dg_mix = _rms_bwd("mix_rmsb", x1, small["mix_norm"], dhm, dx2)

    sums = {}

    def under_dwd(got):
        sums["w_in"] = chip_sums(["w_in"], [dw_in], got["dact"])
        return _scatter_side(sums_sm)

    def under_dwu(got):
        return _swap_side([got["dw_down"], got["dw_gate"]])

    def under_dh(got):
        sums["dg"] = chip_sums(["ffn1_w_down", "ffn1_w_gate"], [got["dw_down"], got["dw_gate"]], got["dwu"])
        return _scatter_side(sums["dg"]).beside(_swap_side([got["dw_up"]]))

    dx0, _, dg_ffn1, dw_g1, dw_u1, dw_d1, got = _ffn_bwd(
        "ffn1", x, small["ffn1_norm"], w["ffn1_w_gate"], w["ffn1_w_up"], w["ffn1_w_down"], saved1, dx1, dx1_bf,
        sides={"dact": lambda got: _swap_side([dw_in]), "dwd": under_dwd,
               "dwg": lambda got: _scatter_side(sums["w_in"]), "dwu": under_dwu, "dh": under_dh})
    halves_sm = core_halves(small_mixer, sums_sm, got["dwd"])
    halves_in = core_halves(["w_in"], sums["w_in"], got["dwg"])
    halves_dg = core_halves(["ffn1_w_down", "ffn1_w_gate"], sums["dg"], got["dh"][:2])
    sums_u = chip_sums(["ffn1_w_up"], [dw_u1], got["dh"][2:])
    halves_u = core_halves(["ffn1_w_up"], sums_u, _scatter_side(sums_u).call("rs_scatter_last"))

    names = ("ffn1_w_down", "ffn1_w_gate", "ffn1_w_up", "w_in") + small_mixer + FFN2
    totals = _join_side(halves_dg + halves_u + halves_in + halves_sm + halves2).call("rs_join")
    grads = dict(zip(names, totals))
    partials = {"ffn1_norm": dg_ffn1, "mix_norm": dg_mix, "ffn2_norm": dg_ffn2, "final_norm": dg_final,
                "b_gate_dil": dbg_d.reshape(-1, d), "b_gate_fox": dbg_f.reshape(-1, d), "b_forget": db_forget, "sq": sq}
    return dx0, grads, partials


def _coords():
    return lax.axis_index("x"), lax.axis_index("y"), lax.axis_index("c")


def _other_chips(x, y):
    return [(1 - x, y), (x, 1 - y), (1 - x, 1 - y)]


ANY_SPEC = pl.BlockSpec(memory_space=pl.ANY)


def _gather_side(shards):
    nw = len(shards)

    def copies(srcs, outs, send_sems, recv_sems):
        x, y, c = _coords()
        chips = _other_chips(x, y)

        def slot(w, px, py, pc):
            half = shards[w].shape[0] // 2
            return outs[w].at[2 * px + py, pl.ds(pc * half, half), :]

        def copy(w, k, src_ref, dst_ref, to):
            return pltpu.make_async_remote_copy(src_ref=src_ref, dst_ref=dst_ref, send_sem=send_sems.at[6 * w + k],
                                                recv_sem=recv_sems.at[6 * w + k], device_id=to, device_id_type=MESH)

        first, arrive, passed, arrive2 = [], [], [], []
        for w in range(nw):
            half = shards[w].shape[0] // 2
            for j, chip in enumerate(chips):
                first.append(copy(w, j, srcs[w].at[pl.ds(c * half, half), :], slot(w, x, y, c), (*chip, c)))
                arrive.append(copy(w, j, slot(w, *chip, c), slot(w, *chip, c), (*chip, c)))
                passed.append(copy(w, 3 + j, slot(w, *chip, c), slot(w, *chip, c), (x, y, 1 - c)))
                arrive2.append(copy(w, 3 + j, slot(w, *chip, 1 - c), slot(w, *chip, 1 - c), (x, y, 1 - c)))
        return first, arrive, passed, arrive2

    def start(srcs, outs, send_sems, recv_sems):
        for cp in copies(srcs, outs, send_sems, recv_sems)[0]:
            cp.start()

    def finish(srcs, outs, send_sems, recv_sems):
        first, arrive, passed, arrive2 = copies(srcs, outs, send_sems, recv_sems)
        for got, fwd in zip(arrive, passed):
            got.wait_recv()
            fwd.start()
        for got in arrive2:
            got.wait_recv()
        for cp in first + passed:
            cp.wait_send()

    return _Side(shards, [jax.ShapeDtypeStruct((N_CHIPS, *t.shape), t.dtype) for t in shards], 6 * nw, start, finish)


def _swap_side(grads):
    nw = len(grads)

    def copies(srcs, outs, send_sems, recv_sems):
        x, y, c = _coords()
        res = []
        for w in range(nw):
            half = grads[w].shape[1] // 2
            for p in range(N_CHIPS):
                k = N_CHIPS * w + p
                res.append(pltpu.make_async_remote_copy(
                    src_ref=srcs[w].at[p, pl.ds((1 - c) * half, half), :], dst_ref=outs[w].at[p],
                    send_sem=send_sems.at[k], recv_sem=recv_sems.at[k], device_id=(x, y, 1 - c), device_id_type=MESH))
        return res

    def start(*refs):
        for cp in copies(*refs):
            cp.start()

    def finish(*refs):
        for cp in copies(*refs):
            cp.wait()

    shapes = [jax.ShapeDtypeStruct((N_CHIPS, t.shape[1] // 2, t.shape[2]), t.dtype) for t in grads]
    return _Side(grads, shapes, N_CHIPS * nw, start, finish)


def _rs_add(name, ids, g, other):
    n, rows, cols = g.shape
    half = rows // 2
    tr = _pick(half, 256, 16)
    nb = half // tr

    def body(ids_ref, g_ref, o_ref, out_ref):
        out_ref[...] = (g_ref[...] + o_ref[...]).astype(BF)

    grid_spec = pltpu.PrefetchScalarGridSpec(
        num_scalar_prefetch=1, grid=(n, nb),
        in_specs=[pl.BlockSpec((None, tr, cols), lambda p, i, ids_ref: (p, ids_ref[1] * nb + i, 0)),
                  pl.BlockSpec((None, tr, cols), lambda p, i, ids_ref: (p, i, 0))],
        out_specs=pl.BlockSpec((None, tr, cols), lambda p, i, ids_ref: (p, i, 0)))
    return pl.pallas_call(body, grid_spec=grid_spec, out_shape=jax.ShapeDtypeStruct((n, half, cols), BF),
                          compiler_params=_params(("parallel", "parallel")), name=name)(ids, g, other)


def _scatter_side(sums):
    nw = len(sums)

    def copies(srcs, outs, send_sems, recv_sems):
        x, y, c = _coords()
        res = []
        for w in range(nw):
            for k, (px, py) in enumerate(_other_chips(x, y)):
                res.append(pltpu.make_async_remote_copy(
                    src_ref=srcs[w].at[2 * px + py], dst_ref=outs[w].at[k], send_sem=send_sems.at[3 * w + k],
                    recv_sem=recv_sems.at[3 * w + k], device_id=(px, py, c), device_id_type=MESH))
        return res

    def start(*refs):
        for cp in copies(*refs):
            cp.start()

    def finish(*refs):
        for cp in copies(*refs):
            cp.wait()

    return _Side(sums, [jax.ShapeDtypeStruct((3, *t.shape[1:]), t.dtype) for t in sums], 3 * nw, start, finish)


def _rs_sum(name, ids, own, got):
    n, half, cols = own.shape
    tr = _pick(half, 256, 16)
    nb = half // tr

    def body(ids_ref, own_ref, got_ref, out_ref):
        t = own_ref[...].astype(F32)
        for k in range(3):
            t = t + got_ref[k].astype(F32)
        out_ref[...] = t

    grid_spec = pltpu.PrefetchScalarGridSpec(
        num_scalar_prefetch=1, grid=(nb,),
        in_specs=[pl.BlockSpec((None, tr, cols), lambda i, ids_ref: (ids_ref[0], i, 0)),
                  pl.BlockSpec((3, tr, cols), lambda i, ids_ref: (0, i, 0))],
        out_specs=pl.BlockSpec((tr, cols), lambda i, ids_ref: (ids_ref[1] * nb + i, 0)))
    return pl.pallas_call(body, grid_spec=grid_spec, out_shape=jax.ShapeDtypeStruct((2 * half, cols), F32),
                          compiler_params=_params(("parallel",)), name=name)(ids, own, got)


def _join_side(totals):
    nw = len(totals)

    def start(ins, bufs, send_sems, recv_sems):
        x, y, c = _coords()
        for w in range(nw):
            half = totals[w].shape[0] // 2
            pltpu.make_async_remote_copy(
                src_ref=bufs[w].at[pl.ds(c * half, half), :], dst_ref=bufs[w].at[pl.ds(c * half, half), :],
                send_sem=send_sems.at[w], recv_sem=recv_sems.at[w], device_id=(x, y, 1 - c),
                device_id_type=MESH).start()

    def finish(ins, bufs, send_sems, recv_sems):
        x, y, c = _coords()
        for w in range(nw):
            half = totals[w].shape[0] // 2
            arrival = pltpu.make_async_remote_copy(
                src_ref=bufs[w].at[pl.ds(c * half, half), :], dst_ref=bufs[w].at[pl.ds((1 - c) * half, half), :],
                send_sem=send_sems.at[w], recv_sem=recv_sems.at[w], device_id=(x, y, 1 - c), device_id_type=MESH)
            arrival.wait_recv()
            arrival.wait_send()

    return _Side(totals, [jax.ShapeDtypeStruct(t.shape, t.dtype) for t in totals], nw, start, finish,
                 aliases={w: w for w in range(nw)})


def _gather_all(name, t):
    rows, cols = t.shape

    def body(src, out, send_sems, recv_sems, local_sem):
        x, y, c = _coords()
        me = 4 * x + 2 * y + c
        mine = pltpu.make_async_copy(src, out.at[me], local_sem)
        mine.start()
        peers = [(x ^ (k >> 2 & 1), y ^ (k >> 1 & 1), c ^ (k & 1)) for k in range(1, N_DEV)]
        sends = [pltpu.make_async_remote_copy(src_ref=src, dst_ref=out.at[me], send_sem=send_sems.at[k],
                                              recv_sem=recv_sems.at[k], device_id=peer, device_id_type=MESH)
                 for k, peer in enumerate(peers)]
        for cp in sends:
            cp.start()
        for k, (px, py, pc) in enumerate(peers):
            pltpu.make_async_remote_copy(src_ref=src, dst_ref=out.at[4 * px + 2 * py + pc], send_sem=send_sems.at[k],
                                         recv_sem=recv_sems.at[k], device_id=(px, py, pc),
                                         device_id_type=MESH).wait_recv()
        for cp in sends:
            cp.wait_send()
        mine.wait()

    vmem = pl.BlockSpec(memory_space=pltpu.VMEM)
    return pl.pallas_call(
        body, in_specs=[vmem], out_specs=vmem, out_shape=jax.ShapeDtypeStruct((N_DEV, rows, cols), t.dtype),
        scratch_shapes=[pltpu.SemaphoreType.DMA((7,)), pltpu.SemaphoreType.DMA((7,)), pltpu.SemaphoreType.DMA],
        name=name)(t)


def _adamw_math(w, g, m, v):
    m = ADAM_B1 * m + (1.0 - ADAM_B1) * g
    v = ADAM_B2 * v + (1.0 - ADAM_B2) * (g * g)
    m_hat = m / (1.0 - ADAM_B1 ** ADAM_STEP)
    v_hat = v / (1.0 - ADAM_B2 ** ADAM_STEP)
    delta = -ADAM_LR * (m_hat / (jnp.sqrt(v_hat) + ADAM_EPS) + ADAM_WD * w)
    return delta, m, v


def _adamw(name, w, g, m, v):
    _, rows, cols = w.shape
    tr = _pick(rows, 256, 8)

    def body(w_ref, g_ref, m_ref, v_ref, g_out, d_out, m_out, v_out):
        g = g_ref[...]
        g_out[...] = g
        d_out[...], m_out[...], v_out[...] = _adamw_math(w_ref[...], g, m_ref[...], v_ref[...])

    blk3 = pl.BlockSpec((None, tr, cols), lambda i: (0, i, 0))
    blk = pl.BlockSpec((tr, cols), lambda i: (i, 0))
    shape = jax.ShapeDtypeStruct((rows, cols), F32)
    return pl.pallas_call(
        body, grid=(rows // tr,), in_specs=[blk3, blk, blk3, blk3], out_specs=[blk] * 4, out_shape=[shape] * 4,
        compiler_params=_params(("parallel",)), name=name)(w, g, m, v)


def _small_reduce(name, parts, width):
    def body(*refs):
        out = refs[-1]
        out[...] = jnp.zeros_like(out)
        for k, r in enumerate(refs[:-1]):
            out[pl.ds(k, 1), :] = jnp.sum(r[...], axis=0, keepdims=True)

    vmem = pl.BlockSpec(memory_space=pltpu.VMEM)
    return pl.pallas_call(body, in_specs=[vmem] * len(parts), out_specs=vmem,
                          out_shape=jax.ShapeDtypeStruct((8, width), F32), name=name)(*parts)


def _small_adamw(name, gathered, w, m, v, loss_row, loss_scale):
    def body(gt_ref, w_ref, m_ref, v_ref, g_out, d_out, m_out, v_out, loss_out):
        g = gt_ref[0]
        for k in range(1, N_DEV):
            g = g + gt_ref[k]
        g_out[...] = g
        row = lax.broadcasted_iota(jnp.int32, g.shape, 0)
        loss_out[...] = jnp.sum(jnp.where(row == loss_row, g, 0.0), keepdims=True) * loss_scale
        d_out[...], m_out[...], v_out[...] = _adamw_math(w_ref[...], g, m_ref[...], v_ref[...])

    vmem = pl.BlockSpec(memory_space=pltpu.VMEM)
    shape = jax.ShapeDtypeStruct(w.shape, F32)
    return pl.pallas_call(body, in_specs=[vmem] * 4, out_specs=[vmem] * 5,
                          out_shape=[shape] * 4 + [jax.ShapeDtypeStruct((1, 1), F32)], name=name)(gathered, w, m, v)


def _full_from_pieces(name, pieces):
    _, rows, cols = pieces.shape
    if name in ROW_SHARDED:
        return pieces.reshape(N_CHIPS * rows, cols)
    return pieces.transpose(1, 0, 2).reshape(rows, N_CHIPS * cols)


def _column_range(segments, lo, hi):
    out, start = [], 0
    for t in segments:
        a, b = max(lo, start), min(hi, start + t.shape[1])
        if a < b:
            out.append(t[:, a - start:b - start])
        start += t.shape[1]
    return out


def _repack_w_in(pieces, hd, d):
    hh = hd // HEAD_DIM
    segs = [pieces[p] for p in range(N_CHIPS)]
    total = N_CHIPS * pieces.shape[2]
    main = jnp.concatenate(_column_range(segs, 0, 6 * hd) + _column_range(segs, 6 * hd + hh, total), axis=1)
    f = jnp.concatenate(_column_range(segs, 6 * hd, 6 * hd + hh), axis=1)
    return main, jnp.pad(f, ((0, 0), (0, LANES - hh)))


def _dw_in_pieces(dw, dw_f, hd, d):
    hh = hd // HEAD_DIM
    segs = [dw[:, :6 * hd], dw_f[:, :hh], dw[:, 6 * hd:]]
    cs = (6 * hd + hh + 2 * d) // N_CHIPS
    return jnp.stack([jnp.concatenate(_column_range(segs, p * cs, (p + 1) * cs), axis=1) for p in range(N_CHIPS)])


def _small_pack(vals, width):
    rows = []
    for name in SMALL:
        t = vals[name].reshape(1, -1)
        rows.append(jnp.pad(t, ((0, 0), (0, width - t.shape[1]))))
    rows.append(jnp.zeros((8 - len(SMALL), width), F32))
    return jnp.concatenate(rows, axis=0)


def kernel(x, ffn1_norm, ffn1_w_gate, ffn1_w_up, ffn1_w_down, mix_norm, w_in, b_forget, b_gate_dil, b_gate_fox, w_proj_dil, w_proj_fox, w_out, ffn2_norm, ffn2_w_gate, ffn2_w_up, ffn2_w_down, final_norm, loss_target, m_ffn1_norm, m_ffn1_w_gate, m_ffn1_w_up, m_ffn1_w_down, m_mix_norm, m_w_in, m_b_forget, m_b_gate_dil, m_b_gate_fox, m_w_proj_dil, m_w_proj_fox, m_w_out, m_ffn2_norm, m_ffn2_w_gate, m_ffn2_w_up, m_ffn2_w_down, m_final_norm, v_ffn1_norm, v_ffn1_w_gate, v_ffn1_w_up, v_ffn1_w_down, v_mix_norm, v_w_in, v_b_forget, v_b_gate_dil, v_b_gate_fox, v_w_proj_dil, v_w_proj_fox, v_w_out, v_ffn2_norm, v_ffn2_w_gate, v_ffn2_w_up, v_ffn2_w_down, v_final_norm):
    given = dict(locals())
    wts = {n: given[n] for n in WEIGHTS}
    mom_m = {n: given["m_" + n] for n in WEIGHTS}
    mom_v = {n: given["v_" + n] for n in WEIGHTS}
    d = x.shape[2]

    shards = {n: wts[n][0].astype(BF) for n in SHARDED}
    small = {n: wts[n] for n in SMALL}
    grad_x, grads, partials = _device_step(x[0], loss_target[0], shards, small)

    out_g, out_d, out_m, out_v = {}, {}, {}, {}
    for n in SHARDED:
        outs = _adamw("adamw_" + n, wts[n], grads[n], mom_m[n], mom_v[n])
        out_g[n], out_d[n], out_m[n], out_v[n] = (t[None] for t in outs)

    width = d
    part_rows = []
    for n in SMALL:
        t = partials[n]
        part_rows.append(jnp.pad(t, ((0, 0), (0, width - t.shape[1]))))
    part_rows.append(partials["sq"])
    local_small = _small_reduce("small_reduce", part_rows, width)
    gathered_small = _gather_all("small_gather", local_small)
    sg, sd_, sm, sv, loss = _small_adamw("small_adamw", gathered_small, _small_pack(wts, width),
                                         _small_pack(mom_m, width), _small_pack(mom_v, width), len(SMALL), 0.5 / d)
    for k, n in enumerate(SMALL):
        shp = wts[n].shape
        take = lambda t: t[k, :shp[-1]].reshape(shp)
        out_g[n], out_d[n], out_m[n], out_v[n] = take(sg), take(sd_), take(sm), take(sv)
    return (loss[0, 0], grad_x[None], *[out_g[n] for n in WEIGHTS], *[out_d[n] for n in WEIGHTS],
            *[out_m[n] for n in WEIGHTS], *[out_v[n] for n in WEIGHTS])
```

```python
import functools
import math

import jax
import jax.numpy as jnp
from jax import lax
from jax.experimental import pallas as pl
from jax.experimental.pallas import tpu as pltpu

HEAD_DIM = 128
ROPE_DIM = HEAD_DIM // 4
ROPE_THETA = 500000.0
DIL_PATTERNS = ((128, 1), (512, 4), (2048, 16))
MAX_WINDOW = 2048
NORM_EPS = 1e-6
ADAM_LR = 0.001
ADAM_B1 = 0.9
ADAM_B2 = 0.999
ADAM_EPS = 1e-08
ADAM_WD = 0.01
ADAM_STEP = 10

BF = jnp.bfloat16
F32 = jnp.float32
NEG = -1e30
LANES = 128
ATT_BLOCK = 512
PREP_COLS = 512
ATT_HEADS = 2
VMEM_LIMIT = 56 * 1024 * 1024
MM_VMEM_BUDGET = 40 * 1024 * 1024
N_CHIPS = 4
N_DEV = 8
MESH = pl.DeviceIdType.MESH

SHARDED = ("ffn1_w_gate", "ffn1_w_up", "ffn1_w_down", "w_in", "w_proj_dil", "w_proj_fox", "w_out",
           "ffn2_w_gate", "ffn2_w_up", "ffn2_w_down")
ROW_SHARDED = ("ffn1_w_down", "w_out", "ffn2_w_down")
SMALL = ("ffn1_norm", "mix_norm", "b_forget", "b_gate_dil", "b_gate_fox", "ffn2_norm", "final_norm")
WEIGHTS = ("ffn1_norm", "ffn1_w_gate", "ffn1_w_up", "ffn1_w_down", "mix_norm", "w_in", "b_forget",
           "b_gate_dil", "b_gate_fox", "w_proj_dil", "w_proj_fox", "w_out", "ffn2_norm", "ffn2_w_gate",
           "ffn2_w_up", "ffn2_w_down", "final_norm")


def _pick(n, target, align):
    best = None
    for d in range(align, min(n, target) + 1, align):
        if n % d == 0:
            best = d
    return n if best is None else best


def _params(sem=None):
    return pltpu.CompilerParams(dimension_semantics=sem, vmem_limit_bytes=VMEM_LIMIT)


_DIMS = {"nn": (((1,), (0,)), ((), ())), "nt": (((1,), (1,)), ((), ())), "tn": (((0,), (0,)), ((), ()))}


class _SemsFrom:
    def __init__(self, sems, first):
        self.sems, self.first = sems, first

    @property
    def at(self):
        return self

    def __getitem__(self, k):
        return self.sems.at[self.first + k]


class _Side:
    def __init__(self, inputs, out_shapes, n_sems, start, finish, aliases=None):
        self.inputs, self.out_shapes, self.n_sems = list(inputs), list(out_shapes), n_sems
        self.start, self.finish, self.aliases = start, finish, aliases or {}

    def scratch(self):
        return [pltpu.SemaphoreType.DMA((self.n_sems,)), pltpu.SemaphoreType.DMA((self.n_sems,))]

    def beside(self, other):
        n_in, n_out, n_sems = len(self.inputs), len(self.out_shapes), self.n_sems

        def part(fn_a, fn_b):
            def run(ins, outs, send_sems, recv_sems):
                fn_a(ins[:n_in], outs[:n_out], send_sems, recv_sems)
                fn_b(ins[n_in:], outs[n_out:], _SemsFrom(send_sems, n_sems), _SemsFrom(recv_sems, n_sems))
            return run

        assert not self.aliases and not other.aliases
        return _Side(self.inputs + other.inputs, self.out_shapes + other.out_shapes, n_sems + other.n_sems,
                     part(self.start, other.start), part(self.finish, other.finish))

    def call(self, name):
        n_in, n_out = len(self.inputs), len(self.out_shapes)

        def body(*refs):
            ins, outs, sems = refs[:n_in], refs[n_in:n_in + n_out], refs[n_in + n_out:]
            self.start(ins, outs, *sems)
            self.finish(ins, outs, *sems)

        return pl.pallas_call(body, in_specs=[ANY_SPEC] * n_in, out_specs=[ANY_SPEC] * n_out, out_shape=self.out_shapes,
                              input_output_aliases=self.aliases, scratch_shapes=self.scratch(), name=name)(*self.inputs)


def _mm(name, pairs, mode, m, n, k, *, epilogue, out_dtypes, extras=(), rows=(), n_colsum=0,
        sum_pairs=False, tm=1024, tn=1152, tk=2048, piece_layout=False, side=None, b_off=0, keep_tn=False):
    m_align = LANES if mode == "tn" else 8
    tm = _pick(m, tm, m_align)
    tn = n // N_CHIPS if piece_layout else _pick(n, tn, LANES)
    tk = _pick(k, tk, LANES)
    n_acc = 1 if sum_pairs else len(pairs)
    lhs = []
    for a, _ in pairs:
        if not any(a is t for t in lhs):
            lhs.append(a)
    lhs_of = [next(t for t in range(len(lhs)) if lhs[t] is a) for a, _ in pairs]
    n_mm = len(lhs) + len(pairs)
    n_in = n_mm + len(extras) + len(rows)
    n_out = len(out_dtypes) + n_colsum

    def vmem_bytes(tm_, tn_, tk_):
        tiles = sum(tm_ * tk_ * a.dtype.itemsize for a in lhs) + sum(tn_ * tk_ * b.dtype.itemsize for _, b in pairs)
        tiles += sum(tm_ * tn_ * arr.dtype.itemsize for arr, _ in extras)
        tiles += sum(tm_ * tn_ * jnp.dtype(dt).itemsize for dt in out_dtypes)
        return 2 * tiles + (n_acc + len(extras) + len(out_dtypes)) * tm_ * tn_ * 4

    while vmem_bytes(tm, tn, tk) > MM_VMEM_BUDGET:
        if tn > 512 and not piece_layout and not keep_tn:
            tn = _pick(n, tn - LANES, LANES)
        elif tm > 512:
            tm = _pick(m, tm - m_align, m_align)
        elif tk > 512:
            tk = _pick(k, tk - LANES, LANES)
        elif tm > 256:
            tm = _pick(m, tm - m_align, m_align)
        else:
            break
    nk = k // tk

    n_side_in = len(side.inputs) if side else 0
    n_side_out = len(side.out_shapes) if side else 0
    grid = (m // tm, n // tn, nk)

    def body(*refs):
        ins, refs = refs[:n_in], refs[n_in:]
        side_ins, refs = refs[:n_side_in], refs[n_side_in:]
        outs, refs = refs[:n_out], refs[n_out:]
        side_outs, refs = refs[:n_side_out], refs[n_side_out:]
        accs, side_sems = refs[:n_acc], refs[n_acc:]
        kk = pl.program_id(2)
        if side:
            at = [pl.program_id(t) for t in range(3)]

            @pl.when(jnp.logical_and(jnp.logical_and(at[0] == 0, at[1] == 0), at[2] == 0))
            def _():
                side.start(side_ins, side_outs, *side_sems)

        @pl.when(kk == 0)
        def _():
            for acc in accs:
                acc[...] = jnp.zeros_like(acc)

        a_tiles = [r[...].astype(BF) for r in ins[:len(lhs)]]
        for p in range(len(pairs)):
            b = ins[len(lhs) + p][...].astype(BF)
            accs[0 if sum_pairs else p][...] += lax.dot_general(a_tiles[lhs_of[p]], b, _DIMS[mode],
                                                                preferred_element_type=F32)

        @pl.when(kk == nk - 1)
        def _():
            ex = [r[...] for r in ins[n_mm:n_mm + len(extras)]]
            rw = [r[...] for r in ins[n_mm + len(extras):]]
            res = epilogue([acc[...] for acc in accs], ex, rw)
            for o, r in zip(outs, res):
                o[...] = r.astype(o.dtype)

        if side:
            @pl.when(jnp.logical_and(jnp.logical_and(at[0] == grid[0] - 1, at[1] == grid[1] - 1), at[2] == nk - 1))
            def _():
                side.finish(side_ins, side_outs, *side_sems)

    in_specs, args = [], []
    for a in lhs:
        if mode == "tn":
            in_specs.append(pl.BlockSpec((tk, tm), lambda i, j, kk: (kk, i)))
        else:
            in_specs.append(pl.BlockSpec((tm, tk), lambda i, j, kk: (i, kk)))
        args.append(a)
    assert b_off % tn == 0 and (b_off == 0 or mode == "nn")
    for _, b in pairs:
        if mode == "nt":
            in_specs.append(pl.BlockSpec((tn, tk), lambda i, j, kk: (j, kk)))
        else:
            in_specs.append(pl.BlockSpec((tk, tn), functools.partial(lambda i, j, kk, o: (kk, j + o), o=b_off // tn)))
        args.append(b)
    for arr, off in extras:
        if off is None:
            in_specs.append(pl.BlockSpec((tm, tn), lambda i, j, kk: (i, 0)))
        else:
            assert off % tn == 0
            in_specs.append(pl.BlockSpec((tm, tn), functools.partial(lambda i, j, kk, o: (i, j + o), o=off // tn)))
        args.append(arr)
    for arr in rows:
        in_specs.append(pl.BlockSpec((1, tn), lambda i, j, kk: (0, j)))
        args.append(arr)
    if piece_layout:
        out_specs = [pl.BlockSpec((None, tm, tn), lambda i, j, kk: (j, i, 0)) for _ in out_dtypes]
        out_shape = [jax.ShapeDtypeStruct((n // tn, m, tn), d) for d in out_dtypes]
    else:
        out_specs = [pl.BlockSpec((tm, tn), lambda i, j, kk: (i, j)) for _ in out_dtypes]
        out_shape = [jax.ShapeDtypeStruct((m, n), d) for d in out_dtypes]
    for _ in range(n_colsum):
        out_specs.append(pl.BlockSpec((None, 1, tn), lambda i, j, kk: (i, 0, j)))
        out_shape.append(jax.ShapeDtypeStruct((m // tm, 1, n), F32))
    scratch = [pltpu.VMEM((tm, tn), F32) for _ in range(n_acc)]
    if side is None:
        return pl.pallas_call(
            body, grid=grid, in_specs=in_specs, out_specs=out_specs, out_shape=out_shape, scratch_shapes=scratch,
            compiler_params=_params(("parallel", "parallel", "arbitrary")), name=name)(*args)
    res = pl.pallas_call(
        body, grid=grid, in_specs=in_specs + [ANY_SPEC] * n_side_in, out_specs=out_specs + [ANY_SPEC] * n_side_out,
        out_shape=out_shape + side.out_shapes, scratch_shapes=scratch + side.scratch(),
        input_output_aliases={n_in + t: n_out + o for t, o in side.aliases.items()},
        compiler_params=_params(("arbitrary", "arbitrary", "arbitrary")), name=name)(*args, *side.inputs)
    return res[:n_out], res[n_out:]


def _col_pieces(full):
    rows, cols = full.shape
    return full.reshape(rows, N_CHIPS, cols // N_CHIPS).transpose(1, 0, 2)


def _dw_col_pieces(name, a, b, m, n, k, side=None):
    ident = lambda accs, ex, rw: (accs[0],)
    aligned = (n // N_CHIPS) % LANES == 0
    res = _mm(name, [(a, b)], "tn", m, n, k, epilogue=ident, out_dtypes=(F32,), tm=512 if aligned else 1024,
              piece_layout=aligned, side=side)
    (out,), side_res = res if side else (res, None)
    out = out if aligned else _col_pieces(out)
    return (out, side_res) if side else out


def _hosted(sides, got, key, call):
    make = sides.get(key) if sides else None
    if make is None:
        return call(None)
    outs, got[key] = call(make(got))
    return outs


def _sigmoid(z):
    return 0.5 * jnp.tanh(0.5 * z) + 0.5


def _row_tile(s):
    return _pick(s, 256, 8)


def _fold8(t):
    r, d = t.shape
    return jnp.sum(t.reshape(r // 8, 8, d), axis=0)


def _rms_fwd(name, x, g):
    s, d = x.shape
    tr = _row_tile(s)

    def body(x_ref, g_ref, h_ref):
        xf = x_ref[...]
        y = xf * lax.rsqrt(jnp.mean(xf * xf, axis=-1, keepdims=True) + NORM_EPS)
        h_ref[...] = (y * g_ref[...]).astype(BF)

    return pl.pallas_call(
        body, grid=(s // tr,),
        in_specs=[pl.BlockSpec((tr, d), lambda i: (i, 0)), pl.BlockSpec((1, d), lambda i: (0, 0))],
        out_specs=pl.BlockSpec((tr, d), lambda i: (i, 0)), out_shape=jax.ShapeDtypeStruct((s, d), BF),
        compiler_params=_params(("parallel",)), name=name)(x, g)


def _rms_bwd(name, x, g, dh, dres):
    s, d = x.shape
    tr = _row_tile(s)

    def body(x_ref, g_ref, dh_ref, dres_ref, dx_ref, dxb_ref, dg_ref):
        @pl.when(pl.program_id(0) == 0)
        def _():
            dg_ref[...] = jnp.zeros_like(dg_ref)

        xf = x_ref[...]
        rstd = lax.rsqrt(jnp.mean(xf * xf, axis=-1, keepdims=True) + NORM_EPS)
        xhat = xf * rstd
        dhf = dh_ref[...]
        dg_ref[...] += _fold8(dhf * xhat)
        dxh = dhf * g_ref[...]
        dx = dres_ref[...] + rstd * (dxh - xhat * jnp.mean(dxh * xhat, axis=-1, keepdims=True))
        dx_ref[...] = dx
        dxb_ref[...] = dx.astype(BF)

    blk = pl.BlockSpec((tr, d), lambda i: (i, 0))
    return pl.pallas_call(
        body, grid=(s // tr,),
        in_specs=[blk, pl.BlockSpec((1, d), lambda i: (0, 0)), blk, blk],
        out_specs=[blk, blk, pl.BlockSpec((8, d), lambda i: (0, 0))],
        out_shape=[jax.ShapeDtypeStruct((s, d), F32), jax.ShapeDtypeStruct((s, d), BF),
                   jax.ShapeDtypeStruct((8, d), F32)],
        compiler_params=_params(("arbitrary",)), name=name)(x, g, dh, dres)


def _final(name, x, g, tgt):
    s, d = x.shape
    tr = _row_tile(s)

    def body(x_ref, g_ref, t_ref, dx_ref, dxb_ref, dg_ref, sq_ref):
        @pl.when(pl.program_id(0) == 0)
        def _():
            dg_ref[...] = jnp.zeros_like(dg_ref)
            sq_ref[...] = jnp.zeros_like(sq_ref)

        xf = x_ref[...]
        rstd = lax.rsqrt(jnp.mean(xf * xf, axis=-1, keepdims=True) + NORM_EPS)
        xhat = xf * rstd
        gf = g_ref[...]
        err = xhat * gf - t_ref[...]
        sq_ref[...] += _fold8(err * err)
        dy = err * (1.0 / d)
        dg_ref[...] += _fold8(dy * xhat)
        dxh = dy * gf
        dx = rstd * (dxh - xhat * jnp.mean(dxh * xhat, axis=-1, keepdims=True))
        dx_ref[...] = dx
        dxb_ref[...] = dx.astype(BF)

    blk = pl.BlockSpec((tr, d), lambda i: (i, 0))
    acc = pl.BlockSpec((8, d), lambda i: (0, 0))
    return pl.pallas_call(
        body, grid=(s // tr,), in_specs=[blk, pl.BlockSpec((1, d), lambda i: (0, 0)), blk],
        out_specs=[blk, blk, acc, acc],
        out_shape=[jax.ShapeDtypeStruct((s, d), F32), jax.ShapeDtypeStruct((s, d), BF),
                   jax.ShapeDtypeStruct((8, d), F32), jax.ShapeDtypeStruct((8, d), F32)],
        compiler_params=_params(("arbitrary",)), name=name)(x, g, tgt)


def _ffn_fwd(tag, x, g, w_gate, w_up, get_w_down, sides=None):
    s, d = x.shape
    f = w_gate.shape[1]
    got = {}
    h = _rms_fwd(tag + "_rms", x, g)

    def up_epi(accs, ex, rw):
        a, b = accs
        return a, b, a * _sigmoid(a) * b

    a, b, act = _hosted(sides, got, "up", lambda side: _mm(
        tag + "_up", [(h, w_gate), (h, w_up)], "nn", s, f, d, epilogue=up_epi, out_dtypes=(BF, BF, BF), side=side))

    def down_epi(accs, ex, rw):
        return (ex[0] + 0.5 * accs[0],)

    w_down = get_w_down(got)
    (y,) = _hosted(sides, got, "down", lambda side: _mm(
        tag + "_down", [(act, w_down)], "nn", s, d, f, epilogue=down_epi, out_dtypes=(F32,), extras=[(x, 0)],
        side=side))
    return y, (h, a, b, act), got


def _ffn_fwd_split(tag, x, g, w_gate, get_w_up, get_w_down, sides):
    s, d = x.shape
    f = w_gate.shape[1]
    got = {}
    h = _rms_fwd(tag + "_rms", x, g)
    (a,) = _hosted(sides, got, "gate", lambda side: _mm(
        tag + "_gate", [(h, w_gate)], "nn", s, f, d, epilogue=lambda accs, ex, rw: (accs[0],), out_dtypes=(BF,),
        side=side))

    def up_epi(accs, ex, rw):
        av, b = ex[0].astype(F32), accs[0]
        return b, av * _sigmoid(av) * b

    w_up = get_w_up(got)
    b, act = _hosted(sides, got, "up", lambda side: _mm(
        tag + "_up", [(h, w_up)], "nn", s, f, d, epilogue=up_epi, out_dtypes=(BF, BF), extras=[(a, 0)], side=side))
    w_down = get_w_down(got)
    (y,) = _hosted(sides, got, "down", lambda side: _mm(
        tag + "_down", [(act, w_down)], "nn", s, d, f, epilogue=lambda accs, ex, rw: (ex[0] + 0.5 * accs[0],),
        out_dtypes=(F32,), extras=[(x, 0)], side=side))
    return y, (h, a, b, act), got


def _ffn_bwd(tag, x, g, w_gate, w_up, w_down, saved, dy, dy_bf, sides=None):
    s, d = x.shape
    f = w_gate.shape[1]
    h, a, b, act = saved
    got = {}

    def act_epi(accs, ex, rw):
        dact = 0.5 * accs[0]
        av, bv = ex[0].astype(F32), ex[1].astype(F32)
        sg = _sigmoid(av)
        return dact * bv * (sg * (1.0 + av * (1.0 - sg))), dact * (av * sg)

    da, db = _hosted(sides, got, "dact", lambda side: _mm(
        tag + "_dact", [(dy_bf, w_down)], "nt", s, f, d, epilogue=act_epi, out_dtypes=(BF, BF),
        extras=[(a, 0), (b, 0)], side=side))
    ident = lambda accs, ex, rw: (accs[0],)
    (dw_down,) = _hosted(sides, got, "dwd", lambda side: _mm(
        tag + "_dwd", [(act, dy_bf)], "tn", f, d, s, epilogue=lambda accs, ex, rw: (0.5 * accs[0],),
        out_dtypes=(F32,), side=side))
    got["dw_down"] = dw_down = dw_down.reshape(N_CHIPS, f // N_CHIPS, d)
    got["dw_gate"] = dw_gate = _hosted(sides, got, "dwg", lambda side: _dw_col_pieces(
        tag + "_dwg", h, da, d, f, s, side=side))
    got["dw_up"] = dw_up = _hosted(sides, got, "dwu", lambda side: _dw_col_pieces(
        tag + "_dwu", h, db, d, f, s, side=side))
    (dh,) = _hosted(sides, got, "dh", lambda side: _mm(
        tag + "_dh", [(da, w_gate), (db, w_up)], "nt", s, d, f, epilogue=ident, out_dtypes=(F32,), sum_pairs=True,
        side=side))
    dx, dx_bf, dg = _rms_bwd(tag + "_rmsb", x, g, dh, dy)
    return dx, dx_bf, dg, dw_gate, dw_up, dw_down, got


def _rope_tables(s):
    half = ROPE_DIM // 2
    pos = jnp.arange(s, dtype=F32)
    inv_freq = ROPE_THETA ** (-jnp.arange(0, ROPE_DIM, 2, dtype=F32) / ROPE_DIM)
    ang = pos[:, None] * inv_freq[None, :]
    cos, sin = jnp.cos(ang), jnp.sin(ang)
    rest = HEAD_DIM - ROPE_DIM
    cos_t = jnp.concatenate([cos, cos, jnp.ones((s, rest), F32)], axis=-1)
    sin_t = jnp.concatenate([-sin, sin, jnp.zeros((s, rest), F32)], axis=-1)
    return cos_t, sin_t


def _swap_halves(t):
    lane = lax.broadcasted_iota(jnp.int32, t.shape, 1) & (HEAD_DIM - 1)
    half = ROPE_DIM // 2
    return jnp.where(lane < half, pltpu.roll(t, t.shape[1] - half, 1), pltpu.roll(t, half, 1))


def _dil_bias(blk):
    n_delta = MAX_WINDOW // blk + 1
    delta = jnp.arange(n_delta, dtype=jnp.int32)[:, None, None]
    r = jnp.arange(blk, dtype=jnp.int32)[None, None, :]
    c = jnp.arange(blk, dtype=jnp.int32)[None, :, None]
    o = delta * blk + r - c
    mult = jnp.zeros(o.shape, F32)
    for w, dd in DIL_PATTERNS:
        mult = mult + ((o >= 0) & (o <= w) & (o % dd == 0)).astype(F32)
    return jnp.where(mult > 0, jnp.log(jnp.maximum(mult, 1.0)), NEG)


def _att_block(s):
    return _pick(s, ATT_BLOCK, LANES)


def _fox_aug(name, c_pad, qkv, q_off, k_off, n_heads):
    s = c_pad.shape[0]
    tr = _pick(s, 1024, 16)

    def body(c_ref, q_ref, k_ref, qc_ref, kc_ref):
        h = pl.program_id(1)
        lane = lax.broadcasted_iota(jnp.int32, (tr, LANES), 1)
        ch = jnp.sum(jnp.where(lane == h, c_ref[...], 0.0), axis=1, keepdims=True)
        hi, mid, lo = (t.astype(F32) for t in _split3(ch))
        zero = jnp.zeros((tr, LANES), F32)
        is_hi = jnp.logical_or(lane == 0, lane == 3)
        is_mid = jnp.logical_or(lane == 1, lane == 4)
        parts = jnp.where(is_hi, hi, jnp.where(is_mid, mid, lo))
        qc_ref[:, :HEAD_DIM] = q_ref[...]
        kc_ref[:, :HEAD_DIM] = k_ref[...]
        qc_ref[:, HEAD_DIM:] = jnp.where(lane < 3, 1.0, jnp.where(lane < 6, parts, zero)).astype(BF)
        kc_ref[:, HEAD_DIM:] = jnp.where(lane < 3, -parts, jnp.where(lane < 6, 1.0, zero)).astype(BF)

    head = lambda o: pl.BlockSpec((tr, HEAD_DIM), functools.partial(lambda i, h, o: (i, o + h), o=o))
    spec = pl.BlockSpec((tr, 2 * HEAD_DIM), lambda i, h: (i, h))
    shape = jax.ShapeDtypeStruct((s, n_heads * 2 * HEAD_DIM), BF)
    return pl.pallas_call(
        body, grid=(s // tr, n_heads),
        in_specs=[pl.BlockSpec((tr, LANES), lambda i, h: (i, 0)), head(q_off), head(k_off)],
        out_specs=[spec, spec], out_shape=[shape, shape],
        compiler_params=_params(("parallel", "arbitrary")), name=name)(c_pad, qkv, qkv)


def _causal_mask(st):
    kpos = lax.broadcasted_iota(jnp.int32, st.shape, 0)
    qpos = lax.broadcasted_iota(jnp.int32, st.shape, 1)
    return jnp.where(kpos <= qpos, st, NEG)


def _flash_fwd(name, q_src, k_src, v_src, n_heads, *, fox, tab_t=None):
    (q_arr, q_off, qw), (k_arr, k_off, kw), (v_arr, v_off, _) = q_src, k_src, v_src
    s = q_arr.shape[0]
    blk = _att_block(s)
    nq = s // blk
    n_delta = MAX_WINDOW // blk + 1
    grp = ATT_HEADS
    assert qw == kw and n_heads % grp == 0 and q_off % grp == 0 and k_off % grp == 0 and v_off % grp == 0
    wide = grp * HEAD_DIM

    def body(*refs):
        if fox:
            q_ref, k_ref, v_ref, o_ref, lse_ref, acc, m_s, l_s = refs
        else:
            q_ref, k_ref, v_ref, tab_ref, o_ref, lse_ref, acc, m_s, l_s = refs
        i = pl.program_id(1)
        acc[...] = jnp.zeros_like(acc)
        m_s[...] = jnp.full_like(m_s, NEG)
        l_s[...] = jnp.zeros_like(l_s)

        def step(j, diagonal):
            ks = pl.ds(pl.multiple_of(j * blk, blk), blk)
            for g in range(grp):
                cols = slice(g * HEAD_DIM, (g + 1) * HEAD_DIM)
                qk_cols = slice(g * qw, (g + 1) * qw)
                st = lax.dot_general(k_ref[ks, qk_cols], q_ref[:, qk_cols], _DIMS["nt"], preferred_element_type=F32)
                if fox:
                    if diagonal:
                        st = _causal_mask(st)
                else:
                    st = st + tab_ref[i - j]
                m_prev = m_s[g]
                m_new = jnp.maximum(m_prev, jnp.max(st, axis=0, keepdims=True))
                alpha = jnp.exp(m_prev - m_new)
                p = jnp.exp(st - m_new)
                l_s[g] = alpha * l_s[g] + jnp.sum(p, axis=0, keepdims=True)
                acc[g] = alpha * acc[g] + lax.dot_general(v_ref[ks, cols], p.astype(BF), _DIMS["tn"],
                                                          preferred_element_type=F32)
                m_s[g] = m_new

        def loop_step(j, carry):
            step(j, False)
            return carry

        if fox:
            lax.fori_loop(0, i, loop_step, 0)
            step(i, True)
        else:
            lax.fori_loop(jnp.maximum(i - (n_delta - 1), 0), i + 1, loop_step, 0)
        for g in range(grp):
            o_ref[:, g * HEAD_DIM:(g + 1) * HEAD_DIM] = (acc[g] / l_s[g]).T.astype(o_ref.dtype)
            lse_ref[g] = m_s[g] + jnp.log(l_s[g])

    off = lambda o: functools.partial(lambda h, i, o: (0, o + h), o=o // grp)
    in_specs = [pl.BlockSpec((blk, grp * qw), functools.partial(lambda h, i, o: (i, o + h), o=q_off // grp)),
                pl.BlockSpec((s, grp * kw), off(k_off)), pl.BlockSpec((s, wide), off(v_off))]
    args = [q_arr, k_arr, v_arr]
    if not fox:
        in_specs.append(pl.BlockSpec((n_delta, blk, blk), lambda h, i: (0, 0, 0)))
        args.append(tab_t)
    return pl.pallas_call(
        body, grid=(n_heads // grp, nq), in_specs=in_specs,
        out_specs=[pl.BlockSpec((blk, wide), lambda h, i: (i, h)),
                   pl.BlockSpec((grp, None, 1, blk), lambda h, i: (h, i, 0, 0))],
        out_shape=[jax.ShapeDtypeStruct((s, n_heads * HEAD_DIM), BF),
                   jax.ShapeDtypeStruct((n_heads, nq, 1, blk), F32)],
        scratch_shapes=[pltpu.VMEM((grp, HEAD_DIM, blk), F32), pltpu.VMEM((grp, 1, blk), F32),
                        pltpu.VMEM((grp, 1, blk), F32)],
        compiler_params=_params(("parallel", "parallel")), name=name)(*args)


def _att_delta(name, do, o, n_heads):
    s = do.shape[0]
    blk = _pick(s, 1024, 16)

    def body(do_ref, o_ref, d_ref):
        d_ref[...] = jnp.sum(do_ref[...].astype(F32) * o_ref[...].astype(F32), axis=-1, keepdims=True)

    spec = pl.BlockSpec((blk, HEAD_DIM), lambda h, i: (i, h))
    return pl.pallas_call(
        body, grid=(n_heads, s // blk), in_specs=[spec, spec],
        out_specs=pl.BlockSpec((None, blk, 1), lambda h, i: (h, i, 0)),
        out_shape=jax.ShapeDtypeStruct((n_heads, s, 1), F32),
        compiler_params=_params(("parallel", "parallel")), name=name)(do, o)


def _flash_bwd(name, q_src, k_src, v_src, do, lse_row, delta_row, n_heads, *, fox, tab_t=None):
    (q_arr, q_off, qw), (k_arr, k_off, kw), (v_arr, v_off, _) = q_src, k_src, v_src
    s = q_arr.shape[0]
    blk = _att_block(s)
    nq = s // blk
    n_delta = MAX_WINDOW // blk + 1
    grp = ATT_HEADS
    assert qw == kw and n_heads % grp == 0 and q_off % grp == 0 and k_off % grp == 0 and v_off % grp == 0
    wide = grp * HEAD_DIM

    def body(*refs):
        if fox:
            (q_ref, do_ref, k_ref, v_ref, lse_ref, dl_ref,
             dq_ref, dk_ref, dv_ref, dc_ref, dcq_ref, dk_acc, dv_acc, dc_acc) = refs
        else:
            (q_ref, do_ref, k_ref, v_ref, lse_ref, dl_ref, tab_ref,
             dq_ref, dk_ref, dv_ref, dk_acc, dv_acc) = refs
        j = pl.program_id(1)

        @pl.when(j == 0)
        def _():
            dq_ref[...] = jnp.zeros_like(dq_ref)
            if fox:
                dcq_ref[...] = jnp.zeros_like(dcq_ref)

        dk_acc[...] = jnp.zeros_like(dk_acc)
        dv_acc[...] = jnp.zeros_like(dv_acc)
        if fox:
            dc_acc[...] = jnp.zeros_like(dc_acc)

        def step(i, diagonal):
            qs = pl.ds(pl.multiple_of(i * blk, blk), blk)
            for g in range(grp):
                cols = slice(g * HEAD_DIM, (g + 1) * HEAD_DIM)
                qk_cols = slice(g * qw, (g + 1) * qw)
                plain = slice(g * qw, g * qw + HEAD_DIM)
                kb, vb = k_ref[:, plain], v_ref[:, cols]
                qb, dob = q_ref[qs, plain], do_ref[qs, cols]
                st = lax.dot_general(k_ref[:, qk_cols], q_ref[qs, qk_cols], _DIMS["nt"], preferred_element_type=F32)
                if fox:
                    if diagonal:
                        st = _causal_mask(st)
                else:
                    st = st + tab_ref[i - j]
                pt = jnp.exp(st - lse_ref[g, i])
                dv_acc[:, cols] += jnp.dot(pt.astype(BF), dob, preferred_element_type=F32)
                dpt = lax.dot_general(vb, dob, _DIMS["nt"], preferred_element_type=F32)
                dst = pt * (dpt - dl_ref[g, i])
                dsb = dst.astype(BF)
                dk_acc[:, cols] += jnp.dot(dsb, qb, preferred_element_type=F32)
                dq_ref[qs, cols] += lax.dot_general(dsb, kb, _DIMS["tn"], preferred_element_type=F32)
                if fox:
                    folded = dst[:, :LANES]
                    for part in range(1, blk // LANES):
                        folded = folded + dst[:, part * LANES:(part + 1) * LANES]
                    dc_acc[g] -= folded
                    dcq_ref[g, i] += jnp.sum(dst, axis=0, keepdims=True)

        def loop_step(i, carry):
            step(i, False)
            return carry

        if fox:
            step(j, True)
            lax.fori_loop(j + 1, nq, loop_step, 0)
        else:
            lax.fori_loop(j, jnp.minimum(nq, j + n_delta), loop_step, 0)
        dk_ref[...] = dk_acc[...]
        dv_ref[...] = dv_acc[...].astype(dv_ref.dtype)
        if fox:
            for g in range(grp):
                dc_ref[g] = jnp.sum(dc_acc[g], axis=-1, keepdims=True)

    full = lambda o, wd=wide: pl.BlockSpec((s, wd), functools.partial(lambda h, j, o: (0, o + h), o=o // grp))
    tile = lambda o, wd=wide: pl.BlockSpec((blk, wd), functools.partial(lambda h, j, o: (j, o + h), o=o // grp))
    per_q = pl.BlockSpec((grp, nq, 1, blk), lambda h, j: (h, 0, 0, 0))
    per_k = pl.BlockSpec((grp, blk, 1), lambda h, j: (h, j, 0))
    in_specs = [full(q_off, grp * qw), full(0), tile(k_off, grp * kw), tile(v_off), per_q, per_q]
    args = [q_arr, do, k_arr, v_arr, lse_row, delta_row]
    out_specs = [full(0), tile(0), tile(0)]
    hd = n_heads * HEAD_DIM
    out_shape = [jax.ShapeDtypeStruct((s, hd), F32), jax.ShapeDtypeStruct((s, hd), F32),
                 jax.ShapeDtypeStruct((s, hd), BF)]
    scratch = [pltpu.VMEM((blk, wide), F32), pltpu.VMEM((blk, wide), F32)]
    if fox:
        out_specs += [per_k, per_q]
        out_shape += [jax.ShapeDtypeStruct((n_heads, s, 1), F32), jax.ShapeDtypeStruct((n_heads, nq, 1, blk), F32)]
        scratch.append(pltpu.VMEM((grp, blk, LANES), F32))
    else:
        in_specs.append(pl.BlockSpec((n_delta, blk, blk), lambda h, j: (0, 0, 0)))
        args.append(tab_t)
    return pl.pallas_call(
        body, grid=(n_heads // grp, nq), in_specs=in_specs, out_specs=out_specs, out_shape=out_shape,
        scratch_shapes=scratch, compiler_params=_params(("parallel", "arbitrary")), name=name)(*args)


def _split3(t):
    hi = t.astype(BF)
    r1 = t - hi.astype(F32)
    mid = r1.astype(BF)
    lo = (r1 - mid.astype(F32)).astype(BF)
    return hi, mid, lo


def _tri_dot(tri, t):
    hi, mid, lo = _split3(t)
    return (jnp.dot(tri, hi, preferred_element_type=F32) + jnp.dot(tri, mid, preferred_element_type=F32)
            + jnp.dot(tri, lo, preferred_element_type=F32))


def _log_sigmoid(z):
    return jnp.minimum(z, 0.0) - jnp.log(1.0 + jnp.exp(-jnp.abs(z)))


def _forget_cumsum(name, proj, f_col, bias):
    s = proj.shape[0]
    blk = _att_block(s)

    def body(f_ref, b_ref, c_ref, carry):
        @pl.when(pl.program_id(0) == 0)
        def _():
            carry[...] = jnp.zeros_like(carry)

        lf = _log_sigmoid(f_ref[...] + b_ref[...])
        r = lax.broadcasted_iota(jnp.int32, (blk, blk), 0)
        c = lax.broadcasted_iota(jnp.int32, (blk, blk), 1)
        tri = (c <= r).astype(BF)
        c_ref[...] = _tri_dot(tri, lf) + carry[...]
        carry[...] = c_ref[pl.ds(blk - 1, 1), :]

    return pl.pallas_call(
        body, grid=(s // blk,),
        in_specs=[pl.BlockSpec((blk, LANES), lambda i: (i, f_col)), pl.BlockSpec((1, LANES), lambda i: (0, 0))],
        out_specs=pl.BlockSpec((blk, LANES), lambda i: (i, 0)), out_shape=jax.ShapeDtypeStruct((s, LANES), F32),
        scratch_shapes=[pltpu.VMEM((1, LANES), F32)],
        compiler_params=_params(("arbitrary",)), name=name)(proj, bias)


def _forget_bwd(name, proj, f_col, bias, dc):
    s = proj.shape[0]
    blk = _att_block(s)
    nb = s // blk

    def body(f_ref, b_ref, dc_ref, df_ref, db_ref, carry):
        @pl.when(pl.program_id(0) == 0)
        def _():
            carry[...] = jnp.zeros_like(carry)
            db_ref[...] = jnp.zeros_like(db_ref)

        r = lax.broadcasted_iota(jnp.int32, (blk, blk), 0)
        c = lax.broadcasted_iota(jnp.int32, (blk, blk), 1)
        tri = (c >= r).astype(BF)
        r = lax.broadcasted_iota(jnp.int32, (blk, LANES), 0)
        dlf = _tri_dot(tri, dc_ref[...]) + carry[...]
        carry[...] = jnp.sum(jnp.where(r == 0, dlf, 0.0), axis=0, keepdims=True)
        dz = dlf * _sigmoid(-(f_ref[...] + b_ref[...]))
        df_ref[...] = dz.astype(BF)
        db_ref[...] += _fold8(dz)

    rev = lambda i: (nb - 1 - i, 0)
    return pl.pallas_call(
        body, grid=(nb,),
        in_specs=[pl.BlockSpec((blk, LANES), lambda i: (nb - 1 - i, f_col)), pl.BlockSpec((1, LANES), lambda i: (0, 0)),
                  pl.BlockSpec((blk, LANES), rev)],
        out_specs=[pl.BlockSpec((blk, LANES), rev), pl.BlockSpec((8, LANES), lambda i: (0, 0))],
        out_shape=[jax.ShapeDtypeStruct((s, LANES), BF), jax.ShapeDtypeStruct((8, LANES), F32)],
        scratch_shapes=[pltpu.VMEM((1, LANES), F32)],
        compiler_params=_params(("arbitrary",)), name=name)(proj, bias, dc)


def _dproj_assemble(name, parts, gates, cos_t, sin_t, n_heads, d_model):
    s = cos_t.shape[0]
    tr = _pick(s, 1024, 16)
    scale = HEAD_DIM ** -0.5
    hd = n_heads * HEAD_DIM
    cw = _pick(math.gcd(hd, d_model), PREP_COLS, HEAD_DIM)
    widths = [hd] * 6 + [d_model] * 2
    starts = [sum(widths[:t]) // cw for t in range(len(widths) + 1)]
    cos_w, sin_w = jnp.tile(cos_t, (1, cw // HEAD_DIM)), jnp.tile(sin_t, (1, cw // HEAD_DIM))

    def body(*refs):
        p_refs, cos_ref, sin_ref, o_ref = refs[:8], refs[8], refs[9], refs[10]
        j = pl.program_id(1)
        for kind in range(8):
            @pl.when(jnp.logical_and(j >= starts[kind], j < starts[kind + 1]))
            def _(kind=kind):
                t = p_refs[kind][...]
                if kind in (0, 1, 3):
                    t = t.astype(F32)
                if kind in (0, 3):
                    t = t * scale
                if kind in (0, 1):
                    t = t * cos_ref[...] - _swap_halves(t) * sin_ref[...]
                o_ref[...] = t.astype(BF)

    def part_spec(kind):
        return pl.BlockSpec((tr, cw), functools.partial(
            lambda i, j, kind: (i, jnp.clip(j - starts[kind], 0, widths[kind] // cw - 1)), kind=kind))

    tab = pl.BlockSpec((tr, cw), lambda i, j: (i, 0))
    return pl.pallas_call(
        body, grid=(s // tr, starts[-1]), in_specs=[part_spec(kind) for kind in range(8)] + [tab, tab],
        out_specs=pl.BlockSpec((tr, cw), lambda i, j: (i, j)),
        out_shape=jax.ShapeDtypeStruct((s, sum(widths)), BF),
        compiler_params=_params(("parallel", "arbitrary")), name=name)(*parts, *gates, cos_w, sin_w)


FFN2 =("ffn2_w_gate", "ffn2_w_up", "ffn2_w_down")


def _device_step(x, tgt, shards, small):
    s, d = x.shape
    hd = shards["w_proj_dil"].shape[0]
    hh = hd // HEAD_DIM
    blk = _att_block(s)
    gate_off = 6 * hd
    f_col = 0
    n_proj = gate_off + 2 * d
    ident = lambda accs, ex, rw: (accs[0],)
    chip = 2 * lax.axis_index("x") + lax.axis_index("y")
    ids = jnp.stack([chip, lax.axis_index("c")]).astype(jnp.int32)
    w = {}

    def take_gathered(names, gathered):
        for n, t in zip(names, gathered):
            t = lax.dynamic_update_index_in_dim(t, shards[n], chip, 0)
            if n == "w_in":
                w["w_in"], w["w_in_f"] = _repack_w_in(t, hd, d)
            else:
                w[n] = _full_from_pieces(n, t)

    def chip_sums(names, pieces, from_sibling):
        return [_rs_add("rs_add_" + n, ids, g, o) for n, g, o in zip(names, pieces, from_sibling)]

    def core_halves(names, sums, from_chips):
        return [_rs_sum("rs_sum_" + n, ids, own, got) for n, own, got in zip(names, sums, from_chips)]

    def gather(names):
        return _gather_side([shards[n] for n in names])

    under_heads = ("w_proj_dil", "w_proj_fox", "w_out", "ffn2_w_gate")
    take_gathered(["ffn1_w_gate"], gather(["ffn1_w_gate"]).call("gather_first"))

    def arrived(key, names, name):
        def take(got):
            take_gathered(names, got[key])
            return w[name]
        return take

    x1, saved1, got = _ffn_fwd_split(
        "ffn1", x, small["ffn1_norm"], w["ffn1_w_gate"], arrived("gate", ["ffn1_w_up"], "ffn1_w_up"),
        arrived("up", ["ffn1_w_down"], "ffn1_w_down"),
        sides={"gate": lambda got: gather(["ffn1_w_up"]), "up": lambda got: gather(["ffn1_w_down"]),
               "down": lambda got: gather(["w_in"])})
    take_gathered(["w_in"], got["down"])

    hm = _rms_fwd("mix_rms", x1, small["mix_norm"])
    cos_t, sin_t = _rope_tables(s)
    head_tile = _pick(hd, 1024, HEAD_DIM)
    tiles_per_kind = hd // head_tile
    scale = HEAD_DIM ** -0.5

    def heads_epi(accs, ex, rw):
        kind = pl.program_id(1) // tiles_per_kind
        t = accs[0]
        r = t * ex[0] + _swap_halves(t) * ex[1]
        t = jnp.where(kind < 2, r, t)
        return (jnp.where(jnp.logical_or(kind == 0, kind == 3), t * scale, t),)

    wide = lambda tab: jnp.tile(tab, (1, head_tile // HEAD_DIM))
    (qkv,), gathered = _mm("proj_heads", [(hm, w["w_in"])], "nn", s, gate_off, d, epilogue=heads_epi, out_dtypes=(BF,),
                           extras=[(wide(cos_t), None), (wide(sin_t), None)], tn=head_tile, keep_tn=True,
                           side=gather(under_heads))
    take_gathered(under_heads, gathered)
    (gates,), gathered = _mm("proj_gates", [(hm, w["w_in"])], "nn", s, 2 * d, d, epilogue=ident, out_dtypes=(F32,),
                             b_off=gate_off, side=gather(["ffn2_w_up"]))
    take_gathered(["ffn2_w_up"], gathered)
    (f_logit,) = _mm("proj_f", [(hm, w["w_in_f"])], "nn", s, LANES, d, epilogue=ident, out_dtypes=(F32,))
    tab_t = _dil_bias(blk)
    dil_src = ((qkv, 0, HEAD_DIM), (qkv, hh, HEAD_DIM), (qkv, 2 * hh, HEAD_DIM))
    y_dil, lse_d = _flash_fwd("dil_fwd", *dil_src, hh, fox=False, tab_t=tab_t)
    bias_f = jnp.pad(small["b_forget"], ((0, 0), (0, LANES - hh)))
    c_pad = _forget_cumsum("forget_cumsum", f_logit, f_col, bias_f)
    q_cat, k_cat = _fox_aug("fox_aug", c_pad, qkv, 3 * hh, 4 * hh, hh)
    fox_src = ((q_cat, 0, 2 * HEAD_DIM), (k_cat, 0, 2 * HEAD_DIM), (qkv, 5 * hh, HEAD_DIM))
    y_fox, lse_f = _flash_fwd("fox_fwd", *fox_src, hh, fox=True)

    def merge_epi(accs, ex, rw):
        ud, uf = accs
        return ud, uf, _sigmoid(ex[0] + rw[0]) * ud + _sigmoid(ex[1] + rw[1]) * uf

    u_d, u_f, merged = _mm("merge", [(y_dil, w["w_proj_dil"]), (y_fox, w["w_proj_fox"])], "nn", s, d, hd,
                           epilogue=merge_epi, out_dtypes=(BF, BF, BF),
                           extras=[(gates, 0), (gates, d)],
                           rows=[small["b_gate_dil"], small["b_gate_fox"]])
    (x2,) = _mm("mix_out", [(merged, w["w_out"])], "nn", s, d, d,
                epilogue=lambda accs, ex, rw: (ex[0] + accs[0],), out_dtypes=(F32,), extras=[(x1, 0)])

    x3, saved2, _ = _ffn_fwd("ffn2", x2, small["ffn2_norm"], w["ffn2_w_gate"], w["ffn2_w_up"],
                             arrived("up", ["ffn2_w_down"], "ffn2_w_down"),
                             sides={"up": lambda got: gather(["ffn2_w_down"])})
    dx3, dx3_bf, dg_final, sq = _final("final", x3, small["final_norm"].reshape(1, d), tgt)

    dx2, dx2_bf, dg_ffn2, dw_g2, dw_u2, dw_d2, _ = _ffn_bwd("ffn2", x2, small["ffn2_norm"], w["ffn2_w_gate"],
                                                            w["ffn2_w_up"], w["ffn2_w_down"], saved2, dx3, dx3_bf)
    pieces2 = [dw_g2, dw_u2, dw_d2]

    def dmerge_epi(accs, ex, rw):
        dm = accs[0]
        gd, gf, ud, uf = ex[0], ex[1], ex[2].astype(F32), ex[3].astype(F32)
        sd, sf = _sigmoid(gd + rw[0]), _sigmoid(gf + rw[1])
        dgd = dm * ud * (sd * (1.0 - sd))
        dgf = dm * uf * (sf * (1.0 - sf))
        return (dm * sd, dm * sf, dgd, dgf, jnp.sum(dgd, axis=0, keepdims=True), jnp.sum(dgf, axis=0, keepdims=True))

    (du_d, du_f, dg_d, dg_f, dbg_d, dbg_f), from_sibling = _mm(
        "dmerge", [(dx2_bf, w["w_out"])], "nt", s, d, d, epilogue=dmerge_epi, out_dtypes=(BF, BF, BF, BF), n_colsum=2,
        extras=[(gates, 0), (gates, d), (u_d, 0), (u_f, 0)],
        rows=[small["b_gate_dil"], small["b_gate_fox"]], side=_swap_side(pieces2))
    sums2 = chip_sums(FFN2, pieces2, from_sibling)
    (dw_out,) = _mm("dw_out", [(merged, dx2_bf)], "tn", d, d, s, epilogue=ident, out_dtypes=(F32,))
    dw_out = dw_out.reshape(N_CHIPS, d // N_CHIPS, d)
    dw_pd = _dw_col_pieces("dw_pd", y_dil, du_d, hd, d, s)
    dw_pf = _dw_col_pieces("dw_pf", y_fox, du_f, hd, d, s)
    (dy_dil,) = _mm("dy_dil", [(du_d, w["w_proj_dil"])], "nt", s, hd, d, epilogue=ident, out_dtypes=(BF,))
    (dy_fox,) = _mm("dy_fox", [(du_f, w["w_proj_fox"])], "nt", s, hd, d, epilogue=ident, out_dtypes=(BF,))

    row = lambda t: t.reshape(hh, s // blk, 1, blk)
    delta_d = _att_delta("dil_delta", dy_dil, y_dil, hh)
    dq_d, dk_d, dv_d = _flash_bwd("dil_bwd", *dil_src, dy_dil, lse_d, row(delta_d), hh, fox=False, tab_t=tab_t)
    delta_f = _att_delta("fox_delta", dy_fox, y_fox, hh)
    dq_f, dk_f, dv_f, dc_k, dc_q = _flash_bwd("fox_bwd", *fox_src, dy_fox, lse_f, row(delta_f), hh, fox=True)
    dc = dc_k.reshape(hh, s) + dc_q.reshape(hh, s)
    dc_pad = jnp.pad(dc.T, ((0, 0), (0, LANES - hh)))
    df, db_forget = _forget_bwd("forget_bwd", f_logit, f_col, bias_f, dc_pad)
    dproj = _dproj_assemble("dproj", [dq_d, dk_d, dv_d, dq_f, dk_f, dv_f], [dg_d, dg_f], cos_t, sin_t, hh, d)
    (dhm_f,) = _mm("dhm_f", [(df, w["w_in_f"])], "nt", s, d, LANES, epilogue=ident, out_dtypes=(F32,))
    (dhm,), from_chips = _mm("dhm", [(dproj, w["w_in"])], "nt", s, d, n_proj,
                             epilogue=lambda accs, ex, rw: (accs[0] + ex[0],), out_dtypes=(F32,),
                             extras=[(dhm_f, 0)], side=_scatter_side(sums2))
    halves2 = core_halves(FFN2, sums2, from_chips)
    small_mixer = ("w_proj_dil", "w_proj_fox", "w_out")
    pieces_sm = [dw_pd, dw_pf, dw_out]
    (dw_in,), from_sibling = _mm("dw_in", [(hm, dproj)], "tn", d, n_proj, s, epilogue=ident, out_dtypes=(F32,),
                                 side=_swap_side(pieces_sm))
    sums_sm = chip_sums(small_mixer, pieces_sm, from_sibling)
    (dw_in_f,) = _mm("dw_in_f", [(hm, df)], "tn", d, LANES, s, epilogue=ident, out_dtypes=(F32,))
    dw_in = _dw_in_pieces(dw_in, dw_in_f, hd, d)
    dx1, dx1_bf, dg_mix = _rms_bwd("mix_rmsb", x1, small["mix_norm"], dhm, dx2)

    sums = {}

    def under_dwd(got):
        sums["w_in"] = chip_sums(["w_in"], [dw_in], got["dact"])
        return _scatter_side(sums_sm)

    def under_dwu(got):
        return _swap_side([got["dw_down"], got["dw_gate"]])

    def under_dh(got):
        sums["dg"] = chip_sums(["ffn1_w_down", "ffn1_w_gate"], [got["dw_down"], got["dw_gate"]], got["dwu"])
        return _scatter_side(sums["dg"]).beside(_swap_side([got["dw_up"]]))

    dx0, _, dg_ffn1, dw_g1, dw_u1, dw_d1, got = _ffn_bwd(
        "ffn1", x, small["ffn1_norm"], w["ffn1_w_gate"], w["ffn1_w_up"], w["ffn1_w_down"], saved1, dx1, dx1_bf,
        sides={"dact": lambda got: _swap_side([dw_in]), "dwd": under_dwd,
               "dwg": lambda got: _scatter_side(sums["w_in"]), "dwu": under_dwu, "dh": under_dh})
    halves_sm = core_halves(small_mixer, sums_sm, got["dwd"])
    halves_in = core_halves(["w_in"], sums["w_in"], got["dwg"])
    halves_dg = core_halves(["ffn1_w_down", "ffn1_w_gate"], sums["dg"], got["dh"][:2])
    sums_u = chip_sums(["ffn1_w_up"], [dw_u1], got["dh"][2:])
    halves_u = core_halves(["ffn1_w_up"], sums_u, _scatter_side(sums_u).call("rs_scatter_last"))

    names = ("ffn1_w_down", "ffn1_w_gate", "ffn1_w_up", "w_in") + small_mixer + FFN2
    totals = _join_side(halves_dg + halves_u + halves_in + halves_sm + halves2).call("rs_join")
    grads = dict(zip(names, totals))
    partials = {"ffn1_norm": dg_ffn1, "mix_norm": dg_mix, "ffn2_norm": dg_ffn2, "final_norm": dg_final,
                "b_gate_dil": dbg_d.reshape(-1, d), "b_gate_fox": dbg_f.reshape(-1, d), "b_forget": db_forget, "sq": sq}
    return dx0, grads, partials


def _coords():
    return lax.axis_index("x"), lax.axis_index("y"), lax.axis_index("c")


def _other_chips(x, y):
    return [(1 - x, y), (x, 1 - y), (1 - x, 1 - y)]


ANY_SPEC = pl.BlockSpec(memory_space=pl.ANY)


def _gather_side(shards):
    nw = len(shards)

    def copies(srcs, outs, send_sems, recv_sems):
        x, y, c = _coords()
        chips = _other_chips(x, y)

        def slot(w, px, py, pc):
            half = shards[w].shape[0] // 2
            return outs[w].at[2 * px + py, pl.ds(pc * half, half), :]

        def copy(w, k, src_ref, dst_ref, to):
            return pltpu.make_async_remote_copy(src_ref=src_ref, dst_ref=dst_ref, send_sem=send_sems.at[6 * w + k],
                                                recv_sem=recv_sems.at[6 * w + k], device_id=to, device_id_type=MESH)

        first, arrive, passed, arrive2 = [], [], [], []
        for w in range(nw):
            half = shards[w].shape[0] // 2
            for j, chip in enumerate(chips):
                first.append(copy(w, j, srcs[w].at[pl.ds(c * half, half), :], slot(w, x, y, c), (*chip, c)))
                arrive.append(copy(w, j, slot(w, *chip, c), slot(w, *chip, c), (*chip, c)))
                passed.append(copy(w, 3 + j, slot(w, *chip, c), slot(w, *chip, c), (x, y, 1 - c)))
                arrive2.append(copy(w, 3 + j, slot(w, *chip, 1 - c), slot(w, *chip, 1 - c), (x, y, 1 - c)))
        return first, arrive, passed, arrive2

    def start(srcs, outs, send_sems, recv_sems):
        for cp in copies(srcs, outs, send_sems, recv_sems)[0]:
            cp.start()

    def finish(srcs, outs, send_sems, recv_sems):
        first, arrive, passed, arrive2 = copies(srcs, outs, send_sems, recv_sems)
        for got, fwd in zip(arrive, passed):
            got.wait_recv()
            fwd.start()
        for got in arrive2:
            got.wait_recv()
        for cp in first + passed:
            cp.wait_send()

    return _Side(shards, [jax.ShapeDtypeStruct((N_CHIPS, *t.shape), t.dtype) for t in shards], 6 * nw, start, finish)


def _swap_side(grads):
    nw = len(grads)

    def copies(srcs, outs, send_sems, recv_sems):
        x, y, c = _coords()
        res = []
        for w in range(nw):
            half = grads[w].shape[1] // 2
            for p in range(N_CHIPS):
                k = N_CHIPS * w + p
                res.append(pltpu.make_async_remote_copy(
                    src_ref=srcs[w].at[p, pl.ds((1 - c) * half, half), :], dst_ref=outs[w].at[p],
                    send_sem=send_sems.at[k], recv_sem=recv_sems.at[k], device_id=(x, y, 1 - c), device_id_type=MESH))
        return res

    def start(*refs):
        for cp in copies(*refs):
            cp.start()

    def finish(*refs):
        for cp in copies(*refs):
            cp.wait()

    shapes = [jax.ShapeDtypeStruct((N_CHIPS, t.shape[1] // 2, t.shape[2]), t.dtype) for t in grads]
    return _Side(grads, shapes, N_CHIPS * nw, start, finish)


def _rs_add(name, ids, g, other):
    n, rows, cols = g.shape
    half = rows // 2
    tr = _pick(half, 256, 16)
    nb = half // tr

    def body(ids_ref, g_ref, o_ref, out_ref):
        out_ref[...] = (g_ref[...] + o_ref[...]).astype(BF)

    grid_spec = pltpu.PrefetchScalarGridSpec(
        num_scalar_prefetch=1, grid=(n, nb),
        in_specs=[pl.BlockSpec((None, tr, cols), lambda p, i, ids_ref: (p, ids_ref[1] * nb + i, 0)),
                  pl.BlockSpec((None, tr, cols), lambda p, i, ids_ref: (p, i, 0))],
        out_specs=pl.BlockSpec((None, tr, cols), lambda p, i, ids_ref: (p, i, 0)))
    return pl.pallas_call(body, grid_spec=grid_spec, out_shape=jax.ShapeDtypeStruct((n, half, cols), BF),
                          compiler_params=_params(("parallel", "parallel")), name=name)(ids, g, other)


def _scatter_side(sums):
    nw = len(sums)

    def copies(srcs, outs, send_sems, recv_sems):
        x, y, c = _coords()
        res = []
        for w in range(nw):
            for k, (px, py) in enumerate(_other_chips(x, y)):
                res.append(pltpu.make_async_remote_copy(
                    src_ref=srcs[w].at[2 * px + py], dst_ref=outs[w].at[k], send_sem=send_sems.at[3 * w + k],
                    recv_sem=recv_sems.at[3 * w + k], device_id=(px, py, c), device_id_type=MESH))
        return res

    def start(*refs):
        for cp in copies(*refs):
            cp.start()

    def finish(*refs):
        for cp in copies(*refs):
            cp.wait()

    return _Side(sums, [jax.ShapeDtypeStruct((3, *t.shape[1:]), t.dtype) for t in sums], 3 * nw, start, finish)


def _rs_sum(name, ids, own, got):
    n, half, cols = own.shape
    tr = _pick(half, 256, 16)
    nb = half // tr

    def body(ids_ref, own_ref, got_ref, out_ref):
        t = own_ref[...].astype(F32)
        for k in range(3):
            t = t + got_ref[k].astype(F32)
        out_ref[...] = t

    grid_spec = pltpu.PrefetchScalarGridSpec(
        num_scalar_prefetch=1, grid=(nb,),
        in_specs=[pl.BlockSpec((None, tr, cols), lambda i, ids_ref: (ids_ref[0], i, 0)),
                  pl.BlockSpec((3, tr, cols), lambda i, ids_ref: (0, i, 0))],
        out_specs=pl.BlockSpec((tr, cols), lambda i, ids_ref: (ids_ref[1] * nb + i, 0)))
    return pl.pallas_call(body, grid_spec=grid_spec, out_shape=jax.ShapeDtypeStruct((2 * half, cols), F32),
                          compiler_params=_params(("parallel",)), name=name)(ids, own, got)


def _join_side(totals):
    nw = len(totals)

    def start(ins, bufs, send_sems, recv_sems):
        x, y, c = _coords()
        for w in range(nw):
            half = totals[w].shape[0] // 2
            pltpu.make_async_remote_copy(
                src_ref=bufs[w].at[pl.ds(c * half, half), :], dst_ref=bufs[w].at[pl.ds(c * half, half), :],
                send_sem=send_sems.at[w], recv_sem=recv_sems.at[w], device_id=(x, y, 1 - c),
                device_id_type=MESH).start()

    def finish(ins, bufs, send_sems, recv_sems):
        x, y, c = _coords()
        for w in range(nw):
            half = totals[w].shape[0] // 2
            arrival = pltpu.make_async_remote_copy(
                src_ref=bufs[w].at[pl.ds(c * half, half), :], dst_ref=bufs[w].at[pl.ds((1 - c) * half, half), :],
                send_sem=send_sems.at[w], recv_sem=recv_sems.at[w], device_id=(x, y, 1 - c), device_id_type=MESH)
            arrival.wait_recv()
            arrival.wait_send()

    return _Side(totals, [jax.ShapeDtypeStruct(t.shape, t.dtype) for t in totals], nw, start, finish,
                 aliases={w: w for w in range(nw)})


def _gather_all(name, t):
    rows, cols = t.shape

    def body(src, out, send_sems, recv_sems, local_sem):
        x, y, c = _coords()
        me = 4 * x + 2 * y + c
        mine = pltpu.make_async_copy(src, out.at[me], local_sem)
        mine.start()
        peers = [(x ^ (k >> 2 & 1), y ^ (k >> 1 & 1), c ^ (k & 1)) for k in range(1, N_DEV)]
        sends = [pltpu.make_async_remote_copy(src_ref=src, dst_ref=out.at[me], send_sem=send_sems.at[k],
                                              recv_sem=recv_sems.at[k], device_id=peer, device_id_type=MESH)
                 for k, peer in enumerate(peers)]
        for cp in sends:
            cp.start()
        for k, (px, py, pc) in enumerate(peers):
            pltpu.make_async_remote_copy(src_ref=src, dst_ref=out.at[4 * px + 2 * py + pc], send_sem=send_sems.at[k],
                                         recv_sem=recv_sems.at[k], device_id=(px, py, pc),
                                         device_id_type=MESH).wait_recv()
        for cp in sends:
            cp.wait_send()
        mine.wait()

    vmem = pl.BlockSpec(memory_space=pltpu.VMEM)
    return pl.pallas_call(
        body, in_specs=[vmem], out_specs=vmem, out_shape=jax.ShapeDtypeStruct((N_DEV, rows, cols), t.dtype),
        scratch_shapes=[pltpu.SemaphoreType.DMA((7,)), pltpu.SemaphoreType.DMA((7,)), pltpu.SemaphoreType.DMA],
        name=name)(t)


def _adamw_math(w, g, m, v):
    m = ADAM_B1 * m + (1.0 - ADAM_B1) * g
    v = ADAM_B2 * v + (1.0 - ADAM_B2) * (g * g)
    m_hat = m / (1.0 - ADAM_B1 ** ADAM_STEP)
    v_hat = v / (1.0 - ADAM_B2 ** ADAM_STEP)
    delta = -ADAM_LR * (m_hat / (jnp.sqrt(v_hat) + ADAM_EPS) + ADAM_WD * w)
    return delta, m, v


def _adamw(name, w, g, m, v):
    _, rows, cols = w.shape
    tr = _pick(rows, 256, 8)

    def body(w_ref, g_ref, m_ref, v_ref, g_out, d_out, m_out, v_out):
        g = g_ref[...]
        g_out[...] = g
        d_out[...], m_out[...], v_out[...] = _adamw_math(w_ref[...], g, m_ref[...], v_ref[...])

    blk3 = pl.BlockSpec((None, tr, cols), lambda i: (0, i, 0))
    blk = pl.BlockSpec((tr, cols), lambda i: (i, 0))
    shape = jax.ShapeDtypeStruct((rows, cols), F32)
    return pl.pallas_call(
        body, grid=(rows // tr,), in_specs=[blk3, blk, blk3, blk3], out_specs=[blk] * 4, out_shape=[shape] * 4,
        compiler_params=_params(("parallel",)), name=name)(w, g, m, v)


def _small_reduce(name, parts, width):
    def body(*refs):
        out = refs[-1]
        out[...] = jnp.zeros_like(out)
        for k, r in enumerate(refs[:-1]):
            out[pl.ds(k, 1), :] = jnp.sum(r[...], axis=0, keepdims=True)

    vmem = pl.BlockSpec(memory_space=pltpu.VMEM)
    return pl.pallas_call(body, in_specs=[vmem] * len(parts), out_specs=vmem,
                          out_shape=jax.ShapeDtypeStruct((8, width), F32), name=name)(*parts)


def _small_adamw(name, gathered, w, m, v, loss_row, loss_scale):
    def body(gt_ref, w_ref, m_ref, v_ref, g_out, d_out, m_out, v_out, loss_out):
        g = gt_ref[0]
        for k in range(1, N_DEV):
            g = g + gt_ref[k]
        g_out[...] = g
        row = lax.broadcasted_iota(jnp.int32, g.shape, 0)
        loss_out[...] = jnp.sum(jnp.where(row == loss_row, g, 0.0), keepdims=True) * loss_scale
        d_out[...], m_out[...], v_out[...] = _adamw_math(w_ref[...], g, m_ref[...], v_ref[...])

    vmem = pl.BlockSpec(memory_space=pltpu.VMEM)
    shape = jax.ShapeDtypeStruct(w.shape, F32)
    return pl.pallas_call(body, in_specs=[vmem] * 4, out_specs=[vmem] * 5,
                          out_shape=[shape] * 4 + [jax.ShapeDtypeStruct((1, 1), F32)], name=name)(gathered, w, m, v)


def _full_from_pieces(name, pieces):
    _, rows, cols = pieces.shape
    if name in ROW_SHARDED:
        return pieces.reshape(N_CHIPS * rows, cols)
    return pieces.transpose(1, 0, 2).reshape(rows, N_CHIPS * cols)


def _column_range(segments, lo, hi):
    out, start = [], 0
    for t in segments:
        a, b = max(lo, start), min(hi, start + t.shape[1])
        if a < b:
            out.append(t[:, a - start:b - start])
        start += t.shape[1]
    return out


def _repack_w_in(pieces, hd, d):
    hh = hd // HEAD_DIM
    segs = [pieces[p] for p in range(N_CHIPS)]
    total = N_CHIPS * pieces.shape[2]
    main = jnp.concatenate(_column_range(segs, 0, 6 * hd) + _column_range(segs, 6 * hd + hh, total), axis=1)
    f = jnp.concatenate(_column_range(segs, 6 * hd, 6 * hd + hh), axis=1)
    return main, jnp.pad(f, ((0, 0), (0, LANES - hh)))


def _dw_in_pieces(dw, dw_f, hd, d):
    hh = hd // HEAD_DIM
    segs = [dw[:, :6 * hd], dw_f[:, :hh], dw[:, 6 * hd:]]
    cs = (6 * hd + hh + 2 * d) // N_CHIPS
    return jnp.stack([jnp.concatenate(_column_range(segs, p * cs, (p + 1) * cs), axis=1) for p in range(N_CHIPS)])


def _small_pack(vals, width):
    rows = []
    for name in SMALL:
        t = vals[name].reshape(1, -1)
        rows.append(jnp.pad(t, ((0, 0), (0, width - t.shape[1]))))
    rows.append(jnp.zeros((8 - len(SMALL), width), F32))
    return jnp.concatenate(rows, axis=0)


def kernel(x, ffn1_norm, ffn1_w_gate, ffn1_w_up, ffn1_w_down, mix_norm, w_in, b_forget, b_gate_dil, b_gate_fox, w_proj_dil, w_proj_fox, w_out, ffn2_norm, ffn2_w_gate, ffn2_w_up, ffn2_w_down, final_norm, loss_target, m_ffn1_norm, m_ffn1_w_gate, m_ffn1_w_up, m_ffn1_w_down, m_mix_norm, m_w_in, m_b_forget, m_b_gate_dil, m_b_gate_fox, m_w_proj_dil, m_w_proj_fox, m_w_out, m_ffn2_norm, m_ffn2_w_gate, m_ffn2_w_up, m_ffn2_w_down, m_final_norm, v_ffn1_norm, v_ffn1_w_gate, v_ffn1_w_up, v_ffn1_w_down, v_mix_norm, v_w_in, v_b_forget, v_b_gate_dil, v_b_gate_fox, v_w_proj_dil, v_w_proj_fox, v_w_out, v_ffn2_norm, v_ffn2_w_gate, v_ffn2_w_up, v_ffn2_w_down, v_final_norm):
    given = dict(locals())
    wts = {n: given[n] for n in WEIGHTS}
    mom_m = {n: given["m_" + n] for n in WEIGHTS}
    mom_v = {n: given["v_" + n] for n in WEIGHTS}
    d = x.shape[2]

    shards = {n: wts[n][0].astype(BF) for n in SHARDED}
    small = {n: wts[n] for n in SMALL}
    grad_x, grads, partials = _device_step(x[0], loss_target[0], shards, small)

    out_g, out_d, out_m, out_v = {}, {}, {}, {}
    for n in SHARDED:
        outs = _adamw("adamw_" + n, wts[n], grads[n], mom_m[n], mom_v[n])
        out_g[n], out_d[n], out_m[n], out_v[n] = (t[None] for t in outs)

    width = d
    part_rows = []
    for n in SMALL:
        t = partials[n]
        part_rows.append(jnp.pad(t, ((0, 0), (0, width - t.shape[1]))))
    part_rows.append(partials["sq"])
    local_small = _small_reduce("small_reduce", part_rows, width)
    gathered_small = _gather_all("small_gather", local_small)
    sg, sd_, sm, sv, loss = _small_adamw("small_adamw", gathered_small, _small_pack(wts, width),
                                         _small_pack(mom_m, width), _small_pack(mom_v, width), len(SMALL), 0.5 / d)
    for k, n in enumerate(SMALL):
        shp = wts[n].shape
        take = lambda t: t[k, :shp[-1]].reshape(shp)
        out_g[n], out_d[n], out_m[n], out_v[n] = take(sg), take(sd_), take(sm), take(sv)
    return (loss[0, 0], grad_x[None], *[out_g[n] for n in WEIGHTS], *[out_d[n] for n in WEIGHTS],
            *[out_m[n] for n in WEIGHTS], *[out_v[n] for n in WEIGHTS])
```

```python
import functools
import math

import jax
import jax.numpy as jnp
from jax import lax
from jax.experimental import pallas as pl
from jax.experimental.pallas import tpu as pltpu

HEAD_DIM = 128
ROPE_DIM = HEAD_DIM // 4
ROPE_THETA = 500000.0
DIL_PATTERNS = ((128, 1), (512, 4), (2048, 16))
MAX_WINDOW = 2048
NORM_EPS = 1e-6
ADAM_LR = 0.001
ADAM_B1 = 0.9
ADAM_B2 = 0.999
ADAM_EPS = 1e-08
ADAM_WD = 0.01
ADAM_STEP = 10

BF = jnp.bfloat16
F32 = jnp.float32
NEG = -1e30
LANES = 128
ATT_BLOCK = 512
PREP_COLS = 512
ATT_HEADS = 2
VMEM_LIMIT = 56 * 1024 * 1024
MM_VMEM_BUDGET = 40 * 1024 * 1024
N_CHIPS = 4
N_DEV = 8
MESH = pl.DeviceIdType.MESH

SHARDED = ("ffn1_w_gate", "ffn1_w_up", "ffn1_w_down", "w_in", "w_proj_dil", "w_proj_fox", "w_out",
           "ffn2_w_gate", "ffn2_w_up", "ffn2_w_down")
ROW_SHARDED = ("ffn1_w_down", "w_out", "ffn2_w_down")
SMALL = ("ffn1_norm", "mix_norm", "b_forget", "b_gate_dil", "b_gate_fox", "ffn2_norm", "final_norm")
WEIGHTS = ("ffn1_norm", "ffn1_w_gate", "ffn1_w_up", "ffn1_w_down", "mix_norm", "w_in", "b_forget",
           "b_gate_dil", "b_gate_fox", "w_proj_dil", "w_proj_fox", "w_out", "ffn2_norm", "ffn2_w_gate",
           "ffn2_w_up", "ffn2_w_down", "final_norm")


def _pick(n, target, align):
    best = None
    for d in range(align, min(n, target) + 1, align):
        if n % d == 0:
            best = d
    return n if best is None else best


def _params(sem=None):
    return pltpu.CompilerParams(dimension_semantics=sem, vmem_limit_bytes=VMEM_LIMIT)


_DIMS = {"nn": (((1,), (0,)), ((), ())), "nt": (((1,), (1,)), ((), ())), "tn": (((0,), (0,)), ((), ()))}


class _SemsFrom:
    def __init__(self, sems, first):
        self.sems, self.first = sems, first

    @property
    def at(self):
        return self

    def __getitem__(self, k):
        return self.sems.at[self.first + k]


class _Side:
    def __init__(self, inputs, out_shapes, n_sems, start, finish, aliases=None):
        self.inputs, self.out_shapes, self.n_sems = list(inputs), list(out_shapes), n_sems
        self.start, self.finish, self.aliases = start, finish, aliases or {}

    def scratch(self):
        return [pltpu.SemaphoreType.DMA((self.n_sems,)), pltpu.SemaphoreType.DMA((self.n_sems,))]

    def beside(self, other):
        n_in, n_out, n_sems = len(self.inputs), len(self.out_shapes), self.n_sems

        def part(fn_a, fn_b):
            def run(ins, outs, send_sems, recv_sems):
                fn_a(ins[:n_in], outs[:n_out], send_sems, recv_sems)
                fn_b(ins[n_in:], outs[n_out:], _SemsFrom(send_sems, n_sems), _SemsFrom(recv_sems, n_sems))
            return run

        aliases = {**self.aliases, **{n_in + t: n_out + o for t, o in other.aliases.items()}}
        return _Side(self.inputs + other.inputs, self.out_shapes + other.out_shapes, n_sems + other.n_sems,
                     part(self.start, other.start), part(self.finish, other.finish), aliases=aliases)

    def call(self, name):
        n_in, n_out = len(self.inputs), len(self.out_shapes)

        def body(*refs):
            ins, outs, sems = refs[:n_in], refs[n_in:n_in + n_out], refs[n_in + n_out:]
            self.start(ins, outs, *sems)
            self.finish(ins, outs, *sems)

        return pl.pallas_call(body, in_specs=[ANY_SPEC] * n_in, out_specs=[ANY_SPEC] * n_out, out_shape=self.out_shapes,
                              input_output_aliases=self.aliases, scratch_shapes=self.scratch(), name=name)(*self.inputs)


def _mm(name, pairs, mode, m, n, k, *, epilogue, out_dtypes, extras=(), rows=(), n_colsum=0,
        sum_pairs=False, tm=1024, tn=1152, tk=2048, piece_layout=False, side=None, b_off=0, keep_tn=False):
    m_align = LANES if mode == "tn" else 8
    tm = _pick(m, tm, m_align)
    tn = n // N_CHIPS if piece_layout else _pick(n, tn, LANES)
    tk = _pick(k, tk, LANES)
    n_acc = 1 if sum_pairs else len(pairs)
    lhs = []
    for a, _ in pairs:
        if not any(a is t for t in lhs):
            lhs.append(a)
    lhs_of = [next(t for t in range(len(lhs)) if lhs[t] is a) for a, _ in pairs]
    n_mm = len(lhs) + len(pairs)
    n_in = n_mm + len(extras) + len(rows)
    n_out = len(out_dtypes) + n_colsum

    def vmem_bytes(tm_, tn_, tk_):
        tiles = sum(tm_ * tk_ * a.dtype.itemsize for a in lhs) + sum(tn_ * tk_ * b.dtype.itemsize for _, b in pairs)
        tiles += sum(tm_ * tn_ * arr.dtype.itemsize for arr, _ in extras)
        tiles += sum(tm_ * tn_ * jnp.dtype(dt).itemsize for dt in out_dtypes)
        return 2 * tiles + (n_acc + len(extras) + len(out_dtypes)) * tm_ * tn_ * 4

    while vmem_bytes(tm, tn, tk) > MM_VMEM_BUDGET:
        if tn > 512 and not piece_layout and not keep_tn:
            tn = _pick(n, tn - LANES, LANES)
        elif tm > 512:
            tm = _pick(m, tm - m_align, m_align)
        elif tk > 512:
            tk = _pick(k, tk - LANES, LANES)
        elif tm > 256:
            tm = _pick(m, tm - m_align, m_align)
        else:
            break
    nk = k // tk

    n_side_in = len(side.inputs) if side else 0
    n_side_out = len(side.out_shapes) if side else 0
    grid = (m // tm, n // tn, nk)

    def body(*refs):
        ins, refs = refs[:n_in], refs[n_in:]
        side_ins, refs = refs[:n_side_in], refs[n_side_in:]
        outs, refs = refs[:n_out], refs[n_out:]
        side_outs, refs = refs[:n_side_out], refs[n_side_out:]
        accs, side_sems = refs[:n_acc], refs[n_acc:]
        kk = pl.program_id(2)
        if side:
            at = [pl.program_id(t) for t in range(3)]

            @pl.when(jnp.logical_and(jnp.logical_and(at[0] == 0, at[1] == 0), at[2] == 0))
            def _():
                side.start(side_ins, side_outs, *side_sems)

        @pl.when(kk == 0)
        def _():
            for acc in accs:
                acc[...] = jnp.zeros_like(acc)

        a_tiles = [r[...].astype(BF) for r in ins[:len(lhs)]]
        for p in range(len(pairs)):
            b = ins[len(lhs) + p][...].astype(BF)
            accs[0 if sum_pairs else p][...] += lax.dot_general(a_tiles[lhs_of[p]], b, _DIMS[mode],
                                                                preferred_element_type=F32)

        @pl.when(kk == nk - 1)
        def _():
            ex = [r[...] for r in ins[n_mm:n_mm + len(extras)]]
            rw = [r[...] for r in ins[n_mm + len(extras):]]
            res = epilogue([acc[...] for acc in accs], ex, rw)
            for o, r in zip(outs, res):
                o[...] = r.astype(o.dtype)

        if side:
            @pl.when(jnp.logical_and(jnp.logical_and(at[0] == grid[0] - 1, at[1] == grid[1] - 1), at[2] == nk - 1))
            def _():
                side.finish(side_ins, side_outs, *side_sems)

    in_specs, args = [], []
    for a in lhs:
        if mode == "tn":
            in_specs.append(pl.BlockSpec((tk, tm), lambda i, j, kk: (kk, i)))
        else:
            in_specs.append(pl.BlockSpec((tm, tk), lambda i, j, kk: (i, kk)))
        args.append(a)
    assert b_off % tn == 0 and (b_off == 0 or mode == "nn")
    for _, b in pairs:
        if mode == "nt":
            in_specs.append(pl.BlockSpec((tn, tk), lambda i, j, kk: (j, kk)))
        else:
            in_specs.append(pl.BlockSpec((tk, tn), functools.partial(lambda i, j, kk, o: (kk, j + o), o=b_off // tn)))
        args.append(b)
    for arr, off in extras:
        if off is None:
            in_specs.append(pl.BlockSpec((tm, tn), lambda i, j, kk: (i, 0)))
        else:
            assert off % tn == 0
            in_specs.append(pl.BlockSpec((tm, tn), functools.partial(lambda i, j, kk, o: (i, j + o), o=off // tn)))
        args.append(arr)
    for arr in rows:
        in_specs.append(pl.BlockSpec((1, tn), lambda i, j, kk: (0, j)))
        args.append(arr)
    if piece_layout:
        out_specs = [pl.BlockSpec((None, tm, tn), lambda i, j, kk: (j, i, 0)) for _ in out_dtypes]
        out_shape = [jax.ShapeDtypeStruct((n // tn, m, tn), d) for d in out_dtypes]
    else:
        out_specs = [pl.BlockSpec((tm, tn), lambda i, j, kk: (i, j)) for _ in out_dtypes]
        out_shape = [jax.ShapeDtypeStruct((m, n), d) for d in out_dtypes]
    for _ in range(n_colsum):
        out_specs.append(pl.BlockSpec((None, 1, tn), lambda i, j, kk: (i, 0, j)))
        out_shape.append(jax.ShapeDtypeStruct((m // tm, 1, n), F32))
    scratch = [pltpu.VMEM((tm, tn), F32) for _ in range(n_acc)]
    if side is None:
        return pl.pallas_call(
            body, grid=grid, in_specs=in_specs, out_specs=out_specs, out_shape=out_shape, scratch_shapes=scratch,
            compiler_params=_params(("parallel", "parallel", "arbitrary")), name=name)(*args)
    res = pl.pallas_call(
        body, grid=grid, in_specs=in_specs + [ANY_SPEC] * n_side_in, out_specs=out_specs + [ANY_SPEC] * n_side_out,
        out_shape=out_shape + side.out_shapes, scratch_shapes=scratch + side.scratch(),
        input_output_aliases={n_in + t: n_out + o for t, o in side.aliases.items()},
        compiler_params=_params(("arbitrary", "arbitrary", "arbitrary")), name=name)(*args, *side.inputs)
    return res[:n_out], res[n_out:]


def _col_pieces(full):
    rows, cols = full.shape
    return full.reshape(rows, N_CHIPS, cols // N_CHIPS).transpose(1, 0, 2)


def _dw_col_pieces(name, a, b, m, n, k, side=None):
    ident = lambda accs, ex, rw: (accs[0],)
    aligned = (n // N_CHIPS) % LANES == 0
    res = _mm(name, [(a, b)], "tn", m, n, k, epilogue=ident, out_dtypes=(F32,), tm=512 if aligned else 1024,
              piece_layout=aligned, side=side)
    (out,), side_res = res if side else (res, None)
    out = out if aligned else _col_pieces(out)
    return (out, side_res) if side else out


def _hosted(sides, got, key, call):
    make = sides.get(key) if sides else None
    if make is None:
        return call(None)
    outs, got[key] = call(make(got))
    return outs


def _sigmoid(z):
    return 0.5 * jnp.tanh(0.5 * z) + 0.5


def _row_tile(s):
    return _pick(s, 256, 8)


def _fold8(t):
    r, d = t.shape
    return jnp.sum(t.reshape(r // 8, 8, d), axis=0)


def _rms_fwd(name, x, g):
    s, d = x.shape
    tr = _row_tile(s)

    def body(x_ref, g_ref, h_ref):
        xf = x_ref[...]
        y = xf * lax.rsqrt(jnp.mean(xf * xf, axis=-1, keepdims=True) + NORM_EPS)
        h_ref[...] = (y * g_ref[...]).astype(BF)

    return pl.pallas_call(
        body, grid=(s // tr,),
        in_specs=[pl.BlockSpec((tr, d), lambda i: (i, 0)), pl.BlockSpec((1, d), lambda i: (0, 0))],
        out_specs=pl.BlockSpec((tr, d), lambda i: (i, 0)), out_shape=jax.ShapeDtypeStruct((s, d), BF),
        compiler_params=_params(("parallel",)), name=name)(x, g)


def _rms_bwd(name, x, g, dh, dres, side=None):
    s, d = x.shape
    tr = _row_tile(s)
    steps = s // tr
    n_side_in = len(side.inputs) if side else 0
    n_side_out = len(side.out_shapes) if side else 0

    def body(*refs):
        x_ref, g_ref, dh_ref, dres_ref = refs[:4]
        side_ins, refs = refs[4:4 + n_side_in], refs[4 + n_side_in:]
        dx_ref, dxb_ref, dg_ref = refs[:3]
        side_outs, side_sems = refs[3:3 + n_side_out], refs[3 + n_side_out:]

        @pl.when(pl.program_id(0) == 0)
        def _():
            dg_ref[...] = jnp.zeros_like(dg_ref)
            if side:
                side.start(side_ins, side_outs, *side_sems)

        xf = x_ref[...]
        rstd = lax.rsqrt(jnp.mean(xf * xf, axis=-1, keepdims=True) + NORM_EPS)
        xhat = xf * rstd
        dhf = dh_ref[...]
        dg_ref[...] += _fold8(dhf * xhat)
        dxh = dhf * g_ref[...]
        dx = dres_ref[...] + rstd * (dxh - xhat * jnp.mean(dxh * xhat, axis=-1, keepdims=True))
        dx_ref[...] = dx
        dxb_ref[...] = dx.astype(BF)

        if side:
            @pl.when(pl.program_id(0) == steps - 1)
            def _():
                side.finish(side_ins, side_outs, *side_sems)

    blk = pl.BlockSpec((tr, d), lambda i: (i, 0))
    res = pl.pallas_call(
        body, grid=(steps,),
        in_specs=[blk, pl.BlockSpec((1, d), lambda i: (0, 0)), blk, blk] + [ANY_SPEC] * n_side_in,
        out_specs=[blk, blk, pl.BlockSpec((8, d), lambda i: (0, 0))] + [ANY_SPEC] * n_side_out,
        out_shape=[jax.ShapeDtypeStruct((s, d), F32), jax.ShapeDtypeStruct((s, d), BF),
                   jax.ShapeDtypeStruct((8, d), F32)] + (side.out_shapes if side else []),
        scratch_shapes=side.scratch() if side else [],
        compiler_params=_params(("arbitrary",)), name=name)(x, g, dh, dres, *(side.inputs if side else []))
    return (res[:3], res[3:]) if side else res


def _final(name, x, g, tgt):
    s, d = x.shape
    tr = _row_tile(s)

    def body(x_ref, g_ref, t_ref, dx_ref, dxb_ref, dg_ref, sq_ref):
        @pl.when(pl.program_id(0) == 0)
        def _():
            dg_ref[...] = jnp.zeros_like(dg_ref)
            sq_ref[...] = jnp.zeros_like(sq_ref)

        xf = x_ref[...]
        rstd = lax.rsqrt(jnp.mean(xf * xf, axis=-1, keepdims=True) + NORM_EPS)
        xhat = xf * rstd
        gf = g_ref[...]
        err = xhat * gf - t_ref[...]
        sq_ref[...] += _fold8(err * err)
        dy = err * (1.0 / d)
        dg_ref[...] += _fold8(dy * xhat)
        dxh = dy * gf
        dx = rstd * (dxh - xhat * jnp.mean(dxh * xhat, axis=-1, keepdims=True))
        dx_ref[...] = dx
        dxb_ref[...] = dx.astype(BF)

    blk = pl.BlockSpec((tr, d), lambda i: (i, 0))
    acc = pl.BlockSpec((8, d), lambda i: (0, 0))
    return pl.pallas_call(
        body, grid=(s // tr,), in_specs=[blk, pl.BlockSpec((1, d), lambda i: (0, 0)), blk],
        out_specs=[blk, blk, acc, acc],
        out_shape=[jax.ShapeDtypeStruct((s, d), F32), jax.ShapeDtypeStruct((s, d), BF),
                   jax.ShapeDtypeStruct((8, d), F32), jax.ShapeDtypeStruct((8, d), F32)],
        compiler_params=_params(("arbitrary",)), name=name)(x, g, tgt)


def _ffn_fwd(tag, x, g, w_gate, w_up, get_w_down, sides=None):
    s, d = x.shape
    f = w_gate.shape[1]
    got = {}
    h = _rms_fwd(tag + "_rms", x, g)

    def up_epi(accs, ex, rw):
        a, b = accs
        return a, b, a * _sigmoid(a) * b

    a, b, act = _hosted(sides, got, "up", lambda side: _mm(
        tag + "_up", [(h, w_gate), (h, w_up)], "nn", s, f, d, epilogue=up_epi, out_dtypes=(BF, BF, BF), side=side))

    def down_epi(accs, ex, rw):
        return (ex[0] + 0.5 * accs[0],)

    w_down = get_w_down(got)
    (y,) = _hosted(sides, got, "down", lambda side: _mm(
        tag + "_down", [(act, w_down)], "nn", s, d, f, epilogue=down_epi, out_dtypes=(F32,), extras=[(x, 0)],
        side=side))
    return y, (h, a, b, act), got


def _ffn_bwd(tag, x, g, w_gate, w_up, w_down, saved, dy, dy_bf, sides=None):
    s, d = x.shape
    f = w_gate.shape[1]
    h, a, b, act = saved
    got = {}

    def act_epi(accs, ex, rw):
        dact = 0.5 * accs[0]
        av, bv = ex[0].astype(F32), ex[1].astype(F32)
        sg = _sigmoid(av)
        return dact * bv * (sg * (1.0 + av * (1.0 - sg))), dact * (av * sg)

    da, db = _hosted(sides, got, "dact", lambda side: _mm(
        tag + "_dact", [(dy_bf, w_down)], "nt", s, f, d, epilogue=act_epi, out_dtypes=(BF, BF),
        extras=[(a, 0), (b, 0)], side=side))
    ident = lambda accs, ex, rw: (accs[0],)
    (dw_down,) = _hosted(sides, got, "dwd", lambda side: _mm(
        tag + "_dwd", [(act, dy_bf)], "tn", f, d, s, epilogue=lambda accs, ex, rw: (0.5 * accs[0],),
        out_dtypes=(F32,), side=side))
    got["dw_down"] = dw_down = dw_down.reshape(N_CHIPS, f // N_CHIPS, d)
    got["dw_gate"] = dw_gate = _hosted(sides, got, "dwg", lambda side: _dw_col_pieces(
        tag + "_dwg", h, da, d, f, s, side=side))
    got["dw_up"] = dw_up = _hosted(sides, got, "dwu", lambda side: _dw_col_pieces(
        tag + "_dwu", h, db, d, f, s, side=side))
    (dh,) = _hosted(sides, got, "dh", lambda side: _mm(
        tag + "_dh", [(da, w_gate), (db, w_up)], "nt", s, d, f, epilogue=ident, out_dtypes=(F32,), sum_pairs=True,
        side=side))
    dx, dx_bf, dg = _hosted(sides, got, "rmsb", lambda side: _rms_bwd(tag + "_rmsb", x, g, dh, dy, side=side))
    return dx, dx_bf, dg, dw_gate, dw_up, dw_down, got


def _rope_tables(s):
    half = ROPE_DIM // 2
    pos = jnp.arange(s, dtype=F32)
    inv_freq = ROPE_THETA ** (-jnp.arange(0, ROPE_DIM, 2, dtype=F32) / ROPE_DIM)
    ang = pos[:, None] * inv_freq[None, :]
    cos, sin = jnp.cos(ang), jnp.sin(ang)
    rest = HEAD_DIM - ROPE_DIM
    cos_t = jnp.concatenate([cos, cos, jnp.ones((s, rest), F32)], axis=-1)
    sin_t = jnp.concatenate([-sin, sin, jnp.zeros((s, rest), F32)], axis=-1)
    return cos_t, sin_t


def _swap_halves(t):
    lane = lax.broadcasted_iota(jnp.int32, t.shape, 1) & (HEAD_DIM - 1)
    half = ROPE_DIM // 2
    return jnp.where(lane < half, pltpu.roll(t, t.shape[1] - half, 1), pltpu.roll(t, half, 1))


def _dil_bias(blk):
    n_delta = MAX_WINDOW // blk + 1
    delta = jnp.arange(n_delta, dtype=jnp.int32)[:, None, None]
    r = jnp.arange(blk, dtype=jnp.int32)[None, None, :]
    c = jnp.arange(blk, dtype=jnp.int32)[None, :, None]
    o = delta * blk + r - c
    mult = jnp.zeros(o.shape, F32)
    for w, dd in DIL_PATTERNS:
        mult = mult + ((o >= 0) & (o <= w) & (o % dd == 0)).astype(F32)
    return jnp.where(mult > 0, jnp.log(jnp.maximum(mult, 1.0)), NEG)


def _att_block(s):
    return _pick(s, ATT_BLOCK, LANES)


def _fox_aug(name, c_pad, qkv, q_off, k_off, n_heads):
    s = c_pad.shape[0]
    tr = _pick(s, 1024, 16)

    def body(c_ref, q_ref, k_ref, qc_ref, kc_ref):
        h = pl.program_id(1)
        lane = lax.broadcasted_iota(jnp.int32, (tr, LANES), 1)
        ch = jnp.sum(jnp.where(lane == h, c_ref[...], 0.0), axis=1, keepdims=True)
        hi, mid, lo = (t.astype(F32) for t in _split3(ch))
        zero = jnp.zeros((tr, LANES), F32)
        is_hi = jnp.logical_or(lane == 0, lane == 3)
        is_mid = jnp.logical_or(lane == 1, lane == 4)
        parts = jnp.where(is_hi, hi, jnp.where(is_mid, mid, lo))
        qc_ref[:, :HEAD_DIM] = q_ref[...]
        kc_ref[:, :HEAD_DIM] = k_ref[...]
        qc_ref[:, HEAD_DIM:] = jnp.where(lane < 3, 1.0, jnp.where(lane < 6, parts, zero)).astype(BF)
        kc_ref[:, HEAD_DIM:] = jnp.where(lane < 3, -parts, jnp.where(lane < 6, 1.0, zero)).astype(BF)

    head = lambda o: pl.BlockSpec((tr, HEAD_DIM), functools.partial(lambda i, h, o: (i, o + h), o=o))
    spec = pl.BlockSpec((tr, 2 * HEAD_DIM), lambda i, h: (i, h))
    shape = jax.ShapeDtypeStruct((s, n_heads * 2 * HEAD_DIM), BF)
    return pl.pallas_call(
        body, grid=(s // tr, n_heads),
        in_specs=[pl.BlockSpec((tr, LANES), lambda i, h: (i, 0)), head(q_off), head(k_off)],
        out_specs=[spec, spec], out_shape=[shape, shape],
        compiler_params=_params(("parallel", "arbitrary")), name=name)(c_pad, qkv, qkv)


def _causal_mask(st):
    kpos = lax.broadcasted_iota(jnp.int32, st.shape, 0)
    qpos = lax.broadcasted_iota(jnp.int32, st.shape, 1)
    return jnp.where(kpos <= qpos, st, NEG)


def _flash_fwd(name, q_src, k_src, v_src, n_heads, *, fox, tab_t=None):
    (q_arr, q_off, qw), (k_arr, k_off, kw), (v_arr, v_off, _) = q_src, k_src, v_src
    s = q_arr.shape[0]
    blk = _att_block(s)
    nq = s // blk
    n_delta = MAX_WINDOW // blk + 1
    grp = ATT_HEADS
    assert qw == kw and n_heads % grp == 0 and q_off % grp == 0 and k_off % grp == 0 and v_off % grp == 0
    wide = grp * HEAD_DIM

    def body(*refs):
        if fox:
            q_ref, k_ref, v_ref, o_ref, lse_ref, acc, m_s, l_s = refs
        else:
            q_ref, k_ref, v_ref, tab_ref, o_ref, lse_ref, acc, m_s, l_s = refs
        i = pl.program_id(1)
        acc[...] = jnp.zeros_like(acc)
        m_s[...] = jnp.full_like(m_s, NEG)
        l_s[...] = jnp.zeros_like(l_s)

        def step(j, diagonal):
            ks = pl.ds(pl.multiple_of(j * blk, blk), blk)
            for g in range(grp):
                cols = slice(g * HEAD_DIM, (g + 1) * HEAD_DIM)
                qk_cols = slice(g * qw, (g + 1) * qw)
                st = lax.dot_general(k_ref[ks, qk_cols], q_ref[:, qk_cols], _DIMS["nt"], preferred_element_type=F32)
                if fox:
                    if diagonal:
                        st = _causal_mask(st)
                else:
                    st = st + tab_ref[i - j]
                m_prev = m_s[g]
                m_new = jnp.maximum(m_prev, jnp.max(st, axis=0, keepdims=True))
                alpha = jnp.exp(m_prev - m_new)
                p = jnp.exp(st - m_new)
                l_s[g] = alpha * l_s[g] + jnp.sum(p, axis=0, keepdims=True)
                acc[g] = alpha * acc[g] + lax.dot_general(v_ref[ks, cols], p.astype(BF), _DIMS["tn"],
                                                          preferred_element_type=F32)
                m_s[g] = m_new

        def loop_step(j, carry):
            step(j, False)
            return carry

        if fox:
            lax.fori_loop(0, i, loop_step, 0)
            step(i, True)
        else:
            lax.fori_loop(jnp.maximum(i - (n_delta - 1), 0), i + 1, loop_step, 0)
        for g in range(grp):
            o_ref[:, g * HEAD_DIM:(g + 1) * HEAD_DIM] = (acc[g] / l_s[g]).T.astype(o_ref.dtype)
            lse_ref[g] = m_s[g] + jnp.log(l_s[g])

    off = lambda o: functools.partial(lambda h, i, o: (0, o + h), o=o // grp)
    in_specs = [pl.BlockSpec((blk, grp * qw), functools.partial(lambda h, i, o: (i, o + h), o=q_off // grp)),
                pl.BlockSpec((s, grp * kw), off(k_off)), pl.BlockSpec((s, wide), off(v_off))]
    args = [q_arr, k_arr, v_arr]
    if not fox:
        in_specs.append(pl.BlockSpec((n_delta, blk, blk), lambda h, i: (0, 0, 0)))
        args.append(tab_t)
    return pl.pallas_call(
        body, grid=(n_heads // grp, nq), in_specs=in_specs,
        out_specs=[pl.BlockSpec((blk, wide), lambda h, i: (i, h)),
                   pl.BlockSpec((grp, None, 1, blk), lambda h, i: (h, i, 0, 0))],
        out_shape=[jax.ShapeDtypeStruct((s, n_heads * HEAD_DIM), BF),
                   jax.ShapeDtypeStruct((n_heads, nq, 1, blk), F32)],
        scratch_shapes=[pltpu.VMEM((grp, HEAD_DIM, blk), F32), pltpu.VMEM((grp, 1, blk), F32),
                        pltpu.VMEM((grp, 1, blk), F32)],
        compiler_params=_params(("parallel", "parallel")), name=name)(*args)


def _att_delta(name, do, o, n_heads):
    s = do.shape[0]
    blk = _pick(s, 1024, 16)

    def body(do_ref, o_ref, d_ref):
        d_ref[...] = jnp.sum(do_ref[...].astype(F32) * o_ref[...].astype(F32), axis=-1, keepdims=True)

    spec = pl.BlockSpec((blk, HEAD_DIM), lambda h, i: (i, h))
    return pl.pallas_call(
        body, grid=(n_heads, s // blk), in_specs=[spec, spec],
        out_specs=pl.BlockSpec((None, blk, 1), lambda h, i: (h, i, 0)),
        out_shape=jax.ShapeDtypeStruct((n_heads, s, 1), F32),
        compiler_params=_params(("parallel", "parallel")), name=name)(do, o)


def _flash_bwd(name, q_src, k_src, v_src, do, lse_row, delta_row, n_heads, *, fox, tab_t=None):
    (q_arr, q_off, qw), (k_arr, k_off, kw), (v_arr, v_off, _) = q_src, k_src, v_src
    s = q_arr.shape[0]
    blk = _att_block(s)
    nq = s // blk
    n_delta = MAX_WINDOW // blk + 1
    grp = ATT_HEADS
    assert qw == kw and n_heads % grp == 0 and q_off % grp == 0 and k_off % grp == 0 and v_off % grp == 0
    wide = grp * HEAD_DIM

    def body(*refs):
        if fox:
            (q_ref, do_ref, k_ref, v_ref, lse_ref, dl_ref,
             dq_ref, dk_ref, dv_ref, dc_ref, dcq_ref, dk_acc, dv_acc, dc_acc) = refs
        else:
            (q_ref, do_ref, k_ref, v_ref, lse_ref, dl_ref, tab_ref,
             dq_ref, dk_ref, dv_ref, dk_acc, dv_acc) = refs
        j = pl.program_id(1)

        @pl.when(j == 0)
        def _():
            dq_ref[...] = jnp.zeros_like(dq_ref)
            if fox:
                dcq_ref[...] = jnp.zeros_like(dcq_ref)

        dk_acc[...] = jnp.zeros_like(dk_acc)
        dv_acc[...] = jnp.zeros_like(dv_acc)
        if fox:
            dc_acc[...] = jnp.zeros_like(dc_acc)

        def step(i, diagonal):
            qs = pl.ds(pl.multiple_of(i * blk, blk), blk)
            for g in range(grp):
                cols = slice(g * HEAD_DIM, (g + 1) * HEAD_DIM)
                qk_cols = slice(g * qw, (g + 1) * qw)
                plain = slice(g * qw, g * qw + HEAD_DIM)
                kb, vb = k_ref[:, plain], v_ref[:, cols]
                qb, dob = q_ref[qs, plain], do_ref[qs, cols]
                st = lax.dot_general(k_ref[:, qk_cols], q_ref[qs, qk_cols], _DIMS["nt"], preferred_element_type=F32)
                if fox:
                    if diagonal:
                        st = _causal_mask(st)
                else:
                    st = st + tab_ref[i - j]
                pt = jnp.exp(st - lse_ref[g, i])
                dv_acc[:, cols] += jnp.dot(pt.astype(BF), dob, preferred_element_type=F32)
                dpt = lax.dot_general(vb, dob, _DIMS["nt"], preferred_element_type=F32)
                dst = pt * (dpt - dl_ref[g, i])
                dsb = dst.astype(BF)
                dk_acc[:, cols] += jnp.dot(dsb, qb, preferred_element_type=F32)
                dq_ref[qs, cols] += lax.dot_general(dsb, kb, _DIMS["tn"], preferred_element_type=F32)
                if fox:
                    folded = dst[:, :LANES]
                    for part in range(1, blk // LANES):
                        folded = folded + dst[:, part * LANES:(part + 1) * LANES]
                    dc_acc[g] -= folded
                    dcq_ref[g, i] += jnp.sum(dst, axis=0, keepdims=True)

        def loop_step(i, carry):
            step(i, False)
            return carry

        if fox:
            step(j, True)
            lax.fori_loop(j + 1, nq, loop_step, 0)
        else:
            lax.fori_loop(j, jnp.minimum(nq, j + n_delta), loop_step, 0)
        dk_ref[...] = dk_acc[...]
        dv_ref[...] = dv_acc[...].astype(dv_ref.dtype)
        if fox:
            for g in range(grp):
                dc_ref[g] = jnp.sum(dc_acc[g], axis=-1, keepdims=True)

    full = lambda o, wd=wide: pl.BlockSpec((s, wd), functools.partial(lambda h, j, o: (0, o + h), o=o // grp))
    tile = lambda o, wd=wide: pl.BlockSpec((blk, wd), functools.partial(lambda h, j, o: (j, o + h), o=o // grp))
    per_q = pl.BlockSpec((grp, nq, 1, blk), lambda h, j: (h, 0, 0, 0))
    per_k = pl.BlockSpec((grp, blk, 1), lambda h, j: (h, j, 0))
    in_specs = [full(q_off, grp * qw), full(0), tile(k_off, grp * kw), tile(v_off), per_q, per_q]
    args = [q_arr, do, k_arr, v_arr, lse_row, delta_row]
    out_specs = [full(0), tile(0), tile(0)]
    hd = n_heads * HEAD_DIM
    out_shape = [jax.ShapeDtypeStruct((s, hd), F32), jax.ShapeDtypeStruct((s, hd), F32),
                 jax.ShapeDtypeStruct((s, hd), BF)]
    scratch = [pltpu.VMEM((blk, wide), F32), pltpu.VMEM((blk, wide), F32)]
    if fox:
        out_specs += [per_k, per_q]
        out_shape += [jax.ShapeDtypeStruct((n_heads, s, 1), F32), jax.ShapeDtypeStruct((n_heads, nq, 1, blk), F32)]
        scratch.append(pltpu.VMEM((grp, blk, LANES), F32))
    else:
        in_specs.append(pl.BlockSpec((n_delta, blk, blk), lambda h, j: (0, 0, 0)))
        args.append(tab_t)
    return pl.pallas_call(
        body, grid=(n_heads // grp, nq), in_specs=in_specs, out_specs=out_specs, out_shape=out_shape,
        scratch_shapes=scratch, compiler_params=_params(("parallel", "arbitrary")), name=name)(*args)


def _split3(t):
    hi = t.astype(BF)
    r1 = t - hi.astype(F32)
    mid = r1.astype(BF)
    lo = (r1 - mid.astype(F32)).astype(BF)
    return hi, mid, lo


def _tri_dot(tri, t):
    hi, mid, lo = _split3(t)
    return (jnp.dot(tri, hi, preferred_element_type=F32) + jnp.dot(tri, mid, preferred_element_type=F32)
            + jnp.dot(tri, lo, preferred_element_type=F32))


def _log_sigmoid(z):
    return jnp.minimum(z, 0.0) - jnp.log(1.0 + jnp.exp(-jnp.abs(z)))


def _forget_cumsum(name, proj, f_col, bias):
    s = proj.shape[0]
    blk = _att_block(s)

    def body(f_ref, b_ref, c_ref, carry):
        @pl.when(pl.program_id(0) == 0)
        def _():
            carry[...] = jnp.zeros_like(carry)

        lf = _log_sigmoid(f_ref[...] + b_ref[...])
        r = lax.broadcasted_iota(jnp.int32, (blk, blk), 0)
        c = lax.broadcasted_iota(jnp.int32, (blk, blk), 1)
        tri = (c <= r).astype(BF)
        c_ref[...] = _tri_dot(tri, lf) + carry[...]
        carry[...] = c_ref[pl.ds(blk - 1, 1), :]

    return pl.pallas_call(
        body, grid=(s // blk,),
        in_specs=[pl.BlockSpec((blk, LANES), lambda i: (i, f_col)), pl.BlockSpec((1, LANES), lambda i: (0, 0))],
        out_specs=pl.BlockSpec((blk, LANES), lambda i: (i, 0)), out_shape=jax.ShapeDtypeStruct((s, LANES), F32),
        scratch_shapes=[pltpu.VMEM((1, LANES), F32)],
        compiler_params=_params(("arbitrary",)), name=name)(proj, bias)


def _forget_bwd(name, proj, f_col, bias, dc):
    s = proj.shape[0]
    blk = _att_block(s)
    nb = s // blk

    def body(f_ref, b_ref, dc_ref, df_ref, db_ref, carry):
        @pl.when(pl.program_id(0) == 0)
        def _():
            carry[...] = jnp.zeros_like(carry)
            db_ref[...] = jnp.zeros_like(db_ref)

        r = lax.broadcasted_iota(jnp.int32, (blk, blk), 0)
        c = lax.broadcasted_iota(jnp.int32, (blk, blk), 1)
        tri = (c >= r).astype(BF)
        r = lax.broadcasted_iota(jnp.int32, (blk, LANES), 0)
        dlf = _tri_dot(tri, dc_ref[...]) + carry[...]
        carry[...] = jnp.sum(jnp.where(r == 0, dlf, 0.0), axis=0, keepdims=True)
        dz = dlf * _sigmoid(-(f_ref[...] + b_ref[...]))
        df_ref[...] = dz.astype(BF)
        db_ref[...] += _fold8(dz)

    rev = lambda i: (nb - 1 - i, 0)
    return pl.pallas_call(
        body, grid=(nb,),
        in_specs=[pl.BlockSpec((blk, LANES), lambda i: (nb - 1 - i, f_col)), pl.BlockSpec((1, LANES), lambda i: (0, 0)),
                  pl.BlockSpec((blk, LANES), rev)],
        out_specs=[pl.BlockSpec((blk, LANES), rev), pl.BlockSpec((8, LANES), lambda i: (0, 0))],
        out_shape=[jax.ShapeDtypeStruct((s, LANES), BF), jax.ShapeDtypeStruct((8, LANES), F32)],
        scratch_shapes=[pltpu.VMEM((1, LANES), F32)],
        compiler_params=_params(("arbitrary",)), name=name)(proj, bias, dc)


def _dproj_assemble(name, parts, gates, cos_t, sin_t, n_heads, d_model):
    s = cos_t.shape[0]
    tr = _pick(s, 1024, 16)
    scale = HEAD_DIM ** -0.5
    hd = n_heads * HEAD_DIM
    cw = _pick(math.gcd(hd, d_model), PREP_COLS, HEAD_DIM)
    widths = [hd] * 6 + [d_model] * 2
    starts = [sum(widths[:t]) // cw for t in range(len(widths) + 1)]
    cos_w, sin_w = jnp.tile(cos_t, (1, cw // HEAD_DIM)), jnp.tile(sin_t, (1, cw // HEAD_DIM))

    def body(*refs):
        p_refs, cos_ref, sin_ref, o_ref = refs[:8], refs[8], refs[9], refs[10]
        j = pl.program_id(1)
        for kind in range(8):
            @pl.when(jnp.logical_and(j >= starts[kind], j < starts[kind + 1]))
            def _(kind=kind):
                t = p_refs[kind][...]
                if kind in (0, 1, 3):
                    t = t.astype(F32)
                if kind in (0, 3):
                    t = t * scale
                if kind in (0, 1):
                    t = t * cos_ref[...] - _swap_halves(t) * sin_ref[...]
                o_ref[...] = t.astype(BF)

    def part_spec(kind):
        return pl.BlockSpec((tr, cw), functools.partial(
            lambda i, j, kind: (i, jnp.clip(j - starts[kind], 0, widths[kind] // cw - 1)), kind=kind))

    tab = pl.BlockSpec((tr, cw), lambda i, j: (i, 0))
    return pl.pallas_call(
        body, grid=(s // tr, starts[-1]), in_specs=[part_spec(kind) for kind in range(8)] + [tab, tab],
        out_specs=pl.BlockSpec((tr, cw), lambda i, j: (i, j)),
        out_shape=jax.ShapeDtypeStruct((s, sum(widths)), BF),
        compiler_params=_params(("parallel", "arbitrary")), name=name)(*parts, *gates, cos_w, sin_w)


FFN2 =("ffn2_w_gate", "ffn2_w_up", "ffn2_w_down")


def _device_step(x, tgt, shards, small):
    s, d = x.shape
    hd = shards["w_proj_dil"].shape[0]
    hh = hd // HEAD_DIM
    blk = _att_block(s)
    gate_off = 6 * hd
    f_col = 0
    n_proj = gate_off + 2 * d
    ident = lambda accs, ex, rw: (accs[0],)
    chip = 2 * lax.axis_index("x") + lax.axis_index("y")
    ids = jnp.stack([chip, lax.axis_index("c")]).astype(jnp.int32)
    w = {}

    def take_gathered(names, gathered):
        for n, t in zip(names, gathered):
            t = lax.dynamic_update_index_in_dim(t, shards[n], chip, 0)
            if n == "w_in":
                w["w_in"], w["w_in_f"] = _repack_w_in(t, hd, d)
            else:
                w[n] = _full_from_pieces(n, t)

    def chip_sums(names, pieces, from_sibling):
        return [_rs_add("rs_add_" + n, ids, g, o) for n, g, o in zip(names, pieces, from_sibling)]

    def core_halves(names, sums, from_chips):
        return [_rs_sum("rs_sum_" + n, ids, own, got) for n, own, got in zip(names, sums, from_chips)]

    def gather(names):
        return _gather_side([shards[n] for n in names])

    first, under_up = ("ffn1_w_gate", "ffn1_w_up"), ("ffn1_w_down", "w_in")
    under_down = ("w_proj_dil", "w_proj_fox", "w_out", "ffn2_w_gate")
    under_proj = ("ffn2_w_up", "ffn2_w_down")
    take_gathered(first, gather(first).call("gather_first"))

    def w_down_of_ffn1(got):
        take_gathered(under_up, got["up"])
        return w["ffn1_w_down"]

    x1, saved1, got = _ffn_fwd("ffn1", x, small["ffn1_norm"], w["ffn1_w_gate"], w["ffn1_w_up"], w_down_of_ffn1,
                               sides={"up": lambda got: gather(under_up), "down": lambda got: gather(under_down)})
    take_gathered(under_down, got["down"])

    hm = _rms_fwd("mix_rms", x1, small["mix_norm"])
    cos_t, sin_t = _rope_tables(s)
    head_tile = _pick(hd, 1024, HEAD_DIM)
    tiles_per_kind = hd // head_tile
    scale = HEAD_DIM ** -0.5

    def heads_epi(accs, ex, rw):
        kind = pl.program_id(1) // tiles_per_kind
        t = accs[0]
        r = t * ex[0] + _swap_halves(t) * ex[1]
        t = jnp.where(kind < 2, r, t)
        return (jnp.where(jnp.logical_or(kind == 0, kind == 3), t * scale, t),)

    wide = lambda tab: jnp.tile(tab, (1, head_tile // HEAD_DIM))
    (qkv,), gathered = _mm("proj_heads", [(hm, w["w_in"])], "nn", s, gate_off, d, epilogue=heads_epi, out_dtypes=(BF,),
                           extras=[(wide(cos_t), None), (wide(sin_t), None)], tn=head_tile, keep_tn=True,
                           side=gather(under_proj[:1]))
    take_gathered(under_proj[:1], gathered)
    (gates,), gathered = _mm("proj_gates", [(hm, w["w_in"])], "nn", s, 2 * d, d, epilogue=ident, out_dtypes=(F32,),
                             b_off=gate_off, side=gather(under_proj[1:]))
    take_gathered(under_proj[1:], gathered)
    (f_logit,) = _mm("proj_f", [(hm, w["w_in_f"])], "nn", s, LANES, d, epilogue=ident, out_dtypes=(F32,))
    tab_t = _dil_bias(blk)
    dil_src = ((qkv, 0, HEAD_DIM), (qkv, hh, HEAD_DIM), (qkv, 2 * hh, HEAD_DIM))
    y_dil, lse_d = _flash_fwd("dil_fwd", *dil_src, hh, fox=False, tab_t=tab_t)
    bias_f = jnp.pad(small["b_forget"], ((0, 0), (0, LANES - hh)))
    c_pad = _forget_cumsum("forget_cumsum", f_logit, f_col, bias_f)
    q_cat, k_cat = _fox_aug("fox_aug", c_pad, qkv, 3 * hh, 4 * hh, hh)
    fox_src = ((q_cat, 0, 2 * HEAD_DIM), (k_cat, 0, 2 * HEAD_DIM), (qkv, 5 * hh, HEAD_DIM))
    y_fox, lse_f = _flash_fwd("fox_fwd", *fox_src, hh, fox=True)

    def merge_epi(accs, ex, rw):
        ud, uf = accs
        return ud, uf, _sigmoid(ex[0] + rw[0]) * ud + _sigmoid(ex[1] + rw[1]) * uf

    u_d, u_f, merged = _mm("merge", [(y_dil, w["w_proj_dil"]), (y_fox, w["w_proj_fox"])], "nn", s, d, hd,
                           epilogue=merge_epi, out_dtypes=(BF, BF, BF),
                           extras=[(gates, 0), (gates, d)],
                           rows=[small["b_gate_dil"], small["b_gate_fox"]])
    (x2,) = _mm("mix_out", [(merged, w["w_out"])], "nn", s, d, d,
                epilogue=lambda accs, ex, rw: (ex[0] + accs[0],), out_dtypes=(F32,), extras=[(x1, 0)])

    x3, saved2, _ = _ffn_fwd("ffn2", x2, small["ffn2_norm"], w["ffn2_w_gate"], w["ffn2_w_up"],
                             lambda got: w["ffn2_w_down"])
    dx3, dx3_bf, dg_final, sq = _final("final", x3, small["final_norm"].reshape(1, d), tgt)

    dx2, dx2_bf, dg_ffn2, dw_g2, dw_u2, dw_d2, _ = _ffn_bwd("ffn2", x2, small["ffn2_norm"], w["ffn2_w_gate"],
                                                            w["ffn2_w_up"], w["ffn2_w_down"], saved2, dx3, dx3_bf)
    pieces2 = [dw_g2, dw_u2, dw_d2]

    def dmerge_epi(accs, ex, rw):
        dm = accs[0]
        gd, gf, ud, uf = ex[0], ex[1], ex[2].astype(F32), ex[3].astype(F32)
        sd, sf = _sigmoid(gd + rw[0]), _sigmoid(gf + rw[1])
        dgd = dm * ud * (sd * (1.0 - sd))
        dgf = dm * uf * (sf * (1.0 - sf))
        return (dm * sd, dm * sf, dgd, dgf, jnp.sum(dgd, axis=0, keepdims=True), jnp.sum(dgf, axis=0, keepdims=True))

    (du_d, du_f, dg_d, dg_f, dbg_d, dbg_f), from_sibling = _mm(
        "dmerge", [(dx2_bf, w["w_out"])], "nt", s, d, d, epilogue=dmerge_epi, out_dtypes=(BF, BF, BF, BF), n_colsum=2,
        extras=[(gates, 0), (gates, d), (u_d, 0), (u_f, 0)],
        rows=[small["b_gate_dil"], small["b_gate_fox"]], side=_swap_side(pieces2))
    sums2 = chip_sums(FFN2, pieces2, from_sibling)
    (dw_out,) = _mm("dw_out", [(merged, dx2_bf)], "tn", d, d, s, epilogue=ident, out_dtypes=(F32,))
    dw_out = dw_out.reshape(N_CHIPS, d // N_CHIPS, d)
    dw_pd = _dw_col_pieces("dw_pd", y_dil, du_d, hd, d, s)
    dw_pf = _dw_col_pieces("dw_pf", y_fox, du_f, hd, d, s)
    (dy_dil,) = _mm("dy_dil", [(du_d, w["w_proj_dil"])], "nt", s, hd, d, epilogue=ident, out_dtypes=(BF,))
    (dy_fox,) = _mm("dy_fox", [(du_f, w["w_proj_fox"])], "nt", s, hd, d, epilogue=ident, out_dtypes=(BF,))

    row = lambda t: t.reshape(hh, s // blk, 1, blk)
    delta_d = _att_delta("dil_delta", dy_dil, y_dil, hh)
    dq_d, dk_d, dv_d = _flash_bwd("dil_bwd", *dil_src, dy_dil, lse_d, row(delta_d), hh, fox=False, tab_t=tab_t)
    delta_f = _att_delta("fox_delta", dy_fox, y_fox, hh)
    dq_f, dk_f, dv_f, dc_k, dc_q = _flash_bwd("fox_bwd", *fox_src, dy_fox, lse_f, row(delta_f), hh, fox=True)
    dc = dc_k.reshape(hh, s) + dc_q.reshape(hh, s)
    dc_pad = jnp.pad(dc.T, ((0, 0), (0, LANES - hh)))
    df, db_forget = _forget_bwd("forget_bwd", f_logit, f_col, bias_f, dc_pad)
    dproj = _dproj_assemble("dproj", [dq_d, dk_d, dv_d, dq_f, dk_f, dv_f], [dg_d, dg_f], cos_t, sin_t, hh, d)
    (dhm_f,) = _mm("dhm_f", [(df, w["w_in_f"])], "nt", s, d, LANES, epilogue=ident, out_dtypes=(F32,))
    (dhm,), from_chips = _mm("dhm", [(dproj, w["w_in"])], "nt", s, d, n_proj,
                             epilogue=lambda accs, ex, rw: (accs[0] + ex[0],), out_dtypes=(F32,),
                             extras=[(dhm_f, 0)], side=_scatter_side(sums2))
    halves2 = core_halves(FFN2, sums2, from_chips)
    small_mixer = ("w_proj_dil", "w_proj_fox", "w_out")
    pieces_sm = [dw_pd, dw_pf, dw_out]
    (dw_in,), from_sibling = _mm("dw_in", [(hm, dproj)], "tn", d, n_proj, s, epilogue=ident, out_dtypes=(F32,),
                                 side=_swap_side(pieces_sm))
    sums_sm = chip_sums(small_mixer, pieces_sm, from_sibling)
    (dw_in_f,) = _mm("dw_in_f", [(hm, df)], "tn", d, LANES, s, epilogue=ident, out_dtypes=(F32,))
    dw_in = _dw_in_pieces(dw_in, dw_in_f, hd, d)
    dx1, dx1_bf, dg_mix = _rms_bwd("mix_rmsb", x1, small["mix_norm"], dhm, dx2)

    sums, early = {}, ("w_in",) + small_mixer + FFN2

    def under_dwd(got):
        sums["w_in"] = chip_sums(["w_in"], [dw_in], got["dact"])
        return _scatter_side(sums_sm)

    def under_dwu(got):
        return _swap_side([got["dw_down"], got["dw_gate"]])

    def under_dh(got):
        sums["dg"] = chip_sums(["ffn1_w_down", "ffn1_w_gate"], [got["dw_down"], got["dw_gate"]], got["dwu"])
        halves = (core_halves(["w_in"], sums["w_in"], got["dwg"]) + core_halves(small_mixer, sums_sm, got["dwd"])
                  + halves2)
        return _scatter_side(sums["dg"]).beside(_swap_side([got["dw_up"]])).beside(_join_side(halves))

    def under_rmsb(got):
        sums["u"] = chip_sums(["ffn1_w_up"], [got["dw_up"]], got["dh"][2:3])
        return _scatter_side(sums["u"])

    dx0, _, dg_ffn1, dw_g1, dw_u1, dw_d1, got = _ffn_bwd(
        "ffn1", x, small["ffn1_norm"], w["ffn1_w_gate"], w["ffn1_w_up"], w["ffn1_w_down"], saved1, dx1, dx1_bf,
        sides={"dact": lambda got: _swap_side([dw_in]), "dwd": under_dwd,
               "dwg": lambda got: _scatter_side(sums["w_in"]), "dwu": under_dwu, "dh": under_dh, "rmsb": under_rmsb})
    halves1 = (core_halves(["ffn1_w_down", "ffn1_w_gate"], sums["dg"], got["dh"][:2])
               + core_halves(["ffn1_w_up"], sums["u"], got["rmsb"]))
    grads = dict(zip(early, got["dh"][3:]))
    grads.update(zip(("ffn1_w_down", "ffn1_w_gate", "ffn1_w_up"), _join_side(halves1).call("rs_join")))
    partials = {"ffn1_norm": dg_ffn1, "mix_norm": dg_mix, "ffn2_norm": dg_ffn2, "final_norm": dg_final,
                "b_gate_dil": dbg_d.reshape(-1, d), "b_gate_fox": dbg_f.reshape(-1, d), "b_forget": db_forget, "sq": sq}
    return dx0, grads, partials


def _coords():
    return lax.axis_index("x"), lax.axis_index("y"), lax.axis_index("c")


def _other_chips(x, y):
    return [(1 - x, y), (x, 1 - y), (1 - x, 1 - y)]


ANY_SPEC = pl.BlockSpec(memory_space=pl.ANY)


def _gather_side(shards):
    nw = len(shards)

    def copies(srcs, outs, send_sems, recv_sems):
        x, y, c = _coords()
        chips = _other_chips(x, y)

        def slot(w, px, py, pc):
            half = shards[w].shape[0] // 2
            return outs[w].at[2 * px + py, pl.ds(pc * half, half), :]

        def copy(w, k, src_ref, dst_ref, to):
            return pltpu.make_async_remote_copy(src_ref=src_ref, dst_ref=dst_ref, send_sem=send_sems.at[6 * w + k],
                                                recv_sem=recv_sems.at[6 * w + k], device_id=to, device_id_type=MESH)

        first, arrive, passed, arrive2 = [], [], [], []
        for w in range(nw):
            half = shards[w].shape[0] // 2
            for j, chip in enumerate(chips):
                first.append(copy(w, j, srcs[w].at[pl.ds(c * half, half), :], slot(w, x, y, c), (*chip, c)))
                arrive.append(copy(w, j, slot(w, *chip, c), slot(w, *chip, c), (*chip, c)))
                passed.append(copy(w, 3 + j, slot(w, *chip, c), slot(w, *chip, c), (x, y, 1 - c)))
                arrive2.append(copy(w, 3 + j, slot(w, *chip, 1 - c), slot(w, *chip, 1 - c), (x, y, 1 - c)))
        return first, arrive, passed, arrive2

    def start(srcs, outs, send_sems, recv_sems):
        for cp in copies(srcs, outs, send_sems, recv_sems)[0]:
            cp.start()

    def finish(srcs, outs, send_sems, recv_sems):
        first, arrive, passed, arrive2 = copies(srcs, outs, send_sems, recv_sems)
        for got, fwd in zip(arrive, passed):
            got.wait_recv()
            fwd.start()
        for got in arrive2:
            got.wait_recv()
        for cp in first + passed:
            cp.wait_send()

    return _Side(shards, [jax.ShapeDtypeStruct((N_CHIPS, *t.shape), t.dtype) for t in shards], 6 * nw, start, finish)


def _swap_side(grads):
    nw = len(grads)

    def copies(srcs, outs, send_sems, recv_sems):
        x, y, c = _coords()
        res = []
        for w in range(nw):
            half = grads[w].shape[1] // 2
            for p in range(N_CHIPS):
                k = N_CHIPS * w + p
                res.append(pltpu.make_async_remote_copy(
                    src_ref=srcs[w].at[p, pl.ds((1 - c) * half, half), :], dst_ref=outs[w].at[p],
                    send_sem=send_sems.at[k], recv_sem=recv_sems.at[k], device_id=(x, y, 1 - c), device_id_type=MESH))
        return res

    def start(*refs):
        for cp in copies(*refs):
            cp.start()

    def finish(*refs):
        for cp in copies(*refs):
            cp.wait()

    shapes = [jax.ShapeDtypeStruct((N_CHIPS, t.shape[1] // 2, t.shape[2]), t.dtype) for t in grads]
    return _Side(grads, shapes, N_CHIPS * nw, start, finish)


def _rs_add(name, ids, g, other):
    n, rows, cols = g.shape
    half = rows // 2
    tr = _pick(half, 256, 16)
    nb = half // tr

    def body(ids_ref, g_ref, o_ref, out_ref):
        out_ref[...] = (g_ref[...] + o_ref[...]).astype(BF)

    grid_spec = pltpu.PrefetchScalarGridSpec(
        num_scalar_prefetch=1, grid=(n, nb),
        in_specs=[pl.BlockSpec((None, tr, cols), lambda p, i, ids_ref: (p, ids_ref[1] * nb + i, 0)),
                  pl.BlockSpec((None, tr, cols), lambda p, i, ids_ref: (p, i, 0))],
        out_specs=pl.BlockSpec((None, tr, cols), lambda p, i, ids_ref: (p, i, 0)))
    return pl.pallas_call(body, grid_spec=grid_spec, out_shape=jax.ShapeDtypeStruct((n, half, cols), BF),
                          compiler_params=_params(("parallel", "parallel")), name=name)(ids, g, other)


def _scatter_side(sums):
    nw = len(sums)

    def copies(srcs, outs, send_sems, recv_sems):
        x, y, c = _coords()
        res = []
        for w in range(nw):
            for k, (px, py) in enumerate(_other_chips(x, y)):
                res.append(pltpu.make_async_remote_copy(
                    src_ref=srcs[w].at[2 * px + py], dst_ref=outs[w].at[k], send_sem=send_sems.at[3 * w + k],
                    recv_sem=recv_sems.at[3 * w + k], device_id=(px, py, c), device_id_type=MESH))
        return res

    def start(*refs):
        for cp in copies(*refs):
            cp.start()

    def finish(*refs):
        for cp in copies(*refs):
            cp.wait()

    return _Side(sums, [jax.ShapeDtypeStruct((3, *t.shape[1:]), t.dtype) for t in sums], 3 * nw, start, finish)


def _rs_sum(name, ids, own, got):
    n, half, cols = own.shape
    tr = _pick(half, 256, 16)
    nb = half // tr

    def body(ids_ref, own_ref, got_ref, out_ref):
        t = own_ref[...].astype(F32)
        for k in range(3):
            t = t + got_ref[k].astype(F32)
        out_ref[...] = t

    grid_spec = pltpu.PrefetchScalarGridSpec(
        num_scalar_prefetch=1, grid=(nb,),
        in_specs=[pl.BlockSpec((None, tr, cols), lambda i, ids_ref: (ids_ref[0], i, 0)),
                  pl.BlockSpec((3, tr, cols), lambda i, ids_ref: (0, i, 0))],
        out_specs=pl.BlockSpec((tr, cols), lambda i, ids_ref: (ids_ref[1] * nb + i, 0)))
    return pl.pallas_call(body, grid_spec=grid_spec, out_shape=jax.ShapeDtypeStruct((2 * half, cols), F32),
                          compiler_params=_params(("parallel",)), name=name)(ids, own, got)


def _join_side(totals):
    nw = len(totals)

    def start(ins, bufs, send_sems, recv_sems):
        x, y, c = _coords()
        for w in range(nw):
            half = totals[w].shape[0] // 2
            pltpu.make_async_remote_copy(
                src_ref=bufs[w].at[pl.ds(c * half, half), :], dst_ref=bufs[w].at[pl.ds(c * half, half), :],
                send_sem=send_sems.at[w], recv_sem=recv_sems.at[w], device_id=(x, y, 1 - c),
                device_id_type=MESH).start()

    def finish(ins, bufs, send_sems, recv_sems):
        x, y, c = _coords()
        for w in range(nw):
            half = totals[w].shape[0] // 2
            arrival = pltpu.make_async_remote_copy(
                src_ref=bufs[w].at[pl.ds(c * half, half), :], dst_ref=bufs[w].at[pl.ds((1 - c) * half, half), :],
                send_sem=send_sems.at[w], recv_sem=recv_sems.at[w], device_id=(x, y, 1 - c), device_id_type=MESH)
            arrival.wait_recv()
            arrival.wait_send()

    return _Side(totals, [jax.ShapeDtypeStruct(t.shape, t.dtype) for t in totals], nw, start, finish,
                 aliases={w: w for w in range(nw)})


def _gather_all(name, t):
    rows, cols = t.shape

    def body(src, out, send_sems, recv_sems, local_sem):
        x, y, c = _coords()
        me = 4 * x + 2 * y + c
        mine = pltpu.make_async_copy(src, out.at[me], local_sem)
        mine.start()
        peers = [(x ^ (k >> 2 & 1), y ^ (k >> 1 & 1), c ^ (k & 1)) for k in range(1, N_DEV)]
        sends = [pltpu.make_async_remote_copy(src_ref=src, dst_ref=out.at[me], send_sem=send_sems.at[k],
                                              recv_sem=recv_sems.at[k], device_id=peer, device_id_type=MESH)
                 for k, peer in enumerate(peers)]
        for cp in sends:
            cp.start()
        for k, (px, py, pc) in enumerate(peers):
            pltpu.make_async_remote_copy(src_ref=src, dst_ref=out.at[4 * px + 2 * py + pc], send_sem=send_sems.at[k],
                                         recv_sem=recv_sems.at[k], device_id=(px, py, pc),
                                         device_id_type=MESH).wait_recv()
        for cp in sends:
            cp.wait_send()
        mine.wait()

    vmem = pl.BlockSpec(memory_space=pltpu.VMEM)
    return pl.pallas_call(
        body, in_specs=[vmem], out_specs=vmem, out_shape=jax.ShapeDtypeStruct((N_DEV, rows, cols), t.dtype),
        scratch_shapes=[pltpu.SemaphoreType.DMA((7,)), pltpu.SemaphoreType.DMA((7,)), pltpu.SemaphoreType.DMA],
        name=name)(t)


def _adamw_math(w, g, m, v):
    m = ADAM_B1 * m + (1.0 - ADAM_B1) * g
    v = ADAM_B2 * v + (1.0 - ADAM_B2) * (g * g)
    m_hat = m / (1.0 - ADAM_B1 ** ADAM_STEP)
    v_hat = v / (1.0 - ADAM_B2 ** ADAM_STEP)
    delta = -ADAM_LR * (m_hat / (jnp.sqrt(v_hat) + ADAM_EPS) + ADAM_WD * w)
    return delta, m, v


def _adamw(name, w, g, m, v):
    _, rows, cols = w.shape
    tr = _pick(rows, 256, 8)

    def body(w_ref, g_ref, m_ref, v_ref, g_out, d_out, m_out, v_out):
        g = g_ref[...]
        g_out[...] = g
        d_out[...], m_out[...], v_out[...] = _adamw_math(w_ref[...], g, m_ref[...], v_ref[...])

    blk3 = pl.BlockSpec((None, tr, cols), lambda i: (0, i, 0))
    blk = pl.BlockSpec((tr, cols), lambda i: (i, 0))
    shape = jax.ShapeDtypeStruct((rows, cols), F32)
    return pl.pallas_call(
        body, grid=(rows // tr,), in_specs=[blk3, blk, blk3, blk3], out_specs=[blk] * 4, out_shape=[shape] * 4,
        compiler_params=_params(("parallel",)), name=name)(w, g, m, v)


def _small_reduce(name, parts, width):
    def body(*refs):
        out = refs[-1]
        out[...] = jnp.zeros_like(out)
        for k, r in enumerate(refs[:-1]):
            out[pl.ds(k, 1), :] = jnp.sum(r[...], axis=0, keepdims=True)

    vmem = pl.BlockSpec(memory_space=pltpu.VMEM)
    return pl.pallas_call(body, in_specs=[vmem] * len(parts), out_specs=vmem,
                          out_shape=jax.ShapeDtypeStruct((8, width), F32), name=name)(*parts)


def _small_adamw(name, gathered, w, m, v, loss_row, loss_scale):
    def body(gt_ref, w_ref, m_ref, v_ref, g_out, d_out, m_out, v_out, loss_out):
        g = gt_ref[0]
        for k in range(1, N_DEV):
            g = g + gt_ref[k]
        g_out[...] = g
        row = lax.broadcasted_iota(jnp.int32, g.shape, 0)
        loss_out[...] = jnp.sum(jnp.where(row == loss_row, g, 0.0), keepdims=True) * loss_scale
        d_out[...], m_out[...], v_out[...] = _adamw_math(w_ref[...], g, m_ref[...], v_ref[...])

    vmem = pl.BlockSpec(memory_space=pltpu.VMEM)
    shape = jax.ShapeDtypeStruct(w.shape, F32)
    return pl.pallas_call(body, in_specs=[vmem] * 4, out_specs=[vmem] * 5,
                          out_shape=[shape] * 4 + [jax.ShapeDtypeStruct((1, 1), F32)], name=name)(gathered, w, m, v)


def _full_from_pieces(name, pieces):
    _, rows, cols = pieces.shape
    if name in ROW_SHARDED:
        return pieces.reshape(N_CHIPS * rows, cols)
    return pieces.transpose(1, 0, 2).reshape(rows, N_CHIPS * cols)


def _column_range(segments, lo, hi):
    out, start = [], 0
    for t in segments:
        a, b = max(lo, start), min(hi, start + t.shape[1])
        if a < b:
            out.append(t[:, a - start:b - start])
        start += t.shape[1]
    return out


def _repack_w_in(pieces, hd, d):
    hh = hd // HEAD_DIM
    segs = [pieces[p] for p in range(N_CHIPS)]
    total = N_CHIPS * pieces.shape[2]
    main = jnp.concatenate(_column_range(segs, 0, 6 * hd) + _column_range(segs, 6 * hd + hh, total), axis=1)
    f = jnp.concatenate(_column_range(segs, 6 * hd, 6 * hd + hh), axis=1)
    return main, jnp.pad(f, ((0, 0), (0, LANES - hh)))


def _dw_in_pieces(dw, dw_f, hd, d):
    hh = hd // HEAD_DIM
    segs = [dw[:, :6 * hd], dw_f[:, :hh], dw[:, 6 * hd:]]
    cs = (6 * hd + hh + 2 * d) // N_CHIPS
    return jnp.stack([jnp.concatenate(_column_range(segs, p * cs, (p + 1) * cs), axis=1) for p in range(N_CHIPS)])


def _small_pack(vals, width):
    rows = []
    for name in SMALL:
        t = vals[name].reshape(1, -1)
        rows.append(jnp.pad(t, ((0, 0), (0, width - t.shape[1]))))
    rows.append(jnp.zeros((8 - len(SMALL), width), F32))
    return jnp.concatenate(rows, axis=0)


def kernel(x, ffn1_norm, ffn1_w_gate, ffn1_w_up, ffn1_w_down, mix_norm, w_in, b_forget, b_gate_dil, b_gate_fox, w_proj_dil, w_proj_fox, w_out, ffn2_norm, ffn2_w_gate, ffn2_w_up, ffn2_w_down, final_norm, loss_target, m_ffn1_norm, m_ffn1_w_gate, m_ffn1_w_up, m_ffn1_w_down, m_mix_norm, m_w_in, m_b_forget, m_b_gate_dil, m_b_gate_fox, m_w_proj_dil, m_w_proj_fox, m_w_out, m_ffn2_norm, m_ffn2_w_gate, m_ffn2_w_up, m_ffn2_w_down, m_final_norm, v_ffn1_norm, v_ffn1_w_gate, v_ffn1_w_up, v_ffn1_w_down, v_mix_norm, v_w_in, v_b_forget, v_b_gate_dil, v_b_gate_fox, v_w_proj_dil, v_w_proj_fox, v_w_out, v_ffn2_norm, v_ffn2_w_gate, v_ffn2_w_up, v_ffn2_w_down, v_final_norm):
    given = dict(locals())
    wts = {n: given[n] for n in WEIGHTS}
    mom_m = {n: given["m_" + n] for n in WEIGHTS}
    mom_v = {n: given["v_" + n] for n in WEIGHTS}
    d = x.shape[2]

    shards = {n: wts[n][0].astype(BF) for n in SHARDED}
    small = {n: wts[n] for n in SMALL}
    grad_x, grads, partials = _device_step(x[0], loss_target[0], shards, small)

    out_g, out_d, out_m, out_v = {}, {}, {}, {}
    for n in SHARDED:
        outs = _adamw("adamw_" + n, wts[n], grads[n], mom_m[n], mom_v[n])
        out_g[n], out_d[n], out_m[n], out_v[n] = (t[None] for t in outs)

    width = d
    part_rows = []
    for n in SMALL:
        t = partials[n]
        part_rows.append(jnp.pad(t, ((0, 0), (0, width - t.shape[1]))))
    part_rows.append(partials["sq"])
    local_small = _small_reduce("small_reduce", part_rows, width)
    gathered_small = _gather_all("small_gather", local_small)
    sg, sd_, sm, sv, loss = _small_adamw("small_adamw", gathered_small, _small_pack(wts, width),
                                         _small_pack(mom_m, width), _small_pack(mom_v, width), len(SMALL), 0.5 / d)
    for k, n in enumerate(SMALL):
        shp = wts[n].shape
        take = lambda t: t[k, :shp[-1]].reshape(shp)
        out_g[n], out_d[n], out_m[n], out_v[n] = take(sg), take(sd_), take(sm), take(sv)
    return (loss[0, 0], grad_x[None], *[out_g[n] for n in WEIGHTS], *[out_d[n] for n in WEIGHTS],
            *[out_m[n] for n in WEIGHTS], *[out_v[n] for n in WEIGHTS])
```

```python
import functools
import math

import jax
import jax.numpy as jnp
from jax import lax
from jax.experimental import pallas as pl
from jax.experimental.pallas import tpu as pltpu

HEAD_DIM = 128
ROPE_DIM = HEAD_DIM // 4
ROPE_THETA = 500000.0
DIL_PATTERNS = ((128, 1), (512, 4), (2048, 16))
MAX_WINDOW = 2048
NORM_EPS = 1e-6
ADAM_LR = 0.001
ADAM_B1 = 0.9
ADAM_B2 = 0.999
ADAM_EPS = 1e-08
ADAM_WD = 0.01
ADAM_STEP = 10

BF = jnp.bfloat16
F32 = jnp.float32
NEG = -1e30
LANES = 128
ATT_BLOCK = 512
PREP_COLS = 512
ATT_HEADS = 2
VMEM_LIMIT = 56 * 1024 * 1024
MM_VMEM_BUDGET = 46 * 1024 * 1024
N_CHIPS = 4
N_DEV = 8
MESH = pl.DeviceIdType.MESH

SHARDED = ("ffn1_w_gate", "ffn1_w_up", "ffn1_w_down", "w_in", "w_proj_dil", "w_proj_fox", "w_out",
           "ffn2_w_gate", "ffn2_w_up", "ffn2_w_down")
ROW_SHARDED = ("ffn1_w_down", "w_out", "ffn2_w_down")
SMALL = ("ffn1_norm", "mix_norm", "b_forget", "b_gate_dil", "b_gate_fox", "ffn2_norm", "final_norm")
WEIGHTS = ("ffn1_norm", "ffn1_w_gate", "ffn1_w_up", "ffn1_w_down", "mix_norm", "w_in", "b_forget",
           "b_gate_dil", "b_gate_fox", "w_proj_dil", "w_proj_fox", "w_out", "ffn2_norm", "ffn2_w_gate",
           "ffn2_w_up", "ffn2_w_down", "final_norm")


def _pick(n, target, align):
    best = None
    for d in range(align, min(n, target) + 1, align):
        if n % d == 0:
            best = d
    return n if best is None else best


def _params(sem=None):
    return pltpu.CompilerParams(dimension_semantics=sem, vmem_limit_bytes=VMEM_LIMIT)


_DIMS = {"nn": (((1,), (0,)), ((), ())), "nt": (((1,), (1,)), ((), ())), "tn": (((0,), (0,)), ((), ()))}


class _SemsFrom:
    def __init__(self, sems, first):
        self.sems, self.first = sems, first

    @property
    def at(self):
        return self

    def __getitem__(self, k):
        return self.sems.at[self.first + k]


class _Side:
    def __init__(self, inputs, out_shapes, n_sems, start, finish, aliases=None):
        self.inputs, self.out_shapes, self.n_sems = list(inputs), list(out_shapes), n_sems
        self.start, self.finish, self.aliases = start, finish, aliases or {}

    def scratch(self):
        return [pltpu.SemaphoreType.DMA((self.n_sems,)), pltpu.SemaphoreType.DMA((self.n_sems,))]

    def beside(self, other):
        n_in, n_out, n_sems = len(self.inputs), len(self.out_shapes), self.n_sems

        def part(fn_a, fn_b):
            def run(ins, outs, send_sems, recv_sems):
                fn_a(ins[:n_in], outs[:n_out], send_sems, recv_sems)
                fn_b(ins[n_in:], outs[n_out:], _SemsFrom(send_sems, n_sems), _SemsFrom(recv_sems, n_sems))
            return run

        aliases = {**self.aliases, **{n_in + t: n_out + o for t, o in other.aliases.items()}}
        return _Side(self.inputs + other.inputs, self.out_shapes + other.out_shapes, n_sems + other.n_sems,
                     part(self.start, other.start), part(self.finish, other.finish), aliases=aliases)

    def call(self, name):
        n_in, n_out = len(self.inputs), len(self.out_shapes)

        def body(*refs):
            ins, outs, sems = refs[:n_in], refs[n_in:n_in + n_out], refs[n_in + n_out:]
            self.start(ins, outs, *sems)
            self.finish(ins, outs, *sems)

        return pl.pallas_call(body, in_specs=[ANY_SPEC] * n_in, out_specs=[ANY_SPEC] * n_out, out_shape=self.out_shapes,
                              input_output_aliases=self.aliases, scratch_shapes=self.scratch(), name=name)(*self.inputs)


def _mm(name, pairs, mode, m, n, k, *, epilogue, out_dtypes, extras=(), rows=(), n_colsum=0,
        sum_pairs=False, tm=1024, tn=1152, tk=2048, piece_layout=False, side=None, b_off=0, keep_tn=False):
    m_align = LANES if mode == "tn" else 8
    tm = _pick(m, tm, m_align)
    tn = n // N_CHIPS if piece_layout else _pick(n, tn, LANES)
    tk = _pick(k, tk, LANES)
    n_acc = 1 if sum_pairs else len(pairs)
    lhs = []
    for a, _ in pairs:
        if not any(a is t for t in lhs):
            lhs.append(a)
    lhs_of = [next(t for t in range(len(lhs)) if lhs[t] is a) for a, _ in pairs]
    n_mm = len(lhs) + len(pairs)
    n_in = n_mm + len(extras) + len(rows)
    n_out = len(out_dtypes) + n_colsum

    def vmem_bytes(tm_, tn_, tk_):
        tiles = sum(tm_ * tk_ * a.dtype.itemsize for a in lhs) + sum(tn_ * tk_ * b.dtype.itemsize for _, b in pairs)
        tiles += sum(tm_ * (arr.shape[1] if off is None else tn_) * arr.dtype.itemsize for arr, off in extras)
        tiles += sum(tm_ * tn_ * jnp.dtype(dt).itemsize for dt in out_dtypes)
        return 2 * tiles + (n_acc + len(extras) + len(out_dtypes)) * tm_ * tn_ * 4

    while vmem_bytes(tm, tn, tk) > MM_VMEM_BUDGET:
        if tn > 512 and not piece_layout and not keep_tn:
            tn = _pick(n, tn - LANES, LANES)
        elif tm > 512:
            tm = _pick(m, tm - m_align, m_align)
        elif tk > 512:
            tk = _pick(k, tk - LANES, LANES)
        elif tm > 256:
            tm = _pick(m, tm - m_align, m_align)
        else:
            break
    nk = k // tk

    n_side_in = len(side.inputs) if side else 0
    n_side_out = len(side.out_shapes) if side else 0
    grid = (m // tm, n // tn, nk)

    def body(*refs):
        ins, refs = refs[:n_in], refs[n_in:]
        side_ins, refs = refs[:n_side_in], refs[n_side_in:]
        outs, refs = refs[:n_out], refs[n_out:]
        side_outs, refs = refs[:n_side_out], refs[n_side_out:]
        accs, side_sems = refs[:n_acc], refs[n_acc:]
        kk = pl.program_id(2)
        if side:
            at = [pl.program_id(t) for t in range(3)]

            @pl.when(jnp.logical_and(jnp.logical_and(at[0] == 0, at[1] == 0), at[2] == 0))
            def _():
                side.start(side_ins, side_outs, *side_sems)

        @pl.when(kk == 0)
        def _():
            for acc in accs:
                acc[...] = jnp.zeros_like(acc)

        a_tiles = [r[...].astype(BF) for r in ins[:len(lhs)]]
        for p in range(len(pairs)):
            b = ins[len(lhs) + p][...].astype(BF)
            accs[0 if sum_pairs else p][...] += lax.dot_general(a_tiles[lhs_of[p]], b, _DIMS[mode],
                                                                preferred_element_type=F32)

        @pl.when(kk == nk - 1)
        def _():
            ex = [r[...] for r in ins[n_mm:n_mm + len(extras)]]
            rw = [r[...] for r in ins[n_mm + len(extras):]]
            res = epilogue([acc[...] for acc in accs], ex, rw)
            for o, r in zip(outs, res):
                o[...] = r.astype(o.dtype)

        if side:
            @pl.when(jnp.logical_and(jnp.logical_and(at[0] == grid[0] - 1, at[1] == grid[1] - 1), at[2] == nk - 1))
            def _():
                side.finish(side_ins, side_outs, *side_sems)

    in_specs, args = [], []
    for a in lhs:
        if mode == "tn":
            in_specs.append(pl.BlockSpec((tk, tm), lambda i, j, kk: (kk, i)))
        else:
            in_specs.append(pl.BlockSpec((tm, tk), lambda i, j, kk: (i, kk)))
        args.append(a)
    assert b_off % tn == 0 and (b_off == 0 or mode == "nn")
    for _, b in pairs:
        if mode == "nt":
            in_specs.append(pl.BlockSpec((tn, tk), lambda i, j, kk: (j, kk)))
        else:
            in_specs.append(pl.BlockSpec((tk, tn), functools.partial(lambda i, j, kk, o: (kk, j + o), o=b_off // tn)))
        args.append(b)
    for arr, off in extras:
        if off is None:
            in_specs.append(pl.BlockSpec((tm, arr.shape[1]), lambda i, j, kk: (i, 0)))
        else:
            assert off % tn == 0
            in_specs.append(pl.BlockSpec((tm, tn), functools.partial(lambda i, j, kk, o: (i, j + o), o=off // tn)))
        args.append(arr)
    for arr in rows:
        in_specs.append(pl.BlockSpec((1, tn), lambda i, j, kk: (0, j)))
        args.append(arr)
    if piece_layout:
        out_specs = [pl.BlockSpec((None, tm, tn), lambda i, j, kk: (j, i, 0)) for _ in out_dtypes]
        out_shape = [jax.ShapeDtypeStruct((n // tn, m, tn), d) for d in out_dtypes]
    else:
        out_specs = [pl.BlockSpec((tm, tn), lambda i, j, kk: (i, j)) for _ in out_dtypes]
        out_shape = [jax.ShapeDtypeStruct((m, n), d) for d in out_dtypes]
    for _ in range(n_colsum):
        out_specs.append(pl.BlockSpec((None, 1, tn), lambda i, j, kk: (i, 0, j)))
        out_shape.append(jax.ShapeDtypeStruct((m // tm, 1, n), F32))
    scratch = [pltpu.VMEM((tm, tn), F32) for _ in range(n_acc)]
    if side is None:
        return pl.pallas_call(
            body, grid=grid, in_specs=in_specs, out_specs=out_specs, out_shape=out_shape, scratch_shapes=scratch,
            compiler_params=_params(("parallel", "parallel", "arbitrary")), name=name)(*args)
    res = pl.pallas_call(
        body, grid=grid, in_specs=in_specs + [ANY_SPEC] * n_side_in, out_specs=out_specs + [ANY_SPEC] * n_side_out,
        out_shape=out_shape + side.out_shapes, scratch_shapes=scratch + side.scratch(),
        input_output_aliases={n_in + t: n_out + o for t, o in side.aliases.items()},
        compiler_params=_params(("arbitrary", "arbitrary", "arbitrary")), name=name)(*args, *side.inputs)
    return res[:n_out], res[n_out:]


def _col_pieces(full):
    rows, cols = full.shape
    return full.reshape(rows, N_CHIPS, cols // N_CHIPS).transpose(1, 0, 2)


def _dw_col_pieces(name, a, b, m, n, k, side=None):
    ident = lambda accs, ex, rw: (accs[0],)
    aligned = (n // N_CHIPS) % LANES == 0
    res = _mm(name, [(a, b)], "tn", m, n, k, epilogue=ident, out_dtypes=(F32,), tm=512 if aligned else 1024,
              piece_layout=aligned, side=side)
    (out,), side_res = res if side else (res, None)
    out = out if aligned else _col_pieces(out)
    return (out, side_res) if side else out


def _hosted(sides, got, key, call):
    make = sides.get(key) if sides else None
    if make is None:
        return call(None)
    outs, got[key] = call(make(got))
    return outs


def _sigmoid(z):
    return 0.5 * jnp.tanh(0.5 * z) + 0.5


def _row_tile(s):
    return _pick(s, 256, 8)


def _fold8(t):
    r, d = t.shape
    return jnp.sum(t.reshape(r // 8, 8, d), axis=0)


def _rms_fwd(name, x, g):
    s, d = x.shape
    tr = _row_tile(s)

    def body(x_ref, g_ref, h_ref):
        xf = x_ref[...]
        y = xf * lax.rsqrt(jnp.mean(xf * xf, axis=-1, keepdims=True) + NORM_EPS)
        h_ref[...] = (y * g_ref[...]).astype(BF)

    return pl.pallas_call(
        body, grid=(s // tr,),
        in_specs=[pl.BlockSpec((tr, d), lambda i: (i, 0)), pl.BlockSpec((1, d), lambda i: (0, 0))],
        out_specs=pl.BlockSpec((tr, d), lambda i: (i, 0)), out_shape=jax.ShapeDtypeStruct((s, d), BF),
        compiler_params=_params(("parallel",)), name=name)(x, g)


def _rms_bwd(name, x, g, dh, dres, side=None):
    s, d = x.shape
    tr = _row_tile(s)
    steps = s // tr
    n_side_in = len(side.inputs) if side else 0
    n_side_out = len(side.out_shapes) if side else 0

    def body(*refs):
        x_ref, g_ref, dh_ref, dres_ref = refs[:4]
        side_ins, refs = refs[4:4 + n_side_in], refs[4 + n_side_in:]
        dx_ref, dxb_ref, dg_ref = refs[:3]
        side_outs, side_sems = refs[3:3 + n_side_out], refs[3 + n_side_out:]

        @pl.when(pl.program_id(0) == 0)
        def _():
            dg_ref[...] = jnp.zeros_like(dg_ref)
            if side:
                side.start(side_ins, side_outs, *side_sems)

        xf = x_ref[...]
        rstd = lax.rsqrt(jnp.mean(xf * xf, axis=-1, keepdims=True) + NORM_EPS)
        xhat = xf * rstd
        dhf = dh_ref[...]
        dg_ref[...] += _fold8(dhf * xhat)
        dxh = dhf * g_ref[...]
        dx = dres_ref[...] + rstd * (dxh - xhat * jnp.mean(dxh * xhat, axis=-1, keepdims=True))
        dx_ref[...] = dx
        dxb_ref[...] = dx.astype(BF)

        if side:
            @pl.when(pl.program_id(0) == steps - 1)
            def _():
                side.finish(side_ins, side_outs, *side_sems)

    blk = pl.BlockSpec((tr, d), lambda i: (i, 0))
    res = pl.pallas_call(
        body, grid=(steps,),
        in_specs=[blk, pl.BlockSpec((1, d), lambda i: (0, 0)), blk, blk] + [ANY_SPEC] * n_side_in,
        out_specs=[blk, blk, pl.BlockSpec((8, d), lambda i: (0, 0))] + [ANY_SPEC] * n_side_out,
        out_shape=[jax.ShapeDtypeStruct((s, d), F32), jax.ShapeDtypeStruct((s, d), BF),
                   jax.ShapeDtypeStruct((8, d), F32)] + (side.out_shapes if side else []),
        scratch_shapes=side.scratch() if side else [],
        compiler_params=_params(("arbitrary",)), name=name)(x, g, dh, dres, *(side.inputs if side else []))
    return (res[:3], res[3:]) if side else res


def _final(name, x, g, tgt):
    s, d = x.shape
    tr = _row_tile(s)

    def body(x_ref, g_ref, t_ref, dx_ref, dxb_ref, dg_ref, sq_ref):
        @pl.when(pl.program_id(0) == 0)
        def _():
            dg_ref[...] = jnp.zeros_like(dg_ref)
            sq_ref[...] = jnp.zeros_like(sq_ref)

        xf = x_ref[...]
        rstd = lax.rsqrt(jnp.mean(xf * xf, axis=-1, keepdims=True) + NORM_EPS)
        xhat = xf * rstd
        gf = g_ref[...]
        err = xhat * gf - t_ref[...]
        sq_ref[...] += _fold8(err * err)
        dy = err * (1.0 / d)
        dg_ref[...] += _fold8(dy * xhat)
        dxh = dy * gf
        dx = rstd * (dxh - xhat * jnp.mean(dxh * xhat, axis=-1, keepdims=True))
        dx_ref[...] = dx
        dxb_ref[...] = dx.astype(BF)

    blk = pl.BlockSpec((tr, d), lambda i: (i, 0))
    acc = pl.BlockSpec((8, d), lambda i: (0, 0))
    return pl.pallas_call(
        body, grid=(s // tr,), in_specs=[blk, pl.BlockSpec((1, d), lambda i: (0, 0)), blk],
        out_specs=[blk, blk, acc, acc],
        out_shape=[jax.ShapeDtypeStruct((s, d), F32), jax.ShapeDtypeStruct((s, d), BF),
                   jax.ShapeDtypeStruct((8, d), F32), jax.ShapeDtypeStruct((8, d), F32)],
        compiler_params=_params(("arbitrary",)), name=name)(x, g, tgt)


def _ffn_fwd(tag, x, g, w_gate, w_up, get_w_down, sides=None):
    s, d = x.shape
    f = w_gate.shape[1]
    got = {}
    h = _rms_fwd(tag + "_rms", x, g)

    def up_epi(accs, ex, rw):
        a, b = accs
        return a, b, a * _sigmoid(a) * b

    a, b, act = _hosted(sides, got, "up", lambda side: _mm(
        tag + "_up", [(h, w_gate), (h, w_up)], "nn", s, f, d, epilogue=up_epi, out_dtypes=(BF, BF, BF), side=side))

    def down_epi(accs, ex, rw):
        return (ex[0] + 0.5 * accs[0],)

    w_down = get_w_down(got)
    (y,) = _hosted(sides, got, "down", lambda side: _mm(
        tag + "_down", [(act, w_down)], "nn", s, d, f, epilogue=down_epi, out_dtypes=(F32,), extras=[(x, 0)],
        side=side))
    return y, (h, a, b, act), got


def _ffn_bwd(tag, x, g, w_gate, w_up, w_down, saved, dy, dy_bf, sides=None):
    s, d = x.shape
    f = w_gate.shape[1]
    h, a, b, act = saved
    got = {}

    def act_epi(accs, ex, rw):
        dact = 0.5 * accs[0]
        av, bv = ex[0].astype(F32), ex[1].astype(F32)
        sg = _sigmoid(av)
        return dact * bv * (sg * (1.0 + av * (1.0 - sg))), dact * (av * sg)

    da, db = _hosted(sides, got, "dact", lambda side: _mm(
        tag + "_dact", [(dy_bf, w_down)], "nt", s, f, d, epilogue=act_epi, out_dtypes=(BF, BF),
        extras=[(a, 0), (b, 0)], side=side))
    ident = lambda accs, ex, rw: (accs[0],)
    (dw_down,) = _hosted(sides, got, "dwd", lambda side: _mm(
        tag + "_dwd", [(act, dy_bf)], "tn", f, d, s, epilogue=lambda accs, ex, rw: (0.5 * accs[0],),
        out_dtypes=(F32,), side=side))
    got["dw_down"] = dw_down = dw_down.reshape(N_CHIPS, f // N_CHIPS, d)
    got["dw_gate"] = dw_gate = _hosted(sides, got, "dwg", lambda side: _dw_col_pieces(
        tag + "_dwg", h, da, d, f, s, side=side))
    got["dw_up"] = dw_up = _hosted(sides, got, "dwu", lambda side: _dw_col_pieces(
        tag + "_dwu", h, db, d, f, s, side=side))
    (dh,) = _hosted(sides, got, "dh", lambda side: _mm(
        tag + "_dh", [(da, w_gate), (db, w_up)], "nt", s, d, f, epilogue=ident, out_dtypes=(F32,), sum_pairs=True,
        side=side))
    dx, dx_bf, dg = _hosted(sides, got, "rmsb", lambda side: _rms_bwd(tag + "_rmsb", x, g, dh, dy, side=side))
    return dx, dx_bf, dg, dw_gate, dw_up, dw_down, got


def _rope_tables(s):
    half = ROPE_DIM // 2
    pos = jnp.arange(s, dtype=F32)
    inv_freq = ROPE_THETA ** (-jnp.arange(0, ROPE_DIM, 2, dtype=F32) / ROPE_DIM)
    ang = pos[:, None] * inv_freq[None, :]
    cos, sin = jnp.cos(ang), jnp.sin(ang)
    rest = HEAD_DIM - ROPE_DIM
    cos_t = jnp.concatenate([cos, cos, jnp.ones((s, rest), F32)], axis=-1)
    sin_t = jnp.concatenate([-sin, sin, jnp.zeros((s, rest), F32)], axis=-1)
    return cos_t, sin_t


def _swap_halves(t):
    lane = lax.broadcasted_iota(jnp.int32, t.shape, 1) & (HEAD_DIM - 1)
    half = ROPE_DIM // 2
    return jnp.where(lane < half, pltpu.roll(t, t.shape[1] - half, 1), pltpu.roll(t, half, 1))


def _dil_bias(blk):
    n_delta = MAX_WINDOW // blk + 1
    delta = jnp.arange(n_delta, dtype=jnp.int32)[:, None, None]
    r = jnp.arange(blk, dtype=jnp.int32)[None, None, :]
    c = jnp.arange(blk, dtype=jnp.int32)[None, :, None]
    o = delta * blk + r - c
    mult = jnp.zeros(o.shape, F32)
    for w, dd in DIL_PATTERNS:
        mult = mult + ((o >= 0) & (o <= w) & (o % dd == 0)).astype(F32)
    return jnp.where(mult > 0, jnp.log(jnp.maximum(mult, 1.0)), NEG)


def _att_block(s):
    return _pick(s, ATT_BLOCK, LANES)


def _fox_aug(name, c_pad, qkv, q_off, k_off, n_heads):
    s = c_pad.shape[0]
    tr = _pick(s, 1024, 16)

    def body(c_ref, q_ref, k_ref, qc_ref, kc_ref):
        h = pl.program_id(1)
        lane = lax.broadcasted_iota(jnp.int32, (tr, LANES), 1)
        ch = jnp.sum(jnp.where(lane == h, c_ref[...], 0.0), axis=1, keepdims=True)
        hi, mid, lo = (t.astype(F32) for t in _split3(ch))
        zero = jnp.zeros((tr, LANES), F32)
        is_hi = jnp.logical_or(lane == 0, lane == 3)
        is_mid = jnp.logical_or(lane == 1, lane == 4)
        parts = jnp.where(is_hi, hi, jnp.where(is_mid, mid, lo))
        qc_ref[:, :HEAD_DIM] = q_ref[...]
        kc_ref[:, :HEAD_DIM] = k_ref[...]
        qc_ref[:, HEAD_DIM:] = jnp.where(lane < 3, 1.0, jnp.where(lane < 6, parts, zero)).astype(BF)
        kc_ref[:, HEAD_DIM:] = jnp.where(lane < 3, -parts, jnp.where(lane < 6, 1.0, zero)).astype(BF)

    head = lambda o: pl.BlockSpec((tr, HEAD_DIM), functools.partial(lambda i, h, o: (i, o + h), o=o))
    spec = pl.BlockSpec((tr, 2 * HEAD_DIM), lambda i, h: (i, h))
    shape = jax.ShapeDtypeStruct((s, n_heads * 2 * HEAD_DIM), BF)
    return pl.pallas_call(
        body, grid=(s // tr, n_heads),
        in_specs=[pl.BlockSpec((tr, LANES), lambda i, h: (i, 0)), head(q_off), head(k_off)],
        out_specs=[spec, spec], out_shape=[shape, shape],
        compiler_params=_params(("parallel", "arbitrary")), name=name)(c_pad, qkv, qkv)


def _causal_mask(st):
    kpos = lax.broadcasted_iota(jnp.int32, st.shape, 0)
    qpos = lax.broadcasted_iota(jnp.int32, st.shape, 1)
    return jnp.where(kpos <= qpos, st, NEG)


def _flash_fwd(name, q_src, k_src, v_src, n_heads, *, fox, tab_t=None):
    (q_arr, q_off, qw), (k_arr, k_off, kw), (v_arr, v_off, _) = q_src, k_src, v_src
    s = q_arr.shape[0]
    blk = _att_block(s)
    nq = s // blk
    n_delta = MAX_WINDOW // blk + 1
    grp = ATT_HEADS
    assert qw == kw and n_heads % grp == 0 and q_off % grp == 0 and k_off % grp == 0 and v_off % grp == 0
    wide = grp * HEAD_DIM

    def body(*refs):
        if fox:
            q_ref, k_ref, v_ref, o_ref, lse_ref, acc, m_s, l_s = refs
        else:
            q_ref, k_ref, v_ref, tab_ref, o_ref, lse_ref, acc, m_s, l_s = refs
        i = pl.program_id(1)
        acc[...] = jnp.zeros_like(acc)
        m_s[...] = jnp.full_like(m_s, NEG)
        l_s[...] = jnp.zeros_like(l_s)

        def step(j, diagonal):
            ks = pl.ds(pl.multiple_of(j * blk, blk), blk)
            for g in range(grp):
                cols = slice(g * HEAD_DIM, (g + 1) * HEAD_DIM)
                qk_cols = slice(g * qw, (g + 1) * qw)
                st = lax.dot_general(k_ref[ks, qk_cols], q_ref[:, qk_cols], _DIMS["nt"], preferred_element_type=F32)
                if fox:
                    if diagonal:
                        st = _causal_mask(st)
                else:
                    st = st + tab_ref[i - j]
                m_prev = m_s[g]
                m_new = jnp.maximum(m_prev, jnp.max(st, axis=0, keepdims=True))
                alpha = jnp.exp(m_prev - m_new)
                p = jnp.exp(st - m_new)
                l_s[g] = alpha * l_s[g] + jnp.sum(p, axis=0, keepdims=True)
                acc[g] = alpha * acc[g] + lax.dot_general(v_ref[ks, cols], p.astype(BF), _DIMS["tn"],
                                                          preferred_element_type=F32)
                m_s[g] = m_new

        def loop_step(j, carry):
            step(j, False)
            return carry

        if fox:
            lax.fori_loop(0, i, loop_step, 0)
            step(i, True)
        else:
            lax.fori_loop(jnp.maximum(i - (n_delta - 1), 0), i + 1, loop_step, 0)
        for g in range(grp):
            o_ref[:, g * HEAD_DIM:(g + 1) * HEAD_DIM] = (acc[g] / l_s[g]).T.astype(o_ref.dtype)
            lse_ref[g] = m_s[g] + jnp.log(l_s[g])

    off = lambda o: functools.partial(lambda h, i, o: (0, o + h), o=o // grp)
    in_specs = [pl.BlockSpec((blk, grp * qw), functools.partial(lambda h, i, o: (i, o + h), o=q_off // grp)),
                pl.BlockSpec((s, grp * kw), off(k_off)), pl.BlockSpec((s, wide), off(v_off))]
    args = [q_arr, k_arr, v_arr]
    if not fox:
        in_specs.append(pl.BlockSpec((n_delta, blk, blk), lambda h, i: (0, 0, 0)))
        args.append(tab_t)
    return pl.pallas_call(
        body, grid=(n_heads // grp, nq), in_specs=in_specs,
        out_specs=[pl.BlockSpec((blk, wide), lambda h, i: (i, h)),
                   pl.BlockSpec((grp, None, 1, blk), lambda h, i: (h, i, 0, 0))],
        out_shape=[jax.ShapeDtypeStruct((s, n_heads * HEAD_DIM), BF),
                   jax.ShapeDtypeStruct((n_heads, nq, 1, blk), F32)],
        scratch_shapes=[pltpu.VMEM((grp, HEAD_DIM, blk), F32), pltpu.VMEM((grp, 1, blk), F32),
                        pltpu.VMEM((grp, 1, blk), F32)],
        compiler_params=_params(("parallel", "parallel")), name=name)(*args)


def _att_delta(name, do, o, n_heads):
    s = do.shape[0]
    blk = _pick(s, 1024, 16)

    def body(do_ref, o_ref, d_ref):
        d_ref[...] = jnp.sum(do_ref[...].astype(F32) * o_ref[...].astype(F32), axis=-1, keepdims=True)

    spec = pl.BlockSpec((blk, HEAD_DIM), lambda h, i: (i, h))
    return pl.pallas_call(
        body, grid=(n_heads, s // blk), in_specs=[spec, spec],
        out_specs=pl.BlockSpec((None, blk, 1), lambda h, i: (h, i, 0)),
        out_shape=jax.ShapeDtypeStruct((n_heads, s, 1), F32),
        compiler_params=_params(("parallel", "parallel")), name=name)(do, o)


def _flash_bwd(name, q_src, k_src, v_src, do, lse_row, delta_row, n_heads, *, fox, tab_t=None):
    (q_arr, q_off, qw), (k_arr, k_off, kw), (v_arr, v_off, _) = q_src, k_src, v_src
    s = q_arr.shape[0]
    blk = _att_block(s)
    nq = s // blk
    n_delta = MAX_WINDOW // blk + 1
    grp = ATT_HEADS
    assert qw == kw and n_heads % grp == 0 and q_off % grp == 0 and k_off % grp == 0 and v_off % grp == 0
    wide = grp * HEAD_DIM

    def body(*refs):
        if fox:
            (q_ref, do_ref, k_ref, v_ref, lse_ref, dl_ref,
             dq_ref, dk_ref, dv_ref, dc_ref, dcq_ref, dk_acc, dv_acc, dc_acc) = refs
        else:
            (q_ref, do_ref, k_ref, v_ref, lse_ref, dl_ref, tab_ref,
             dq_ref, dk_ref, dv_ref, dk_acc, dv_acc) = refs
        j = pl.program_id(1)

        @pl.when(j == 0)
        def _():
            dq_ref[...] = jnp.zeros_like(dq_ref)
            if fox:
                dcq_ref[...] = jnp.zeros_like(dcq_ref)

        dk_acc[...] = jnp.zeros_like(dk_acc)
        dv_acc[...] = jnp.zeros_like(dv_acc)
        if fox:
            dc_acc[...] = jnp.zeros_like(dc_acc)

        def step(i, diagonal):
            qs = pl.ds(pl.multiple_of(i * blk, blk), blk)
            for g in range(grp):
                cols = slice(g * HEAD_DIM, (g + 1) * HEAD_DIM)
                qk_cols = slice(g * qw, (g + 1) * qw)
                plain = slice(g * qw, g * qw + HEAD_DIM)
                kb, vb = k_ref[:, plain], v_ref[:, cols]
                qb, dob = q_ref[qs, plain], do_ref[qs, cols]
                st = lax.dot_general(k_ref[:, qk_cols], q_ref[qs, qk_cols], _DIMS["nt"], preferred_element_type=F32)
                if fox:
                    if diagonal:
                        st = _causal_mask(st)
                else:
                    st = st + tab_ref[i - j]
                pt = jnp.exp(st - lse_ref[g, i])
                dv_acc[:, cols] += jnp.dot(pt.astype(BF), dob, preferred_element_type=F32)
                dpt = lax.dot_general(vb, dob, _DIMS["nt"], preferred_element_type=F32)
                dst = pt * (dpt - dl_ref[g, i])
                dsb = dst.astype(BF)
                dk_acc[:, cols] += jnp.dot(dsb, qb, preferred_element_type=F32)
                dq_ref[qs, cols] += lax.dot_general(dsb, kb, _DIMS["tn"], preferred_element_type=F32)
                if fox:
                    folded = dst[:, :LANES]
                    for part in range(1, blk // LANES):
                        folded = folded + dst[:, part * LANES:(part + 1) * LANES]
                    dc_acc[g] -= folded
                    dcq_ref[g, i] += jnp.sum(dst, axis=0, keepdims=True)

        def loop_step(i, carry):
            step(i, False)
            return carry

        if fox:
            step(j, True)
            lax.fori_loop(j + 1, nq, loop_step, 0)
        else:
            lax.fori_loop(j, jnp.minimum(nq, j + n_delta), loop_step, 0)
        dk_ref[...] = dk_acc[...]
        dv_ref[...] = dv_acc[...].astype(dv_ref.dtype)
        if fox:
            for g in range(grp):
                dc_ref[g] = jnp.sum(dc_acc[g], axis=-1, keepdims=True)

    full = lambda o, wd=wide: pl.BlockSpec((s, wd), functools.partial(lambda h, j, o: (0, o + h), o=o // grp))
    tile = lambda o, wd=wide: pl.BlockSpec((blk, wd), functools.partial(lambda h, j, o: (j, o + h), o=o // grp))
    per_q = pl.BlockSpec((grp, nq, 1, blk), lambda h, j: (h, 0, 0, 0))
    per_k = pl.BlockSpec((grp, blk, 1), lambda h, j: (h, j, 0))
    in_specs = [full(q_off, grp * qw), full(0), tile(k_off, grp * kw), tile(v_off), per_q, per_q]
    args = [q_arr, do, k_arr, v_arr, lse_row, delta_row]
    out_specs = [full(0), tile(0), tile(0)]
    hd = n_heads * HEAD_DIM
    out_shape = [jax.ShapeDtypeStruct((s, hd), F32), jax.ShapeDtypeStruct((s, hd), F32),
                 jax.ShapeDtypeStruct((s, hd), BF)]
    scratch = [pltpu.VMEM((blk, wide), F32), pltpu.VMEM((blk, wide), F32)]
    if fox:
        out_specs += [per_k, per_q]
        out_shape += [jax.ShapeDtypeStruct((n_heads, s, 1), F32), jax.ShapeDtypeStruct((n_heads, nq, 1, blk), F32)]
        scratch.append(pltpu.VMEM((grp, blk, LANES), F32))
    else:
        in_specs.append(pl.BlockSpec((n_delta, blk, blk), lambda h, j: (0, 0, 0)))
        args.append(tab_t)
    return pl.pallas_call(
        body, grid=(n_heads // grp, nq), in_specs=in_specs, out_specs=out_specs, out_shape=out_shape,
        scratch_shapes=scratch, compiler_params=_params(("parallel", "arbitrary")), name=name)(*args)


def _split3(t):
    hi = t.astype(BF)
    r1 = t - hi.astype(F32)
    mid = r1.astype(BF)
    lo = (r1 - mid.astype(F32)).astype(BF)
    return hi, mid, lo


def _tri_dot(tri, t):
    hi, mid, lo = _split3(t)
    return (jnp.dot(tri, hi, preferred_element_type=F32) + jnp.dot(tri, mid, preferred_element_type=F32)
            + jnp.dot(tri, lo, preferred_element_type=F32))


def _log_sigmoid(z):
    return jnp.minimum(z, 0.0) - jnp.log(1.0 + jnp.exp(-jnp.abs(z)))


def _forget_cumsum(name, proj, f_col, bias):
    s = proj.shape[0]
    blk = _att_block(s)

    def body(f_ref, b_ref, c_ref, carry):
        @pl.when(pl.program_id(0) == 0)
        def _():
            carry[...] = jnp.zeros_like(carry)

        lf = _log_sigmoid(f_ref[...] + b_ref[...])
        r = lax.broadcasted_iota(jnp.int32, (blk, blk), 0)
        c = lax.broadcasted_iota(jnp.int32, (blk, blk), 1)
        tri = (c <= r).astype(BF)
        c_ref[...] = _tri_dot(tri, lf) + carry[...]
        carry[...] = c_ref[pl.ds(blk - 1, 1), :]

    return pl.pallas_call(
        body, grid=(s // blk,),
        in_specs=[pl.BlockSpec((blk, LANES), lambda i: (i, f_col)), pl.BlockSpec((1, LANES), lambda i: (0, 0))],
        out_specs=pl.BlockSpec((blk, LANES), lambda i: (i, 0)), out_shape=jax.ShapeDtypeStruct((s, LANES), F32),
        scratch_shapes=[pltpu.VMEM((1, LANES), F32)],
        compiler_params=_params(("arbitrary",)), name=name)(proj, bias)


def _forget_bwd(name, proj, f_col, bias, dc):
    s = proj.shape[0]
    blk = _att_block(s)
    nb = s // blk

    def body(f_ref, b_ref, dc_ref, df_ref, db_ref, carry):
        @pl.when(pl.program_id(0) == 0)
        def _():
            carry[...] = jnp.zeros_like(carry)
            db_ref[...] = jnp.zeros_like(db_ref)

        r = lax.broadcasted_iota(jnp.int32, (blk, blk), 0)
        c = lax.broadcasted_iota(jnp.int32, (blk, blk), 1)
        tri = (c >= r).astype(BF)
        r = lax.broadcasted_iota(jnp.int32, (blk, LANES), 0)
        dlf = _tri_dot(tri, dc_ref[...]) + carry[...]
        carry[...] = jnp.sum(jnp.where(r == 0, dlf, 0.0), axis=0, keepdims=True)
        dz = dlf * _sigmoid(-(f_ref[...] + b_ref[...]))
        df_ref[...] = dz.astype(BF)
        db_ref[...] += _fold8(dz)

    rev = lambda i: (nb - 1 - i, 0)
    return pl.pallas_call(
        body, grid=(nb,),
        in_specs=[pl.BlockSpec((blk, LANES), lambda i: (nb - 1 - i, f_col)), pl.BlockSpec((1, LANES), lambda i: (0, 0)),
                  pl.BlockSpec((blk, LANES), rev)],
        out_specs=[pl.BlockSpec((blk, LANES), rev), pl.BlockSpec((8, LANES), lambda i: (0, 0))],
        out_shape=[jax.ShapeDtypeStruct((s, LANES), BF), jax.ShapeDtypeStruct((8, LANES), F32)],
        scratch_shapes=[pltpu.VMEM((1, LANES), F32)],
        compiler_params=_params(("arbitrary",)), name=name)(proj, bias, dc)


def _dproj_assemble(name, parts, gates, cos_t, sin_t, n_heads, d_model):
    s = cos_t.shape[0]
    tr = _pick(s, 1024, 16)
    scale = HEAD_DIM ** -0.5
    hd = n_heads * HEAD_DIM
    cw = _pick(math.gcd(hd, d_model), PREP_COLS, HEAD_DIM)
    widths = [hd] * 6 + [d_model] * 2
    starts = [sum(widths[:t]) // cw for t in range(len(widths) + 1)]
    cos_w, sin_w = jnp.tile(cos_t, (1, cw // HEAD_DIM)), jnp.tile(sin_t, (1, cw // HEAD_DIM))

    def body(*refs):
        p_refs, cos_ref, sin_ref, o_ref = refs[:8], refs[8], refs[9], refs[10]
        j = pl.program_id(1)
        for kind in range(8):
            @pl.when(jnp.logical_and(j >= starts[kind], j < starts[kind + 1]))
            def _(kind=kind):
                t = p_refs[kind][...]
                if kind in (0, 1, 3):
                    t = t.astype(F32)
                if kind in (0, 3):
                    t = t * scale
                if kind in (0, 1):
                    t = t * cos_ref[...] - _swap_halves(t) * sin_ref[...]
                o_ref[...] = t.astype(BF)

    def part_spec(kind):
        return pl.BlockSpec((tr, cw), functools.partial(
            lambda i, j, kind: (i, jnp.clip(j - starts[kind], 0, widths[kind] // cw - 1)), kind=kind))

    tab = pl.BlockSpec((tr, cw), lambda i, j: (i, 0))
    return pl.pallas_call(
        body, grid=(s // tr, starts[-1]), in_specs=[part_spec(kind) for kind in range(8)] + [tab, tab],
        out_specs=pl.BlockSpec((tr, cw), lambda i, j: (i, j)),
        out_shape=jax.ShapeDtypeStruct((s, sum(widths)), BF),
        compiler_params=_params(("parallel", "arbitrary")), name=name)(*parts, *gates, cos_w, sin_w)


FFN2 =("ffn2_w_gate", "ffn2_w_up", "ffn2_w_down")


def _device_step(x, tgt, shards, small):
    s, d = x.shape
    hd = shards["w_proj_dil"].shape[0]
    hh = hd // HEAD_DIM
    blk = _att_block(s)
    gate_off = 6 * hd
    f_col = 0
    n_proj = gate_off + 2 * d
    ident = lambda accs, ex, rw: (accs[0],)
    chip = 2 * lax.axis_index("x") + lax.axis_index("y")
    ids = jnp.stack([chip, lax.axis_index("c")]).astype(jnp.int32)
    w = {}

    def take_gathered(names, gathered):
        for n, t in zip(names, gathered):
            t = lax.dynamic_update_index_in_dim(t, shards[n], chip, 0)
            if n == "w_in":
                w["w_in"], w["w_in_f"] = _repack_w_in(t, hd, d)
            else:
                w[n] = _full_from_pieces(n, t)

    def chip_sums(names, pieces, from_sibling):
        return [_rs_add("rs_add_" + n, ids, g, o) for n, g, o in zip(names, pieces, from_sibling)]

    def core_halves(names, sums, from_chips):
        return [_rs_sum("rs_sum_" + n, ids, own, got) for n, own, got in zip(names, sums, from_chips)]

    def gather(names):
        return _gather_side([shards[n] for n in names])

    first, under_up = ("ffn1_w_gate", "ffn1_w_up"), ("ffn1_w_down", "w_in")
    under_down = ("w_proj_dil", "w_proj_fox", "w_out", "ffn2_w_gate")
    under_proj = ("ffn2_w_up", "ffn2_w_down")
    take_gathered(first, gather(first).call("gather_first"))

    def w_down_of_ffn1(got):
        take_gathered(under_up, got["up"])
        return w["ffn1_w_down"]

    x1, saved1, got = _ffn_fwd("ffn1", x, small["ffn1_norm"], w["ffn1_w_gate"], w["ffn1_w_up"], w_down_of_ffn1,
                               sides={"up": lambda got: gather(under_up), "down": lambda got: gather(under_down)})
    take_gathered(under_down, got["down"])

    hm = _rms_fwd("mix_rms", x1, small["mix_norm"])
    cos_t, sin_t = _rope_tables(s)
    head_tile = _pick(hd, 1024, HEAD_DIM)
    tiles_per_kind = hd // head_tile
    scale = HEAD_DIM ** -0.5

    def heads_epi(accs, ex, rw):
        kind = pl.program_id(1) // tiles_per_kind
        t = accs[0]
        reps = (1, head_tile // HEAD_DIM)
        r = t * jnp.tile(ex[0], reps) + _swap_halves(t) * jnp.tile(ex[1], reps)
        t = jnp.where(kind < 2, r, t)
        return (jnp.where(jnp.logical_or(kind == 0, kind == 3), t * scale, t),)

    (qkv,), gathered = _mm("proj_heads", [(hm, w["w_in"])], "nn", s, gate_off, d, epilogue=heads_epi, out_dtypes=(BF,),
                           extras=[(cos_t, None), (sin_t, None)], tn=head_tile, keep_tn=True,
                           side=gather(under_proj[:1]))
    take_gathered(under_proj[:1], gathered)
    (gates,), gathered = _mm("proj_gates", [(hm, w["w_in"])], "nn", s, 2 * d, d, epilogue=ident, out_dtypes=(F32,),
                             b_off=gate_off, side=gather(under_proj[1:]))
    take_gathered(under_proj[1:], gathered)
    (f_logit,) = _mm("proj_f", [(hm, w["w_in_f"])], "nn", s, LANES, d, epilogue=ident, out_dtypes=(F32,))
    tab_t = _dil_bias(blk)
    dil_src = ((qkv, 0, HEAD_DIM), (qkv, hh, HEAD_DIM), (qkv, 2 * hh, HEAD_DIM))
    y_dil, lse_d = _flash_fwd("dil_fwd", *dil_src, hh, fox=False, tab_t=tab_t)
    bias_f = jnp.pad(small["b_forget"], ((0, 0), (0, LANES - hh)))
    c_pad = _forget_cumsum("forget_cumsum", f_logit, f_col, bias_f)
    q_cat, k_cat = _fox_aug("fox_aug", c_pad, qkv, 3 * hh, 4 * hh, hh)
    fox_src = ((q_cat, 0, 2 * HEAD_DIM), (k_cat, 0, 2 * HEAD_DIM), (qkv, 5 * hh, HEAD_DIM))
    y_fox, lse_f = _flash_fwd("fox_fwd", *fox_src, hh, fox=True)

    def merge_epi(accs, ex, rw):
        ud, uf = accs
        return ud, uf, _sigmoid(ex[0] + rw[0]) * ud + _sigmoid(ex[1] + rw[1]) * uf

    u_d, u_f, merged = _mm("merge", [(y_dil, w["w_proj_dil"]), (y_fox, w["w_proj_fox"])], "nn", s, d, hd,
                           epilogue=merge_epi, out_dtypes=(BF, BF, BF),
                           extras=[(gates, 0), (gates, d)],
                           rows=[small["b_gate_dil"], small["b_gate_fox"]])
    (x2,) = _mm("mix_out", [(merged, w["w_out"])], "nn", s, d, d,
                epilogue=lambda accs, ex, rw: (ex[0] + accs[0],), out_dtypes=(F32,), extras=[(x1, 0)])

    x3, saved2, _ = _ffn_fwd("ffn2", x2, small["ffn2_norm"], w["ffn2_w_gate"], w["ffn2_w_up"],
                             lambda got: w["ffn2_w_down"])
    dx3, dx3_bf, dg_final, sq = _final("final", x3, small["final_norm"].reshape(1, d), tgt)

    dx2, dx2_bf, dg_ffn2, dw_g2, dw_u2, dw_d2, _ = _ffn_bwd("ffn2", x2, small["ffn2_norm"], w["ffn2_w_gate"],
                                                            w["ffn2_w_up"], w["ffn2_w_down"], saved2, dx3, dx3_bf)
    pieces2 = [dw_g2, dw_u2, dw_d2]

    def dmerge_epi(accs, ex, rw):
        dm = accs[0]
        gd, gf, ud, uf = ex[0], ex[1], ex[2].astype(F32), ex[3].astype(F32)
        sd, sf = _sigmoid(gd + rw[0]), _sigmoid(gf + rw[1])
        dgd = dm * ud * (sd * (1.0 - sd))
        dgf = dm * uf * (sf * (1.0 - sf))
        return (dm * sd, dm * sf, dgd, dgf, jnp.sum(dgd, axis=0, keepdims=True), jnp.sum(dgf, axis=0, keepdims=True))

    (du_d, du_f, dg_d, dg_f, dbg_d, dbg_f), from_sibling = _mm(
        "dmerge", [(dx2_bf, w["w_out"])], "nt", s, d, d, epilogue=dmerge_epi, out_dtypes=(BF, BF, BF, BF), n_colsum=2,
        extras=[(gates, 0), (gates, d), (u_d, 0), (u_f, 0)],
        rows=[small["b_gate_dil"], small["b_gate_fox"]], side=_swap_side(pieces2))
    sums2 = chip_sums(FFN2, pieces2, from_sibling)
    (dw_out,) = _mm("dw_out", [(merged, dx2_bf)], "tn", d, d, s, epilogue=ident, out_dtypes=(F32,))
    dw_out = dw_out.reshape(N_CHIPS, d // N_CHIPS, d)
    dw_pd = _dw_col_pieces("dw_pd", y_dil, du_d, hd, d, s)
    dw_pf = _dw_col_pieces("dw_pf", y_fox, du_f, hd, d, s)
    (dy_dil,) = _mm("dy_dil", [(du_d, w["w_proj_dil"])], "nt", s, hd, d, epilogue=ident, out_dtypes=(BF,))
    (dy_fox,) = _mm("dy_fox", [(du_f, w["w_proj_fox"])], "nt", s, hd, d, epilogue=ident, out_dtypes=(BF,))

    row = lambda t: t.reshape(hh, s // blk, 1, blk)
    delta_d = _att_delta("dil_delta", dy_dil, y_dil, hh)
    dq_d, dk_d, dv_d = _flash_bwd("dil_bwd", *dil_src, dy_dil, lse_d, row(delta_d), hh, fox=False, tab_t=tab_t)
    delta_f = _att_delta("fox_delta", dy_fox, y_fox, hh)
    dq_f, dk_f, dv_f, dc_k, dc_q = _flash_bwd("fox_bwd", *fox_src, dy_fox, lse_f, row(delta_f), hh, fox=True)
    dc = dc_k.reshape(hh, s) + dc_q.reshape(hh, s)
    dc_pad = jnp.pad(dc.T, ((0, 0), (0, LANES - hh)))
    df, db_forget = _forget_bwd("forget_bwd", f_logit, f_col, bias_f, dc_pad)
    dproj = _dproj_assemble("dproj", [dq_d, dk_d, dv_d, dq_f, dk_f, dv_f], [dg_d, dg_f], cos_t, sin_t, hh, d)
    (dhm_f,) = _mm("dhm_f", [(df, w["w_in_f"])], "nt", s, d, LANES, epilogue=ident, out_dtypes=(F32,))
    (dhm,), from_chips = _mm("dhm", [(dproj, w["w_in"])], "nt", s, d, n_proj,
                             epilogue=lambda accs, ex, rw: (accs[0] + ex[0],), out_dtypes=(F32,),
                             extras=[(dhm_f, 0)], side=_scatter_side(sums2))
    halves2 = core_halves(FFN2, sums2, from_chips)
    small_mixer = ("w_proj_dil", "w_proj_fox", "w_out")
    pieces_sm = [dw_pd, dw_pf, dw_out]
    (dw_in,), from_sibling = _mm("dw_in", [(hm, dproj)], "tn", d, n_proj, s, epilogue=ident, out_dtypes=(F32,),
                                 side=_swap_side(pieces_sm))
    sums_sm = chip_sums(small_mixer, pieces_sm, from_sibling)
    (dw_in_f,) = _mm("dw_in_f", [(hm, df)], "tn", d, LANES, s, epilogue=ident, out_dtypes=(F32,))
    dw_in = _dw_in_pieces(dw_in, dw_in_f, hd, d)
    dx1, dx1_bf, dg_mix = _rms_bwd("mix_rmsb", x1, small["mix_norm"], dhm, dx2)

    sums, early = {}, ("w_in",) + small_mixer + FFN2

    def under_dwd(got):
        sums["w_in"] = chip_sums(["w_in"], [dw_in], got["dact"])
        return _scatter_side(sums_sm)

    def under_dwu(got):
        return _swap_side([got["dw_down"], got["dw_gate"]])

    def under_dh(got):
        sums["dg"] = chip_sums(["ffn1_w_down", "ffn1_w_gate"], [got["dw_down"], got["dw_gate"]], got["dwu"])
        halves = (core_halves(["w_in"], sums["w_in"], got["dwg"]) + core_halves(small_mixer, sums_sm, got["dwd"])
                  + halves2)
        return _scatter_side(sums["dg"]).beside(_swap_side([got["dw_up"]])).beside(_join_side(halves))

    def under_rmsb(got):
        sums["u"] = chip_sums(["ffn1_w_up"], [got["dw_up"]], got["dh"][2:3])
        return _scatter_side(sums["u"])

    dx0, _, dg_ffn1, dw_g1, dw_u1, dw_d1, got = _ffn_bwd(
        "ffn1", x, small["ffn1_norm"], w["ffn1_w_gate"], w["ffn1_w_up"], w["ffn1_w_down"], saved1, dx1, dx1_bf,
        sides={"dact": lambda got: _swap_side([dw_in]), "dwd": under_dwd,
               "dwg": lambda got: _scatter_side(sums["w_in"]), "dwu": under_dwu, "dh": under_dh, "rmsb": under_rmsb})
    halves1 = (core_halves(["ffn1_w_down", "ffn1_w_gate"], sums["dg"], got["dh"][:2])
               + core_halves(["ffn1_w_up"], sums["u"], got["rmsb"]))
    grads = dict(zip(early, got["dh"][3:]))
    grads.update(zip(("ffn1_w_down", "ffn1_w_gate", "ffn1_w_up"), _join_side(halves1).call("rs_join")))
    partials = {"ffn1_norm": dg_ffn1, "mix_norm": dg_mix, "ffn2_norm": dg_ffn2, "final_norm": dg_final,
                "b_gate_dil": dbg_d.reshape(-1, d), "b_gate_fox": dbg_f.reshape(-1, d), "b_forget": db_forget, "sq": sq}
    return dx0, grads, partials


def _coords():
    return lax.axis_index("x"), lax.axis_index("y"), lax.axis_index("c")


def _other_chips(x, y):
    return [(1 - x, y), (x, 1 - y), (1 - x, 1 - y)]


ANY_SPEC = pl.BlockSpec(memory_space=pl.ANY)


def _gather_side(shards):
    nw = len(shards)

    def copies(srcs, outs, send_sems, recv_sems):
        x, y, c = _coords()
        chips = _other_chips(x, y)

        def slot(w, px, py, pc):
            half = shards[w].shape[0] // 2
            return outs[w].at[2 * px + py, pl.ds(pc * half, half), :]

        def copy(w, k, src_ref, dst_ref, to):
            return pltpu.make_async_remote_copy(src_ref=src_ref, dst_ref=dst_ref, send_sem=send_sems.at[6 * w + k],
                                                recv_sem=recv_sems.at[6 * w + k], device_id=to, device_id_type=MESH)

        first, arrive, passed, arrive2 = [], [], [], []
        for w in range(nw):
            half = shards[w].shape[0] // 2
            for j, chip in enumerate(chips):
                first.append(copy(w, j, srcs[w].at[pl.ds(c * half, half), :], slot(w, x, y, c), (*chip, c)))
                arrive.append(copy(w, j, slot(w, *chip, c), slot(w, *chip, c), (*chip, c)))
                passed.append(copy(w, 3 + j, slot(w, *chip, c), slot(w, *chip, c), (x, y, 1 - c)))
                arrive2.append(copy(w, 3 + j, slot(w, *chip, 1 - c), slot(w, *chip, 1 - c), (x, y, 1 - c)))
        return first, arrive, passed, arrive2

    def start(srcs, outs, send_sems, recv_sems):
        for cp in copies(srcs, outs, send_sems, recv_sems)[0]:
            cp.start()

    def finish(srcs, outs, send_sems, recv_sems):
        first, arrive, passed, arrive2 = copies(srcs, outs, send_sems, recv_sems)
        for got, fwd in zip(arrive, passed):
            got.wait_recv()
            fwd.start()
        for got in arrive2:
            got.wait_recv()
        for cp in first + passed:
            cp.wait_send()

    return _Side(shards, [jax.ShapeDtypeStruct((N_CHIPS, *t.shape), t.dtype) for t in shards], 6 * nw, start, finish)


def _swap_side(grads):
    nw = len(grads)

    def copies(srcs, outs, send_sems, recv_sems):
        x, y, c = _coords()
        res = []
        for w in range(nw):
            half = grads[w].shape[1] // 2
            for p in range(N_CHIPS):
                k = N_CHIPS * w + p
                res.append(pltpu.make_async_remote_copy(
                    src_ref=srcs[w].at[p, pl.ds((1 - c) * half, half), :], dst_ref=outs[w].at[p],
                    send_sem=send_sems.at[k], recv_sem=recv_sems.at[k], device_id=(x, y, 1 - c), device_id_type=MESH))
        return res

    def start(*refs):
        for cp in copies(*refs):
            cp.start()

    def finish(*refs):
        for cp in copies(*refs):
            cp.wait()

    shapes = [jax.ShapeDtypeStruct((N_CHIPS, t.shape[1] // 2, t.shape[2]), t.dtype) for t in grads]
    return _Side(grads, shapes, N_CHIPS * nw, start, finish)


def _rs_add(name, ids, g, other):
    n, rows, cols = g.shape
    half = rows // 2
    tr = _pick(half, 256, 16)
    nb = half // tr

    def body(ids_ref, g_ref, o_ref, out_ref):
        out_ref[...] = (g_ref[...] + o_ref[...]).astype(BF)

    grid_spec = pltpu.PrefetchScalarGridSpec(
        num_scalar_prefetch=1, grid=(n, nb),
        in_specs=[pl.BlockSpec((None, tr, cols), lambda p, i, ids_ref: (p, ids_ref[1] * nb + i, 0)),
                  pl.BlockSpec((None, tr, cols), lambda p, i, ids_ref: (p, i, 0))],
        out_specs=pl.BlockSpec((None, tr, cols), lambda p, i, ids_ref: (p, i, 0)))
    return pl.pallas_call(body, grid_spec=grid_spec, out_shape=jax.ShapeDtypeStruct((n, half, cols), BF),
                          compiler_params=_params(("parallel", "parallel")), name=name)(ids, g, other)


def _scatter_side(sums):
    nw = len(sums)

    def copies(srcs, outs, send_sems, recv_sems):
        x, y, c = _coords()
        res = []
        for w in range(nw):
            for k, (px, py) in enumerate(_other_chips(x, y)):
                res.append(pltpu.make_async_remote_copy(
                    src_ref=srcs[w].at[2 * px + py], dst_ref=outs[w].at[k], send_sem=send_sems.at[3 * w + k],
                    recv_sem=recv_sems.at[3 * w + k], device_id=(px, py, c), device_id_type=MESH))
        return res

    def start(*refs):
        for cp in copies(*refs):
            cp.start()

    def finish(*refs):
        for cp in copies(*refs):
            cp.wait()

    return _Side(sums, [jax.ShapeDtypeStruct((3, *t.shape[1:]), t.dtype) for t in sums], 3 * nw, start, finish)


def _rs_sum(name, ids, own, got):
    n, half, cols = own.shape
    tr = _pick(half, 256, 16)
    nb = half // tr

    def body(ids_ref, own_ref, got_ref, out_ref):
        t = own_ref[...].astype(F32)
        for k in range(3):
            t = t + got_ref[k].astype(F32)
        out_ref[...] = t

    grid_spec = pltpu.PrefetchScalarGridSpec(
        num_scalar_prefetch=1, grid=(nb,),
        in_specs=[pl.BlockSpec((None, tr, cols), lambda i, ids_ref: (ids_ref[0], i, 0)),
                  pl.BlockSpec((3, tr, cols), lambda i, ids_ref: (0, i, 0))],
        out_specs=pl.BlockSpec((tr, cols), lambda i, ids_ref: (ids_ref[1] * nb + i, 0)))
    return pl.pallas_call(body, grid_spec=grid_spec, out_shape=jax.ShapeDtypeStruct((2 * half, cols), F32),
                          compiler_params=_params(("parallel",)), name=name)(ids, own, got)


def _join_side(totals):
    nw = len(totals)

    def start(ins, bufs, send_sems, recv_sems):
        x, y, c = _coords()
        for w in range(nw):
            half = totals[w].shape[0] // 2
            pltpu.make_async_remote_copy(
                src_ref=bufs[w].at[pl.ds(c * half, half), :], dst_ref=bufs[w].at[pl.ds(c * half, half), :],
                send_sem=send_sems.at[w], recv_sem=recv_sems.at[w], device_id=(x, y, 1 - c),
                device_id_type=MESH).start()

    def finish(ins, bufs, send_sems, recv_sems):
        x, y, c = _coords()
        for w in range(nw):
            half = totals[w].shape[0] // 2
            arrival = pltpu.make_async_remote_copy(
                src_ref=bufs[w].at[pl.ds(c * half, half), :], dst_ref=bufs[w].at[pl.ds((1 - c) * half, half), :],
                send_sem=send_sems.at[w], recv_sem=recv_sems.at[w], device_id=(x, y, 1 - c), device_id_type=MESH)
            arrival.wait_recv()
            arrival.wait_send()

    return _Side(totals, [jax.ShapeDtypeStruct(t.shape, t.dtype) for t in totals], nw, start, finish,
                 aliases={w: w for w in range(nw)})


def _gather_all(name, t):
    rows, cols = t.shape

    def body(src, out, send_sems, recv_sems, local_sem):
        x, y, c = _coords()
        me = 4 * x + 2 * y + c
        mine = pltpu.make_async_copy(src, out.at[me], local_sem)
        mine.start()
        peers = [(x ^ (k >> 2 & 1), y ^ (k >> 1 & 1), c ^ (k & 1)) for k in range(1, N_DEV)]
        sends = [pltpu.make_async_remote_copy(src_ref=src, dst_ref=out.at[me], send_sem=send_sems.at[k],
                                              recv_sem=recv_sems.at[k], device_id=peer, device_id_type=MESH)
                 for k, peer in enumerate(peers)]
        for cp in sends:
            cp.start()
        for k, (px, py, pc) in enumerate(peers):
            pltpu.make_async_remote_copy(src_ref=src, dst_ref=out.at[4 * px + 2 * py + pc], send_sem=send_sems.at[k],
                                         recv_sem=recv_sems.at[k], device_id=(px, py, pc),
                                         device_id_type=MESH).wait_recv()
        for cp in sends:
            cp.wait_send()
        mine.wait()

    vmem = pl.BlockSpec(memory_space=pltpu.VMEM)
    return pl.pallas_call(
        body, in_specs=[vmem], out_specs=vmem, out_shape=jax.ShapeDtypeStruct((N_DEV, rows, cols), t.dtype),
        scratch_shapes=[pltpu.SemaphoreType.DMA((7,)), pltpu.SemaphoreType.DMA((7,)), pltpu.SemaphoreType.DMA],
        name=name)(t)


def _adamw_math(w, g, m, v):
    m = ADAM_B1 * m + (1.0 - ADAM_B1) * g
    v = ADAM_B2 * v + (1.0 - ADAM_B2) * (g * g)
    m_hat = m / (1.0 - ADAM_B1 ** ADAM_STEP)
    v_hat = v / (1.0 - ADAM_B2 ** ADAM_STEP)
    delta = -ADAM_LR * (m_hat / (jnp.sqrt(v_hat) + ADAM_EPS) + ADAM_WD * w)
    return delta, m, v


def _adamw(name, w, g, m, v):
    _, rows, cols = w.shape
    tr = _pick(rows, 256, 8)

    def body(w_ref, g_ref, m_ref, v_ref, g_out, d_out, m_out, v_out):
        g = g_ref[...]
        g_out[...] = g
        d_out[...], m_out[...], v_out[...] = _adamw_math(w_ref[...], g, m_ref[...], v_ref[...])

    blk3 = pl.BlockSpec((None, tr, cols), lambda i: (0, i, 0))
    blk = pl.BlockSpec((tr, cols), lambda i: (i, 0))
    shape = jax.ShapeDtypeStruct((rows, cols), F32)
    return pl.pallas_call(
        body, grid=(rows // tr,), in_specs=[blk3, blk, blk3, blk3], out_specs=[blk] * 4, out_shape=[shape] * 4,
        compiler_params=_params(("parallel",)), name=name)(w, g, m, v)


def _small_reduce(name, parts, width):
    def body(*refs):
        out = refs[-1]
        out[...] = jnp.zeros_like(out)
        for k, r in enumerate(refs[:-1]):
            out[pl.ds(k, 1), :] = jnp.sum(r[...], axis=0, keepdims=True)

    vmem = pl.BlockSpec(memory_space=pltpu.VMEM)
    return pl.pallas_call(body, in_specs=[vmem] * len(parts), out_specs=vmem,
                          out_shape=jax.ShapeDtypeStruct((8, width), F32), name=name)(*parts)


def _small_adamw(name, gathered, w, m, v, loss_row, loss_scale):
    def body(gt_ref, w_ref, m_ref, v_ref, g_out, d_out, m_out, v_out, loss_out):
        g = gt_ref[0]
        for k in range(1, N_DEV):
            g = g + gt_ref[k]
        g_out[...] = g
        row = lax.broadcasted_iota(jnp.int32, g.shape, 0)
        loss_out[...] = jnp.sum(jnp.where(row == loss_row, g, 0.0), keepdims=True) * loss_scale
        d_out[...], m_out[...], v_out[...] = _adamw_math(w_ref[...], g, m_ref[...], v_ref[...])

    vmem = pl.BlockSpec(memory_space=pltpu.VMEM)
    shape = jax.ShapeDtypeStruct(w.shape, F32)
    return pl.pallas_call(body, in_specs=[vmem] * 4, out_specs=[vmem] * 5,
                          out_shape=[shape] * 4 + [jax.ShapeDtypeStruct((1, 1), F32)], name=name)(gathered, w, m, v)


def _full_from_pieces(name, pieces):
    _, rows, cols = pieces.shape
    if name in ROW_SHARDED:
        return pieces.reshape(N_CHIPS * rows, cols)
    return pieces.transpose(1, 0, 2).reshape(rows, N_CHIPS * cols)


def _column_range(segments, lo, hi):
    out, start = [], 0
    for t in segments:
        a, b = max(lo, start), min(hi, start + t.shape[1])
        if a < b:
            out.append(t[:, a - start:b - start])
        start += t.shape[1]
    return out


def _repack_w_in(pieces, hd, d):
    hh = hd // HEAD_DIM
    segs = [pieces[p] for p in range(N_CHIPS)]
    total = N_CHIPS * pieces.shape[2]
    main = jnp.concatenate(_column_range(segs, 0, 6 * hd) + _column_range(segs, 6 * hd + hh, total), axis=1)
    f = jnp.concatenate(_column_range(segs, 6 * hd, 6 * hd + hh), axis=1)
    return main, jnp.pad(f, ((0, 0), (0, LANES - hh)))


def _dw_in_pieces(dw, dw_f, hd, d):
    hh = hd // HEAD_DIM
    segs = [dw[:, :6 * hd], dw_f[:, :hh], dw[:, 6 * hd:]]
    cs = (6 * hd + hh + 2 * d) // N_CHIPS
    return jnp.stack([jnp.concatenate(_column_range(segs, p * cs, (p + 1) * cs), axis=1) for p in range(N_CHIPS)])


def _small_pack(vals, width):
    rows = []
    for name in SMALL:
        t = vals[name].reshape(1, -1)
        rows.append(jnp.pad(t, ((0, 0), (0, width - t.shape[1]))))
    rows.append(jnp.zeros((8 - len(SMALL), width), F32))
    return jnp.concatenate(rows, axis=0)


def kernel(x, ffn1_norm, ffn1_w_gate, ffn1_w_up, ffn1_w_down, mix_norm, w_in, b_forget, b_gate_dil, b_gate_fox, w_proj_dil, w_proj_fox, w_out, ffn2_norm, ffn2_w_gate, ffn2_w_up, ffn2_w_down, final_norm, loss_target, m_ffn1_norm, m_ffn1_w_gate, m_ffn1_w_up, m_ffn1_w_down, m_mix_norm, m_w_in, m_b_forget, m_b_gate_dil, m_b_gate_fox, m_w_proj_dil, m_w_proj_fox, m_w_out, m_ffn2_norm, m_ffn2_w_gate, m_ffn2_w_up, m_ffn2_w_down, m_final_norm, v_ffn1_norm, v_ffn1_w_gate, v_ffn1_w_up, v_ffn1_w_down, v_mix_norm, v_w_in, v_b_forget, v_b_gate_dil, v_b_gate_fox, v_w_proj_dil, v_w_proj_fox, v_w_out, v_ffn2_norm, v_ffn2_w_gate, v_ffn2_w_up, v_ffn2_w_down, v_final_norm):
    given = dict(locals())
    wts = {n: given[n] for n in WEIGHTS}
    mom_m = {n: given["m_" + n] for n in WEIGHTS}
    mom_v = {n: given["v_" + n] for n in WEIGHTS}
    d = x.shape[2]

    shards = {n: wts[n][0].astype(BF) for n in SHARDED}
    small = {n: wts[n] for n in SMALL}
    grad_x, grads, partials = _device_step(x[0], loss_target[0], shards, small)

    out_g, out_d, out_m, out_v = {}, {}, {}, {}
    for n in SHARDED:
        outs = _adamw("adamw_" + n, wts[n], grads[n], mom_m[n], mom_v[n])
        out_g[n], out_d[n], out_m[n], out_v[n] = (t[None] for t in outs)

    width = d
    part_rows = []
    for n in SMALL:
        t = partials[n]
        part_rows.append(jnp.pad(t, ((0, 0), (0, width - t.shape[1]))))
    part_rows.append(partials["sq"])
    local_small = _small_reduce("small_reduce", part_rows, width)
    gathered_small = _gather_all("small_gather", local_small)
    sg, sd_, sm, sv, loss = _small_adamw("small_adamw", gathered_small, _small_pack(wts, width),
                                         _small_pack(mom_m, width), _small_pack(mom_v, width), len(SMALL), 0.5 / d)
    for k, n in enumerate(SMALL):
        shp = wts[n].shape
        take = lambda t: t[k, :shp[-1]].reshape(shp)
        out_g[n], out_d[n], out_m[n], out_v[n] = take(sg), take(sd_), take(sm), take(sv)
    return (loss[0, 0], grad_x[None], *[out_g[n] for n in WEIGHTS], *[out_d[n] for n in WEIGHTS],
            *[out_m[n] for n in WEIGHTS], *[out_v[n] for n in WEIGHTS])
```

```python
import functools
import math

import jax
import jax.numpy as jnp
from jax import lax
from jax.experimental import pallas as pl
from jax.experimental.pallas import tpu as pltpu

HEAD_DIM = 128
ROPE_DIM = HEAD_DIM // 4
ROPE_THETA = 500000.0
DIL_PATTERNS = ((128, 1), (512, 4), (2048, 16))
MAX_WINDOW = 2048
NORM_EPS = 1e-6
ADAM_LR = 0.001
ADAM_B1 = 0.9
ADAM_B2 = 0.999
ADAM_EPS = 1e-08
ADAM_WD = 0.01
ADAM_STEP = 10

BF = jnp.bfloat16
F32 = jnp.float32
NEG = -1e30
LANES = 128
ATT_BLOCK = 512
PREP_COLS = 512
ATT_HEADS = 2
ATT_HEADS_FWD = 4
VMEM_LIMIT = 56 * 1024 * 1024
MM_VMEM_BUDGET = 46 * 1024 * 1024
N_CHIPS = 4
N_DEV = 8
MESH = pl.DeviceIdType.MESH

SHARDED = ("ffn1_w_gate", "ffn1_w_up", "ffn1_w_down", "w_in", "w_proj_dil", "w_proj_fox", "w_out",
           "ffn2_w_gate", "ffn2_w_up", "ffn2_w_down")
ROW_SHARDED = ("ffn1_w_down", "w_out", "ffn2_w_down")
SMALL = ("ffn1_norm", "mix_norm", "b_forget", "b_gate_dil", "b_gate_fox", "ffn2_norm", "final_norm")
WEIGHTS = ("ffn1_norm", "ffn1_w_gate", "ffn1_w_up", "ffn1_w_down", "mix_norm", "w_in", "b_forget",
           "b_gate_dil", "b_gate_fox", "w_proj_dil", "w_proj_fox", "w_out", "ffn2_norm", "ffn2_w_gate",
           "ffn2_w_up", "ffn2_w_down", "final_norm")


def _pick(n, target, align):
    best = None
    for d in range(align, min(n, target) + 1, align):
        if n % d == 0:
            best = d
    return n if best is None else best


def _params(sem=None):
    return pltpu.CompilerParams(dimension_semantics=sem, vmem_limit_bytes=VMEM_LIMIT)


_DIMS = {"nn": (((1,), (0,)), ((), ())), "nt": (((1,), (1,)), ((), ())), "tn": (((0,), (0,)), ((), ()))}


class _SemsFrom:
    def __init__(self, sems, first):
        self.sems, self.first = sems, first

    @property
    def at(self):
        return self

    def __getitem__(self, k):
        return self.sems.at[self.first + k]


class _Side:
    def __init__(self, inputs, out_shapes, n_sems, start, finish, aliases=None):
        self.inputs, self.out_shapes, self.n_sems = list(inputs), list(out_shapes), n_sems
        self.start, self.finish, self.aliases = start, finish, aliases or {}

    def scratch(self):
        return [pltpu.SemaphoreType.DMA((self.n_sems,)), pltpu.SemaphoreType.DMA((self.n_sems,))]

    def beside(self, other):
        n_in, n_out, n_sems = len(self.inputs), len(self.out_shapes), self.n_sems

        def part(fn_a, fn_b):
            def run(ins, outs, send_sems, recv_sems):
                fn_a(ins[:n_in], outs[:n_out], send_sems, recv_sems)
                fn_b(ins[n_in:], outs[n_out:], _SemsFrom(send_sems, n_sems), _SemsFrom(recv_sems, n_sems))
            return run

        aliases = {**self.aliases, **{n_in + t: n_out + o for t, o in other.aliases.items()}}
        return _Side(self.inputs + other.inputs, self.out_shapes + other.out_shapes, n_sems + other.n_sems,
                     part(self.start, other.start), part(self.finish, other.finish), aliases=aliases)

    def call(self, name):
        n_in, n_out = len(self.inputs), len(self.out_shapes)

        def body(*refs):
            ins, outs, sems = refs[:n_in], refs[n_in:n_in + n_out], refs[n_in + n_out:]
            self.start(ins, outs, *sems)
            self.finish(ins, outs, *sems)

        return pl.pallas_call(body, in_specs=[ANY_SPEC] * n_in, out_specs=[ANY_SPEC] * n_out, out_shape=self.out_shapes,
                              input_output_aliases=self.aliases, scratch_shapes=self.scratch(), name=name)(*self.inputs)


def _mm(name, pairs, mode, m, n, k, *, epilogue, out_dtypes, extras=(), rows=(), n_colsum=0,
        sum_pairs=False, tm=1024, tn=1152, tk=2048, piece_layout=False, side=None, b_off=0, keep_tn=False):
    m_align = LANES if mode == "tn" else 8
    tm = _pick(m, tm, m_align)
    tn = n // N_CHIPS if piece_layout else _pick(n, tn, LANES)
    tk = _pick(k, tk, LANES)
    n_acc = 1 if sum_pairs else len(pairs)
    lhs = []
    for a, _ in pairs:
        if not any(a is t for t in lhs):
            lhs.append(a)
    lhs_of = [next(t for t in range(len(lhs)) if lhs[t] is a) for a, _ in pairs]
    n_mm = len(lhs) + len(pairs)
    n_in = n_mm + len(extras) + len(rows)
    n_out = len(out_dtypes) + n_colsum

    def vmem_bytes(tm_, tn_, tk_):
        tiles = sum(tm_ * tk_ * a.dtype.itemsize for a in lhs) + sum(tn_ * tk_ * b.dtype.itemsize for _, b in pairs)
        tiles += sum(tm_ * (arr.shape[1] if off is None else tn_) * arr.dtype.itemsize for arr, off in extras)
        tiles += sum(tm_ * tn_ * jnp.dtype(dt).itemsize for dt in out_dtypes)
        return 2 * tiles + (n_acc + len(extras) + len(out_dtypes)) * tm_ * tn_ * 4

    while vmem_bytes(tm, tn, tk) > MM_VMEM_BUDGET:
        if tn > 512 and not piece_layout and not keep_tn:
            tn = _pick(n, tn - LANES, LANES)
        elif tm > 512:
            tm = _pick(m, tm - m_align, m_align)
        elif tk > 512:
            tk = _pick(k, tk - LANES, LANES)
        elif tm > 256:
            tm = _pick(m, tm - m_align, m_align)
        else:
            break
    nk = k // tk

    n_side_in = len(side.inputs) if side else 0
    n_side_out = len(side.out_shapes) if side else 0
    grid = (m // tm, n // tn, nk)

    def body(*refs):
        ins, refs = refs[:n_in], refs[n_in:]
        side_ins, refs = refs[:n_side_in], refs[n_side_in:]
        outs, refs = refs[:n_out], refs[n_out:]
        side_outs, refs = refs[:n_side_out], refs[n_side_out:]
        accs, side_sems = refs[:n_acc], refs[n_acc:]
        kk = pl.program_id(2)
        if side:
            at = [pl.program_id(t) for t in range(3)]

            @pl.when(jnp.logical_and(jnp.logical_and(at[0] == 0, at[1] == 0), at[2] == 0))
            def _():
                side.start(side_ins, side_outs, *side_sems)

        @pl.when(kk == 0)
        def _():
            for acc in accs:
                acc[...] = jnp.zeros_like(acc)

        a_tiles = [r[...].astype(BF) for r in ins[:len(lhs)]]
        for p in range(len(pairs)):
            b = ins[len(lhs) + p][...].astype(BF)
            accs[0 if sum_pairs else p][...] += lax.dot_general(a_tiles[lhs_of[p]], b, _DIMS[mode],
                                                                preferred_element_type=F32)

        @pl.when(kk == nk - 1)
        def _():
            ex = [r[...] for r in ins[n_mm:n_mm + len(extras)]]
            rw = [r[...] for r in ins[n_mm + len(extras):]]
            res = epilogue([acc[...] for acc in accs], ex, rw)
            for o, r in zip(outs, res):
                o[...] = r.astype(o.dtype)

        if side:
            @pl.when(jnp.logical_and(jnp.logical_and(at[0] == grid[0] - 1, at[1] == grid[1] - 1), at[2] == nk - 1))
            def _():
                side.finish(side_ins, side_outs, *side_sems)

    in_specs, args = [], []
    for a in lhs:
        if mode == "tn":
            in_specs.append(pl.BlockSpec((tk, tm), lambda i, j, kk: (kk, i)))
        else:
            in_specs.append(pl.BlockSpec((tm, tk), lambda i, j, kk: (i, kk)))
        args.append(a)
    assert b_off % tn == 0 and (b_off == 0 or mode == "nn")
    for _, b in pairs:
        if mode == "nt":
            in_specs.append(pl.BlockSpec((tn, tk), lambda i, j, kk: (j, kk)))
        else:
            in_specs.append(pl.BlockSpec((tk, tn), functools.partial(lambda i, j, kk, o: (kk, j + o), o=b_off // tn)))
        args.append(b)
    for arr, off in extras:
        if off is None:
            in_specs.append(pl.BlockSpec((tm, arr.shape[1]), lambda i, j, kk: (i, 0)))
        else:
            assert off % tn == 0
            in_specs.append(pl.BlockSpec((tm, tn), functools.partial(lambda i, j, kk, o: (i, j + o), o=off // tn)))
        args.append(arr)
    for arr in rows:
        in_specs.append(pl.BlockSpec((1, tn), lambda i, j, kk: (0, j)))
        args.append(arr)
    if piece_layout:
        out_specs = [pl.BlockSpec((None, tm, tn), lambda i, j, kk: (j, i, 0)) for _ in out_dtypes]
        out_shape = [jax.ShapeDtypeStruct((n // tn, m, tn), d) for d in out_dtypes]
    else:
        out_specs = [pl.BlockSpec((tm, tn), lambda i, j, kk: (i, j)) for _ in out_dtypes]
        out_shape = [jax.ShapeDtypeStruct((m, n), d) for d in out_dtypes]
    for _ in range(n_colsum):
        out_specs.append(pl.BlockSpec((None, 1, tn), lambda i, j, kk: (i, 0, j)))
        out_shape.append(jax.ShapeDtypeStruct((m // tm, 1, n), F32))
    scratch = [pltpu.VMEM((tm, tn), F32) for _ in range(n_acc)]
    if side is None:
        return pl.pallas_call(
            body, grid=grid, in_specs=in_specs, out_specs=out_specs, out_shape=out_shape, scratch_shapes=scratch,
            compiler_params=_params(("parallel", "parallel", "arbitrary")), name=name)(*args)
    res = pl.pallas_call(
        body, grid=grid, in_specs=in_specs + [ANY_SPEC] * n_side_in, out_specs=out_specs + [ANY_SPEC] * n_side_out,
        out_shape=out_shape + side.out_shapes, scratch_shapes=scratch + side.scratch(),
        input_output_aliases={n_in + t: n_out + o for t, o in side.aliases.items()},
        compiler_params=_params(("arbitrary", "arbitrary", "arbitrary")), name=name)(*args, *side.inputs)
    return res[:n_out], res[n_out:]


def _col_pieces(full):
    rows, cols = full.shape
    return full.reshape(rows, N_CHIPS, cols // N_CHIPS).transpose(1, 0, 2)


def _dw_col_pieces(name, a, b, m, n, k, side=None):
    ident = lambda accs, ex, rw: (accs[0],)
    aligned = (n // N_CHIPS) % LANES == 0
    res = _mm(name, [(a, b)], "tn", m, n, k, epilogue=ident, out_dtypes=(F32,), tm=512 if aligned else 1024,
              piece_layout=aligned, side=side)
    (out,), side_res = res if side else (res, None)
    out = out if aligned else _col_pieces(out)
    return (out, side_res) if side else out


def _hosted(sides, got, key, call):
    make = sides.get(key) if sides else None
    if make is None:
        return call(None)
    outs, got[key] = call(make(got))
    return outs


def _sigmoid(z):
    return 0.5 * jnp.tanh(0.5 * z) + 0.5


def _row_tile(s):
    return _pick(s, 256, 8)


def _fold8(t):
    r, d = t.shape
    return jnp.sum(t.reshape(r // 8, 8, d), axis=0)


def _rms_fwd(name, x, g):
    s, d = x.shape
    tr = _row_tile(s)

    def body(x_ref, g_ref, h_ref):
        xf = x_ref[...]
        y = xf * lax.rsqrt(jnp.mean(xf * xf, axis=-1, keepdims=True) + NORM_EPS)
        h_ref[...] = (y * g_ref[...]).astype(BF)

    return pl.pallas_call(
        body, grid=(s // tr,),
        in_specs=[pl.BlockSpec((tr, d), lambda i: (i, 0)), pl.BlockSpec((1, d), lambda i: (0, 0))],
        out_specs=pl.BlockSpec((tr, d), lambda i: (i, 0)), out_shape=jax.ShapeDtypeStruct((s, d), BF),
        compiler_params=_params(("parallel",)), name=name)(x, g)


def _rms_bwd(name, x, g, dh, dres, side=None):
    s, d = x.shape
    tr = _row_tile(s)
    steps = s // tr
    n_side_in = len(side.inputs) if side else 0
    n_side_out = len(side.out_shapes) if side else 0

    def body(*refs):
        x_ref, g_ref, dh_ref, dres_ref = refs[:4]
        side_ins, refs = refs[4:4 + n_side_in], refs[4 + n_side_in:]
        dx_ref, dxb_ref, dg_ref = refs[:3]
        side_outs, side_sems = refs[3:3 + n_side_out], refs[3 + n_side_out:]

        @pl.when(pl.program_id(0) == 0)
        def _():
            dg_ref[...] = jnp.zeros_like(dg_ref)
            if side:
                side.start(side_ins, side_outs, *side_sems)

        xf = x_ref[...]
        rstd = lax.rsqrt(jnp.mean(xf * xf, axis=-1, keepdims=True) + NORM_EPS)
        xhat = xf * rstd
        dhf = dh_ref[...]
        dg_ref[...] += _fold8(dhf * xhat)
        dxh = dhf * g_ref[...]
        dx = dres_ref[...] + rstd * (dxh - xhat * jnp.mean(dxh * xhat, axis=-1, keepdims=True))
        dx_ref[...] = dx
        dxb_ref[...] = dx.astype(BF)

        if side:
            @pl.when(pl.program_id(0) == steps - 1)
            def _():
                side.finish(side_ins, side_outs, *side_sems)

    blk = pl.BlockSpec((tr, d), lambda i: (i, 0))
    res = pl.pallas_call(
        body, grid=(steps,),
        in_specs=[blk, pl.BlockSpec((1, d), lambda i: (0, 0)), blk, blk] + [ANY_SPEC] * n_side_in,
        out_specs=[blk, blk, pl.BlockSpec((8, d), lambda i: (0, 0))] + [ANY_SPEC] * n_side_out,
        out_shape=[jax.ShapeDtypeStruct((s, d), F32), jax.ShapeDtypeStruct((s, d), BF),
                   jax.ShapeDtypeStruct((8, d), F32)] + (side.out_shapes if side else []),
        scratch_shapes=side.scratch() if side else [],
        compiler_params=_params(("arbitrary",)), name=name)(x, g, dh, dres, *(side.inputs if side else []))
    return (res[:3], res[3:]) if side else res


def _final(name, x, g, tgt):
    s, d = x.shape
    tr = _row_tile(s)

    def body(x_ref, g_ref, t_ref, dx_ref, dxb_ref, dg_ref, sq_ref):
        @pl.when(pl.program_id(0) == 0)
        def _():
            dg_ref[...] = jnp.zeros_like(dg_ref)
            sq_ref[...] = jnp.zeros_like(sq_ref)

        xf = x_ref[...]
        rstd = lax.rsqrt(jnp.mean(xf * xf, axis=-1, keepdims=True) + NORM_EPS)
        xhat = xf * rstd
        gf = g_ref[...]
        err = xhat * gf - t_ref[...]
        sq_ref[...] += _fold8(err * err)
        dy = err * (1.0 / d)
        dg_ref[...] += _fold8(dy * xhat)
        dxh = dy * gf
        dx = rstd * (dxh - xhat * jnp.mean(dxh * xhat, axis=-1, keepdims=True))
        dx_ref[...] = dx
        dxb_ref[...] = dx.astype(BF)

    blk = pl.BlockSpec((tr, d), lambda i: (i, 0))
    acc = pl.BlockSpec((8, d), lambda i: (0, 0))
    return pl.pallas_call(
        body, grid=(s // tr,), in_specs=[blk, pl.BlockSpec((1, d), lambda i: (0, 0)), blk],
        out_specs=[blk, blk, acc, acc],
        out_shape=[jax.ShapeDtypeStruct((s, d), F32), jax.ShapeDtypeStruct((s, d), BF),
                   jax.ShapeDtypeStruct((8, d), F32), jax.ShapeDtypeStruct((8, d), F32)],
        compiler_params=_params(("arbitrary",)), name=name)(x, g, tgt)


def _ffn_fwd(tag, x, g, w_gate, w_up, get_w_down, sides=None):
    s, d = x.shape
    f = w_gate.shape[1]
    got = {}
    h = _rms_fwd(tag + "_rms", x, g)

    def up_epi(accs, ex, rw):
        a, b = accs
        return a, b, a * _sigmoid(a) * b

    a, b, act = _hosted(sides, got, "up", lambda side: _mm(
        tag + "_up", [(h, w_gate), (h, w_up)], "nn", s, f, d, epilogue=up_epi, out_dtypes=(BF, BF, BF), side=side))

    def down_epi(accs, ex, rw):
        return (ex[0] + 0.5 * accs[0],)

    w_down = get_w_down(got)
    (y,) = _hosted(sides, got, "down", lambda side: _mm(
        tag + "_down", [(act, w_down)], "nn", s, d, f, epilogue=down_epi, out_dtypes=(F32,), extras=[(x, 0)],
        side=side))
    return y, (h, a, b, act), got


def _ffn_bwd(tag, x, g, w_gate, w_up, w_down, saved, dy, dy_bf, sides=None):
    s, d = x.shape
    f = w_gate.shape[1]
    h, a, b, act = saved
    got = {}

    def act_epi(accs, ex, rw):
        dact = 0.5 * accs[0]
        av, bv = ex[0].astype(F32), ex[1].astype(F32)
        sg = _sigmoid(av)
        return dact * bv * (sg * (1.0 + av * (1.0 - sg))), dact * (av * sg)

    da, db = _hosted(sides, got, "dact", lambda side: _mm(
        tag + "_dact", [(dy_bf, w_down)], "nt", s, f, d, epilogue=act_epi, out_dtypes=(BF, BF),
        extras=[(a, 0), (b, 0)], side=side))
    ident = lambda accs, ex, rw: (accs[0],)
    (dw_down,) = _hosted(sides, got, "dwd", lambda side: _mm(
        tag + "_dwd", [(act, dy_bf)], "tn", f, d, s, epilogue=lambda accs, ex, rw: (0.5 * accs[0],),
        out_dtypes=(F32,), side=side))
    got["dw_down"] = dw_down = dw_down.reshape(N_CHIPS, f // N_CHIPS, d)
    got["dw_gate"] = dw_gate = _hosted(sides, got, "dwg", lambda side: _dw_col_pieces(
        tag + "_dwg", h, da, d, f, s, side=side))
    got["dw_up"] = dw_up = _hosted(sides, got, "dwu", lambda side: _dw_col_pieces(
        tag + "_dwu", h, db, d, f, s, side=side))
    (dh,) = _hosted(sides, got, "dh", lambda side: _mm(
        tag + "_dh", [(da, w_gate), (db, w_up)], "nt", s, d, f, epilogue=ident, out_dtypes=(F32,), sum_pairs=True,
        side=side))
    dx, dx_bf, dg = _hosted(sides, got, "rmsb", lambda side: _rms_bwd(tag + "_rmsb", x, g, dh, dy, side=side))
    return dx, dx_bf, dg, dw_gate, dw_up, dw_down, got


def _rope_tables(s):
    half = ROPE_DIM // 2
    pos = jnp.arange(s, dtype=F32)
    inv_freq = ROPE_THETA ** (-jnp.arange(0, ROPE_DIM, 2, dtype=F32) / ROPE_DIM)
    ang = pos[:, None] * inv_freq[None, :]
    cos, sin = jnp.cos(ang), jnp.sin(ang)
    rest = HEAD_DIM - ROPE_DIM
    cos_t = jnp.concatenate([cos, cos, jnp.ones((s, rest), F32)], axis=-1)
    sin_t = jnp.concatenate([-sin, sin, jnp.zeros((s, rest), F32)], axis=-1)
    return cos_t, sin_t


def _swap_halves(t):
    lane = lax.broadcasted_iota(jnp.int32, t.shape, 1) & (HEAD_DIM - 1)
    half = ROPE_DIM // 2
    return jnp.where(lane < half, pltpu.roll(t, t.shape[1] - half, 1), pltpu.roll(t, half, 1))


def _dil_bias(blk):
    n_delta = MAX_WINDOW // blk + 1
    delta = jnp.arange(n_delta, dtype=jnp.int32)[:, None, None]
    r = jnp.arange(blk, dtype=jnp.int32)[None, None, :]
    c = jnp.arange(blk, dtype=jnp.int32)[None, :, None]
    o = delta * blk + r - c
    mult = jnp.zeros(o.shape, F32)
    for w, dd in DIL_PATTERNS:
        mult = mult + ((o >= 0) & (o <= w) & (o % dd == 0)).astype(F32)
    return jnp.where(mult > 0, jnp.log(jnp.maximum(mult, 1.0)), NEG)


def _att_block(s):
    return _pick(s, ATT_BLOCK, LANES)


def _fox_aug(name, c_pad, qkv, q_off, k_off, n_heads):
    s = c_pad.shape[0]
    tr = _pick(s, 1024, 16)

    def body(c_ref, q_ref, k_ref, qc_ref, kc_ref):
        h = pl.program_id(1)
        lane = lax.broadcasted_iota(jnp.int32, (tr, LANES), 1)
        ch = jnp.sum(jnp.where(lane == h, c_ref[...], 0.0), axis=1, keepdims=True)
        hi, mid, lo = (t.astype(F32) for t in _split3(ch))
        zero = jnp.zeros((tr, LANES), F32)
        is_hi = jnp.logical_or(lane == 0, lane == 3)
        is_mid = jnp.logical_or(lane == 1, lane == 4)
        parts = jnp.where(is_hi, hi, jnp.where(is_mid, mid, lo))
        qc_ref[:, :HEAD_DIM] = q_ref[...]
        kc_ref[:, :HEAD_DIM] = k_ref[...]
        qc_ref[:, HEAD_DIM:] = jnp.where(lane < 3, 1.0, jnp.where(lane < 6, parts, zero)).astype(BF)
        kc_ref[:, HEAD_DIM:] = jnp.where(lane < 3, -parts, jnp.where(lane < 6, 1.0, zero)).astype(BF)

    head = lambda o: pl.BlockSpec((tr, HEAD_DIM), functools.partial(lambda i, h, o: (i, o + h), o=o))
    spec = pl.BlockSpec((tr, 2 * HEAD_DIM), lambda i, h: (i, h))
    shape = jax.ShapeDtypeStruct((s, n_heads * 2 * HEAD_DIM), BF)
    return pl.pallas_call(
        body, grid=(s // tr, n_heads),
        in_specs=[pl.BlockSpec((tr, LANES), lambda i, h: (i, 0)), head(q_off), head(k_off)],
        out_specs=[spec, spec], out_shape=[shape, shape],
        compiler_params=_params(("parallel", "arbitrary")), name=name)(c_pad, qkv, qkv)


def _causal_mask(st):
    kpos = lax.broadcasted_iota(jnp.int32, st.shape, 0)
    qpos = lax.broadcasted_iota(jnp.int32, st.shape, 1)
    return jnp.where(kpos <= qpos, st, NEG)


def _flash_fwd(name, q_src, k_src, v_src, n_heads, *, fox, tab_t=None):
    (q_arr, q_off, qw), (k_arr, k_off, kw), (v_arr, v_off, _) = q_src, k_src, v_src
    s = q_arr.shape[0]
    blk = _att_block(s)
    nq = s // blk
    n_delta = MAX_WINDOW // blk + 1
    grp = ATT_HEADS_FWD
    assert qw == kw and n_heads % grp == 0 and q_off % grp == 0 and k_off % grp == 0 and v_off % grp == 0
    wide = grp * HEAD_DIM

    def body(*refs):
        if fox:
            q_ref, k_ref, v_ref, o_ref, lse_ref, acc, m_s, l_s = refs
        else:
            q_ref, k_ref, v_ref, tab_ref, o_ref, lse_ref, acc, m_s, l_s = refs
        i = pl.program_id(1)
        acc[...] = jnp.zeros_like(acc)
        m_s[...] = jnp.full_like(m_s, NEG)
        l_s[...] = jnp.zeros_like(l_s)

        def step(j, diagonal):
            ks = pl.ds(pl.multiple_of(j * blk, blk), blk)
            for g in range(grp):
                cols = slice(g * HEAD_DIM, (g + 1) * HEAD_DIM)
                qk_cols = slice(g * qw, (g + 1) * qw)
                st = lax.dot_general(k_ref[ks, qk_cols], q_ref[:, qk_cols], _DIMS["nt"], preferred_element_type=F32)
                if fox:
                    if diagonal:
                        st = _causal_mask(st)
                else:
                    st = st + tab_ref[i - j]
                m_prev = m_s[g]
                m_new = jnp.maximum(m_prev, jnp.max(st, axis=0, keepdims=True))
                alpha = jnp.exp(m_prev - m_new)
                p = jnp.exp(st - m_new)
                l_s[g] = alpha * l_s[g] + jnp.sum(p, axis=0, keepdims=True)
                acc[g] = alpha * acc[g] + lax.dot_general(v_ref[ks, cols], p.astype(BF), _DIMS["tn"],
                                                          preferred_element_type=F32)
                m_s[g] = m_new

        def loop_step(j, carry):
            step(j, False)
            return carry

        if fox:
            lax.fori_loop(0, i, loop_step, 0)
            step(i, True)
        else:
            lax.fori_loop(jnp.maximum(i - (n_delta - 1), 0), i + 1, loop_step, 0)
        for g in range(grp):
            o_ref[:, g * HEAD_DIM:(g + 1) * HEAD_DIM] = (acc[g] / l_s[g]).T.astype(o_ref.dtype)
            lse_ref[g] = m_s[g] + jnp.log(l_s[g])

    off = lambda o: functools.partial(lambda h, i, o: (0, o + h), o=o // grp)
    in_specs = [pl.BlockSpec((blk, grp * qw), functools.partial(lambda h, i, o: (i, o + h), o=q_off // grp)),
                pl.BlockSpec((s, grp * kw), off(k_off), pipeline_mode=pl.Buffered(1)),
                pl.BlockSpec((s, wide), off(v_off), pipeline_mode=pl.Buffered(1))]
    args = [q_arr, k_arr, v_arr]
    if not fox:
        in_specs.append(pl.BlockSpec((n_delta, blk, blk), lambda h, i: (0, 0, 0)))
        args.append(tab_t)
    return pl.pallas_call(
        body, grid=(n_heads // grp, nq), in_specs=in_specs,
        out_specs=[pl.BlockSpec((blk, wide), lambda h, i: (i, h)),
                   pl.BlockSpec((grp, None, 1, blk), lambda h, i: (h, i, 0, 0))],
        out_shape=[jax.ShapeDtypeStruct((s, n_heads * HEAD_DIM), BF),
                   jax.ShapeDtypeStruct((n_heads, nq, 1, blk), F32)],
        scratch_shapes=[pltpu.VMEM((grp, HEAD_DIM, blk), F32), pltpu.VMEM((grp, 1, blk), F32),
                        pltpu.VMEM((grp, 1, blk), F32)],
        compiler_params=_params(("parallel", "parallel")), name=name)(*args)


def _att_delta(name, do, o, n_heads):
    s = do.shape[0]
    blk = _pick(s, 1024, 16)

    def body(do_ref, o_ref, d_ref):
        d_ref[...] = jnp.sum(do_ref[...].astype(F32) * o_ref[...].astype(F32), axis=-1, keepdims=True)

    spec = pl.BlockSpec((blk, HEAD_DIM), lambda h, i: (i, h))
    return pl.pallas_call(
        body, grid=(n_heads, s // blk), in_specs=[spec, spec],
        out_specs=pl.BlockSpec((None, blk, 1), lambda h, i: (h, i, 0)),
        out_shape=jax.ShapeDtypeStruct((n_heads, s, 1), F32),
        compiler_params=_params(("parallel", "parallel")), name=name)(do, o)


def _flash_bwd(name, q_src, k_src, v_src, do, lse_row, delta_row, n_heads, *, fox, tab_t=None):
    (q_arr, q_off, qw), (k_arr, k_off, kw), (v_arr, v_off, _) = q_src, k_src, v_src
    s = q_arr.shape[0]
    blk = _att_block(s)
    nq = s // blk
    n_delta = MAX_WINDOW // blk + 1
    grp = ATT_HEADS
    assert qw == kw and n_heads % grp == 0 and q_off % grp == 0 and k_off % grp == 0 and v_off % grp == 0
    wide = grp * HEAD_DIM

    def body(*refs):
        if fox:
            (q_ref, do_ref, k_ref, v_ref, lse_ref, dl_ref,
             dq_ref, dk_ref, dv_ref, dc_ref, dcq_ref, dk_acc, dv_acc, dc_acc) = refs
        else:
            (q_ref, do_ref, k_ref, v_ref, lse_ref, dl_ref, tab_ref,
             dq_ref, dk_ref, dv_ref, dk_acc, dv_acc) = refs
        j = pl.program_id(1)

        @pl.when(j == 0)
        def _():
            dq_ref[...] = jnp.zeros_like(dq_ref)
            if fox:
                dcq_ref[...] = jnp.zeros_like(dcq_ref)

        dk_acc[...] = jnp.zeros_like(dk_acc)
        dv_acc[...] = jnp.zeros_like(dv_acc)
        if fox:
            dc_acc[...] = jnp.zeros_like(dc_acc)

        def step(i, diagonal):
            qs = pl.ds(pl.multiple_of(i * blk, blk), blk)
            for g in range(grp):
                cols = slice(g * HEAD_DIM, (g + 1) * HEAD_DIM)
                qk_cols = slice(g * qw, (g + 1) * qw)
                plain = slice(g * qw, g * qw + HEAD_DIM)
                kb, vb = k_ref[:, plain], v_ref[:, cols]
                qb, dob = q_ref[qs, plain], do_ref[qs, cols]
                st = lax.dot_general(k_ref[:, qk_cols], q_ref[qs, qk_cols], _DIMS["nt"], preferred_element_type=F32)
                if fox:
                    if diagonal:
                        st = _causal_mask(st)
                else:
                    st = st + tab_ref[i - j]
                pt = jnp.exp(st - lse_ref[g, i])
                dv_acc[:, cols] += jnp.dot(pt.astype(BF), dob, preferred_element_type=F32)
                dpt = lax.dot_general(vb, dob, _DIMS["nt"], preferred_element_type=F32)
                dst = pt * (dpt - dl_ref[g, i])
                dsb = dst.astype(BF)
                dk_acc[:, cols] += jnp.dot(dsb, qb, preferred_element_type=F32)
                dq_ref[qs, cols] += lax.dot_general(dsb, kb, _DIMS["tn"], preferred_element_type=F32)
                if fox:
                    folded = dst[:, :LANES]
                    for part in range(1, blk // LANES):
                        folded = folded + dst[:, part * LANES:(part + 1) * LANES]
                    dc_acc[g] -= folded
                    dcq_ref[g, i] += jnp.sum(dst, axis=0, keepdims=True)

        def loop_step(i, carry):
            step(i, False)
            return carry

        if fox:
            step(j, True)
            lax.fori_loop(j + 1, nq, loop_step, 0)
        else:
            lax.fori_loop(j, jnp.minimum(nq, j + n_delta), loop_step, 0)
        dk_ref[...] = dk_acc[...]
        dv_ref[...] = dv_acc[...].astype(dv_ref.dtype)
        if fox:
            for g in range(grp):
                dc_ref[g] = jnp.sum(dc_acc[g], axis=-1, keepdims=True)

    full = lambda o, wd=wide: pl.BlockSpec((s, wd), functools.partial(lambda h, j, o: (0, o + h), o=o // grp))
    tile = lambda o, wd=wide: pl.BlockSpec((blk, wd), functools.partial(lambda h, j, o: (j, o + h), o=o // grp))
    per_q = pl.BlockSpec((grp, nq, 1, blk), lambda h, j: (h, 0, 0, 0))
    per_k = pl.BlockSpec((grp, blk, 1), lambda h, j: (h, j, 0))
    in_specs = [full(q_off, grp * qw), full(0), tile(k_off, grp * kw), tile(v_off), per_q, per_q]
    args = [q_arr, do, k_arr, v_arr, lse_row, delta_row]
    out_specs = [full(0), tile(0), tile(0)]
    hd = n_heads * HEAD_DIM
    out_shape = [jax.ShapeDtypeStruct((s, hd), F32), jax.ShapeDtypeStruct((s, hd), F32),
                 jax.ShapeDtypeStruct((s, hd), BF)]
    scratch = [pltpu.VMEM((blk, wide), F32), pltpu.VMEM((blk, wide), F32)]
    if fox:
        out_specs += [per_k, per_q]
        out_shape += [jax.ShapeDtypeStruct((n_heads, s, 1), F32), jax.ShapeDtypeStruct((n_heads, nq, 1, blk), F32)]
        scratch.append(pltpu.VMEM((grp, blk, LANES), F32))
    else:
        in_specs.append(pl.BlockSpec((n_delta, blk, blk), lambda h, j: (0, 0, 0)))
        args.append(tab_t)
    return pl.pallas_call(
        body, grid=(n_heads // grp, nq), in_specs=in_specs, out_specs=out_specs, out_shape=out_shape,
        scratch_shapes=scratch, compiler_params=_params(("parallel", "arbitrary")), name=name)(*args)


def _split3(t):
    hi = t.astype(BF)
    r1 = t - hi.astype(F32)
    mid = r1.astype(BF)
    lo = (r1 - mid.astype(F32)).astype(BF)
    return hi, mid, lo


def _tri_dot(tri, t):
    hi, mid, lo = _split3(t)
    return (jnp.dot(tri, hi, preferred_element_type=F32) + jnp.dot(tri, mid, preferred_element_type=F32)
            + jnp.dot(tri, lo, preferred_element_type=F32))


def _log_sigmoid(z):
    return jnp.minimum(z, 0.0) - jnp.log(1.0 + jnp.exp(-jnp.abs(z)))


def _forget_cumsum(name, proj, f_col, bias):
    s = proj.shape[0]
    blk = _att_block(s)

    def body(f_ref, b_ref, c_ref, carry):
        @pl.when(pl.program_id(0) == 0)
        def _():
            carry[...] = jnp.zeros_like(carry)

        lf = _log_sigmoid(f_ref[...] + b_ref[...])
        r = lax.broadcasted_iota(jnp.int32, (blk, blk), 0)
        c = lax.broadcasted_iota(jnp.int32, (blk, blk), 1)
        tri = (c <= r).astype(BF)
        c_ref[...] = _tri_dot(tri, lf) + carry[...]
        carry[...] = c_ref[pl.ds(blk - 1, 1), :]

    return pl.pallas_call(
        body, grid=(s // blk,),
        in_specs=[pl.BlockSpec((blk, LANES), lambda i: (i, f_col)), pl.BlockSpec((1, LANES), lambda i: (0, 0))],
        out_specs=pl.BlockSpec((blk, LANES), lambda i: (i, 0)), out_shape=jax.ShapeDtypeStruct((s, LANES), F32),
        scratch_shapes=[pltpu.VMEM((1, LANES), F32)],
        compiler_params=_params(("arbitrary",)), name=name)(proj, bias)


def _forget_bwd(name, proj, f_col, bias, dc):
    s = proj.shape[0]
    blk = _att_block(s)
    nb = s // blk

    def body(f_ref, b_ref, dc_ref, df_ref, db_ref, carry):
        @pl.when(pl.program_id(0) == 0)
        def _():
            carry[...] = jnp.zeros_like(carry)
            db_ref[...] = jnp.zeros_like(db_ref)

        r = lax.broadcasted_iota(jnp.int32, (blk, blk), 0)
        c = lax.broadcasted_iota(jnp.int32, (blk, blk), 1)
        tri = (c >= r).astype(BF)
        r = lax.broadcasted_iota(jnp.int32, (blk, LANES), 0)
        dlf = _tri_dot(tri, dc_ref[...]) + carry[...]
        carry[...] = jnp.sum(jnp.where(r == 0, dlf, 0.0), axis=0, keepdims=True)
        dz = dlf * _sigmoid(-(f_ref[...] + b_ref[...]))
        df_ref[...] = dz.astype(BF)
        db_ref[...] += _fold8(dz)

    rev = lambda i: (nb - 1 - i, 0)
    return pl.pallas_call(
        body, grid=(nb,),
        in_specs=[pl.BlockSpec((blk, LANES), lambda i: (nb - 1 - i, f_col)), pl.BlockSpec((1, LANES), lambda i: (0, 0)),
                  pl.BlockSpec((blk, LANES), rev)],
        out_specs=[pl.BlockSpec((blk, LANES), rev), pl.BlockSpec((8, LANES), lambda i: (0, 0))],
        out_shape=[jax.ShapeDtypeStruct((s, LANES), BF), jax.ShapeDtypeStruct((8, LANES), F32)],
        scratch_shapes=[pltpu.VMEM((1, LANES), F32)],
        compiler_params=_params(("arbitrary",)), name=name)(proj, bias, dc)


def _dproj_assemble(name, parts, gates, cos_t, sin_t, n_heads, d_model):
    s = cos_t.shape[0]
    tr = _pick(s, 1024, 16)
    scale = HEAD_DIM ** -0.5
    hd = n_heads * HEAD_DIM
    cw = _pick(math.gcd(hd, d_model), PREP_COLS, HEAD_DIM)
    widths = [hd] * 6 + [d_model] * 2
    starts = [sum(widths[:t]) // cw for t in range(len(widths) + 1)]
    cos_w, sin_w = jnp.tile(cos_t, (1, cw // HEAD_DIM)), jnp.tile(sin_t, (1, cw // HEAD_DIM))

    def body(*refs):
        p_refs, cos_ref, sin_ref, o_ref = refs[:8], refs[8], refs[9], refs[10]
        j = pl.program_id(1)
        for kind in range(8):
            @pl.when(jnp.logical_and(j >= starts[kind], j < starts[kind + 1]))
            def _(kind=kind):
                t = p_refs[kind][...]
                if kind in (0, 1, 3):
                    t = t.astype(F32)
                if kind in (0, 3):
                    t = t * scale
                if kind in (0, 1):
                    t = t * cos_ref[...] - _swap_halves(t) * sin_ref[...]
                o_ref[...] = t.astype(BF)

    def part_spec(kind):
        return pl.BlockSpec((tr, cw), functools.partial(
            lambda i, j, kind: (i, jnp.clip(j - starts[kind], 0, widths[kind] // cw - 1)), kind=kind))

    tab = pl.BlockSpec((tr, cw), lambda i, j: (i, 0))
    return pl.pallas_call(
        body, grid=(s // tr, starts[-1]), in_specs=[part_spec(kind) for kind in range(8)] + [tab, tab],
        out_specs=pl.BlockSpec((tr, cw), lambda i, j: (i, j)),
        out_shape=jax.ShapeDtypeStruct((s, sum(widths)), BF),
        compiler_params=_params(("parallel", "arbitrary")), name=name)(*parts, *gates, cos_w, sin_w)


FFN2 =("ffn2_w_gate", "ffn2_w_up", "ffn2_w_down")


def _device_step(x, tgt, shards, small):
    s, d = x.shape
    hd = shards["w_proj_dil"].shape[0]
    hh = hd // HEAD_DIM
    blk = _att_block(s)
    gate_off = 6 * hd
    f_col = 0
    n_proj = gate_off + 2 * d
    ident = lambda accs, ex, rw: (accs[0],)
    chip = 2 * lax.axis_index("x") + lax.axis_index("y")
    ids = jnp.stack([chip, lax.axis_index("c")]).astype(jnp.int32)
    w = {}

    def take_gathered(names, gathered):
        for n, t in zip(names, gathered):
            t = lax.dynamic_update_index_in_dim(t, shards[n], chip, 0)
            if n == "w_in":
                w["w_in"], w["w_in_f"] = _repack_w_in(t, hd, d)
            else:
                w[n] = _full_from_pieces(n, t)

    def chip_sums(names, pieces, from_sibling):
        return [_rs_add("rs_add_" + n, ids, g, o) for n, g, o in zip(names, pieces, from_sibling)]

    def core_halves(names, sums, from_chips):
        return [_rs_sum("rs_sum_" + n, ids, own, got) for n, own, got in zip(names, sums, from_chips)]

    def gather(names):
        return _gather_side([shards[n] for n in names])

    first, under_up = ("ffn1_w_gate", "ffn1_w_up"), ("ffn1_w_down", "w_in")
    under_down = ("w_proj_dil", "w_proj_fox", "w_out", "ffn2_w_gate")
    under_proj = ("ffn2_w_up", "ffn2_w_down")
    take_gathered(first, gather(first).call("gather_first"))

    def w_down_of_ffn1(got):
        take_gathered(under_up, got["up"])
        return w["ffn1_w_down"]

    x1, saved1, got = _ffn_fwd("ffn1", x, small["ffn1_norm"], w["ffn1_w_gate"], w["ffn1_w_up"], w_down_of_ffn1,
                               sides={"up": lambda got: gather(under_up), "down": lambda got: gather(under_down)})
    take_gathered(under_down, got["down"])

    hm = _rms_fwd("mix_rms", x1, small["mix_norm"])
    cos_t, sin_t = _rope_tables(s)
    head_tile = _pick(hd, 1024, HEAD_DIM)
    tiles_per_kind = hd // head_tile
    scale = HEAD_DIM ** -0.5

    def heads_epi(accs, ex, rw):
        kind = pl.program_id(1) // tiles_per_kind
        t = accs[0]
        reps = (1, head_tile // HEAD_DIM)
        r = t * jnp.tile(ex[0], reps) + _swap_halves(t) * jnp.tile(ex[1], reps)
        t = jnp.where(kind < 2, r, t)
        return (jnp.where(jnp.logical_or(kind == 0, kind == 3), t * scale, t),)

    (qkv,), gathered = _mm("proj_heads", [(hm, w["w_in"])], "nn", s, gate_off, d, epilogue=heads_epi, out_dtypes=(BF,),
                           extras=[(cos_t, None), (sin_t, None)], tn=head_tile, keep_tn=True,
                           side=gather(under_proj[:1]))
    take_gathered(under_proj[:1], gathered)
    (gates,), gathered = _mm("proj_gates", [(hm, w["w_in"])], "nn", s, 2 * d, d, epilogue=ident, out_dtypes=(F32,),
                             b_off=gate_off, side=gather(under_proj[1:]))
    take_gathered(under_proj[1:], gathered)
    (f_logit,) = _mm("proj_f", [(hm, w["w_in_f"])], "nn", s, LANES, d, epilogue=ident, out_dtypes=(F32,))
    tab_t = _dil_bias(blk)
    dil_src = ((qkv, 0, HEAD_DIM), (qkv, hh, HEAD_DIM), (qkv, 2 * hh, HEAD_DIM))
    y_dil, lse_d = _flash_fwd("dil_fwd", *dil_src, hh, fox=False, tab_t=tab_t)
    bias_f = jnp.pad(small["b_forget"], ((0, 0), (0, LANES - hh)))
    c_pad = _forget_cumsum("forget_cumsum", f_logit, f_col, bias_f)
    q_cat, k_cat = _fox_aug("fox_aug", c_pad, qkv, 3 * hh, 4 * hh, hh)
    fox_src = ((q_cat, 0, 2 * HEAD_DIM), (k_cat, 0, 2 * HEAD_DIM), (qkv, 5 * hh, HEAD_DIM))
    y_fox, lse_f = _flash_fwd("fox_fwd", *fox_src, hh, fox=True)

    def merge_epi(accs, ex, rw):
        ud, uf = accs
        return ud, uf, _sigmoid(ex[0] + rw[0]) * ud + _sigmoid(ex[1] + rw[1]) * uf

    u_d, u_f, merged = _mm("merge", [(y_dil, w["w_proj_dil"]), (y_fox, w["w_proj_fox"])], "nn", s, d, hd,
                           epilogue=merge_epi, out_dtypes=(BF, BF, BF),
                           extras=[(gates, 0), (gates, d)],
                           rows=[small["b_gate_dil"], small["b_gate_fox"]])
    (x2,) = _mm("mix_out", [(merged, w["w_out"])], "nn", s, d, d,
                epilogue=lambda accs, ex, rw: (ex[0] + accs[0],), out_dtypes=(F32,), extras=[(x1, 0)])

    x3, saved2, _ = _ffn_fwd("ffn2", x2, small["ffn2_norm"], w["ffn2_w_gate"], w["ffn2_w_up"],
                             lambda got: w["ffn2_w_down"])
    dx3, dx3_bf, dg_final, sq = _final("final", x3, small["final_norm"].reshape(1, d), tgt)

    dx2, dx2_bf, dg_ffn2, dw_g2, dw_u2, dw_d2, _ = _ffn_bwd("ffn2", x2, small["ffn2_norm"], w["ffn2_w_gate"],
                                                            w["ffn2_w_up"], w["ffn2_w_down"], saved2, dx3, dx3_bf)
    pieces2 = [dw_g2, dw_u2, dw_d2]

    def dmerge_epi(accs, ex, rw):
        dm = accs[0]
        gd, gf, ud, uf = ex[0], ex[1], ex[2].astype(F32), ex[3].astype(F32)
        sd, sf = _sigmoid(gd + rw[0]), _sigmoid(gf + rw[1])
        dgd = dm * ud * (sd * (1.0 - sd))
        dgf = dm * uf * (sf * (1.0 - sf))
        return (dm * sd, dm * sf, dgd, dgf, jnp.sum(dgd, axis=0, keepdims=True), jnp.sum(dgf, axis=0, keepdims=True))

    (du_d, du_f, dg_d, dg_f, dbg_d, dbg_f), from_sibling = _mm(
        "dmerge", [(dx2_bf, w["w_out"])], "nt", s, d, d, epilogue=dmerge_epi, out_dtypes=(BF, BF, BF, BF), n_colsum=2,
        extras=[(gates, 0), (gates, d), (u_d, 0), (u_f, 0)],
        rows=[small["b_gate_dil"], small["b_gate_fox"]], side=_swap_side(pieces2))
    sums2 = chip_sums(FFN2, pieces2, from_sibling)
    (dw_out,) = _mm("dw_out", [(merged, dx2_bf)], "tn", d, d, s, epilogue=ident, out_dtypes=(F32,))
    dw_out = dw_out.reshape(N_CHIPS, d // N_CHIPS, d)
    dw_pd = _dw_col_pieces("dw_pd", y_dil, du_d, hd, d, s)
    dw_pf = _dw_col_pieces("dw_pf", y_fox, du_f, hd, d, s)
    (dy_dil,) = _mm("dy_dil", [(du_d, w["w_proj_dil"])], "nt", s, hd, d, epilogue=ident, out_dtypes=(BF,))
    (dy_fox,) = _mm("dy_fox", [(du_f, w["w_proj_fox"])], "nt", s, hd, d, epilogue=ident, out_dtypes=(BF,))

    row = lambda t: t.reshape(hh, s // blk, 1, blk)
    delta_d = _att_delta("dil_delta", dy_dil, y_dil, hh)
    dq_d, dk_d, dv_d = _flash_bwd("dil_bwd", *dil_src, dy_dil, lse_d, row(delta_d), hh, fox=False, tab_t=tab_t)
    delta_f = _att_delta("fox_delta", dy_fox, y_fox, hh)
    dq_f, dk_f, dv_f, dc_k, dc_q = _flash_bwd("fox_bwd", *fox_src, dy_fox, lse_f, row(delta_f), hh, fox=True)
    dc = dc_k.reshape(hh, s) + dc_q.reshape(hh, s)
    dc_pad = jnp.pad(dc.T, ((0, 0), (0, LANES - hh)))
    df, db_forget = _forget_bwd("forget_bwd", f_logit, f_col, bias_f, dc_pad)
    dproj = _dproj_assemble("dproj", [dq_d, dk_d, dv_d, dq_f, dk_f, dv_f], [dg_d, dg_f], cos_t, sin_t, hh, d)
    (dhm_f,) = _mm("dhm_f", [(df, w["w_in_f"])], "nt", s, d, LANES, epilogue=ident, out_dtypes=(F32,))
    (dhm,), from_chips = _mm("dhm", [(dproj, w["w_in"])], "nt", s, d, n_proj,
                             epilogue=lambda accs, ex, rw: (accs[0] + ex[0],), out_dtypes=(F32,),
                             extras=[(dhm_f, 0)], side=_scatter_side(sums2))
    halves2 = core_halves(FFN2, sums2, from_chips)
    small_mixer = ("w_proj_dil", "w_proj_fox", "w_out")
    pieces_sm = [dw_pd, dw_pf, dw_out]
    (dw_in,), from_sibling = _mm("dw_in", [(hm, dproj)], "tn", d, n_proj, s, epilogue=ident, out_dtypes=(F32,),
                                 side=_swap_side(pieces_sm))
    sums_sm = chip_sums(small_mixer, pieces_sm, from_sibling)
    (dw_in_f,) = _mm("dw_in_f", [(hm, df)], "tn", d, LANES, s, epilogue=ident, out_dtypes=(F32,))
    dw_in = _dw_in_pieces(dw_in, dw_in_f, hd, d)
    dx1, dx1_bf, dg_mix = _rms_bwd("mix_rmsb", x1, small["mix_norm"], dhm, dx2)

    sums, early = {}, ("w_in",) + small_mixer + FFN2

    def under_dwd(got):
        sums["w_in"] = chip_sums(["w_in"], [dw_in], got["dact"])
        return _scatter_side(sums_sm)

    def under_dwu(got):
        return _swap_side([got["dw_down"], got["dw_gate"]])

    def under_dh(got):
        sums["dg"] = chip_sums(["ffn1_w_down", "ffn1_w_gate"], [got["dw_down"], got["dw_gate"]], got["dwu"])
        halves = (core_halves(["w_in"], sums["w_in"], got["dwg"]) + core_halves(small_mixer, sums_sm, got["dwd"])
                  + halves2)
        return _scatter_side(sums["dg"]).beside(_swap_side([got["dw_up"]])).beside(_join_side(halves))

    def under_rmsb(got):
        sums["u"] = chip_sums(["ffn1_w_up"], [got["dw_up"]], got["dh"][2:3])
        return _scatter_side(sums["u"])

    dx0, _, dg_ffn1, dw_g1, dw_u1, dw_d1, got = _ffn_bwd(
        "ffn1", x, small["ffn1_norm"], w["ffn1_w_gate"], w["ffn1_w_up"], w["ffn1_w_down"], saved1, dx1, dx1_bf,
        sides={"dact": lambda got: _swap_side([dw_in]), "dwd": under_dwd,
               "dwg": lambda got: _scatter_side(sums["w_in"]), "dwu": under_dwu, "dh": under_dh, "rmsb": under_rmsb})
    halves1 = (core_halves(["ffn1_w_down", "ffn1_w_gate"], sums["dg"], got["dh"][:2])
               + core_halves(["ffn1_w_up"], sums["u"], got["rmsb"]))
    grads = dict(zip(early, got["dh"][3:]))
    grads.update(zip(("ffn1_w_down", "ffn1_w_gate", "ffn1_w_up"), _join_side(halves1).call("rs_join")))
    partials = {"ffn1_norm": dg_ffn1, "mix_norm": dg_mix, "ffn2_norm": dg_ffn2, "final_norm": dg_final,
                "b_gate_dil": dbg_d.reshape(-1, d), "b_gate_fox": dbg_f.reshape(-1, d), "b_forget": db_forget, "sq": sq}
    return dx0, grads, partials


def _coords():
    return lax.axis_index("x"), lax.axis_index("y"), lax.axis_index("c")


def _other_chips(x, y):
    return [(1 - x, y), (x, 1 - y), (1 - x, 1 - y)]


ANY_SPEC = pl.BlockSpec(memory_space=pl.ANY)


def _gather_side(shards):
    nw = len(shards)

    def copies(srcs, outs, send_sems, recv_sems):
        x, y, c = _coords()
        chips = _other_chips(x, y)

        def slot(w, px, py, pc):
            half = shards[w].shape[0] // 2
            return outs[w].at[2 * px + py, pl.ds(pc * half, half), :]

        def copy(w, k, src_ref, dst_ref, to):
            return pltpu.make_async_remote_copy(src_ref=src_ref, dst_ref=dst_ref, send_sem=send_sems.at[6 * w + k],
                                                recv_sem=recv_sems.at[6 * w + k], device_id=to, device_id_type=MESH)

        first, arrive, passed, arrive2 = [], [], [], []
        for w in range(nw):
            half = shards[w].shape[0] // 2
            for j, chip in enumerate(chips):
                first.append(copy(w, j, srcs[w].at[pl.ds(c * half, half), :], slot(w, x, y, c), (*chip, c)))
                arrive.append(copy(w, j, slot(w, *chip, c), slot(w, *chip, c), (*chip, c)))
                passed.append(copy(w, 3 + j, slot(w, *chip, c), slot(w, *chip, c), (x, y, 1 - c)))
                arrive2.append(copy(w, 3 + j, slot(w, *chip, 1 - c), slot(w, *chip, 1 - c), (x, y, 1 - c)))
        return first, arrive, passed, arrive2

    def start(srcs, outs, send_sems, recv_sems):
        for cp in copies(srcs, outs, send_sems, recv_sems)[0]:
            cp.start()

    def finish(srcs, outs, send_sems, recv_sems):
        first, arrive, passed, arrive2 = copies(srcs, outs, send_sems, recv_sems)
        for got, fwd in zip(arrive, passed):
            got.wait_recv()
            fwd.start()
        for got in arrive2:
            got.wait_recv()
        for cp in first + passed:
            cp.wait_send()

    return _Side(shards, [jax.ShapeDtypeStruct((N_CHIPS, *t.shape), t.dtype) for t in shards], 6 * nw, start, finish)


def _swap_side(grads):
    nw = len(grads)

    def copies(srcs, outs, send_sems, recv_sems):
        x, y, c = _coords()
        res = []
        for w in range(nw):
            half = grads[w].shape[1] // 2
            for p in range(N_CHIPS):
                k = N_CHIPS * w + p
                res.append(pltpu.make_async_remote_copy(
                    src_ref=srcs[w].at[p, pl.ds((1 - c) * half, half), :], dst_ref=outs[w].at[p],
                    send_sem=send_sems.at[k], recv_sem=recv_sems.at[k], device_id=(x, y, 1 - c), device_id_type=MESH))
        return res

    def start(*refs):
        for cp in copies(*refs):
            cp.start()

    def finish(*refs):
        for cp in copies(*refs):
            cp.wait()

    shapes = [jax.ShapeDtypeStruct((N_CHIPS, t.shape[1] // 2, t.shape[2]), t.dtype) for t in grads]
    return _Side(grads, shapes, N_CHIPS * nw, start, finish)


def _rs_add(name, ids, g, other):
    n, rows, cols = g.shape
    half = rows // 2
    tr = _pick(half, 256, 16)
    nb = half // tr

    def body(ids_ref, g_ref, o_ref, out_ref):
        out_ref[...] = (g_ref[...] + o_ref[...]).astype(BF)

    grid_spec = pltpu.PrefetchScalarGridSpec(
        num_scalar_prefetch=1, grid=(n, nb),
        in_specs=[pl.BlockSpec((None, tr, cols), lambda p, i, ids_ref: (p, ids_ref[1] * nb + i, 0)),
                  pl.BlockSpec((None, tr, cols), lambda p, i, ids_ref: (p, i, 0))],
        out_specs=pl.BlockSpec((None, tr, cols), lambda p, i, ids_ref: (p, i, 0)))
    return pl.pallas_call(body, grid_spec=grid_spec, out_shape=jax.ShapeDtypeStruct((n, half, cols), BF),
                          compiler_params=_params(("parallel", "parallel")), name=name)(ids, g, other)


def _scatter_side(sums):
    nw = len(sums)

    def copies(srcs, outs, send_sems, recv_sems):
        x, y, c = _coords()
        res = []
        for w in range(nw):
            for k, (px, py) in enumerate(_other_chips(x, y)):
                res.append(pltpu.make_async_remote_copy(
                    src_ref=srcs[w].at[2 * px + py], dst_ref=outs[w].at[k], send_sem=send_sems.at[3 * w + k],
                    recv_sem=recv_sems.at[3 * w + k], device_id=(px, py, c), device_id_type=MESH))
        return res

    def start(*refs):
        for cp in copies(*refs):
            cp.start()

    def finish(*refs):
        for cp in copies(*refs):
            cp.wait()

    return _Side(sums, [jax.ShapeDtypeStruct((3, *t.shape[1:]), t.dtype) for t in sums], 3 * nw, start, finish)


def _rs_sum(name, ids, own, got):
    n, half, cols = own.shape
    tr = _pick(half, 256, 16)
    nb = half // tr

    def body(ids_ref, own_ref, got_ref, out_ref):
        t = own_ref[...].astype(F32)
        for k in range(3):
            t = t + got_ref[k].astype(F32)
        out_ref[...] = t

    grid_spec = pltpu.PrefetchScalarGridSpec(
        num_scalar_prefetch=1, grid=(nb,),
        in_specs=[pl.BlockSpec((None, tr, cols), lambda i, ids_ref: (ids_ref[0], i, 0)),
                  pl.BlockSpec((3, tr, cols), lambda i, ids_ref: (0, i, 0))],
        out_specs=pl.BlockSpec((tr, cols), lambda i, ids_ref: (ids_ref[1] * nb + i, 0)))
    return pl.pallas_call(body, grid_spec=grid_spec, out_shape=jax.ShapeDtypeStruct((2 * half, cols), F32),
                          compiler_params=_params(("parallel",)), name=name)(ids, own, got)


def _join_side(totals):
    nw = len(totals)

    def start(ins, bufs, send_sems, recv_sems):
        x, y, c = _coords()
        for w in range(nw):
            half = totals[w].shape[0] // 2
            pltpu.make_async_remote_copy(
                src_ref=bufs[w].at[pl.ds(c * half, half), :], dst_ref=bufs[w].at[pl.ds(c * half, half), :],
                send_sem=send_sems.at[w], recv_sem=recv_sems.at[w], device_id=(x, y, 1 - c),
                device_id_type=MESH).start()

    def finish(ins, bufs, send_sems, recv_sems):
        x, y, c = _coords()
        for w in range(nw):
            half = totals[w].shape[0] // 2
            arrival = pltpu.make_async_remote_copy(
                src_ref=bufs[w].at[pl.ds(c * half, half), :], dst_ref=bufs[w].at[pl.ds((1 - c) * half, half), :],
                send_sem=send_sems.at[w], recv_sem=recv_sems.at[w], device_id=(x, y, 1 - c), device_id_type=MESH)
            arrival.wait_recv()
            arrival.wait_send()

    return _Side(totals, [jax.ShapeDtypeStruct(t.shape, t.dtype) for t in totals], nw, start, finish,
                 aliases={w: w for w in range(nw)})


def _gather_all(name, t):
    rows, cols = t.shape

    def body(src, out, send_sems, recv_sems, local_sem):
        x, y, c = _coords()
        me = 4 * x + 2 * y + c
        mine = pltpu.make_async_copy(src, out.at[me], local_sem)
        mine.start()
        peers = [(x ^ (k >> 2 & 1), y ^ (k >> 1 & 1), c ^ (k & 1)) for k in range(1, N_DEV)]
        sends = [pltpu.make_async_remote_copy(src_ref=src, dst_ref=out.at[me], send_sem=send_sems.at[k],
                                              recv_sem=recv_sems.at[k], device_id=peer, device_id_type=MESH)
                 for k, peer in enumerate(peers)]
        for cp in sends:
            cp.start()
        for k, (px, py, pc) in enumerate(peers):
            pltpu.make_async_remote_copy(src_ref=src, dst_ref=out.at[4 * px + 2 * py + pc], send_sem=send_sems.at[k],
                                         recv_sem=recv_sems.at[k], device_id=(px, py, pc),
                                         device_id_type=MESH).wait_recv()
        for cp in sends:
            cp.wait_send()
        mine.wait()

    vmem = pl.BlockSpec(memory_space=pltpu.VMEM)
    return pl.pallas_call(
        body, in_specs=[vmem], out_specs=vmem, out_shape=jax.ShapeDtypeStruct((N_DEV, rows, cols), t.dtype),
        scratch_shapes=[pltpu.SemaphoreType.DMA((7,)), pltpu.SemaphoreType.DMA((7,)), pltpu.SemaphoreType.DMA],
        name=name)(t)


def _adamw_math(w, g, m, v):
    m = ADAM_B1 * m + (1.0 - ADAM_B1) * g
    v = ADAM_B2 * v + (1.0 - ADAM_B2) * (g * g)
    m_hat = m / (1.0 - ADAM_B1 ** ADAM_STEP)
    v_hat = v / (1.0 - ADAM_B2 ** ADAM_STEP)
    delta = -ADAM_LR * (m_hat / (jnp.sqrt(v_hat) + ADAM_EPS) + ADAM_WD * w)
    return delta, m, v


def _adamw(name, w, g, m, v):
    _, rows, cols = w.shape
    tr = _pick(rows, 256, 8)

    def body(w_ref, g_ref, m_ref, v_ref, g_out, d_out, m_out, v_out):
        g = g_ref[...]
        g_out[...] = g
        d_out[...], m_out[...], v_out[...] = _adamw_math(w_ref[...], g, m_ref[...], v_ref[...])

    blk3 = pl.BlockSpec((None, tr, cols), lambda i: (0, i, 0))
    blk = pl.BlockSpec((tr, cols), lambda i: (i, 0))
    shape = jax.ShapeDtypeStruct((rows, cols), F32)
    return pl.pallas_call(
        body, grid=(rows // tr,), in_specs=[blk3, blk, blk3, blk3], out_specs=[blk] * 4, out_shape=[shape] * 4,
        compiler_params=_params(("parallel",)), name=name)(w, g, m, v)


def _small_reduce(name, parts, width):
    def body(*refs):
        out = refs[-1]
        out[...] = jnp.zeros_like(out)
        for k, r in enumerate(refs[:-1]):
            out[pl.ds(k, 1), :] = jnp.sum(r[...], axis=0, keepdims=True)

    vmem = pl.BlockSpec(memory_space=pltpu.VMEM)
    return pl.pallas_call(body, in_specs=[vmem] * len(parts), out_specs=vmem,
                          out_shape=jax.ShapeDtypeStruct((8, width), F32), name=name)(*parts)


def _small_adamw(name, gathered, w, m, v, loss_row, loss_scale):
    def body(gt_ref, w_ref, m_ref, v_ref, g_out, d_out, m_out, v_out, loss_out):
        g = gt_ref[0]
        for k in range(1, N_DEV):
            g = g + gt_ref[k]
        g_out[...] = g
        row = lax.broadcasted_iota(jnp.int32, g.shape, 0)
        loss_out[...] = jnp.sum(jnp.where(row == loss_row, g, 0.0), keepdims=True) * loss_scale
        d_out[...], m_out[...], v_out[...] = _adamw_math(w_ref[...], g, m_ref[...], v_ref[...])

    vmem = pl.BlockSpec(memory_space=pltpu.VMEM)
    shape = jax.ShapeDtypeStruct(w.shape, F32)
    return pl.pallas_call(body, in_specs=[vmem] * 4, out_specs=[vmem] * 5,
                          out_shape=[shape] * 4 + [jax.ShapeDtypeStruct((1, 1), F32)], name=name)(gathered, w, m, v)


def _full_from_pieces(name, pieces):
    _, rows, cols = pieces.shape
    if name in ROW_SHARDED:
        return pieces.reshape(N_CHIPS * rows, cols)
    return pieces.transpose(1, 0, 2).reshape(rows, N_CHIPS * cols)


def _column_range(segments, lo, hi):
    out, start = [], 0
    for t in segments:
        a, b = max(lo, start), min(hi, start + t.shape[1])
        if a < b:
            out.append(t[:, a - start:b - start])
        start += t.shape[1]
    return out


def _repack_w_in(pieces, hd, d):
    hh = hd // HEAD_DIM
    segs = [pieces[p] for p in range(N_CHIPS)]
    total = N_CHIPS * pieces.shape[2]
    main = jnp.concatenate(_column_range(segs, 0, 6 * hd) + _column_range(segs, 6 * hd + hh, total), axis=1)
    f = jnp.concatenate(_column_range(segs, 6 * hd, 6 * hd + hh), axis=1)
    return main, jnp.pad(f, ((0, 0), (0, LANES - hh)))


def _dw_in_pieces(dw, dw_f, hd, d):
    hh = hd // HEAD_DIM
    segs = [dw[:, :6 * hd], dw_f[:, :hh], dw[:, 6 * hd:]]
    cs = (6 * hd + hh + 2 * d) // N_CHIPS
    return jnp.stack([jnp.concatenate(_column_range(segs, p * cs, (p + 1) * cs), axis=1) for p in range(N_CHIPS)])


def _small_pack(vals, width):
    rows = []
    for name in SMALL:
        t = vals[name].reshape(1, -1)
        rows.append(jnp.pad(t, ((0, 0), (0, width - t.shape[1]))))
    rows.append(jnp.zeros((8 - len(SMALL), width), F32))
    return jnp.concatenate(rows, axis=0)


def kernel(x, ffn1_norm, ffn1_w_gate, ffn1_w_up, ffn1_w_down, mix_norm, w_in, b_forget, b_gate_dil, b_gate_fox, w_proj_dil, w_proj_fox, w_out, ffn2_norm, ffn2_w_gate, ffn2_w_up, ffn2_w_down, final_norm, loss_target, m_ffn1_norm, m_ffn1_w_gate, m_ffn1_w_up, m_ffn1_w_down, m_mix_norm, m_w_in, m_b_forget, m_b_gate_dil, m_b_gate_fox, m_w_proj_dil, m_w_proj_fox, m_w_out, m_ffn2_norm, m_ffn2_w_gate, m_ffn2_w_up, m_ffn2_w_down, m_final_norm, v_ffn1_norm, v_ffn1_w_gate, v_ffn1_w_up, v_ffn1_w_down, v_mix_norm, v_w_in, v_b_forget, v_b_gate_dil, v_b_gate_fox, v_w_proj_dil, v_w_proj_fox, v_w_out, v_ffn2_norm, v_ffn2_w_gate, v_ffn2_w_up, v_ffn2_w_down, v_final_norm):
    given = dict(locals())
    wts = {n: given[n] for n in WEIGHTS}
    mom_m = {n: given["m_" + n] for n in WEIGHTS}
    mom_v = {n: given["v_" + n] for n in WEIGHTS}
    d = x.shape[2]

    shards = {n: wts[n][0].astype(BF) for n in SHARDED}
    small = {n: wts[n] for n in SMALL}
    grad_x, grads, partials = _device_step(x[0], loss_target[0], shards, small)

    out_g, out_d, out_m, out_v = {}, {}, {}, {}
    for n in SHARDED:
        outs = _adamw("adamw_" + n, wts[n], grads[n], mom_m[n], mom_v[n])
        out_g[n], out_d[n], out_m[n], out_v[n] = (t[None] for t in outs)

    width = d
    part_rows = []
    for n in SMALL:
        t = partials[n]
        part_rows.append(jnp.pad(t, ((0, 0), (0, width - t.shape[1]))))
    part_rows.append(partials["sq"])
    local_small = _small_reduce("small_reduce", part_rows, width)
    gathered_small = _gather_all("small_gather", local_small)
    sg, sd_, sm, sv, loss = _small_adamw("small_adamw", gathered_small, _small_pack(wts, width),
                                         _small_pack(mom_m, width), _small_pack(mom_v, width), len(SMALL), 0.5 / d)
    for k, n in enumerate(SMALL):
        shp = wts[n].shape
        take = lambda t: t[k, :shp[-1]].reshape(shp)
        out_g[n], out_d[n], out_m[n], out_v[n] = take(sg), take(sd_), take(sm), take(sv)
    return (loss[0, 0], grad_x[None], *[out_g[n] for n in WEIGHTS], *[out_d[n] for n in WEIGHTS],
            *[out_m[n] for n in WEIGHTS], *[out_v[n] for n in WEIGHTS])
```

```python
import functools
import math

import jax
import jax.numpy as jnp
from jax import lax
from jax.experimental import pallas as pl
from jax.experimental.pallas import tpu as pltpu

HEAD_DIM = 128
ROPE_DIM = HEAD_DIM // 4
ROPE_THETA = 500000.0
DIL_PATTERNS = ((128, 1), (512, 4), (2048, 16))
MAX_WINDOW = 2048
NORM_EPS = 1e-6
ADAM_LR = 0.001
ADAM_B1 = 0.9
ADAM_B2 = 0.999
ADAM_EPS = 1e-08
ADAM_WD = 0.01
ADAM_STEP = 10

BF = jnp.bfloat16
F32 = jnp.float32
NEG = -1e30
LANES = 128
ATT_BLOCK = 512
PREP_COLS = 512
ATT_HEADS = 2
ATT_HEADS_FWD = 4
VMEM_LIMIT = 56 * 1024 * 1024
MM_VMEM_BUDGET = 46 * 1024 * 1024
N_CHIPS = 4
N_DEV = 8
MESH = pl.DeviceIdType.MESH

SHARDED = ("ffn1_w_gate", "ffn1_w_up", "ffn1_w_down", "w_in", "w_proj_dil", "w_proj_fox", "w_out",
           "ffn2_w_gate", "ffn2_w_up", "ffn2_w_down")
ROW_SHARDED = ("ffn1_w_down", "w_out", "ffn2_w_down")
SMALL = ("ffn1_norm", "mix_norm", "b_forget", "b_gate_dil", "b_gate_fox", "ffn2_norm", "final_norm")
WEIGHTS = ("ffn1_norm", "ffn1_w_gate", "ffn1_w_up", "ffn1_w_down", "mix_norm", "w_in", "b_forget",
           "b_gate_dil", "b_gate_fox", "w_proj_dil", "w_proj_fox", "w_out", "ffn2_norm", "ffn2_w_gate",
           "ffn2_w_up", "ffn2_w_down", "final_norm")


def _pick(n, target, align):
    best = None
    for d in range(align, min(n, target) + 1, align):
        if n % d == 0:
            best = d
    return n if best is None else best


def _params(sem=None):
    return pltpu.CompilerParams(dimension_semantics=sem, vmem_limit_bytes=VMEM_LIMIT)


_DIMS = {"nn": (((1,), (0,)), ((), ())), "nt": (((1,), (1,)), ((), ())), "tn": (((0,), (0,)), ((), ()))}


class _SemsFrom:
    def __init__(self, sems, first):
        self.sems, self.first = sems, first

    @property
    def at(self):
        return self

    def __getitem__(self, k):
        return self.sems.at[self.first + k]


class _Side:
    def __init__(self, inputs, out_shapes, n_sems, start, finish, aliases=None):
        self.inputs, self.out_shapes, self.n_sems = list(inputs), list(out_shapes), n_sems
        self.start, self.finish, self.aliases = start, finish, aliases or {}

    def scratch(self):
        return [pltpu.SemaphoreType.DMA((self.n_sems,)), pltpu.SemaphoreType.DMA((self.n_sems,))]

    def beside(self, other):
        n_in, n_out, n_sems = len(self.inputs), len(self.out_shapes), self.n_sems

        def part(fn_a, fn_b):
            def run(ins, outs, send_sems, recv_sems):
                fn_a(ins[:n_in], outs[:n_out], send_sems, recv_sems)
                fn_b(ins[n_in:], outs[n_out:], _SemsFrom(send_sems, n_sems), _SemsFrom(recv_sems, n_sems))
            return run

        aliases = {**self.aliases, **{n_in + t: n_out + o for t, o in other.aliases.items()}}
        return _Side(self.inputs + other.inputs, self.out_shapes + other.out_shapes, n_sems + other.n_sems,
                     part(self.start, other.start), part(self.finish, other.finish), aliases=aliases)

    def call(self, name):
        n_in, n_out = len(self.inputs), len(self.out_shapes)

        def body(*refs):
            ins, outs, sems = refs[:n_in], refs[n_in:n_in + n_out], refs[n_in + n_out:]
            self.start(ins, outs, *sems)
            self.finish(ins, outs, *sems)

        return pl.pallas_call(body, in_specs=[ANY_SPEC] * n_in, out_specs=[ANY_SPEC] * n_out, out_shape=self.out_shapes,
                              input_output_aliases=self.aliases, scratch_shapes=self.scratch(), name=name)(*self.inputs)


def _mm(name, pairs, mode, m, n, k, *, epilogue, out_dtypes, extras=(), rows=(), n_colsum=0,
        sum_pairs=False, tm=1024, tn=1152, tk=2048, piece_layout=False, side=None, b_off=0, keep_tn=False):
    m_align = LANES if mode == "tn" else 8
    tm = _pick(m, tm, m_align)
    tn = n // N_CHIPS if piece_layout else _pick(n, tn, LANES)
    tk = _pick(k, tk, LANES)
    n_acc = 1 if sum_pairs else len(pairs)
    lhs = []
    for a, _ in pairs:
        if not any(a is t for t in lhs):
            lhs.append(a)
    lhs_of = [next(t for t in range(len(lhs)) if lhs[t] is a) for a, _ in pairs]
    n_mm = len(lhs) + len(pairs)
    n_in = n_mm + len(extras) + len(rows)
    n_out = len(out_dtypes) + n_colsum

    def vmem_bytes(tm_, tn_, tk_):
        tiles = sum(tm_ * tk_ * a.dtype.itemsize for a in lhs) + sum(tn_ * tk_ * b.dtype.itemsize for _, b in pairs)
        tiles += sum(tm_ * (arr.shape[1] if off is None else tn_) * arr.dtype.itemsize for arr, off in extras)
        tiles += sum(tm_ * tn_ * jnp.dtype(dt).itemsize for dt in out_dtypes)
        return 2 * tiles + (n_acc + len(extras) + len(out_dtypes)) * tm_ * tn_ * 4

    while vmem_bytes(tm, tn, tk) > MM_VMEM_BUDGET:
        if tn > 512 and not piece_layout and not keep_tn:
            tn = _pick(n, tn - LANES, LANES)
        elif tm > 512:
            tm = _pick(m, tm - m_align, m_align)
        elif tk > 512:
            tk = _pick(k, tk - LANES, LANES)
        elif tm > 256:
            tm = _pick(m, tm - m_align, m_align)
        else:
            break
    nk = k // tk

    n_side_in = len(side.inputs) if side else 0
    n_side_out = len(side.out_shapes) if side else 0
    grid = (m // tm, n // tn, nk)

    def body(*refs):
        ins, refs = refs[:n_in], refs[n_in:]
        side_ins, refs = refs[:n_side_in], refs[n_side_in:]
        outs, refs = refs[:n_out], refs[n_out:]
        side_outs, refs = refs[:n_side_out], refs[n_side_out:]
        accs, side_sems = refs[:n_acc], refs[n_acc:]
        kk = pl.program_id(2)
        if side:
            at = [pl.program_id(t) for t in range(3)]

            @pl.when(jnp.logical_and(jnp.logical_and(at[0] == 0, at[1] == 0), at[2] == 0))
            def _():
                side.start(side_ins, side_outs, *side_sems)

        @pl.when(kk == 0)
        def _():
            for acc in accs:
                acc[...] = jnp.zeros_like(acc)

        a_tiles = [r[...].astype(BF) for r in ins[:len(lhs)]]
        for p in range(len(pairs)):
            b = ins[len(lhs) + p][...].astype(BF)
            accs[0 if sum_pairs else p][...] += lax.dot_general(a_tiles[lhs_of[p]], b, _DIMS[mode],
                                                                preferred_element_type=F32)

        @pl.when(kk == nk - 1)
        def _():
            ex = [r[...] for r in ins[n_mm:n_mm + len(extras)]]
            rw = [r[...] for r in ins[n_mm + len(extras):]]
            res = epilogue([acc[...] for acc in accs], ex, rw)
            for o, r in zip(outs, res):
                o[...] = r.astype(o.dtype)

        if side:
            @pl.when(jnp.logical_and(jnp.logical_and(at[0] == grid[0] - 1, at[1] == grid[1] - 1), at[2] == nk - 1))
            def _():
                side.finish(side_ins, side_outs, *side_sems)

    in_specs, args = [], []
    for a in lhs:
        if mode == "tn":
            in_specs.append(pl.BlockSpec((tk, tm), lambda i, j, kk: (kk, i)))
        else:
            in_specs.append(pl.BlockSpec((tm, tk), lambda i, j, kk: (i, kk)))
        args.append(a)
    assert b_off % tn == 0 and (b_off == 0 or mode == "nn")
    for _, b in pairs:
        if mode == "nt":
            in_specs.append(pl.BlockSpec((tn, tk), lambda i, j, kk: (j, kk)))
        else:
            in_specs.append(pl.BlockSpec((tk, tn), functools.partial(lambda i, j, kk, o: (kk, j + o), o=b_off // tn)))
        args.append(b)
    for arr, off in extras:
        if off is None:
            in_specs.append(pl.BlockSpec((tm, arr.shape[1]), lambda i, j, kk: (i, 0)))
        else:
            assert off % tn == 0
            in_specs.append(pl.BlockSpec((tm, tn), functools.partial(lambda i, j, kk, o: (i, j + o), o=off // tn)))
        args.append(arr)
    for arr in rows:
        in_specs.append(pl.BlockSpec((1, tn), lambda i, j, kk: (0, j)))
        args.append(arr)
    if piece_layout:
        out_specs = [pl.BlockSpec((None, tm, tn), lambda i, j, kk: (j, i, 0)) for _ in out_dtypes]
        out_shape = [jax.ShapeDtypeStruct((n // tn, m, tn), d) for d in out_dtypes]
    else:
        out_specs = [pl.BlockSpec((tm, tn), lambda i, j, kk: (i, j)) for _ in out_dtypes]
        out_shape = [jax.ShapeDtypeStruct((m, n), d) for d in out_dtypes]
    for _ in range(n_colsum):
        out_specs.append(pl.BlockSpec((None, 1, tn), lambda i, j, kk: (i, 0, j)))
        out_shape.append(jax.ShapeDtypeStruct((m // tm, 1, n), F32))
    scratch = [pltpu.VMEM((tm, tn), F32) for _ in range(n_acc)]
    if side is None:
        return pl.pallas_call(
            body, grid=grid, in_specs=in_specs, out_specs=out_specs, out_shape=out_shape, scratch_shapes=scratch,
            compiler_params=_params(("parallel", "parallel", "arbitrary")), name=name)(*args)
    res = pl.pallas_call(
        body, grid=grid, in_specs=in_specs + [ANY_SPEC] * n_side_in, out_specs=out_specs + [ANY_SPEC] * n_side_out,
        out_shape=out_shape + side.out_shapes, scratch_shapes=scratch + side.scratch(),
        input_output_aliases={n_in + t: n_out + o for t, o in side.aliases.items()},
        compiler_params=_params(("arbitrary", "arbitrary", "arbitrary")), name=name)(*args, *side.inputs)
    return res[:n_out], res[n_out:]


def _col_pieces(full):
    rows, cols = full.shape
    return full.reshape(rows, N_CHIPS, cols // N_CHIPS).transpose(1, 0, 2)


def _dw_col_pieces(name, a, b, m, n, k, side=None):
    ident = lambda accs, ex, rw: (accs[0],)
    aligned = (n // N_CHIPS) % LANES == 0
    res = _mm(name, [(a, b)], "tn", m, n, k, epilogue=ident, out_dtypes=(F32,), tm=512 if aligned else 1024,
              piece_layout=aligned, side=side)
    (out,), side_res = res if side else (res, None)
    out = out if aligned else _col_pieces(out)
    return (out, side_res) if side else out


def _hosted(sides, got, key, call):
    make = sides.get(key) if sides else None
    if make is None:
        return call(None)
    outs, got[key] = call(make(got))
    return outs


def _sigmoid(z):
    return 0.5 * jnp.tanh(0.5 * z) + 0.5


def _row_tile(s):
    return _pick(s, 256, 8)


def _fold8(t):
    r, d = t.shape
    return jnp.sum(t.reshape(r // 8, 8, d), axis=0)


def _rms_fwd(name, x, g):
    s, d = x.shape
    tr = _row_tile(s)

    def body(x_ref, g_ref, h_ref):
        xf = x_ref[...]
        y = xf * lax.rsqrt(jnp.mean(xf * xf, axis=-1, keepdims=True) + NORM_EPS)
        h_ref[...] = (y * g_ref[...]).astype(BF)

    return pl.pallas_call(
        body, grid=(s // tr,),
        in_specs=[pl.BlockSpec((tr, d), lambda i: (i, 0)), pl.BlockSpec((1, d), lambda i: (0, 0))],
        out_specs=pl.BlockSpec((tr, d), lambda i: (i, 0)), out_shape=jax.ShapeDtypeStruct((s, d), BF),
        compiler_params=_params(("parallel",)), name=name)(x, g)


def _rms_bwd(name, x, g, dh, dres, side=None):
    s, d = x.shape
    tr = _row_tile(s)
    steps = s // tr
    n_side_in = len(side.inputs) if side else 0
    n_side_out = len(side.out_shapes) if side else 0

    def body(*refs):
        x_ref, g_ref, dh_ref, dres_ref = refs[:4]
        side_ins, refs = refs[4:4 + n_side_in], refs[4 + n_side_in:]
        dx_ref, dxb_ref, dg_ref = refs[:3]
        side_outs, side_sems = refs[3:3 + n_side_out], refs[3 + n_side_out:]

        @pl.when(pl.program_id(0) == 0)
        def _():
            dg_ref[...] = jnp.zeros_like(dg_ref)
            if side:
                side.start(side_ins, side_outs, *side_sems)

        xf = x_ref[...]
        rstd = lax.rsqrt(jnp.mean(xf * xf, axis=-1, keepdims=True) + NORM_EPS)
        xhat = xf * rstd
        dhf = dh_ref[...]
        dg_ref[...] += _fold8(dhf * xhat)
        dxh = dhf * g_ref[...]
        dx = dres_ref[...] + rstd * (dxh - xhat * jnp.mean(dxh * xhat, axis=-1, keepdims=True))
        dx_ref[...] = dx
        dxb_ref[...] = dx.astype(BF)

        if side:
            @pl.when(pl.program_id(0) == steps - 1)
            def _():
                side.finish(side_ins, side_outs, *side_sems)

    blk = pl.BlockSpec((tr, d), lambda i: (i, 0))
    res = pl.pallas_call(
        body, grid=(steps,),
        in_specs=[blk, pl.BlockSpec((1, d), lambda i: (0, 0)), blk, blk] + [ANY_SPEC] * n_side_in,
        out_specs=[blk, blk, pl.BlockSpec((8, d), lambda i: (0, 0))] + [ANY_SPEC] * n_side_out,
        out_shape=[jax.ShapeDtypeStruct((s, d), F32), jax.ShapeDtypeStruct((s, d), BF),
                   jax.ShapeDtypeStruct((8, d), F32)] + (side.out_shapes if side else []),
        scratch_shapes=side.scratch() if side else [],
        compiler_params=_params(("arbitrary",)), name=name)(x, g, dh, dres, *(side.inputs if side else []))
    return (res[:3], res[3:]) if side else res


def _final(name, x, g, tgt):
    s, d = x.shape
    tr = _row_tile(s)

    def body(x_ref, g_ref, t_ref, dx_ref, dxb_ref, dg_ref, sq_ref):
        @pl.when(pl.program_id(0) == 0)
        def _():
            dg_ref[...] = jnp.zeros_like(dg_ref)
            sq_ref[...] = jnp.zeros_like(sq_ref)

        xf = x_ref[...]
        rstd = lax.rsqrt(jnp.mean(xf * xf, axis=-1, keepdims=True) + NORM_EPS)
        xhat = xf * rstd
        gf = g_ref[...]
        err = xhat * gf - t_ref[...]
        sq_ref[...] += _fold8(err * err)
        dy = err * (1.0 / d)
        dg_ref[...] += _fold8(dy * xhat)
        dxh = dy * gf
        dx = rstd * (dxh - xhat * jnp.mean(dxh * xhat, axis=-1, keepdims=True))
        dx_ref[...] = dx
        dxb_ref[...] = dx.astype(BF)

    blk = pl.BlockSpec((tr, d), lambda i: (i, 0))
    acc = pl.BlockSpec((8, d), lambda i: (0, 0))
    return pl.pallas_call(
        body, grid=(s // tr,), in_specs=[blk, pl.BlockSpec((1, d), lambda i: (0, 0)), blk],
        out_specs=[blk, blk, acc, acc],
        out_shape=[jax.ShapeDtypeStruct((s, d), F32), jax.ShapeDtypeStruct((s, d), BF),
                   jax.ShapeDtypeStruct((8, d), F32), jax.ShapeDtypeStruct((8, d), F32)],
        compiler_params=_params(("arbitrary",)), name=name)(x, g, tgt)


def _ffn_fwd(tag, x, g, w_gate, w_up, get_w_down, sides=None):
    s, d = x.shape
    f = w_gate.shape[1]
    got = {}
    h = _rms_fwd(tag + "_rms", x, g)

    def up_epi(accs, ex, rw):
        a, b = accs
        return a, b, a * _sigmoid(a) * b

    a, b, act = _hosted(sides, got, "up", lambda side: _mm(
        tag + "_up", [(h, w_gate), (h, w_up)], "nn", s, f, d, epilogue=up_epi, out_dtypes=(BF, BF, BF), side=side))

    def down_epi(accs, ex, rw):
        return (ex[0] + 0.5 * accs[0],)

    w_down = get_w_down(got)
    (y,) = _hosted(sides, got, "down", lambda side: _mm(
        tag + "_down", [(act, w_down)], "nn", s, d, f, epilogue=down_epi, out_dtypes=(F32,), extras=[(x, 0)],
        side=side))
    return y, (h, a, b, act), got


def _ffn_bwd(tag, x, g, w_gate, w_up, w_down, saved, dy, dy_bf, sides=None):
    s, d = x.shape
    f = w_gate.shape[1]
    h, a, b, act = saved
    got = {}

    def act_epi(accs, ex, rw):
        dact = 0.5 * accs[0]
        av, bv = ex[0].astype(F32), ex[1].astype(F32)
        sg = _sigmoid(av)
        return dact * bv * (sg * (1.0 + av * (1.0 - sg))), dact * (av * sg)

    da, db = _hosted(sides, got, "dact", lambda side: _mm(
        tag + "_dact", [(dy_bf, w_down)], "nt", s, f, d, epilogue=act_epi, out_dtypes=(BF, BF),
        extras=[(a, 0), (b, 0)], side=side))
    ident = lambda accs, ex, rw: (accs[0],)
    (dw_down,) = _hosted(sides, got, "dwd", lambda side: _mm(
        tag + "_dwd", [(act, dy_bf)], "tn", f, d, s, epilogue=lambda accs, ex, rw: (0.5 * accs[0],),
        out_dtypes=(F32,), side=side))
    got["dw_down"] = dw_down = dw_down.reshape(N_CHIPS, f // N_CHIPS, d)
    got["dw_gate"] = dw_gate = _hosted(sides, got, "dwg", lambda side: _dw_col_pieces(
        tag + "_dwg", h, da, d, f, s, side=side))
    got["dw_up"] = dw_up = _hosted(sides, got, "dwu", lambda side: _dw_col_pieces(
        tag + "_dwu", h, db, d, f, s, side=side))
    (dh,) = _hosted(sides, got, "dh", lambda side: _mm(
        tag + "_dh", [(da, w_gate), (db, w_up)], "nt", s, d, f, epilogue=ident, out_dtypes=(F32,), sum_pairs=True,
        side=side))
    dx, dx_bf, dg = _hosted(sides, got, "rmsb", lambda side: _rms_bwd(tag + "_rmsb", x, g, dh, dy, side=side))
    return dx, dx_bf, dg, dw_gate, dw_up, dw_down, got


def _rope_tables(s):
    half = ROPE_DIM // 2
    pos = jnp.arange(s, dtype=F32)
    inv_freq = ROPE_THETA ** (-jnp.arange(0, ROPE_DIM, 2, dtype=F32) / ROPE_DIM)
    ang = pos[:, None] * inv_freq[None, :]
    cos, sin = jnp.cos(ang), jnp.sin(ang)
    rest = HEAD_DIM - ROPE_DIM
    cos_t = jnp.concatenate([cos, cos, jnp.ones((s, rest), F32)], axis=-1)
    sin_t = jnp.concatenate([-sin, sin, jnp.zeros((s, rest), F32)], axis=-1)
    return cos_t, sin_t


def _swap_halves(t):
    lane = lax.broadcasted_iota(jnp.int32, t.shape, 1) & (HEAD_DIM - 1)
    half = ROPE_DIM // 2
    return jnp.where(lane < half, pltpu.roll(t, t.shape[1] - half, 1), pltpu.roll(t, half, 1))


def _dil_bias(blk):
    n_delta = MAX_WINDOW // blk + 1
    delta = jnp.arange(n_delta, dtype=jnp.int32)[:, None, None]
    r = jnp.arange(blk, dtype=jnp.int32)[None, None, :]
    c = jnp.arange(blk, dtype=jnp.int32)[None, :, None]
    o = delta * blk + r - c
    mult = jnp.zeros(o.shape, F32)
    for w, dd in DIL_PATTERNS:
        mult = mult + ((o >= 0) & (o <= w) & (o % dd == 0)).astype(F32)
    return jnp.where(mult > 0, jnp.log(jnp.maximum(mult, 1.0)), NEG)


def _att_block(s):
    return _pick(s, ATT_BLOCK, LANES)


def _fox_aug(name, c_pad, qkv, q_off, k_off, n_heads):
    s = c_pad.shape[0]
    tr = _pick(s, 1024, 16)

    def body(c_ref, q_ref, k_ref, qc_ref, kc_ref):
        h = pl.program_id(1)
        lane = lax.broadcasted_iota(jnp.int32, (tr, LANES), 1)
        ch = jnp.sum(jnp.where(lane == h, c_ref[...], 0.0), axis=1, keepdims=True)
        hi, mid, lo = (t.astype(F32) for t in _split3(ch))
        zero = jnp.zeros((tr, LANES), F32)
        is_hi = jnp.logical_or(lane == 0, lane == 3)
        is_mid = jnp.logical_or(lane == 1, lane == 4)
        parts = jnp.where(is_hi, hi, jnp.where(is_mid, mid, lo))
        qc_ref[:, :HEAD_DIM] = q_ref[...]
        kc_ref[:, :HEAD_DIM] = k_ref[...]
        qc_ref[:, HEAD_DIM:] = jnp.where(lane < 3, 1.0, jnp.where(lane < 6, parts, zero)).astype(BF)
        kc_ref[:, HEAD_DIM:] = jnp.where(lane < 3, -parts, jnp.where(lane < 6, 1.0, zero)).astype(BF)

    head = lambda o: pl.BlockSpec((tr, HEAD_DIM), functools.partial(lambda i, h, o: (i, o + h), o=o))
    spec = pl.BlockSpec((tr, 2 * HEAD_DIM), lambda i, h: (i, h))
    shape = jax.ShapeDtypeStruct((s, n_heads * 2 * HEAD_DIM), BF)
    return pl.pallas_call(
        body, grid=(s // tr, n_heads),
        in_specs=[pl.BlockSpec((tr, LANES), lambda i, h: (i, 0)), head(q_off), head(k_off)],
        out_specs=[spec, spec], out_shape=[shape, shape],
        compiler_params=_params(("parallel", "arbitrary")), name=name)(c_pad, qkv, qkv)


def _causal_mask(st):
    kpos = lax.broadcasted_iota(jnp.int32, st.shape, 0)
    qpos = lax.broadcasted_iota(jnp.int32, st.shape, 1)
    return jnp.where(kpos <= qpos, st, NEG)


def _flash_fwd(name, q_src, k_src, v_src, n_heads, *, fox, tab_t=None):
    (q_arr, q_off, qw), (k_arr, k_off, kw), (v_arr, v_off, _) = q_src, k_src, v_src
    s = q_arr.shape[0]
    blk = _att_block(s)
    nq = s // blk
    n_delta = MAX_WINDOW // blk + 1
    grp = ATT_HEADS_FWD
    assert qw == kw and n_heads % grp == 0 and q_off % grp == 0 and k_off % grp == 0 and v_off % grp == 0
    wide = grp * HEAD_DIM

    def body(*refs):
        if fox:
            q_ref, k_ref, v_ref, o_ref, lse_ref, acc, m_s, l_s = refs
        else:
            q_ref, k_ref, v_ref, tab_ref, o_ref, lse_ref, acc, m_s, l_s = refs
        i = pl.program_id(1)
        acc[...] = jnp.zeros_like(acc)
        m_s[...] = jnp.full_like(m_s, NEG)
        l_s[...] = jnp.zeros_like(l_s)

        bk = 2 * blk if fox and s % (2 * blk) == 0 else blk
        per_key_block = bk // blk

        def step(j, diagonal):
            ks = pl.ds(pl.multiple_of(j * bk, bk), bk)
            for g in range(grp):
                cols = slice(g * HEAD_DIM, (g + 1) * HEAD_DIM)
                qk_cols = slice(g * qw, (g + 1) * qw)
                st = lax.dot_general(k_ref[ks, qk_cols], q_ref[:, qk_cols], _DIMS["nt"], preferred_element_type=F32)
                if fox:
                    if diagonal:
                        kpos = lax.broadcasted_iota(jnp.int32, st.shape, 0) + j * bk
                        qpos = lax.broadcasted_iota(jnp.int32, st.shape, 1) + i * blk
                        st = jnp.where(kpos <= qpos, st, NEG)
                else:
                    st = st + tab_ref[i - j]
                m_prev = m_s[g]
                m_new = jnp.maximum(m_prev, jnp.max(st, axis=0, keepdims=True))
                alpha = jnp.exp(m_prev - m_new)
                p = jnp.exp(st - m_new)
                l_s[g] = alpha * l_s[g] + jnp.sum(p, axis=0, keepdims=True)
                acc[g] = alpha * acc[g] + lax.dot_general(v_ref[ks, cols], p.astype(BF), _DIMS["tn"],
                                                          preferred_element_type=F32)
                m_s[g] = m_new

        def loop_step(j, carry):
            step(j, False)
            return carry

        if fox:
            lax.fori_loop(0, i // per_key_block, loop_step, 0)
            step(i // per_key_block, True)
        else:
            lax.fori_loop(jnp.maximum(i - (n_delta - 1), 0), i + 1, loop_step, 0)
        for g in range(grp):
            o_ref[:, g * HEAD_DIM:(g + 1) * HEAD_DIM] = (acc[g] / l_s[g]).T.astype(o_ref.dtype)
            lse_ref[g] = m_s[g] + jnp.log(l_s[g])

    off = lambda o: functools.partial(lambda h, i, o: (0, o + h), o=o // grp)
    in_specs = [pl.BlockSpec((blk, grp * qw), functools.partial(lambda h, i, o: (i, o + h), o=q_off // grp)),
                pl.BlockSpec((s, grp * kw), off(k_off), pipeline_mode=pl.Buffered(1)),
                pl.BlockSpec((s, wide), off(v_off), pipeline_mode=pl.Buffered(1))]
    args = [q_arr, k_arr, v_arr]
    if not fox:
        in_specs.append(pl.BlockSpec((n_delta, blk, blk), lambda h, i: (0, 0, 0)))
        args.append(tab_t)
    return pl.pallas_call(
        body, grid=(n_heads // grp, nq), in_specs=in_specs,
        out_specs=[pl.BlockSpec((blk, wide), lambda h, i: (i, h)),
                   pl.BlockSpec((grp, None, 1, blk), lambda h, i: (h, i, 0, 0))],
        out_shape=[jax.ShapeDtypeStruct((s, n_heads * HEAD_DIM), BF),
                   jax.ShapeDtypeStruct((n_heads, nq, 1, blk), F32)],
        scratch_shapes=[pltpu.VMEM((grp, HEAD_DIM, blk), F32), pltpu.VMEM((grp, 1, blk), F32),
                        pltpu.VMEM((grp, 1, blk), F32)],
        compiler_params=_params(("parallel", "parallel")), name=name)(*args)


def _att_delta(name, do, o, n_heads):
    s = do.shape[0]
    blk = _pick(s, 1024, 16)

    def body(do_ref, o_ref, d_ref):
        d_ref[...] = jnp.sum(do_ref[...].astype(F32) * o_ref[...].astype(F32), axis=-1, keepdims=True)

    spec = pl.BlockSpec((blk, HEAD_DIM), lambda h, i: (i, h))
    return pl.pallas_call(
        body, grid=(n_heads, s // blk), in_specs=[spec, spec],
        out_specs=pl.BlockSpec((None, blk, 1), lambda h, i: (h, i, 0)),
        out_shape=jax.ShapeDtypeStruct((n_heads, s, 1), F32),
        compiler_params=_params(("parallel", "parallel")), name=name)(do, o)


def _flash_bwd(name, q_src, k_src, v_src, do, lse_row, delta_row, n_heads, *, fox, tab_t=None):
    (q_arr, q_off, qw), (k_arr, k_off, kw), (v_arr, v_off, _) = q_src, k_src, v_src
    s = q_arr.shape[0]
    blk = _att_block(s)
    nq = s // blk
    n_delta = MAX_WINDOW // blk + 1
    grp = ATT_HEADS
    assert qw == kw and n_heads % grp == 0 and q_off % grp == 0 and k_off % grp == 0 and v_off % grp == 0
    wide = grp * HEAD_DIM

    def body(*refs):
        if fox:
            (q_ref, do_ref, k_ref, v_ref, lse_ref, dl_ref,
             dq_ref, dk_ref, dv_ref, dc_ref, dcq_ref, dk_acc, dv_acc, dc_acc) = refs
        else:
            (q_ref, do_ref, k_ref, v_ref, lse_ref, dl_ref, tab_ref,
             dq_ref, dk_ref, dv_ref, dk_acc, dv_acc) = refs
        j = pl.program_id(1)

        @pl.when(j == 0)
        def _():
            dq_ref[...] = jnp.zeros_like(dq_ref)
            if fox:
                dcq_ref[...] = jnp.zeros_like(dcq_ref)

        dk_acc[...] = jnp.zeros_like(dk_acc)
        dv_acc[...] = jnp.zeros_like(dv_acc)
        if fox:
            dc_acc[...] = jnp.zeros_like(dc_acc)

        def step(i, diagonal):
            qs = pl.ds(pl.multiple_of(i * blk, blk), blk)
            for g in range(grp):
                cols = slice(g * HEAD_DIM, (g + 1) * HEAD_DIM)
                qk_cols = slice(g * qw, (g + 1) * qw)
                plain = slice(g * qw, g * qw + HEAD_DIM)
                kb, vb = k_ref[:, plain], v_ref[:, cols]
                qb, dob = q_ref[qs, plain], do_ref[qs, cols]
                st = lax.dot_general(k_ref[:, qk_cols], q_ref[qs, qk_cols], _DIMS["nt"], preferred_element_type=F32)
                if fox:
                    if diagonal:
                        st = _causal_mask(st)
                else:
                    st = st + tab_ref[i - j]
                pt = jnp.exp(st - lse_ref[g, i])
                dv_acc[:, cols] += jnp.dot(pt.astype(BF), dob, preferred_element_type=F32)
                dpt = lax.dot_general(vb, dob, _DIMS["nt"], preferred_element_type=F32)
                dst = pt * (dpt - dl_ref[g, i])
                dsb = dst.astype(BF)
                dk_acc[:, cols] += jnp.dot(dsb, qb, preferred_element_type=F32)
                dq_ref[qs, cols] += lax.dot_general(dsb, kb, _DIMS["tn"], preferred_element_type=F32)
                if fox:
                    folded = dst[:, :LANES]
                    for part in range(1, blk // LANES):
                        folded = folded + dst[:, part * LANES:(part + 1) * LANES]
                    dc_acc[g] -= folded
                    dcq_ref[g, i] += jnp.sum(dst, axis=0, keepdims=True)

        def loop_step(i, carry):
            step(i, False)
            return carry

        if fox:
            step(j, True)
            lax.fori_loop(j + 1, nq, loop_step, 0)
        else:
            lax.fori_loop(j, jnp.minimum(nq, j + n_delta), loop_step, 0)
        dk_ref[...] = dk_acc[...]
        dv_ref[...] = dv_acc[...].astype(dv_ref.dtype)
        if fox:
            for g in range(grp):
                dc_ref[g] = jnp.sum(dc_acc[g], axis=-1, keepdims=True)

    full = lambda o, wd=wide: pl.BlockSpec((s, wd), functools.partial(lambda h, j, o: (0, o + h), o=o // grp))
    tile = lambda o, wd=wide: pl.BlockSpec((blk, wd), functools.partial(lambda h, j, o: (j, o + h), o=o // grp))
    per_q = pl.BlockSpec((grp, nq, 1, blk), lambda h, j: (h, 0, 0, 0))
    per_k = pl.BlockSpec((grp, blk, 1), lambda h, j: (h, j, 0))
    in_specs = [full(q_off, grp * qw), full(0), tile(k_off, grp * kw), tile(v_off), per_q, per_q]
    args = [q_arr, do, k_arr, v_arr, lse_row, delta_row]
    out_specs = [full(0), tile(0), tile(0)]
    hd = n_heads * HEAD_DIM
    out_shape = [jax.ShapeDtypeStruct((s, hd), F32), jax.ShapeDtypeStruct((s, hd), F32),
                 jax.ShapeDtypeStruct((s, hd), BF)]
    scratch = [pltpu.VMEM((blk, wide), F32), pltpu.VMEM((blk, wide), F32)]
    if fox:
        out_specs += [per_k, per_q]
        out_shape += [jax.ShapeDtypeStruct((n_heads, s, 1), F32), jax.ShapeDtypeStruct((n_heads, nq, 1, blk), F32)]
        scratch.append(pltpu.VMEM((grp, blk, LANES), F32))
    else:
        in_specs.append(pl.BlockSpec((n_delta, blk, blk), lambda h, j: (0, 0, 0)))
        args.append(tab_t)
    return pl.pallas_call(
        body, grid=(n_heads // grp, nq), in_specs=in_specs, out_specs=out_specs, out_shape=out_shape,
        scratch_shapes=scratch, compiler_params=_params(("parallel", "arbitrary")), name=name)(*args)


def _split3(t):
    hi = t.astype(BF)
    r1 = t - hi.astype(F32)
    mid = r1.astype(BF)
    lo = (r1 - mid.astype(F32)).astype(BF)
    return hi, mid, lo


def _tri_dot(tri, t):
    hi, mid, lo = _split3(t)
    return (jnp.dot(tri, hi, preferred_element_type=F32) + jnp.dot(tri, mid, preferred_element_type=F32)
            + jnp.dot(tri, lo, preferred_element_type=F32))


def _log_sigmoid(z):
    return jnp.minimum(z, 0.0) - jnp.log(1.0 + jnp.exp(-jnp.abs(z)))


def _forget_cumsum(name, proj, f_col, bias):
    s = proj.shape[0]
    blk = _att_block(s)

    def body(f_ref, b_ref, c_ref, carry):
        @pl.when(pl.program_id(0) == 0)
        def _():
            carry[...] = jnp.zeros_like(carry)

        lf = _log_sigmoid(f_ref[...] + b_ref[...])
        r = lax.broadcasted_iota(jnp.int32, (blk, blk), 0)
        c = lax.broadcasted_iota(jnp.int32, (blk, blk), 1)
        tri = (c <= r).astype(BF)
        c_ref[...] = _tri_dot(tri, lf) + carry[...]
        carry[...] = c_ref[pl.ds(blk - 1, 1), :]

    return pl.pallas_call(
        body, grid=(s // blk,),
        in_specs=[pl.BlockSpec((blk, LANES), lambda i: (i, f_col)), pl.BlockSpec((1, LANES), lambda i: (0, 0))],
        out_specs=pl.BlockSpec((blk, LANES), lambda i: (i, 0)), out_shape=jax.ShapeDtypeStruct((s, LANES), F32),
        scratch_shapes=[pltpu.VMEM((1, LANES), F32)],
        compiler_params=_params(("arbitrary",)), name=name)(proj, bias)


def _forget_bwd(name, proj, f_col, bias, dc):
    s = proj.shape[0]
    blk = _att_block(s)
    nb = s // blk

    def body(f_ref, b_ref, dc_ref, df_ref, db_ref, carry):
        @pl.when(pl.program_id(0) == 0)
        def _():
            carry[...] = jnp.zeros_like(carry)
            db_ref[...] = jnp.zeros_like(db_ref)

        r = lax.broadcasted_iota(jnp.int32, (blk, blk), 0)
        c = lax.broadcasted_iota(jnp.int32, (blk, blk), 1)
        tri = (c >= r).astype(BF)
        r = lax.broadcasted_iota(jnp.int32, (blk, LANES), 0)
        dlf = _tri_dot(tri, dc_ref[...]) + carry[...]
        carry[...] = jnp.sum(jnp.where(r == 0, dlf, 0.0), axis=0, keepdims=True)
        dz = dlf * _sigmoid(-(f_ref[...] + b_ref[...]))
        df_ref[...] = dz.astype(BF)
        db_ref[...] += _fold8(dz)

    rev = lambda i: (nb - 1 - i, 0)
    return pl.pallas_call(
        body, grid=(nb,),
        in_specs=[pl.BlockSpec((blk, LANES), lambda i: (nb - 1 - i, f_col)), pl.BlockSpec((1, LANES), lambda i: (0, 0)),
                  pl.BlockSpec((blk, LANES), rev)],
        out_specs=[pl.BlockSpec((blk, LANES), rev), pl.BlockSpec((8, LANES), lambda i: (0, 0))],
        out_shape=[jax.ShapeDtypeStruct((s, LANES), BF), jax.ShapeDtypeStruct((8, LANES), F32)],
        scratch_shapes=[pltpu.VMEM((1, LANES), F32)],
        compiler_params=_params(("arbitrary",)), name=name)(proj, bias, dc)


def _dproj_assemble(name, parts, gates, cos_t, sin_t, n_heads, d_model):
    s = cos_t.shape[0]
    tr = _pick(s, 1024, 16)
    scale = HEAD_DIM ** -0.5
    hd = n_heads * HEAD_DIM
    cw = _pick(math.gcd(hd, d_model), PREP_COLS, HEAD_DIM)
    widths = [hd] * 6 + [d_model] * 2
    starts = [sum(widths[:t]) // cw for t in range(len(widths) + 1)]
    cos_w, sin_w = jnp.tile(cos_t, (1, cw // HEAD_DIM)), jnp.tile(sin_t, (1, cw // HEAD_DIM))

    def body(*refs):
        p_refs, cos_ref, sin_ref, o_ref = refs[:8], refs[8], refs[9], refs[10]
        j = pl.program_id(1)
        for kind in range(8):
            @pl.when(jnp.logical_and(j >= starts[kind], j < starts[kind + 1]))
            def _(kind=kind):
                t = p_refs[kind][...]
                if kind in (0, 1, 3):
                    t = t.astype(F32)
                if kind in (0, 3):
                    t = t * scale
                if kind in (0, 1):
                    t = t * cos_ref[...] - _swap_halves(t) * sin_ref[...]
                o_ref[...] = t.astype(BF)

    def part_spec(kind):
        return pl.BlockSpec((tr, cw), functools.partial(
            lambda i, j, kind: (i, jnp.clip(j - starts[kind], 0, widths[kind] // cw - 1)), kind=kind))

    tab = pl.BlockSpec((tr, cw), lambda i, j: (i, 0))
    return pl.pallas_call(
        body, grid=(s // tr, starts[-1]), in_specs=[part_spec(kind) for kind in range(8)] + [tab, tab],
        out_specs=pl.BlockSpec((tr, cw), lambda i, j: (i, j)),
        out_shape=jax.ShapeDtypeStruct((s, sum(widths)), BF),
        compiler_params=_params(("parallel", "arbitrary")), name=name)(*parts, *gates, cos_w, sin_w)


FFN2 =("ffn2_w_gate", "ffn2_w_up", "ffn2_w_down")


def _device_step(x, tgt, shards, small):
    s, d = x.shape
    hd = shards["w_proj_dil"].shape[0]
    hh = hd // HEAD_DIM
    blk = _att_block(s)
    gate_off = 6 * hd
    f_col = 0
    n_proj = gate_off + 2 * d
    ident = lambda accs, ex, rw: (accs[0],)
    chip = 2 * lax.axis_index("x") + lax.axis_index("y")
    ids = jnp.stack([chip, lax.axis_index("c")]).astype(jnp.int32)
    w = {}

    def take_gathered(names, gathered):
        for n, t in zip(names, gathered):
            t = lax.dynamic_update_index_in_dim(t, shards[n], chip, 0)
            if n == "w_in":
                w["w_in"], w["w_in_f"] = _repack_w_in(t, hd, d)
            else:
                w[n] = _full_from_pieces(n, t)

    def chip_sums(names, pieces, from_sibling):
        return [_rs_add("rs_add_" + n, ids, g, o) for n, g, o in zip(names, pieces, from_sibling)]

    def core_halves(names, sums, from_chips):
        return [_rs_sum("rs_sum_" + n, ids, own, got) for n, own, got in zip(names, sums, from_chips)]

    def gather(names):
        return _gather_side([shards[n] for n in names])

    first, under_up = ("ffn1_w_gate", "ffn1_w_up"), ("ffn1_w_down", "w_in")
    under_down = ("w_proj_dil", "w_proj_fox", "w_out", "ffn2_w_gate")
    under_proj = ("ffn2_w_up", "ffn2_w_down")
    take_gathered(first, gather(first).call("gather_first"))

    def w_down_of_ffn1(got):
        take_gathered(under_up, got["up"])
        return w["ffn1_w_down"]

    x1, saved1, got = _ffn_fwd("ffn1", x, small["ffn1_norm"], w["ffn1_w_gate"], w["ffn1_w_up"], w_down_of_ffn1,
                               sides={"up": lambda got: gather(under_up), "down": lambda got: gather(under_down)})
    take_gathered(under_down, got["down"])

    hm = _rms_fwd("mix_rms", x1, small["mix_norm"])
    cos_t, sin_t = _rope_tables(s)
    head_tile = _pick(hd, 1024, HEAD_DIM)
    tiles_per_kind = hd // head_tile
    scale = HEAD_DIM ** -0.5

    def heads_epi(accs, ex, rw):
        kind = pl.program_id(1) // tiles_per_kind
        t = accs[0]
        reps = (1, head_tile // HEAD_DIM)
        r = t * jnp.tile(ex[0], reps) + _swap_halves(t) * jnp.tile(ex[1], reps)
        t = jnp.where(kind < 2, r, t)
        return (jnp.where(jnp.logical_or(kind == 0, kind == 3), t * scale, t),)

    (qkv,), gathered = _mm("proj_heads", [(hm, w["w_in"])], "nn", s, gate_off, d, epilogue=heads_epi, out_dtypes=(BF,),
                           extras=[(cos_t, None), (sin_t, None)], tn=head_tile, keep_tn=True,
                           side=gather(under_proj[:1]))
    take_gathered(under_proj[:1], gathered)
    (gates,), gathered = _mm("proj_gates", [(hm, w["w_in"])], "nn", s, 2 * d, d, epilogue=ident, out_dtypes=(F32,),
                             b_off=gate_off, side=gather(under_proj[1:]))
    take_gathered(under_proj[1:], gathered)
    (f_logit,) = _mm("proj_f", [(hm, w["w_in_f"])], "nn", s, LANES, d, epilogue=ident, out_dtypes=(F32,))
    tab_t = _dil_bias(blk)
    dil_src = ((qkv, 0, HEAD_DIM), (qkv, hh, HEAD_DIM), (qkv, 2 * hh, HEAD_DIM))
    y_dil, lse_d = _flash_fwd("dil_fwd", *dil_src, hh, fox=False, tab_t=tab_t)
    bias_f = jnp.pad(small["b_forget"], ((0, 0), (0, LANES - hh)))
    c_pad = _forget_cumsum("forget_cumsum", f_logit, f_col, bias_f)
    q_cat, k_cat = _fox_aug("fox_aug", c_pad, qkv, 3 * hh, 4 * hh, hh)
    fox_src = ((q_cat, 0, 2 * HEAD_DIM), (k_cat, 0, 2 * HEAD_DIM), (qkv, 5 * hh, HEAD_DIM))
    y_fox, lse_f = _flash_fwd("fox_fwd", *fox_src, hh, fox=True)

    def merge_epi(accs, ex, rw):
        ud, uf = accs
        return ud, uf, _sigmoid(ex[0] + rw[0]) * ud + _sigmoid(ex[1] + rw[1]) * uf

    u_d, u_f, merged = _mm("merge", [(y_dil, w["w_proj_dil"]), (y_fox, w["w_proj_fox"])], "nn", s, d, hd,
                           epilogue=merge_epi, out_dtypes=(BF, BF, BF),
                           extras=[(gates, 0), (gates, d)],
                           rows=[small["b_gate_dil"], small["b_gate_fox"]])
    (x2,) = _mm("mix_out", [(merged, w["w_out"])], "nn", s, d, d,
                epilogue=lambda accs, ex, rw: (ex[0] + accs[0],), out_dtypes=(F32,), extras=[(x1, 0)])

    x3, saved2, _ = _ffn_fwd("ffn2", x2, small["ffn2_norm"], w["ffn2_w_gate"], w["ffn2_w_up"],
                             lambda got: w["ffn2_w_down"])
    dx3, dx3_bf, dg_final, sq = _final("final", x3, small["final_norm"].reshape(1, d), tgt)

    dx2, dx2_bf, dg_ffn2, dw_g2, dw_u2, dw_d2, _ = _ffn_bwd("ffn2", x2, small["ffn2_norm"], w["ffn2_w_gate"],
                                                            w["ffn2_w_up"], w["ffn2_w_down"], saved2, dx3, dx3_bf)
    pieces2 = [dw_g2, dw_u2, dw_d2]

    def dmerge_epi(accs, ex, rw):
        dm = accs[0]
        gd, gf, ud, uf = ex[0], ex[1], ex[2].astype(F32), ex[3].astype(F32)
        sd, sf = _sigmoid(gd + rw[0]), _sigmoid(gf + rw[1])
        dgd = dm * ud * (sd * (1.0 - sd))
        dgf = dm * uf * (sf * (1.0 - sf))
        return (dm * sd, dm * sf, dgd, dgf, jnp.sum(dgd, axis=0, keepdims=True), jnp.sum(dgf, axis=0, keepdims=True))

    (du_d, du_f, dg_d, dg_f, dbg_d, dbg_f), from_sibling = _mm(
        "dmerge", [(dx2_bf, w["w_out"])], "nt", s, d, d, epilogue=dmerge_epi, out_dtypes=(BF, BF, BF, BF), n_colsum=2,
        extras=[(gates, 0), (gates, d), (u_d, 0), (u_f, 0)],
        rows=[small["b_gate_dil"], small["b_gate_fox"]], side=_swap_side(pieces2))
    sums2 = chip_sums(FFN2, pieces2, from_sibling)
    (dw_out,) = _mm("dw_out", [(merged, dx2_bf)], "tn", d, d, s, epilogue=ident, out_dtypes=(F32,))
    dw_out = dw_out.reshape(N_CHIPS, d // N_CHIPS, d)
    dw_pd = _dw_col_pieces("dw_pd", y_dil, du_d, hd, d, s)
    dw_pf = _dw_col_pieces("dw_pf", y_fox, du_f, hd, d, s)
    (dy_dil,) = _mm("dy_dil", [(du_d, w["w_proj_dil"])], "nt", s, hd, d, epilogue=ident, out_dtypes=(BF,))
    (dy_fox,) = _mm("dy_fox", [(du_f, w["w_proj_fox"])], "nt", s, hd, d, epilogue=ident, out_dtypes=(BF,))

    row = lambda t: t.reshape(hh, s // blk, 1, blk)
    delta_d = _att_delta("dil_delta", dy_dil, y_dil, hh)
    dq_d, dk_d, dv_d = _flash_bwd("dil_bwd", *dil_src, dy_dil, lse_d, row(delta_d), hh, fox=False, tab_t=tab_t)
    delta_f = _att_delta("fox_delta", dy_fox, y_fox, hh)
    dq_f, dk_f, dv_f, dc_k, dc_q = _flash_bwd("fox_bwd", *fox_src, dy_fox, lse_f, row(delta_f), hh, fox=True)
    dc = dc_k.reshape(hh, s) + dc_q.reshape(hh, s)
    dc_pad = jnp.pad(dc.T, ((0, 0), (0, LANES - hh)))
    df, db_forget = _forget_bwd("forget_bwd", f_logit, f_col, bias_f, dc_pad)
    dproj = _dproj_assemble("dproj", [dq_d, dk_d, dv_d, dq_f, dk_f, dv_f], [dg_d, dg_f], cos_t, sin_t, hh, d)
    (dhm_f,) = _mm("dhm_f", [(df, w["w_in_f"])], "nt", s, d, LANES, epilogue=ident, out_dtypes=(F32,))
    (dhm,), from_chips = _mm("dhm", [(dproj, w["w_in"])], "nt", s, d, n_proj,
                             epilogue=lambda accs, ex, rw: (accs[0] + ex[0],), out_dtypes=(F32,),
                             extras=[(dhm_f, 0)], side=_scatter_side(sums2))
    halves2 = core_halves(FFN2, sums2, from_chips)
    small_mixer = ("w_proj_dil", "w_proj_fox", "w_out")
    pieces_sm = [dw_pd, dw_pf, dw_out]
    (dw_in,), from_sibling = _mm("dw_in", [(hm, dproj)], "tn", d, n_proj, s, epilogue=ident, out_dtypes=(F32,),
                                 side=_swap_side(pieces_sm))
    sums_sm = chip_sums(small_mixer, pieces_sm, from_sibling)
    (dw_in_f,) = _mm("dw_in_f", [(hm, df)], "tn", d, LANES, s, epilogue=ident, out_dtypes=(F32,))
    dw_in = _dw_in_pieces(dw_in, dw_in_f, hd, d)
    dx1, dx1_bf, dg_mix = _rms_bwd("mix_rmsb", x1, small["mix_norm"], dhm, dx2)

    sums, early = {}, ("w_in",) + small_mixer + FFN2

    def under_dwd(got):
        sums["w_in"] = chip_sums(["w_in"], [dw_in], got["dact"])
        return _scatter_side(sums_sm)

    def under_dwu(got):
        return _swap_side([got["dw_down"], got["dw_gate"]])

    def under_dh(got):
        sums["dg"] = chip_sums(["ffn1_w_down", "ffn1_w_gate"], [got["dw_down"], got["dw_gate"]], got["dwu"])
        halves = (core_halves(["w_in"], sums["w_in"], got["dwg"]) + core_halves(small_mixer, sums_sm, got["dwd"])
                  + halves2)
        return _scatter_side(sums["dg"]).beside(_swap_side([got["dw_up"]])).beside(_join_side(halves))

    def under_rmsb(got):
        sums["u"] = chip_sums(["ffn1_w_up"], [got["dw_up"]], got["dh"][2:3])
        return _scatter_side(sums["u"])

    dx0, _, dg_ffn1, dw_g1, dw_u1, dw_d1, got = _ffn_bwd(
        "ffn1", x, small["ffn1_norm"], w["ffn1_w_gate"], w["ffn1_w_up"], w["ffn1_w_down"], saved1, dx1, dx1_bf,
        sides={"dact": lambda got: _swap_side([dw_in]), "dwd": under_dwd,
               "dwg": lambda got: _scatter_side(sums["w_in"]), "dwu": under_dwu, "dh": under_dh, "rmsb": under_rmsb})
    halves1 = (core_halves(["ffn1_w_down", "ffn1_w_gate"], sums["dg"], got["dh"][:2])
               + core_halves(["ffn1_w_up"], sums["u"], got["rmsb"]))
    grads = dict(zip(early, got["dh"][3:]))
    grads.update(zip(("ffn1_w_down", "ffn1_w_gate", "ffn1_w_up"), _join_side(halves1).call("rs_join")))
    partials = {"ffn1_norm": dg_ffn1, "mix_norm": dg_mix, "ffn2_norm": dg_ffn2, "final_norm": dg_final,
                "b_gate_dil": dbg_d.reshape(-1, d), "b_gate_fox": dbg_f.reshape(-1, d), "b_forget": db_forget, "sq": sq}
    return dx0, grads, partials


def _coords():
    return lax.axis_index("x"), lax.axis_index("y"), lax.axis_index("c")


def _other_chips(x, y):
    return [(1 - x, y), (x, 1 - y), (1 - x, 1 - y)]


ANY_SPEC = pl.BlockSpec(memory_space=pl.ANY)


def _gather_side(shards):
    nw = len(shards)

    def copies(srcs, outs, send_sems, recv_sems):
        x, y, c = _coords()
        chips = _other_chips(x, y)

        def slot(w, px, py, pc):
            half = shards[w].shape[0] // 2
            return outs[w].at[2 * px + py, pl.ds(pc * half, half), :]

        def copy(w, k, src_ref, dst_ref, to):
            return pltpu.make_async_remote_copy(src_ref=src_ref, dst_ref=dst_ref, send_sem=send_sems.at[6 * w + k],
                                                recv_sem=recv_sems.at[6 * w + k], device_id=to, device_id_type=MESH)

        first, arrive, passed, arrive2 = [], [], [], []
        for w in range(nw):
            half = shards[w].shape[0] // 2
            for j, chip in enumerate(chips):
                first.append(copy(w, j, srcs[w].at[pl.ds(c * half, half), :], slot(w, x, y, c), (*chip, c)))
                arrive.append(copy(w, j, slot(w, *chip, c), slot(w, *chip, c), (*chip, c)))
                passed.append(copy(w, 3 + j, slot(w, *chip, c), slot(w, *chip, c), (x, y, 1 - c)))
                arrive2.append(copy(w, 3 + j, slot(w, *chip, 1 - c), slot(w, *chip, 1 - c), (x, y, 1 - c)))
        return first, arrive, passed, arrive2

    def start(srcs, outs, send_sems, recv_sems):
        for cp in copies(srcs, outs, send_sems, recv_sems)[0]:
            cp.start()

    def finish(srcs, outs, send_sems, recv_sems):
        first, arrive, passed, arrive2 = copies(srcs, outs, send_sems, recv_sems)
        for got, fwd in zip(arrive, passed):
            got.wait_recv()
            fwd.start()
        for got in arrive2:
            got.wait_recv()
        for cp in first + passed:
            cp.wait_send()

    return _Side(shards, [jax.ShapeDtypeStruct((N_CHIPS, *t.shape), t.dtype) for t in shards], 6 * nw, start, finish)


def _swap_side(grads):
    nw = len(grads)

    def copies(srcs, outs, send_sems, recv_sems):
        x, y, c = _coords()
        res = []
        for w in range(nw):
            half = grads[w].shape[1] // 2
            for p in range(N_CHIPS):
                k = N_CHIPS * w + p
                res.append(pltpu.make_async_remote_copy(
                    src_ref=srcs[w].at[p, pl.ds((1 - c) * half, half), :], dst_ref=outs[w].at[p],
                    send_sem=send_sems.at[k], recv_sem=recv_sems.at[k], device_id=(x, y, 1 - c), device_id_type=MESH))
        return res

    def start(*refs):
        for cp in copies(*refs):
            cp.start()

    def finish(*refs):
        for cp in copies(*refs):
            cp.wait()

    shapes = [jax.ShapeDtypeStruct((N_CHIPS, t.shape[1] // 2, t.shape[2]), t.dtype) for t in grads]
    return _Side(grads, shapes, N_CHIPS * nw, start, finish)


def _rs_add(name, ids, g, other):
    n, rows, cols = g.shape
    half = rows // 2
    tr = _pick(half, 256, 16)
    nb = half // tr

    def body(ids_ref, g_ref, o_ref, out_ref):
        out_ref[...] = (g_ref[...] + o_ref[...]).astype(BF)

    grid_spec = pltpu.PrefetchScalarGridSpec(
        num_scalar_prefetch=1, grid=(n, nb),
        in_specs=[pl.BlockSpec((None, tr, cols), lambda p, i, ids_ref: (p, ids_ref[1] * nb + i, 0)),
                  pl.BlockSpec((None, tr, cols), lambda p, i, ids_ref: (p, i, 0))],
        out_specs=pl.BlockSpec((None, tr, cols), lambda p, i, ids_ref: (p, i, 0)))
    return pl.pallas_call(body, grid_spec=grid_spec, out_shape=jax.ShapeDtypeStruct((n, half, cols), BF),
                          compiler_params=_params(("parallel", "parallel")), name=name)(ids, g, other)


def _scatter_side(sums):
    nw = len(sums)

    def copies(srcs, outs, send_sems, recv_sems):
        x, y, c = _coords()
        res = []
        for w in range(nw):
            for k, (px, py) in enumerate(_other_chips(x, y)):
                res.append(pltpu.make_async_remote_copy(
                    src_ref=srcs[w].at[2 * px + py], dst_ref=outs[w].at[k], send_sem=send_sems.at[3 * w + k],
                    recv_sem=recv_sems.at[3 * w + k], device_id=(px, py, c), device_id_type=MESH))
        return res

    def start(*refs):
        for cp in copies(*refs):
            cp.start()

    def finish(*refs):
        for cp in copies(*refs):
            cp.wait()

    return _Side(sums, [jax.ShapeDtypeStruct((3, *t.shape[1:]), t.dtype) for t in sums], 3 * nw, start, finish)


def _rs_sum(name, ids, own, got):
    n, half, cols = own.shape
    tr = _pick(half, 256, 16)
    nb = half // tr

    def body(ids_ref, own_ref, got_ref, out_ref):
        t = own_ref[...].astype(F32)
        for k in range(3):
            t = t + got_ref[k].astype(F32)
        out_ref[...] = t

    grid_spec = pltpu.PrefetchScalarGridSpec(
        num_scalar_prefetch=1, grid=(nb,),
        in_specs=[pl.BlockSpec((None, tr, cols), lambda i, ids_ref: (ids_ref[0], i, 0)),
                  pl.BlockSpec((3, tr, cols), lambda i, ids_ref: (0, i, 0))],
        out_specs=pl.BlockSpec((tr, cols), lambda i, ids_ref: (ids_ref[1] * nb + i, 0)))
    return pl.pallas_call(body, grid_spec=grid_spec, out_shape=jax.ShapeDtypeStruct((2 * half, cols), F32),
                          compiler_params=_params(("parallel",)), name=name)(ids, own, got)


def _join_side(totals):
    nw = len(totals)

    def start(ins, bufs, send_sems, recv_sems):
        x, y, c = _coords()
        for w in range(nw):
            half = totals[w].shape[0] // 2
            pltpu.make_async_remote_copy(
                src_ref=bufs[w].at[pl.ds(c * half, half), :], dst_ref=bufs[w].at[pl.ds(c * half, half), :],
                send_sem=send_sems.at[w], recv_sem=recv_sems.at[w], device_id=(x, y, 1 - c),
                device_id_type=MESH).start()

    def finish(ins, bufs, send_sems, recv_sems):
        x, y, c = _coords()
        for w in range(nw):
            half = totals[w].shape[0] // 2
            arrival = pltpu.make_async_remote_copy(
                src_ref=bufs[w].at[pl.ds(c * half, half), :], dst_ref=bufs[w].at[pl.ds((1 - c) * half, half), :],
                send_sem=send_sems.at[w], recv_sem=recv_sems.at[w], device_id=(x, y, 1 - c), device_id_type=MESH)
            arrival.wait_recv()
            arrival.wait_send()

    return _Side(totals, [jax.ShapeDtypeStruct(t.shape, t.dtype) for t in totals], nw, start, finish,
                 aliases={w: w for w in range(nw)})


def _gather_all(name, t):
    rows, cols = t.shape

    def body(src, out, send_sems, recv_sems, local_sem):
        x, y, c = _coords()
        me = 4 * x + 2 * y + c
        mine = pltpu.make_async_copy(src, out.at[me], local_sem)
        mine.start()
        peers = [(x ^ (k >> 2 & 1), y ^ (k >> 1 & 1), c ^ (k & 1)) for k in range(1, N_DEV)]
        sends = [pltpu.make_async_remote_copy(src_ref=src, dst_ref=out.at[me], send_sem=send_sems.at[k],
                                              recv_sem=recv_sems.at[k], device_id=peer, device_id_type=MESH)
                 for k, peer in enumerate(peers)]
        for cp in sends:
            cp.start()
        for k, (px, py, pc) in enumerate(peers):
            pltpu.make_async_remote_copy(src_ref=src, dst_ref=out.at[4 * px + 2 * py + pc], send_sem=send_sems.at[k],
                                         recv_sem=recv_sems.at[k], device_id=(px, py, pc),
                                         device_id_type=MESH).wait_recv()
        for cp in sends:
            cp.wait_send()
        mine.wait()

    vmem = pl.BlockSpec(memory_space=pltpu.VMEM)
    return pl.pallas_call(
        body, in_specs=[vmem], out_specs=vmem, out_shape=jax.ShapeDtypeStruct((N_DEV, rows, cols), t.dtype),
        scratch_shapes=[pltpu.SemaphoreType.DMA((7,)), pltpu.SemaphoreType.DMA((7,)), pltpu.SemaphoreType.DMA],
        name=name)(t)


def _adamw_math(w, g, m, v):
    m = ADAM_B1 * m + (1.0 - ADAM_B1) * g
    v = ADAM_B2 * v + (1.0 - ADAM_B2) * (g * g)
    m_hat = m / (1.0 - ADAM_B1 ** ADAM_STEP)
    v_hat = v / (1.0 - ADAM_B2 ** ADAM_STEP)
    delta = -ADAM_LR * (m_hat / (jnp.sqrt(v_hat) + ADAM_EPS) + ADAM_WD * w)
    return delta, m, v


def _adamw(name, w, g, m, v):
    _, rows, cols = w.shape
    tr = _pick(rows, 256, 8)

    def body(w_ref, g_ref, m_ref, v_ref, g_out, d_out, m_out, v_out):
        g = g_ref[...]
        g_out[...] = g
        d_out[...], m_out[...], v_out[...] = _adamw_math(w_ref[...], g, m_ref[...], v_ref[...])

    blk3 = pl.BlockSpec((None, tr, cols), lambda i: (0, i, 0))
    blk = pl.BlockSpec((tr, cols), lambda i: (i, 0))
    shape = jax.ShapeDtypeStruct((rows, cols), F32)
    return pl.pallas_call(
        body, grid=(rows // tr,), in_specs=[blk3, blk, blk3, blk3], out_specs=[blk] * 4, out_shape=[shape] * 4,
        compiler_params=_params(("parallel",)), name=name)(w, g, m, v)


def _small_reduce(name, parts, width):
    def body(*refs):
        out = refs[-1]
        out[...] = jnp.zeros_like(out)
        for k, r in enumerate(refs[:-1]):
            out[pl.ds(k, 1), :] = jnp.sum(r[...], axis=0, keepdims=True)

    vmem = pl.BlockSpec(memory_space=pltpu.VMEM)
    return pl.pallas_call(body, in_specs=[vmem] * len(parts), out_specs=vmem,
                          out_shape=jax.ShapeDtypeStruct((8, width), F32), name=name)(*parts)


def _small_adamw(name, gathered, w, m, v, loss_row, loss_scale):
    def body(gt_ref, w_ref, m_ref, v_ref, g_out, d_out, m_out, v_out, loss_out):
        g = gt_ref[0]
        for k in range(1, N_DEV):
            g = g + gt_ref[k]
        g_out[...] = g
        row = lax.broadcasted_iota(jnp.int32, g.shape, 0)
        loss_out[...] = jnp.sum(jnp.where(row == loss_row, g, 0.0), keepdims=True) * loss_scale
        d_out[...], m_out[...], v_out[...] = _adamw_math(w_ref[...], g, m_ref[...], v_ref[...])

    vmem = pl.BlockSpec(memory_space=pltpu.VMEM)
    shape = jax.ShapeDtypeStruct(w.shape, F32)
    return pl.pallas_call(body, in_specs=[vmem] * 4, out_specs=[vmem] * 5,
                          out_shape=[shape] * 4 + [jax.ShapeDtypeStruct((1, 1), F32)], name=name)(gathered, w, m, v)


def _full_from_pieces(name, pieces):
    _, rows, cols = pieces.shape
    if name in ROW_SHARDED:
        return pieces.reshape(N_CHIPS * rows, cols)
    return pieces.transpose(1, 0, 2).reshape(rows, N_CHIPS * cols)


def _column_range(segments, lo, hi):
    out, start = [], 0
    for t in segments:
        a, b = max(lo, start), min(hi, start + t.shape[1])
        if a < b:
            out.append(t[:, a - start:b - start])
        start += t.shape[1]
    return out


def _repack_w_in(pieces, hd, d):
    hh = hd // HEAD_DIM
    segs = [pieces[p] for p in range(N_CHIPS)]
    total = N_CHIPS * pieces.shape[2]
    main = jnp.concatenate(_column_range(segs, 0, 6 * hd) + _column_range(segs, 6 * hd + hh, total), axis=1)
    f = jnp.concatenate(_column_range(segs, 6 * hd, 6 * hd + hh), axis=1)
    return main, jnp.pad(f, ((0, 0), (0, LANES - hh)))


def _dw_in_pieces(dw, dw_f, hd, d):
    hh = hd // HEAD_DIM
    segs = [dw[:, :6 * hd], dw_f[:, :hh], dw[:, 6 * hd:]]
    cs = (6 * hd + hh + 2 * d) // N_CHIPS
    return jnp.stack([jnp.concatenate(_column_range(segs, p * cs, (p + 1) * cs), axis=1) for p in range(N_CHIPS)])


def _small_pack(vals, width):
    rows = []
    for name in SMALL:
        t = vals[name].reshape(1, -1)
        rows.append(jnp.pad(t, ((0, 0), (0, width - t.shape[1]))))
    rows.append(jnp.zeros((8 - len(SMALL), width), F32))
    return jnp.concatenate(rows, axis=0)


def kernel(x, ffn1_norm, ffn1_w_gate, ffn1_w_up, ffn1_w_down, mix_norm, w_in, b_forget, b_gate_dil, b_gate_fox, w_proj_dil, w_proj_fox, w_out, ffn2_norm, ffn2_w_gate, ffn2_w_up, ffn2_w_down, final_norm, loss_target, m_ffn1_norm, m_ffn1_w_gate, m_ffn1_w_up, m_ffn1_w_down, m_mix_norm, m_w_in, m_b_forget, m_b_gate_dil, m_b_gate_fox, m_w_proj_dil, m_w_proj_fox, m_w_out, m_ffn2_norm, m_ffn2_w_gate, m_ffn2_w_up, m_ffn2_w_down, m_final_norm, v_ffn1_norm, v_ffn1_w_gate, v_ffn1_w_up, v_ffn1_w_down, v_mix_norm, v_w_in, v_b_forget, v_b_gate_dil, v_b_gate_fox, v_w_proj_dil, v_w_proj_fox, v_w_out, v_ffn2_norm, v_ffn2_w_gate, v_ffn2_w_up, v_ffn2_w_down, v_final_norm):
    given = dict(locals())
    wts = {n: given[n] for n in WEIGHTS}
    mom_m = {n: given["m_" + n] for n in WEIGHTS}
    mom_v = {n: given["v_" + n] for n in WEIGHTS}
    d = x.shape[2]

    shards = {n: wts[n][0].astype(BF) for n in SHARDED}
    small = {n: wts[n] for n in SMALL}
    grad_x, grads, partials = _device_step(x[0], loss_target[0], shards, small)

    out_g, out_d, out_m, out_v = {}, {}, {}, {}
    for n in SHARDED:
        outs = _adamw("adamw_" + n, wts[n], grads[n], mom_m[n], mom_v[n])
        out_g[n], out_d[n], out_m[n], out_v[n] = (t[None] for t in outs)

    width = d
    part_rows = []
    for n in SMALL:
        t = partials[n]
        part_rows.append(jnp.pad(t, ((0, 0), (0, width - t.shape[1]))))
    part_rows.append(partials["sq"])
    local_small = _small_reduce("small_reduce", part_rows, width)
    gathered_small = _gather_all("small_gather", local_small)
    sg, sd_, sm, sv, loss = _small_adamw("small_adamw", gathered_small, _small_pack(wts, width),
                                         _small_pack(mom_m, width), _small_pack(mom_v, width), len(SMALL), 0.5 / d)
    for k, n in enumerate(SMALL):
        shp = wts[n].shape
        take = lambda t: t[k, :shp[-1]].reshape(shp)
        out_g[n], out_d[n], out_m[n], out_v[n] = take(sg), take(sd_), take(sm), take(sv)
    return (loss[0, 0], grad_x[None], *[out_g[n] for n in WEIGHTS], *[out_d[n] for n in WEIGHTS],
            *[out_m[n] for n in WEIGHTS], *[out_v[n] for n in WEIGHTS])
```

```python
import functools
import math

import jax
import jax.numpy as jnp
from jax import lax
from jax.experimental import pallas as pl
from jax.experimental.pallas import tpu as pltpu

HEAD_DIM = 128
ROPE_DIM = HEAD_DIM // 4
ROPE_THETA = 500000.0
DIL_PATTERNS = ((128, 1), (512, 4), (2048, 16))
MAX_WINDOW = 2048
NORM_EPS = 1e-6
ADAM_LR = 0.001
ADAM_B1 = 0.9
ADAM_B2 = 0.999
ADAM_EPS = 1e-08
ADAM_WD = 0.01
ADAM_STEP = 10

BF = jnp.bfloat16
F32 = jnp.float32
NEG = -1e30
LANES = 128
ATT_BLOCK = 512
PREP_COLS = 512
RELAY_AT_PERCENT = 70
ATT_HEADS = 2
ATT_HEADS_FWD = 4
VMEM_LIMIT = 56 * 1024 * 1024
MM_VMEM_BUDGET = 46 * 1024 * 1024
N_CHIPS = 4
N_DEV = 8
MESH = pl.DeviceIdType.MESH

SHARDED = ("ffn1_w_gate", "ffn1_w_up", "ffn1_w_down", "w_in", "w_proj_dil", "w_proj_fox", "w_out",
           "ffn2_w_gate", "ffn2_w_up", "ffn2_w_down")
ROW_SHARDED = ("ffn1_w_down", "w_out", "ffn2_w_down")
SMALL = ("ffn1_norm", "mix_norm", "b_forget", "b_gate_dil", "b_gate_fox", "ffn2_norm", "final_norm")
WEIGHTS = ("ffn1_norm", "ffn1_w_gate", "ffn1_w_up", "ffn1_w_down", "mix_norm", "w_in", "b_forget",
           "b_gate_dil", "b_gate_fox", "w_proj_dil", "w_proj_fox", "w_out", "ffn2_norm", "ffn2_w_gate",
           "ffn2_w_up", "ffn2_w_down", "final_norm")


def _pick(n, target, align):
    best = None
    for d in range(align, min(n, target) + 1, align):
        if n % d == 0:
            best = d
    return n if best is None else best


def _params(sem=None):
    return pltpu.CompilerParams(dimension_semantics=sem, vmem_limit_bytes=VMEM_LIMIT)


_DIMS = {"nn": (((1,), (0,)), ((), ())), "nt": (((1,), (1,)), ((), ())), "tn": (((0,), (0,)), ((), ()))}


class _SemsFrom:
    def __init__(self, sems, first):
        self.sems, self.first = sems, first

    @property
    def at(self):
        return self

    def __getitem__(self, k):
        return self.sems.at[self.first + k]


class _Side:
    def __init__(self, inputs, out_shapes, n_sems, start, finish, aliases=None, relay=None):
        self.inputs, self.out_shapes, self.n_sems = list(inputs), list(out_shapes), n_sems
        self.start, self.finish, self.aliases = start, finish, aliases or {}
        self.relay = relay or (lambda *refs: None)

    def scratch(self):
        return [pltpu.SemaphoreType.DMA((self.n_sems,)), pltpu.SemaphoreType.DMA((self.n_sems,))]

    def beside(self, other):
        n_in, n_out, n_sems = len(self.inputs), len(self.out_shapes), self.n_sems

        def part(fn_a, fn_b):
            def run(ins, outs, send_sems, recv_sems):
                fn_a(ins[:n_in], outs[:n_out], send_sems, recv_sems)
                fn_b(ins[n_in:], outs[n_out:], _SemsFrom(send_sems, n_sems), _SemsFrom(recv_sems, n_sems))
            return run

        aliases = {**self.aliases, **{n_in + t: n_out + o for t, o in other.aliases.items()}}
        return _Side(self.inputs + other.inputs, self.out_shapes + other.out_shapes, n_sems + other.n_sems,
                     part(self.start, other.start), part(self.finish, other.finish), aliases=aliases,
                     relay=part(self.relay, other.relay))

    def call(self, name):
        n_in, n_out = len(self.inputs), len(self.out_shapes)

        def body(*refs):
            ins, outs, sems = refs[:n_in], refs[n_in:n_in + n_out], refs[n_in + n_out:]
            self.start(ins, outs, *sems)
            self.relay(ins, outs, *sems)
            self.finish(ins, outs, *sems)

        return pl.pallas_call(body, in_specs=[ANY_SPEC] * n_in, out_specs=[ANY_SPEC] * n_out, out_shape=self.out_shapes,
                              input_output_aliases=self.aliases, scratch_shapes=self.scratch(), name=name)(*self.inputs)


def _mm(name, pairs, mode, m, n, k, *, epilogue, out_dtypes, extras=(), rows=(), n_colsum=0,
        sum_pairs=False, tm=1024, tn=1152, tk=2048, piece_layout=False, side=None, b_off=0, keep_tn=False):
    m_align = LANES if mode == "tn" else 8
    tm = _pick(m, tm, m_align)
    tn = n // N_CHIPS if piece_layout else _pick(n, tn, LANES)
    tk = _pick(k, tk, LANES)
    n_acc = 1 if sum_pairs else len(pairs)
    lhs = []
    for a, _ in pairs:
        if not any(a is t for t in lhs):
            lhs.append(a)
    lhs_of = [next(t for t in range(len(lhs)) if lhs[t] is a) for a, _ in pairs]
    n_mm = len(lhs) + len(pairs)
    n_in = n_mm + len(extras) + len(rows)
    n_out = len(out_dtypes) + n_colsum

    def vmem_bytes(tm_, tn_, tk_):
        tiles = sum(tm_ * tk_ * a.dtype.itemsize for a in lhs) + sum(tn_ * tk_ * b.dtype.itemsize for _, b in pairs)
        tiles += sum(tm_ * (arr.shape[1] if off is None else tn_) * arr.dtype.itemsize for arr, off in extras)
        tiles += sum(tm_ * tn_ * jnp.dtype(dt).itemsize for dt in out_dtypes)
        return 2 * tiles + (n_acc + len(extras) + len(out_dtypes)) * tm_ * tn_ * 4

    while vmem_bytes(tm, tn, tk) > MM_VMEM_BUDGET:
        if tn > 512 and not piece_layout and not keep_tn:
            tn = _pick(n, tn - LANES, LANES)
        elif tm > 512:
            tm = _pick(m, tm - m_align, m_align)
        elif tk > 512:
            tk = _pick(k, tk - LANES, LANES)
        elif tm > 256:
            tm = _pick(m, tm - m_align, m_align)
        else:
            break
    nk = k // tk

    n_side_in = len(side.inputs) if side else 0
    n_side_out = len(side.out_shapes) if side else 0
    grid = (m // tm, n // tn, nk)
    relay_step = (RELAY_AT_PERCENT * grid[0] * grid[1] * nk) // 100
    relay_at = (relay_step // (grid[1] * nk), relay_step // nk % grid[1], relay_step % nk)

    def body(*refs):
        ins, refs = refs[:n_in], refs[n_in:]
        side_ins, refs = refs[:n_side_in], refs[n_side_in:]
        outs, refs = refs[:n_out], refs[n_out:]
        side_outs, refs = refs[:n_side_out], refs[n_side_out:]
        accs, side_sems = refs[:n_acc], refs[n_acc:]
        kk = pl.program_id(2)
        if side:
            at = [pl.program_id(t) for t in range(3)]

            @pl.when(jnp.logical_and(jnp.logical_and(at[0] == 0, at[1] == 0), at[2] == 0))
            def _():
                side.start(side_ins, side_outs, *side_sems)

        @pl.when(kk == 0)
        def _():
            for acc in accs:
                acc[...] = jnp.zeros_like(acc)

        a_tiles = [r[...].astype(BF) for r in ins[:len(lhs)]]
        for p in range(len(pairs)):
            b = ins[len(lhs) + p][...].astype(BF)
            accs[0 if sum_pairs else p][...] += lax.dot_general(a_tiles[lhs_of[p]], b, _DIMS[mode],
                                                                preferred_element_type=F32)

        @pl.when(kk == nk - 1)
        def _():
            ex = [r[...] for r in ins[n_mm:n_mm + len(extras)]]
            rw = [r[...] for r in ins[n_mm + len(extras):]]
            res = epilogue([acc[...] for acc in accs], ex, rw)
            for o, r in zip(outs, res):
                o[...] = r.astype(o.dtype)

        if side:
            @pl.when(jnp.logical_and(jnp.logical_and(at[0] == relay_at[0], at[1] == relay_at[1]), at[2] == relay_at[2]))
            def _():
                side.relay(side_ins, side_outs, *side_sems)

            @pl.when(jnp.logical_and(jnp.logical_and(at[0] == grid[0] - 1, at[1] == grid[1] - 1), at[2] == nk - 1))
            def _():
                side.finish(side_ins, side_outs, *side_sems)

    in_specs, args = [], []
    for a in lhs:
        if mode == "tn":
            in_specs.append(pl.BlockSpec((tk, tm), lambda i, j, kk: (kk, i)))
        else:
            in_specs.append(pl.BlockSpec((tm, tk), lambda i, j, kk: (i, kk)))
        args.append(a)
    assert b_off % tn == 0 and (b_off == 0 or mode == "nn")
    for _, b in pairs:
        if mode == "nt":
            in_specs.append(pl.BlockSpec((tn, tk), lambda i, j, kk: (j, kk)))
        else:
            in_specs.append(pl.BlockSpec((tk, tn), functools.partial(lambda i, j, kk, o: (kk, j + o), o=b_off // tn)))
        args.append(b)
    for arr, off in extras:
        if off is None:
            in_specs.append(pl.BlockSpec((tm, arr.shape[1]), lambda i, j, kk: (i, 0)))
        else:
            assert off % tn == 0
            in_specs.append(pl.BlockSpec((tm, tn), functools.partial(lambda i, j, kk, o: (i, j + o), o=off // tn)))
        args.append(arr)
    for arr in rows:
        in_specs.append(pl.BlockSpec((1, tn), lambda i, j, kk: (0, j)))
        args.append(arr)
    if piece_layout:
        out_specs = [pl.BlockSpec((None, tm, tn), lambda i, j, kk: (j, i, 0)) for _ in out_dtypes]
        out_shape = [jax.ShapeDtypeStruct((n // tn, m, tn), d) for d in out_dtypes]
    else:
        out_specs = [pl.BlockSpec((tm, tn), lambda i, j, kk: (i, j)) for _ in out_dtypes]
        out_shape = [jax.ShapeDtypeStruct((m, n), d) for d in out_dtypes]
    for _ in range(n_colsum):
        out_specs.append(pl.BlockSpec((None, 1, tn), lambda i, j, kk: (i, 0, j)))
        out_shape.append(jax.ShapeDtypeStruct((m // tm, 1, n), F32))
    scratch = [pltpu.VMEM((tm, tn), F32) for _ in range(n_acc)]
    if side is None:
        return pl.pallas_call(
            body, grid=grid, in_specs=in_specs, out_specs=out_specs, out_shape=out_shape, scratch_shapes=scratch,
            compiler_params=_params(("parallel", "parallel", "arbitrary")), name=name)(*args)
    res = pl.pallas_call(
        body, grid=grid, in_specs=in_specs + [ANY_SPEC] * n_side_in, out_specs=out_specs + [ANY_SPEC] * n_side_out,
        out_shape=out_shape + side.out_shapes, scratch_shapes=scratch + side.scratch(),
        input_output_aliases={n_in + t: n_out + o for t, o in side.aliases.items()},
        compiler_params=_params(("arbitrary", "arbitrary", "arbitrary")), name=name)(*args, *side.inputs)
    return res[:n_out], res[n_out:]


def _col_pieces(full):
    rows, cols = full.shape
    return full.reshape(rows, N_CHIPS, cols // N_CHIPS).transpose(1, 0, 2)


def _dw_col_pieces(name, a, b, m, n, k, side=None):
    ident = lambda accs, ex, rw: (accs[0],)
    aligned = (n // N_CHIPS) % LANES == 0
    res = _mm(name, [(a, b)], "tn", m, n, k, epilogue=ident, out_dtypes=(F32,), tm=512 if aligned else 1024,
              piece_layout=aligned, side=side)
    (out,), side_res = res if side else (res, None)
    out = out if aligned else _col_pieces(out)
    return (out, side_res) if side else out


def _hosted(sides, got, key, call):
    make = sides.get(key) if sides else None
    if make is None:
        return call(None)
    outs, got[key] = call(make(got))
    return outs


def _sigmoid(z):
    return 0.5 * jnp.tanh(0.5 * z) + 0.5


def _row_tile(s):
    return _pick(s, 256, 8)


def _fold8(t):
    r, d = t.shape
    return jnp.sum(t.reshape(r // 8, 8, d), axis=0)


def _rms_fwd(name, x, g):
    s, d = x.shape
    tr = _row_tile(s)

    def body(x_ref, g_ref, h_ref):
        xf = x_ref[...]
        y = xf * lax.rsqrt(jnp.mean(xf * xf, axis=-1, keepdims=True) + NORM_EPS)
        h_ref[...] = (y * g_ref[...]).astype(BF)

    return pl.pallas_call(
        body, grid=(s // tr,),
        in_specs=[pl.BlockSpec((tr, d), lambda i: (i, 0)), pl.BlockSpec((1, d), lambda i: (0, 0))],
        out_specs=pl.BlockSpec((tr, d), lambda i: (i, 0)), out_shape=jax.ShapeDtypeStruct((s, d), BF),
        compiler_params=_params(("parallel",)), name=name)(x, g)


def _rms_bwd(name, x, g, dh, dres, side=None):
    s, d = x.shape
    tr = _row_tile(s)
    steps = s // tr
    n_side_in = len(side.inputs) if side else 0
    n_side_out = len(side.out_shapes) if side else 0

    def body(*refs):
        x_ref, g_ref, dh_ref, dres_ref = refs[:4]
        side_ins, refs = refs[4:4 + n_side_in], refs[4 + n_side_in:]
        dx_ref, dxb_ref, dg_ref = refs[:3]
        side_outs, side_sems = refs[3:3 + n_side_out], refs[3 + n_side_out:]

        @pl.when(pl.program_id(0) == 0)
        def _():
            dg_ref[...] = jnp.zeros_like(dg_ref)
            if side:
                side.start(side_ins, side_outs, *side_sems)

        xf = x_ref[...]
        rstd = lax.rsqrt(jnp.mean(xf * xf, axis=-1, keepdims=True) + NORM_EPS)
        xhat = xf * rstd
        dhf = dh_ref[...]
        dg_ref[...] += _fold8(dhf * xhat)
        dxh = dhf * g_ref[...]
        dx = dres_ref[...] + rstd * (dxh - xhat * jnp.mean(dxh * xhat, axis=-1, keepdims=True))
        dx_ref[...] = dx
        dxb_ref[...] = dx.astype(BF)

        if side:
            @pl.when(pl.program_id(0) == steps - 1)
            def _():
                side.relay(side_ins, side_outs, *side_sems)
                side.finish(side_ins, side_outs, *side_sems)

    blk = pl.BlockSpec((tr, d), lambda i: (i, 0))
    res = pl.pallas_call(
        body, grid=(steps,),
        in_specs=[blk, pl.BlockSpec((1, d), lambda i: (0, 0)), blk, blk] + [ANY_SPEC] * n_side_in,
        out_specs=[blk, blk, pl.BlockSpec((8, d), lambda i: (0, 0))] + [ANY_SPEC] * n_side_out,
        out_shape=[jax.ShapeDtypeStruct((s, d), F32), jax.ShapeDtypeStruct((s, d), BF),
                   jax.ShapeDtypeStruct((8, d), F32)] + (side.out_shapes if side else []),
        scratch_shapes=side.scratch() if side else [],
        compiler_params=_params(("arbitrary",)), name=name)(x, g, dh, dres, *(side.inputs if side else []))
    return (res[:3], res[3:]) if side else res


def _final(name, x, g, tgt):
    s, d = x.shape
    tr = _row_tile(s)

    def body(x_ref, g_ref, t_ref, dx_ref, dxb_ref, dg_ref, sq_ref):
        @pl.when(pl.program_id(0) == 0)
        def _():
            dg_ref[...] = jnp.zeros_like(dg_ref)
            sq_ref[...] = jnp.zeros_like(sq_ref)

        xf = x_ref[...]
        rstd = lax.rsqrt(jnp.mean(xf * xf, axis=-1, keepdims=True) + NORM_EPS)
        xhat = xf * rstd
        gf = g_ref[...]
        err = xhat * gf - t_ref[...]
        sq_ref[...] += _fold8(err * err)
        dy = err * (1.0 / d)
        dg_ref[...] += _fold8(dy * xhat)
        dxh = dy * gf
        dx = rstd * (dxh - xhat * jnp.mean(dxh * xhat, axis=-1, keepdims=True))
        dx_ref[...] = dx
        dxb_ref[...] = dx.astype(BF)

    blk = pl.BlockSpec((tr, d), lambda i: (i, 0))
    acc = pl.BlockSpec((8, d), lambda i: (0, 0))
    return pl.pallas_call(
        body, grid=(s // tr,), in_specs=[blk, pl.BlockSpec((1, d), lambda i: (0, 0)), blk],
        out_specs=[blk, blk, acc, acc],
        out_shape=[jax.ShapeDtypeStruct((s, d), F32), jax.ShapeDtypeStruct((s, d), BF),
                   jax.ShapeDtypeStruct((8, d), F32), jax.ShapeDtypeStruct((8, d), F32)],
        compiler_params=_params(("arbitrary",)), name=name)(x, g, tgt)


def _ffn_fwd(tag, x, g, w_gate, w_up, get_w_down, sides=None):
    s, d = x.shape
    f = w_gate.shape[1]
    got = {}
    h = _rms_fwd(tag + "_rms", x, g)

    def up_epi(accs, ex, rw):
        a, b = accs
        return a, b, a * _sigmoid(a) * b

    a, b, act = _hosted(sides, got, "up", lambda side: _mm(
        tag + "_up", [(h, w_gate), (h, w_up)], "nn", s, f, d, epilogue=up_epi, out_dtypes=(BF, BF, BF), side=side))

    def down_epi(accs, ex, rw):
        return (ex[0] + 0.5 * accs[0],)

    w_down = get_w_down(got)
    (y,) = _hosted(sides, got, "down", lambda side: _mm(
        tag + "_down", [(act, w_down)], "nn", s, d, f, epilogue=down_epi, out_dtypes=(F32,), extras=[(x, 0)],
        side=side))
    return y, (h, a, b, act), got


def _ffn_bwd(tag, x, g, w_gate, w_up, w_down, saved, dy, dy_bf, sides=None):
    s, d = x.shape
    f = w_gate.shape[1]
    h, a, b, act = saved
    got = {}

    def act_epi(accs, ex, rw):
        dact = 0.5 * accs[0]
        av, bv = ex[0].astype(F32), ex[1].astype(F32)
        sg = _sigmoid(av)
        return dact * bv * (sg * (1.0 + av * (1.0 - sg))), dact * (av * sg)

    da, db = _hosted(sides, got, "dact", lambda side: _mm(
        tag + "_dact", [(dy_bf, w_down)], "nt", s, f, d, epilogue=act_epi, out_dtypes=(BF, BF),
        extras=[(a, 0), (b, 0)], side=side))
    ident = lambda accs, ex, rw: (accs[0],)
    (dw_down,) = _hosted(sides, got, "dwd", lambda side: _mm(
        tag + "_dwd", [(act, dy_bf)], "tn", f, d, s, epilogue=lambda accs, ex, rw: (0.5 * accs[0],),
        out_dtypes=(F32,), side=side))
    got["dw_down"] = dw_down = dw_down.reshape(N_CHIPS, f // N_CHIPS, d)
    got["dw_gate"] = dw_gate = _hosted(sides, got, "dwg", lambda side: _dw_col_pieces(
        tag + "_dwg", h, da, d, f, s, side=side))
    got["dw_up"] = dw_up = _hosted(sides, got, "dwu", lambda side: _dw_col_pieces(
        tag + "_dwu", h, db, d, f, s, side=side))
    (dh,) = _hosted(sides, got, "dh", lambda side: _mm(
        tag + "_dh", [(da, w_gate), (db, w_up)], "nt", s, d, f, epilogue=ident, out_dtypes=(F32,), sum_pairs=True,
        side=side))
    dx, dx_bf, dg = _hosted(sides, got, "rmsb", lambda side: _rms_bwd(tag + "_rmsb", x, g, dh, dy, side=side))
    return dx, dx_bf, dg, dw_gate, dw_up, dw_down, got


def _rope_tables(s):
    half = ROPE_DIM // 2
    pos = jnp.arange(s, dtype=F32)
    inv_freq = ROPE_THETA ** (-jnp.arange(0, ROPE_DIM, 2, dtype=F32) / ROPE_DIM)
    ang = pos[:, None] * inv_freq[None, :]
    cos, sin = jnp.cos(ang), jnp.sin(ang)
    rest = HEAD_DIM - ROPE_DIM
    cos_t = jnp.concatenate([cos, cos, jnp.ones((s, rest), F32)], axis=-1)
    sin_t = jnp.concatenate([-sin, sin, jnp.zeros((s, rest), F32)], axis=-1)
    return cos_t, sin_t


def _swap_halves(t):
    lane = lax.broadcasted_iota(jnp.int32, t.shape, 1) & (HEAD_DIM - 1)
    half = ROPE_DIM // 2
    return jnp.where(lane < half, pltpu.roll(t, t.shape[1] - half, 1), pltpu.roll(t, half, 1))


def _dil_bias(blk):
    n_delta = MAX_WINDOW // blk + 1
    delta = jnp.arange(n_delta, dtype=jnp.int32)[:, None, None]
    r = jnp.arange(blk, dtype=jnp.int32)[None, None, :]
    c = jnp.arange(blk, dtype=jnp.int32)[None, :, None]
    o = delta * blk + r - c
    mult = jnp.zeros(o.shape, F32)
    for w, dd in DIL_PATTERNS:
        mult = mult + ((o >= 0) & (o <= w) & (o % dd == 0)).astype(F32)
    return jnp.where(mult > 0, jnp.log(jnp.maximum(mult, 1.0)), NEG)


def _att_block(s):
    return _pick(s, ATT_BLOCK, LANES)


def _fox_aug(name, c_pad, qkv, q_off, k_off, n_heads):
    s = c_pad.shape[0]
    tr = _pick(s, 1024, 16)

    def body(c_ref, q_ref, k_ref, qc_ref, kc_ref):
        h = pl.program_id(1)
        lane = lax.broadcasted_iota(jnp.int32, (tr, LANES), 1)
        ch = jnp.sum(jnp.where(lane == h, c_ref[...], 0.0), axis=1, keepdims=True)
        hi, mid, lo = (t.astype(F32) for t in _split3(ch))
        zero = jnp.zeros((tr, LANES), F32)
        is_hi = jnp.logical_or(lane == 0, lane == 3)
        is_mid = jnp.logical_or(lane == 1, lane == 4)
        parts = jnp.where(is_hi, hi, jnp.where(is_mid, mid, lo))
        qc_ref[:, :HEAD_DIM] = q_ref[...]
        kc_ref[:, :HEAD_DIM] = k_ref[...]
        qc_ref[:, HEAD_DIM:] = jnp.where(lane < 3, 1.0, jnp.where(lane < 6, parts, zero)).astype(BF)
        kc_ref[:, HEAD_DIM:] = jnp.where(lane < 3, -parts, jnp.where(lane < 6, 1.0, zero)).astype(BF)

    head = lambda o: pl.BlockSpec((tr, HEAD_DIM), functools.partial(lambda i, h, o: (i, o + h), o=o))
    spec = pl.BlockSpec((tr, 2 * HEAD_DIM), lambda i, h: (i, h))
    shape = jax.ShapeDtypeStruct((s, n_heads * 2 * HEAD_DIM), BF)
    return pl.pallas_call(
        body, grid=(s // tr, n_heads),
        in_specs=[pl.BlockSpec((tr, LANES), lambda i, h: (i, 0)), head(q_off), head(k_off)],
        out_specs=[spec, spec], out_shape=[shape, shape],
        compiler_params=_params(("parallel", "arbitrary")), name=name)(c_pad, qkv, qkv)


def _causal_mask(st):
    kpos = lax.broadcasted_iota(jnp.int32, st.shape, 0)
    qpos = lax.broadcasted_iota(jnp.int32, st.shape, 1)
    return jnp.where(kpos <= qpos, st, NEG)


def _flash_fwd(name, q_src, k_src, v_src, n_heads, *, fox, tab_t=None):
    (q_arr, q_off, qw), (k_arr, k_off, kw), (v_arr, v_off, _) = q_src, k_src, v_src
    s = q_arr.shape[0]
    blk = _att_block(s)
    nq = s // blk
    n_delta = MAX_WINDOW // blk + 1
    grp = ATT_HEADS_FWD
    assert qw == kw and n_heads % grp == 0 and q_off % grp == 0 and k_off % grp == 0 and v_off % grp == 0
    wide = grp * HEAD_DIM

    def body(*refs):
        if fox:
            q_ref, k_ref, v_ref, o_ref, lse_ref, acc, m_s, l_s = refs
        else:
            q_ref, k_ref, v_ref, tab_ref, o_ref, lse_ref, acc, m_s, l_s = refs
        i = pl.program_id(1)
        acc[...] = jnp.zeros_like(acc)
        m_s[...] = jnp.full_like(m_s, NEG)
        l_s[...] = jnp.zeros_like(l_s)

        bk = 2 * blk if fox and s % (2 * blk) == 0 else blk
        per_key_block = bk // blk

        def step(j, diagonal):
            ks = pl.ds(pl.multiple_of(j * bk, bk), bk)
            for g in range(grp):
                cols = slice(g * HEAD_DIM, (g + 1) * HEAD_DIM)
                qk_cols = slice(g * qw, (g + 1) * qw)
                st = lax.dot_general(k_ref[ks, qk_cols], q_ref[:, qk_cols], _DIMS["nt"], preferred_element_type=F32)
                if fox:
                    if diagonal:
                        kpos = lax.broadcasted_iota(jnp.int32, st.shape, 0) + j * bk
                        qpos = lax.broadcasted_iota(jnp.int32, st.shape, 1) + i * blk
                        st = jnp.where(kpos <= qpos, st, NEG)
                else:
                    st = st + tab_ref[i - j]
                m_prev = m_s[g]
                m_new = jnp.maximum(m_prev, jnp.max(st, axis=0, keepdims=True))
                alpha = jnp.exp(m_prev - m_new)
                p = jnp.exp(st - m_new)
                l_s[g] = alpha * l_s[g] + jnp.sum(p, axis=0, keepdims=True)
                acc[g] = alpha * acc[g] + lax.dot_general(v_ref[ks, cols], p.astype(BF), _DIMS["tn"],
                                                          preferred_element_type=F32)
                m_s[g] = m_new

        def loop_step(j, carry):
            step(j, False)
            return carry

        if fox:
            lax.fori_loop(0, i // per_key_block, loop_step, 0)
            step(i // per_key_block, True)
        else:
            lax.fori_loop(jnp.maximum(i - (n_delta - 1), 0), i + 1, loop_step, 0)
        for g in range(grp):
            o_ref[:, g * HEAD_DIM:(g + 1) * HEAD_DIM] = (acc[g] / l_s[g]).T.astype(o_ref.dtype)
            lse_ref[g] = m_s[g] + jnp.log(l_s[g])

    off = lambda o: functools.partial(lambda h, i, o: (0, o + h), o=o // grp)
    in_specs = [pl.BlockSpec((blk, grp * qw), functools.partial(lambda h, i, o: (i, o + h), o=q_off // grp)),
                pl.BlockSpec((s, grp * kw), off(k_off), pipeline_mode=pl.Buffered(1)),
                pl.BlockSpec((s, wide), off(v_off), pipeline_mode=pl.Buffered(1))]
    args = [q_arr, k_arr, v_arr]
    if not fox:
        in_specs.append(pl.BlockSpec((n_delta, blk, blk), lambda h, i: (0, 0, 0)))
        args.append(tab_t)
    return pl.pallas_call(
        body, grid=(n_heads // grp, nq), in_specs=in_specs,
        out_specs=[pl.BlockSpec((blk, wide), lambda h, i: (i, h)),
                   pl.BlockSpec((grp, None, 1, blk), lambda h, i: (h, i, 0, 0))],
        out_shape=[jax.ShapeDtypeStruct((s, n_heads * HEAD_DIM), BF),
                   jax.ShapeDtypeStruct((n_heads, nq, 1, blk), F32)],
        scratch_shapes=[pltpu.VMEM((grp, HEAD_DIM, blk), F32), pltpu.VMEM((grp, 1, blk), F32),
                        pltpu.VMEM((grp, 1, blk), F32)],
        compiler_params=_params(("parallel", "parallel")), name=name)(*args)


def _att_delta(name, do, o, n_heads):
    s = do.shape[0]
    blk = _pick(s, 1024, 16)

    def body(do_ref, o_ref, d_ref):
        d_ref[...] = jnp.sum(do_ref[...].astype(F32) * o_ref[...].astype(F32), axis=-1, keepdims=True)

    spec = pl.BlockSpec((blk, HEAD_DIM), lambda h, i: (i, h))
    return pl.pallas_call(
        body, grid=(n_heads, s // blk), in_specs=[spec, spec],
        out_specs=pl.BlockSpec((None, blk, 1), lambda h, i: (h, i, 0)),
        out_shape=jax.ShapeDtypeStruct((n_heads, s, 1), F32),
        compiler_params=_params(("parallel", "parallel")), name=name)(do, o)


def _flash_bwd(name, q_src, k_src, v_src, do, lse_row, delta_row, n_heads, *, fox, tab_t=None):
    (q_arr, q_off, qw), (k_arr, k_off, kw), (v_arr, v_off, _) = q_src, k_src, v_src
    s = q_arr.shape[0]
    blk = _att_block(s)
    nq = s // blk
    n_delta = MAX_WINDOW // blk + 1
    grp = ATT_HEADS
    assert qw == kw and n_heads % grp == 0 and q_off % grp == 0 and k_off % grp == 0 and v_off % grp == 0
    wide = grp * HEAD_DIM

    def body(*refs):
        if fox:
            (q_ref, do_ref, k_ref, v_ref, lse_ref, dl_ref,
             dq_ref, dk_ref, dv_ref, dc_ref, dcq_ref, dk_acc, dv_acc, dc_acc) = refs
        else:
            (q_ref, do_ref, k_ref, v_ref, lse_ref, dl_ref, tab_ref,
             dq_ref, dk_ref, dv_ref, dk_acc, dv_acc) = refs
        j = pl.program_id(1)

        @pl.when(j == 0)
        def _():
            dq_ref[...] = jnp.zeros_like(dq_ref)
            if fox:
                dcq_ref[...] = jnp.zeros_like(dcq_ref)

        dk_acc[...] = jnp.zeros_like(dk_acc)
        dv_acc[...] = jnp.zeros_like(dv_acc)
        if fox:
            dc_acc[...] = jnp.zeros_like(dc_acc)

        def step(i, diagonal):
            qs = pl.ds(pl.multiple_of(i * blk, blk), blk)
            for g in range(grp):
                cols = slice(g * HEAD_DIM, (g + 1) * HEAD_DIM)
                qk_cols = slice(g * qw, (g + 1) * qw)
                plain = slice(g * qw, g * qw + HEAD_DIM)
                kb, vb = k_ref[:, plain], v_ref[:, cols]
                qb, dob = q_ref[qs, plain], do_ref[qs, cols]
                st = lax.dot_general(k_ref[:, qk_cols], q_ref[qs, qk_cols], _DIMS["nt"], preferred_element_type=F32)
                if fox:
                    if diagonal:
                        st = _causal_mask(st)
                else:
                    st = st + tab_ref[i - j]
                pt = jnp.exp(st - lse_ref[g, i])
                dv_acc[:, cols] += jnp.dot(pt.astype(BF), dob, preferred_element_type=F32)
                dpt = lax.dot_general(vb, dob, _DIMS["nt"], preferred_element_type=F32)
                dst = pt * (dpt - dl_ref[g, i])
                dsb = dst.astype(BF)
                dk_acc[:, cols] += jnp.dot(dsb, qb, preferred_element_type=F32)
                dq_ref[qs, cols] += lax.dot_general(dsb, kb, _DIMS["tn"], preferred_element_type=F32)
                if fox:
                    folded = dst[:, :LANES]
                    for part in range(1, blk // LANES):
                        folded = folded + dst[:, part * LANES:(part + 1) * LANES]
                    dc_acc[g] -= folded
                    dcq_ref[g, i] += jnp.sum(dst, axis=0, keepdims=True)

        def loop_step(i, carry):
            step(i, False)
            return carry

        if fox:
            step(j, True)
            lax.fori_loop(j + 1, nq, loop_step, 0)
        else:
            lax.fori_loop(j, jnp.minimum(nq, j + n_delta), loop_step, 0)
        dk_ref[...] = dk_acc[...]
        dv_ref[...] = dv_acc[...].astype(dv_ref.dtype)
        if fox:
            for g in range(grp):
                dc_ref[g] = jnp.sum(dc_acc[g], axis=-1, keepdims=True)

    full = lambda o, wd=wide: pl.BlockSpec((s, wd), functools.partial(lambda h, j, o: (0, o + h), o=o // grp))
    tile = lambda o, wd=wide: pl.BlockSpec((blk, wd), functools.partial(lambda h, j, o: (j, o + h), o=o // grp))
    per_q = pl.BlockSpec((grp, nq, 1, blk), lambda h, j: (h, 0, 0, 0))
    per_k = pl.BlockSpec((grp, blk, 1), lambda h, j: (h, j, 0))
    in_specs = [full(q_off, grp * qw), full(0), tile(k_off, grp * kw), tile(v_off), per_q, per_q]
    args = [q_arr, do, k_arr, v_arr, lse_row, delta_row]
    out_specs = [full(0), tile(0), tile(0)]
    hd = n_heads * HEAD_DIM
    out_shape = [jax.ShapeDtypeStruct((s, hd), F32), jax.ShapeDtypeStruct((s, hd), F32),
                 jax.ShapeDtypeStruct((s, hd), BF)]
    scratch = [pltpu.VMEM((blk, wide), F32), pltpu.VMEM((blk, wide), F32)]
    if fox:
        out_specs += [per_k, per_q]
        out_shape += [jax.ShapeDtypeStruct((n_heads, s, 1), F32), jax.ShapeDtypeStruct((n_heads, nq, 1, blk), F32)]
        scratch.append(pltpu.VMEM((grp, blk, LANES), F32))
    else:
        in_specs.append(pl.BlockSpec((n_delta, blk, blk), lambda h, j: (0, 0, 0)))
        args.append(tab_t)
    return pl.pallas_call(
        body, grid=(n_heads // grp, nq), in_specs=in_specs, out_specs=out_specs, out_shape=out_shape,
        scratch_shapes=scratch, compiler_params=_params(("parallel", "arbitrary")), name=name)(*args)


def _split3(t):
    hi = t.astype(BF)
    r1 = t - hi.astype(F32)
    mid = r1.astype(BF)
    lo = (r1 - mid.astype(F32)).astype(BF)
    return hi, mid, lo


def _tri_dot(tri, t):
    hi, mid, lo = _split3(t)
    return (jnp.dot(tri, hi, preferred_element_type=F32) + jnp.dot(tri, mid, preferred_element_type=F32)
            + jnp.dot(tri, lo, preferred_element_type=F32))


def _log_sigmoid(z):
    return jnp.minimum(z, 0.0) - jnp.log(1.0 + jnp.exp(-jnp.abs(z)))


def _forget_cumsum(name, proj, f_col, bias):
    s = proj.shape[0]
    blk = _att_block(s)

    def body(f_ref, b_ref, c_ref, carry):
        @pl.when(pl.program_id(0) == 0)
        def _():
            carry[...] = jnp.zeros_like(carry)

        lf = _log_sigmoid(f_ref[...] + b_ref[...])
        r = lax.broadcasted_iota(jnp.int32, (blk, blk), 0)
        c = lax.broadcasted_iota(jnp.int32, (blk, blk), 1)
        tri = (c <= r).astype(BF)
        c_ref[...] = _tri_dot(tri, lf) + carry[...]
        carry[...] = c_ref[pl.ds(blk - 1, 1), :]

    return pl.pallas_call(
        body, grid=(s // blk,),
        in_specs=[pl.BlockSpec((blk, LANES), lambda i: (i, f_col)), pl.BlockSpec((1, LANES), lambda i: (0, 0))],
        out_specs=pl.BlockSpec((blk, LANES), lambda i: (i, 0)), out_shape=jax.ShapeDtypeStruct((s, LANES), F32),
        scratch_shapes=[pltpu.VMEM((1, LANES), F32)],
        compiler_params=_params(("arbitrary",)), name=name)(proj, bias)


def _forget_bwd(name, proj, f_col, bias, dc):
    s = proj.shape[0]
    blk = _att_block(s)
    nb = s // blk

    def body(f_ref, b_ref, dc_ref, df_ref, db_ref, carry):
        @pl.when(pl.program_id(0) == 0)
        def _():
            carry[...] = jnp.zeros_like(carry)
            db_ref[...] = jnp.zeros_like(db_ref)

        r = lax.broadcasted_iota(jnp.int32, (blk, blk), 0)
        c = lax.broadcasted_iota(jnp.int32, (blk, blk), 1)
        tri = (c >= r).astype(BF)
        r = lax.broadcasted_iota(jnp.int32, (blk, LANES), 0)
        dlf = _tri_dot(tri, dc_ref[...]) + carry[...]
        carry[...] = jnp.sum(jnp.where(r == 0, dlf, 0.0), axis=0, keepdims=True)
        dz = dlf * _sigmoid(-(f_ref[...] + b_ref[...]))
        df_ref[...] = dz.astype(BF)
        db_ref[...] += _fold8(dz)

    rev = lambda i: (nb - 1 - i, 0)
    return pl.pallas_call(
        body, grid=(nb,),
        in_specs=[pl.BlockSpec((blk, LANES), lambda i: (nb - 1 - i, f_col)), pl.BlockSpec((1, LANES), lambda i: (0, 0)),
                  pl.BlockSpec((blk, LANES), rev)],
        out_specs=[pl.BlockSpec((blk, LANES), rev), pl.BlockSpec((8, LANES), lambda i: (0, 0))],
        out_shape=[jax.ShapeDtypeStruct((s, LANES), BF), jax.ShapeDtypeStruct((8, LANES), F32)],
        scratch_shapes=[pltpu.VMEM((1, LANES), F32)],
        compiler_params=_params(("arbitrary",)), name=name)(proj, bias, dc)


def _dproj_assemble(name, parts, gates, cos_t, sin_t, n_heads, d_model):
    s = cos_t.shape[0]
    tr = _pick(s, 1024, 16)
    scale = HEAD_DIM ** -0.5
    hd = n_heads * HEAD_DIM
    cw = _pick(math.gcd(hd, d_model), PREP_COLS, HEAD_DIM)
    widths = [hd] * 6 + [d_model] * 2
    starts = [sum(widths[:t]) // cw for t in range(len(widths) + 1)]
    cos_w, sin_w = jnp.tile(cos_t, (1, cw // HEAD_DIM)), jnp.tile(sin_t, (1, cw // HEAD_DIM))

    def body(*refs):
        p_refs, cos_ref, sin_ref, o_ref = refs[:8], refs[8], refs[9], refs[10]
        j = pl.program_id(1)
        for kind in range(8):
            @pl.when(jnp.logical_and(j >= starts[kind], j < starts[kind + 1]))
            def _(kind=kind):
                t = p_refs[kind][...]
                if kind in (0, 1, 3):
                    t = t.astype(F32)
                if kind in (0, 3):
                    t = t * scale
                if kind in (0, 1):
                    t = t * cos_ref[...] - _swap_halves(t) * sin_ref[...]
                o_ref[...] = t.astype(BF)

    def part_spec(kind):
        return pl.BlockSpec((tr, cw), functools.partial(
            lambda i, j, kind: (i, jnp.clip(j - starts[kind], 0, widths[kind] // cw - 1)), kind=kind))

    tab = pl.BlockSpec((tr, cw), lambda i, j: (i, 0))
    return pl.pallas_call(
        body, grid=(s // tr, starts[-1]), in_specs=[part_spec(kind) for kind in range(8)] + [tab, tab],
        out_specs=pl.BlockSpec((tr, cw), lambda i, j: (i, j)),
        out_shape=jax.ShapeDtypeStruct((s, sum(widths)), BF),
        compiler_params=_params(("parallel", "arbitrary")), name=name)(*parts, *gates, cos_w, sin_w)


FFN2 =("ffn2_w_gate", "ffn2_w_up", "ffn2_w_down")


def _device_step(x, tgt, shards, small):
    s, d = x.shape
    hd = shards["w_proj_dil"].shape[0]
    hh = hd // HEAD_DIM
    blk = _att_block(s)
    gate_off = 6 * hd
    f_col = 0
    n_proj = gate_off + 2 * d
    ident = lambda accs, ex, rw: (accs[0],)
    chip = 2 * lax.axis_index("x") + lax.axis_index("y")
    ids = jnp.stack([chip, lax.axis_index("c")]).astype(jnp.int32)
    w = {}

    def take_gathered(names, gathered):
        for n, t in zip(names, gathered):
            t = lax.dynamic_update_index_in_dim(t, shards[n], chip, 0)
            if n == "w_in":
                w["w_in"], w["w_in_f"] = _repack_w_in(t, hd, d)
            else:
                w[n] = _full_from_pieces(n, t)

    def chip_sums(names, pieces, from_sibling):
        return [_rs_add("rs_add_" + n, ids, g, o) for n, g, o in zip(names, pieces, from_sibling)]

    def core_halves(names, sums, from_chips):
        return [_rs_sum("rs_sum_" + n, ids, own, got) for n, own, got in zip(names, sums, from_chips)]

    def gather(names):
        return _gather_side([shards[n] for n in names])

    first, under_up = ("ffn1_w_gate", "ffn1_w_up"), ("ffn1_w_down", "w_in")
    under_down = ("w_proj_dil", "w_proj_fox", "w_out", "ffn2_w_gate")
    under_proj = ("ffn2_w_up", "ffn2_w_down")
    take_gathered(first, gather(first).call("gather_first"))

    def w_down_of_ffn1(got):
        take_gathered(under_up, got["up"])
        return w["ffn1_w_down"]

    x1, saved1, got = _ffn_fwd("ffn1", x, small["ffn1_norm"], w["ffn1_w_gate"], w["ffn1_w_up"], w_down_of_ffn1,
                               sides={"up": lambda got: gather(under_up), "down": lambda got: gather(under_down)})
    take_gathered(under_down, got["down"])

    hm = _rms_fwd("mix_rms", x1, small["mix_norm"])
    cos_t, sin_t = _rope_tables(s)
    head_tile = _pick(hd, 1024, HEAD_DIM)
    tiles_per_kind = hd // head_tile
    scale = HEAD_DIM ** -0.5

    def heads_epi(accs, ex, rw):
        kind = pl.program_id(1) // tiles_per_kind
        t = accs[0]
        reps = (1, head_tile // HEAD_DIM)
        r = t * jnp.tile(ex[0], reps) + _swap_halves(t) * jnp.tile(ex[1], reps)
        t = jnp.where(kind < 2, r, t)
        return (jnp.where(jnp.logical_or(kind == 0, kind == 3), t * scale, t),)

    (qkv,), gathered = _mm("proj_heads", [(hm, w["w_in"])], "nn", s, gate_off, d, epilogue=heads_epi, out_dtypes=(BF,),
                           extras=[(cos_t, None), (sin_t, None)], tn=head_tile, keep_tn=True,
                           side=gather(under_proj[:1]))
    take_gathered(under_proj[:1], gathered)
    (gates,), gathered = _mm("proj_gates", [(hm, w["w_in"])], "nn", s, 2 * d, d, epilogue=ident, out_dtypes=(F32,),
                             b_off=gate_off, side=gather(under_proj[1:]))
    take_gathered(under_proj[1:], gathered)
    (f_logit,) = _mm("proj_f", [(hm, w["w_in_f"])], "nn", s, LANES, d, epilogue=ident, out_dtypes=(F32,))
    tab_t = _dil_bias(blk)
    dil_src = ((qkv, 0, HEAD_DIM), (qkv, hh, HEAD_DIM), (qkv, 2 * hh, HEAD_DIM))
    y_dil, lse_d = _flash_fwd("dil_fwd", *dil_src, hh, fox=False, tab_t=tab_t)
    bias_f = jnp.pad(small["b_forget"], ((0, 0), (0, LANES - hh)))
    c_pad = _forget_cumsum("forget_cumsum", f_logit, f_col, bias_f)
    q_cat, k_cat = _fox_aug("fox_aug", c_pad, qkv, 3 * hh, 4 * hh, hh)
    fox_src = ((q_cat, 0, 2 * HEAD_DIM), (k_cat, 0, 2 * HEAD_DIM), (qkv, 5 * hh, HEAD_DIM))
    y_fox, lse_f = _flash_fwd("fox_fwd", *fox_src, hh, fox=True)

    def merge_epi(accs, ex, rw):
        ud, uf = accs
        return ud, uf, _sigmoid(ex[0] + rw[0]) * ud + _sigmoid(ex[1] + rw[1]) * uf

    u_d, u_f, merged = _mm("merge", [(y_dil, w["w_proj_dil"]), (y_fox, w["w_proj_fox"])], "nn", s, d, hd,
                           epilogue=merge_epi, out_dtypes=(BF, BF, BF),
                           extras=[(gates, 0), (gates, d)],
                           rows=[small["b_gate_dil"], small["b_gate_fox"]])
    (x2,) = _mm("mix_out", [(merged, w["w_out"])], "nn", s, d, d,
                epilogue=lambda accs, ex, rw: (ex[0] + accs[0],), out_dtypes=(F32,), extras=[(x1, 0)])

    x3, saved2, _ = _ffn_fwd("ffn2", x2, small["ffn2_norm"], w["ffn2_w_gate"], w["ffn2_w_up"],
                             lambda got: w["ffn2_w_down"])
    dx3, dx3_bf, dg_final, sq = _final("final", x3, small["final_norm"].reshape(1, d), tgt)

    dx2, dx2_bf, dg_ffn2, dw_g2, dw_u2, dw_d2, _ = _ffn_bwd("ffn2", x2, small["ffn2_norm"], w["ffn2_w_gate"],
                                                            w["ffn2_w_up"], w["ffn2_w_down"], saved2, dx3, dx3_bf)
    pieces2 = [dw_g2, dw_u2, dw_d2]

    def dmerge_epi(accs, ex, rw):
        dm = accs[0]
        gd, gf, ud, uf = ex[0], ex[1], ex[2].astype(F32), ex[3].astype(F32)
        sd, sf = _sigmoid(gd + rw[0]), _sigmoid(gf + rw[1])
        dgd = dm * ud * (sd * (1.0 - sd))
        dgf = dm * uf * (sf * (1.0 - sf))
        return (dm * sd, dm * sf, dgd, dgf, jnp.sum(dgd, axis=0, keepdims=True), jnp.sum(dgf, axis=0, keepdims=True))

    (du_d, du_f, dg_d, dg_f, dbg_d, dbg_f), from_sibling = _mm(
        "dmerge", [(dx2_bf, w["w_out"])], "nt", s, d, d, epilogue=dmerge_epi, out_dtypes=(BF, BF, BF, BF), n_colsum=2,
        extras=[(gates, 0), (gates, d), (u_d, 0), (u_f, 0)],
        rows=[small["b_gate_dil"], small["b_gate_fox"]], side=_swap_side(pieces2))
    sums2 = chip_sums(FFN2, pieces2, from_sibling)
    (dw_out,) = _mm("dw_out", [(merged, dx2_bf)], "tn", d, d, s, epilogue=ident, out_dtypes=(F32,))
    dw_out = dw_out.reshape(N_CHIPS, d // N_CHIPS, d)
    dw_pd = _dw_col_pieces("dw_pd", y_dil, du_d, hd, d, s)
    dw_pf = _dw_col_pieces("dw_pf", y_fox, du_f, hd, d, s)
    (dy_dil,) = _mm("dy_dil", [(du_d, w["w_proj_dil"])], "nt", s, hd, d, epilogue=ident, out_dtypes=(BF,))
    (dy_fox,) = _mm("dy_fox", [(du_f, w["w_proj_fox"])], "nt", s, hd, d, epilogue=ident, out_dtypes=(BF,))

    row = lambda t: t.reshape(hh, s // blk, 1, blk)
    delta_d = _att_delta("dil_delta", dy_dil, y_dil, hh)
    dq_d, dk_d, dv_d = _flash_bwd("dil_bwd", *dil_src, dy_dil, lse_d, row(delta_d), hh, fox=False, tab_t=tab_t)
    delta_f = _att_delta("fox_delta", dy_fox, y_fox, hh)
    dq_f, dk_f, dv_f, dc_k, dc_q = _flash_bwd("fox_bwd", *fox_src, dy_fox, lse_f, row(delta_f), hh, fox=True)
    dc = dc_k.reshape(hh, s) + dc_q.reshape(hh, s)
    dc_pad = jnp.pad(dc.T, ((0, 0), (0, LANES - hh)))
    df, db_forget = _forget_bwd("forget_bwd", f_logit, f_col, bias_f, dc_pad)
    dproj = _dproj_assemble("dproj", [dq_d, dk_d, dv_d, dq_f, dk_f, dv_f], [dg_d, dg_f], cos_t, sin_t, hh, d)
    (dhm_f,) = _mm("dhm_f", [(df, w["w_in_f"])], "nt", s, d, LANES, epilogue=ident, out_dtypes=(F32,))
    (dhm,), from_chips = _mm("dhm", [(dproj, w["w_in"])], "nt", s, d, n_proj,
                             epilogue=lambda accs, ex, rw: (accs[0] + ex[0],), out_dtypes=(F32,),
                             extras=[(dhm_f, 0)], side=_scatter_side(sums2))
    halves2 = core_halves(FFN2, sums2, from_chips)
    small_mixer = ("w_proj_dil", "w_proj_fox", "w_out")
    pieces_sm = [dw_pd, dw_pf, dw_out]
    (dw_in,), from_sibling = _mm("dw_in", [(hm, dproj)], "tn", d, n_proj, s, epilogue=ident, out_dtypes=(F32,),
                                 side=_swap_side(pieces_sm))
    sums_sm = chip_sums(small_mixer, pieces_sm, from_sibling)
    (dw_in_f,) = _mm("dw_in_f", [(hm, df)], "tn", d, LANES, s, epilogue=ident, out_dtypes=(F32,))
    dw_in = _dw_in_pieces(dw_in, dw_in_f, hd, d)
    dx1, dx1_bf, dg_mix = _rms_bwd("mix_rmsb", x1, small["mix_norm"], dhm, dx2)

    sums, early = {}, ("w_in",) + small_mixer + FFN2

    def under_dwd(got):
        sums["w_in"] = chip_sums(["w_in"], [dw_in], got["dact"])
        return _scatter_side(sums_sm)

    def under_dwu(got):
        return _swap_side([got["dw_down"], got["dw_gate"]])

    def under_dh(got):
        sums["dg"] = chip_sums(["ffn1_w_down", "ffn1_w_gate"], [got["dw_down"], got["dw_gate"]], got["dwu"])
        halves = (core_halves(["w_in"], sums["w_in"], got["dwg"]) + core_halves(small_mixer, sums_sm, got["dwd"])
                  + halves2)
        return _scatter_side(sums["dg"]).beside(_swap_side([got["dw_up"]])).beside(_join_side(halves))

    def under_rmsb(got):
        sums["u"] = chip_sums(["ffn1_w_up"], [got["dw_up"]], got["dh"][2:3])
        return _scatter_side(sums["u"])

    dx0, _, dg_ffn1, dw_g1, dw_u1, dw_d1, got = _ffn_bwd(
        "ffn1", x, small["ffn1_norm"], w["ffn1_w_gate"], w["ffn1_w_up"], w["ffn1_w_down"], saved1, dx1, dx1_bf,
        sides={"dact": lambda got: _swap_side([dw_in]), "dwd": under_dwd,
               "dwg": lambda got: _scatter_side(sums["w_in"]), "dwu": under_dwu, "dh": under_dh, "rmsb": under_rmsb})
    halves1 = (core_halves(["ffn1_w_down", "ffn1_w_gate"], sums["dg"], got["dh"][:2])
               + core_halves(["ffn1_w_up"], sums["u"], got["rmsb"]))
    grads = dict(zip(early, got["dh"][3:]))
    grads.update(zip(("ffn1_w_down", "ffn1_w_gate", "ffn1_w_up"), _join_side(halves1).call("rs_join")))
    partials = {"ffn1_norm": dg_ffn1, "mix_norm": dg_mix, "ffn2_norm": dg_ffn2, "final_norm": dg_final,
                "b_gate_dil": dbg_d.reshape(-1, d), "b_gate_fox": dbg_f.reshape(-1, d), "b_forget": db_forget, "sq": sq}
    return dx0, grads, partials


def _coords():
    return lax.axis_index("x"), lax.axis_index("y"), lax.axis_index("c")


def _other_chips(x, y):
    return [(1 - x, y), (x, 1 - y), (1 - x, 1 - y)]


ANY_SPEC = pl.BlockSpec(memory_space=pl.ANY)


def _gather_side(shards):
    nw = len(shards)

    def copies(srcs, outs, send_sems, recv_sems):
        x, y, c = _coords()
        chips = _other_chips(x, y)

        def slot(w, px, py, pc):
            half = shards[w].shape[0] // 2
            return outs[w].at[2 * px + py, pl.ds(pc * half, half), :]

        def copy(w, k, src_ref, dst_ref, to):
            return pltpu.make_async_remote_copy(src_ref=src_ref, dst_ref=dst_ref, send_sem=send_sems.at[6 * w + k],
                                                recv_sem=recv_sems.at[6 * w + k], device_id=to, device_id_type=MESH)

        first, arrive, passed, arrive2 = [], [], [], []
        for w in range(nw):
            half = shards[w].shape[0] // 2
            for j, chip in enumerate(chips):
                first.append(copy(w, j, srcs[w].at[pl.ds(c * half, half), :], slot(w, x, y, c), (*chip, c)))
                arrive.append(copy(w, j, slot(w, *chip, c), slot(w, *chip, c), (*chip, c)))
                passed.append(copy(w, 3 + j, slot(w, *chip, c), slot(w, *chip, c), (x, y, 1 - c)))
                arrive2.append(copy(w, 3 + j, slot(w, *chip, 1 - c), slot(w, *chip, 1 - c), (x, y, 1 - c)))
        return first, arrive, passed, arrive2

    def start(srcs, outs, send_sems, recv_sems):
        for cp in copies(srcs, outs, send_sems, recv_sems)[0]:
            cp.start()

    def relay(srcs, outs, send_sems, recv_sems):
        _, arrive, passed, _ = copies(srcs, outs, send_sems, recv_sems)
        for got, fwd in zip(arrive, passed):
            got.wait_recv()
            fwd.start()

    def finish(srcs, outs, send_sems, recv_sems):
        first, _, passed, arrive2 = copies(srcs, outs, send_sems, recv_sems)
        for got in arrive2:
            got.wait_recv()
        for cp in first + passed:
            cp.wait_send()

    return _Side(shards, [jax.ShapeDtypeStruct((N_CHIPS, *t.shape), t.dtype) for t in shards], 6 * nw, start, finish,
                 relay=relay)


def _swap_side(grads):
    nw = len(grads)

    def copies(srcs, outs, send_sems, recv_sems):
        x, y, c = _coords()
        res = []
        for w in range(nw):
            half = grads[w].shape[1] // 2
            for p in range(N_CHIPS):
                k = N_CHIPS * w + p
                res.append(pltpu.make_async_remote_copy(
                    src_ref=srcs[w].at[p, pl.ds((1 - c) * half, half), :], dst_ref=outs[w].at[p],
                    send_sem=send_sems.at[k], recv_sem=recv_sems.at[k], device_id=(x, y, 1 - c), device_id_type=MESH))
        return res

    def start(*refs):
        for cp in copies(*refs):
            cp.start()

    def finish(*refs):
        for cp in copies(*refs):
            cp.wait()

    shapes = [jax.ShapeDtypeStruct((N_CHIPS, t.shape[1] // 2, t.shape[2]), t.dtype) for t in grads]
    return _Side(grads, shapes, N_CHIPS * nw, start, finish)


def _rs_add(name, ids, g, other):
    n, rows, cols = g.shape
    half = rows // 2
    tr = _pick(half, 256, 16)
    nb = half // tr

    def body(ids_ref, g_ref, o_ref, out_ref):
        out_ref[...] = (g_ref[...] + o_ref[...]).astype(BF)

    grid_spec = pltpu.PrefetchScalarGridSpec(
        num_scalar_prefetch=1, grid=(n, nb),
        in_specs=[pl.BlockSpec((None, tr, cols), lambda p, i, ids_ref: (p, ids_ref[1] * nb + i, 0)),
                  pl.BlockSpec((None, tr, cols), lambda p, i, ids_ref: (p, i, 0))],
        out_specs=pl.BlockSpec((None, tr, cols), lambda p, i, ids_ref: (p, i, 0)))
    return pl.pallas_call(body, grid_spec=grid_spec, out_shape=jax.ShapeDtypeStruct((n, half, cols), BF),
                          compiler_params=_params(("parallel", "parallel")), name=name)(ids, g, other)


def _scatter_side(sums):
    nw = len(sums)

    def copies(srcs, outs, send_sems, recv_sems):
        x, y, c = _coords()
        res = []
        for w in range(nw):
            for k, (px, py) in enumerate(_other_chips(x, y)):
                res.append(pltpu.make_async_remote_copy(
                    src_ref=srcs[w].at[2 * px + py], dst_ref=outs[w].at[k], send_sem=send_sems.at[3 * w + k],
                    recv_sem=recv_sems.at[3 * w + k], device_id=(px, py, c), device_id_type=MESH))
        return res

    def start(*refs):
        for cp in copies(*refs):
            cp.start()

    def finish(*refs):
        for cp in copies(*refs):
            cp.wait()

    return _Side(sums, [jax.ShapeDtypeStruct((3, *t.shape[1:]), t.dtype) for t in sums], 3 * nw, start, finish)


def _rs_sum(name, ids, own, got):
    n, half, cols = own.shape
    tr = _pick(half, 256, 16)
    nb = half // tr

    def body(ids_ref, own_ref, got_ref, out_ref):
        t = own_ref[...].astype(F32)
        for k in range(3):
            t = t + got_ref[k].astype(F32)
        out_ref[...] = t

    grid_spec = pltpu.PrefetchScalarGridSpec(
        num_scalar_prefetch=1, grid=(nb,),
        in_specs=[pl.BlockSpec((None, tr, cols), lambda i, ids_ref: (ids_ref[0], i, 0)),
                  pl.BlockSpec((3, tr, cols), lambda i, ids_ref: (0, i, 0))],
        out_specs=pl.BlockSpec((tr, cols), lambda i, ids_ref: (ids_ref[1] * nb + i, 0)))
    return pl.pallas_call(body, grid_spec=grid_spec, out_shape=jax.ShapeDtypeStruct((2 * half, cols), F32),
                          compiler_params=_params(("parallel",)), name=name)(ids, own, got)


def _join_side(totals):
    nw = len(totals)

    def start(ins, bufs, send_sems, recv_sems):
        x, y, c = _coords()
        for w in range(nw):
            half = totals[w].shape[0] // 2
            pltpu.make_async_remote_copy(
                src_ref=bufs[w].at[pl.ds(c * half, half), :], dst_ref=bufs[w].at[pl.ds(c * half, half), :],
                send_sem=send_sems.at[w], recv_sem=recv_sems.at[w], device_id=(x, y, 1 - c),
                device_id_type=MESH).start()

    def finish(ins, bufs, send_sems, recv_sems):
        x, y, c = _coords()
        for w in range(nw):
            half = totals[w].shape[0] // 2
            arrival = pltpu.make_async_remote_copy(
                src_ref=bufs[w].at[pl.ds(c * half, half), :], dst_ref=bufs[w].at[pl.ds((1 - c) * half, half), :],
                send_sem=send_sems.at[w], recv_sem=recv_sems.at[w], device_id=(x, y, 1 - c), device_id_type=MESH)
            arrival.wait_recv()
            arrival.wait_send()

    return _Side(totals, [jax.ShapeDtypeStruct(t.shape, t.dtype) for t in totals], nw, start, finish,
                 aliases={w: w for w in range(nw)})


def _gather_all(name, t):
    rows, cols = t.shape

    def body(src, out, send_sems, recv_sems, local_sem):
        x, y, c = _coords()
        me = 4 * x + 2 * y + c
        mine = pltpu.make_async_copy(src, out.at[me], local_sem)
        mine.start()
        peers = [(x ^ (k >> 2 & 1), y ^ (k >> 1 & 1), c ^ (k & 1)) for k in range(1, N_DEV)]
        sends = [pltpu.make_async_remote_copy(src_ref=src, dst_ref=out.at[me], send_sem=send_sems.at[k],
                                              recv_sem=recv_sems.at[k], device_id=peer, device_id_type=MESH)
                 for k, peer in enumerate(peers)]
        for cp in sends:
            cp.start()
        for k, (px, py, pc) in enumerate(peers):
            pltpu.make_async_remote_copy(src_ref=src, dst_ref=out.at[4 * px + 2 * py + pc], send_sem=send_sems.at[k],
                                         recv_sem=recv_sems.at[k], device_id=(px, py, pc),
                                         device_id_type=MESH).wait_recv()
        for cp in sends:
            cp.wait_send()
        mine.wait()

    vmem = pl.BlockSpec(memory_space=pltpu.VMEM)
    return pl.pallas_call(
        body, in_specs=[vmem], out_specs=vmem, out_shape=jax.ShapeDtypeStruct((N_DEV, rows, cols), t.dtype),
        scratch_shapes=[pltpu.SemaphoreType.DMA((7,)), pltpu.SemaphoreType.DMA((7,)), pltpu.SemaphoreType.DMA],
        name=name)(t)


def _adamw_math(w, g, m, v):
    m = ADAM_B1 * m + (1.0 - ADAM_B1) * g
    v = ADAM_B2 * v + (1.0 - ADAM_B2) * (g * g)
    m_hat = m / (1.0 - ADAM_B1 ** ADAM_STEP)
    v_hat = v / (1.0 - ADAM_B2 ** ADAM_STEP)
    delta = -ADAM_LR * (m_hat / (jnp.sqrt(v_hat) + ADAM_EPS) + ADAM_WD * w)
    return delta, m, v


def _adamw(name, w, g, m, v):
    _, rows, cols = w.shape
    tr = _pick(rows, 256, 8)

    def body(w_ref, g_ref, m_ref, v_ref, g_out, d_out, m_out, v_out):
        g = g_ref[...]
        g_out[...] = g
        d_out[...], m_out[...], v_out[...] = _adamw_math(w_ref[...], g, m_ref[...], v_ref[...])

    blk3 = pl.BlockSpec((None, tr, cols), lambda i: (0, i, 0))
    blk = pl.BlockSpec((tr, cols), lambda i: (i, 0))
    shape = jax.ShapeDtypeStruct((rows, cols), F32)
    return pl.pallas_call(
        body, grid=(rows // tr,), in_specs=[blk3, blk, blk3, blk3], out_specs=[blk] * 4, out_shape=[shape] * 4,
        compiler_params=_params(("parallel",)), name=name)(w, g, m, v)


def _small_reduce(name, parts, width):
    def body(*refs):
        out = refs[-1]
        out[...] = jnp.zeros_like(out)
        for k, r in enumerate(refs[:-1]):
            out[pl.ds(k, 1), :] = jnp.sum(r[...], axis=0, keepdims=True)

    vmem = pl.BlockSpec(memory_space=pltpu.VMEM)
    return pl.pallas_call(body, in_specs=[vmem] * len(parts), out_specs=vmem,
                          out_shape=jax.ShapeDtypeStruct((8, width), F32), name=name)(*parts)


def _small_adamw(name, gathered, w, m, v, loss_row, loss_scale):
    def body(gt_ref, w_ref, m_ref, v_ref, g_out, d_out, m_out, v_out, loss_out):
        g = gt_ref[0]
        for k in range(1, N_DEV):
            g = g + gt_ref[k]
        g_out[...] = g
        row = lax.broadcasted_iota(jnp.int32, g.shape, 0)
        loss_out[...] = jnp.sum(jnp.where(row == loss_row, g, 0.0), keepdims=True) * loss_scale
        d_out[...], m_out[...], v_out[...] = _adamw_math(w_ref[...], g, m_ref[...], v_ref[...])

    vmem = pl.BlockSpec(memory_space=pltpu.VMEM)
    shape = jax.ShapeDtypeStruct(w.shape, F32)
    return pl.pallas_call(body, in_specs=[vmem] * 4, out_specs=[vmem] * 5,
                          out_shape=[shape] * 4 + [jax.ShapeDtypeStruct((1, 1), F32)], name=name)(gathered, w, m, v)


def _full_from_pieces(name, pieces):
    _, rows, cols = pieces.shape
    if name in ROW_SHARDED:
        return pieces.reshape(N_CHIPS * rows, cols)
    return pieces.transpose(1, 0, 2).reshape(rows, N_CHIPS * cols)


def _column_range(segments, lo, hi):
    out, start = [], 0
    for t in segments:
        a, b = max(lo, start), min(hi, start + t.shape[1])
        if a < b:
            out.append(t[:, a - start:b - start])
        start += t.shape[1]
    return out


def _repack_w_in(pieces, hd, d):
    hh = hd // HEAD_DIM
    segs = [pieces[p] for p in range(N_CHIPS)]
    total = N_CHIPS * pieces.shape[2]
    main = jnp.concatenate(_column_range(segs, 0, 6 * hd) + _column_range(segs, 6 * hd + hh, total), axis=1)
    f = jnp.concatenate(_column_range(segs, 6 * hd, 6 * hd + hh), axis=1)
    return main, jnp.pad(f, ((0, 0), (0, LANES - hh)))


def _dw_in_pieces(dw, dw_f, hd, d):
    hh = hd // HEAD_DIM
    segs = [dw[:, :6 * hd], dw_f[:, :hh], dw[:, 6 * hd:]]
    cs = (6 * hd + hh + 2 * d) // N_CHIPS
    return jnp.stack([jnp.concatenate(_column_range(segs, p * cs, (p + 1) * cs), axis=1) for p in range(N_CHIPS)])


def _small_pack(vals, width):
    rows = []
    for name in SMALL:
        t = vals[name].reshape(1, -1)
        rows.append(jnp.pad(t, ((0, 0), (0, width - t.shape[1]))))
    rows.append(jnp.zeros((8 - len(SMALL), width), F32))
    return jnp.concatenate(rows, axis=0)


def kernel(x, ffn1_norm, ffn1_w_gate, ffn1_w_up, ffn1_w_down, mix_norm, w_in, b_forget, b_gate_dil, b_gate_fox, w_proj_dil, w_proj_fox, w_out, ffn2_norm, ffn2_w_gate, ffn2_w_up, ffn2_w_down, final_norm, loss_target, m_ffn1_norm, m_ffn1_w_gate, m_ffn1_w_up, m_ffn1_w_down, m_mix_norm, m_w_in, m_b_forget, m_b_gate_dil, m_b_gate_fox, m_w_proj_dil, m_w_proj_fox, m_w_out, m_ffn2_norm, m_ffn2_w_gate, m_ffn2_w_up, m_ffn2_w_down, m_final_norm, v_ffn1_norm, v_ffn1_w_gate, v_ffn1_w_up, v_ffn1_w_down, v_mix_norm, v_w_in, v_b_forget, v_b_gate_dil, v_b_gate_fox, v_w_proj_dil, v_w_proj_fox, v_w_out, v_ffn2_norm, v_ffn2_w_gate, v_ffn2_w_up, v_ffn2_w_down, v_final_norm):
    given = dict(locals())
    wts = {n: given[n] for n in WEIGHTS}
    mom_m = {n: given["m_" + n] for n in WEIGHTS}
    mom_v = {n: given["v_" + n] for n in WEIGHTS}
    d = x.shape[2]

    shards = {n: wts[n][0].astype(BF) for n in SHARDED}
    small = {n: wts[n] for n in SMALL}
    grad_x, grads, partials = _device_step(x[0], loss_target[0], shards, small)

    out_g, out_d, out_m, out_v = {}, {}, {}, {}
    for n in SHARDED:
        outs = _adamw("adamw_" + n, wts[n], grads[n], mom_m[n], mom_v[n])
        out_g[n], out_d[n], out_m[n], out_v[n] = (t[None] for t in outs)

    width = d
    part_rows = []
    for n in SMALL:
        t = partials[n]
        part_rows.append(jnp.pad(t, ((0, 0), (0, width - t.shape[1]))))
    part_rows.append(partials["sq"])
    local_small = _small_reduce("small_reduce", part_rows, width)
    gathered_small = _gather_all("small_gather", local_small)
    sg, sd_, sm, sv, loss = _small_adamw("small_adamw", gathered_small, _small_pack(wts, width),
                                         _small_pack(mom_m, width), _small_pack(mom_v, width), len(SMALL), 0.5 / d)
    for k, n in enumerate(SMALL):
        shp = wts[n].shape
        take = lambda t: t[k, :shp[-1]].reshape(shp)
        out_g[n], out_d[n], out_m[n], out_v[n] = take(sg), take(sd_), take(sm), take(sv)
    return (loss[0, 0], grad_x[None], *[out_g[n] for n in WEIGHTS], *[out_d[n] for n in WEIGHTS],
            *[out_m[n] for n in WEIGHTS], *[out_v[n] for n in WEIGHTS])
```

```python
import functools
import math

import jax
import jax.numpy as jnp
from jax import lax
from jax.experimental import pallas as pl
from jax.experimental.pallas import tpu as pltpu

HEAD_DIM = 128
ROPE_DIM = HEAD_DIM // 4
ROPE_THETA = 500000.0
DIL_PATTERNS = ((128, 1), (512, 4), (2048, 16))
MAX_WINDOW = 2048
NORM_EPS = 1e-6
ADAM_LR = 0.001
ADAM_B1 = 0.9
ADAM_B2 = 0.999
ADAM_EPS = 1e-08
ADAM_WD = 0.01
ADAM_STEP = 10

BF = jnp.bfloat16
F32 = jnp.float32
NEG = -1e30
LANES = 128
ATT_BLOCK = 512
PREP_COLS = 512
ATT_HEADS = 2
ATT_HEADS_FWD = 4
VMEM_LIMIT = 56 * 1024 * 1024
MM_VMEM_BUDGET = 46 * 1024 * 1024
N_CHIPS = 4
N_DEV = 8
MESH = pl.DeviceIdType.MESH

SHARDED = ("ffn1_w_gate", "ffn1_w_up", "ffn1_w_down", "w_in", "w_proj_dil", "w_proj_fox", "w_out",
           "ffn2_w_gate", "ffn2_w_up", "ffn2_w_down")
ROW_SHARDED = ("ffn1_w_down", "w_out", "ffn2_w_down")
SMALL = ("ffn1_norm", "mix_norm", "b_forget", "b_gate_dil", "b_gate_fox", "ffn2_norm", "final_norm")
WEIGHTS = ("ffn1_norm", "ffn1_w_gate", "ffn1_w_up", "ffn1_w_down", "mix_norm", "w_in", "b_forget",
           "b_gate_dil", "b_gate_fox", "w_proj_dil", "w_proj_fox", "w_out", "ffn2_norm", "ffn2_w_gate",
           "ffn2_w_up", "ffn2_w_down", "final_norm")


def _pick(n, target, align):
    best = None
    for d in range(align, min(n, target) + 1, align):
        if n % d == 0:
            best = d
    return n if best is None else best


def _params(sem=None):
    return pltpu.CompilerParams(dimension_semantics=sem, vmem_limit_bytes=VMEM_LIMIT)


_DIMS = {"nn": (((1,), (0,)), ((), ())), "nt": (((1,), (1,)), ((), ())), "tn": (((0,), (0,)), ((), ()))}


class _SemsFrom:
    def __init__(self, sems, first):
        self.sems, self.first = sems, first

    @property
    def at(self):
        return self

    def __getitem__(self, k):
        return self.sems.at[self.first + k]


class _Side:
    def __init__(self, inputs, out_shapes, n_sems, start, finish, aliases=None):
        self.inputs, self.out_shapes, self.n_sems = list(inputs), list(out_shapes), n_sems
        self.start, self.finish, self.aliases = start, finish, aliases or {}

    def scratch(self):
        return [pltpu.SemaphoreType.DMA((self.n_sems,)), pltpu.SemaphoreType.DMA((self.n_sems,))]

    def beside(self, other):
        n_in, n_out, n_sems = len(self.inputs), len(self.out_shapes), self.n_sems

        def part(fn_a, fn_b):
            def run(ins, outs, send_sems, recv_sems):
                fn_a(ins[:n_in], outs[:n_out], send_sems, recv_sems)
                fn_b(ins[n_in:], outs[n_out:], _SemsFrom(send_sems, n_sems), _SemsFrom(recv_sems, n_sems))
            return run

        aliases = {**self.aliases, **{n_in + t: n_out + o for t, o in other.aliases.items()}}
        return _Side(self.inputs + other.inputs, self.out_shapes + other.out_shapes, n_sems + other.n_sems,
                     part(self.start, other.start), part(self.finish, other.finish), aliases=aliases)

    def call(self, name):
        n_in, n_out = len(self.inputs), len(self.out_shapes)

        def body(*refs):
            ins, outs, sems = refs[:n_in], refs[n_in:n_in + n_out], refs[n_in + n_out:]
            self.start(ins, outs, *sems)
            self.finish(ins, outs, *sems)

        return pl.pallas_call(body, in_specs=[ANY_SPEC] * n_in, out_specs=[ANY_SPEC] * n_out, out_shape=self.out_shapes,
                              input_output_aliases=self.aliases, scratch_shapes=self.scratch(), name=name)(*self.inputs)


def _mm(name, pairs, mode, m, n, k, *, epilogue, out_dtypes, extras=(), rows=(), n_colsum=0,
        sum_pairs=False, tm=1024, tn=1152, tk=2048, piece_layout=False, side=None, b_off=0, keep_tn=False):
    m_align = LANES if mode == "tn" else 8
    tm = _pick(m, tm, m_align)
    tn = n // N_CHIPS if piece_layout else _pick(n, tn, LANES)
    tk = _pick(k, tk, LANES)
    n_acc = 1 if sum_pairs else len(pairs)
    lhs = []
    for a, _ in pairs:
        if not any(a is t for t in lhs):
            lhs.append(a)
    lhs_of = [next(t for t in range(len(lhs)) if lhs[t] is a) for a, _ in pairs]
    n_mm = len(lhs) + len(pairs)
    n_in = n_mm + len(extras) + len(rows)
    n_out = len(out_dtypes) + n_colsum

    def vmem_bytes(tm_, tn_, tk_):
        tiles = sum(tm_ * tk_ * a.dtype.itemsize for a in lhs) + sum(tn_ * tk_ * b.dtype.itemsize for _, b in pairs)
        tiles += sum(tm_ * (arr.shape[1] if off is None else tn_) * arr.dtype.itemsize for arr, off in extras)
        tiles += sum(tm_ * tn_ * jnp.dtype(dt).itemsize for dt in out_dtypes)
        return 2 * tiles + (n_acc + len(extras) + len(out_dtypes)) * tm_ * tn_ * 4

    while vmem_bytes(tm, tn, tk) > MM_VMEM_BUDGET:
        if tn > 512 and not piece_layout and not keep_tn:
            tn = _pick(n, tn - LANES, LANES)
        elif tm > 512:
            tm = _pick(m, tm - m_align, m_align)
        elif tk > 512:
            tk = _pick(k, tk - LANES, LANES)
        elif tm > 256:
            tm = _pick(m, tm - m_align, m_align)
        else:
            break
    nk = k // tk

    n_side_in = len(side.inputs) if side else 0
    n_side_out = len(side.out_shapes) if side else 0
    grid = (m // tm, n // tn, nk)

    def body(*refs):
        ins, refs = refs[:n_in], refs[n_in:]
        side_ins, refs = refs[:n_side_in], refs[n_side_in:]
        outs, refs = refs[:n_out], refs[n_out:]
        side_outs, refs = refs[:n_side_out], refs[n_side_out:]
        accs, side_sems = refs[:n_acc], refs[n_acc:]
        kk = pl.program_id(2)
        if side:
            at = [pl.program_id(t) for t in range(3)]

            @pl.when(jnp.logical_and(jnp.logical_and(at[0] == 0, at[1] == 0), at[2] == 0))
            def _():
                side.start(side_ins, side_outs, *side_sems)

        @pl.when(kk == 0)
        def _():
            for acc in accs:
                acc[...] = jnp.zeros_like(acc)

        a_tiles = [r[...].astype(BF) for r in ins[:len(lhs)]]
        for p in range(len(pairs)):
            b = ins[len(lhs) + p][...].astype(BF)
            accs[0 if sum_pairs else p][...] += lax.dot_general(a_tiles[lhs_of[p]], b, _DIMS[mode],
                                                                preferred_element_type=F32)

        @pl.when(kk == nk - 1)
        def _():
            ex = [r[...] for r in ins[n_mm:n_mm + len(extras)]]
            rw = [r[...] for r in ins[n_mm + len(extras):]]
            res = epilogue([acc[...] for acc in accs], ex, rw)
            for o, r in zip(outs, res):
                o[...] = r.astype(o.dtype)

        if side:
            @pl.when(jnp.logical_and(jnp.logical_and(at[0] == grid[0] - 1, at[1] == grid[1] - 1), at[2] == nk - 1))
            def _():
                side.finish(side_ins, side_outs, *side_sems)

    in_specs, args = [], []
    for a in lhs:
        if mode == "tn":
            in_specs.append(pl.BlockSpec((tk, tm), lambda i, j, kk: (kk, i)))
        else:
            in_specs.append(pl.BlockSpec((tm, tk), lambda i, j, kk: (i, kk)))
        args.append(a)
    assert b_off % tn == 0 and (b_off == 0 or mode == "nn")
    for _, b in pairs:
        if mode == "nt":
            in_specs.append(pl.BlockSpec((tn, tk), lambda i, j, kk: (j, kk)))
        else:
            in_specs.append(pl.BlockSpec((tk, tn), functools.partial(lambda i, j, kk, o: (kk, j + o), o=b_off // tn)))
        args.append(b)
    for arr, off in extras:
        if off is None:
            in_specs.append(pl.BlockSpec((tm, arr.shape[1]), lambda i, j, kk: (i, 0)))
        else:
            assert off % tn == 0
            in_specs.append(pl.BlockSpec((tm, tn), functools.partial(lambda i, j, kk, o: (i, j + o), o=off // tn)))
        args.append(arr)
    for arr in rows:
        in_specs.append(pl.BlockSpec((1, tn), lambda i, j, kk: (0, j)))
        args.append(arr)
    if piece_layout:
        out_specs = [pl.BlockSpec((None, tm, tn), lambda i, j, kk: (j, i, 0)) for _ in out_dtypes]
        out_shape = [jax.ShapeDtypeStruct((n // tn, m, tn), d) for d in out_dtypes]
    else:
        out_specs = [pl.BlockSpec((tm, tn), lambda i, j, kk: (i, j)) for _ in out_dtypes]
        out_shape = [jax.ShapeDtypeStruct((m, n), d) for d in out_dtypes]
    for _ in range(n_colsum):
        out_specs.append(pl.BlockSpec((None, 1, tn), lambda i, j, kk: (i, 0, j)))
        out_shape.append(jax.ShapeDtypeStruct((m // tm, 1, n), F32))
    scratch = [pltpu.VMEM((tm, tn), F32) for _ in range(n_acc)]
    if side is None:
        return pl.pallas_call(
            body, grid=grid, in_specs=in_specs, out_specs=out_specs, out_shape=out_shape, scratch_shapes=scratch,
            compiler_params=_params(("parallel", "parallel", "arbitrary")), name=name)(*args)
    res = pl.pallas_call(
        body, grid=grid, in_specs=in_specs + [ANY_SPEC] * n_side_in, out_specs=out_specs + [ANY_SPEC] * n_side_out,
        out_shape=out_shape + side.out_shapes, scratch_shapes=scratch + side.scratch(),
        input_output_aliases={n_in + t: n_out + o for t, o in side.aliases.items()},
        compiler_params=_params(("arbitrary", "arbitrary", "arbitrary")), name=name)(*args, *side.inputs)
    return res[:n_out], res[n_out:]


def _col_pieces(full):
    rows, cols = full.shape
    return full.reshape(rows, N_CHIPS, cols // N_CHIPS).transpose(1, 0, 2)


def _dw_col_pieces(name, a, b, m, n, k, side=None):
    ident = lambda accs, ex, rw: (accs[0],)
    aligned = (n // N_CHIPS) % LANES == 0
    res = _mm(name, [(a, b)], "tn", m, n, k, epilogue=ident, out_dtypes=(F32,), tm=512 if aligned else 1024,
              piece_layout=aligned, side=side)
    (out,), side_res = res if side else (res, None)
    out = out if aligned else _col_pieces(out)
    return (out, side_res) if side else out


def _hosted(sides, got, key, call):
    make = sides.get(key) if sides else None
    if make is None:
        return call(None)
    outs, got[key] = call(make(got))
    return outs


def _sigmoid(z):
    return 0.5 * jnp.tanh(0.5 * z) + 0.5


def _row_tile(s):
    return _pick(s, 256, 8)


def _fold8(t):
    r, d = t.shape
    return jnp.sum(t.reshape(r // 8, 8, d), axis=0)


def _rms_fwd(name, x, g):
    s, d = x.shape
    tr = _row_tile(s)

    def body(x_ref, g_ref, h_ref):
        xf = x_ref[...]
        y = xf * lax.rsqrt(jnp.mean(xf * xf, axis=-1, keepdims=True) + NORM_EPS)
        h_ref[...] = (y * g_ref[...]).astype(BF)

    return pl.pallas_call(
        body, grid=(s // tr,),
        in_specs=[pl.BlockSpec((tr, d), lambda i: (i, 0)), pl.BlockSpec((1, d), lambda i: (0, 0))],
        out_specs=pl.BlockSpec((tr, d), lambda i: (i, 0)), out_shape=jax.ShapeDtypeStruct((s, d), BF),
        compiler_params=_params(("parallel",)), name=name)(x, g)


def _rms_bwd(name, x, g, dh, dres, side=None):
    s, d = x.shape
    tr = _row_tile(s)
    steps = s // tr
    n_side_in = len(side.inputs) if side else 0
    n_side_out = len(side.out_shapes) if side else 0

    def body(*refs):
        x_ref, g_ref, dh_ref, dres_ref = refs[:4]
        side_ins, refs = refs[4:4 + n_side_in], refs[4 + n_side_in:]
        dx_ref, dxb_ref, dg_ref = refs[:3]
        side_outs, side_sems = refs[3:3 + n_side_out], refs[3 + n_side_out:]

        @pl.when(pl.program_id(0) == 0)
        def _():
            dg_ref[...] = jnp.zeros_like(dg_ref)
            if side:
                side.start(side_ins, side_outs, *side_sems)

        xf = x_ref[...]
        rstd = lax.rsqrt(jnp.mean(xf * xf, axis=-1, keepdims=True) + NORM_EPS)
        xhat = xf * rstd
        dhf = dh_ref[...]
        dg_ref[...] += _fold8(dhf * xhat)
        dxh = dhf * g_ref[...]
        dx = dres_ref[...] + rstd * (dxh - xhat * jnp.mean(dxh * xhat, axis=-1, keepdims=True))
        dx_ref[...] = dx
        dxb_ref[...] = dx.astype(BF)

        if side:
            @pl.when(pl.program_id(0) == steps - 1)
            def _():
                side.finish(side_ins, side_outs, *side_sems)

    blk = pl.BlockSpec((tr, d), lambda i: (i, 0))
    res = pl.pallas_call(
        body, grid=(steps,),
        in_specs=[blk, pl.BlockSpec((1, d), lambda i: (0, 0)), blk, blk] + [ANY_SPEC] * n_side_in,
        out_specs=[blk, blk, pl.BlockSpec((8, d), lambda i: (0, 0))] + [ANY_SPEC] * n_side_out,
        out_shape=[jax.ShapeDtypeStruct((s, d), F32), jax.ShapeDtypeStruct((s, d), BF),
                   jax.ShapeDtypeStruct((8, d), F32)] + (side.out_shapes if side else []),
        scratch_shapes=side.scratch() if side else [],
        compiler_params=_params(("arbitrary",)), name=name)(x, g, dh, dres, *(side.inputs if side else []))
    return (res[:3], res[3:]) if side else res


def _final(name, x, g, tgt):
    s, d = x.shape
    tr = _row_tile(s)

    def body(x_ref, g_ref, t_ref, dx_ref, dxb_ref, dg_ref, sq_ref):
        @pl.when(pl.program_id(0) == 0)
        def _():
            dg_ref[...] = jnp.zeros_like(dg_ref)
            sq_ref[...] = jnp.zeros_like(sq_ref)

        xf = x_ref[...]
        rstd = lax.rsqrt(jnp.mean(xf * xf, axis=-1, keepdims=True) + NORM_EPS)
        xhat = xf * rstd
        gf = g_ref[...]
        err = xhat * gf - t_ref[...]
        sq_ref[...] += _fold8(err * err)
        dy = err * (1.0 / d)
        dg_ref[...] += _fold8(dy * xhat)
        dxh = dy * gf
        dx = rstd * (dxh - xhat * jnp.mean(dxh * xhat, axis=-1, keepdims=True))
        dx_ref[...] = dx
        dxb_ref[...] = dx.astype(BF)

    blk = pl.BlockSpec((tr, d), lambda i: (i, 0))
    acc = pl.BlockSpec((8, d), lambda i: (0, 0))
    return pl.pallas_call(
        body, grid=(s // tr,), in_specs=[blk, pl.BlockSpec((1, d), lambda i: (0, 0)), blk],
        out_specs=[blk, blk, acc, acc],
        out_shape=[jax.ShapeDtypeStruct((s, d), F32), jax.ShapeDtypeStruct((s, d), BF),
                   jax.ShapeDtypeStruct((8, d), F32), jax.ShapeDtypeStruct((8, d), F32)],
        compiler_params=_params(("arbitrary",)), name=name)(x, g, tgt)


def _ffn_fwd(tag, x, g, w_gate, w_up, get_w_down, sides=None):
    s, d = x.shape
    f = w_gate.shape[1]
    got = {}
    h = _rms_fwd(tag + "_rms", x, g)

    def up_epi(accs, ex, rw):
        a, b = accs
        return a, b, a * _sigmoid(a) * b

    a, b, act = _hosted(sides, got, "up", lambda side: _mm(
        tag + "_up", [(h, w_gate), (h, w_up)], "nn", s, f, d, epilogue=up_epi, out_dtypes=(BF, BF, BF), side=side))

    def down_epi(accs, ex, rw):
        return (ex[0] + 0.5 * accs[0],)

    w_down = get_w_down(got)
    (y,) = _hosted(sides, got, "down", lambda side: _mm(
        tag + "_down", [(act, w_down)], "nn", s, d, f, epilogue=down_epi, out_dtypes=(F32,), extras=[(x, 0)],
        side=side))
    return y, (h, a, b, act), got


def _ffn_bwd(tag, x, g, w_gate, w_up, w_down, saved, dy, dy_bf, sides=None):
    s, d = x.shape
    f = w_gate.shape[1]
    h, a, b, act = saved
    got = {}

    def act_epi(accs, ex, rw):
        dact = 0.5 * accs[0]
        av, bv = ex[0].astype(F32), ex[1].astype(F32)
        sg = _sigmoid(av)
        return dact * bv * (sg * (1.0 + av * (1.0 - sg))), dact * (av * sg)

    da, db = _hosted(sides, got, "dact", lambda side: _mm(
        tag + "_dact", [(dy_bf, w_down)], "nt", s, f, d, epilogue=act_epi, out_dtypes=(BF, BF),
        extras=[(a, 0), (b, 0)], side=side))
    ident = lambda accs, ex, rw: (accs[0],)
    (dw_down,) = _hosted(sides, got, "dwd", lambda side: _mm(
        tag + "_dwd", [(act, dy_bf)], "tn", f, d, s, epilogue=lambda accs, ex, rw: (0.5 * accs[0],),
        out_dtypes=(F32,), side=side))
    got["dw_down"] = dw_down = dw_down.reshape(N_CHIPS, f // N_CHIPS, d)
    got["dw_gate"] = dw_gate = _hosted(sides, got, "dwg", lambda side: _dw_col_pieces(
        tag + "_dwg", h, da, d, f, s, side=side))
    got["dw_up"] = dw_up = _hosted(sides, got, "dwu", lambda side: _dw_col_pieces(
        tag + "_dwu", h, db, d, f, s, side=side))
    (dh,) = _hosted(sides, got, "dh", lambda side: _mm(
        tag + "_dh", [(da, w_gate), (db, w_up)], "nt", s, d, f, epilogue=ident, out_dtypes=(F32,), sum_pairs=True,
        side=side))
    dx, dx_bf, dg = _hosted(sides, got, "rmsb", lambda side: _rms_bwd(tag + "_rmsb", x, g, dh, dy, side=side))
    return dx, dx_bf, dg, dw_gate, dw_up, dw_down, got


def _rope_tables(s):
    half = ROPE_DIM // 2
    pos = jnp.arange(s, dtype=F32)
    inv_freq = ROPE_THETA ** (-jnp.arange(0, ROPE_DIM, 2, dtype=F32) / ROPE_DIM)
    ang = pos[:, None] * inv_freq[None, :]
    cos, sin = jnp.cos(ang), jnp.sin(ang)
    rest = HEAD_DIM - ROPE_DIM
    cos_t = jnp.concatenate([cos, cos, jnp.ones((s, rest), F32)], axis=-1)
    sin_t = jnp.concatenate([-sin, sin, jnp.zeros((s, rest), F32)], axis=-1)
    return cos_t, sin_t


def _swap_halves(t):
    lane = lax.broadcasted_iota(jnp.int32, t.shape, 1) & (HEAD_DIM - 1)
    half = ROPE_DIM // 2
    return jnp.where(lane < half, pltpu.roll(t, t.shape[1] - half, 1), pltpu.roll(t, half, 1))


def _dil_bias(blk):
    n_delta = MAX_WINDOW // blk + 1
    delta = jnp.arange(n_delta, dtype=jnp.int32)[:, None, None]
    r = jnp.arange(blk, dtype=jnp.int32)[None, None, :]
    c = jnp.arange(blk, dtype=jnp.int32)[None, :, None]
    o = delta * blk + r - c
    mult = jnp.zeros(o.shape, F32)
    for w, dd in DIL_PATTERNS:
        mult = mult + ((o >= 0) & (o <= w) & (o % dd == 0)).astype(F32)
    return jnp.where(mult > 0, jnp.log(jnp.maximum(mult, 1.0)), NEG)


def _att_block(s):
    return _pick(s, ATT_BLOCK, LANES)


def _fox_aug(name, c_pad, qkv, q_off, k_off, n_heads):
    s = c_pad.shape[0]
    tr = _pick(s, 1024, 16)

    def body(c_ref, q_ref, k_ref, qc_ref, kc_ref):
        h = pl.program_id(1)
        lane = lax.broadcasted_iota(jnp.int32, (tr, LANES), 1)
        ch = jnp.sum(jnp.where(lane == h, c_ref[...], 0.0), axis=1, keepdims=True)
        hi, mid, lo = (t.astype(F32) for t in _split3(ch))
        zero = jnp.zeros((tr, LANES), F32)
        is_hi = jnp.logical_or(lane == 0, lane == 3)
        is_mid = jnp.logical_or(lane == 1, lane == 4)
        parts = jnp.where(is_hi, hi, jnp.where(is_mid, mid, lo))
        qc_ref[:, :HEAD_DIM] = q_ref[...]
        kc_ref[:, :HEAD_DIM] = k_ref[...]
        qc_ref[:, HEAD_DIM:] = jnp.where(lane < 3, 1.0, jnp.where(lane < 6, parts, zero)).astype(BF)
        kc_ref[:, HEAD_DIM:] = jnp.where(lane < 3, -parts, jnp.where(lane < 6, 1.0, zero)).astype(BF)

    head = lambda o: pl.BlockSpec((tr, HEAD_DIM), functools.partial(lambda i, h, o: (i, o + h), o=o))
    spec = pl.BlockSpec((tr, 2 * HEAD_DIM), lambda i, h: (i, h))
    shape = jax.ShapeDtypeStruct((s, n_heads * 2 * HEAD_DIM), BF)
    return pl.pallas_call(
        body, grid=(s // tr, n_heads),
        in_specs=[pl.BlockSpec((tr, LANES), lambda i, h: (i, 0)), head(q_off), head(k_off)],
        out_specs=[spec, spec], out_shape=[shape, shape],
        compiler_params=_params(("parallel", "arbitrary")), name=name)(c_pad, qkv, qkv)


def _causal_mask(st):
    kpos = lax.broadcasted_iota(jnp.int32, st.shape, 0)
    qpos = lax.broadcasted_iota(jnp.int32, st.shape, 1)
    return jnp.where(kpos <= qpos, st, NEG)


def _flash_fwd(name, q_src, k_src, v_src, n_heads, *, fox, tab_t=None):
    (q_arr, q_off, qw), (k_arr, k_off, kw), (v_arr, v_off, _) = q_src, k_src, v_src
    s = q_arr.shape[0]
    blk = _att_block(s)
    nq = s // blk
    n_delta = MAX_WINDOW // blk + 1
    grp = ATT_HEADS_FWD
    assert qw == kw and n_heads % grp == 0 and q_off % grp == 0 and k_off % grp == 0 and v_off % grp == 0
    wide = grp * HEAD_DIM

    def body(*refs):
        if fox:
            q_ref, k_ref, v_ref, o_ref, lse_ref, acc, m_s, l_s = refs
        else:
            q_ref, k_ref, v_ref, tab_ref, o_ref, lse_ref, acc, m_s, l_s = refs
        i = pl.program_id(1)
        acc[...] = jnp.zeros_like(acc)
        m_s[...] = jnp.full_like(m_s, NEG)
        l_s[...] = jnp.zeros_like(l_s)

        bk = 2 * blk if fox and s % (2 * blk) == 0 else blk
        per_key_block = bk // blk
        qs_w = _pick(blk, 2 * LANES, LANES)

        def step(j, diagonal):
            ks = pl.ds(pl.multiple_of(j * bk, bk), bk)
            for g, strip in [(g, t) for g in range(grp) for t in range(blk // qs_w)]:
                cols = slice(g * HEAD_DIM, (g + 1) * HEAD_DIM)
                qk_cols = slice(g * qw, (g + 1) * qw)
                qr = slice(strip * qs_w, (strip + 1) * qs_w)
                st = lax.dot_general(k_ref[ks, qk_cols], q_ref[qr, qk_cols], _DIMS["nt"], preferred_element_type=F32)
                if fox:
                    if diagonal:
                        kpos = lax.broadcasted_iota(jnp.int32, st.shape, 0) + j * bk
                        qpos = lax.broadcasted_iota(jnp.int32, st.shape, 1) + i * blk + strip * qs_w
                        st = jnp.where(kpos <= qpos, st, NEG)
                else:
                    st = st + tab_ref[i - j, :, qr]
                m_prev = m_s[g, :, qr]
                m_new = jnp.maximum(m_prev, jnp.max(st, axis=0, keepdims=True))
                alpha = jnp.exp(m_prev - m_new)
                p = jnp.exp(st - m_new)
                l_s[g, :, qr] = alpha * l_s[g, :, qr] + jnp.sum(p, axis=0, keepdims=True)
                acc[g, :, qr] = alpha * acc[g, :, qr] + lax.dot_general(v_ref[ks, cols], p.astype(BF), _DIMS["tn"],
                                                                        preferred_element_type=F32)
                m_s[g, :, qr] = m_new

        def loop_step(j, carry):
            step(j, False)
            return carry

        if fox:
            lax.fori_loop(0, i // per_key_block, loop_step, 0)
            step(i // per_key_block, True)
        else:
            lax.fori_loop(jnp.maximum(i - (n_delta - 1), 0), i + 1, loop_step, 0)
        for g in range(grp):
            o_ref[:, g * HEAD_DIM:(g + 1) * HEAD_DIM] = (acc[g] / l_s[g]).T.astype(o_ref.dtype)
            lse_ref[g] = m_s[g] + jnp.log(l_s[g])

    off = lambda o: functools.partial(lambda h, i, o: (0, o + h), o=o // grp)
    in_specs = [pl.BlockSpec((blk, grp * qw), functools.partial(lambda h, i, o: (i, o + h), o=q_off // grp)),
                pl.BlockSpec((s, grp * kw), off(k_off), pipeline_mode=pl.Buffered(1)),
                pl.BlockSpec((s, wide), off(v_off), pipeline_mode=pl.Buffered(1))]
    args = [q_arr, k_arr, v_arr]
    if not fox:
        in_specs.append(pl.BlockSpec((n_delta, blk, blk), lambda h, i: (0, 0, 0)))
        args.append(tab_t)
    return pl.pallas_call(
        body, grid=(n_heads // grp, nq), in_specs=in_specs,
        out_specs=[pl.BlockSpec((blk, wide), lambda h, i: (i, h)),
                   pl.BlockSpec((grp, None, 1, blk), lambda h, i: (h, i, 0, 0))],
        out_shape=[jax.ShapeDtypeStruct((s, n_heads * HEAD_DIM), BF),
                   jax.ShapeDtypeStruct((n_heads, nq, 1, blk), F32)],
        scratch_shapes=[pltpu.VMEM((grp, HEAD_DIM, blk), F32), pltpu.VMEM((grp, 1, blk), F32),
                        pltpu.VMEM((grp, 1, blk), F32)],
        compiler_params=_params(("parallel", "parallel")), name=name)(*args)


def _att_delta(name, do, o, n_heads):
    s = do.shape[0]
    blk = _pick(s, 1024, 16)

    def body(do_ref, o_ref, d_ref):
        d_ref[...] = jnp.sum(do_ref[...].astype(F32) * o_ref[...].astype(F32), axis=-1, keepdims=True)

    spec = pl.BlockSpec((blk, HEAD_DIM), lambda h, i: (i, h))
    return pl.pallas_call(
        body, grid=(n_heads, s // blk), in_specs=[spec, spec],
        out_specs=pl.BlockSpec((None, blk, 1), lambda h, i: (h, i, 0)),
        out_shape=jax.ShapeDtypeStruct((n_heads, s, 1), F32),
        compiler_params=_params(("parallel", "parallel")), name=name)(do, o)


def _flash_bwd(name, q_src, k_src, v_src, do, lse_row, delta_row, n_heads, *, fox, tab_t=None):
    (q_arr, q_off, qw), (k_arr, k_off, kw), (v_arr, v_off, _) = q_src, k_src, v_src
    s = q_arr.shape[0]
    blk = _att_block(s)
    nq = s // blk
    n_delta = MAX_WINDOW // blk + 1
    grp = ATT_HEADS
    assert qw == kw and n_heads % grp == 0 and q_off % grp == 0 and k_off % grp == 0 and v_off % grp == 0
    wide = grp * HEAD_DIM

    def body(*refs):
        if fox:
            (q_ref, do_ref, k_ref, v_ref, lse_ref, dl_ref,
             dq_ref, dk_ref, dv_ref, dc_ref, dcq_ref, dk_acc, dv_acc, dc_acc) = refs
        else:
            (q_ref, do_ref, k_ref, v_ref, lse_ref, dl_ref, tab_ref,
             dq_ref, dk_ref, dv_ref, dk_acc, dv_acc) = refs
        j = pl.program_id(1)

        @pl.when(j == 0)
        def _():
            dq_ref[...] = jnp.zeros_like(dq_ref)
            if fox:
                dcq_ref[...] = jnp.zeros_like(dcq_ref)

        dk_acc[...] = jnp.zeros_like(dk_acc)
        dv_acc[...] = jnp.zeros_like(dv_acc)
        if fox:
            dc_acc[...] = jnp.zeros_like(dc_acc)

        def step(i, diagonal):
            qs = pl.ds(pl.multiple_of(i * blk, blk), blk)
            for g in range(grp):
                cols = slice(g * HEAD_DIM, (g + 1) * HEAD_DIM)
                qk_cols = slice(g * qw, (g + 1) * qw)
                plain = slice(g * qw, g * qw + HEAD_DIM)
                kb, vb = k_ref[:, plain], v_ref[:, cols]
                qb, dob = q_ref[qs, plain], do_ref[qs, cols]
                st = lax.dot_general(k_ref[:, qk_cols], q_ref[qs, qk_cols], _DIMS["nt"], preferred_element_type=F32)
                if fox:
                    if diagonal:
                        st = _causal_mask(st)
                else:
                    st = st + tab_ref[i - j]
                pt = jnp.exp(st - lse_ref[g, i])
                dv_acc[:, cols] += jnp.dot(pt.astype(BF), dob, preferred_element_type=F32)
                dpt = lax.dot_general(vb, dob, _DIMS["nt"], preferred_element_type=F32)
                dst = pt * (dpt - dl_ref[g, i])
                dsb = dst.astype(BF)
                dk_acc[:, cols] += jnp.dot(dsb, qb, preferred_element_type=F32)
                dq_ref[qs, cols] += lax.dot_general(dsb, kb, _DIMS["tn"], preferred_element_type=F32)
                if fox:
                    folded = dst[:, :LANES]
                    for part in range(1, blk // LANES):
                        folded = folded + dst[:, part * LANES:(part + 1) * LANES]
                    dc_acc[g] -= folded
                    dcq_ref[g, i] += jnp.sum(dst, axis=0, keepdims=True)

        def loop_step(i, carry):
            step(i, False)
            return carry

        if fox:
            step(j, True)
            lax.fori_loop(j + 1, nq, loop_step, 0)
        else:
            lax.fori_loop(j, jnp.minimum(nq, j + n_delta), loop_step, 0)
        dk_ref[...] = dk_acc[...]
        dv_ref[...] = dv_acc[...].astype(dv_ref.dtype)
        if fox:
            for g in range(grp):
                dc_ref[g] = jnp.sum(dc_acc[g], axis=-1, keepdims=True)

    full = lambda o, wd=wide: pl.BlockSpec((s, wd), functools.partial(lambda h, j, o: (0, o + h), o=o // grp))
    tile = lambda o, wd=wide: pl.BlockSpec((blk, wd), functools.partial(lambda h, j, o: (j, o + h), o=o // grp))
    per_q = pl.BlockSpec((grp, nq, 1, blk), lambda h, j: (h, 0, 0, 0))
    per_k = pl.BlockSpec((grp, blk, 1), lambda h, j: (h, j, 0))
    in_specs = [full(q_off, grp * qw), full(0), tile(k_off, grp * kw), tile(v_off), per_q, per_q]
    args = [q_arr, do, k_arr, v_arr, lse_row, delta_row]
    out_specs = [full(0), tile(0), tile(0)]
    hd = n_heads * HEAD_DIM
    out_shape = [jax.ShapeDtypeStruct((s, hd), F32), jax.ShapeDtypeStruct((s, hd), F32),
                 jax.ShapeDtypeStruct((s, hd), BF)]
    scratch = [pltpu.VMEM((blk, wide), F32), pltpu.VMEM((blk, wide), F32)]
    if fox:
        out_specs += [per_k, per_q]
        out_shape += [jax.ShapeDtypeStruct((n_heads, s, 1), F32), jax.ShapeDtypeStruct((n_heads, nq, 1, blk), F32)]
        scratch.append(pltpu.VMEM((grp, blk, LANES), F32))
    else:
        in_specs.append(pl.BlockSpec((n_delta, blk, blk), lambda h, j: (0, 0, 0)))
        args.append(tab_t)
    return pl.pallas_call(
        body, grid=(n_heads // grp, nq), in_specs=in_specs, out_specs=out_specs, out_shape=out_shape,
        scratch_shapes=scratch, compiler_params=_params(("parallel", "arbitrary")), name=name)(*args)


def _split3(t):
    hi = t.astype(BF)
    r1 = t - hi.astype(F32)
    mid = r1.astype(BF)
    lo = (r1 - mid.astype(F32)).astype(BF)
    return hi, mid, lo


def _tri_dot(tri, t):
    hi, mid, lo = _split3(t)
    return (jnp.dot(tri, hi, preferred_element_type=F32) + jnp.dot(tri, mid, preferred_element_type=F32)
            + jnp.dot(tri, lo, preferred_element_type=F32))


def _log_sigmoid(z):
    return jnp.minimum(z, 0.0) - jnp.log(1.0 + jnp.exp(-jnp.abs(z)))


def _forget_cumsum(name, proj, f_col, bias):
    s = proj.shape[0]
    blk = _att_block(s)

    def body(f_ref, b_ref, c_ref, carry):
        @pl.when(pl.program_id(0) == 0)
        def _():
            carry[...] = jnp.zeros_like(carry)

        lf = _log_sigmoid(f_ref[...] + b_ref[...])
        r = lax.broadcasted_iota(jnp.int32, (blk, blk), 0)
        c = lax.broadcasted_iota(jnp.int32, (blk, blk), 1)
        tri = (c <= r).astype(BF)
        c_ref[...] = _tri_dot(tri, lf) + carry[...]
        carry[...] = c_ref[pl.ds(blk - 1, 1), :]

    return pl.pallas_call(
        body, grid=(s // blk,),
        in_specs=[pl.BlockSpec((blk, LANES), lambda i: (i, f_col)), pl.BlockSpec((1, LANES), lambda i: (0, 0))],
        out_specs=pl.BlockSpec((blk, LANES), lambda i: (i, 0)), out_shape=jax.ShapeDtypeStruct((s, LANES), F32),
        scratch_shapes=[pltpu.VMEM((1, LANES), F32)],
        compiler_params=_params(("arbitrary",)), name=name)(proj, bias)


def _forget_bwd(name, proj, f_col, bias, dc):
    s = proj.shape[0]
    blk = _att_block(s)
    nb = s // blk

    def body(f_ref, b_ref, dc_ref, df_ref, db_ref, carry):
        @pl.when(pl.program_id(0) == 0)
        def _():
            carry[...] = jnp.zeros_like(carry)
            db_ref[...] = jnp.zeros_like(db_ref)

        r = lax.broadcasted_iota(jnp.int32, (blk, blk), 0)
        c = lax.broadcasted_iota(jnp.int32, (blk, blk), 1)
        tri = (c >= r).astype(BF)
        r = lax.broadcasted_iota(jnp.int32, (blk, LANES), 0)
        dlf = _tri_dot(tri, dc_ref[...]) + carry[...]
        carry[...] = jnp.sum(jnp.where(r == 0, dlf, 0.0), axis=0, keepdims=True)
        dz = dlf * _sigmoid(-(f_ref[...] + b_ref[...]))
        df_ref[...] = dz.astype(BF)
        db_ref[...] += _fold8(dz)

    rev = lambda i: (nb - 1 - i, 0)
    return pl.pallas_call(
        body, grid=(nb,),
        in_specs=[pl.BlockSpec((blk, LANES), lambda i: (nb - 1 - i, f_col)), pl.BlockSpec((1, LANES), lambda i: (0, 0)),
                  pl.BlockSpec((blk, LANES), rev)],
        out_specs=[pl.BlockSpec((blk, LANES), rev), pl.BlockSpec((8, LANES), lambda i: (0, 0))],
        out_shape=[jax.ShapeDtypeStruct((s, LANES), BF), jax.ShapeDtypeStruct((8, LANES), F32)],
        scratch_shapes=[pltpu.VMEM((1, LANES), F32)],
        compiler_params=_params(("arbitrary",)), name=name)(proj, bias, dc)


def _dproj_assemble(name, parts, gates, cos_t, sin_t, n_heads, d_model):
    s = cos_t.shape[0]
    tr = _pick(s, 1024, 16)
    scale = HEAD_DIM ** -0.5
    hd = n_heads * HEAD_DIM
    cw = _pick(math.gcd(hd, d_model), PREP_COLS, HEAD_DIM)
    widths = [hd] * 6 + [d_model] * 2
    starts = [sum(widths[:t]) // cw for t in range(len(widths) + 1)]
    cos_w, sin_w = jnp.tile(cos_t, (1, cw // HEAD_DIM)), jnp.tile(sin_t, (1, cw // HEAD_DIM))

    def body(*refs):
        p_refs, cos_ref, sin_ref, o_ref = refs[:8], refs[8], refs[9], refs[10]
        j = pl.program_id(1)
        for kind in range(8):
            @pl.when(jnp.logical_and(j >= starts[kind], j < starts[kind + 1]))
            def _(kind=kind):
                t = p_refs[kind][...]
                if kind in (0, 1, 3):
                    t = t.astype(F32)
                if kind in (0, 3):
                    t = t * scale
                if kind in (0, 1):
                    t = t * cos_ref[...] - _swap_halves(t) * sin_ref[...]
                o_ref[...] = t.astype(BF)

    def part_spec(kind):
        return pl.BlockSpec((tr, cw), functools.partial(
            lambda i, j, kind: (i, jnp.clip(j - starts[kind], 0, widths[kind] // cw - 1)), kind=kind))

    tab = pl.BlockSpec((tr, cw), lambda i, j: (i, 0))
    return pl.pallas_call(
        body, grid=(s // tr, starts[-1]), in_specs=[part_spec(kind) for kind in range(8)] + [tab, tab],
        out_specs=pl.BlockSpec((tr, cw), lambda i, j: (i, j)),
        out_shape=jax.ShapeDtypeStruct((s, sum(widths)), BF),
        compiler_params=_params(("parallel", "arbitrary")), name=name)(*parts, *gates, cos_w, sin_w)


FFN2 =("ffn2_w_gate", "ffn2_w_up", "ffn2_w_down")


def _device_step(x, tgt, shards, small):
    s, d = x.shape
    hd = shards["w_proj_dil"].shape[0]
    hh = hd // HEAD_DIM
    blk = _att_block(s)
    gate_off = 6 * hd
    f_col = 0
    n_proj = gate_off + 2 * d
    ident = lambda accs, ex, rw: (accs[0],)
    chip = 2 * lax.axis_index("x") + lax.axis_index("y")
    ids = jnp.stack([chip, lax.axis_index("c")]).astype(jnp.int32)
    w = {}

    def take_gathered(names, gathered):
        for n, t in zip(names, gathered):
            t = lax.dynamic_update_index_in_dim(t, shards[n], chip, 0)
            if n == "w_in":
                w["w_in"], w["w_in_f"] = _repack_w_in(t, hd, d)
            else:
                w[n] = _full_from_pieces(n, t)

    def chip_sums(names, pieces, from_sibling):
        return [_rs_add("rs_add_" + n, ids, g, o) for n, g, o in zip(names, pieces, from_sibling)]

    def core_halves(names, sums, from_chips):
        return [_rs_sum("rs_sum_" + n, ids, own, got) for n, own, got in zip(names, sums, from_chips)]

    def gather(names):
        return _gather_side([shards[n] for n in names])

    first, under_up = ("ffn1_w_gate", "ffn1_w_up"), ("ffn1_w_down", "w_in")
    under_down = ("w_proj_dil", "w_proj_fox", "w_out", "ffn2_w_gate")
    under_proj = ("ffn2_w_up", "ffn2_w_down")
    take_gathered(first, gather(first).call("gather_first"))

    def w_down_of_ffn1(got):
        take_gathered(under_up, got["up"])
        return w["ffn1_w_down"]

    x1, saved1, got = _ffn_fwd("ffn1", x, small["ffn1_norm"], w["ffn1_w_gate"], w["ffn1_w_up"], w_down_of_ffn1,
                               sides={"up": lambda got: gather(under_up), "down": lambda got: gather(under_down)})
    take_gathered(under_down, got["down"])

    hm = _rms_fwd("mix_rms", x1, small["mix_norm"])
    cos_t, sin_t = _rope_tables(s)
    head_tile = _pick(hd, 1024, HEAD_DIM)
    tiles_per_kind = hd // head_tile
    scale = HEAD_DIM ** -0.5

    def heads_epi(accs, ex, rw):
        kind = pl.program_id(1) // tiles_per_kind
        t = accs[0]
        reps = (1, head_tile // HEAD_DIM)
        r = t * jnp.tile(ex[0], reps) + _swap_halves(t) * jnp.tile(ex[1], reps)
        t = jnp.where(kind < 2, r, t)
        return (jnp.where(jnp.logical_or(kind == 0, kind == 3), t * scale, t),)

    (qkv,), gathered = _mm("proj_heads", [(hm, w["w_in"])], "nn", s, gate_off, d, epilogue=heads_epi, out_dtypes=(BF,),
                           extras=[(cos_t, None), (sin_t, None)], tn=head_tile, keep_tn=True,
                           side=gather(under_proj[:1]))
    take_gathered(under_proj[:1], gathered)
    (gates,), gathered = _mm("proj_gates", [(hm, w["w_in"])], "nn", s, 2 * d, d, epilogue=ident, out_dtypes=(F32,),
                             b_off=gate_off, side=gather(under_proj[1:]))
    take_gathered(under_proj[1:], gathered)
    (f_logit,) = _mm("proj_f", [(hm, w["w_in_f"])], "nn", s, LANES, d, epilogue=ident, out_dtypes=(F32,))
    tab_t = _dil_bias(blk)
    dil_src = ((qkv, 0, HEAD_DIM), (qkv, hh, HEAD_DIM), (qkv, 2 * hh, HEAD_DIM))
    y_dil, lse_d = _flash_fwd("dil_fwd", *dil_src, hh, fox=False, tab_t=tab_t)
    bias_f = jnp.pad(small["b_forget"], ((0, 0), (0, LANES - hh)))
    c_pad = _forget_cumsum("forget_cumsum", f_logit, f_col, bias_f)
    q_cat, k_cat = _fox_aug("fox_aug", c_pad, qkv, 3 * hh, 4 * hh, hh)
    fox_src = ((q_cat, 0, 2 * HEAD_DIM), (k_cat, 0, 2 * HEAD_DIM), (qkv, 5 * hh, HEAD_DIM))
    y_fox, lse_f = _flash_fwd("fox_fwd", *fox_src, hh, fox=True)

    def merge_epi(accs, ex, rw):
        ud, uf = accs
        return ud, uf, _sigmoid(ex[0] + rw[0]) * ud + _sigmoid(ex[1] + rw[1]) * uf

    u_d, u_f, merged = _mm("merge", [(y_dil, w["w_proj_dil"]), (y_fox, w["w_proj_fox"])], "nn", s, d, hd,
                           epilogue=merge_epi, out_dtypes=(BF, BF, BF),
                           extras=[(gates, 0), (gates, d)],
                           rows=[small["b_gate_dil"], small["b_gate_fox"]])
    (x2,) = _mm("mix_out", [(merged, w["w_out"])], "nn", s, d, d,
                epilogue=lambda accs, ex, rw: (ex[0] + accs[0],), out_dtypes=(F32,), extras=[(x1, 0)])

    x3, saved2, _ = _ffn_fwd("ffn2", x2, small["ffn2_norm"], w["ffn2_w_gate"], w["ffn2_w_up"],
                             lambda got: w["ffn2_w_down"])
    dx3, dx3_bf, dg_final, sq = _final("final", x3, small["final_norm"].reshape(1, d), tgt)

    dx2, dx2_bf, dg_ffn2, dw_g2, dw_u2, dw_d2, _ = _ffn_bwd("ffn2", x2, small["ffn2_norm"], w["ffn2_w_gate"],
                                                            w["ffn2_w_up"], w["ffn2_w_down"], saved2, dx3, dx3_bf)
    pieces2 = [dw_g2, dw_u2, dw_d2]

    def dmerge_epi(accs, ex, rw):
        dm = accs[0]
        gd, gf, ud, uf = ex[0], ex[1], ex[2].astype(F32), ex[3].astype(F32)
        sd, sf = _sigmoid(gd + rw[0]), _sigmoid(gf + rw[1])
        dgd = dm * ud * (sd * (1.0 - sd))
        dgf = dm * uf * (sf * (1.0 - sf))
        return (dm * sd, dm * sf, dgd, dgf, jnp.sum(dgd, axis=0, keepdims=True), jnp.sum(dgf, axis=0, keepdims=True))

    (du_d, du_f, dg_d, dg_f, dbg_d, dbg_f), from_sibling = _mm(
        "dmerge", [(dx2_bf, w["w_out"])], "nt", s, d, d, epilogue=dmerge_epi, out_dtypes=(BF, BF, BF, BF), n_colsum=2,
        extras=[(gates, 0), (gates, d), (u_d, 0), (u_f, 0)],
        rows=[small["b_gate_dil"], small["b_gate_fox"]], side=_swap_side(pieces2))
    sums2 = chip_sums(FFN2, pieces2, from_sibling)
    (dw_out,) = _mm("dw_out", [(merged, dx2_bf)], "tn", d, d, s, epilogue=ident, out_dtypes=(F32,))
    dw_out = dw_out.reshape(N_CHIPS, d // N_CHIPS, d)
    dw_pd = _dw_col_pieces("dw_pd", y_dil, du_d, hd, d, s)
    dw_pf = _dw_col_pieces("dw_pf", y_fox, du_f, hd, d, s)
    (dy_dil,) = _mm("dy_dil", [(du_d, w["w_proj_dil"])], "nt", s, hd, d, epilogue=ident, out_dtypes=(BF,))
    (dy_fox,) = _mm("dy_fox", [(du_f, w["w_proj_fox"])], "nt", s, hd, d, epilogue=ident, out_dtypes=(BF,))

    row = lambda t: t.reshape(hh, s // blk, 1, blk)
    delta_d = _att_delta("dil_delta", dy_dil, y_dil, hh)
    dq_d, dk_d, dv_d = _flash_bwd("dil_bwd", *dil_src, dy_dil, lse_d, row(delta_d), hh, fox=False, tab_t=tab_t)
    delta_f = _att_delta("fox_delta", dy_fox, y_fox, hh)
    dq_f, dk_f, dv_f, dc_k, dc_q = _flash_bwd("fox_bwd", *fox_src, dy_fox, lse_f, row(delta_f), hh, fox=True)
    dc = dc_k.reshape(hh, s) + dc_q.reshape(hh, s)
    dc_pad = jnp.pad(dc.T, ((0, 0), (0, LANES - hh)))
    df, db_forget = _forget_bwd("forget_bwd", f_logit, f_col, bias_f, dc_pad)
    dproj = _dproj_assemble("dproj", [dq_d, dk_d, dv_d, dq_f, dk_f, dv_f], [dg_d, dg_f], cos_t, sin_t, hh, d)
    (dhm_f,) = _mm("dhm_f", [(df, w["w_in_f"])], "nt", s, d, LANES, epilogue=ident, out_dtypes=(F32,))
    (dhm,), from_chips = _mm("dhm", [(dproj, w["w_in"])], "nt", s, d, n_proj,
                             epilogue=lambda accs, ex, rw: (accs[0] + ex[0],), out_dtypes=(F32,),
                             extras=[(dhm_f, 0)], side=_scatter_side(sums2))
    halves2 = core_halves(FFN2, sums2, from_chips)
    small_mixer = ("w_proj_dil", "w_proj_fox", "w_out")
    pieces_sm = [dw_pd, dw_pf, dw_out]
    (dw_in,), from_sibling = _mm("dw_in", [(hm, dproj)], "tn", d, n_proj, s, epilogue=ident, out_dtypes=(F32,),
                                 side=_swap_side(pieces_sm))
    sums_sm = chip_sums(small_mixer, pieces_sm, from_sibling)
    (dw_in_f,) = _mm("dw_in_f", [(hm, df)], "tn", d, LANES, s, epilogue=ident, out_dtypes=(F32,))
    dw_in = _dw_in_pieces(dw_in, dw_in_f, hd, d)
    dx1, dx1_bf, dg_mix = _rms_bwd("mix_rmsb", x1, small["mix_norm"], dhm, dx2)

    sums, early = {}, ("w_in",) + small_mixer + FFN2

    def under_dwd(got):
        sums["w_in"] = chip_sums(["w_in"], [dw_in], got["dact"])
        return _scatter_side(sums_sm)

    def under_dwu(got):
        return _swap_side([got["dw_down"], got["dw_gate"]])

    def under_dh(got):
        sums["dg"] = chip_sums(["ffn1_w_down", "ffn1_w_gate"], [got["dw_down"], got["dw_gate"]], got["dwu"])
        halves = (core_halves(["w_in"], sums["w_in"], got["dwg"]) + core_halves(small_mixer, sums_sm, got["dwd"])
                  + halves2)
        return _scatter_side(sums["dg"]).beside(_swap_side([got["dw_up"]])).beside(_join_side(halves))

    def under_rmsb(got):
        sums["u"] = chip_sums(["ffn1_w_up"], [got["dw_up"]], got["dh"][2:3])
        return _scatter_side(sums["u"])

    dx0, _, dg_ffn1, dw_g1, dw_u1, dw_d1, got = _ffn_bwd(
        "ffn1", x, small["ffn1_norm"], w["ffn1_w_gate"], w["ffn1_w_up"], w["ffn1_w_down"], saved1, dx1, dx1_bf,
        sides={"dact": lambda got: _swap_side([dw_in]), "dwd": under_dwd,
               "dwg": lambda got: _scatter_side(sums["w_in"]), "dwu": under_dwu, "dh": under_dh, "rmsb": under_rmsb})
    halves1 = (core_halves(["ffn1_w_down", "ffn1_w_gate"], sums["dg"], got["dh"][:2])
               + core_halves(["ffn1_w_up"], sums["u"], got["rmsb"]))
    grads = dict(zip(early, got["dh"][3:]))
    grads.update(zip(("ffn1_w_down", "ffn1_w_gate", "ffn1_w_up"), _join_side(halves1).call("rs_join")))
    partials = {"ffn1_norm": dg_ffn1, "mix_norm": dg_mix, "ffn2_norm": dg_ffn2, "final_norm": dg_final,
                "b_gate_dil": dbg_d.reshape(-1, d), "b_gate_fox": dbg_f.reshape(-1, d), "b_forget": db_forget, "sq": sq}
    return dx0, grads, partials


def _coords():
    return lax.axis_index("x"), lax.axis_index("y"), lax.axis_index("c")


def _other_chips(x, y):
    return [(1 - x, y), (x, 1 - y), (1 - x, 1 - y)]


ANY_SPEC = pl.BlockSpec(memory_space=pl.ANY)


def _gather_side(shards):
    nw = len(shards)

    def copies(srcs, outs, send_sems, recv_sems):
        x, y, c = _coords()
        chips = _other_chips(x, y)

        def slot(w, px, py, pc):
            half = shards[w].shape[0] // 2
            return outs[w].at[2 * px + py, pl.ds(pc * half, half), :]

        def copy(w, k, src_ref, dst_ref, to):
            return pltpu.make_async_remote_copy(src_ref=src_ref, dst_ref=dst_ref, send_sem=send_sems.at[6 * w + k],
                                                recv_sem=recv_sems.at[6 * w + k], device_id=to, device_id_type=MESH)

        first, arrive, passed, arrive2 = [], [], [], []
        for w in range(nw):
            half = shards[w].shape[0] // 2
            for j, chip in enumerate(chips):
                first.append(copy(w, j, srcs[w].at[pl.ds(c * half, half), :], slot(w, x, y, c), (*chip, c)))
                arrive.append(copy(w, j, slot(w, *chip, c), slot(w, *chip, c), (*chip, c)))
                passed.append(copy(w, 3 + j, slot(w, *chip, c), slot(w, *chip, c), (x, y, 1 - c)))
                arrive2.append(copy(w, 3 + j, slot(w, *chip, 1 - c), slot(w, *chip, 1 - c), (x, y, 1 - c)))
        return first, arrive, passed, arrive2

    def start(srcs, outs, send_sems, recv_sems):
        for cp in copies(srcs, outs, send_sems, recv_sems)[0]:
            cp.start()

    def finish(srcs, outs, send_sems, recv_sems):
        first, arrive, passed, arrive2 = copies(srcs, outs, send_sems, recv_sems)
        for got, fwd in zip(arrive, passed):
            got.wait_recv()
            fwd.start()
        for got in arrive2:
            got.wait_recv()
        for cp in first + passed:
            cp.wait_send()

    return _Side(shards, [jax.ShapeDtypeStruct((N_CHIPS, *t.shape), t.dtype) for t in shards], 6 * nw, start, finish)


def _swap_side(grads):
    nw = len(grads)

    def copies(srcs, outs, send_sems, recv_sems):
        x, y, c = _coords()
        res = []
        for w in range(nw):
            half = grads[w].shape[1] // 2
            for p in range(N_CHIPS):
                k = N_CHIPS * w + p
                res.append(pltpu.make_async_remote_copy(
                    src_ref=srcs[w].at[p, pl.ds((1 - c) * half, half), :], dst_ref=outs[w].at[p],
                    send_sem=send_sems.at[k], recv_sem=recv_sems.at[k], device_id=(x, y, 1 - c), device_id_type=MESH))
        return res

    def start(*refs):
        for cp in copies(*refs):
            cp.start()

    def finish(*refs):
        for cp in copies(*refs):
            cp.wait()

    shapes = [jax.ShapeDtypeStruct((N_CHIPS, t.shape[1] // 2, t.shape[2]), t.dtype) for t in grads]
    return _Side(grads, shapes, N_CHIPS * nw, start, finish)


def _rs_add(name, ids, g, other):
    n, rows, cols = g.shape
    half = rows // 2
    tr = _pick(half, 256, 16)
    nb = half // tr

    def body(ids_ref, g_ref, o_ref, out_ref):
        out_ref[...] = (g_ref[...] + o_ref[...]).astype(BF)

    grid_spec = pltpu.PrefetchScalarGridSpec(
        num_scalar_prefetch=1, grid=(n, nb),
        in_specs=[pl.BlockSpec((None, tr, cols), lambda p, i, ids_ref: (p, ids_ref[1] * nb + i, 0)),
                  pl.BlockSpec((None, tr, cols), lambda p, i, ids_ref: (p, i, 0))],
        out_specs=pl.BlockSpec((None, tr, cols), lambda p, i, ids_ref: (p, i, 0)))
    return pl.pallas_call(body, grid_spec=grid_spec, out_shape=jax.ShapeDtypeStruct((n, half, cols), BF),
                          compiler_params=_params(("parallel", "parallel")), name=name)(ids, g, other)


def _scatter_side(sums):
    nw = len(sums)

    def copies(srcs, outs, send_sems, recv_sems):
        x, y, c = _coords()
        res = []
        for w in range(nw):
            for k, (px, py) in enumerate(_other_chips(x, y)):
                res.append(pltpu.make_async_remote_copy(
                    src_ref=srcs[w].at[2 * px + py], dst_ref=outs[w].at[k], send_sem=send_sems.at[3 * w + k],
                    recv_sem=recv_sems.at[3 * w + k], device_id=(px, py, c), device_id_type=MESH))
        return res

    def start(*refs):
        for cp in copies(*refs):
            cp.start()

    def finish(*refs):
        for cp in copies(*refs):
            cp.wait()

    return _Side(sums, [jax.ShapeDtypeStruct((3, *t.shape[1:]), t.dtype) for t in sums], 3 * nw, start, finish)


def _rs_sum(name, ids, own, got):
    n, half, cols = own.shape
    tr = _pick(half, 256, 16)
    nb = half // tr

    def body(ids_ref, own_ref, got_ref, out_ref):
        t = own_ref[...].astype(F32)
        for k in range(3):
            t = t + got_ref[k].astype(F32)
        out_ref[...] = t

    grid_spec = pltpu.PrefetchScalarGridSpec(
        num_scalar_prefetch=1, grid=(nb,),
        in_specs=[pl.BlockSpec((None, tr, cols), lambda i, ids_ref: (ids_ref[0], i, 0)),
                  pl.BlockSpec((3, tr, cols), lambda i, ids_ref: (0, i, 0))],
        out_specs=pl.BlockSpec((tr, cols), lambda i, ids_ref: (ids_ref[1] * nb + i, 0)))
    return pl.pallas_call(body, grid_spec=grid_spec, out_shape=jax.ShapeDtypeStruct((2 * half, cols), F32),
                          compiler_params=_params(("parallel",)), name=name)(ids, own, got)


def _join_side(totals):
    nw = len(totals)

    def start(ins, bufs, send_sems, recv_sems):
        x, y, c = _coords()
        for w in range(nw):
            half = totals[w].shape[0] // 2
            pltpu.make_async_remote_copy(
                src_ref=bufs[w].at[pl.ds(c * half, half), :], dst_ref=bufs[w].at[pl.ds(c * half, half), :],
                send_sem=send_sems.at[w], recv_sem=recv_sems.at[w], device_id=(x, y, 1 - c),
                device_id_type=MESH).start()

    def finish(ins, bufs, send_sems, recv_sems):
        x, y, c = _coords()
        for w in range(nw):
            half = totals[w].shape[0] // 2
            arrival = pltpu.make_async_remote_copy(
                src_ref=bufs[w].at[pl.ds(c * half, half), :], dst_ref=bufs[w].at[pl.ds((1 - c) * half, half), :],
                send_sem=send_sems.at[w], recv_sem=recv_sems.at[w], device_id=(x, y, 1 - c), device_id_type=MESH)
            arrival.wait_recv()
            arrival.wait_send()

    return _Side(totals, [jax.ShapeDtypeStruct(t.shape, t.dtype) for t in totals], nw, start, finish,
                 aliases={w: w for w in range(nw)})


def _gather_all(name, t):
    rows, cols = t.shape

    def body(src, out, send_sems, recv_sems, local_sem):
        x, y, c = _coords()
        me = 4 * x + 2 * y + c
        mine = pltpu.make_async_copy(src, out.at[me], local_sem)
        mine.start()
        peers = [(x ^ (k >> 2 & 1), y ^ (k >> 1 & 1), c ^ (k & 1)) for k in range(1, N_DEV)]
        sends = [pltpu.make_async_remote_copy(src_ref=src, dst_ref=out.at[me], send_sem=send_sems.at[k],
                                              recv_sem=recv_sems.at[k], device_id=peer, device_id_type=MESH)
                 for k, peer in enumerate(peers)]
        for cp in sends:
            cp.start()
        for k, (px, py, pc) in enumerate(peers):
            pltpu.make_async_remote_copy(src_ref=src, dst_ref=out.at[4 * px + 2 * py + pc], send_sem=send_sems.at[k],
                                         recv_sem=recv_sems.at[k], device_id=(px, py, pc),
                                         device_id_type=MESH).wait_recv()
        for cp in sends:
            cp.wait_send()
        mine.wait()

    vmem = pl.BlockSpec(memory_space=pltpu.VMEM)
    return pl.pallas_call(
        body, in_specs=[vmem], out_specs=vmem, out_shape=jax.ShapeDtypeStruct((N_DEV, rows, cols), t.dtype),
        scratch_shapes=[pltpu.SemaphoreType.DMA((7,)), pltpu.SemaphoreType.DMA((7,)), pltpu.SemaphoreType.DMA],
        name=name)(t)


def _adamw_math(w, g, m, v):
    m = ADAM_B1 * m + (1.0 - ADAM_B1) * g
    v = ADAM_B2 * v + (1.0 - ADAM_B2) * (g * g)
    m_hat = m / (1.0 - ADAM_B1 ** ADAM_STEP)
    v_hat = v / (1.0 - ADAM_B2 ** ADAM_STEP)
    delta = -ADAM_LR * (m_hat / (jnp.sqrt(v_hat) + ADAM_EPS) + ADAM_WD * w)
    return delta, m, v


def _adamw(name, w, g, m, v):
    _, rows, cols = w.shape
    tr = _pick(rows, 256, 8)

    def body(w_ref, g_ref, m_ref, v_ref, g_out, d_out, m_out, v_out):
        g = g_ref[...]
        g_out[...] = g
        d_out[...], m_out[...], v_out[...] = _adamw_math(w_ref[...], g, m_ref[...], v_ref[...])

    blk3 = pl.BlockSpec((None, tr, cols), lambda i: (0, i, 0))
    blk = pl.BlockSpec((tr, cols), lambda i: (i, 0))
    shape = jax.ShapeDtypeStruct((rows, cols), F32)
    return pl.pallas_call(
        body, grid=(rows // tr,), in_specs=[blk3, blk, blk3, blk3], out_specs=[blk] * 4, out_shape=[shape] * 4,
        compiler_params=_params(("parallel",)), name=name)(w, g, m, v)


def _small_reduce(name, parts, width):
    def body(*refs):
        out = refs[-1]
        out[...] = jnp.zeros_like(out)
        for k, r in enumerate(refs[:-1]):
            out[pl.ds(k, 1), :] = jnp.sum(r[...], axis=0, keepdims=True)

    vmem = pl.BlockSpec(memory_space=pltpu.VMEM)
    return pl.pallas_call(body, in_specs=[vmem] * len(parts), out_specs=vmem,
                          out_shape=jax.ShapeDtypeStruct((8, width), F32), name=name)(*parts)


def _small_adamw(name, gathered, w, m, v, loss_row, loss_scale):
    def body(gt_ref, w_ref, m_ref, v_ref, g_out, d_out, m_out, v_out, loss_out):
        g = gt_ref[0]
        for k in range(1, N_DEV):
            g = g + gt_ref[k]
        g_out[...] = g
        row = lax.broadcasted_iota(jnp.int32, g.shape, 0)
        loss_out[...] = jnp.sum(jnp.where(row == loss_row, g, 0.0), keepdims=True) * loss_scale
        d_out[...], m_out[...], v_out[...] = _adamw_math(w_ref[...], g, m_ref[...], v_ref[...])

    vmem = pl.BlockSpec(memory_space=pltpu.VMEM)
    shape = jax.ShapeDtypeStruct(w.shape, F32)
    return pl.pallas_call(body, in_specs=[vmem] * 4, out_specs=[vmem] * 5,
                          out_shape=[shape] * 4 + [jax.ShapeDtypeStruct((1, 1), F32)], name=name)(gathered, w, m, v)


def _full_from_pieces(name, pieces):
    _, rows, cols = pieces.shape
    if name in ROW_SHARDED:
        return pieces.reshape(N_CHIPS * rows, cols)
    return pieces.transpose(1, 0, 2).reshape(rows, N_CHIPS * cols)


def _column_range(segments, lo, hi):
    out, start = [], 0
    for t in segments:
        a, b = max(lo, start), min(hi, start + t.shape[1])
        if a < b:
            out.append(t[:, a - start:b - start])
        start += t.shape[1]
    return out


def _repack_w_in(pieces, hd, d):
    hh = hd // HEAD_DIM
    segs = [pieces[p] for p in range(N_CHIPS)]
    total = N_CHIPS * pieces.shape[2]
    main = jnp.concatenate(_column_range(segs, 0, 6 * hd) + _column_range(segs, 6 * hd + hh, total), axis=1)
    f = jnp.concatenate(_column_range(segs, 6 * hd, 6 * hd + hh), axis=1)
    return main, jnp.pad(f, ((0, 0), (0, LANES - hh)))


def _dw_in_pieces(dw, dw_f, hd, d):
    hh = hd // HEAD_DIM
    segs = [dw[:, :6 * hd], dw_f[:, :hh], dw[:, 6 * hd:]]
    cs = (6 * hd + hh + 2 * d) // N_CHIPS
    return jnp.stack([jnp.concatenate(_column_range(segs, p * cs, (p + 1) * cs), axis=1) for p in range(N_CHIPS)])


def _small_pack(vals, width):
    rows = []
    for name in SMALL:
        t = vals[name].reshape(1, -1)
        rows.append(jnp.pad(t, ((0, 0), (0, width - t.shape[1]))))
    rows.append(jnp.zeros((8 - len(SMALL), width), F32))
    return jnp.concatenate(rows, axis=0)


def kernel(x, ffn1_norm, ffn1_w_gate, ffn1_w_up, ffn1_w_down, mix_norm, w_in, b_forget, b_gate_dil, b_gate_fox, w_proj_dil, w_proj_fox, w_out, ffn2_norm, ffn2_w_gate, ffn2_w_up, ffn2_w_down, final_norm, loss_target, m_ffn1_norm, m_ffn1_w_gate, m_ffn1_w_up, m_ffn1_w_down, m_mix_norm, m_w_in, m_b_forget, m_b_gate_dil, m_b_gate_fox, m_w_proj_dil, m_w_proj_fox, m_w_out, m_ffn2_norm, m_ffn2_w_gate, m_ffn2_w_up, m_ffn2_w_down, m_final_norm, v_ffn1_norm, v_ffn1_w_gate, v_ffn1_w_up, v_ffn1_w_down, v_mix_norm, v_w_in, v_b_forget, v_b_gate_dil, v_b_gate_fox, v_w_proj_dil, v_w_proj_fox, v_w_out, v_ffn2_norm, v_ffn2_w_gate, v_ffn2_w_up, v_ffn2_w_down, v_final_norm):
    given = dict(locals())
    wts = {n: given[n] for n in WEIGHTS}
    mom_m = {n: given["m_" + n] for n in WEIGHTS}
    mom_v = {n: given["v_" + n] for n in WEIGHTS}
    d = x.shape[2]

    shards = {n: wts[n][0].astype(BF) for n in SHARDED}
    small = {n: wts[n] for n in SMALL}
    grad_x, grads, partials = _device_step(x[0], loss_target[0], shards, small)

    out_g, out_d, out_m, out_v = {}, {}, {}, {}
    for n in SHARDED:
        outs = _adamw("adamw_" + n, wts[n], grads[n], mom_m[n], mom_v[n])
        out_g[n], out_d[n], out_m[n], out_v[n] = (t[None] for t in outs)

    width = d
    part_rows = []
    for n in SMALL:
        t = partials[n]
        part_rows.append(jnp.pad(t, ((0, 0), (0, width - t.shape[1]))))
    part_rows.append(partials["sq"])
    local_small = _small_reduce("small_reduce", part_rows, width)
    gathered_small = _gather_all("small_gather", local_small)
    sg, sd_, sm, sv, loss = _small_adamw("small_adamw", gathered_small, _small_pack(wts, width),
                                         _small_pack(mom_m, width), _small_pack(mom_v, width), len(SMALL), 0.5 / d)
    for k, n in enumerate(SMALL):
        shp = wts[n].shape
        take = lambda t: t[k, :shp[-1]].reshape(shp)
        out_g[n], out_d[n], out_m[n], out_v[n] = take(sg), take(sd_), take(sm), take(sv)
    return (loss[0, 0], grad_x[None], *[out_g[n] for n in WEIGHTS], *[out_d[n] for n in WEIGHTS],
            *[out_m[n] for n in WEIGHTS], *[out_v[n] for n in WEIGHTS])
```
